```python
import math
import jax, jax.numpy as jnp
from jax import lax
import numpy as np

D_MODEL = 1024
BATCH = 8
SEQ = 4096
DEPTH = 4

MEM_LEN = 256
SSM_WIDTH = D_MODEL // 2
SSM_GROUP = 16
SSM_GROUPS = SSM_WIDTH // SSM_GROUP
SSM_STATE = 64
DT_MIN = 1e-3
DT_MAX = 1e-1
MLA_HEADS = 8
MLA_NOPE = 64
MLA_ROPE = 32
MLA_V = 64
MLA_Q_RANK = 256
MLA_KV_RANK = 128
MLA_WIDTH = MLA_HEADS * MLA_V
ROPE_THETA = 10000.0
Q_BLOCK = 128
X_HEADS = 4
X_HEAD_DIM = 128
X_WIDTH = X_HEADS * X_HEAD_DIM
N_BRANCH = 3
IN_WIDTHS = (SSM_WIDTH, SSM_WIDTH, MLA_Q_RANK, MLA_KV_RANK, MLA_ROPE, MLA_WIDTH, X_WIDTH, X_WIDTH, N_BRANCH * D_MODEL)
D_IN = sum(IN_WIDTHS)
ALPHA = (2 * DEPTH) ** 0.25
BETA = (8 * DEPTH) ** -0.25
NORM_EPS = 1e-5
POS_OFFSET_MAX = 1024

kernel_name = 'hybrid_s5_mla_memory_gated_deepnorm'


def _layer_norm(x, g, b):
    xf = x.astype(jnp.float32)
    mu = jnp.mean(xf, axis=-1, keepdims=True)
    var = jnp.mean(jnp.square(xf - mu), axis=-1, keepdims=True)
    y = (xf - mu) * lax.rsqrt(var + NORM_EPS) * g.astype(jnp.float32) + b.astype(jnp.float32)
    return y.astype(x.dtype)


def _rms_norm(x, g):
    xf = x.astype(jnp.float32)
    y = xf * lax.rsqrt(jnp.mean(jnp.square(xf), axis=-1, keepdims=True) + NORM_EPS) * g.astype(jnp.float32)
    return y.astype(x.dtype)


def _rope_tables(positions):
    inv_freq = ROPE_THETA ** (-jnp.arange(0, MLA_ROPE, 2, dtype=jnp.float32) / MLA_ROPE)
    ang = positions.astype(jnp.float32)[..., None] * inv_freq
    return jnp.cos(ang)[:, :, None, :], jnp.sin(ang)[:, :, None, :]


def _apply_rope(t, cos, sin):
    tf = t.astype(jnp.float32)
    t1, t2 = jnp.split(tf, 2, axis=-1)
    out = jnp.concatenate([t1 * cos - t2 * sin, t1 * sin + t2 * cos], axis=-1)
    return out.astype(t.dtype)


def _complex_scan_combine(e1, e2):
    a1r, a1i, b1r, b1i = e1
    a2r, a2i, b2r, b2i = e2
    ar = a1r * a2r - a1i * a2i
    ai = a1r * a2i + a1i * a2r
    br = a2r * b1r - a2i * b1i + b2r
    bi = a2r * b1i + a2i * b1r + b2i
    return ar, ai, br, bi


def _s5_ssm(u, a_re, a_im, log_dt, b_re, b_im, c_re, c_im, d_skip):
    bsz, s, _ = u.shape
    f32 = jnp.float32
    uf = u.astype(f32).reshape(bsz, s, SSM_GROUPS, SSM_GROUP)
    dt = jnp.exp(log_dt.astype(f32))[:, None]
    lr, li = a_re.astype(f32), a_im.astype(f32)
    mag = jnp.exp(lr * dt)
    lb_re = mag * jnp.cos(li * dt)
    lb_im = mag * jnp.sin(li * dt)
    nr, ni = lb_re - 1.0, lb_im
    den = lr * lr + li * li
    f_re = (nr * lr + ni * li) / den
    f_im = (ni * lr - nr * li) / den
    br, bi = b_re.astype(f32), b_im.astype(f32)
    bb_re = f_re[..., None] * br - f_im[..., None] * bi
    bb_im = f_re[..., None] * bi + f_im[..., None] * br
    bu_re = jnp.einsum('bsgc,gpc->bsgp', uf, bb_re)
    bu_im = jnp.einsum('bsgc,gpc->bsgp', uf, bb_im)
    a_re_t = jnp.broadcast_to(lb_re, bu_re.shape)
    a_im_t = jnp.broadcast_to(lb_im, bu_im.shape)
    _, _, h_re, h_im = lax.associative_scan(_complex_scan_combine, (a_re_t, a_im_t, bu_re, bu_im), axis=1)
    y = (jnp.einsum('bsgp,gcp->bsgc', h_re, c_re.astype(f32))
         - jnp.einsum('bsgp,gcp->bsgc', h_im, c_im.astype(f32)))
    y = y.reshape(bsz, s, SSM_WIDTH) + d_skip.astype(f32) * u.astype(f32)
    return y.astype(u.dtype)


def _s5_glu(y, w_glu, b_glu):
    g = jax.nn.gelu(y)
    a, b = jnp.split(g @ w_glu + b_glu, 2, axis=-1)
    return a * jax.nn.sigmoid(b)


def _mla_attention(c_q, c_kv, k_rope_in, q_norm, w_uq, kv_norm, w_ukv, cos, sin):
    bsz, s, _ = c_q.shape
    q = (_rms_norm(c_q, q_norm) @ w_uq).reshape(bsz, s, MLA_HEADS, MLA_NOPE + MLA_ROPE)
    q_nope = q[..., :MLA_NOPE]
    q_rope = _apply_rope(q[..., MLA_NOPE:], cos, sin)
    kv = (_rms_norm(c_kv, kv_norm) @ w_ukv).reshape(bsz, s, MLA_HEADS, MLA_NOPE + MLA_V)
    k_nope, v = kv[..., :MLA_NOPE], kv[..., MLA_NOPE:]
    k_rope = _apply_rope(k_rope_in[:, :, None, :], cos, sin)[:, :, 0, :]
    scale = (MLA_NOPE + MLA_ROPE) ** -0.5
    neg = jnp.finfo(jnp.float32).min
    outs = []
    for blk in range(s // Q_BLOCK):
        q0 = blk * Q_BLOCK
        kend = q0 + Q_BLOCK
        sc = (jnp.einsum('bqhd,bkhd->bhqk', q_nope[:, q0:kend], k_nope[:, :kend])
              + jnp.einsum('bqhr,bkr->bhqk', q_rope[:, q0:kend], k_rope[:, :kend]))
        sc = sc.astype(jnp.float32) * scale
        causal = (q0 + jnp.arange(Q_BLOCK))[:, None] >= jnp.arange(kend)[None, :]
        p = jax.nn.softmax(jnp.where(causal, sc, neg), axis=-1)
        outs.append(jnp.einsum('bhqk,bkhd->bqhd', p.astype(v.dtype), v[:, :kend]))
    return jnp.concatenate(outs, axis=1).reshape(bsz, s, MLA_WIDTH)


def _memory_attention(q_in, mem, w_mem_kv):
    bsz, s, _ = q_in.shape
    m = mem.shape[1]
    kv = (mem @ w_mem_kv).reshape(bsz, m, 2, X_HEADS, X_HEAD_DIM)
    k, v = kv[:, :, 0], kv[:, :, 1]
    q = q_in.reshape(bsz, s, X_HEADS, X_HEAD_DIM)
    sc = jnp.einsum('bshd,bmhd->bhsm', q, k).astype(jnp.float32) * X_HEAD_DIM ** -0.5
    p = jax.nn.softmax(sc, axis=-1)
    return jnp.einsum('bhsm,bmhd->bshd', p.astype(v.dtype), v).reshape(bsz, s, X_WIDTH)


def _fwd_setup_inputs(seed: int = 0) -> dict:
    key = jax.random.key(seed)
    ks = jax.random.split(key, 26)
    f32 = jnp.float32

    def nrm(k, shape, scale):
        return scale * jax.random.normal(k, shape, f32)

    x = nrm(ks[0], (BATCH, SEQ, D_MODEL), 1.0)
    mem = nrm(ks[1], (BATCH, MEM_LEN, D_MODEL), 1.0)
    offsets = jax.random.randint(ks[2], (BATCH, 1), 0, POS_OFFSET_MAX, dtype=jnp.int32)
    positions = offsets + jnp.arange(SEQ, dtype=jnp.int32)[None, :]
    w_in = nrm(ks[3], (DEPTH, D_MODEL, D_IN), D_MODEL ** -0.5)
    b_gate = nrm(ks[4], (DEPTH, N_BRANCH * D_MODEL), 0.01)
    ssm_a_re = -0.5 + nrm(ks[5], (DEPTH, SSM_GROUPS, SSM_STATE), 0.01)
    ssm_a_im = math.pi * jnp.arange(SSM_STATE, dtype=f32) + nrm(ks[6], (DEPTH, SSM_GROUPS, SSM_STATE), 0.01)
    ssm_log_dt = jax.random.uniform(ks[7], (DEPTH, SSM_GROUPS), f32, math.log(DT_MIN), math.log(DT_MAX))
    ssm_b_re = nrm(ks[8], (DEPTH, SSM_GROUPS, SSM_STATE, SSM_GROUP), (2 * SSM_GROUP) ** -0.5)
    ssm_b_im = nrm(ks[9], (DEPTH, SSM_GROUPS, SSM_STATE, SSM_GROUP), (2 * SSM_GROUP) ** -0.5)
    ssm_c_re = nrm(ks[10], (DEPTH, SSM_GROUPS, SSM_GROUP, SSM_STATE), (2 * SSM_STATE) ** -0.5)
    ssm_c_im = nrm(ks[11], (DEPTH, SSM_GROUPS, SSM_GROUP, SSM_STATE), (2 * SSM_STATE) ** -0.5)
    ssm_d = nrm(ks[12], (DEPTH, SSM_WIDTH), 1.0)
    w_glu = nrm(ks[13], (DEPTH, SSM_WIDTH, 2 * SSM_WIDTH), SSM_WIDTH ** -0.5)
    b_glu = nrm(ks[14], (DEPTH, 2 * SSM_WIDTH), 0.01)
    mla_q_norm = 1.0 + nrm(ks[15], (DEPTH, MLA_Q_RANK), 0.01)
    w_uq = nrm(ks[16], (DEPTH, MLA_Q_RANK, MLA_HEADS * (MLA_NOPE + MLA_ROPE)), MLA_Q_RANK ** -0.5)
    mla_kv_norm = 1.0 + nrm(ks[17], (DEPTH, MLA_KV_RANK), 0.01)
    w_ukv = nrm(ks[18], (DEPTH, MLA_KV_RANK, MLA_HEADS * (MLA_NOPE + MLA_V)), MLA_KV_RANK ** -0.5)
    w_mem_kv = nrm(ks[19], (DEPTH, D_MODEL, 2 * X_WIDTH), D_MODEL ** -0.5)
    p_ssm = nrm(ks[20], (DEPTH, SSM_WIDTH, D_MODEL), BETA * SSM_WIDTH ** -0.5)
    p_mla = nrm(ks[21], (DEPTH, MLA_WIDTH, D_MODEL), BETA * MLA_WIDTH ** -0.5)
    p_mem = nrm(ks[22], (DEPTH, X_WIDTH, D_MODEL), BETA * X_WIDTH ** -0.5)
    w_out = nrm(ks[23], (DEPTH, D_MODEL, D_MODEL), BETA * D_MODEL ** -0.5)
    ln_g = 1.0 + nrm(ks[24], (DEPTH, D_MODEL), 0.01)
    ln_b = nrm(ks[25], (DEPTH, D_MODEL), 0.01)
    return {'x': x, 'mem': mem, 'positions': positions, 'w_in': w_in, 'b_gate': b_gate,
            'ssm_a_re': ssm_a_re, 'ssm_a_im': ssm_a_im, 'ssm_log_dt': ssm_log_dt,
            'ssm_b_re': ssm_b_re, 'ssm_b_im': ssm_b_im, 'ssm_c_re': ssm_c_re, 'ssm_c_im': ssm_c_im,
            'ssm_d': ssm_d, 'w_glu': w_glu, 'b_glu': b_glu,
            'mla_q_norm': mla_q_norm, 'w_uq': w_uq, 'mla_kv_norm': mla_kv_norm, 'w_ukv': w_ukv,
            'w_mem_kv': w_mem_kv, 'p_ssm': p_ssm, 'p_mla': p_mla, 'p_mem': p_mem,
            'w_out': w_out, 'ln_g': ln_g, 'ln_b': ln_b}


def _fwd_reference(x, mem, positions, w_in, b_gate, ssm_a_re, ssm_a_im, ssm_log_dt, ssm_b_re, ssm_b_im,
              ssm_c_re, ssm_c_im, ssm_d, w_glu, b_glu, mla_q_norm, w_uq, mla_kv_norm, w_ukv,
              w_mem_kv, p_ssm, p_mla, p_mem, w_out, ln_g, ln_b):
    bsz, s, d = x.shape
    cos, sin = _rope_tables(positions)
    split_points = np.cumsum(IN_WIDTHS)[:-1].tolist()
    for l in range(DEPTH):
        proj = x @ w_in[l]
        u, z_ssm, c_q, c_kv, k_rope, z_mla, q_mem, z_mem, gate_logits = jnp.split(proj, split_points, axis=-1)
        gates = jax.nn.sigmoid((gate_logits + b_gate[l]).astype(jnp.float32)).astype(x.dtype)
        gates = gates.reshape(bsz, s, N_BRANCH, d)
        y_ssm = _s5_ssm(u, ssm_a_re[l], ssm_a_im[l], ssm_log_dt[l], ssm_b_re[l], ssm_b_im[l],
                        ssm_c_re[l], ssm_c_im[l], ssm_d[l])
        y_ssm = _s5_glu(y_ssm, w_glu[l], b_glu[l]) * jax.nn.silu(z_ssm)
        y_mla = _mla_attention(c_q, c_kv, k_rope, mla_q_norm[l], w_uq[l], mla_kv_norm[l], w_ukv[l], cos, sin)
        y_mla = y_mla * jax.nn.silu(z_mla)
        y_mem = _memory_attention(q_mem, mem, w_mem_kv[l]) * jax.nn.silu(z_mem)
        merged = (gates[:, :, 0] * (y_ssm @ p_ssm[l])
                  + gates[:, :, 1] * (y_mla @ p_mla[l])
                  + gates[:, :, 2] * (y_mem @ p_mem[l]))
        x = _layer_norm(ALPHA * x + merged @ w_out[l], ln_g[l], ln_b[l])
    return x


import jax as _jax
import jax.numpy as _jnp

TWIN_FORMAT = 'train_step'
FWD_PARAMS = ['x', 'mem', 'positions', 'w_in', 'b_gate', 'ssm_a_re', 'ssm_a_im', 'ssm_log_dt', 'ssm_b_re', 'ssm_b_im', 'ssm_c_re', 'ssm_c_im', 'ssm_d', 'w_glu', 'b_glu', 'mla_q_norm', 'w_uq', 'mla_kv_norm', 'w_ukv', 'w_mem_kv', 'p_ssm', 'p_mla', 'p_mem', 'w_out', 'ln_g', 'ln_b']
TWIN_WEIGHTS = ['w_in', 'b_gate', 'ssm_a_re', 'ssm_a_im', 'ssm_log_dt', 'ssm_b_re', 'ssm_b_im', 'ssm_c_re', 'ssm_c_im', 'ssm_d', 'w_glu', 'b_glu', 'mla_q_norm', 'w_uq', 'mla_kv_norm', 'w_ukv', 'w_mem_kv', 'p_ssm', 'p_mla', 'p_mem', 'w_out', 'ln_g', 'ln_b']
TWIN_DIFF_INPUT = 'x'
TWIN_INPUTS = ['x', 'mem', 'positions', 'w_in', 'b_gate', 'ssm_a_re', 'ssm_a_im', 'ssm_log_dt', 'ssm_b_re', 'ssm_b_im', 'ssm_c_re', 'ssm_c_im', 'ssm_d', 'w_glu', 'b_glu', 'mla_q_norm', 'w_uq', 'mla_kv_norm', 'w_ukv', 'w_mem_kv', 'p_ssm', 'p_mla', 'p_mem', 'w_out', 'ln_g', 'ln_b', 'loss_target', 'm_w_in', 'm_b_gate', 'm_ssm_a_re', 'm_ssm_a_im', 'm_ssm_log_dt', 'm_ssm_b_re', 'm_ssm_b_im', 'm_ssm_c_re', 'm_ssm_c_im', 'm_ssm_d', 'm_w_glu', 'm_b_glu', 'm_mla_q_norm', 'm_w_uq', 'm_mla_kv_norm', 'm_w_ukv', 'm_w_mem_kv', 'm_p_ssm', 'm_p_mla', 'm_p_mem', 'm_w_out', 'm_ln_g', 'm_ln_b', 'v_w_in', 'v_b_gate', 'v_ssm_a_re', 'v_ssm_a_im', 'v_ssm_log_dt', 'v_ssm_b_re', 'v_ssm_b_im', 'v_ssm_c_re', 'v_ssm_c_im', 'v_ssm_d', 'v_w_glu', 'v_b_glu', 'v_mla_q_norm', 'v_w_uq', 'v_mla_kv_norm', 'v_w_ukv', 'v_w_mem_kv', 'v_p_ssm', 'v_p_mla', 'v_p_mem', 'v_w_out', 'v_ln_g', 'v_ln_b']
TWIN_OUTPUTS = ['loss', 'grad_x', 'grad_w_in', 'grad_b_gate', 'grad_ssm_a_re', 'grad_ssm_a_im', 'grad_ssm_log_dt', 'grad_ssm_b_re', 'grad_ssm_b_im', 'grad_ssm_c_re', 'grad_ssm_c_im', 'grad_ssm_d', 'grad_w_glu', 'grad_b_glu', 'grad_mla_q_norm', 'grad_w_uq', 'grad_mla_kv_norm', 'grad_w_ukv', 'grad_w_mem_kv', 'grad_p_ssm', 'grad_p_mla', 'grad_p_mem', 'grad_w_out', 'grad_ln_g', 'grad_ln_b', 'delta_w_in', 'delta_b_gate', 'delta_ssm_a_re', 'delta_ssm_a_im', 'delta_ssm_log_dt', 'delta_ssm_b_re', 'delta_ssm_b_im', 'delta_ssm_c_re', 'delta_ssm_c_im', 'delta_ssm_d', 'delta_w_glu', 'delta_b_glu', 'delta_mla_q_norm', 'delta_w_uq', 'delta_mla_kv_norm', 'delta_w_ukv', 'delta_w_mem_kv', 'delta_p_ssm', 'delta_p_mla', 'delta_p_mem', 'delta_w_out', 'delta_ln_g', 'delta_ln_b', 'new_m_w_in', 'new_m_b_gate', 'new_m_ssm_a_re', 'new_m_ssm_a_im', 'new_m_ssm_log_dt', 'new_m_ssm_b_re', 'new_m_ssm_b_im', 'new_m_ssm_c_re', 'new_m_ssm_c_im', 'new_m_ssm_d', 'new_m_w_glu', 'new_m_b_glu', 'new_m_mla_q_norm', 'new_m_w_uq', 'new_m_mla_kv_norm', 'new_m_w_ukv', 'new_m_w_mem_kv', 'new_m_p_ssm', 'new_m_p_mla', 'new_m_p_mem', 'new_m_w_out', 'new_m_ln_g', 'new_m_ln_b', 'new_v_w_in', 'new_v_b_gate', 'new_v_ssm_a_re', 'new_v_ssm_a_im', 'new_v_ssm_log_dt', 'new_v_ssm_b_re', 'new_v_ssm_b_im', 'new_v_ssm_c_re', 'new_v_ssm_c_im', 'new_v_ssm_d', 'new_v_w_glu', 'new_v_b_glu', 'new_v_mla_q_norm', 'new_v_w_uq', 'new_v_mla_kv_norm', 'new_v_w_ukv', 'new_v_w_mem_kv', 'new_v_p_ssm', 'new_v_p_mla', 'new_v_p_mem', 'new_v_w_out', 'new_v_ln_g', 'new_v_ln_b']
TWIN_LEAF_KINDS = {'loss': 'loss', 'grad_x': 'grad_x', 'grad_w_in': 'grad_w', 'grad_b_gate': 'grad_w', 'grad_ssm_a_re': 'grad_w', 'grad_ssm_a_im': 'grad_w', 'grad_ssm_log_dt': 'grad_w', 'grad_ssm_b_re': 'grad_w', 'grad_ssm_b_im': 'grad_w', 'grad_ssm_c_re': 'grad_w', 'grad_ssm_c_im': 'grad_w', 'grad_ssm_d': 'grad_w', 'grad_w_glu': 'grad_w', 'grad_b_glu': 'grad_w', 'grad_mla_q_norm': 'grad_w', 'grad_w_uq': 'grad_w', 'grad_mla_kv_norm': 'grad_w', 'grad_w_ukv': 'grad_w', 'grad_w_mem_kv': 'grad_w', 'grad_p_ssm': 'grad_w', 'grad_p_mla': 'grad_w', 'grad_p_mem': 'grad_w', 'grad_w_out': 'grad_w', 'grad_ln_g': 'grad_w', 'grad_ln_b': 'grad_w', 'delta_w_in': 'delta_w', 'delta_b_gate': 'delta_w', 'delta_ssm_a_re': 'delta_w', 'delta_ssm_a_im': 'delta_w', 'delta_ssm_log_dt': 'delta_w', 'delta_ssm_b_re': 'delta_w', 'delta_ssm_b_im': 'delta_w', 'delta_ssm_c_re': 'delta_w', 'delta_ssm_c_im': 'delta_w', 'delta_ssm_d': 'delta_w', 'delta_w_glu': 'delta_w', 'delta_b_glu': 'delta_w', 'delta_mla_q_norm': 'delta_w', 'delta_w_uq': 'delta_w', 'delta_mla_kv_norm': 'delta_w', 'delta_w_ukv': 'delta_w', 'delta_w_mem_kv': 'delta_w', 'delta_p_ssm': 'delta_w', 'delta_p_mla': 'delta_w', 'delta_p_mem': 'delta_w', 'delta_w_out': 'delta_w', 'delta_ln_g': 'delta_w', 'delta_ln_b': 'delta_w', 'new_m_w_in': 'new_m', 'new_m_b_gate': 'new_m', 'new_m_ssm_a_re': 'new_m', 'new_m_ssm_a_im': 'new_m', 'new_m_ssm_log_dt': 'new_m', 'new_m_ssm_b_re': 'new_m', 'new_m_ssm_b_im': 'new_m', 'new_m_ssm_c_re': 'new_m', 'new_m_ssm_c_im': 'new_m', 'new_m_ssm_d': 'new_m', 'new_m_w_glu': 'new_m', 'new_m_b_glu': 'new_m', 'new_m_mla_q_norm': 'new_m', 'new_m_w_uq': 'new_m', 'new_m_mla_kv_norm': 'new_m', 'new_m_w_ukv': 'new_m', 'new_m_w_mem_kv': 'new_m', 'new_m_p_ssm': 'new_m', 'new_m_p_mla': 'new_m', 'new_m_p_mem': 'new_m', 'new_m_w_out': 'new_m', 'new_m_ln_g': 'new_m', 'new_m_ln_b': 'new_m', 'new_v_w_in': 'new_v', 'new_v_b_gate': 'new_v', 'new_v_ssm_a_re': 'new_v', 'new_v_ssm_a_im': 'new_v', 'new_v_ssm_log_dt': 'new_v', 'new_v_ssm_b_re': 'new_v', 'new_v_ssm_b_im': 'new_v', 'new_v_ssm_c_re': 'new_v', 'new_v_ssm_c_im': 'new_v', 'new_v_ssm_d': 'new_v', 'new_v_w_glu': 'new_v', 'new_v_b_glu': 'new_v', 'new_v_mla_q_norm': 'new_v', 'new_v_w_uq': 'new_v', 'new_v_mla_kv_norm': 'new_v', 'new_v_w_ukv': 'new_v', 'new_v_w_mem_kv': 'new_v', 'new_v_p_ssm': 'new_v', 'new_v_p_mla': 'new_v', 'new_v_p_mem': 'new_v', 'new_v_w_out': 'new_v', 'new_v_ln_g': 'new_v', 'new_v_ln_b': 'new_v'}


def _forward(args):
    return _fwd_reference(*[args[k] for k in FWD_PARAMS])


def _output_shape():
    out = _jax.eval_shape(lambda: _forward(_fwd_setup_inputs(0)))
    return out.shape, out.dtype

N_MICROBATCH = 1
ADAM_LR = 0.001
ADAM_B1 = 0.9
ADAM_B2 = 0.999
ADAM_EPS = 1e-08
ADAM_WD = 0.01
ADAM_STEP = 10
PER_EXAMPLE_BATCH_AXIS = {'x': 0, 'mem': 0, 'positions': 0, 'loss_target': 0}
SHARED_INPUTS = []
_WEIGHT_DTYPES = {'w_in': _jnp.float32, 'b_gate': _jnp.float32, 'ssm_a_re': _jnp.float32, 'ssm_a_im': _jnp.float32, 'ssm_log_dt': _jnp.float32, 'ssm_b_re': _jnp.float32, 'ssm_b_im': _jnp.float32, 'ssm_c_re': _jnp.float32, 'ssm_c_im': _jnp.float32, 'ssm_d': _jnp.float32, 'w_glu': _jnp.float32, 'b_glu': _jnp.float32, 'mla_q_norm': _jnp.float32, 'w_uq': _jnp.float32, 'mla_kv_norm': _jnp.float32, 'w_ukv': _jnp.float32, 'w_mem_kv': _jnp.float32, 'p_ssm': _jnp.float32, 'p_mla': _jnp.float32, 'p_mem': _jnp.float32, 'w_out': _jnp.float32, 'ln_g': _jnp.float32, 'ln_b': _jnp.float32}
MOMENT_SCALE = {'w_in': 1.708731e-03, 'b_gate': 5.730685e-04, 'ssm_a_re': 1.738008e-04, 'ssm_a_im': 1.766356e-04, 'ssm_log_dt': 2.095644e-01, 'ssm_b_re': 1.025149e-04, 'ssm_b_im': 1.044495e-04, 'ssm_c_re': 2.057670e-04, 'ssm_c_im': 2.054544e-04, 'ssm_d': 3.221494e-03, 'w_glu': 2.209824e-03, 'b_glu': 3.642160e-03, 'mla_q_norm': 2.264739e-03, 'w_uq': 1.292721e-03, 'mla_kv_norm': 5.055484e-03, 'w_ukv': 1.655530e-03, 'w_mem_kv': 9.027731e-04, 'p_ssm': 4.946867e-03, 'p_mla': 3.219648e-03, 'p_mem': 1.524488e-03, 'w_out': 6.054594e-03, 'ln_g': 1.599105e+01, 'ln_b': 3.316391e-01}


def _to_microbatches(a, axis):
    t = _jnp.moveaxis(a, axis, 0)
    t = t.reshape((N_MICROBATCH, t.shape[0] // N_MICROBATCH) + t.shape[1:])
    return _jnp.moveaxis(t, 1, axis + 1)


def setup_inputs(seed: int = 0) -> dict:
    inp = _fwd_setup_inputs(seed)
    key = _jax.random.fold_in(_jax.random.key(seed), 7919)
    shape, _ = _output_shape()
    out = dict(inp)
    out["loss_target"] = _jax.random.normal(_jax.random.fold_in(key, 0), shape, _jnp.float32)
    for i, name in enumerate(TWIN_WEIGHTS):
        w = inp[name].astype(_jnp.float32)
        if MOMENT_SCALE is None:
            s = _jnp.sqrt(_jnp.mean(_jnp.square(w)) + 1e-30)
        else:
            s = MOMENT_SCALE[name]
        km, kv = _jax.random.split(_jax.random.fold_in(key, i + 1))
        out[name] = w
        out["m_" + name] = s * _jax.random.normal(km, w.shape, _jnp.float32)
        out["v_" + name] = (s * s) * _jax.random.uniform(kv, w.shape, _jnp.float32, 0.5, 1.5)
    if N_MICROBATCH > 1:
        for name, axis in PER_EXAMPLE_BATCH_AXIS.items():
            out[name] = _to_microbatches(out[name], axis)
    return {'x': out['x'], 'mem': out['mem'], 'positions': out['positions'], 'w_in': out['w_in'], 'b_gate': out['b_gate'], 'ssm_a_re': out['ssm_a_re'], 'ssm_a_im': out['ssm_a_im'], 'ssm_log_dt': out['ssm_log_dt'], 'ssm_b_re': out['ssm_b_re'], 'ssm_b_im': out['ssm_b_im'], 'ssm_c_re': out['ssm_c_re'], 'ssm_c_im': out['ssm_c_im'], 'ssm_d': out['ssm_d'], 'w_glu': out['w_glu'], 'b_glu': out['b_glu'], 'mla_q_norm': out['mla_q_norm'], 'w_uq': out['w_uq'], 'mla_kv_norm': out['mla_kv_norm'], 'w_ukv': out['w_ukv'], 'w_mem_kv': out['w_mem_kv'], 'p_ssm': out['p_ssm'], 'p_mla': out['p_mla'], 'p_mem': out['p_mem'], 'w_out': out['w_out'], 'ln_g': out['ln_g'], 'ln_b': out['ln_b'], 'loss_target': out['loss_target'], 'm_w_in': out['m_w_in'], 'm_b_gate': out['m_b_gate'], 'm_ssm_a_re': out['m_ssm_a_re'], 'm_ssm_a_im': out['m_ssm_a_im'], 'm_ssm_log_dt': out['m_ssm_log_dt'], 'm_ssm_b_re': out['m_ssm_b_re'], 'm_ssm_b_im': out['m_ssm_b_im'], 'm_ssm_c_re': out['m_ssm_c_re'], 'm_ssm_c_im': out['m_ssm_c_im'], 'm_ssm_d': out['m_ssm_d'], 'm_w_glu': out['m_w_glu'], 'm_b_glu': out['m_b_glu'], 'm_mla_q_norm': out['m_mla_q_norm'], 'm_w_uq': out['m_w_uq'], 'm_mla_kv_norm': out['m_mla_kv_norm'], 'm_w_ukv': out['m_w_ukv'], 'm_w_mem_kv': out['m_w_mem_kv'], 'm_p_ssm': out['m_p_ssm'], 'm_p_mla': out['m_p_mla'], 'm_p_mem': out['m_p_mem'], 'm_w_out': out['m_w_out'], 'm_ln_g': out['m_ln_g'], 'm_ln_b': out['m_ln_b'], 'v_w_in': out['v_w_in'], 'v_b_gate': out['v_b_gate'], 'v_ssm_a_re': out['v_ssm_a_re'], 'v_ssm_a_im': out['v_ssm_a_im'], 'v_ssm_log_dt': out['v_ssm_log_dt'], 'v_ssm_b_re': out['v_ssm_b_re'], 'v_ssm_b_im': out['v_ssm_b_im'], 'v_ssm_c_re': out['v_ssm_c_re'], 'v_ssm_c_im': out['v_ssm_c_im'], 'v_ssm_d': out['v_ssm_d'], 'v_w_glu': out['v_w_glu'], 'v_b_glu': out['v_b_glu'], 'v_mla_q_norm': out['v_mla_q_norm'], 'v_w_uq': out['v_w_uq'], 'v_mla_kv_norm': out['v_mla_kv_norm'], 'v_w_ukv': out['v_w_ukv'], 'v_w_mem_kv': out['v_w_mem_kv'], 'v_p_ssm': out['v_p_ssm'], 'v_p_mla': out['v_p_mla'], 'v_p_mem': out['v_p_mem'], 'v_w_out': out['v_w_out'], 'v_ln_g': out['v_ln_g'], 'v_ln_b': out['v_ln_b']}


def _loss(weights, diff, rest, loss_target):
    with _jax.named_scope("forward"):
        args = {**rest, TWIN_DIFF_INPUT: diff, **{k: w.astype(_WEIGHT_DTYPES[k]) for k, w in weights.items()}}
        y = _forward(args)
    with _jax.named_scope("loss_head"):
        err = _jnp.square(y.astype(_jnp.float32) - loss_target)
        return 0.5 * _jnp.sum(_jnp.mean(err, axis=-1)) if err.ndim else 0.5 * err


def _adamw(w, g, m, v):
    m = ADAM_B1 * m + (1.0 - ADAM_B1) * g
    v = ADAM_B2 * v + (1.0 - ADAM_B2) * _jnp.square(g)
    m_hat = m / (1.0 - ADAM_B1 ** ADAM_STEP)
    v_hat = v / (1.0 - ADAM_B2 ** ADAM_STEP)
    delta = -ADAM_LR * (m_hat / (_jnp.sqrt(v_hat) + ADAM_EPS) + ADAM_WD * w)
    return delta, m, v


def reference(x, mem, positions, w_in, b_gate, ssm_a_re, ssm_a_im, ssm_log_dt, ssm_b_re, ssm_b_im, ssm_c_re, ssm_c_im, ssm_d, w_glu, b_glu, mla_q_norm, w_uq, mla_kv_norm, w_ukv, w_mem_kv, p_ssm, p_mla, p_mem, w_out, ln_g, ln_b, loss_target, m_w_in, m_b_gate, m_ssm_a_re, m_ssm_a_im, m_ssm_log_dt, m_ssm_b_re, m_ssm_b_im, m_ssm_c_re, m_ssm_c_im, m_ssm_d, m_w_glu, m_b_glu, m_mla_q_norm, m_w_uq, m_mla_kv_norm, m_w_ukv, m_w_mem_kv, m_p_ssm, m_p_mla, m_p_mem, m_w_out, m_ln_g, m_ln_b, v_w_in, v_b_gate, v_ssm_a_re, v_ssm_a_im, v_ssm_log_dt, v_ssm_b_re, v_ssm_b_im, v_ssm_c_re, v_ssm_c_im, v_ssm_d, v_w_glu, v_b_glu, v_mla_q_norm, v_w_uq, v_mla_kv_norm, v_w_ukv, v_w_mem_kv, v_p_ssm, v_p_mla, v_p_mem, v_w_out, v_ln_g, v_ln_b):
    given = dict(x=x, mem=mem, positions=positions, w_in=w_in, b_gate=b_gate, ssm_a_re=ssm_a_re, ssm_a_im=ssm_a_im, ssm_log_dt=ssm_log_dt, ssm_b_re=ssm_b_re, ssm_b_im=ssm_b_im, ssm_c_re=ssm_c_re, ssm_c_im=ssm_c_im, ssm_d=ssm_d, w_glu=w_glu, b_glu=b_glu, mla_q_norm=mla_q_norm, w_uq=w_uq, mla_kv_norm=mla_kv_norm, w_ukv=w_ukv, w_mem_kv=w_mem_kv, p_ssm=p_ssm, p_mla=p_mla, p_mem=p_mem, w_out=w_out, ln_g=ln_g, ln_b=ln_b, loss_target=loss_target, m_w_in=m_w_in, m_b_gate=m_b_gate, m_ssm_a_re=m_ssm_a_re, m_ssm_a_im=m_ssm_a_im, m_ssm_log_dt=m_ssm_log_dt, m_ssm_b_re=m_ssm_b_re, m_ssm_b_im=m_ssm_b_im, m_ssm_c_re=m_ssm_c_re, m_ssm_c_im=m_ssm_c_im, m_ssm_d=m_ssm_d, m_w_glu=m_w_glu, m_b_glu=m_b_glu, m_mla_q_norm=m_mla_q_norm, m_w_uq=m_w_uq, m_mla_kv_norm=m_mla_kv_norm, m_w_ukv=m_w_ukv, m_w_mem_kv=m_w_mem_kv, m_p_ssm=m_p_ssm, m_p_mla=m_p_mla, m_p_mem=m_p_mem, m_w_out=m_w_out, m_ln_g=m_ln_g, m_ln_b=m_ln_b, v_w_in=v_w_in, v_b_gate=v_b_gate, v_ssm_a_re=v_ssm_a_re, v_ssm_a_im=v_ssm_a_im, v_ssm_log_dt=v_ssm_log_dt, v_ssm_b_re=v_ssm_b_re, v_ssm_b_im=v_ssm_b_im, v_ssm_c_re=v_ssm_c_re, v_ssm_c_im=v_ssm_c_im, v_ssm_d=v_ssm_d, v_w_glu=v_w_glu, v_b_glu=v_b_glu, v_mla_q_norm=v_mla_q_norm, v_w_uq=v_w_uq, v_mla_kv_norm=v_mla_kv_norm, v_w_ukv=v_w_ukv, v_w_mem_kv=v_w_mem_kv, v_p_ssm=v_p_ssm, v_p_mla=v_p_mla, v_p_mem=v_p_mem, v_w_out=v_w_out, v_ln_g=v_ln_g, v_ln_b=v_ln_b)
    weights = {n: given[n] for n in TWIN_WEIGHTS}
    shared = {n: given[n] for n in SHARED_INPUTS}
    per_example = {n: given[n] for n in ['x', 'mem', 'positions']}
    grad_fn = _jax.value_and_grad(_loss, argnums=(0, 1))

    def one_microbatch(ex, loss_target):
        ex = dict(ex)
        diff = ex.pop(TWIN_DIFF_INPUT)
        return grad_fn(weights, diff, {**shared, **ex}, loss_target)

    if N_MICROBATCH == 1:
        loss, (grad_w, grad_x) = one_microbatch(per_example, given["loss_target"])
    else:
        def body(carry, xs):
            loss_sum, grad_sum = carry
            l_k, (gw_k, gx_k) = one_microbatch(xs[0], xs[1])
            with _jax.named_scope("update"):
                return (loss_sum + l_k, _jax.tree.map(_jnp.add, grad_sum, gw_k)), gx_k

        init = (_jnp.zeros((), _jnp.float32), _jax.tree.map(_jnp.zeros_like, weights))
        (loss, grad_w), grad_x = _jax.lax.scan(body, init, (per_example, given["loss_target"]))
    with _jax.named_scope("update"):
        delta_w, new_m, new_v = {}, {}, {}
        for n in TWIN_WEIGHTS:
            delta_w[n], new_m[n], new_v[n] = _adamw(weights[n], grad_w[n], given["m_" + n], given["v_" + n])
    return (loss, grad_x, *[grad_w[n] for n in TWIN_WEIGHTS], *[delta_w[n] for n in TWIN_WEIGHTS],
            *[new_m[n] for n in TWIN_WEIGHTS], *[new_v[n] for n in TWIN_WEIGHTS])
```

```python
import functools
import math

import jax
import jax.numpy as jnp
from jax import lax
from jax.experimental import pallas as pl
from jax.experimental.pallas import tpu as pltpu

F32 = jnp.float32
BF = jnp.bfloat16

D_MODEL = 1024
DEPTH = 4
N_DEV = 8
SSM_WIDTH = 512
SSM_GROUP = 16
SSM_GROUPS = 32
SSM_STATE = 64
MLA_HEADS = 8
MLA_NOPE = 64
MLA_ROPE = 32
MLA_V = 64
MLA_Q_RANK = 256
MLA_KV_RANK = 128
ROPE_THETA = 10000.0
X_HEADS = 4
X_HEAD_DIM = 128
D_IN = 6048
ALPHA = (2 * DEPTH) ** 0.25
NORM_EPS = 1e-5
ADAM_LR = 0.001
ADAM_B1 = 0.9
ADAM_B2 = 0.999
ADAM_EPS = 1e-08
ADAM_WD = 0.01
ADAM_STEP = 10

LANES = 128
SUBLANES = 8
VMEM_LIMIT = 56 * 1024 * 1024

PW = 6144
ROPE_SLOT_LO = 1408
MLA_SCALE = (MLA_NOPE + MLA_ROPE) ** -0.5
MEM_SCALE = X_HEAD_DIM ** -0.5
NEG = -1e30

T_ROWS = 512
T_ROWS_BWD = 256
T_ATT = 512
T_MM = 512

MESH = pl.DeviceIdType.MESH


def _cparams(sem):
    return pltpu.CompilerParams(dimension_semantics=sem, vmem_limit_bytes=VMEM_LIMIT)


def _dot(a, b):
    return lax.dot_general(a, b, (((1,), (0,)), ((), ())), preferred_element_type=F32)


def _dot_nt(a, b):
    return lax.dot_general(a, b, (((1,), (1,)), ((), ())), preferred_element_type=F32)


def _dot_tn(a, b):
    return lax.dot_general(a, b, (((0,), (0,)), ((), ())), preferred_element_type=F32)


def _sigmoid(x):
    return 1.0 / (1.0 + jnp.exp(-x))


def _silu(x):
    return x * _sigmoid(x)


def _dsilu(x):
    s = _sigmoid(x)
    return s * (1.0 + x * (1.0 - s))


_GELU_C = math.sqrt(2.0 / math.pi)


def _gelu(x):
    return 0.5 * x * (1.0 + jnp.tanh(_GELU_C * (x + 0.044715 * x * x * x)))


def _dgelu(x):
    t = jnp.tanh(_GELU_C * (x + 0.044715 * x * x * x))
    return 0.5 * (1.0 + t) + 0.5 * x * (1.0 - t * t) * _GELU_C * (1.0 + 3 * 0.044715 * x * x)


def _rows(tr, w, col=0):
    return pl.BlockSpec((tr, w), lambda i: (i, col))


def _full(shape):
    n = len(shape)
    return pl.BlockSpec(shape, lambda i: (0,) * n)


def _mm(a, b, *, name, ta=False, tb=False, out_dtype=F32, add=None, tm=T_MM, tn=T_MM, tk=1024):
    M, K = (a.shape[1], a.shape[0]) if ta else a.shape
    N = b.shape[0] if tb else b.shape[1]
    tm, tn, tk = min(tm, M), min(tn, N), min(tk, K)
    assert M % tm == 0 and N % tn == 0 and K % tk == 0, (M, N, K)
    nk = K // tk
    dn = (((0 if ta else 1,), (1 if tb else 0,)), ((), ()))

    def body(*refs):
        if add is not None:
            a_ref, b_ref, c_ref, o_ref, acc = refs
        else:
            a_ref, b_ref, o_ref, acc = refs
        k = pl.program_id(2)

        @pl.when(k == 0)
        def _():
            acc[...] = jnp.zeros_like(acc)

        acc[...] += lax.dot_general(a_ref[...].astype(BF), b_ref[...].astype(BF), dn, preferred_element_type=F32)

        @pl.when(k == nk - 1)
        def _():
            r = acc[...]
            if add is not None:
                r = r + c_ref[...]
            o_ref[...] = r.astype(out_dtype)

    a_spec = pl.BlockSpec((tk, tm), lambda i, j, k: (k, i)) if ta else pl.BlockSpec((tm, tk), lambda i, j, k: (i, k))
    b_spec = pl.BlockSpec((tn, tk), lambda i, j, k: (j, k)) if tb else pl.BlockSpec((tk, tn), lambda i, j, k: (k, j))
    o_spec = pl.BlockSpec((tm, tn), lambda i, j, k: (i, j))
    in_specs = [a_spec, b_spec] + ([o_spec] if add is not None else [])
    args = (a, b) + ((add,) if add is not None else ())
    return pl.pallas_call(
        body, name=name, grid=(M // tm, N // tn, nk), in_specs=in_specs, out_specs=o_spec,
        out_shape=jax.ShapeDtypeStruct((M, N), out_dtype), scratch_shapes=[pltpu.VMEM((tm, tn), F32)],
        compiler_params=_cparams(("parallel", "parallel", "arbitrary")),
    )(*args)


def _cpow(ar, ai, n):
    rr, ri = None, None
    br, bi = ar, ai
    while n:
        if n & 1:
            if rr is None:
                rr, ri = br, bi
            else:
                rr, ri = rr * br - ri * bi, rr * bi + ri * br
        n >>= 1
        if n:
            br, bi = br * br - bi * bi, 2.0 * br * bi
    return rr, ri


def _seg_shift(v, k, reverse):
    sub = lax.broadcasted_iota(jnp.int32, v.shape, 0)
    if not reverse:
        return jnp.where(sub >= k, pltpu.roll(v, k, 0), 0.0)
    return jnp.where(sub < SUBLANES - k, pltpu.roll(v, SUBLANES - k, 0), 0.0)


def _ssm_scan(hre, him, ar, ai, seglen, reverse):
    w = hre.shape[1]
    zero = jnp.zeros((SUBLANES, w), F32)

    def rows(j):
        jj = (seglen - 1 - j) if reverse else j
        return pl.ds(pl.multiple_of(jj * SUBLANES, SUBLANES), SUBLANES)

    def local(j, c):
        hr, hi = c
        r = rows(j)
        nhr = ar * hr - ai * hi + hre[r, :]
        nhi = ar * hi + ai * hr + him[r, :]
        hre[r, :] = nhr
        him[r, :] = nhi
        return nhr, nhi

    er, ei = lax.fori_loop(0, seglen, local, (zero, zero))
    pr, pi_ = _cpow(ar, ai, seglen)
    for k in (1, 2, 4):
        sr, si = _seg_shift(er, k, reverse), _seg_shift(ei, k, reverse)
        er, ei = er + pr * sr - pi_ * si, ei + pr * si + pi_ * sr
        pr, pi_ = pr * pr - pi_ * pi_, 2.0 * pr * pi_
    cr, ci = _seg_shift(er, 1, reverse), _seg_shift(ei, 1, reverse)

    def carry_in(j, c):
        qr, qi = c
        nqr = qr * ar - qi * ai
        nqi = qr * ai + qi * ar
        r = rows(j)
        hre[r, :] = hre[r, :] + (nqr * cr - nqi * ci)
        him[r, :] = him[r, :] + (nqr * ci + nqi * cr)
        return nqr, nqi

    lax.fori_loop(0, seglen, carry_in, (jnp.ones((SUBLANES, w), F32), zero))


SSM_CB = 128
SSM_SB = 256


def _ssm_specs(S):
    u_spec = pl.BlockSpec((S, SSM_CB), lambda g, h: (0, g))
    bb_spec = pl.BlockSpec((1, SSM_CB, SSM_SB), lambda g, h: (g, 0, h))
    a_spec = pl.BlockSpec((1, 1, SSM_SB), lambda g, h: (g, 0, h))
    c_spec = pl.BlockSpec((1, SSM_SB, SSM_CB), lambda g, h: (g, h, 0))
    d_spec = pl.BlockSpec((1, SSM_CB), lambda g, h: (0, g))
    return u_spec, bb_spec, a_spec, c_spec, d_spec


def _ssm_fwd(u, bbre, bbim, are, aim, cre, cim, d):
    S = u.shape[0]
    seglen = S // SUBLANES
    ch = min(512, S)
    nch = S // ch

    def body(u_ref, bbre_ref, bbim_ref, are_ref, aim_ref, cre_ref, cim_ref, d_ref, y_ref, hre, him):
        hf = pl.program_id(1)
        wre = bbre_ref[0].astype(BF)
        wim = bbim_ref[0].astype(BF)

        def mk(c, _):
            r = pl.ds(pl.multiple_of(c * ch, ch), ch)
            ub = u_ref[r, :].astype(BF)
            hre[r, :] = _dot(ub, wre)
            him[r, :] = _dot(ub, wim)
            return 0

        lax.fori_loop(0, nch, mk, 0)
        ar = jnp.broadcast_to(are_ref[0], (SUBLANES, SSM_SB))
        ai = jnp.broadcast_to(aim_ref[0], (SUBLANES, SSM_SB))
        _ssm_scan(hre, him, ar, ai, seglen, False)
        cr = cre_ref[0].astype(BF)
        ci = cim_ref[0].astype(BF)

        def out(c, _):
            r = pl.ds(pl.multiple_of(c * ch, ch), ch)
            y = _dot(hre[r, :].astype(BF), cr) - _dot(him[r, :].astype(BF), ci)

            @pl.when(hf == 0)
            def _():
                y_ref[r, :] = y + d_ref[...] * u_ref[r, :]

            @pl.when(hf != 0)
            def _():
                y_ref[r, :] = y_ref[r, :] + y

            return 0

        lax.fori_loop(0, nch, out, 0)

    u_spec, bb_spec, a_spec, c_spec, d_spec = _ssm_specs(S)
    return pl.pallas_call(
        body, name="ssm_fwd", grid=(SSM_WIDTH // SSM_CB, 2),
        in_specs=[u_spec, bb_spec, bb_spec, a_spec, a_spec, c_spec, c_spec, d_spec], out_specs=u_spec,
        out_shape=jax.ShapeDtypeStruct((S, SSM_WIDTH), F32),
        scratch_shapes=[pltpu.VMEM((S, SSM_SB), F32), pltpu.VMEM((S, SSM_SB), F32)],
        compiler_params=_cparams(("parallel", "arbitrary")),
    )(u, bbre, bbim, are, aim, cre, cim, d)


def _ssm_bwd(u, dy, bbre, bbim, are, aim, cre, cim, d):
    S = u.shape[0]
    seglen = S // SUBLANES
    ch = min(512, S)
    nch = S // ch
    nblk = SSM_WIDTH // SSM_CB

    def body(u_ref, dy_ref, bbre_ref, bbim_ref, are_ref, aim_ref, cre_ref, cim_ref, d_ref,
             du_ref, dbbre_ref, dbbim_ref, dare_ref, daim_ref, dcre_ref, dcim_ref, dd_ref,
             hre, him, lre, lim):
        hf = pl.program_id(1)
        wre = bbre_ref[0].astype(BF)
        wim = bbim_ref[0].astype(BF)
        cr = cre_ref[0].astype(BF)
        ci = cim_ref[0].astype(BF)

        def mk(c, _):
            r = pl.ds(pl.multiple_of(c * ch, ch), ch)
            ub = u_ref[r, :].astype(BF)
            hre[r, :] = _dot(ub, wre)
            him[r, :] = _dot(ub, wim)
            return 0

        lax.fori_loop(0, nch, mk, 0)
        ar = jnp.broadcast_to(are_ref[0], (SUBLANES, SSM_SB))
        ai = jnp.broadcast_to(aim_ref[0], (SUBLANES, SSM_SB))
        _ssm_scan(hre, him, ar, ai, seglen, False)

        dcre_ref[...] = jnp.zeros_like(dcre_ref)
        dcim_ref[...] = jnp.zeros_like(dcim_ref)

        @pl.when(hf == 0)
        def _():
            dd_ref[...] = jnp.zeros_like(dd_ref)

        def cot(c, _):
            r = pl.ds(pl.multiple_of(c * ch, ch), ch)
            dyv = dy_ref[r, :]
            dyb = dyv.astype(BF)
            lre[r, :] = _dot_nt(dyb, cr)
            lim[r, :] = -_dot_nt(dyb, ci)
            dcre_ref[0] = dcre_ref[0] + _dot_tn(hre[r, :].astype(BF), dyb)
            dcim_ref[0] = dcim_ref[0] - _dot_tn(him[r, :].astype(BF), dyb)

            @pl.when(hf == 0)
            def _():
                dd_ref[...] = dd_ref[...] + jnp.sum(dyv * u_ref[r, :], axis=0, keepdims=True)

            return 0

        lax.fori_loop(0, nch, cot, 0)
        _ssm_scan(lre, lim, ar, -ai, seglen, True)

        last = pl.ds((seglen - 1) * SUBLANES, SUBLANES)
        first = pl.ds(0, SUBLANES)
        pr0 = _seg_shift(hre[last, :], 1, False)
        pi0 = _seg_shift(him[last, :], 1, False)
        acr0 = lre[first, :] * pr0 + lim[first, :] * pi0
        aci0 = lim[first, :] * pr0 - lre[first, :] * pi0

        def da(j, c):
            acr, aci = c
            r = pl.ds(pl.multiple_of(j * SUBLANES, SUBLANES), SUBLANES)
            rp = pl.ds(pl.multiple_of((j - 1) * SUBLANES, SUBLANES), SUBLANES)
            lr, li = lre[r, :], lim[r, :]
            pr, pi_ = hre[rp, :], him[rp, :]
            return acr + lr * pr + li * pi_, aci + li * pr - lr * pi_

        acr, aci = lax.fori_loop(1, seglen, da, (acr0, aci0))
        dare_ref[0] = jnp.sum(acr, axis=0, keepdims=True)
        daim_ref[0] = jnp.sum(aci, axis=0, keepdims=True)

        dbbre_ref[...] = jnp.zeros_like(dbbre_ref)
        dbbim_ref[...] = jnp.zeros_like(dbbim_ref)

        def fin(c, _):
            r = pl.ds(pl.multiple_of(c * ch, ch), ch)
            lrb = lre[r, :].astype(BF)
            lib = lim[r, :].astype(BF)
            ub = u_ref[r, :].astype(BF)
            du = _dot_nt(lrb, wre) + _dot_nt(lib, wim)
            dbbre_ref[0] = dbbre_ref[0] + _dot_tn(ub, lrb)
            dbbim_ref[0] = dbbim_ref[0] + _dot_tn(ub, lib)

            @pl.when(hf == 0)
            def _():
                du_ref[r, :] = du + d_ref[...] * dy_ref[r, :]

            @pl.when(hf != 0)
            def _():
                du_ref[r, :] = du_ref[r, :] + du

            return 0

        lax.fori_loop(0, nch, fin, 0)

    u_spec, bb_spec, a_spec, c_spec, d_spec = _ssm_specs(S)
    out_shape = (
        jax.ShapeDtypeStruct((S, SSM_WIDTH), F32),
        jax.ShapeDtypeStruct((nblk, SSM_CB, 2 * SSM_SB), F32), jax.ShapeDtypeStruct((nblk, SSM_CB, 2 * SSM_SB), F32),
        jax.ShapeDtypeStruct((nblk, 1, 2 * SSM_SB), F32), jax.ShapeDtypeStruct((nblk, 1, 2 * SSM_SB), F32),
        jax.ShapeDtypeStruct((nblk, 2 * SSM_SB, SSM_CB), F32), jax.ShapeDtypeStruct((nblk, 2 * SSM_SB, SSM_CB), F32),
        jax.ShapeDtypeStruct((1, SSM_WIDTH), F32),
    )
    return pl.pallas_call(
        body, name="ssm_bwd", grid=(nblk, 2),
        in_specs=[u_spec, u_spec, bb_spec, bb_spec, a_spec, a_spec, c_spec, c_spec, d_spec],
        out_specs=(u_spec, bb_spec, bb_spec, a_spec, a_spec, c_spec, c_spec, d_spec),
        out_shape=out_shape,
        scratch_shapes=[pltpu.VMEM((S, SSM_SB), F32) for _ in range(4)],
        compiler_params=_cparams(("parallel", "arbitrary")),
    )(u, dy, bbre, bbim, are, aim, cre, cim, d)


def _ssm_post_fwd(y_raw, proj, w_glu, b_glu, p_ssm):
    S = y_raw.shape[0]
    tr = min(T_ROWS, S)

    def body(y_ref, z_ref, wg_ref, bg_ref, p_ref, o_ref):
        g = _gelu(y_ref[...])
        t = _dot(g.astype(BF), wg_ref[...]) + bg_ref[...]
        glu = t[:, :SSM_WIDTH] * _sigmoid(t[:, SSM_WIDTH:])
        ys = glu * _silu(z_ref[...])
        o_ref[...] = _dot(ys.astype(BF), p_ref[...])

    return pl.pallas_call(
        body, name="ssm_post_fwd", grid=(S // tr,),
        in_specs=[_rows(tr, 512), _rows(tr, 512, 1), _full((512, 1024)), _full((1, 1024)), _full((512, 1024))],
        out_specs=_rows(tr, 1024), out_shape=jax.ShapeDtypeStruct((S, D_MODEL), F32),
        compiler_params=_cparams(("parallel",)),
    )(y_raw, proj, w_glu, b_glu, p_ssm)


def _ssm_post_bwd(do, y_raw, proj, w_glu, b_glu, p_ssm):
    S = y_raw.shape[0]
    tr = min(T_ROWS_BWD, S)

    def body(do_ref, y_ref, z_ref, wg_ref, bg_ref, p_ref, dy_ref, dz_ref, dwg_ref, dbg_ref, dp_ref):
        @pl.when(pl.program_id(0) == 0)
        def _():
            dwg_ref[...] = jnp.zeros_like(dwg_ref)
            dbg_ref[...] = jnp.zeros_like(dbg_ref)
            dp_ref[...] = jnp.zeros_like(dp_ref)

        y = y_ref[...]
        z = z_ref[...]
        g = _gelu(y)
        gb = g.astype(BF)
        t = _dot(gb, wg_ref[...]) + bg_ref[...]
        a = t[:, :SSM_WIDTH]
        sb = _sigmoid(t[:, SSM_WIDTH:])
        glu = a * sb
        ys = glu * _silu(z)
        dob = do_ref[...].astype(BF)
        dys = _dot_nt(dob, p_ref[...])
        dp_ref[...] += _dot_tn(ys.astype(BF), dob)
        dglu = dys * _silu(z)
        dz_ref[...] = (dys * glu * _dsilu(z)).astype(dz_ref.dtype)
        dt = jnp.concatenate([dglu * sb, dglu * a * sb * (1.0 - sb)], axis=1)
        dbg_ref[...] += jnp.sum(dt, axis=0, keepdims=True)
        dtb = dt.astype(BF)
        dg = _dot_nt(dtb, wg_ref[...])
        dwg_ref[...] += _dot_tn(gb, dtb)
        dy_ref[...] = dg * _dgelu(y)

    return pl.pallas_call(
        body, name="ssm_post_bwd", grid=(S // tr,),
        in_specs=[_rows(tr, 1024), _rows(tr, 512), _rows(tr, 512, 1), _full((512, 1024)), _full((1, 1024)),
                  _full((512, 1024))],
        out_specs=(_rows(tr, 512), _rows(tr, 512), _full((512, 1024)), _full((1, 1024)), _full((512, 1024))),
        out_shape=(jax.ShapeDtypeStruct((S, 512), F32), jax.ShapeDtypeStruct((S, 512), BF),
                   jax.ShapeDtypeStruct((512, 1024), F32), jax.ShapeDtypeStruct((1, 1024), F32),
                   jax.ShapeDtypeStruct((512, 1024), F32)),
        compiler_params=_cparams(("arbitrary",)),
    )(do, y_raw, proj, w_glu, b_glu, p_ssm)


def _rope(t, c, sa, sb):
    return t * c + pltpu.roll(t, LANES - 16, 1) * sa + pltpu.roll(t, 16, 1) * sb


def _rope_t(dy, c, sa, sb):
    return dy * c + pltpu.roll(dy * sa, 16, 1) + pltpu.roll(dy * sb, LANES - 16, 1)


def _rms(x, g):
    r = lax.rsqrt(jnp.mean(x * x, axis=-1, keepdims=True) + NORM_EPS)
    return x * r * g, r


def _mla_pre_fwd(proj, q_norm, kv_norm, wuq, wk, wv, tc, tsa, tsb):
    S = proj.shape[0]
    tr = min(T_ROWS, S)

    def body(cq_ref, ckv_ref, slot_ref, qn_ref, kn_ref, wuq_ref, wk_ref, wv_ref, c_ref, sa_ref, sb_ref,
             q_out, k_out, v_out):
        c, sa, sb = c_ref[...], sa_ref[...], sb_ref[...]
        qn, _ = _rms(cq_ref[...], qn_ref[...])
        q = _dot(qn.astype(BF), wuq_ref[...]) * MLA_SCALE
        kn, _ = _rms(ckv_ref[...], kn_ref[...])
        knb = kn.astype(BF)
        kp = _dot(knb, wk_ref[...])
        v_out[...] = _dot(knb, wv_ref[...]).astype(BF)
        kr = _rope(slot_ref[...], c, sa, sb)
        for h in range(MLA_HEADS):
            cs = slice(h * LANES, (h + 1) * LANES)
            q_out[:, cs] = _rope(q[:, cs], c, sa, sb).astype(BF)
            k_out[:, cs] = (kp[:, cs] + kr).astype(BF)

    return pl.pallas_call(
        body, name="mla_pre_fwd", grid=(S // tr,),
        in_specs=[_rows(tr, 256, 4), _rows(tr, 128, 10), _rows(tr, 128, 11), _full((1, 256)), _full((1, 128)),
                  _full((256, 1024)), _full((128, 1024)), _full((128, 512)),
                  _rows(tr, 128), _rows(tr, 128), _rows(tr, 128)],
        out_specs=(_rows(tr, 1024), _rows(tr, 1024), _rows(tr, 512)),
        out_shape=(jax.ShapeDtypeStruct((S, 1024), BF), jax.ShapeDtypeStruct((S, 1024), BF),
                   jax.ShapeDtypeStruct((S, 512), BF)),
        compiler_params=_cparams(("parallel",)),
    )(proj, proj, proj, q_norm, kv_norm, wuq, wk, wv, tc, tsa, tsb)


def _mla_pre_bwd(dq, dk, dv, proj, q_norm, kv_norm, wuq, wk, wv, tc, tsa, tsb):
    S = proj.shape[0]
    tr = min(T_ROWS_BWD, S)

    def body(dq_ref, dk_ref, dv_ref, cq_ref, ckv_ref, qn_ref, kn_ref, wuq_ref, wk_ref, wv_ref, c_ref, sa_ref, sb_ref,
             dcq_ref, dckv_ref, dslot_ref, dwuq_ref, dwk_ref, dwv_ref, dqn_ref, dkn_ref, dqp):
        @pl.when(pl.program_id(0) == 0)
        def _():
            dwuq_ref[...] = jnp.zeros_like(dwuq_ref)
            dwk_ref[...] = jnp.zeros_like(dwk_ref)
            dwv_ref[...] = jnp.zeros_like(dwv_ref)
            dqn_ref[...] = jnp.zeros_like(dqn_ref)
            dkn_ref[...] = jnp.zeros_like(dkn_ref)

        c, sa, sb = c_ref[...], sa_ref[...], sb_ref[...]
        dkr = jnp.zeros((tr, LANES), F32)
        for h in range(MLA_HEADS):
            cs = slice(h * LANES, (h + 1) * LANES)
            dqp[:, cs] = (_rope_t(dq_ref[:, cs], c, sa, sb) * MLA_SCALE).astype(BF)
            dkr = dkr + dk_ref[:, cs]
        lane = lax.broadcasted_iota(jnp.int32, (tr, LANES), 1)
        in_rope = (lane >= MLA_NOPE) & (lane < MLA_NOPE + MLA_ROPE)
        dslot_ref[...] = jnp.where(in_rope, _rope_t(dkr, c, sa, sb), 0.0).astype(dslot_ref.dtype)

        cq = cq_ref[...]
        gq = qn_ref[...]
        qn, rq = _rms(cq, gq)
        dqpb = dqp[...]
        dwuq_ref[...] += _dot_tn(qn.astype(BF), dqpb)
        dqn = _dot_nt(dqpb, wuq_ref[...])
        dqn_ref[...] += jnp.sum(dqn * cq * rq, axis=0, keepdims=True)
        dyg = dqn * gq
        dcq_ref[...] = (rq * dyg - cq * (rq * rq * rq) * jnp.mean(dyg * cq, axis=-1, keepdims=True)).astype(dcq_ref.dtype)

        ckv = ckv_ref[...]
        gk = kn_ref[...]
        kn, rk = _rms(ckv, gk)
        knb = kn.astype(BF)
        dkb = dk_ref[...].astype(BF)
        dvb = dv_ref[...].astype(BF)
        dwk_ref[...] += _dot_tn(knb, dkb)
        dwv_ref[...] += _dot_tn(knb, dvb)
        dkn = _dot_nt(dkb, wk_ref[...]) + _dot_nt(dvb, wv_ref[...])
        dkn_ref[...] += jnp.sum(dkn * ckv * rk, axis=0, keepdims=True)
        dyk = dkn * gk
        dckv_ref[...] = (rk * dyk - ckv * (rk * rk * rk) * jnp.mean(dyk * ckv, axis=-1, keepdims=True)).astype(dckv_ref.dtype)

    return pl.pallas_call(
        body, name="mla_pre_bwd", grid=(S // tr,),
        in_specs=[_rows(tr, 1024), _rows(tr, 1024), _rows(tr, 512), _rows(tr, 256, 4), _rows(tr, 128, 10),
                  _full((1, 256)), _full((1, 128)), _full((256, 1024)), _full((128, 1024)), _full((128, 512)),
                  _rows(tr, 128), _rows(tr, 128), _rows(tr, 128)],
        out_specs=(_rows(tr, 256), _rows(tr, 128), _rows(tr, 128), _full((256, 1024)), _full((128, 1024)),
                   _full((128, 512)), _full((1, 256)), _full((1, 128))),
        out_shape=(jax.ShapeDtypeStruct((S, 256), BF), jax.ShapeDtypeStruct((S, 128), BF),
                   jax.ShapeDtypeStruct((S, 128), BF), jax.ShapeDtypeStruct((256, 1024), F32),
                   jax.ShapeDtypeStruct((128, 1024), F32), jax.ShapeDtypeStruct((128, 512), F32),
                   jax.ShapeDtypeStruct((1, 256), F32), jax.ShapeDtypeStruct((1, 128), F32)),
        scratch_shapes=[pltpu.VMEM((tr, 1024), BF)],
        compiler_params=_cparams(("arbitrary",)),
    )(dq, dk, dv, proj, proj, q_norm, kv_norm, wuq, wk, wv, tc, tsa, tsb)


def _causal_mask(qi, ki, tq, tk):
    row = qi * tq + lax.broadcasted_iota(jnp.int32, (tq, tk), 0)
    col = ki * tk + lax.broadcasted_iota(jnp.int32, (tq, tk), 1)
    return col <= row


def _flash_fwd(q, k, v):
    S = q.shape[0]
    t = min(T_ATT, S)
    n = S // t

    def body(q_ref, k_ref, v_ref, o_ref, lse_ref, m_s, l_s, acc):
        qi, ki = pl.program_id(1), pl.program_id(2)
        lo = lax.broadcasted_iota(jnp.int32, (t, LANES), 1) < MLA_V

        @pl.when(ki == 0)
        def _():
            m_s[...] = jnp.full_like(m_s, NEG)
            l_s[...] = jnp.zeros_like(l_s)
            acc[...] = jnp.zeros_like(acc)

        @pl.when(ki <= qi)
        def _():
            mask = _causal_mask(qi, ki, t, t)
            vv = v_ref[...]
            al, pv = [], []
            for h in range(2):
                cs = slice(h * LANES, (h + 1) * LANES)
                s = jnp.where(mask, _dot_nt(q_ref[:, cs], k_ref[:, cs]), NEG)
                m_prev = m_s[h]
                m_new = jnp.maximum(m_prev, jnp.max(s, axis=1, keepdims=True))
                alpha = jnp.exp(m_prev - m_new)
                p = jnp.exp(s - m_new[:, :1])
                l_s[h] = alpha * l_s[h] + jnp.sum(p, axis=1, keepdims=True)
                m_s[h] = m_new
                al.append(alpha)
                pv.append(_dot(p.astype(BF), vv))
            acc[...] = jnp.where(lo, al[0], al[1]) * acc[...] + jnp.where(lo, pv[0], pv[1])

        @pl.when(ki == qi)
        def _():
            o_ref[...] = acc[...] / jnp.where(lo, l_s[0], l_s[1])
            lse_ref[0] = jnp.where(lo, m_s[0] + jnp.log(l_s[0]), m_s[1] + jnp.log(l_s[1]))

    return pl.pallas_call(
        body, name="mla_flash_fwd", grid=(MLA_HEADS // 2, n, n),
        in_specs=[pl.BlockSpec((t, 256), lambda p, i, j: (i, p)),
                  pl.BlockSpec((t, 256), lambda p, i, j: (jnp.minimum(j, i), p)),
                  pl.BlockSpec((t, 128), lambda p, i, j: (jnp.minimum(j, i), p))],
        out_specs=(pl.BlockSpec((t, 128), lambda p, i, j: (i, p)),
                   pl.BlockSpec((1, t, 128), lambda p, i, j: (p, i, 0))),
        out_shape=(jax.ShapeDtypeStruct((S, 512), F32), jax.ShapeDtypeStruct((MLA_HEADS // 2, S, 128), F32)),
        scratch_shapes=[pltpu.VMEM((2, t, 128), F32), pltpu.VMEM((2, t, 128), F32), pltpu.VMEM((t, 128), F32)],
        compiler_params=_cparams(("parallel", "parallel", "arbitrary")),
    )(q, k, v)


def _flash_probs(q_ref, k_ref, v_ref, do_ref, lse_ref, dl_ref, mask, lo, h):
    cs = slice(h * LANES, (h + 1) * LANES)
    col = slice(h * MLA_V, h * MLA_V + 1)
    s = jnp.where(mask, _dot_nt(q_ref[:, cs], k_ref[:, cs]), NEG)
    p = jnp.exp(s - lse_ref[0][:, col])
    dom = jnp.where(lo if h == 0 else ~lo, do_ref[...], 0).astype(BF)
    dp = _dot_nt(dom, v_ref[...])
    ds = p * (dp - dl_ref[0][:, col])
    return p.astype(BF), ds.astype(BF), dom


def _flash_bwd_dq(q, k, v, do, lse, delta):
    S = q.shape[0]
    t = min(T_ATT, S)
    n = S // t

    def body(q_ref, k_ref, v_ref, do_ref, lse_ref, dl_ref, dq_ref, acc):
        qi, ki = pl.program_id(1), pl.program_id(2)
        lo = lax.broadcasted_iota(jnp.int32, (t, LANES), 1) < MLA_V

        @pl.when(ki == 0)
        def _():
            acc[...] = jnp.zeros_like(acc)

        @pl.when(ki <= qi)
        def _():
            mask = _causal_mask(qi, ki, t, t)
            for h in range(2):
                cs = slice(h * LANES, (h + 1) * LANES)
                _, ds, _ = _flash_probs(q_ref, k_ref, v_ref, do_ref, lse_ref, dl_ref, mask, lo, h)
                acc[:, cs] += _dot(ds, k_ref[:, cs])

        @pl.when(ki == qi)
        def _():
            dq_ref[...] = acc[...]

    return pl.pallas_call(
        body, name="mla_flash_dq", grid=(MLA_HEADS // 2, n, n),
        in_specs=[pl.BlockSpec((t, 256), lambda p, i, j: (i, p)),
                  pl.BlockSpec((t, 256), lambda p, i, j: (jnp.minimum(j, i), p)),
                  pl.BlockSpec((t, 128), lambda p, i, j: (jnp.minimum(j, i), p)),
                  pl.BlockSpec((t, 128), lambda p, i, j: (i, p)),
                  pl.BlockSpec((1, t, 128), lambda p, i, j: (p, i, 0)),
                  pl.BlockSpec((1, t, 128), lambda p, i, j: (p, i, 0))],
        out_specs=pl.BlockSpec((t, 256), lambda p, i, j: (i, p)),
        out_shape=jax.ShapeDtypeStruct((S, 1024), F32),
        scratch_shapes=[pltpu.VMEM((t, 256), F32)],
        compiler_params=_cparams(("parallel", "parallel", "arbitrary")),
    )(q, k, v, do, lse, delta)


def _flash_bwd_dkv(q, k, v, do, lse, delta):
    S = q.shape[0]
    t = min(T_ATT, S)
    n = S // t

    def body(q_ref, k_ref, v_ref, do_ref, lse_ref, dl_ref, dk_ref, dv_ref, dk_acc, dv_acc):
        ki, qi = pl.program_id(1), pl.program_id(2)
        lo = lax.broadcasted_iota(jnp.int32, (t, LANES), 1) < MLA_V

        @pl.when(qi == 0)
        def _():
            dk_acc[...] = jnp.zeros_like(dk_acc)
            dv_acc[...] = jnp.zeros_like(dv_acc)

        @pl.when(qi >= ki)
        def _():
            mask = _causal_mask(qi, ki, t, t)
            for h in range(2):
                cs = slice(h * LANES, (h + 1) * LANES)
                p, ds, dom = _flash_probs(q_ref, k_ref, v_ref, do_ref, lse_ref, dl_ref, mask, lo, h)
                dv_acc[...] += _dot_tn(p, dom)
                dk_acc[:, cs] += _dot_tn(ds, q_ref[:, cs])

        @pl.when(qi == n - 1)
        def _():
            dk_ref[...] = dk_acc[...]
            dv_ref[...] = dv_acc[...]

    return pl.pallas_call(
        body, name="mla_flash_dkv", grid=(MLA_HEADS // 2, n, n),
        in_specs=[pl.BlockSpec((t, 256), lambda p, j, i: (jnp.maximum(i, j), p)),
                  pl.BlockSpec((t, 256), lambda p, j, i: (j, p)),
                  pl.BlockSpec((t, 128), lambda p, j, i: (j, p)),
                  pl.BlockSpec((t, 128), lambda p, j, i: (jnp.maximum(i, j), p)),
                  pl.BlockSpec((1, t, 128), lambda p, j, i: (p, jnp.maximum(i, j), 0)),
                  pl.BlockSpec((1, t, 128), lambda p, j, i: (p, jnp.maximum(i, j), 0))],
        out_specs=(pl.BlockSpec((t, 256), lambda p, j, i: (j, p)), pl.BlockSpec((t, 128), lambda p, j, i: (j, p))),
        out_shape=(jax.ShapeDtypeStruct((S, 1024), F32), jax.ShapeDtypeStruct((S, 512), F32)),
        scratch_shapes=[pltpu.VMEM((t, 256), F32), pltpu.VMEM((t, 128), F32)],
        compiler_params=_cparams(("parallel", "parallel", "arbitrary")),
    )(q, k, v, do, lse, delta)


def _mem_heads(qm, km_ref, vm_ref):
    ps, os_ = [], []
    for h in range(X_HEADS):
        cs = slice(h * X_HEAD_DIM, (h + 1) * X_HEAD_DIM)
        s = _dot_nt(qm[:, cs].astype(BF), km_ref[:, cs]) * MEM_SCALE
        e = jnp.exp(s - jnp.max(s, axis=1, keepdims=True))
        p = e / jnp.sum(e, axis=1, keepdims=True)
        ps.append(p)
        os_.append(_dot(p.astype(BF), vm_ref[:, cs]))
    return ps, jnp.concatenate(os_, axis=1)


def _mem_fwd(proj, km, vm, p_mem):
    S = proj.shape[0]
    tr = min(T_ROWS, S)
    M = km.shape[0]

    def body(q_ref, z_ref, km_ref, vm_ref, p_ref, o_ref):
        _, o = _mem_heads(q_ref[...], km_ref, vm_ref)
        y = o * _silu(z_ref[...])
        o_ref[...] = _dot(y.astype(BF), p_ref[...])

    return pl.pallas_call(
        body, name="mem_fwd", grid=(S // tr,),
        in_specs=[_rows(tr, 512, 4), _rows(tr, 512, 5), _full((M, 512)), _full((M, 512)), _full((512, 1024))],
        out_specs=_rows(tr, 1024), out_shape=jax.ShapeDtypeStruct((S, D_MODEL), F32),
        compiler_params=_cparams(("parallel",)),
    )(proj, proj, km, vm, p_mem)


def _mem_bwd(do, proj, km, vm, p_mem):
    S = proj.shape[0]
    tr = min(T_ROWS_BWD, S)
    M = km.shape[0]

    def body(do_ref, q_ref, z_ref, km_ref, vm_ref, p_ref, dq_ref, dz_ref, dkm_ref, dvm_ref, dp_ref):
        @pl.when(pl.program_id(0) == 0)
        def _():
            dkm_ref[...] = jnp.zeros_like(dkm_ref)
            dvm_ref[...] = jnp.zeros_like(dvm_ref)
            dp_ref[...] = jnp.zeros_like(dp_ref)

        qm = q_ref[...]
        z = z_ref[...]
        ps, o = _mem_heads(qm, km_ref, vm_ref)
        sz = _silu(z)
        y = o * sz
        dob = do_ref[...].astype(BF)
        dy = _dot_nt(dob, p_ref[...])
        dp_ref[...] += _dot_tn(y.astype(BF), dob)
        dz_ref[...] = (dy * o * _dsilu(z)).astype(dz_ref.dtype)
        d_o = dy * sz
        for h in range(X_HEADS):
            cs = slice(h * X_HEAD_DIM, (h + 1) * X_HEAD_DIM)
            doh = d_o[:, cs]
            dohb = doh.astype(BF)
            p = ps[h]
            dpr = _dot_nt(dohb, vm_ref[:, cs])
            ds = (p * (dpr - jnp.sum(doh * o[:, cs], axis=1, keepdims=True)) * MEM_SCALE).astype(BF)
            dq_ref[:, cs] = _dot(ds, km_ref[:, cs]).astype(dq_ref.dtype)
            dkm_ref[:, cs] += _dot_tn(ds, qm[:, cs].astype(BF))
            dvm_ref[:, cs] += _dot_tn(p.astype(BF), dohb)

    return pl.pallas_call(
        body, name="mem_bwd", grid=(S // tr,),
        in_specs=[_rows(tr, 1024), _rows(tr, 512, 4), _rows(tr, 512, 5), _full((M, 512)), _full((M, 512)),
                  _full((512, 1024))],
        out_specs=(_rows(tr, 512), _rows(tr, 512), _full((M, 512)), _full((M, 512)), _full((512, 1024))),
        out_shape=(jax.ShapeDtypeStruct((S, 512), BF), jax.ShapeDtypeStruct((S, 512), BF),
                   jax.ShapeDtypeStruct((M, 512), F32), jax.ShapeDtypeStruct((M, 512), F32),
                   jax.ShapeDtypeStruct((512, 1024), F32)),
        compiler_params=_cparams(("arbitrary",)),
    )(do, proj, proj, km, vm, p_mem)


def _merge_fwd(x, proj, o_ssm, o_att, o_mem, b_gate, p_mla, w_out, ln_g, ln_b):
    S = x.shape[0]
    tr = min(T_ROWS, S)

    def body(x_ref, lg_ref, z_ref, os_ref, oa_ref, om_ref, bg_ref, p_ref, w_ref, g_ref, b_ref,
             xn_ref, pre_ref, mg_ref):
        gates = _sigmoid(lg_ref[...] + bg_ref[...])
        ya = oa_ref[...] * _silu(z_ref[...])
        o_mla = _dot(ya.astype(BF), p_ref[...])
        merged = (gates[:, :D_MODEL] * os_ref[...] + gates[:, D_MODEL:2 * D_MODEL] * o_mla
                  + gates[:, 2 * D_MODEL:] * om_ref[...])
        mb = merged.astype(BF)
        mg_ref[...] = mb
        pre = ALPHA * x_ref[...] + _dot(mb, w_ref[...])
        pre_ref[...] = pre
        mu = jnp.mean(pre, axis=-1, keepdims=True)
        xc = pre - mu
        var = jnp.mean(xc * xc, axis=-1, keepdims=True)
        xn_ref[...] = xc * lax.rsqrt(var + NORM_EPS) * g_ref[...] + b_ref[...]

    return pl.pallas_call(
        body, name="merge_fwd", grid=(S // tr,),
        in_specs=[_rows(tr, 1024), _rows(tr, 3072, 1), _rows(tr, 512, 3), _rows(tr, 1024), _rows(tr, 512),
                  _rows(tr, 1024), _full((1, 3072)), _full((512, 1024)), _full((1024, 1024)), _full((1, 1024)),
                  _full((1, 1024))],
        out_specs=(_rows(tr, 1024), _rows(tr, 1024), _rows(tr, 1024)),
        out_shape=(jax.ShapeDtypeStruct((S, 1024), F32), jax.ShapeDtypeStruct((S, 1024), F32),
                   jax.ShapeDtypeStruct((S, 1024), BF)),
        compiler_params=_cparams(("parallel",)),
    )(x, proj, proj, o_ssm, o_att, o_mem, b_gate, p_mla, w_out, ln_g, ln_b)


def _merge_bwd(dxn, pre, merged, proj, o_ssm, o_att, o_mem, b_gate, p_mla, w_out, ln_g):
    S = pre.shape[0]
    tr = min(T_ROWS_BWD, S)

    def body(dxn_ref, pre_ref, mg_ref, lg_ref, z_ref, os_ref, oa_ref, om_ref, bg_ref, p_ref, w_ref, g_ref,
             dxr_ref, dlg_ref, dos_ref, dom_ref, doa_ref, dz_ref, dl_ref, dw_ref, dp_ref, dbg_ref, dg_ref, db_ref):
        @pl.when(pl.program_id(0) == 0)
        def _():
            dw_ref[...] = jnp.zeros_like(dw_ref)
            dp_ref[...] = jnp.zeros_like(dp_ref)
            dbg_ref[...] = jnp.zeros_like(dbg_ref)
            dg_ref[...] = jnp.zeros_like(dg_ref)
            db_ref[...] = jnp.zeros_like(db_ref)

        dxn = dxn_ref[...]
        pre = pre_ref[...]
        mu = jnp.mean(pre, axis=-1, keepdims=True)
        xc = pre - mu
        rstd = lax.rsqrt(jnp.mean(xc * xc, axis=-1, keepdims=True) + NORM_EPS)
        xhat = xc * rstd
        dg_ref[...] += jnp.sum(dxn * xhat, axis=0, keepdims=True)
        db_ref[...] += jnp.sum(dxn, axis=0, keepdims=True)
        dxh = dxn * g_ref[...]
        dpre = rstd * (dxh - jnp.mean(dxh, axis=-1, keepdims=True)
                       - xhat * jnp.mean(dxh * xhat, axis=-1, keepdims=True))
        dxr_ref[...] = ALPHA * dpre
        dpb = dpre.astype(BF)
        dw_ref[...] += _dot_tn(mg_ref[...], dpb)
        dm = _dot_nt(dpb, w_ref[...])

        gates = _sigmoid(lg_ref[...] + bg_ref[...])
        g0, g1, g2 = gates[:, :D_MODEL], gates[:, D_MODEL:2 * D_MODEL], gates[:, 2 * D_MODEL:]
        z = z_ref[...]
        oa = oa_ref[...]
        sz = _silu(z)
        ya = (oa * sz).astype(BF)
        o_mla = _dot(ya, p_ref[...])
        dos_ref[...] = (g0 * dm).astype(dos_ref.dtype)
        dom_ref[...] = (g2 * dm).astype(dom_ref.dtype)
        do_mla = (g1 * dm).astype(BF)
        dl0 = dm * os_ref[...] * g0 * (1.0 - g0)
        dl1 = dm * o_mla * g1 * (1.0 - g1)
        dl2 = dm * om_ref[...] * g2 * (1.0 - g2)
        dl = jnp.concatenate([dl0, dl1, dl2], axis=1)
        dbg_ref[...] += jnp.sum(dl, axis=0, keepdims=True)
        dlg_ref[...] = dl.astype(dlg_ref.dtype)
        dp_ref[...] += _dot_tn(ya, do_mla)
        dya = _dot_nt(do_mla, p_ref[...])
        doa = dya * sz
        doa_ref[...] = doa.astype(doa_ref.dtype)
        dz_ref[...] = (dya * oa * _dsilu(z)).astype(dz_ref.dtype)
        prod = doa * oa
        lo = lax.broadcasted_iota(jnp.int32, (tr, LANES), 1) < MLA_V
        for pr in range(MLA_HEADS // 2):
            blk = prod[:, pr * LANES:(pr + 1) * LANES]
            d0 = jnp.sum(jnp.where(lo, blk, 0.0), axis=1, keepdims=True)
            d1 = jnp.sum(jnp.where(lo, 0.0, blk), axis=1, keepdims=True)
            dl_ref[pr] = jnp.where(lo, d0, d1)

    return pl.pallas_call(
        body, name="merge_bwd", grid=(S // tr,),
        in_specs=[_rows(tr, 1024), _rows(tr, 1024), _rows(tr, 1024), _rows(tr, 3072, 1), _rows(tr, 512, 3),
                  _rows(tr, 1024), _rows(tr, 512), _rows(tr, 1024), _full((1, 3072)), _full((512, 1024)),
                  _full((1024, 1024)), _full((1, 1024))],
        out_specs=(_rows(tr, 1024), _rows(tr, 3072), _rows(tr, 1024), _rows(tr, 1024), _rows(tr, 512),
                   _rows(tr, 512), pl.BlockSpec((MLA_HEADS // 2, tr, 128), lambda i: (0, i, 0)),
                   _full((1024, 1024)), _full((512, 1024)), _full((1, 3072)), _full((1, 1024)), _full((1, 1024))),
        out_shape=(jax.ShapeDtypeStruct((S, 1024), F32), jax.ShapeDtypeStruct((S, 3072), BF),
                   jax.ShapeDtypeStruct((S, 1024), BF), jax.ShapeDtypeStruct((S, 1024), BF),
                   jax.ShapeDtypeStruct((S, 512), BF), jax.ShapeDtypeStruct((S, 512), BF),
                   jax.ShapeDtypeStruct((MLA_HEADS // 2, S, 128), F32),
                   jax.ShapeDtypeStruct((1024, 1024), F32), jax.ShapeDtypeStruct((512, 1024), F32),
                   jax.ShapeDtypeStruct((1, 3072), F32), jax.ShapeDtypeStruct((1, 1024), F32),
                   jax.ShapeDtypeStruct((1, 1024), F32)),
        compiler_params=_cparams(("arbitrary",)),
    )(dxn, pre, merged, proj, proj, o_ssm, o_att, o_mem, b_gate, p_mla, w_out, ln_g)


def _loss_head(y, t):
    S = y.shape[0]
    tr = min(T_ROWS, S)
    n = S // tr

    def body(y_ref, t_ref, dy_ref, l_ref, acc):
        i = pl.program_id(0)

        @pl.when(i == 0)
        def _():
            acc[...] = jnp.zeros_like(acc)

        e = y_ref[...] - t_ref[...]
        dy_ref[...] = e * (1.0 / D_MODEL)
        acc[...] += jnp.sum(e * e, axis=0, keepdims=True)

        @pl.when(i == n - 1)
        def _():
            tot = jnp.sum(acc[...], axis=1, keepdims=True) * (0.5 / D_MODEL)
            l_ref[...] = jnp.broadcast_to(tot, l_ref.shape)

    return pl.pallas_call(
        body, name="loss_head", grid=(n,),
        in_specs=[_rows(tr, 1024), _rows(tr, 1024)],
        out_specs=(_rows(tr, 1024), _full((SUBLANES, LANES))),
        out_shape=(jax.ShapeDtypeStruct((S, 1024), F32), jax.ShapeDtypeStruct((SUBLANES, LANES), F32)),
        scratch_shapes=[pltpu.VMEM((1, 1024), F32)],
        compiler_params=_cparams(("arbitrary",)),
    )(y, t)


def _rope_tables(pos):
    inv_freq = ROPE_THETA ** (-jnp.arange(0, MLA_ROPE, 2, dtype=F32) / MLA_ROPE)
    ang = pos.astype(F32)[:, None] * inv_freq
    cos, sin = jnp.cos(ang), jnp.sin(ang)
    S = pos.shape[0]
    half = MLA_ROPE // 2
    ones = jnp.ones((S, MLA_NOPE), F32)
    z16 = jnp.zeros((S, half), F32)
    z32 = jnp.zeros((S, LANES - MLA_NOPE - MLA_ROPE), F32)
    z64 = jnp.zeros((S, MLA_NOPE), F32)
    c = jnp.concatenate([ones, cos, cos, z32], axis=1)
    sa = jnp.concatenate([z64, -sin, z16, z32], axis=1)
    sb = jnp.concatenate([z64, z16, sin, z32], axis=1)
    return c, sa, sb


def _ssm_discretise(a_re, a_im, log_dt, b_re, b_im):
    dt = jnp.exp(log_dt)[:, None]
    mag = jnp.exp(a_re * dt)
    lb_re = mag * jnp.cos(a_im * dt)
    lb_im = mag * jnp.sin(a_im * dt)
    nr, ni = lb_re - 1.0, lb_im
    den = a_re * a_re + a_im * a_im
    f_re = (nr * a_re + ni * a_im) / den
    f_im = (ni * a_re - nr * a_im) / den
    bb_re = f_re[..., None] * b_re - f_im[..., None] * b_im
    bb_im = f_re[..., None] * b_im + f_im[..., None] * b_re
    return lb_re, lb_im, bb_re, bb_im


_GPB = SSM_CB // SSM_GROUP


def _bd_in(bb):
    nb = SSM_GROUPS // _GPB
    t = bb.reshape(nb, _GPB, SSM_STATE, SSM_GROUP)
    eye = jnp.eye(_GPB, dtype=bb.dtype)
    return jnp.einsum("ngpc,gh->ngchp", t, eye).reshape(nb, SSM_CB, _GPB * SSM_STATE)


def _bd_in_t(d):
    nb = SSM_GROUPS // _GPB
    t = d.reshape(nb, _GPB, SSM_GROUP, _GPB, SSM_STATE)
    eye = jnp.eye(_GPB, dtype=d.dtype)
    return jnp.einsum("ngchp,gh->ngpc", t, eye).reshape(SSM_GROUPS, SSM_STATE, SSM_GROUP)


def _bd_out(c):
    nb = SSM_GROUPS // _GPB
    t = c.reshape(nb, _GPB, SSM_GROUP, SSM_STATE)
    eye = jnp.eye(_GPB, dtype=c.dtype)
    return jnp.einsum("ngcp,gh->ngphc", t, eye).reshape(nb, _GPB * SSM_STATE, SSM_CB)


def _bd_out_t(d):
    nb = SSM_GROUPS // _GPB
    t = d.reshape(nb, _GPB, SSM_STATE, _GPB, SSM_GROUP)
    eye = jnp.eye(_GPB, dtype=d.dtype)
    return jnp.einsum("ngphc,gh->ngcp", t, eye).reshape(SSM_GROUPS, SSM_GROUP, SSM_STATE)


def _interleave(a):
    S, w = a.shape
    return a.reshape(SUBLANES, S // SUBLANES, w).transpose(1, 0, 2).reshape(S, w)


def _deinterleave(a):
    S, w = a.shape
    return a.reshape(S // SUBLANES, SUBLANES, w).transpose(1, 0, 2).reshape(S, w)


def _pad_w_in(w):
    z = lambda n: jnp.zeros(w.shape[:-1] + (n,), w.dtype)
    return jnp.concatenate([w[..., :ROPE_SLOT_LO], z(MLA_NOPE), w[..., ROPE_SLOT_LO:ROPE_SLOT_LO + MLA_ROPE],
                            z(LANES - MLA_NOPE - MLA_ROPE), w[..., ROPE_SLOT_LO + MLA_ROPE:]], axis=-1)


def _unpad_w_in(w):
    lo = ROPE_SLOT_LO + MLA_NOPE
    return jnp.concatenate([w[..., :ROPE_SLOT_LO], w[..., lo:lo + MLA_ROPE], w[..., ROPE_SLOT_LO + LANES:]], axis=-1)


def _pad_w_uq(w):
    t = w.reshape(w.shape[:-1] + (MLA_HEADS, MLA_NOPE + MLA_ROPE))
    t = jnp.concatenate([t, jnp.zeros(t.shape[:-1] + (LANES - MLA_NOPE - MLA_ROPE,), w.dtype)], axis=-1)
    return t.reshape(w.shape[:-1] + (MLA_HEADS * LANES,))


def _unpad_w_uq(w):
    t = w.reshape(w.shape[:-1] + (MLA_HEADS, LANES))[..., :MLA_NOPE + MLA_ROPE]
    return t.reshape(w.shape[:-1] + (MLA_HEADS * (MLA_NOPE + MLA_ROPE),))


def _split_w_ukv(w):
    t = w.reshape(w.shape[:-1] + (MLA_HEADS, MLA_NOPE + MLA_V))
    k = jnp.concatenate([t[..., :MLA_NOPE], jnp.zeros(t.shape[:-1] + (LANES - MLA_NOPE,), w.dtype)], axis=-1)
    return k.reshape(w.shape[:-1] + (MLA_HEADS * LANES,)), t[..., MLA_NOPE:].reshape(w.shape[:-1] + (MLA_HEADS * MLA_V,))


def _join_w_ukv(dk, dv):
    tk = dk.reshape(dk.shape[:-1] + (MLA_HEADS, LANES))[..., :MLA_NOPE]
    tv = dv.reshape(dv.shape[:-1] + (MLA_HEADS, MLA_V))
    return jnp.concatenate([tk, tv], axis=-1).reshape(dk.shape[:-1] + (MLA_HEADS * (MLA_NOPE + MLA_V),))


BIG = ("w_in", "w_glu", "w_uq", "w_ukv", "w_mem_kv", "p_ssm", "p_mla", "p_mem", "w_out")
SMALL = ("b_gate", "ssm_a_re", "ssm_a_im", "ssm_log_dt", "ssm_b_re", "ssm_b_im", "ssm_c_re", "ssm_c_im", "ssm_d",
         "b_glu", "mla_q_norm", "mla_kv_norm", "ln_g", "ln_b")


def _local_step(x, mem, pos, target, wb, ws):
    S = x.shape[0]
    tc, tsa, tsb = _rope_tables(pos)
    w_in_p = _pad_w_in(wb["w_in"])
    w_uq_p = _pad_w_uq(wb["w_uq"])
    w_k, w_v = _split_w_ukv(wb["w_ukv"])

    def disc(l):
        return _ssm_discretise(ws["ssm_a_re"][l], ws["ssm_a_im"][l], ws["ssm_log_dt"][l], ws["ssm_b_re"][l],
                               ws["ssm_b_im"][l])

    def ssm_mats(l):
        lb_re, lb_im, bb_re, bb_im = disc(l)
        nb = SSM_GROUPS // _GPB
        return (_bd_in(bb_re), _bd_in(bb_im), lb_re.reshape(nb, 1, -1), lb_im.reshape(nb, 1, -1),
                _bd_out(ws["ssm_c_re"][l]), _bd_out(ws["ssm_c_im"][l]), ws["ssm_d"][l].reshape(1, -1))

    saved = []
    xs = x
    for l in range(DEPTH):
        proj = _mm(xs, w_in_p[l], name="proj_fwd")
        mats = ssm_mats(l)
        u_il = _interleave(proj[:, :SSM_WIDTH])
        y_raw = _deinterleave(_ssm_fwd(u_il, *mats))
        b_glu = ws["b_glu"][l].reshape(1, -1)
        o_ssm = _ssm_post_fwd(y_raw, proj, wb["w_glu"][l], b_glu, wb["p_ssm"][l])
        qn = ws["mla_q_norm"][l].reshape(1, -1)
        kn = ws["mla_kv_norm"][l].reshape(1, -1)
        q, k, v = _mla_pre_fwd(proj, qn, kn, w_uq_p[l], w_k[l], w_v[l], tc, tsa, tsb)
        o_att, lse = _flash_fwd(q, k, v)
        kvm = _mm(mem, wb["w_mem_kv"][l], name="memkv_fwd", out_dtype=BF)
        km, vm = kvm[:, :512], kvm[:, 512:]
        o_mem = _mem_fwd(proj, km, vm, wb["p_mem"][l])
        b_gate = ws["b_gate"][l].reshape(1, -1)
        ln_g = ws["ln_g"][l].reshape(1, -1)
        ln_b = ws["ln_b"][l].reshape(1, -1)
        xn, pre, merged = _merge_fwd(xs, proj, o_ssm, o_att, o_mem, b_gate, wb["p_mla"][l], wb["w_out"][l], ln_g, ln_b)
        saved.append(dict(x=xs, proj=proj, u_il=u_il, y_raw=y_raw, o_ssm=o_ssm, q=q, k=k, v=v, o_att=o_att, lse=lse,
                          km=km, vm=vm, o_mem=o_mem, pre=pre, merged=merged))
        xs = xn

    dxs, lvec = _loss_head(xs, target)
    loss = lvec[0, 0]

    gb = {n: [None] * DEPTH for n in BIG}
    gs = {n: [None] * DEPTH for n in SMALL}
    for l in reversed(range(DEPTH)):
        sv = saved[l]
        proj = sv["proj"]
        b_gate = ws["b_gate"][l].reshape(1, -1)
        ln_g = ws["ln_g"][l].reshape(1, -1)
        (dx_res, dlg, do_ssm, do_mem, do_att, dz_mla, delta, d_w_out, d_p_mla, d_b_gate, d_ln_g, d_ln_b) = _merge_bwd(
            dxs, sv["pre"], sv["merged"], proj, sv["o_ssm"], sv["o_att"], sv["o_mem"], b_gate, wb["p_mla"][l],
            wb["w_out"][l], ln_g)
        dq_mem, dz_mem, d_km, d_vm, d_p_mem = _mem_bwd(do_mem, proj, sv["km"], sv["vm"], wb["p_mem"][l])
        d_w_mem = _mm(mem, jnp.concatenate([d_km, d_vm], axis=1), name="memkv_bwd", ta=True)
        dq = _flash_bwd_dq(sv["q"], sv["k"], sv["v"], do_att, sv["lse"], delta)
        dk, dv = _flash_bwd_dkv(sv["q"], sv["k"], sv["v"], do_att, sv["lse"], delta)
        qn = ws["mla_q_norm"][l].reshape(1, -1)
        kn = ws["mla_kv_norm"][l].reshape(1, -1)
        dcq, dckv, dslot, d_wuq_p, d_wk, d_wv, d_qn, d_kn = _mla_pre_bwd(
            dq, dk, dv, proj, qn, kn, w_uq_p[l], w_k[l], w_v[l], tc, tsa, tsb)
        b_glu = ws["b_glu"][l].reshape(1, -1)
        dy_raw, dz_ssm, d_w_glu, d_b_glu, d_p_ssm = _ssm_post_bwd(do_ssm, sv["y_raw"], proj, wb["w_glu"][l], b_glu,
                                                                 wb["p_ssm"][l])
        mats = ssm_mats(l)
        du_il, dbbre, dbbim, dare, daim, dcre, dcim, dd = _ssm_bwd(sv["u_il"], _interleave(dy_raw), *mats)
        du = _deinterleave(du_il).astype(BF)
        _, disc_vjp = jax.vjp(_ssm_discretise, ws["ssm_a_re"][l], ws["ssm_a_im"][l], ws["ssm_log_dt"][l],
                              ws["ssm_b_re"][l], ws["ssm_b_im"][l])
        d_a_re, d_a_im, d_log_dt, d_b_re, d_b_im = disc_vjp(
            (dare.reshape(SSM_GROUPS, SSM_STATE), daim.reshape(SSM_GROUPS, SSM_STATE), _bd_in_t(dbbre), _bd_in_t(dbbim)))
        dproj = jnp.concatenate([du, dz_ssm, dcq, dckv, dslot, dz_mla, dq_mem, dz_mem, dlg], axis=1)
        d_w_in_p = _mm(sv["x"], dproj, name="proj_dw", ta=True)
        dxs = _mm(dproj, w_in_p[l], name="proj_dx", tb=True, add=dx_res)

        gb["w_in"][l] = _unpad_w_in(d_w_in_p)
        gb["w_glu"][l] = d_w_glu
        gb["w_uq"][l] = _unpad_w_uq(d_wuq_p)
        gb["w_ukv"][l] = _join_w_ukv(d_wk, d_wv)
        gb["w_mem_kv"][l] = d_w_mem
        gb["p_ssm"][l] = d_p_ssm
        gb["p_mla"][l] = d_p_mla
        gb["p_mem"][l] = d_p_mem
        gb["w_out"][l] = d_w_out
        gs["b_gate"][l] = d_b_gate.reshape(-1)
        gs["ssm_a_re"][l] = d_a_re
        gs["ssm_a_im"][l] = d_a_im
        gs["ssm_log_dt"][l] = d_log_dt
        gs["ssm_b_re"][l] = d_b_re
        gs["ssm_b_im"][l] = d_b_im
        gs["ssm_c_re"][l] = _bd_out_t(dcre)
        gs["ssm_c_im"][l] = _bd_out_t(dcim)
        gs["ssm_d"][l] = dd.reshape(-1)
        gs["b_glu"][l] = d_b_glu.reshape(-1)
        gs["mla_q_norm"][l] = d_qn.reshape(-1)
        gs["mla_kv_norm"][l] = d_kn.reshape(-1)
        gs["ln_g"][l] = d_ln_g.reshape(-1)
        gs["ln_b"][l] = d_ln_b.reshape(-1)

    gb = {n: jnp.stack(v) for n, v in gb.items()}
    gs = {n: jnp.stack(v) for n, v in gs.items()}
    return loss, dxs, gb, gs


ANY = pl.BlockSpec(memory_space=pl.ANY)


def _all_gather(block, name):
    R, C = block.shape

    def body(x_ref, out_ref, send_sems, recv_sems, local_sem):
        x, y, c = lax.axis_index("x"), lax.axis_index("y"), lax.axis_index("c")
        me, sibling = (x, y, c), (x, y, 1 - c)
        chips = [(1 - x, y), (x, 1 - y), (1 - x, 1 - y)]

        def slot(px, py, pc):
            return out_ref.at[4 * px + 2 * py + pc]

        def copy(k, blk, to, src=None):
            return pltpu.make_async_remote_copy(
                src_ref=slot(*blk) if src is None else src, dst_ref=slot(*blk),
                send_sem=send_sems.at[k], recv_sem=recv_sems.at[k], device_id=to, device_id_type=MESH)

        mine = pltpu.make_async_copy(x_ref, slot(*me), local_sem)
        mine.start()
        first = [copy(0, me, sibling, src=x_ref)]
        first += [copy(1 + j, me, (*chip, c), src=x_ref) for j, chip in enumerate(chips)]
        for cp in first:
            cp.start()
        passed = [copy(4 + j, (*chip, c), sibling) for j, chip in enumerate(chips)]
        for j, chip in enumerate(chips):
            copy(1 + j, (*chip, c), me).wait_recv()
            passed[j].start()
        copy(0, sibling, me).wait_recv()
        for j, chip in enumerate(chips):
            copy(4 + j, (*chip, 1 - c), me).wait_recv()
        for cp in first + passed:
            cp.wait_send()
        mine.wait()

    return pl.pallas_call(
        body, name=name, out_shape=jax.ShapeDtypeStruct((N_DEV, R, C), block.dtype),
        in_specs=[ANY], out_specs=ANY,
        scratch_shapes=[pltpu.SemaphoreType.DMA((7,)), pltpu.SemaphoreType.DMA((7,)), pltpu.SemaphoreType.DMA],
    )(block)


def _all_to_all(parts, name):
    _, R, C = parts.shape

    def body(g_ref, out_ref, send_sems, recv_sems, local_sem):
        x, y, c = lax.axis_index("x"), lax.axis_index("y"), lax.axis_index("c")
        me = 4 * x + 2 * y + c
        mine = pltpu.make_async_copy(g_ref.at[me], out_ref.at[me], local_sem)
        mine.start()
        copies = []
        for rel in range(1, N_DEV):
            px = 1 - x if rel & 4 else x
            py = 1 - y if rel & 2 else y
            pc = 1 - c if rel & 1 else c
            peer = 4 * px + 2 * py + pc
            copies.append(pltpu.make_async_remote_copy(
                src_ref=g_ref.at[peer], dst_ref=out_ref.at[me], send_sem=send_sems.at[rel - 1],
                recv_sem=recv_sems.at[rel - 1], device_id=(px, py, pc), device_id_type=MESH))
        for cp in copies:
            cp.start()
        for cp in copies:
            cp.wait()
        mine.wait()

    return pl.pallas_call(
        body, name=name, out_shape=jax.ShapeDtypeStruct(parts.shape, parts.dtype),
        in_specs=[ANY], out_specs=ANY,
        scratch_shapes=[pltpu.SemaphoreType.DMA((7,)), pltpu.SemaphoreType.DMA((7,)), pltpu.SemaphoreType.DMA],
    )(parts)


def _sum_parts(parts, tile):
    _, R, C = parts.shape
    assert R % tile == 0

    def body(p_ref, o_ref):
        s = p_ref[0].astype(F32)
        for k in range(1, N_DEV):
            s = s + p_ref[k].astype(F32)
        o_ref[...] = s

    return pl.pallas_call(
        body, name="grad_sum", grid=(R // tile,),
        in_specs=[pl.BlockSpec((N_DEV, tile, C), lambda i: (0, i, 0))], out_specs=_rows(tile, C),
        out_shape=jax.ShapeDtypeStruct((R, C), F32), compiler_params=_cparams(("parallel",)),
    )(parts)


def _adamw_math(w, g, m, v):
    m = ADAM_B1 * m + (1.0 - ADAM_B1) * g
    v = ADAM_B2 * v + (1.0 - ADAM_B2) * (g * g)
    m_hat = m / (1.0 - ADAM_B1 ** ADAM_STEP)
    v_hat = v / (1.0 - ADAM_B2 ** ADAM_STEP)
    delta = -ADAM_LR * (m_hat / (jnp.sqrt(v_hat) + ADAM_EPS) + ADAM_WD * w)
    return delta, m, v


def _adamw(w, g, m, v, name):
    R, C = w.shape
    tile = min(512, R)
    assert R % tile == 0

    def body(w_ref, g_ref, m_ref, v_ref, d_out, m_out, v_out):
        d, mn, vn = _adamw_math(w_ref[...], g_ref[...], m_ref[...], v_ref[...])
        d_out[...] = d
        m_out[...] = mn
        v_out[...] = vn

    spec = _rows(tile, C)
    shp = jax.ShapeDtypeStruct((R, C), F32)
    return pl.pallas_call(
        body, name=name, grid=(R // tile,), in_specs=[spec] * 4, out_specs=(spec,) * 3, out_shape=(shp,) * 3,
        compiler_params=_cparams(("parallel",)),
    )(w, g, m, v)


def _adamw_gathered(parts, w, m, v, tile):
    _, R, C = parts.shape
    assert R % tile == 0

    def body(p_ref, w_ref, m_ref, v_ref, g_out, d_out, m_out, v_out):
        g = p_ref[0]
        for k in range(1, N_DEV):
            g = g + p_ref[k]
        d, mn, vn = _adamw_math(w_ref[...], g, m_ref[...], v_ref[...])
        g_out[...] = g
        d_out[...] = d
        m_out[...] = mn
        v_out[...] = vn

    spec = _rows(tile, C)
    shp = jax.ShapeDtypeStruct((R, C), F32)
    return pl.pallas_call(
        body, name="adamw_replicated", grid=(R // tile,),
        in_specs=[pl.BlockSpec((N_DEV, tile, C), lambda i: (0, i, 0)), spec, spec, spec],
        out_specs=(spec,) * 4, out_shape=(shp,) * 4, compiler_params=_cparams(("parallel",)),
    )(parts, w, m, v)


ROW_SHARDED = ("w_mem_kv", "w_out")
SUM_TILE = 384
SMALL_TILE = 512


def _to_rows(a):
    return a.reshape(-1, LANES)


def _pack_small(d):
    parts = []
    for n in SMALL:
        f = d[n].reshape(-1)
        pad = (-f.shape[0]) % (SUBLANES * LANES)
        if pad:
            f = jnp.concatenate([f, jnp.zeros((pad,), f.dtype)])
        parts.append(f.reshape(-1, LANES))
    rows = sum(p.shape[0] for p in parts)
    pad = (-rows) % SMALL_TILE
    if pad:
        parts.append(jnp.zeros((pad, LANES), parts[0].dtype))
    return jnp.concatenate(parts, axis=0)


def _unpack_small(buf, like):
    out, off = {}, 0
    for n in SMALL:
        size = math.prod(like[n].shape)
        rows = -(-size // (SUBLANES * LANES)) * SUBLANES
        out[n] = buf[off:off + rows].reshape(-1)[:size].reshape(like[n].shape)
        off += rows
    return out


def _gather_weights(local):
    buf = jnp.concatenate([_to_rows(local[n].astype(BF)) for n in BIG], axis=0)
    allb = _all_gather(buf, "weights_all_gather")
    full, off = {}, 0
    for n in BIG:
        shp = local[n].shape
        rows = math.prod(shp) // LANES
        t = allb[:, off:off + rows].reshape((N_DEV,) + shp)
        off += rows
        if n in ROW_SHARDED:
            full[n] = t.transpose(1, 0, 2, 3).reshape(shp[0], N_DEV * shp[1], shp[2])
        else:
            full[n] = t.transpose(1, 2, 0, 3).reshape(shp[0], shp[1], N_DEV * shp[2])
    return full


def _scatter_grads(gfull, local):
    parts = []
    for n in BIG:
        shp = local[n].shape
        g = gfull[n].astype(BF)
        if n in ROW_SHARDED:
            t = g.reshape(shp[0], N_DEV, shp[1], shp[2]).transpose(1, 0, 2, 3)
        else:
            t = g.reshape(shp[0], shp[1], N_DEV, shp[2]).transpose(2, 0, 1, 3)
        parts.append(t.reshape(N_DEV, -1, LANES))
    buf = jnp.concatenate(parts, axis=1)
    got = _all_to_all(buf, "grads_all_to_all")
    summed = _sum_parts(got, SUM_TILE)
    out, off = {}, 0
    for n in BIG:
        shp = local[n].shape
        rows = math.prod(shp) // LANES
        out[n] = summed[off:off + rows].reshape(shp)
        off += rows
    return out


WEIGHTS = ("w_in", "b_gate", "ssm_a_re", "ssm_a_im", "ssm_log_dt", "ssm_b_re", "ssm_b_im", "ssm_c_re", "ssm_c_im",
           "ssm_d", "w_glu", "b_glu", "mla_q_norm", "w_uq", "mla_kv_norm", "w_ukv", "w_mem_kv", "p_ssm", "p_mla",
           "p_mem", "w_out", "ln_g", "ln_b")


def kernel(x, mem, positions, w_in, b_gate, ssm_a_re, ssm_a_im, ssm_log_dt, ssm_b_re, ssm_b_im, ssm_c_re, ssm_c_im, ssm_d, w_glu, b_glu, mla_q_norm, w_uq, mla_kv_norm, w_ukv, w_mem_kv, p_ssm, p_mla, p_mem, w_out, ln_g, ln_b, loss_target, m_w_in, m_b_gate, m_ssm_a_re, m_ssm_a_im, m_ssm_log_dt, m_ssm_b_re, m_ssm_b_im, m_ssm_c_re, m_ssm_c_im, m_ssm_d, m_w_glu, m_b_glu, m_mla_q_norm, m_w_uq, m_mla_kv_norm, m_w_ukv, m_w_mem_kv, m_p_ssm, m_p_mla, m_p_mem, m_w_out, m_ln_g, m_ln_b, v_w_in, v_b_gate, v_ssm_a_re, v_ssm_a_im, v_ssm_log_dt, v_ssm_b_re, v_ssm_b_im, v_ssm_c_re, v_ssm_c_im, v_ssm_d, v_w_glu, v_b_glu, v_mla_q_norm, v_w_uq, v_mla_kv_norm, v_w_ukv, v_w_mem_kv, v_p_ssm, v_p_mla, v_p_mem, v_w_out, v_ln_g, v_ln_b):
    w = dict(w_in=w_in, b_gate=b_gate, ssm_a_re=ssm_a_re, ssm_a_im=ssm_a_im, ssm_log_dt=ssm_log_dt, ssm_b_re=ssm_b_re,
             ssm_b_im=ssm_b_im, ssm_c_re=ssm_c_re, ssm_c_im=ssm_c_im, ssm_d=ssm_d, w_glu=w_glu, b_glu=b_glu,
             mla_q_norm=mla_q_norm, w_uq=w_uq, mla_kv_norm=mla_kv_norm, w_ukv=w_ukv, w_mem_kv=w_mem_kv, p_ssm=p_ssm,
             p_mla=p_mla, p_mem=p_mem, w_out=w_out, ln_g=ln_g, ln_b=ln_b)
    m = dict(w_in=m_w_in, b_gate=m_b_gate, ssm_a_re=m_ssm_a_re, ssm_a_im=m_ssm_a_im, ssm_log_dt=m_ssm_log_dt,
             ssm_b_re=m_ssm_b_re, ssm_b_im=m_ssm_b_im, ssm_c_re=m_ssm_c_re, ssm_c_im=m_ssm_c_im, ssm_d=m_ssm_d,
             w_glu=m_w_glu, b_glu=m_b_glu, mla_q_norm=m_mla_q_norm, w_uq=m_w_uq, mla_kv_norm=m_mla_kv_norm,
             w_ukv=m_w_ukv, w_mem_kv=m_w_mem_kv, p_ssm=m_p_ssm, p_mla=m_p_mla, p_mem=m_p_mem, w_out=m_w_out,
             ln_g=m_ln_g, ln_b=m_ln_b)
    v = dict(w_in=v_w_in, b_gate=v_b_gate, ssm_a_re=v_ssm_a_re, ssm_a_im=v_ssm_a_im, ssm_log_dt=v_ssm_log_dt,
             ssm_b_re=v_ssm_b_re, ssm_b_im=v_ssm_b_im, ssm_c_re=v_ssm_c_re, ssm_c_im=v_ssm_c_im, ssm_d=v_ssm_d,
             w_glu=v_w_glu, b_glu=v_b_glu, mla_q_norm=v_mla_q_norm, w_uq=v_w_uq, mla_kv_norm=v_mla_kv_norm,
             w_ukv=v_w_ukv, w_mem_kv=v_w_mem_kv, p_ssm=v_p_ssm, p_mla=v_p_mla, p_mem=v_p_mem, w_out=v_w_out,
             ln_g=v_ln_g, ln_b=v_ln_b)

    local_big = {n: w[n] for n in BIG}
    small = {n: w[n] for n in SMALL}
    full = _gather_weights(local_big)
    loss_local, dx, g_full, g_small = _local_step(x[0], mem[0], positions[0], loss_target[0], full, small)
    loss = lax.psum(loss_local, ("x", "y", "c"))

    grads, delta, new_m, new_v = {}, {}, {}, {}
    g_big = _scatter_grads(g_full, local_big)
    for n in BIG:
        shp = w[n].shape
        to2d = lambda a: a.reshape(-1, shp[-1])
        d2, m2, v2 = _adamw(to2d(w[n]), to2d(g_big[n]), to2d(m[n]), to2d(v[n]), "adamw_" + n)
        grads[n] = g_big[n]
        delta[n], new_m[n], new_v[n] = d2.reshape(shp), m2.reshape(shp), v2.reshape(shp)

    parts = _all_gather(_pack_small(g_small), "small_grads_all_gather")
    gs, ds, ms, vs = _adamw_gathered(parts, _pack_small(small), _pack_small({n: m[n] for n in SMALL}),
                                     _pack_small({n: v[n] for n in SMALL}), SMALL_TILE)
    for dst, buf in ((grads, gs), (delta, ds), (new_m, ms), (new_v, vs)):
        dst.update(_unpack_small(buf, small))

    return (loss, dx[None], *[grads[n] for n in WEIGHTS], *[delta[n] for n in WEIGHTS],
            *[new_m[n] for n in WEIGHTS], *[new_v[n] for n in WEIGHTS])
```

```python
import math

import jax
import jax.numpy as jnp
from jax import lax
from jax.experimental import pallas as pl
from jax.experimental.pallas import tpu as pltpu

F32 = jnp.float32
BF = jnp.bfloat16

D_MODEL = 1024
DEPTH = 4
N_DEV = 8
SSM_WIDTH = 512
SSM_GROUP = 16
SSM_GROUPS = 32
SSM_STATE = 64
MLA_HEADS = 8
MLA_NOPE = 64
MLA_ROPE = 32
MLA_V = 64
MLA_Q_RANK = 256
MLA_KV_RANK = 128
ROPE_THETA = 10000.0
X_HEADS = 4
X_HEAD_DIM = 128
D_IN = 6048
ALPHA = (2 * DEPTH) ** 0.25
NORM_EPS = 1e-5
ADAM_LR = 0.001
ADAM_B1 = 0.9
ADAM_B2 = 0.999
ADAM_EPS = 1e-08
ADAM_WD = 0.01
ADAM_STEP = 10

LANES = 128
SUBLANES = 8
VMEM_LIMIT = 56 * 1024 * 1024

PW = 6144
ROPE_SLOT_LO = 1408
MLA_SCALE = (MLA_NOPE + MLA_ROPE) ** -0.5
MEM_SCALE = X_HEAD_DIM ** -0.5
NEG = -1e30

T_ROWS = 512
T_ROWS_BWD = 256
T_ATT = 1024
T_MM = 512

MESH = pl.DeviceIdType.MESH


def _cparams(sem):
    return pltpu.CompilerParams(dimension_semantics=sem, vmem_limit_bytes=VMEM_LIMIT)


def _dot(a, b):
    return lax.dot_general(a, b, (((1,), (0,)), ((), ())), preferred_element_type=F32)


def _dot_nt(a, b):
    return lax.dot_general(a, b, (((1,), (1,)), ((), ())), preferred_element_type=F32)


def _dot_tn(a, b):
    return lax.dot_general(a, b, (((0,), (0,)), ((), ())), preferred_element_type=F32)


def _sigmoid(x):
    return 1.0 / (1.0 + jnp.exp(-x))


def _silu(x):
    return x * _sigmoid(x)


def _dsilu(x):
    s = _sigmoid(x)
    return s * (1.0 + x * (1.0 - s))


_GELU_C = math.sqrt(2.0 / math.pi)


def _gelu(x):
    return 0.5 * x * (1.0 + jnp.tanh(_GELU_C * (x + 0.044715 * x * x * x)))


def _dgelu(x):
    t = jnp.tanh(_GELU_C * (x + 0.044715 * x * x * x))
    return 0.5 * (1.0 + t) + 0.5 * x * (1.0 - t * t) * _GELU_C * (1.0 + 3 * 0.044715 * x * x)


def _rows(tr, w, col=0):
    return pl.BlockSpec((tr, w), lambda i: (i, col))


def _full(shape):
    n = len(shape)
    return pl.BlockSpec(shape, lambda i: (0,) * n)


def _mm(a, b, *, name, ta=False, tb=False, out_dtype=F32, add=None, tm=T_MM, tn=T_MM, tk=1024):
    M, K = (a.shape[1], a.shape[0]) if ta else a.shape
    N = b.shape[0] if tb else b.shape[1]
    tm, tn, tk = min(tm, M), min(tn, N), min(tk, K)
    assert M % tm == 0 and N % tn == 0 and K % tk == 0, (M, N, K)
    nk = K // tk
    dn = (((0 if ta else 1,), (1 if tb else 0,)), ((), ()))

    def body(*refs):
        if add is not None:
            a_ref, b_ref, c_ref, o_ref = refs[:4]
        else:
            a_ref, b_ref, o_ref = refs[:3]
        part = lax.dot_general(a_ref[...].astype(BF), b_ref[...].astype(BF), dn, preferred_element_type=F32)
        if nk == 1:
            if add is not None:
                part = part + c_ref[...]
            o_ref[...] = part.astype(out_dtype)
            return
        acc = refs[-1]
        k = pl.program_id(2)

        @pl.when(k == 0)
        def _():
            acc[...] = part

        @pl.when(k != 0)
        def _():
            acc[...] += part

        @pl.when(k == nk - 1)
        def _():
            r = acc[...]
            if add is not None:
                r = r + c_ref[...]
            o_ref[...] = r.astype(out_dtype)

    a_spec = pl.BlockSpec((tk, tm), lambda i, j, k: (k, i)) if ta else pl.BlockSpec((tm, tk), lambda i, j, k: (i, k))
    b_spec = pl.BlockSpec((tn, tk), lambda i, j, k: (j, k)) if tb else pl.BlockSpec((tk, tn), lambda i, j, k: (k, j))
    o_spec = pl.BlockSpec((tm, tn), lambda i, j, k: (i, j))
    in_specs = [a_spec, b_spec] + ([o_spec] if add is not None else [])
    args = (a, b) + ((add,) if add is not None else ())
    return pl.pallas_call(
        body, name=name, grid=(M // tm, N // tn, nk), in_specs=in_specs, out_specs=o_spec,
        out_shape=jax.ShapeDtypeStruct((M, N), out_dtype),
        scratch_shapes=[pltpu.VMEM((tm, tn), F32)] if nk > 1 else [],
        compiler_params=_cparams(("parallel", "parallel", "arbitrary")),
    )(*args)


def _cpow(ar, ai, n):
    rr, ri = None, None
    br, bi = ar, ai
    while n:
        if n & 1:
            if rr is None:
                rr, ri = br, bi
            else:
                rr, ri = rr * br - ri * bi, rr * bi + ri * br
        n >>= 1
        if n:
            br, bi = br * br - bi * bi, 2.0 * br * bi
    return rr, ri


def _seg_shift(v, k, reverse):
    sub = lax.broadcasted_iota(jnp.int32, v.shape, 0)
    if not reverse:
        return jnp.where(sub >= k, pltpu.roll(v, k, 0), 0.0)
    return jnp.where(sub < SUBLANES - k, pltpu.roll(v, SUBLANES - k, 0), 0.0)


def _ssm_scan(hre, him, ar, ai, seglen, reverse):
    w = hre.shape[1]
    zero = jnp.zeros((SUBLANES, w), F32)

    def rows(j):
        jj = (seglen - 1 - j) if reverse else j
        return pl.ds(pl.multiple_of(jj * SUBLANES, SUBLANES), SUBLANES)

    def local(j, c):
        hr, hi = c
        r = rows(j)
        nhr = ar * hr - ai * hi + hre[r, :]
        nhi = ar * hi + ai * hr + him[r, :]
        hre[r, :] = nhr
        him[r, :] = nhi
        return nhr, nhi

    er, ei = lax.fori_loop(0, seglen, local, (zero, zero))
    pr, pi_ = _cpow(ar, ai, seglen)
    for k in (1, 2, 4):
        sr, si = _seg_shift(er, k, reverse), _seg_shift(ei, k, reverse)
        er, ei = er + pr * sr - pi_ * si, ei + pr * si + pi_ * sr
        pr, pi_ = pr * pr - pi_ * pi_, 2.0 * pr * pi_
    cr, ci = _seg_shift(er, 1, reverse), _seg_shift(ei, 1, reverse)

    def carry_in(j, c):
        qr, qi = c
        nqr = qr * ar - qi * ai
        nqi = qr * ai + qi * ar
        r = rows(j)
        hre[r, :] = hre[r, :] + (nqr * cr - nqi * ci)
        him[r, :] = him[r, :] + (nqr * ci + nqi * cr)
        return nqr, nqi

    lax.fori_loop(0, seglen, carry_in, (jnp.ones((SUBLANES, w), F32), zero))


SSM_CB = 128
SSM_SB = 256


def _ssm_specs(S):
    u_spec = pl.BlockSpec((S, SSM_CB), lambda g, h: (0, g))
    bb_spec = pl.BlockSpec((1, SSM_CB, SSM_SB), lambda g, h: (g, 0, h))
    a_spec = pl.BlockSpec((1, 1, SSM_SB), lambda g, h: (g, 0, h))
    c_spec = pl.BlockSpec((1, SSM_SB, SSM_CB), lambda g, h: (g, h, 0))
    d_spec = pl.BlockSpec((1, SSM_CB), lambda g, h: (0, g))
    return u_spec, bb_spec, a_spec, c_spec, d_spec


def _ssm_fwd(u, bbre, bbim, are, aim, cre, cim, d):
    S = u.shape[0]
    seglen = S // SUBLANES
    ch = min(512, S)
    nch = S // ch

    def body(u_ref, bbre_ref, bbim_ref, are_ref, aim_ref, cre_ref, cim_ref, d_ref, y_ref, hre, him):
        hf = pl.program_id(1)
        wre = bbre_ref[0].astype(BF)
        wim = bbim_ref[0].astype(BF)

        def mk(c, _):
            r = pl.ds(pl.multiple_of(c * ch, ch), ch)
            ub = u_ref[r, :].astype(BF)
            hre[r, :] = _dot(ub, wre)
            him[r, :] = _dot(ub, wim)
            return 0

        lax.fori_loop(0, nch, mk, 0)
        ar = jnp.broadcast_to(are_ref[0], (SUBLANES, SSM_SB))
        ai = jnp.broadcast_to(aim_ref[0], (SUBLANES, SSM_SB))
        _ssm_scan(hre, him, ar, ai, seglen, False)
        cr = cre_ref[0].astype(BF)
        ci = cim_ref[0].astype(BF)

        def out(c, _):
            r = pl.ds(pl.multiple_of(c * ch, ch), ch)
            y = _dot(hre[r, :].astype(BF), cr) - _dot(him[r, :].astype(BF), ci)

            @pl.when(hf == 0)
            def _():
                y_ref[r, :] = y + d_ref[...] * u_ref[r, :]

            @pl.when(hf != 0)
            def _():
                y_ref[r, :] = y_ref[r, :] + y

            return 0

        lax.fori_loop(0, nch, out, 0)

    u_spec, bb_spec, a_spec, c_spec, d_spec = _ssm_specs(S)
    return pl.pallas_call(
        body, name="ssm_fwd", grid=(SSM_WIDTH // SSM_CB, 2),
        in_specs=[u_spec, bb_spec, bb_spec, a_spec, a_spec, c_spec, c_spec, d_spec], out_specs=u_spec,
        out_shape=jax.ShapeDtypeStruct((S, SSM_WIDTH), F32),
        scratch_shapes=[pltpu.VMEM((S, SSM_SB), F32), pltpu.VMEM((S, SSM_SB), F32)],
        compiler_params=_cparams(("parallel", "arbitrary")),
    )(u, bbre, bbim, are, aim, cre, cim, d)


def _ssm_bwd(u, dy, bbre, bbim, are, aim, cre, cim, d):
    S = u.shape[0]
    seglen = S // SUBLANES
    ch = min(512, S)
    nch = S // ch
    nblk = SSM_WIDTH // SSM_CB

    def body(u_ref, dy_ref, bbre_ref, bbim_ref, are_ref, aim_ref, cre_ref, cim_ref, d_ref,
             du_ref, dbbre_ref, dbbim_ref, dare_ref, daim_ref, dcre_ref, dcim_ref, dd_ref,
             hre, him, lre, lim):
        hf = pl.program_id(1)
        wre = bbre_ref[0].astype(BF)
        wim = bbim_ref[0].astype(BF)
        cr = cre_ref[0].astype(BF)
        ci = cim_ref[0].astype(BF)

        def mk(c, _):
            r = pl.ds(pl.multiple_of(c * ch, ch), ch)
            ub = u_ref[r, :].astype(BF)
            hre[r, :] = _dot(ub, wre)
            him[r, :] = _dot(ub, wim)
            return 0

        lax.fori_loop(0, nch, mk, 0)
        ar = jnp.broadcast_to(are_ref[0], (SUBLANES, SSM_SB))
        ai = jnp.broadcast_to(aim_ref[0], (SUBLANES, SSM_SB))
        _ssm_scan(hre, him, ar, ai, seglen, False)

        dcre_ref[...] = jnp.zeros_like(dcre_ref)
        dcim_ref[...] = jnp.zeros_like(dcim_ref)

        @pl.when(hf == 0)
        def _():
            dd_ref[...] = jnp.zeros_like(dd_ref)

        def cot(c, _):
            r = pl.ds(pl.multiple_of(c * ch, ch), ch)
            dyv = dy_ref[r, :]
            dyb = dyv.astype(BF)
            lre[r, :] = _dot_nt(dyb, cr)
            lim[r, :] = -_dot_nt(dyb, ci)
            dcre_ref[0] = dcre_ref[0] + _dot_tn(hre[r, :].astype(BF), dyb)
            dcim_ref[0] = dcim_ref[0] - _dot_tn(him[r, :].astype(BF), dyb)

            @pl.when(hf == 0)
            def _():
                dd_ref[...] = dd_ref[...] + jnp.sum(dyv * u_ref[r, :], axis=0, keepdims=True)

            return 0

        lax.fori_loop(0, nch, cot, 0)
        _ssm_scan(lre, lim, ar, -ai, seglen, True)

        last = pl.ds((seglen - 1) * SUBLANES, SUBLANES)
        first = pl.ds(0, SUBLANES)
        pr0 = _seg_shift(hre[last, :], 1, False)
        pi0 = _seg_shift(him[last, :], 1, False)
        acr0 = lre[first, :] * pr0 + lim[first, :] * pi0
        aci0 = lim[first, :] * pr0 - lre[first, :] * pi0

        def da(j, c):
            acr, aci = c
            r = pl.ds(pl.multiple_of(j * SUBLANES, SUBLANES), SUBLANES)
            rp = pl.ds(pl.multiple_of((j - 1) * SUBLANES, SUBLANES), SUBLANES)
            lr, li = lre[r, :], lim[r, :]
            pr, pi_ = hre[rp, :], him[rp, :]
            return acr + lr * pr + li * pi_, aci + li * pr - lr * pi_

        acr, aci = lax.fori_loop(1, seglen, da, (acr0, aci0))
        dare_ref[0] = jnp.sum(acr, axis=0, keepdims=True)
        daim_ref[0] = jnp.sum(aci, axis=0, keepdims=True)

        dbbre_ref[...] = jnp.zeros_like(dbbre_ref)
        dbbim_ref[...] = jnp.zeros_like(dbbim_ref)

        def fin(c, _):
            r = pl.ds(pl.multiple_of(c * ch, ch), ch)
            lrb = lre[r, :].astype(BF)
            lib = lim[r, :].astype(BF)
            ub = u_ref[r, :].astype(BF)
            du = _dot_nt(lrb, wre) + _dot_nt(lib, wim)
            dbbre_ref[0] = dbbre_ref[0] + _dot_tn(ub, lrb)
            dbbim_ref[0] = dbbim_ref[0] + _dot_tn(ub, lib)

            @pl.when(hf == 0)
            def _():
                du_ref[r, :] = du + d_ref[...] * dy_ref[r, :]

            @pl.when(hf != 0)
            def _():
                du_ref[r, :] = du_ref[r, :] + du

            return 0

        lax.fori_loop(0, nch, fin, 0)

    u_spec, bb_spec, a_spec, c_spec, d_spec = _ssm_specs(S)
    out_shape = (
        jax.ShapeDtypeStruct((S, SSM_WIDTH), F32),
        jax.ShapeDtypeStruct((nblk, SSM_CB, 2 * SSM_SB), F32), jax.ShapeDtypeStruct((nblk, SSM_CB, 2 * SSM_SB), F32),
        jax.ShapeDtypeStruct((nblk, 1, 2 * SSM_SB), F32), jax.ShapeDtypeStruct((nblk, 1, 2 * SSM_SB), F32),
        jax.ShapeDtypeStruct((nblk, 2 * SSM_SB, SSM_CB), F32), jax.ShapeDtypeStruct((nblk, 2 * SSM_SB, SSM_CB), F32),
        jax.ShapeDtypeStruct((1, SSM_WIDTH), F32),
    )
    return pl.pallas_call(
        body, name="ssm_bwd", grid=(nblk, 2),
        in_specs=[u_spec, u_spec, bb_spec, bb_spec, a_spec, a_spec, c_spec, c_spec, d_spec],
        out_specs=(u_spec, bb_spec, bb_spec, a_spec, a_spec, c_spec, c_spec, d_spec),
        out_shape=out_shape,
        scratch_shapes=[pltpu.VMEM((S, SSM_SB), F32) for _ in range(4)],
        compiler_params=_cparams(("parallel", "arbitrary")),
    )(u, dy, bbre, bbim, are, aim, cre, cim, d)


def _ssm_post_fwd(y_raw, proj, w_glu, b_glu, p_ssm):
    S = y_raw.shape[0]
    tr = min(T_ROWS, S)

    def body(y_ref, z_ref, wg_ref, bg_ref, p_ref, o_ref):
        g = _gelu(y_ref[...])
        t = _dot(g.astype(BF), wg_ref[...]) + bg_ref[...]
        glu = t[:, :SSM_WIDTH] * _sigmoid(t[:, SSM_WIDTH:])
        ys = glu * _silu(z_ref[...])
        o_ref[...] = _dot(ys.astype(BF), p_ref[...])

    return pl.pallas_call(
        body, name="ssm_post_fwd", grid=(S // tr,),
        in_specs=[_rows(tr, 512), _rows(tr, 512, 1), _full((512, 1024)), _full((1, 1024)), _full((512, 1024))],
        out_specs=_rows(tr, 1024), out_shape=jax.ShapeDtypeStruct((S, D_MODEL), F32),
        compiler_params=_cparams(("parallel",)),
    )(y_raw, proj, w_glu, b_glu, p_ssm)


def _ssm_post_bwd(do, y_raw, proj, w_glu, b_glu, p_ssm):
    S = y_raw.shape[0]
    tr = min(T_ROWS_BWD, S)

    def body(do_ref, y_ref, z_ref, wg_ref, bg_ref, p_ref, dy_ref, dz_ref, dwg_ref, dbg_ref, dp_ref):
        @pl.when(pl.program_id(0) == 0)
        def _():
            dwg_ref[...] = jnp.zeros_like(dwg_ref)
            dbg_ref[...] = jnp.zeros_like(dbg_ref)
            dp_ref[...] = jnp.zeros_like(dp_ref)

        y = y_ref[...]
        z = z_ref[...]
        g = _gelu(y)
        gb = g.astype(BF)
        t = _dot(gb, wg_ref[...]) + bg_ref[...]
        a = t[:, :SSM_WIDTH]
        sb = _sigmoid(t[:, SSM_WIDTH:])
        glu = a * sb
        ys = glu * _silu(z)
        dob = do_ref[...].astype(BF)
        dys = _dot_nt(dob, p_ref[...])
        dp_ref[...] += _dot_tn(ys.astype(BF), dob)
        dglu = dys * _silu(z)
        dz_ref[...] = (dys * glu * _dsilu(z)).astype(dz_ref.dtype)
        dt = jnp.concatenate([dglu * sb, dglu * a * sb * (1.0 - sb)], axis=1)
        dbg_ref[...] += jnp.sum(dt, axis=0, keepdims=True)
        dtb = dt.astype(BF)
        dg = _dot_nt(dtb, wg_ref[...])
        dwg_ref[...] += _dot_tn(gb, dtb)
        dy_ref[...] = dg * _dgelu(y)

    return pl.pallas_call(
        body, name="ssm_post_bwd", grid=(S // tr,),
        in_specs=[_rows(tr, 1024), _rows(tr, 512), _rows(tr, 512, 1), _full((512, 1024)), _full((1, 1024)),
                  _full((512, 1024))],
        out_specs=(_rows(tr, 512), _rows(tr, 512), _full((512, 1024)), _full((1, 1024)), _full((512, 1024))),
        out_shape=(jax.ShapeDtypeStruct((S, 512), F32), jax.ShapeDtypeStruct((S, 512), BF),
                   jax.ShapeDtypeStruct((512, 1024), F32), jax.ShapeDtypeStruct((1, 1024), F32),
                   jax.ShapeDtypeStruct((512, 1024), F32)),
        compiler_params=_cparams(("arbitrary",)),
    )(do, y_raw, proj, w_glu, b_glu, p_ssm)


def _rope(t, c, sa, sb):
    return t * c + pltpu.roll(t, LANES - 16, 1) * sa + pltpu.roll(t, 16, 1) * sb


def _rope_t(dy, c, sa, sb):
    return dy * c + pltpu.roll(dy * sa, 16, 1) + pltpu.roll(dy * sb, LANES - 16, 1)


def _rms(x, g):
    r = lax.rsqrt(jnp.mean(x * x, axis=-1, keepdims=True) + NORM_EPS)
    return x * r * g, r


def _mla_pre_fwd(proj, q_norm, kv_norm, wuq, wk, wv, tc, tsa, tsb):
    S = proj.shape[0]
    tr = min(T_ROWS, S)

    def body(cq_ref, ckv_ref, slot_ref, qn_ref, kn_ref, wuq_ref, wk_ref, wv_ref, c_ref, sa_ref, sb_ref,
             q_out, k_out, v_out):
        c, sa, sb = c_ref[...], sa_ref[...], sb_ref[...]
        qn, _ = _rms(cq_ref[...], qn_ref[...])
        q = _dot(qn.astype(BF), wuq_ref[...]) * MLA_SCALE
        kn, _ = _rms(ckv_ref[...], kn_ref[...])
        knb = kn.astype(BF)
        kp = _dot(knb, wk_ref[...])
        v_out[...] = _dot(knb, wv_ref[...]).astype(BF)
        kr = _rope(slot_ref[...], c, sa, sb)
        for h in range(MLA_HEADS):
            cs = slice(h * LANES, (h + 1) * LANES)
            q_out[:, cs] = _rope(q[:, cs], c, sa, sb).astype(BF)
            k_out[:, cs] = (kp[:, cs] + kr).astype(BF)

    return pl.pallas_call(
        body, name="mla_pre_fwd", grid=(S // tr,),
        in_specs=[_rows(tr, 256, 4), _rows(tr, 128, 10), _rows(tr, 128, 11), _full((1, 256)), _full((1, 128)),
                  _full((256, 1024)), _full((128, 1024)), _full((128, 512)),
                  _rows(tr, 128), _rows(tr, 128), _rows(tr, 128)],
        out_specs=(_rows(tr, 1024), _rows(tr, 1024), _rows(tr, 512)),
        out_shape=(jax.ShapeDtypeStruct((S, 1024), BF), jax.ShapeDtypeStruct((S, 1024), BF),
                   jax.ShapeDtypeStruct((S, 512), BF)),
        compiler_params=_cparams(("parallel",)),
    )(proj, proj, proj, q_norm, kv_norm, wuq, wk, wv, tc, tsa, tsb)


def _mla_pre_bwd(dq, dk, dv, proj, q_norm, kv_norm, wuq, wk, wv, tc, tsa, tsb):
    S = proj.shape[0]
    tr = min(T_ROWS_BWD, S)

    def body(dq_ref, dk_ref, dv_ref, cq_ref, ckv_ref, qn_ref, kn_ref, wuq_ref, wk_ref, wv_ref, c_ref, sa_ref, sb_ref,
             dcq_ref, dckv_ref, dslot_ref, dwuq_ref, dwk_ref, dwv_ref, dqn_ref, dkn_ref, dqp):
        @pl.when(pl.program_id(0) == 0)
        def _():
            dwuq_ref[...] = jnp.zeros_like(dwuq_ref)
            dwk_ref[...] = jnp.zeros_like(dwk_ref)
            dwv_ref[...] = jnp.zeros_like(dwv_ref)
            dqn_ref[...] = jnp.zeros_like(dqn_ref)
            dkn_ref[...] = jnp.zeros_like(dkn_ref)

        c, sa, sb = c_ref[...], sa_ref[...], sb_ref[...]
        dkr = jnp.zeros((tr, LANES), F32)
        for h in range(MLA_HEADS):
            cs = slice(h * LANES, (h + 1) * LANES)
            dqp[:, cs] = (_rope_t(dq_ref[:, cs], c, sa, sb) * MLA_SCALE).astype(BF)
            dkr = dkr + dk_ref[:, cs]
        lane = lax.broadcasted_iota(jnp.int32, (tr, LANES), 1)
        in_rope = (lane >= MLA_NOPE) & (lane < MLA_NOPE + MLA_ROPE)
        dslot_ref[...] = jnp.where(in_rope, _rope_t(dkr, c, sa, sb), 0.0).astype(dslot_ref.dtype)

        cq = cq_ref[...]
        gq = qn_ref[...]
        qn, rq = _rms(cq, gq)
        dqpb = dqp[...]
        dwuq_ref[...] += _dot_tn(qn.astype(BF), dqpb)
        dqn = _dot_nt(dqpb, wuq_ref[...])
        dqn_ref[...] += jnp.sum(dqn * cq * rq, axis=0, keepdims=True)
        dyg = dqn * gq
        dcq_ref[...] = (rq * dyg - cq * (rq * rq * rq) * jnp.mean(dyg * cq, axis=-1, keepdims=True)).astype(dcq_ref.dtype)

        ckv = ckv_ref[...]
        gk = kn_ref[...]
        kn, rk = _rms(ckv, gk)
        knb = kn.astype(BF)
        dkb = dk_ref[...].astype(BF)
        dvb = dv_ref[...].astype(BF)
        dwk_ref[...] += _dot_tn(knb, dkb)
        dwv_ref[...] += _dot_tn(knb, dvb)
        dkn = _dot_nt(dkb, wk_ref[...]) + _dot_nt(dvb, wv_ref[...])
        dkn_ref[...] += jnp.sum(dkn * ckv * rk, axis=0, keepdims=True)
        dyk = dkn * gk
        dckv_ref[...] = (rk * dyk - ckv * (rk * rk * rk) * jnp.mean(dyk * ckv, axis=-1, keepdims=True)).astype(dckv_ref.dtype)

    return pl.pallas_call(
        body, name="mla_pre_bwd", grid=(S // tr,),
        in_specs=[_rows(tr, 1024), _rows(tr, 1024), _rows(tr, 512), _rows(tr, 256, 4), _rows(tr, 128, 10),
                  _full((1, 256)), _full((1, 128)), _full((256, 1024)), _full((128, 1024)), _full((128, 512)),
                  _rows(tr, 128), _rows(tr, 128), _rows(tr, 128)],
        out_specs=(_rows(tr, 256), _rows(tr, 128), _rows(tr, 128), _full((256, 1024)), _full((128, 1024)),
                   _full((128, 512)), _full((1, 256)), _full((1, 128))),
        out_shape=(jax.ShapeDtypeStruct((S, 256), BF), jax.ShapeDtypeStruct((S, 128), BF),
                   jax.ShapeDtypeStruct((S, 128), BF), jax.ShapeDtypeStruct((256, 1024), F32),
                   jax.ShapeDtypeStruct((128, 1024), F32), jax.ShapeDtypeStruct((128, 512), F32),
                   jax.ShapeDtypeStruct((1, 256), F32), jax.ShapeDtypeStruct((1, 128), F32)),
        scratch_shapes=[pltpu.VMEM((tr, 1024), BF)],
        compiler_params=_cparams(("arbitrary",)),
    )(dq, dk, dv, proj, proj, q_norm, kv_norm, wuq, wk, wv, tc, tsa, tsb)


ANY = pl.BlockSpec(memory_space=pl.ANY)
N_REL = N_DEV - 1


def _coords():
    return lax.axis_index("x"), lax.axis_index("y"), lax.axis_index("c")


def _sem_shapes(nbuf):
    return [pltpu.SemaphoreType.DMA((N_REL * nbuf,)), pltpu.SemaphoreType.DMA((N_REL * nbuf,)),
            pltpu.SemaphoreType.DMA((nbuf,))]


def _ag_plan(srcs, dsts, sems):
    send_sems, recv_sems, _ = sems
    plan = []
    for b, (src, dst) in enumerate(zip(srcs, dsts)):
        def slot(px, py, pc, dst=dst):
            return dst.at[4 * px + 2 * py + pc]

        def copy(k, blk, to, s=None, b=b, slot=slot):
            return pltpu.make_async_remote_copy(
                src_ref=slot(*blk) if s is None else s, dst_ref=slot(*blk), send_sem=send_sems.at[N_REL * b + k],
                recv_sem=recv_sems.at[N_REL * b + k], device_id=to, device_id_type=MESH)

        plan.append((b, src, slot, copy))
    return plan


def _ag_start(srcs, dsts, sems):
    x, y, c = _coords()
    chips = [(1 - x, y), (x, 1 - y), (1 - x, 1 - y)]
    for b, src, slot, copy in _ag_plan(srcs, dsts, sems):
        pltpu.make_async_copy(src, slot(x, y, c), sems[2].at[b]).start()
        copy(0, (x, y, c), (x, y, 1 - c), src).start()
        for j, chip in enumerate(chips):
            copy(1 + j, (x, y, c), (*chip, c), src).start()


def _ag_finish(srcs, dsts, sems):
    x, y, c = _coords()
    me, sibling = (x, y, c), (x, y, 1 - c)
    chips = [(1 - x, y), (x, 1 - y), (1 - x, 1 - y)]
    plan = _ag_plan(srcs, dsts, sems)
    for b, src, slot, copy in plan:
        for j, chip in enumerate(chips):
            copy(1 + j, (*chip, c), me).wait_recv()
            copy(4 + j, (*chip, c), sibling).start()
    for b, src, slot, copy in plan:
        copy(0, sibling, me).wait_recv()
        for j, chip in enumerate(chips):
            copy(4 + j, (*chip, 1 - c), me).wait_recv()
        copy(0, me, sibling, src).wait_send()
        for j, chip in enumerate(chips):
            copy(1 + j, me, (*chip, c), src).wait_send()
            copy(4 + j, (*chip, c), sibling).wait_send()
        pltpu.make_async_copy(src, slot(*me), sems[2].at[b]).wait()


def _a2a_copies(srcs, dsts, sems):
    send_sems, recv_sems, local_sems = sems
    x, y, c = _coords()
    me = 4 * x + 2 * y + c
    local, remote = [], []
    for b, (src, dst) in enumerate(zip(srcs, dsts)):
        for rel in range(1, N_DEV):
            px = 1 - x if rel & 4 else x
            py = 1 - y if rel & 2 else y
            pc = 1 - c if rel & 1 else c
            remote.append(pltpu.make_async_remote_copy(
                src_ref=src.at[4 * px + 2 * py + pc], dst_ref=dst.at[me], send_sem=send_sems.at[N_REL * b + rel - 1],
                recv_sem=recv_sems.at[N_REL * b + rel - 1], device_id=(px, py, pc), device_id_type=MESH))
        local.append(pltpu.make_async_copy(src.at[me], dst.at[me], local_sems.at[b]))
    return local, remote


def _a2a_start(srcs, dsts, sems):
    local, remote = _a2a_copies(srcs, dsts, sems)
    for d in local + remote:
        d.start()


def _a2a_finish(srcs, dsts, sems):
    local, remote = _a2a_copies(srcs, dsts, sems)
    for d in remote + local:
        d.wait()


class _Exchange:
    def __init__(self, kind, srcs):
        self.kind, self.srcs = kind, list(srcs)
        self.n = len(self.srcs)

    def out_shapes(self):
        if self.kind == "ag":
            return [jax.ShapeDtypeStruct((N_DEV,) + s.shape, s.dtype) for s in self.srcs]
        return [jax.ShapeDtypeStruct(s.shape, s.dtype) for s in self.srcs]

    def start(self, src_refs, dst_refs, sems):
        (_ag_start if self.kind == "ag" else _a2a_start)(src_refs, dst_refs, sems)

    def finish(self, src_refs, dst_refs, sems):
        (_ag_finish if self.kind == "ag" else _a2a_finish)(src_refs, dst_refs, sems)


def _exchange_call(name, exchanges):
    ns = [e.n for e in exchanges]
    tot = sum(ns)

    def body(*refs):
        srcs, dsts, sems = refs[:tot], refs[tot:2 * tot], refs[2 * tot:]
        views, off = [], 0
        for i, e in enumerate(exchanges):
            views.append((srcs[off:off + e.n], dsts[off:off + e.n], sems[3 * i:3 * i + 3]))
            off += e.n
        for e, view in zip(exchanges, views):
            e.start(*view)
        for e, view in zip(exchanges, views):
            e.finish(*view)

    outs = pl.pallas_call(
        body, name=name, in_specs=[ANY] * tot, out_specs=[ANY] * tot,
        out_shape=[s for e in exchanges for s in e.out_shapes()],
        scratch_shapes=[s for e in exchanges for s in _sem_shapes(e.n)],
    )(*[s for e in exchanges for s in e.srcs])
    res, off = [], 0
    for e in exchanges:
        res.append(list(outs[off:off + e.n]))
        off += e.n
    return res


def _grid_ends(n):
    ids = [pl.program_id(a) for a in range(3)]
    first = (ids[0] == 0) & (ids[1] == 0) & (ids[2] == 0)
    last = (ids[0] == MLA_HEADS // 2 - 1) & (ids[1] == n - 1) & (ids[2] == n - 1)
    return first, last


def _with_exchange(body, ex, n_in, n_out, n_scratch, n):
    if ex is None:
        return body

    def wrapped(*refs):
        ins = refs[:n_in]
        srcs = refs[n_in:n_in + ex.n]
        outs = refs[n_in + ex.n:n_in + ex.n + n_out]
        dsts = refs[n_in + ex.n + n_out:n_in + 2 * ex.n + n_out]
        scratch = refs[n_in + 2 * ex.n + n_out:n_in + 2 * ex.n + n_out + n_scratch]
        sems = refs[n_in + 2 * ex.n + n_out + n_scratch:]
        first, last = _grid_ends(n)

        @pl.when(first)
        def _():
            ex.start(srcs, dsts, sems)

        body(*ins, *outs, *scratch)

        @pl.when(last)
        def _():
            ex.finish(srcs, dsts, sems)

    return wrapped


def _flash_call(body, name, ex, in_specs, out_specs, out_shape, scratch, n, args):
    if ex is None:
        return pl.pallas_call(
            body, name=name, grid=(MLA_HEADS // 2, n, n), in_specs=in_specs, out_specs=out_specs, out_shape=out_shape,
            scratch_shapes=scratch, compiler_params=_cparams(("parallel", "parallel", "arbitrary")))(*args), []
    res = pl.pallas_call(
        _with_exchange(body, ex, len(in_specs), len(out_specs), len(scratch), n), name=name + "_x",
        grid=(MLA_HEADS // 2, n, n), in_specs=list(in_specs) + [ANY] * ex.n, out_specs=list(out_specs) + [ANY] * ex.n,
        out_shape=list(out_shape) + ex.out_shapes(), scratch_shapes=list(scratch) + _sem_shapes(ex.n),
        compiler_params=_cparams(("arbitrary",) * 3))(*args, *ex.srcs)
    return res[:len(out_specs)], list(res[len(out_specs):])


def _lower_tri(t):
    return lax.broadcasted_iota(jnp.int32, (t, t), 0) >= lax.broadcasted_iota(jnp.int32, (t, t), 1)


def _upper_tri(t):
    return lax.broadcasted_iota(jnp.int32, (t, t), 1) >= lax.broadcasted_iota(jnp.int32, (t, t), 0)


def _flash_fwd(q, kt, v, ex=None):
    S = q.shape[0]
    t = min(T_ATT, S)
    n = S // t

    def body(q_ref, kt_ref, v_ref, o_ref, lse_ref, m_s, l_s, acc):
        qi, ki = pl.program_id(1), pl.program_id(2)
        lo = lax.broadcasted_iota(jnp.int32, (t, LANES), 1) < MLA_V

        @pl.when(ki == 0)
        def _():
            m_s[...] = jnp.full_like(m_s, NEG)
            l_s[...] = jnp.zeros_like(l_s)
            acc[...] = jnp.zeros_like(acc)

        @pl.when(ki <= qi)
        def _():
            keep = _lower_tri(t) | (ki < qi)
            vv = v_ref[...]
            heads = range(2)
            ss = [jnp.where(keep, _dot(q_ref[:, h * LANES:(h + 1) * LANES], kt_ref[h * LANES:(h + 1) * LANES, :]), NEG)
                  for h in heads]
            m_prev = [m_s[h] for h in heads]
            l_prev = [l_s[h] for h in heads]
            m_new = [jnp.maximum(m_prev[h], jnp.max(ss[h], axis=1, keepdims=True)) for h in heads]
            al = [jnp.exp(m_prev[h] - m_new[h]) for h in heads]
            ps = [jnp.exp(ss[h] - m_new[h][:, :1]) for h in heads]
            l_new = [al[h] * l_prev[h] + jnp.sum(ps[h], axis=1, keepdims=True) for h in heads]
            pv = [_dot(ps[h].astype(BF), vv) for h in heads]
            for h in heads:
                m_s[h] = m_new[h]
                l_s[h] = l_new[h]
            acc[...] = jnp.where(lo, al[0], al[1]) * acc[...] + jnp.where(lo, pv[0], pv[1])

        @pl.when(ki == qi)
        def _():
            o_ref[...] = acc[...] / jnp.where(lo, l_s[0], l_s[1])
            lse_ref[0] = jnp.where(lo, m_s[0] + jnp.log(l_s[0]), m_s[1] + jnp.log(l_s[1]))

    return _flash_call(
        body, "mla_flash_fwd", ex,
        [pl.BlockSpec((t, 256), lambda p, i, j: (i, p)),
         pl.BlockSpec((256, t), lambda p, i, j: (p, jnp.minimum(j, i))),
         pl.BlockSpec((t, 128), lambda p, i, j: (jnp.minimum(j, i), p))],
        [pl.BlockSpec((t, 128), lambda p, i, j: (i, p)), pl.BlockSpec((1, t, 128), lambda p, i, j: (p, i, 0))],
        [jax.ShapeDtypeStruct((S, 512), F32), jax.ShapeDtypeStruct((MLA_HEADS // 2, S, 128), F32)],
        [pltpu.VMEM((2, t, 128), F32), pltpu.VMEM((2, t, 128), F32), pltpu.VMEM((t, 128), F32)], n, (q, kt, v))


def _flash_bwd_dq(q, k, kt, vt, do, lse, delta, ex=None):
    S = q.shape[0]
    t = min(T_ATT, S)
    n = S // t

    def body(q_ref, k_ref, kt_ref, vt_ref, do_ref, lse_ref, dl_ref, dq_ref, acc):
        qi, ki = pl.program_id(1), pl.program_id(2)
        lo = lax.broadcasted_iota(jnp.int32, (t, LANES), 1) < MLA_V

        @pl.when(ki == 0)
        def _():
            acc[...] = jnp.zeros_like(acc)

        @pl.when(ki <= qi)
        def _():
            keep = _lower_tri(t) | (ki < qi)
            heads = range(2)
            cs = [slice(h * LANES, (h + 1) * LANES) for h in heads]
            col = [slice(h * MLA_V, h * MLA_V + 1) for h in heads]
            lse, dl, dov, vt = lse_ref[0], dl_ref[0], do_ref[...], vt_ref[...]
            ss = [jnp.where(keep, _dot(q_ref[:, cs[h]], kt_ref[cs[h], :]), NEG) for h in heads]
            dp = [_dot(jnp.where(lo if h == 0 else ~lo, dov, 0).astype(BF), vt) for h in heads]
            ds = [(jnp.exp(ss[h] - lse[:, col[h]]) * (dp[h] - dl[:, col[h]])).astype(BF) for h in heads]
            dq = [_dot(ds[h], k_ref[:, cs[h]]) for h in heads]
            acc[...] += jnp.concatenate(dq, axis=1)

        @pl.when(ki == qi)
        def _():
            dq_ref[...] = acc[...]

    (dq,), got = _flash_call(
        body, "mla_flash_dq", ex,
        [pl.BlockSpec((t, 256), lambda p, i, j: (i, p)),
         pl.BlockSpec((t, 256), lambda p, i, j: (jnp.minimum(j, i), p)),
         pl.BlockSpec((256, t), lambda p, i, j: (p, jnp.minimum(j, i))),
         pl.BlockSpec((128, t), lambda p, i, j: (p, jnp.minimum(j, i))),
         pl.BlockSpec((t, 128), lambda p, i, j: (i, p)),
         pl.BlockSpec((1, t, 128), lambda p, i, j: (p, i, 0)),
         pl.BlockSpec((1, t, 128), lambda p, i, j: (p, i, 0))],
        [pl.BlockSpec((t, 256), lambda p, i, j: (i, p))],
        [jax.ShapeDtypeStruct((S, 1024), F32)],
        [pltpu.VMEM((t, 256), F32)], n, (q, k, kt, vt, do, lse, delta))
    return dq, got


def _flash_bwd_dkv(q, qt, k, v, do, dot_, lse_t, delta_t, ex=None):
    S = q.shape[0]
    t = min(T_ATT, S)
    n = S // t

    def body(q_ref, qt_ref, k_ref, v_ref, do_ref, dot_ref, lse_ref, dl_ref, dk_ref, dv_ref, dk_acc, dv_acc):
        ki, qi = pl.program_id(1), pl.program_id(2)
        lo = lax.broadcasted_iota(jnp.int32, (t, LANES), 1) < MLA_V
        top = lax.broadcasted_iota(jnp.int32, (LANES, t), 0) < MLA_V

        @pl.when(qi == 0)
        def _():
            dk_acc[...] = jnp.zeros_like(dk_acc)
            dv_acc[...] = jnp.zeros_like(dv_acc)

        @pl.when(qi >= ki)
        def _():
            keep = _upper_tri(t) | (qi > ki)
            heads = range(2)
            cs = [slice(h * LANES, (h + 1) * LANES) for h in heads]
            vv, lse, dl, dov, dot_v = v_ref[...], lse_ref[0], dl_ref[0], do_ref[...], dot_ref[...]
            st = [jnp.where(keep, _dot(k_ref[:, cs[h]], qt_ref[cs[h], :]), NEG) for h in heads]
            dpt = [_dot(vv, jnp.where(top if h == 0 else ~top, dot_v, 0).astype(BF)) for h in heads]
            pt = [jnp.exp(st[h] - lse[h:h + 1, :]) for h in heads]
            dst = [(pt[h] * (dpt[h] - dl[h:h + 1, :])).astype(BF) for h in heads]
            dv = [_dot(pt[h].astype(BF), jnp.where(lo if h == 0 else ~lo, dov, 0).astype(BF)) for h in heads]
            dk = [_dot(dst[h], q_ref[:, cs[h]]) for h in heads]
            dv_acc[...] += dv[0] + dv[1]
            dk_acc[...] += jnp.concatenate(dk, axis=1)

        @pl.when(qi == n - 1)
        def _():
            dk_ref[...] = dk_acc[...]
            dv_ref[...] = dv_acc[...]

    (dk, dv), got = _flash_call(
        body, "mla_flash_dkv", ex,
        [pl.BlockSpec((t, 256), lambda p, j, i: (jnp.maximum(i, j), p)),
         pl.BlockSpec((256, t), lambda p, j, i: (p, jnp.maximum(i, j))),
         pl.BlockSpec((t, 256), lambda p, j, i: (j, p)),
         pl.BlockSpec((t, 128), lambda p, j, i: (j, p)),
         pl.BlockSpec((t, 128), lambda p, j, i: (jnp.maximum(i, j), p)),
         pl.BlockSpec((128, t), lambda p, j, i: (p, jnp.maximum(i, j))),
         pl.BlockSpec((1, SUBLANES, t), lambda p, j, i: (p, 0, jnp.maximum(i, j))),
         pl.BlockSpec((1, SUBLANES, t), lambda p, j, i: (p, 0, jnp.maximum(i, j)))],
        [pl.BlockSpec((t, 256), lambda p, j, i: (j, p)), pl.BlockSpec((t, 128), lambda p, j, i: (j, p))],
        [jax.ShapeDtypeStruct((S, 1024), F32), jax.ShapeDtypeStruct((S, 512), F32)],
        [pltpu.VMEM((t, 256), F32), pltpu.VMEM((t, 128), F32)], n, (q, qt, k, v, do, dot_, lse_t, delta_t))
    return dk, dv, got


def _row_stats(a):
    two = jnp.stack([a[:, :, 0], a[:, :, MLA_V]], axis=1)
    return jnp.concatenate([two, jnp.zeros((a.shape[0], SUBLANES - 2, a.shape[1]), a.dtype)], axis=1)


def _mem_heads(qm, km_ref, vm_ref):
    ps, os_ = [], []
    for h in range(X_HEADS):
        cs = slice(h * X_HEAD_DIM, (h + 1) * X_HEAD_DIM)
        s = _dot_nt(qm[:, cs].astype(BF), km_ref[:, cs]) * MEM_SCALE
        e = jnp.exp(s - jnp.max(s, axis=1, keepdims=True))
        p = e / jnp.sum(e, axis=1, keepdims=True)
        ps.append(p)
        os_.append(_dot(p.astype(BF), vm_ref[:, cs]))
    return ps, jnp.concatenate(os_, axis=1)


def _mem_fwd(proj, km, vm, p_mem):
    S = proj.shape[0]
    tr = min(T_ROWS, S)
    M = km.shape[0]

    def body(q_ref, z_ref, km_ref, vm_ref, p_ref, o_ref):
        _, o = _mem_heads(q_ref[...], km_ref, vm_ref)
        y = o * _silu(z_ref[...])
        o_ref[...] = _dot(y.astype(BF), p_ref[...])

    return pl.pallas_call(
        body, name="mem_fwd", grid=(S // tr,),
        in_specs=[_rows(tr, 512, 4), _rows(tr, 512, 5), _full((M, 512)), _full((M, 512)), _full((512, 1024))],
        out_specs=_rows(tr, 1024), out_shape=jax.ShapeDtypeStruct((S, D_MODEL), F32),
        compiler_params=_cparams(("parallel",)),
    )(proj, proj, km, vm, p_mem)


def _mem_bwd(do, proj, km, vm, p_mem):
    S = proj.shape[0]
    tr = min(T_ROWS_BWD, S)
    M = km.shape[0]

    def body(do_ref, q_ref, z_ref, km_ref, vm_ref, p_ref, dq_ref, dz_ref, dkm_ref, dvm_ref, dp_ref):
        @pl.when(pl.program_id(0) == 0)
        def _():
            dkm_ref[...] = jnp.zeros_like(dkm_ref)
            dvm_ref[...] = jnp.zeros_like(dvm_ref)
            dp_ref[...] = jnp.zeros_like(dp_ref)

        qm = q_ref[...]
        z = z_ref[...]
        ps, o = _mem_heads(qm, km_ref, vm_ref)
        sz = _silu(z)
        y = o * sz
        dob = do_ref[...].astype(BF)
        dy = _dot_nt(dob, p_ref[...])
        dp_ref[...] += _dot_tn(y.astype(BF), dob)
        dz_ref[...] = (dy * o * _dsilu(z)).astype(dz_ref.dtype)
        d_o = dy * sz
        for h in range(X_HEADS):
            cs = slice(h * X_HEAD_DIM, (h + 1) * X_HEAD_DIM)
            doh = d_o[:, cs]
            dohb = doh.astype(BF)
            p = ps[h]
            dpr = _dot_nt(dohb, vm_ref[:, cs])
            ds = (p * (dpr - jnp.sum(doh * o[:, cs], axis=1, keepdims=True)) * MEM_SCALE).astype(BF)
            dq_ref[:, cs] = _dot(ds, km_ref[:, cs]).astype(dq_ref.dtype)
            dkm_ref[:, cs] += _dot_tn(ds, qm[:, cs].astype(BF))
            dvm_ref[:, cs] += _dot_tn(p.astype(BF), dohb)

    return pl.pallas_call(
        body, name="mem_bwd", grid=(S // tr,),
        in_specs=[_rows(tr, 1024), _rows(tr, 512, 4), _rows(tr, 512, 5), _full((M, 512)), _full((M, 512)),
                  _full((512, 1024))],
        out_specs=(_rows(tr, 512), _rows(tr, 512), _full((M, 512)), _full((M, 512)), _full((512, 1024))),
        out_shape=(jax.ShapeDtypeStruct((S, 512), BF), jax.ShapeDtypeStruct((S, 512), BF),
                   jax.ShapeDtypeStruct((M, 512), F32), jax.ShapeDtypeStruct((M, 512), F32),
                   jax.ShapeDtypeStruct((512, 1024), F32)),
        compiler_params=_cparams(("arbitrary",)),
    )(do, proj, proj, km, vm, p_mem)


def _merge_fwd(x, proj, o_ssm, o_att, o_mem, b_gate, p_mla, w_out, ln_g, ln_b):
    S = x.shape[0]
    tr = min(T_ROWS, S)

    def body(x_ref, lg_ref, z_ref, os_ref, oa_ref, om_ref, bg_ref, p_ref, w_ref, g_ref, b_ref,
             xn_ref, xb_ref, pre_ref, mg_ref):
        gates = _sigmoid(lg_ref[...] + bg_ref[...])
        ya = oa_ref[...] * _silu(z_ref[...])
        o_mla = _dot(ya.astype(BF), p_ref[...])
        merged = (gates[:, :D_MODEL] * os_ref[...] + gates[:, D_MODEL:2 * D_MODEL] * o_mla
                  + gates[:, 2 * D_MODEL:] * om_ref[...])
        mb = merged.astype(BF)
        mg_ref[...] = mb
        pre = ALPHA * x_ref[...] + _dot(mb, w_ref[...])
        pre_ref[...] = pre
        mu = jnp.mean(pre, axis=-1, keepdims=True)
        xc = pre - mu
        var = jnp.mean(xc * xc, axis=-1, keepdims=True)
        xn = xc * lax.rsqrt(var + NORM_EPS) * g_ref[...] + b_ref[...]
        xn_ref[...] = xn
        xb_ref[...] = xn.astype(BF)

    return pl.pallas_call(
        body, name="merge_fwd", grid=(S // tr,),
        in_specs=[_rows(tr, 1024), _rows(tr, 3072, 1), _rows(tr, 512, 3), _rows(tr, 1024), _rows(tr, 512),
                  _rows(tr, 1024), _full((1, 3072)), _full((512, 1024)), _full((1024, 1024)), _full((1, 1024)),
                  _full((1, 1024))],
        out_specs=(_rows(tr, 1024), _rows(tr, 1024), _rows(tr, 1024), _rows(tr, 1024)),
        out_shape=(jax.ShapeDtypeStruct((S, 1024), F32), jax.ShapeDtypeStruct((S, 1024), BF),
                   jax.ShapeDtypeStruct((S, 1024), F32), jax.ShapeDtypeStruct((S, 1024), BF)),
        compiler_params=_cparams(("parallel",)),
    )(x, proj, proj, o_ssm, o_att, o_mem, b_gate, p_mla, w_out, ln_g, ln_b)


def _merge_bwd(dxn, pre, merged, proj, o_ssm, o_att, o_mem, b_gate, p_mla, w_out, ln_g):
    S = pre.shape[0]
    tr = min(T_ROWS_BWD, S)

    def body(dxn_ref, pre_ref, mg_ref, lg_ref, z_ref, os_ref, oa_ref, om_ref, bg_ref, p_ref, w_ref, g_ref,
             dxr_ref, dlg_ref, dos_ref, dom_ref, doa_ref, dz_ref, dl_ref, dw_ref, dp_ref, dbg_ref, dg_ref, db_ref):
        @pl.when(pl.program_id(0) == 0)
        def _():
            dw_ref[...] = jnp.zeros_like(dw_ref)
            dp_ref[...] = jnp.zeros_like(dp_ref)
            dbg_ref[...] = jnp.zeros_like(dbg_ref)
            dg_ref[...] = jnp.zeros_like(dg_ref)
            db_ref[...] = jnp.zeros_like(db_ref)

        dxn = dxn_ref[...]
        pre = pre_ref[...]
        mu = jnp.mean(pre, axis=-1, keepdims=True)
        xc = pre - mu
        rstd = lax.rsqrt(jnp.mean(xc * xc, axis=-1, keepdims=True) + NORM_EPS)
        xhat = xc * rstd
        dg_ref[...] += jnp.sum(dxn * xhat, axis=0, keepdims=True)
        db_ref[...] += jnp.sum(dxn, axis=0, keepdims=True)
        dxh = dxn * g_ref[...]
        dpre = rstd * (dxh - jnp.mean(dxh, axis=-1, keepdims=True)
                       - xhat * jnp.mean(dxh * xhat, axis=-1, keepdims=True))
        dxr_ref[...] = ALPHA * dpre
        dpb = dpre.astype(BF)
        dw_ref[...] += _dot_tn(mg_ref[...], dpb)
        dm = _dot_nt(dpb, w_ref[...])

        gates = _sigmoid(lg_ref[...] + bg_ref[...])
        g0, g1, g2 = gates[:, :D_MODEL], gates[:, D_MODEL:2 * D_MODEL], gates[:, 2 * D_MODEL:]
        z = z_ref[...]
        oa = oa_ref[...]
        sz = _silu(z)
        ya = (oa * sz).astype(BF)
        o_mla = _dot(ya, p_ref[...])
        dos_ref[...] = (g0 * dm).astype(dos_ref.dtype)
        dom_ref[...] = (g2 * dm).astype(dom_ref.dtype)
        do_mla = (g1 * dm).astype(BF)
        dl0 = dm * os_ref[...] * g0 * (1.0 - g0)
        dl1 = dm * o_mla * g1 * (1.0 - g1)
        dl2 = dm * om_ref[...] * g2 * (1.0 - g2)
        dl = jnp.concatenate([dl0, dl1, dl2], axis=1)
        dbg_ref[...] += jnp.sum(dl, axis=0, keepdims=True)
        dlg_ref[...] = dl.astype(dlg_ref.dtype)
        dp_ref[...] += _dot_tn(ya, do_mla)
        dya = _dot_nt(do_mla, p_ref[...])
        doa = dya * sz
        doa_ref[...] = doa.astype(doa_ref.dtype)
        dz_ref[...] = (dya * oa * _dsilu(z)).astype(dz_ref.dtype)
        prod = doa * oa
        lo = lax.broadcasted_iota(jnp.int32, (tr, LANES), 1) < MLA_V
        for pr in range(MLA_HEADS // 2):
            blk = prod[:, pr * LANES:(pr + 1) * LANES]
            d0 = jnp.sum(jnp.where(lo, blk, 0.0), axis=1, keepdims=True)
            d1 = jnp.sum(jnp.where(lo, 0.0, blk), axis=1, keepdims=True)
            dl_ref[pr] = jnp.where(lo, d0, d1)

    return pl.pallas_call(
        body, name="merge_bwd", grid=(S // tr,),
        in_specs=[_rows(tr, 1024), _rows(tr, 1024), _rows(tr, 1024), _rows(tr, 3072, 1), _rows(tr, 512, 3),
                  _rows(tr, 1024), _rows(tr, 512), _rows(tr, 1024), _full((1, 3072)), _full((512, 1024)),
                  _full((1024, 1024)), _full((1, 1024))],
        out_specs=(_rows(tr, 1024), _rows(tr, 3072), _rows(tr, 1024), _rows(tr, 1024), _rows(tr, 512),
                   _rows(tr, 512), pl.BlockSpec((MLA_HEADS // 2, tr, 128), lambda i: (0, i, 0)),
                   _full((1024, 1024)), _full((512, 1024)), _full((1, 3072)), _full((1, 1024)), _full((1, 1024))),
        out_shape=(jax.ShapeDtypeStruct((S, 1024), F32), jax.ShapeDtypeStruct((S, 3072), BF),
                   jax.ShapeDtypeStruct((S, 1024), BF), jax.ShapeDtypeStruct((S, 1024), BF),
                   jax.ShapeDtypeStruct((S, 512), BF), jax.ShapeDtypeStruct((S, 512), BF),
                   jax.ShapeDtypeStruct((MLA_HEADS // 2, S, 128), F32),
                   jax.ShapeDtypeStruct((1024, 1024), F32), jax.ShapeDtypeStruct((512, 1024), F32),
                   jax.ShapeDtypeStruct((1, 3072), F32), jax.ShapeDtypeStruct((1, 1024), F32),
                   jax.ShapeDtypeStruct((1, 1024), F32)),
        compiler_params=_cparams(("arbitrary",)),
    )(dxn, pre, merged, proj, proj, o_ssm, o_att, o_mem, b_gate, p_mla, w_out, ln_g)


def _loss_head(y, t):
    S = y.shape[0]
    tr = min(T_ROWS, S)
    n = S // tr

    def body(y_ref, t_ref, dy_ref, l_ref, acc):
        i = pl.program_id(0)

        @pl.when(i == 0)
        def _():
            acc[...] = jnp.zeros_like(acc)

        e = y_ref[...] - t_ref[...]
        dy_ref[...] = e * (1.0 / D_MODEL)
        acc[...] += jnp.sum(e * e, axis=0, keepdims=True)

        @pl.when(i == n - 1)
        def _():
            tot = jnp.sum(acc[...], axis=1, keepdims=True) * (0.5 / D_MODEL)
            l_ref[...] = jnp.broadcast_to(tot, l_ref.shape)

    return pl.pallas_call(
        body, name="loss_head", grid=(n,),
        in_specs=[_rows(tr, 1024), _rows(tr, 1024)],
        out_specs=(_rows(tr, 1024), _full((SUBLANES, LANES))),
        out_shape=(jax.ShapeDtypeStruct((S, 1024), F32), jax.ShapeDtypeStruct((SUBLANES, LANES), F32)),
        scratch_shapes=[pltpu.VMEM((1, 1024), F32)],
        compiler_params=_cparams(("arbitrary",)),
    )(y, t)


def _rope_tables(pos):
    inv_freq = ROPE_THETA ** (-jnp.arange(0, MLA_ROPE, 2, dtype=F32) / MLA_ROPE)
    ang = pos.astype(F32)[:, None] * inv_freq
    cos, sin = jnp.cos(ang), jnp.sin(ang)
    S = pos.shape[0]
    half = MLA_ROPE // 2
    ones = jnp.ones((S, MLA_NOPE), F32)
    z16 = jnp.zeros((S, half), F32)
    z32 = jnp.zeros((S, LANES - MLA_NOPE - MLA_ROPE), F32)
    z64 = jnp.zeros((S, MLA_NOPE), F32)
    c = jnp.concatenate([ones, cos, cos, z32], axis=1)
    sa = jnp.concatenate([z64, -sin, z16, z32], axis=1)
    sb = jnp.concatenate([z64, z16, sin, z32], axis=1)
    return c, sa, sb


def _ssm_discretise(a_re, a_im, log_dt, b_re, b_im):
    dt = jnp.exp(log_dt)[..., None]
    mag = jnp.exp(a_re * dt)
    lb_re = mag * jnp.cos(a_im * dt)
    lb_im = mag * jnp.sin(a_im * dt)
    nr, ni = lb_re - 1.0, lb_im
    den = a_re * a_re + a_im * a_im
    f_re = (nr * a_re + ni * a_im) / den
    f_im = (ni * a_re - nr * a_im) / den
    bb_re = f_re[..., None] * b_re - f_im[..., None] * b_im
    bb_im = f_re[..., None] * b_im + f_im[..., None] * b_re
    return lb_re, lb_im, bb_re, bb_im


_GPB = SSM_CB // SSM_GROUP


def _bd_in(bb):
    nb = SSM_GROUPS // _GPB
    t = bb.reshape(nb, _GPB, SSM_STATE, SSM_GROUP)
    eye = jnp.eye(_GPB, dtype=bb.dtype)
    return jnp.einsum("ngpc,gh->ngchp", t, eye).reshape(nb, SSM_CB, _GPB * SSM_STATE)


def _bd_in_t(d):
    nb = SSM_GROUPS // _GPB
    t = d.reshape(nb, _GPB, SSM_GROUP, _GPB, SSM_STATE)
    eye = jnp.eye(_GPB, dtype=d.dtype)
    return jnp.einsum("ngchp,gh->ngpc", t, eye).reshape(SSM_GROUPS, SSM_STATE, SSM_GROUP)


def _bd_out(c):
    nb = SSM_GROUPS // _GPB
    t = c.reshape(nb, _GPB, SSM_GROUP, SSM_STATE)
    eye = jnp.eye(_GPB, dtype=c.dtype)
    return jnp.einsum("ngcp,gh->ngphc", t, eye).reshape(nb, _GPB * SSM_STATE, SSM_CB)


def _bd_out_t(d):
    nb = SSM_GROUPS // _GPB
    t = d.reshape(nb, _GPB, SSM_STATE, _GPB, SSM_GROUP)
    eye = jnp.eye(_GPB, dtype=d.dtype)
    return jnp.einsum("ngphc,gh->ngcp", t, eye).reshape(SSM_GROUPS, SSM_GROUP, SSM_STATE)


def _interleave(a):
    S, w = a.shape
    return a.reshape(SUBLANES, S // SUBLANES, w).transpose(1, 0, 2).reshape(S, w)


def _deinterleave(a):
    S, w = a.shape
    return a.reshape(S // SUBLANES, SUBLANES, w).transpose(1, 0, 2).reshape(S, w)


def _pad_w_in(w):
    z = lambda n: jnp.zeros(w.shape[:-1] + (n,), w.dtype)
    return jnp.concatenate([w[..., :ROPE_SLOT_LO], z(MLA_NOPE), w[..., ROPE_SLOT_LO:ROPE_SLOT_LO + MLA_ROPE],
                            z(LANES - MLA_NOPE - MLA_ROPE), w[..., ROPE_SLOT_LO + MLA_ROPE:]], axis=-1)


def _unpad_w_in(w):
    lo = ROPE_SLOT_LO + MLA_NOPE
    return jnp.concatenate([w[..., :ROPE_SLOT_LO], w[..., lo:lo + MLA_ROPE], w[..., ROPE_SLOT_LO + LANES:]], axis=-1)


def _adamw_math(w, g, m, v):
    m = ADAM_B1 * m + (1.0 - ADAM_B1) * g
    v = ADAM_B2 * v + (1.0 - ADAM_B2) * (g * g)
    m_hat = m / (1.0 - ADAM_B1 ** ADAM_STEP)
    v_hat = v / (1.0 - ADAM_B2 ** ADAM_STEP)
    delta = -ADAM_LR * (m_hat / (jnp.sqrt(v_hat) + ADAM_EPS) + ADAM_WD * w)
    return delta, m, v


def _adamw_sharded(parts, w, m, v, tile, name):
    L, _, R, C = parts.shape
    assert R % tile == 0

    def body(p_ref, w_ref, m_ref, v_ref, g_out, d_out, m_out, v_out):
        g = p_ref[0, 0].astype(F32)
        for k in range(1, N_DEV):
            g = g + p_ref[0, k].astype(F32)
        d, mn, vn = _adamw_math(w_ref[0], g, m_ref[0], v_ref[0])
        g_out[0] = g
        d_out[0] = d
        m_out[0] = mn
        v_out[0] = vn

    spec = pl.BlockSpec((1, tile, C), lambda l, i: (l, i, 0))
    shp = jax.ShapeDtypeStruct((L, R, C), F32)
    return pl.pallas_call(
        body, name=name, grid=(L, R // tile),
        in_specs=[pl.BlockSpec((1, N_DEV, tile, C), lambda l, i: (l, 0, i, 0)), spec, spec, spec],
        out_specs=(spec,) * 4, out_shape=(shp,) * 4, compiler_params=_cparams(("parallel", "parallel")),
    )(parts, w, m, v)


COL_GROUP = (("w_glu", 512), ("w_ukv", 128), ("p_ssm", 512), ("p_mla", 512), ("p_mem", 512), ("w_uq", 256))
ROW_GROUP = ("w_mem_kv", "w_out")
SMALL = ("b_gate", "ssm_a_re", "ssm_a_im", "ssm_log_dt", "ssm_b_re", "ssm_b_im", "ssm_c_re", "ssm_c_im", "ssm_d",
         "b_glu", "mla_q_norm", "mla_kv_norm", "ln_g", "ln_b")
SMALL_TILE = 512
UQ_COLS = MLA_NOPE + MLA_ROPE


def _pad_lanes(a):
    return jnp.concatenate([a, jnp.zeros(a.shape[:-1] + (LANES - a.shape[-1],), a.dtype)], axis=-1)


def _group_buffers(d, dtype):
    col = jnp.concatenate([_pad_lanes(d[n]) if n == "w_uq" else d[n] for n, _ in COL_GROUP], axis=1)
    row = jnp.concatenate([d[n] for n in ROW_GROUP], axis=1)
    return d["w_in"].astype(dtype), col.astype(dtype), row.astype(dtype)


def _ungroup(bufs):
    b_in, col, row = bufs
    out, off = {"w_in": b_in}, 0
    for n, rows in COL_GROUP:
        t = col[:, off:off + rows]
        out[n] = t[..., :UQ_COLS] if n == "w_uq" else t
        off += rows
    k = row.shape[1] // 2
    out["w_mem_kv"], out["w_out"] = row[:, :k], row[:, k:]
    return out


def _colcat(t):
    return t.transpose(1, 0, 2).reshape(t.shape[1], -1)


def _colsplit(g, n):
    return g.reshape(g.shape[0], N_DEV, n).transpose(1, 0, 2)


def _unpack_weights(g_in, g_col, g_row):
    off, parts = 0, {}
    for n, rows in COL_GROUP:
        parts[n] = g_col[:, off:off + rows]
        off += rows
    ukv = parts["w_ukv"]
    lane = lax.broadcasted_iota(jnp.int32, ukv.shape, 2)
    k = g_row.shape[1] // 2
    return dict(
        w_in=_pad_w_in(_colcat(g_in)), w_glu=_colcat(parts["w_glu"]), w_uq=_colcat(parts["w_uq"]),
        w_k=_colcat(jnp.where(lane < MLA_NOPE, ukv, jnp.zeros_like(ukv))), w_v=_colcat(ukv[..., MLA_NOPE:]),
        p_ssm=_colcat(parts["p_ssm"]), p_mla=_colcat(parts["p_mla"]), p_mem=_colcat(parts["p_mem"]),
        w_mem_kv=g_row[:, :k].reshape(-1, g_row.shape[2]), w_out=g_row[:, k:].reshape(-1, g_row.shape[2]))


def _pack_grads(d):
    ukv = jnp.concatenate([d["w_k"].reshape(-1, MLA_HEADS, LANES)[..., :MLA_NOPE],
                           d["w_v"].reshape(-1, MLA_HEADS, MLA_V)], axis=-1).transpose(1, 0, 2)
    cols = dict(w_glu=_colsplit(d["w_glu"], LANES), w_ukv=ukv, p_ssm=_colsplit(d["p_ssm"], LANES),
                p_mla=_colsplit(d["p_mla"], LANES), p_mem=_colsplit(d["p_mem"], LANES), w_uq=_colsplit(d["w_uq"], LANES))
    col = jnp.concatenate([cols[n] for n, _ in COL_GROUP], axis=1)
    row = jnp.concatenate([d[n].reshape(N_DEV, -1, d[n].shape[1]) for n in ROW_GROUP], axis=1)
    return [_colsplit(_unpad_w_in(d["w_in"]), D_IN // N_DEV).astype(BF), col.astype(BF), row.astype(BF)]


def _pack_small(d):
    parts = []
    for n in SMALL:
        f = d[n].reshape(-1)
        pad = (-f.shape[0]) % (SUBLANES * LANES)
        if pad:
            f = jnp.concatenate([f, jnp.zeros((pad,), f.dtype)])
        parts.append(f.reshape(-1, LANES))
    rows = sum(p.shape[0] for p in parts)
    pad = (-rows) % SMALL_TILE
    if pad:
        parts.append(jnp.zeros((pad, LANES), parts[0].dtype))
    return jnp.concatenate(parts, axis=0)


def _unpack_small(buf, like):
    out, off = {}, 0
    for n in SMALL:
        size = math.prod(like[n].shape)
        rows = -(-size // (SUBLANES * LANES)) * SUBLANES
        out[n] = buf[off:off + rows].reshape(-1)[:size].reshape(like[n].shape)
        off += rows
    return out


WEIGHTS = ("w_in", "b_gate", "ssm_a_re", "ssm_a_im", "ssm_log_dt", "ssm_b_re", "ssm_b_im", "ssm_c_re", "ssm_c_im",
           "ssm_d", "w_glu", "b_glu", "mla_q_norm", "w_uq", "mla_kv_norm", "w_ukv", "w_mem_kv", "p_ssm", "p_mla",
           "p_mem", "w_out", "ln_g", "ln_b")
BIG = ("w_in",) + tuple(n for n, _ in COL_GROUP) + ROW_GROUP


def _train_step(x, mem, pos, target, wl, ws):
    S = x.shape[0]
    tc, tsa, tsb = _rope_tables(pos)
    loc = _group_buffers(wl, BF)
    loc = [[b[l] for b in loc] for l in range(DEPTH)]

    def ssm_mats(l):
        lb_re, lb_im, bb_re, bb_im = _ssm_discretise(ws["ssm_a_re"][l], ws["ssm_a_im"][l], ws["ssm_log_dt"][l],
                                                     ws["ssm_b_re"][l], ws["ssm_b_im"][l])
        nb = SSM_GROUPS // _GPB
        return (_bd_in(bb_re), _bd_in(bb_im), lb_re.reshape(nb, 1, -1), lb_im.reshape(nb, 1, -1),
                _bd_out(ws["ssm_c_re"][l]), _bd_out(ws["ssm_c_im"][l]), ws["ssm_d"][l].reshape(1, -1))

    def small(n, l):
        return ws[n][l].reshape(1, -1)

    (gathered,) = _exchange_call("weights_gather_first", [_Exchange("ag", loc[0])])
    W = [None] * DEPTH
    W[0] = _unpack_weights(*gathered)
    saved = []
    xs, xb = x, x.astype(BF)
    for l in range(DEPTH):
        w = W[l]
        proj = _mm(xb, w["w_in"], name="proj_fwd", tm=S, tn=512)
        mats = ssm_mats(l)
        u_il = _interleave(proj[:, :SSM_WIDTH])
        y_raw = _deinterleave(_ssm_fwd(u_il, *mats))
        o_ssm = _ssm_post_fwd(y_raw, proj, w["w_glu"], small("b_glu", l), w["p_ssm"])
        q, k, v = _mla_pre_fwd(proj, small("mla_q_norm", l), small("mla_kv_norm", l), w["w_uq"], w["w_k"], w["w_v"],
                               tc, tsa, tsb)
        nxt = _Exchange("ag", loc[l + 1]) if l + 1 < DEPTH else None
        kt = k.T
        (o_att, lse), gathered = _flash_fwd(q, kt, v, nxt)
        if nxt is not None:
            W[l + 1] = _unpack_weights(*gathered)
        kvm = _mm(mem, w["w_mem_kv"], name="memkv_fwd", out_dtype=BF)
        km, vm = kvm[:, :512], kvm[:, 512:]
        o_mem = _mem_fwd(proj, km, vm, w["p_mem"])
        xn, xnb, pre, merged = _merge_fwd(xs, proj, o_ssm, o_att, o_mem, small("b_gate", l), w["p_mla"], w["w_out"],
                                          small("ln_g", l), small("ln_b", l))
        saved.append(dict(xb=xb, proj=proj, u_il=u_il, y_raw=y_raw, o_ssm=o_ssm, q=q, k=k, kt=kt, v=v, o_att=o_att, lse=lse,
                          km=km, vm=vm, o_mem=o_mem, pre=pre, merged=merged))
        xs, xb = xn, xnb

    dxs, lvec = _loss_head(xs, target)
    loss = lvec[0, 0]

    gs = {n: [None] * DEPTH for n in SMALL}
    disc_ct = [None] * DEPTH
    got = [None] * DEPTH
    pending = None
    for l in reversed(range(DEPTH)):
        sv, w = saved[l], W[l]
        proj = sv["proj"]
        (dx_res, dlg, do_ssm, do_mem, do_att, dz_mla, delta, d_w_out, d_p_mla, d_b_gate, d_ln_g, d_ln_b) = _merge_bwd(
            dxs, sv["pre"], sv["merged"], proj, sv["o_ssm"], sv["o_att"], sv["o_mem"], small("b_gate", l), w["p_mla"],
            w["w_out"], small("ln_g", l))
        dq_mem, dz_mem, d_km, d_vm, d_p_mem = _mem_bwd(do_mem, proj, sv["km"], sv["vm"], w["p_mem"])
        d_w_mem = _mm(mem, jnp.concatenate([d_km, d_vm], axis=1), name="memkv_bwd", ta=True)
        dq, _ = _flash_bwd_dq(sv["q"], sv["k"], sv["kt"], sv["v"].T, do_att, sv["lse"], delta)
        ex = _Exchange("a2a", pending) if pending is not None else None
        dk, dv, arrived = _flash_bwd_dkv(sv["q"], sv["q"].T, sv["k"], sv["v"], do_att, do_att.T, _row_stats(sv["lse"]),
                                         _row_stats(delta), ex)
        if ex is not None:
            got[l + 1] = arrived
        dcq, dckv, dslot, d_wuq, d_wk, d_wv, d_qn, d_kn = _mla_pre_bwd(
            dq, dk, dv, proj, small("mla_q_norm", l), small("mla_kv_norm", l), w["w_uq"], w["w_k"], w["w_v"],
            tc, tsa, tsb)
        dy_raw, dz_ssm, d_w_glu, d_b_glu, d_p_ssm = _ssm_post_bwd(do_ssm, sv["y_raw"], proj, w["w_glu"],
                                                                 small("b_glu", l), w["p_ssm"])
        du_il, dbbre, dbbim, dare, daim, dcre, dcim, dd = _ssm_bwd(sv["u_il"], _interleave(dy_raw), *ssm_mats(l))
        du = _deinterleave(du_il).astype(BF)
        disc_ct[l] = (dare.reshape(SSM_GROUPS, SSM_STATE), daim.reshape(SSM_GROUPS, SSM_STATE), _bd_in_t(dbbre),
                      _bd_in_t(dbbim))
        dproj = jnp.concatenate([du, dz_ssm, dcq, dckv, dslot, dz_mla, dq_mem, dz_mem, dlg], axis=1)
        d_w_in = _mm(sv["xb"], dproj, name="proj_dw", ta=True, tm=1024, tn=512, tk=S)
        dxs = _mm(dproj, w["w_in"], name="proj_dx", tb=True, add=dx_res, tm=1024, tn=1024, tk=1024)
        pending = _pack_grads(dict(w_in=d_w_in, w_glu=d_w_glu, w_uq=d_wuq, w_k=d_wk, w_v=d_wv, w_mem_kv=d_w_mem,
                                   p_ssm=d_p_ssm, p_mla=d_p_mla, p_mem=d_p_mem, w_out=d_w_out))
        gs["b_gate"][l] = d_b_gate.reshape(-1)
        gs["ssm_c_re"][l] = _bd_out_t(dcre)
        gs["ssm_c_im"][l] = _bd_out_t(dcim)
        gs["ssm_d"][l] = dd.reshape(-1)
        gs["b_glu"][l] = d_b_glu.reshape(-1)
        gs["mla_q_norm"][l] = d_qn.reshape(-1)
        gs["mla_kv_norm"][l] = d_kn.reshape(-1)
        gs["ln_g"][l] = d_ln_g.reshape(-1)
        gs["ln_b"][l] = d_ln_b.reshape(-1)

    disc_in = tuple(ws[n] for n in ("ssm_a_re", "ssm_a_im", "ssm_log_dt", "ssm_b_re", "ssm_b_im"))
    _, disc_vjp = jax.vjp(_ssm_discretise, *disc_in)
    d_disc = disc_vjp(tuple(jnp.stack([disc_ct[l][i] for l in range(DEPTH)]) for i in range(4)))
    gsm = {n: jnp.stack(v) for n, v in gs.items() if v[0] is not None}
    for n, g in zip(("ssm_a_re", "ssm_a_im", "ssm_log_dt", "ssm_b_re", "ssm_b_im"), d_disc):
        gsm[n] = g
    got[0], (small_parts,) = _exchange_call(
        "final_exchange", [_Exchange("a2a", pending), _Exchange("ag", [_pack_small(gsm)])])
    return loss, dxs, got, small_parts


def kernel(x, mem, positions, w_in, b_gate, ssm_a_re, ssm_a_im, ssm_log_dt, ssm_b_re, ssm_b_im, ssm_c_re, ssm_c_im, ssm_d, w_glu, b_glu, mla_q_norm, w_uq, mla_kv_norm, w_ukv, w_mem_kv, p_ssm, p_mla, p_mem, w_out, ln_g, ln_b, loss_target, m_w_in, m_b_gate, m_ssm_a_re, m_ssm_a_im, m_ssm_log_dt, m_ssm_b_re, m_ssm_b_im, m_ssm_c_re, m_ssm_c_im, m_ssm_d, m_w_glu, m_b_glu, m_mla_q_norm, m_w_uq, m_mla_kv_norm, m_w_ukv, m_w_mem_kv, m_p_ssm, m_p_mla, m_p_mem, m_w_out, m_ln_g, m_ln_b, v_w_in, v_b_gate, v_ssm_a_re, v_ssm_a_im, v_ssm_log_dt, v_ssm_b_re, v_ssm_b_im, v_ssm_c_re, v_ssm_c_im, v_ssm_d, v_w_glu, v_b_glu, v_mla_q_norm, v_w_uq, v_mla_kv_norm, v_w_ukv, v_w_mem_kv, v_p_ssm, v_p_mla, v_p_mem, v_w_out, v_ln_g, v_ln_b):
    w = dict(w_in=w_in, b_gate=b_gate, ssm_a_re=ssm_a_re, ssm_a_im=ssm_a_im, ssm_log_dt=ssm_log_dt, ssm_b_re=ssm_b_re,
             ssm_b_im=ssm_b_im, ssm_c_re=ssm_c_re, ssm_c_im=ssm_c_im, ssm_d=ssm_d, w_glu=w_glu, b_glu=b_glu,
             mla_q_norm=mla_q_norm, w_uq=w_uq, mla_kv_norm=mla_kv_norm, w_ukv=w_ukv, w_mem_kv=w_mem_kv, p_ssm=p_ssm,
             p_mla=p_mla, p_mem=p_mem, w_out=w_out, ln_g=ln_g, ln_b=ln_b)
    m = dict(w_in=m_w_in, b_gate=m_b_gate, ssm_a_re=m_ssm_a_re, ssm_a_im=m_ssm_a_im, ssm_log_dt=m_ssm_log_dt,
             ssm_b_re=m_ssm_b_re, ssm_b_im=m_ssm_b_im, ssm_c_re=m_ssm_c_re, ssm_c_im=m_ssm_c_im, ssm_d=m_ssm_d,
             w_glu=m_w_glu, b_glu=m_b_glu, mla_q_norm=m_mla_q_norm, w_uq=m_w_uq, mla_kv_norm=m_mla_kv_norm,
             w_ukv=m_w_ukv, w_mem_kv=m_w_mem_kv, p_ssm=m_p_ssm, p_mla=m_p_mla, p_mem=m_p_mem, w_out=m_w_out,
             ln_g=m_ln_g, ln_b=m_ln_b)
    v = dict(w_in=v_w_in, b_gate=v_b_gate, ssm_a_re=v_ssm_a_re, ssm_a_im=v_ssm_a_im, ssm_log_dt=v_ssm_log_dt,
             ssm_b_re=v_ssm_b_re, ssm_b_im=v_ssm_b_im, ssm_c_re=v_ssm_c_re, ssm_c_im=v_ssm_c_im, ssm_d=v_ssm_d,
             w_glu=v_w_glu, b_glu=v_b_glu, mla_q_norm=v_mla_q_norm, w_uq=v_w_uq, mla_kv_norm=v_mla_kv_norm,
             w_ukv=v_w_ukv, w_mem_kv=v_w_mem_kv, p_ssm=v_p_ssm, p_mla=v_p_mla, p_mem=v_p_mem, w_out=v_w_out,
             ln_g=v_ln_g, ln_b=v_ln_b)

    wl = {n: w[n] for n in BIG}
    small = {n: w[n] for n in SMALL}
    loss_local, dx, got, small_parts = _train_step(x[0], mem[0], positions[0], loss_target[0], wl, small)
    loss = lax.psum(loss_local, ("x", "y", "c"))

    grads, delta, new_m, new_v = {}, {}, {}, {}
    wg = _group_buffers(wl, F32)
    mg = _group_buffers({n: m[n] for n in BIG}, F32)
    vg = _group_buffers({n: v[n] for n in BIG}, F32)
    res = []
    for i, (tile, tag) in enumerate(((256, "in"), (128, "col"), (256, "row"))):
        parts = jnp.stack([got[l][i] for l in range(DEPTH)])
        res.append(_adamw_sharded(parts, wg[i], mg[i], vg[i], tile, "adamw_" + tag))
    for dst, j in ((grads, 0), (delta, 1), (new_m, 2), (new_v, 3)):
        dst.update(_ungroup([r[j] for r in res]))

    sw, sm, sv = (_pack_small(small), _pack_small({n: m[n] for n in SMALL}), _pack_small({n: v[n] for n in SMALL}))
    rs = _adamw_sharded(small_parts[None], sw[None], sm[None], sv[None], SMALL_TILE, "adamw_replicated")
    for dst, buf in zip((grads, delta, new_m, new_v), rs):
        dst.update(_unpack_small(buf[0], small))

    return (loss, dx[None], *[grads[n] for n in WEIGHTS], *[delta[n] for n in WEIGHTS],
            *[new_m[n] for n in WEIGHTS], *[new_v[n] for n in WEIGHTS])
```

```python
import math

import jax
import jax.numpy as jnp
from jax import lax
from jax.experimental import pallas as pl
from jax.experimental.pallas import tpu as pltpu

F32 = jnp.float32
BF = jnp.bfloat16

D_MODEL = 1024
DEPTH = 4
N_DEV = 8
SSM_WIDTH = 512
SSM_GROUP = 16
SSM_GROUPS = 32
SSM_STATE = 64
MLA_HEADS = 8
MLA_NOPE = 64
MLA_ROPE = 32
MLA_V = 64
MLA_Q_RANK = 256
MLA_KV_RANK = 128
ROPE_THETA = 10000.0
X_HEADS = 4
X_HEAD_DIM = 128
D_IN = 6048
ALPHA = (2 * DEPTH) ** 0.25
NORM_EPS = 1e-5
ADAM_LR = 0.001
ADAM_B1 = 0.9
ADAM_B2 = 0.999
ADAM_EPS = 1e-08
ADAM_WD = 0.01
ADAM_STEP = 10

LANES = 128
SUBLANES = 8
VMEM_LIMIT = 56 * 1024 * 1024

PW = 6144
ROPE_SLOT_LO = 1408
MLA_SCALE = (MLA_NOPE + MLA_ROPE) ** -0.5
MEM_SCALE = X_HEAD_DIM ** -0.5
NEG = -1e30

T_ROWS = 512
T_ROWS_BWD = 256
T_ATT = 1024
T_MM = 512

MESH = pl.DeviceIdType.MESH


def _cparams(sem):
    return pltpu.CompilerParams(dimension_semantics=sem, vmem_limit_bytes=VMEM_LIMIT)


def _dot(a, b):
    return lax.dot_general(a, b, (((1,), (0,)), ((), ())), preferred_element_type=F32)


def _dot_nt(a, b):
    return lax.dot_general(a, b, (((1,), (1,)), ((), ())), preferred_element_type=F32)


def _dot_tn(a, b):
    return lax.dot_general(a, b, (((0,), (0,)), ((), ())), preferred_element_type=F32)


def _sigmoid(x):
    return 0.5 * jnp.tanh(0.5 * x) + 0.5


def _silu(x):
    return x * _sigmoid(x)


def _dsilu(x):
    s = _sigmoid(x)
    return s * (1.0 + x * (1.0 - s))


_GELU_C = math.sqrt(2.0 / math.pi)


def _gelu(x):
    return 0.5 * x * (1.0 + jnp.tanh(_GELU_C * (x + 0.044715 * x * x * x)))


def _dgelu(x):
    t = jnp.tanh(_GELU_C * (x + 0.044715 * x * x * x))
    return 0.5 * (1.0 + t) + 0.5 * x * (1.0 - t * t) * _GELU_C * (1.0 + 3 * 0.044715 * x * x)


def _rows(tr, w, col=0):
    return pl.BlockSpec((tr, w), lambda i: (i, col))


def _cols(h, tc):
    return pl.BlockSpec((h, tc), lambda i: (0, i))


def _full(shape):
    n = len(shape)
    return pl.BlockSpec(shape, lambda i: (0,) * n)


def _mm(a, b, *, name, ta=False, tb=False, out_dtype=F32, add=None, tm=T_MM, tn=T_MM, tk=1024, exs=None):
    M, K = (a.shape[1], a.shape[0]) if ta else a.shape
    N = b.shape[0] if tb else b.shape[1]
    tm, tn, tk = min(tm, M), min(tn, N), min(tk, K)
    assert M % tm == 0 and N % tn == 0 and K % tk == 0, (M, N, K)
    nk = K // tk
    dn = (((0 if ta else 1,), (1 if tb else 0,)), ((), ()))

    def body(*refs):
        if add is not None:
            a_ref, b_ref, c_ref, o_ref = refs[:4]
        else:
            a_ref, b_ref, o_ref = refs[:3]
        part = lax.dot_general(a_ref[...].astype(BF), b_ref[...].astype(BF), dn, preferred_element_type=F32)
        if nk == 1:
            if add is not None:
                part = part + c_ref[...]
            o_ref[...] = part.astype(out_dtype)
            return
        acc = refs[-1]
        k = pl.program_id(2)

        @pl.when(k == 0)
        def _():
            acc[...] = part

        @pl.when(k != 0)
        def _():
            acc[...] += part

        @pl.when(k == nk - 1)
        def _():
            r = acc[...]
            if add is not None:
                r = r + c_ref[...]
            o_ref[...] = r.astype(out_dtype)

    a_spec = pl.BlockSpec((tk, tm), lambda i, j, k: (k, i)) if ta else pl.BlockSpec((tm, tk), lambda i, j, k: (i, k))
    b_spec = pl.BlockSpec((tn, tk), lambda i, j, k: (j, k)) if tb else pl.BlockSpec((tk, tn), lambda i, j, k: (k, j))
    o_spec = pl.BlockSpec((tm, tn), lambda i, j, k: (i, j))
    in_specs = [a_spec, b_spec] + ([o_spec] if add is not None else [])
    args = (a, b) + ((add,) if add is not None else ())
    (out,), got = _carry_call(
        body, name, (M // tm, N // tn, nk), in_specs, [o_spec], [jax.ShapeDtypeStruct((M, N), out_dtype)],
        [pltpu.VMEM((tm, tn), F32)] if nk > 1 else [], ("parallel", "parallel", "arbitrary"), args, exs)
    return out if exs is None else (out, got)


def _cpow(ar, ai, n):
    rr, ri = None, None
    br, bi = ar, ai
    while n:
        if n & 1:
            if rr is None:
                rr, ri = br, bi
            else:
                rr, ri = rr * br - ri * bi, rr * bi + ri * br
        n >>= 1
        if n:
            br, bi = br * br - bi * bi, 2.0 * br * bi
    return rr, ri


def _seg_shift(v, k, reverse):
    sub = lax.broadcasted_iota(jnp.int32, v.shape, 0)
    if not reverse:
        return jnp.where(sub >= k, pltpu.roll(v, k, 0), 0.0)
    return jnp.where(sub < SUBLANES - k, pltpu.roll(v, SUBLANES - k, 0), 0.0)


def _steps(n, step, init, unroll):
    u = unroll if n % unroll == 0 else 1

    def trip(i, c):
        for s in range(u):
            c = step(i * u + s, c)
        return c

    return lax.fori_loop(0, n // u, trip, init)


def _ssm_scan(hre, him, ar, ai, seglen, reverse, tail=None, tail_init=()):
    w = hre.shape[1]
    zero = jnp.zeros((SUBLANES, w), F32)

    def rows(j):
        jj = (seglen - 1 - j) if reverse else j
        return pl.ds(pl.multiple_of(jj * SUBLANES, SUBLANES), SUBLANES)

    def local(j, c):
        hr, hi = c
        r = rows(j)
        nhr = ar * hr - ai * hi + hre[r, :]
        nhi = ar * hi + ai * hr + him[r, :]
        hre[r, :] = nhr
        him[r, :] = nhi
        return nhr, nhi

    er, ei = _steps(seglen, local, (zero, zero), 4 if reverse else 1)
    pr, pi_ = _cpow(ar, ai, seglen)
    for k in (1, 2, 4):
        sr, si = _seg_shift(er, k, reverse), _seg_shift(ei, k, reverse)
        er, ei = er + pr * sr - pi_ * si, ei + pr * si + pi_ * sr
        pr, pi_ = pr * pr - pi_ * pi_, 2.0 * pr * pi_
    cr, ci = _seg_shift(er, 1, reverse), _seg_shift(ei, 1, reverse)

    def carry_in(j, c):
        tr, ti = c[0] * ar - c[1] * ai, c[0] * ai + c[1] * ar
        r = rows(j)
        fr = hre[r, :] + tr
        fi = him[r, :] + ti
        hre[r, :] = fr
        him[r, :] = fi
        if tail is None:
            return tr, ti
        return (tr, ti) + tuple(tail(j, fr, fi, c[2:]))

    out = _steps(seglen, carry_in, (cr, ci) + tuple(tail_init), 4)
    return out[2:]


SSM_CB = 128
SSM_SB = 256


def _ssm_specs(S):
    u_spec = pl.BlockSpec((S, SSM_CB), lambda g, h: (0, g))
    bb_spec = pl.BlockSpec((1, SSM_CB, SSM_SB), lambda g, h: (g, 0, h))
    a_spec = pl.BlockSpec((1, 1, SSM_SB), lambda g, h: (g, 0, h))
    c_spec = pl.BlockSpec((1, SSM_SB, SSM_CB), lambda g, h: (g, h, 0))
    d_spec = pl.BlockSpec((1, SSM_CB), lambda g, h: (0, g))
    return u_spec, bb_spec, a_spec, c_spec, d_spec


def _ssm_fwd(u, bbre, bbim, are, aim, cre, cim, d):
    S = u.shape[0]
    seglen = S // SUBLANES
    ch = min(512, S)
    nch = S // ch

    def body(u_ref, bbre_ref, bbim_ref, are_ref, aim_ref, cre_ref, cim_ref, d_ref, y_ref, hre, him):
        hf = pl.program_id(1)
        wre = bbre_ref[0].astype(BF)
        wim = bbim_ref[0].astype(BF)

        def mk(c, _):
            r = pl.ds(pl.multiple_of(c * ch, ch), ch)
            ub = u_ref[r, :].astype(BF)
            hre[r, :] = _dot(ub, wre)
            him[r, :] = _dot(ub, wim)
            return 0

        lax.fori_loop(0, nch, mk, 0)
        ar = jnp.broadcast_to(are_ref[0], (SUBLANES, SSM_SB))
        ai = jnp.broadcast_to(aim_ref[0], (SUBLANES, SSM_SB))
        _ssm_scan(hre, him, ar, ai, seglen, False)
        cr = cre_ref[0].astype(BF)
        ci = cim_ref[0].astype(BF)

        def out(c, _):
            r = pl.ds(pl.multiple_of(c * ch, ch), ch)
            y = _dot(hre[r, :].astype(BF), cr) - _dot(him[r, :].astype(BF), ci)

            @pl.when(hf == 0)
            def _():
                y_ref[r, :] = y + d_ref[...] * u_ref[r, :]

            @pl.when(hf != 0)
            def _():
                y_ref[r, :] = y_ref[r, :] + y

            return 0

        lax.fori_loop(0, nch, out, 0)

    u_spec, bb_spec, a_spec, c_spec, d_spec = _ssm_specs(S)
    return pl.pallas_call(
        body, name="ssm_fwd", grid=(SSM_WIDTH // SSM_CB, 2),
        in_specs=[u_spec, bb_spec, bb_spec, a_spec, a_spec, c_spec, c_spec, d_spec], out_specs=u_spec,
        out_shape=jax.ShapeDtypeStruct((S, SSM_WIDTH), F32),
        scratch_shapes=[pltpu.VMEM((S, SSM_SB), F32), pltpu.VMEM((S, SSM_SB), F32)],
        compiler_params=_cparams(("parallel", "arbitrary")),
    )(u, bbre, bbim, are, aim, cre, cim, d)


def _ssm_bwd(u, dy, bbre, bbim, are, aim, cre, cim, d):
    S = u.shape[0]
    seglen = S // SUBLANES
    ch = min(512, S)
    nch = S // ch
    nblk = SSM_WIDTH // SSM_CB

    def body(u_ref, dy_ref, bbre_ref, bbim_ref, are_ref, aim_ref, cre_ref, cim_ref, d_ref,
             du_ref, dbbre_ref, dbbim_ref, dare_ref, daim_ref, dcre_ref, dcim_ref, dd_ref,
             hre, him, lre, lim):
        hf = pl.program_id(1)
        wre = bbre_ref[0].astype(BF)
        wim = bbim_ref[0].astype(BF)
        wre_t, wim_t = wre.T, wim.T
        cr_t = cre_ref[0].astype(BF).T
        ci_t = cim_ref[0].astype(BF).T

        def mk(c, _):
            r = pl.ds(pl.multiple_of(c * ch, ch), ch)
            ub = u_ref[r, :].astype(BF)
            hre[r, :] = _dot(ub, wre)
            him[r, :] = _dot(ub, wim)
            return 0

        lax.fori_loop(0, nch, mk, 0)
        ar = jnp.broadcast_to(are_ref[0], (SUBLANES, SSM_SB))
        ai = jnp.broadcast_to(aim_ref[0], (SUBLANES, SSM_SB))
        _ssm_scan(hre, him, ar, ai, seglen, False)

        dcre_ref[...] = jnp.zeros_like(dcre_ref)
        dcim_ref[...] = jnp.zeros_like(dcim_ref)

        @pl.when(hf == 0)
        def _():
            dd_ref[...] = jnp.zeros_like(dd_ref)

        def cot(c, _):
            r = pl.ds(pl.multiple_of(c * ch, ch), ch)
            dyv = dy_ref[r, :]
            dyb = dyv.astype(BF)
            lre[r, :] = _dot(dyb, cr_t)
            lim[r, :] = -_dot(dyb, ci_t)
            dcre_ref[0] = dcre_ref[0] + _dot_tn(dyb, hre[r, :].astype(BF))
            dcim_ref[0] = dcim_ref[0] - _dot_tn(dyb, him[r, :].astype(BF))

            @pl.when(hf == 0)
            def _():
                dd_ref[...] = dd_ref[...] + jnp.sum(dyv * u_ref[r, :], axis=0, keepdims=True)

            return 0

        lax.fori_loop(0, nch, cot, 0)

        last = pl.ds((seglen - 1) * SUBLANES, SUBLANES)
        pr0 = _seg_shift(hre[last, :], 1, False)
        pi0 = _seg_shift(him[last, :], 1, False)

        def da(j, lr, li, c):
            acr, aci = c
            jp = jnp.maximum(seglen - 2 - j, 0)
            rp = pl.ds(pl.multiple_of(jp * SUBLANES, SUBLANES), SUBLANES)
            inner = j < seglen - 1
            pr = jnp.where(inner, hre[rp, :], pr0)
            pi_ = jnp.where(inner, him[rp, :], pi0)
            return acr + lr * pr + li * pi_, aci + li * pr - lr * pi_

        zero = jnp.zeros((SUBLANES, SSM_SB), F32)
        acr, aci = _ssm_scan(lre, lim, ar, -ai, seglen, True, tail=da, tail_init=(zero, zero))
        dare_ref[0] = jnp.sum(acr, axis=0, keepdims=True)
        daim_ref[0] = jnp.sum(aci, axis=0, keepdims=True)

        dbbre_ref[...] = jnp.zeros_like(dbbre_ref)
        dbbim_ref[...] = jnp.zeros_like(dbbim_ref)

        def fin(c, _):
            r = pl.ds(pl.multiple_of(c * ch, ch), ch)
            lrb = lre[r, :].astype(BF)
            lib = lim[r, :].astype(BF)
            ub = u_ref[r, :].astype(BF)
            du = _dot(lrb, wre_t) + _dot(lib, wim_t)
            dbbre_ref[0] = dbbre_ref[0] + _dot_tn(ub, lrb)
            dbbim_ref[0] = dbbim_ref[0] + _dot_tn(ub, lib)

            @pl.when(hf == 0)
            def _():
                du_ref[r, :] = du + d_ref[...] * dy_ref[r, :]

            @pl.when(hf != 0)
            def _():
                du_ref[r, :] = du_ref[r, :] + du

            return 0

        lax.fori_loop(0, nch, fin, 0)

    u_spec, bb_spec, a_spec, c_spec, d_spec = _ssm_specs(S)
    out_shape = (
        jax.ShapeDtypeStruct((S, SSM_WIDTH), F32),
        jax.ShapeDtypeStruct((nblk, SSM_CB, 2 * SSM_SB), F32), jax.ShapeDtypeStruct((nblk, SSM_CB, 2 * SSM_SB), F32),
        jax.ShapeDtypeStruct((nblk, 1, 2 * SSM_SB), F32), jax.ShapeDtypeStruct((nblk, 1, 2 * SSM_SB), F32),
        jax.ShapeDtypeStruct((nblk, SSM_CB, 2 * SSM_SB), F32), jax.ShapeDtypeStruct((nblk, SSM_CB, 2 * SSM_SB), F32),
        jax.ShapeDtypeStruct((1, SSM_WIDTH), F32),
    )
    return pl.pallas_call(
        body, name="ssm_bwd", grid=(nblk, 2),
        in_specs=[u_spec, u_spec, bb_spec, bb_spec, a_spec, a_spec, c_spec, c_spec, d_spec],
        out_specs=(u_spec, bb_spec, bb_spec, a_spec, a_spec, bb_spec, bb_spec, d_spec),
        out_shape=out_shape,
        scratch_shapes=[pltpu.VMEM((S, SSM_SB), F32) for _ in range(4)],
        compiler_params=_cparams(("parallel", "arbitrary")),
    )(u, dy, bbre, bbim, are, aim, cre, cim, d)


def _ssm_post_fwd(y_raw, proj, w_glu, b_glu, p_ssm):
    S = y_raw.shape[0]
    tr = min(T_ROWS, S)

    def body(y_ref, z_ref, wg_ref, bg_ref, p_ref, o_ref):
        g = _gelu(y_ref[...])
        t = _dot(g.astype(BF), wg_ref[...]) + bg_ref[...]
        glu = t[:, :SSM_WIDTH] * _sigmoid(t[:, SSM_WIDTH:])
        ys = glu * _silu(z_ref[...])
        o_ref[...] = _dot(ys.astype(BF), p_ref[...])

    return pl.pallas_call(
        body, name="ssm_post_fwd", grid=(S // tr,),
        in_specs=[_rows(tr, 512), _rows(tr, 512, 1), _full((512, 1024)), _full((1, 1024)), _full((512, 1024))],
        out_specs=_rows(tr, 1024), out_shape=jax.ShapeDtypeStruct((S, D_MODEL), F32),
        compiler_params=_cparams(("parallel",)),
    )(y_raw, proj, w_glu, b_glu, p_ssm)


def _ssm_post_bwd(do, y_raw, proj, w_glu, b_glu, p_ssm):
    S = y_raw.shape[0]
    tr = min(T_ROWS_BWD, S)

    def body(do_ref, y_ref, z_ref, wg_ref, bg_ref, p_ref, dy_ref, dz_ref, dwg_ref, dbg_ref, dp_ref):
        @pl.when(pl.program_id(0) == 0)
        def _():
            dwg_ref[...] = jnp.zeros_like(dwg_ref)
            dbg_ref[...] = jnp.zeros_like(dbg_ref)
            dp_ref[...] = jnp.zeros_like(dp_ref)

        y = y_ref[...]
        z = z_ref[...]
        g = _gelu(y)
        gb = g.astype(BF)
        t = _dot(gb, wg_ref[...]) + bg_ref[...]
        a = t[:, :SSM_WIDTH]
        sb = _sigmoid(t[:, SSM_WIDTH:])
        glu = a * sb
        ys = glu * _silu(z)
        dob = do_ref[...].astype(BF)
        dys = _dot_nt(dob, p_ref[...])
        dp_ref[...] += _dot_tn(ys.astype(BF), dob)
        dglu = dys * _silu(z)
        dz_ref[...] = (dys * glu * _dsilu(z)).astype(dz_ref.dtype)
        dt = jnp.concatenate([dglu * sb, dglu * a * sb * (1.0 - sb)], axis=1)
        dbg_ref[...] += jnp.sum(dt, axis=0, keepdims=True)
        dtb = dt.astype(BF)
        dg = _dot_nt(dtb, wg_ref[...])
        dwg_ref[...] += _dot_tn(gb, dtb)
        dy_ref[...] = dg * _dgelu(y)

    return pl.pallas_call(
        body, name="ssm_post_bwd", grid=(S // tr,),
        in_specs=[_rows(tr, 1024), _rows(tr, 512), _rows(tr, 512, 1), _full((512, 1024)), _full((1, 1024)),
                  _full((512, 1024))],
        out_specs=(_rows(tr, 512), _rows(tr, 512), _full((512, 1024)), _full((1, 1024)), _full((512, 1024))),
        out_shape=(jax.ShapeDtypeStruct((S, 512), F32), jax.ShapeDtypeStruct((S, 512), BF),
                   jax.ShapeDtypeStruct((512, 1024), F32), jax.ShapeDtypeStruct((1, 1024), F32),
                   jax.ShapeDtypeStruct((512, 1024), F32)),
        compiler_params=_cparams(("arbitrary",)),
    )(do, y_raw, proj, w_glu, b_glu, p_ssm)


def _rope(t, c, sa, sb):
    return t * c + pltpu.roll(t, LANES - 16, 1) * sa + pltpu.roll(t, 16, 1) * sb


def _rope_t(dy, c, sa, sb):
    return dy * c + pltpu.roll(dy * sa, 16, 1) + pltpu.roll(dy * sb, LANES - 16, 1)


def _rms(x, g):
    r = lax.rsqrt(jnp.mean(x * x, axis=-1, keepdims=True) + NORM_EPS)
    return x * r * g, r


def _mla_pre_fwd(proj, q_norm, kv_norm, wuq, wk, wv, tc, tsa, tsb):
    S = proj.shape[0]
    tr = min(T_ROWS, S)

    def body(cq_ref, ckv_ref, slot_ref, qn_ref, kn_ref, wuq_ref, wk_ref, wv_ref, c_ref, sa_ref, sb_ref,
             q_out, k_out, v_out, qt_out, kt_out, vt_out):
        c, sa, sb = c_ref[...], sa_ref[...], sb_ref[...]
        qn, _ = _rms(cq_ref[...], qn_ref[...])
        q = _dot(qn.astype(BF), wuq_ref[...]) * MLA_SCALE
        kn, _ = _rms(ckv_ref[...], kn_ref[...])
        knb = kn.astype(BF)
        kp = _dot(knb, wk_ref[...])
        v = _dot(knb, wv_ref[...]).astype(BF)
        v_out[...] = v
        vt_out[...] = v.T
        kr = _rope(slot_ref[...], c, sa, sb)
        for h in range(MLA_HEADS):
            cs = slice(h * LANES, (h + 1) * LANES)
            qh = _rope(q[:, cs], c, sa, sb).astype(BF)
            kh = (kp[:, cs] + kr).astype(BF)
            q_out[:, cs] = qh
            k_out[:, cs] = kh
            qt_out[cs, :] = qh.T
            kt_out[cs, :] = kh.T

    return pl.pallas_call(
        body, name="mla_pre_fwd", grid=(S // tr,),
        in_specs=[_rows(tr, 256, 4), _rows(tr, 128, 10), _rows(tr, 128, 11), _full((1, 256)), _full((1, 128)),
                  _full((256, 1024)), _full((128, 1024)), _full((128, 512)),
                  _rows(tr, 128), _rows(tr, 128), _rows(tr, 128)],
        out_specs=(_rows(tr, 1024), _rows(tr, 1024), _rows(tr, 512), _cols(1024, tr), _cols(1024, tr), _cols(512, tr)),
        out_shape=(jax.ShapeDtypeStruct((S, 1024), BF), jax.ShapeDtypeStruct((S, 1024), BF),
                   jax.ShapeDtypeStruct((S, 512), BF), jax.ShapeDtypeStruct((1024, S), BF),
                   jax.ShapeDtypeStruct((1024, S), BF), jax.ShapeDtypeStruct((512, S), BF)),
        compiler_params=_cparams(("parallel",)),
    )(proj, proj, proj, q_norm, kv_norm, wuq, wk, wv, tc, tsa, tsb)


def _mla_pre_bwd(dq, dk, dv, proj, q_norm, kv_norm, wuq, wk, wv, tc, tsa, tsb):
    S = proj.shape[0]
    tr = min(T_ROWS_BWD, S)

    def body(dq_ref, dk_ref, dv_ref, cq_ref, ckv_ref, qn_ref, kn_ref, wuq_ref, wk_ref, wv_ref, c_ref, sa_ref, sb_ref,
             dcq_ref, dckv_ref, dslot_ref, dwuq_ref, dwk_ref, dwv_ref, dqn_ref, dkn_ref, dqp):
        @pl.when(pl.program_id(0) == 0)
        def _():
            dwuq_ref[...] = jnp.zeros_like(dwuq_ref)
            dwk_ref[...] = jnp.zeros_like(dwk_ref)
            dwv_ref[...] = jnp.zeros_like(dwv_ref)
            dqn_ref[...] = jnp.zeros_like(dqn_ref)
            dkn_ref[...] = jnp.zeros_like(dkn_ref)

        c, sa, sb = c_ref[...], sa_ref[...], sb_ref[...]
        dkr = jnp.zeros((tr, LANES), F32)
        for h in range(MLA_HEADS):
            cs = slice(h * LANES, (h + 1) * LANES)
            dqp[:, cs] = (_rope_t(dq_ref[:, cs], c, sa, sb) * MLA_SCALE).astype(BF)
            dkr = dkr + dk_ref[:, cs]
        lane = lax.broadcasted_iota(jnp.int32, (tr, LANES), 1)
        in_rope = (lane >= MLA_NOPE) & (lane < MLA_NOPE + MLA_ROPE)
        dslot_ref[...] = jnp.where(in_rope, _rope_t(dkr, c, sa, sb), 0.0).astype(dslot_ref.dtype)

        cq = cq_ref[...]
        gq = qn_ref[...]
        qn, rq = _rms(cq, gq)
        dqpb = dqp[...]
        dwuq_ref[...] += _dot_tn(qn.astype(BF), dqpb)
        dqn = _dot_nt(dqpb, wuq_ref[...])
        dqn_ref[...] += jnp.sum(dqn * cq * rq, axis=0, keepdims=True)
        dyg = dqn * gq
        dcq_ref[...] = (rq * dyg - cq * (rq * rq * rq) * jnp.mean(dyg * cq, axis=-1, keepdims=True)).astype(dcq_ref.dtype)

        ckv = ckv_ref[...]
        gk = kn_ref[...]
        kn, rk = _rms(ckv, gk)
        knb = kn.astype(BF)
        dkb = dk_ref[...].astype(BF)
        dvb = dv_ref[...].astype(BF)
        dwk_ref[...] += _dot_tn(knb, dkb)
        dwv_ref[...] += _dot_tn(knb, dvb)
        dkn = _dot_nt(dkb, wk_ref[...]) + _dot_nt(dvb, wv_ref[...])
        dkn_ref[...] += jnp.sum(dkn * ckv * rk, axis=0, keepdims=True)
        dyk = dkn * gk
        dckv_ref[...] = (rk * dyk - ckv * (rk * rk * rk) * jnp.mean(dyk * ckv, axis=-1, keepdims=True)).astype(dckv_ref.dtype)

    return pl.pallas_call(
        body, name="mla_pre_bwd", grid=(S // tr,),
        in_specs=[_rows(tr, 1024), _rows(tr, 1024), _rows(tr, 512), _rows(tr, 256, 4), _rows(tr, 128, 10),
                  _full((1, 256)), _full((1, 128)), _full((256, 1024)), _full((128, 1024)), _full((128, 512)),
                  _rows(tr, 128), _rows(tr, 128), _rows(tr, 128)],
        out_specs=(_rows(tr, 256), _rows(tr, 128), _rows(tr, 128), _full((256, 1024)), _full((128, 1024)),
                   _full((128, 512)), _full((1, 256)), _full((1, 128))),
        out_shape=(jax.ShapeDtypeStruct((S, 256), BF), jax.ShapeDtypeStruct((S, 128), BF),
                   jax.ShapeDtypeStruct((S, 128), BF), jax.ShapeDtypeStruct((256, 1024), F32),
                   jax.ShapeDtypeStruct((128, 1024), F32), jax.ShapeDtypeStruct((128, 512), F32),
                   jax.ShapeDtypeStruct((1, 256), F32), jax.ShapeDtypeStruct((1, 128), F32)),
        scratch_shapes=[pltpu.VMEM((tr, 1024), BF)],
        compiler_params=_cparams(("arbitrary",)),
    )(dq, dk, dv, proj, proj, q_norm, kv_norm, wuq, wk, wv, tc, tsa, tsb)


ANY = pl.BlockSpec(memory_space=pl.ANY)
N_REL = N_DEV - 1


def _coords():
    return lax.axis_index("x"), lax.axis_index("y"), lax.axis_index("c")


def _sem_shapes(nbuf):
    return [pltpu.SemaphoreType.DMA((N_REL * nbuf,)), pltpu.SemaphoreType.DMA((N_REL * nbuf,)),
            pltpu.SemaphoreType.DMA((nbuf,))]


def _ag_plan(srcs, dsts, sems):
    send_sems, recv_sems, _ = sems
    plan = []
    for b, (src, dst) in enumerate(zip(srcs, dsts)):
        def slot(px, py, pc, dst=dst):
            return dst.at[4 * px + 2 * py + pc]

        def copy(k, blk, to, s=None, b=b, slot=slot):
            return pltpu.make_async_remote_copy(
                src_ref=slot(*blk) if s is None else s, dst_ref=slot(*blk), send_sem=send_sems.at[N_REL * b + k],
                recv_sem=recv_sems.at[N_REL * b + k], device_id=to, device_id_type=MESH)

        plan.append((b, src, slot, copy))
    return plan


def _ag_start(srcs, dsts, sems):
    x, y, c = _coords()
    chips = [(1 - x, y), (x, 1 - y), (1 - x, 1 - y)]
    for b, src, slot, copy in _ag_plan(srcs, dsts, sems):
        pltpu.make_async_copy(src, slot(x, y, c), sems[2].at[b]).start()
        copy(0, (x, y, c), (x, y, 1 - c), src).start()
        for j, chip in enumerate(chips):
            copy(1 + j, (x, y, c), (*chip, c), src).start()


def _ag_finish(srcs, dsts, sems):
    x, y, c = _coords()
    me, sibling = (x, y, c), (x, y, 1 - c)
    chips = [(1 - x, y), (x, 1 - y), (1 - x, 1 - y)]
    plan = _ag_plan(srcs, dsts, sems)
    for b, src, slot, copy in plan:
        for j, chip in enumerate(chips):
            copy(1 + j, (*chip, c), me).wait_recv()
            copy(4 + j, (*chip, c), sibling).start()
    for b, src, slot, copy in plan:
        copy(0, sibling, me).wait_recv()
        for j, chip in enumerate(chips):
            copy(4 + j, (*chip, 1 - c), me).wait_recv()
        copy(0, me, sibling, src).wait_send()
        for j, chip in enumerate(chips):
            copy(1 + j, me, (*chip, c), src).wait_send()
            copy(4 + j, (*chip, c), sibling).wait_send()
        pltpu.make_async_copy(src, slot(*me), sems[2].at[b]).wait()


def _a2a_copies(srcs, dsts, sems):
    send_sems, recv_sems, local_sems = sems
    x, y, c = _coords()
    me = 4 * x + 2 * y + c
    local, remote = [], []
    for b, (src, dst) in enumerate(zip(srcs, dsts)):
        for rel in range(1, N_DEV):
            px = 1 - x if rel & 4 else x
            py = 1 - y if rel & 2 else y
            pc = 1 - c if rel & 1 else c
            remote.append(pltpu.make_async_remote_copy(
                src_ref=src.at[4 * px + 2 * py + pc], dst_ref=dst.at[me], send_sem=send_sems.at[N_REL * b + rel - 1],
                recv_sem=recv_sems.at[N_REL * b + rel - 1], device_id=(px, py, pc), device_id_type=MESH))
        local.append(pltpu.make_async_copy(src.at[me], dst.at[me], local_sems.at[b]))
    return local, remote


def _a2a_start(srcs, dsts, sems):
    local, remote = _a2a_copies(srcs, dsts, sems)
    for d in local + remote:
        d.start()


def _a2a_finish(srcs, dsts, sems):
    local, remote = _a2a_copies(srcs, dsts, sems)
    for d in remote + local:
        d.wait()


class _Exchange:
    def __init__(self, kind, srcs):
        self.kind, self.srcs = kind, list(srcs)
        self.n = len(self.srcs)

    def out_shapes(self):
        if self.kind == "ag":
            return [jax.ShapeDtypeStruct((N_DEV,) + s.shape, s.dtype) for s in self.srcs]
        return [jax.ShapeDtypeStruct(s.shape, s.dtype) for s in self.srcs]

    def start(self, src_refs, dst_refs, sems):
        (_ag_start if self.kind == "ag" else _a2a_start)(src_refs, dst_refs, sems)

    def finish(self, src_refs, dst_refs, sems):
        (_ag_finish if self.kind == "ag" else _a2a_finish)(src_refs, dst_refs, sems)


def _carry_call(body, name, grid, in_specs, out_specs, out_shape, scratch, semantics, args, exs):
    in_specs, out_specs, out_shape, scratch = list(in_specs), list(out_specs), list(out_shape), list(scratch)
    if not exs:
        return pl.pallas_call(body, name=name, grid=grid, in_specs=in_specs, out_specs=out_specs, out_shape=out_shape,
                              scratch_shapes=scratch, compiler_params=_cparams(semantics))(*args), []
    n_in, n_out, n_scr = len(in_specs), len(out_specs), len(scratch)
    n_ex = sum(e.n for e in exs)

    def wrapped(*refs):
        ins, refs = refs[:n_in], refs[n_in:]
        srcs, refs = refs[:n_ex], refs[n_ex:]
        outs, refs = refs[:n_out], refs[n_out:]
        dsts, refs = refs[:n_ex], refs[n_ex:]
        scr, sems = refs[:n_scr], refs[n_scr:]
        views, off = [], 0
        for i, e in enumerate(exs):
            views.append((srcs[off:off + e.n], dsts[off:off + e.n], sems[3 * i:3 * i + 3]))
            off += e.n
        first = last = None
        for axis, size in enumerate(grid):
            at0, at1 = pl.program_id(axis) == 0, pl.program_id(axis) == size - 1
            first = at0 if first is None else first & at0
            last = at1 if last is None else last & at1

        @pl.when(first)
        def _():
            for e, view in zip(exs, views):
                e.start(*view)

        body(*ins, *outs, *scr)

        @pl.when(last)
        def _():
            for e, view in zip(exs, views):
                e.finish(*view)

    res = pl.pallas_call(
        wrapped, name=name + "_x", grid=grid, in_specs=in_specs + [ANY] * n_ex, out_specs=out_specs + [ANY] * n_ex,
        out_shape=out_shape + [s for e in exs for s in e.out_shapes()],
        scratch_shapes=scratch + [s for e in exs for s in _sem_shapes(e.n)],
        compiler_params=_cparams(("arbitrary",) * len(grid)))(*args, *[s for e in exs for s in e.srcs])
    got, off = [], n_out
    for e in exs:
        got.append(list(res[off:off + e.n]))
        off += e.n
    return res[:n_out], got


def _exchange_call(name, exs):
    tot = sum(e.n for e in exs)

    def body(*refs):
        srcs, dsts, sems = refs[:tot], refs[tot:2 * tot], refs[2 * tot:]
        views, off = [], 0
        for i, e in enumerate(exs):
            views.append((srcs[off:off + e.n], dsts[off:off + e.n], sems[3 * i:3 * i + 3]))
            off += e.n
        for e, view in zip(exs, views):
            e.start(*view)
        for e, view in zip(exs, views):
            e.finish(*view)

    outs = pl.pallas_call(
        body, name=name, in_specs=[ANY] * tot, out_specs=[ANY] * tot,
        out_shape=[s for e in exs for s in e.out_shapes()],
        scratch_shapes=[s for e in exs for s in _sem_shapes(e.n)],
    )(*[s for e in exs for s in e.srcs])
    res, off = [], 0
    for e in exs:
        res.append(list(outs[off:off + e.n]))
        off += e.n
    return res


def _flash_call(body, name, exs, in_specs, out_specs, out_shape, scratch, n, args):
    return _carry_call(body, name, (MLA_HEADS // 2, n, n), in_specs, out_specs, out_shape, scratch,
                       ("parallel", "parallel", "arbitrary"), args, exs)


def _pair_rows(a):
    at = a.T
    return jnp.concatenate([at[0:1, :], at[MLA_V:MLA_V + 1, :], jnp.zeros((SUBLANES - 2, a.shape[0]), a.dtype)], axis=0)


def _lower_tri(t):
    return lax.broadcasted_iota(jnp.int32, (t, t), 0) >= lax.broadcasted_iota(jnp.int32, (t, t), 1)


def _upper_tri(t):
    return lax.broadcasted_iota(jnp.int32, (t, t), 1) >= lax.broadcasted_iota(jnp.int32, (t, t), 0)


def _flash_fwd(q, kt, v, exs=()):
    S = q.shape[0]
    t = min(T_ATT, S)
    n = S // t

    def body(q_ref, kt_ref, v_ref, o_ref, lse_ref, lse_t_ref, m_s, l_s, acc):
        qi, ki = pl.program_id(1), pl.program_id(2)
        lo = lax.broadcasted_iota(jnp.int32, (t, LANES), 1) < MLA_V

        @pl.when(ki == 0)
        def _():
            m_s[...] = jnp.full_like(m_s, NEG)
            l_s[...] = jnp.zeros_like(l_s)
            acc[...] = jnp.zeros_like(acc)

        @pl.when(ki <= qi)
        def _():
            keep = _lower_tri(t) | (ki < qi)
            vv = v_ref[...]
            heads = range(2)
            ss = [jnp.where(keep, _dot(q_ref[:, h * LANES:(h + 1) * LANES], kt_ref[h * LANES:(h + 1) * LANES, :]), NEG)
                  for h in heads]
            m_prev = [m_s[h] for h in heads]
            l_prev = [l_s[h] for h in heads]
            m_new = [jnp.maximum(m_prev[h], jnp.max(ss[h], axis=1, keepdims=True)) for h in heads]
            al = [jnp.exp(m_prev[h] - m_new[h]) for h in heads]
            ps = [jnp.exp(ss[h] - m_new[h][:, :1]) for h in heads]
            l_new = [al[h] * l_prev[h] + jnp.sum(ps[h], axis=1, keepdims=True) for h in heads]
            pv = [_dot(ps[h].astype(BF), vv) for h in heads]
            for h in heads:
                m_s[h] = m_new[h]
                l_s[h] = l_new[h]
            acc[...] = jnp.where(lo, al[0], al[1]) * acc[...] + jnp.where(lo, pv[0], pv[1])

        @pl.when(ki == qi)
        def _():
            o_ref[...] = acc[...] / jnp.where(lo, l_s[0], l_s[1])
            lse = jnp.where(lo, m_s[0] + jnp.log(l_s[0]), m_s[1] + jnp.log(l_s[1]))
            lse_ref[0] = lse
            lse_t_ref[0] = _pair_rows(lse)

    return _flash_call(
        body, "mla_flash_fwd", exs,
        [pl.BlockSpec((t, 256), lambda p, i, j: (i, p)),
         pl.BlockSpec((256, t), lambda p, i, j: (p, jnp.minimum(j, i))),
         pl.BlockSpec((t, 128), lambda p, i, j: (jnp.minimum(j, i), p))],
        [pl.BlockSpec((t, 128), lambda p, i, j: (i, p)), pl.BlockSpec((1, t, 128), lambda p, i, j: (p, i, 0)),
         pl.BlockSpec((1, SUBLANES, t), lambda p, i, j: (p, 0, i))],
        [jax.ShapeDtypeStruct((S, 512), F32), jax.ShapeDtypeStruct((MLA_HEADS // 2, S, 128), F32),
         jax.ShapeDtypeStruct((MLA_HEADS // 2, SUBLANES, S), F32)],
        [pltpu.VMEM((2, t, 128), F32), pltpu.VMEM((2, t, 128), F32), pltpu.VMEM((t, 128), F32)], n, (q, kt, v))


def _flash_bwd_dq(q, k, kt, vt, do, lse, delta, exs=()):
    S = q.shape[0]
    t = min(T_ATT, S)
    n = S // t

    def body(q_ref, k_ref, kt_ref, vt_ref, do_ref, lse_ref, dl_ref, dq_ref, acc):
        qi, ki = pl.program_id(1), pl.program_id(2)
        lo = lax.broadcasted_iota(jnp.int32, (t, LANES), 1) < MLA_V

        @pl.when(ki == 0)
        def _():
            acc[...] = jnp.zeros_like(acc)

        @pl.when(ki <= qi)
        def _():
            keep = _lower_tri(t) | (ki < qi)
            heads = range(2)
            cs = [slice(h * LANES, (h + 1) * LANES) for h in heads]
            col = [slice(h * MLA_V, h * MLA_V + 1) for h in heads]
            lse, dl, dov, vt = lse_ref[0], dl_ref[0], do_ref[...], vt_ref[...]
            ss = [jnp.where(keep, _dot(q_ref[:, cs[h]], kt_ref[cs[h], :]), NEG) for h in heads]
            dp = [_dot(jnp.where(lo if h == 0 else ~lo, dov, 0).astype(BF), vt) for h in heads]
            ds = [(jnp.exp(ss[h] - lse[:, col[h]]) * (dp[h] - dl[:, col[h]])).astype(BF) for h in heads]
            dq = [_dot(ds[h], k_ref[:, cs[h]]) for h in heads]
            acc[...] += jnp.concatenate(dq, axis=1)

        @pl.when(ki == qi)
        def _():
            dq_ref[...] = acc[...]

    (dq,), got = _flash_call(
        body, "mla_flash_dq", exs,
        [pl.BlockSpec((t, 256), lambda p, i, j: (i, p)),
         pl.BlockSpec((t, 256), lambda p, i, j: (jnp.minimum(j, i), p)),
         pl.BlockSpec((256, t), lambda p, i, j: (p, jnp.minimum(j, i))),
         pl.BlockSpec((128, t), lambda p, i, j: (p, jnp.minimum(j, i))),
         pl.BlockSpec((t, 128), lambda p, i, j: (i, p)),
         pl.BlockSpec((1, t, 128), lambda p, i, j: (p, i, 0)),
         pl.BlockSpec((1, t, 128), lambda p, i, j: (p, i, 0))],
        [pl.BlockSpec((t, 256), lambda p, i, j: (i, p))],
        [jax.ShapeDtypeStruct((S, 1024), F32)],
        [pltpu.VMEM((t, 256), F32)], n, (q, k, kt, vt, do, lse, delta))
    return dq, got


def _flash_bwd_dkv(q, qt, k, v, do, dot_, lse_t, delta_t, exs=()):
    S = q.shape[0]
    t = min(T_ATT, S)
    n = S // t

    def body(q_ref, qt_ref, k_ref, v_ref, do_ref, dot_ref, lse_ref, dl_ref, dk_ref, dv_ref, dk_acc, dv_acc):
        ki, qi = pl.program_id(1), pl.program_id(2)
        lo = lax.broadcasted_iota(jnp.int32, (t, LANES), 1) < MLA_V
        top = lax.broadcasted_iota(jnp.int32, (LANES, t), 0) < MLA_V

        @pl.when(qi == 0)
        def _():
            dk_acc[...] = jnp.zeros_like(dk_acc)
            dv_acc[...] = jnp.zeros_like(dv_acc)

        @pl.when(qi >= ki)
        def _():
            keep = _upper_tri(t) | (qi > ki)
            heads = range(2)
            cs = [slice(h * LANES, (h + 1) * LANES) for h in heads]
            vv, lse, dl, dov, dot_v = v_ref[...], lse_ref[0], dl_ref[0], do_ref[...], dot_ref[...]
            st = [jnp.where(keep, _dot(k_ref[:, cs[h]], qt_ref[cs[h], :]), NEG) for h in heads]
            dpt = [_dot(vv, jnp.where(top if h == 0 else ~top, dot_v, 0).astype(BF)) for h in heads]
            pt = [jnp.exp(st[h] - lse[h:h + 1, :]) for h in heads]
            dst = [(pt[h] * (dpt[h] - dl[h:h + 1, :])).astype(BF) for h in heads]
            dv = [_dot(pt[h].astype(BF), jnp.where(lo if h == 0 else ~lo, dov, 0).astype(BF)) for h in heads]
            dk = [_dot(dst[h], q_ref[:, cs[h]]) for h in heads]
            dv_acc[...] += dv[0] + dv[1]
            dk_acc[...] += jnp.concatenate(dk, axis=1)

        @pl.when(qi == n - 1)
        def _():
            dk_ref[...] = dk_acc[...]
            dv_ref[...] = dv_acc[...]

    (dk, dv), got = _flash_call(
        body, "mla_flash_dkv", exs,
        [pl.BlockSpec((t, 256), lambda p, j, i: (jnp.maximum(i, j), p)),
         pl.BlockSpec((256, t), lambda p, j, i: (p, jnp.maximum(i, j))),
         pl.BlockSpec((t, 256), lambda p, j, i: (j, p)),
         pl.BlockSpec((t, 128), lambda p, j, i: (j, p)),
         pl.BlockSpec((t, 128), lambda p, j, i: (jnp.maximum(i, j), p)),
         pl.BlockSpec((128, t), lambda p, j, i: (p, jnp.maximum(i, j))),
         pl.BlockSpec((1, SUBLANES, t), lambda p, j, i: (p, 0, jnp.maximum(i, j))),
         pl.BlockSpec((1, SUBLANES, t), lambda p, j, i: (p, 0, jnp.maximum(i, j)))],
        [pl.BlockSpec((t, 256), lambda p, j, i: (j, p)), pl.BlockSpec((t, 128), lambda p, j, i: (j, p))],
        [jax.ShapeDtypeStruct((S, 1024), F32), jax.ShapeDtypeStruct((S, 512), F32)],
        [pltpu.VMEM((t, 256), F32), pltpu.VMEM((t, 128), F32)], n, (q, qt, k, v, do, dot_, lse_t, delta_t))
    return dk, dv, got


def _mem_heads(qm, km_ref, vm_ref):
    ps, os_ = [], []
    for h in range(X_HEADS):
        cs = slice(h * X_HEAD_DIM, (h + 1) * X_HEAD_DIM)
        s = _dot_nt(qm[:, cs].astype(BF), km_ref[:, cs]) * MEM_SCALE
        e = jnp.exp(s - jnp.max(s, axis=1, keepdims=True))
        p = e / jnp.sum(e, axis=1, keepdims=True)
        ps.append(p)
        os_.append(_dot(p.astype(BF), vm_ref[:, cs]))
    return ps, jnp.concatenate(os_, axis=1)


def _mem_fwd(proj, km, vm, p_mem):
    S = proj.shape[0]
    tr = min(T_ROWS, S)
    M = km.shape[0]

    def body(q_ref, z_ref, km_ref, vm_ref, p_ref, o_ref):
        _, o = _mem_heads(q_ref[...], km_ref, vm_ref)
        y = o * _silu(z_ref[...])
        o_ref[...] = _dot(y.astype(BF), p_ref[...])

    return pl.pallas_call(
        body, name="mem_fwd", grid=(S // tr,),
        in_specs=[_rows(tr, 512, 4), _rows(tr, 512, 5), _full((M, 512)), _full((M, 512)), _full((512, 1024))],
        out_specs=_rows(tr, 1024), out_shape=jax.ShapeDtypeStruct((S, D_MODEL), F32),
        compiler_params=_cparams(("parallel",)),
    )(proj, proj, km, vm, p_mem)


def _mem_bwd(do, proj, km, vm, p_mem):
    S = proj.shape[0]
    tr = min(T_ROWS_BWD, S)
    M = km.shape[0]

    def body(do_ref, q_ref, z_ref, km_ref, vm_ref, p_ref, dq_ref, dz_ref, dkm_ref, dvm_ref, dp_ref):
        @pl.when(pl.program_id(0) == 0)
        def _():
            dkm_ref[...] = jnp.zeros_like(dkm_ref)
            dvm_ref[...] = jnp.zeros_like(dvm_ref)
            dp_ref[...] = jnp.zeros_like(dp_ref)

        qm = q_ref[...]
        z = z_ref[...]
        ps, o = _mem_heads(qm, km_ref, vm_ref)
        sz = _silu(z)
        y = o * sz
        dob = do_ref[...].astype(BF)
        dy = _dot_nt(dob, p_ref[...])
        dp_ref[...] += _dot_tn(y.astype(BF), dob)
        dz_ref[...] = (dy * o * _dsilu(z)).astype(dz_ref.dtype)
        d_o = dy * sz
        for h in range(X_HEADS):
            cs = slice(h * X_HEAD_DIM, (h + 1) * X_HEAD_DIM)
            doh = d_o[:, cs]
            dohb = doh.astype(BF)
            p = ps[h]
            dpr = _dot_nt(dohb, vm_ref[:, cs])
            ds = (p * (dpr - jnp.sum(doh * o[:, cs], axis=1, keepdims=True)) * MEM_SCALE).astype(BF)
            dq_ref[:, cs] = _dot(ds, km_ref[:, cs]).astype(dq_ref.dtype)
            dkm_ref[:, cs] += _dot_tn(ds, qm[:, cs].astype(BF))
            dvm_ref[:, cs] += _dot_tn(p.astype(BF), dohb)

    return pl.pallas_call(
        body, name="mem_bwd", grid=(S // tr,),
        in_specs=[_rows(tr, 1024), _rows(tr, 512, 4), _rows(tr, 512, 5), _full((M, 512)), _full((M, 512)),
                  _full((512, 1024))],
        out_specs=(_rows(tr, 512), _rows(tr, 512), _full((M, 512)), _full((M, 512)), _full((512, 1024))),
        out_shape=(jax.ShapeDtypeStruct((S, 512), BF), jax.ShapeDtypeStruct((S, 512), BF),
                   jax.ShapeDtypeStruct((M, 512), F32), jax.ShapeDtypeStruct((M, 512), F32),
                   jax.ShapeDtypeStruct((512, 1024), F32)),
        compiler_params=_cparams(("arbitrary",)),
    )(do, proj, proj, km, vm, p_mem)


def _merge_fwd(x, proj, o_ssm, o_att, o_mem, b_gate, p_mla, w_out, ln_g, ln_b):
    S = x.shape[0]
    tr = min(T_ROWS, S)

    def body(x_ref, lg_ref, z_ref, os_ref, oa_ref, om_ref, bg_ref, p_ref, w_ref, g_ref, b_ref,
             xn_ref, xb_ref, pre_ref, mg_ref):
        gates = _sigmoid(lg_ref[...] + bg_ref[...])
        ya = oa_ref[...] * _silu(z_ref[...])
        o_mla = _dot(ya.astype(BF), p_ref[...])
        merged = (gates[:, :D_MODEL] * os_ref[...] + gates[:, D_MODEL:2 * D_MODEL] * o_mla
                  + gates[:, 2 * D_MODEL:] * om_ref[...])
        mb = merged.astype(BF)
        mg_ref[...] = mb
        pre = ALPHA * x_ref[...] + _dot(mb, w_ref[...])
        pre_ref[...] = pre
        mu = jnp.mean(pre, axis=-1, keepdims=True)
        xc = pre - mu
        var = jnp.mean(xc * xc, axis=-1, keepdims=True)
        xn = xc * lax.rsqrt(var + NORM_EPS) * g_ref[...] + b_ref[...]
        xn_ref[...] = xn
        xb_ref[...] = xn.astype(BF)

    return pl.pallas_call(
        body, name="merge_fwd", grid=(S // tr,),
        in_specs=[_rows(tr, 1024), _rows(tr, 3072, 1), _rows(tr, 512, 3), _rows(tr, 1024), _rows(tr, 512),
                  _rows(tr, 1024), _full((1, 3072)), _full((512, 1024)), _full((1024, 1024)), _full((1, 1024)),
                  _full((1, 1024))],
        out_specs=(_rows(tr, 1024), _rows(tr, 1024), _rows(tr, 1024), _rows(tr, 1024)),
        out_shape=(jax.ShapeDtypeStruct((S, 1024), F32), jax.ShapeDtypeStruct((S, 1024), BF),
                   jax.ShapeDtypeStruct((S, 1024), F32), jax.ShapeDtypeStruct((S, 1024), BF)),
        compiler_params=_cparams(("parallel",)),
    )(x, proj, proj, o_ssm, o_att, o_mem, b_gate, p_mla, w_out, ln_g, ln_b)


def _merge_bwd(dxn, pre, merged, proj, o_ssm, o_att, o_mem, b_gate, p_mla, w_out, ln_g):
    S = pre.shape[0]
    tr = min(T_ROWS_BWD, S)

    def body(dxn_ref, pre_ref, mg_ref, lg_ref, z_ref, os_ref, oa_ref, om_ref, bg_ref, p_ref, w_ref, g_ref,
             dxr_ref, dlg_ref, dos_ref, dom_ref, doa_ref, dz_ref, doat_ref, dl_ref, dlt_ref, dw_ref, dp_ref, dbg_ref,
             dg_ref, db_ref):
        @pl.when(pl.program_id(0) == 0)
        def _():
            dw_ref[...] = jnp.zeros_like(dw_ref)
            dp_ref[...] = jnp.zeros_like(dp_ref)
            dbg_ref[...] = jnp.zeros_like(dbg_ref)
            dg_ref[...] = jnp.zeros_like(dg_ref)
            db_ref[...] = jnp.zeros_like(db_ref)

        dxn = dxn_ref[...]
        pre = pre_ref[...]
        mu = jnp.mean(pre, axis=-1, keepdims=True)
        xc = pre - mu
        rstd = lax.rsqrt(jnp.mean(xc * xc, axis=-1, keepdims=True) + NORM_EPS)
        xhat = xc * rstd
        dg_ref[...] += jnp.sum(dxn * xhat, axis=0, keepdims=True)
        db_ref[...] += jnp.sum(dxn, axis=0, keepdims=True)
        dxh = dxn * g_ref[...]
        dpre = rstd * (dxh - jnp.mean(dxh, axis=-1, keepdims=True)
                       - xhat * jnp.mean(dxh * xhat, axis=-1, keepdims=True))
        dxr_ref[...] = ALPHA * dpre
        dpb = dpre.astype(BF)
        dw_ref[...] += _dot_tn(mg_ref[...], dpb)
        dm = _dot_nt(dpb, w_ref[...])

        gates = _sigmoid(lg_ref[...] + bg_ref[...])
        g0, g1, g2 = gates[:, :D_MODEL], gates[:, D_MODEL:2 * D_MODEL], gates[:, 2 * D_MODEL:]
        z = z_ref[...]
        oa = oa_ref[...]
        sz = _silu(z)
        ya = (oa * sz).astype(BF)
        o_mla = _dot(ya, p_ref[...])
        dos_ref[...] = (g0 * dm).astype(dos_ref.dtype)
        dom_ref[...] = (g2 * dm).astype(dom_ref.dtype)
        do_mla = (g1 * dm).astype(BF)
        dl0 = dm * os_ref[...] * g0 * (1.0 - g0)
        dl1 = dm * o_mla * g1 * (1.0 - g1)
        dl2 = dm * om_ref[...] * g2 * (1.0 - g2)
        dl = jnp.concatenate([dl0, dl1, dl2], axis=1)
        dbg_ref[...] += jnp.sum(dl, axis=0, keepdims=True)
        dlg_ref[...] = dl.astype(dlg_ref.dtype)
        dp_ref[...] += _dot_tn(ya, do_mla)
        dya = _dot_nt(do_mla, p_ref[...])
        doa = dya * sz
        doab = doa.astype(BF)
        doa_ref[...] = doab
        doat_ref[...] = doab.T
        dz_ref[...] = (dya * oa * _dsilu(z)).astype(dz_ref.dtype)
        prod = doa * oa
        lo = lax.broadcasted_iota(jnp.int32, (tr, LANES), 1) < MLA_V
        for pr in range(MLA_HEADS // 2):
            blk = prod[:, pr * LANES:(pr + 1) * LANES]
            d0 = jnp.sum(jnp.where(lo, blk, 0.0), axis=1, keepdims=True)
            d1 = jnp.sum(jnp.where(lo, 0.0, blk), axis=1, keepdims=True)
            dl = jnp.where(lo, d0, d1)
            dl_ref[pr] = dl
            dlt_ref[pr] = _pair_rows(dl)

    return pl.pallas_call(
        body, name="merge_bwd", grid=(S // tr,),
        in_specs=[_rows(tr, 1024), _rows(tr, 1024), _rows(tr, 1024), _rows(tr, 3072, 1), _rows(tr, 512, 3),
                  _rows(tr, 1024), _rows(tr, 512), _rows(tr, 1024), _full((1, 3072)), _full((512, 1024)),
                  _full((1024, 1024)), _full((1, 1024))],
        out_specs=(_rows(tr, 1024), _rows(tr, 3072), _rows(tr, 1024), _rows(tr, 1024), _rows(tr, 512),
                   _rows(tr, 512), _cols(512, tr), pl.BlockSpec((MLA_HEADS // 2, tr, 128), lambda i: (0, i, 0)),
                   pl.BlockSpec((MLA_HEADS // 2, SUBLANES, tr), lambda i: (0, 0, i)),
                   _full((1024, 1024)), _full((512, 1024)), _full((1, 3072)), _full((1, 1024)), _full((1, 1024))),
        out_shape=(jax.ShapeDtypeStruct((S, 1024), F32), jax.ShapeDtypeStruct((S, 3072), BF),
                   jax.ShapeDtypeStruct((S, 1024), BF), jax.ShapeDtypeStruct((S, 1024), BF),
                   jax.ShapeDtypeStruct((S, 512), BF), jax.ShapeDtypeStruct((S, 512), BF),
                   jax.ShapeDtypeStruct((512, S), BF), jax.ShapeDtypeStruct((MLA_HEADS // 2, S, 128), F32),
                   jax.ShapeDtypeStruct((MLA_HEADS // 2, SUBLANES, S), F32),
                   jax.ShapeDtypeStruct((1024, 1024), F32), jax.ShapeDtypeStruct((512, 1024), F32),
                   jax.ShapeDtypeStruct((1, 3072), F32), jax.ShapeDtypeStruct((1, 1024), F32),
                   jax.ShapeDtypeStruct((1, 1024), F32)),
        compiler_params=_cparams(("arbitrary",)),
    )(dxn, pre, merged, proj, proj, o_ssm, o_att, o_mem, b_gate, p_mla, w_out, ln_g)


def _loss_head(y, t):
    S = y.shape[0]
    tr = min(T_ROWS, S)
    n = S // tr

    def body(y_ref, t_ref, dy_ref, l_ref, acc):
        i = pl.program_id(0)

        @pl.when(i == 0)
        def _():
            acc[...] = jnp.zeros_like(acc)

        e = y_ref[...] - t_ref[...]
        dy_ref[...] = e * (1.0 / D_MODEL)
        acc[...] += jnp.sum(e * e, axis=0, keepdims=True)

        @pl.when(i == n - 1)
        def _():
            tot = jnp.sum(acc[...], axis=1, keepdims=True) * (0.5 / D_MODEL)
            l_ref[...] = jnp.broadcast_to(tot, l_ref.shape)

    return pl.pallas_call(
        body, name="loss_head", grid=(n,),
        in_specs=[_rows(tr, 1024), _rows(tr, 1024)],
        out_specs=(_rows(tr, 1024), _full((SUBLANES, LANES))),
        out_shape=(jax.ShapeDtypeStruct((S, 1024), F32), jax.ShapeDtypeStruct((SUBLANES, LANES), F32)),
        scratch_shapes=[pltpu.VMEM((1, 1024), F32)],
        compiler_params=_cparams(("arbitrary",)),
    )(y, t)


def _rope_tables(pos):
    inv_freq = ROPE_THETA ** (-jnp.arange(0, MLA_ROPE, 2, dtype=F32) / MLA_ROPE)
    ang = pos.astype(F32)[:, None] * inv_freq
    cos, sin = jnp.cos(ang), jnp.sin(ang)
    S = pos.shape[0]
    half = MLA_ROPE // 2
    ones = jnp.ones((S, MLA_NOPE), F32)
    z16 = jnp.zeros((S, half), F32)
    z32 = jnp.zeros((S, LANES - MLA_NOPE - MLA_ROPE), F32)
    z64 = jnp.zeros((S, MLA_NOPE), F32)
    c = jnp.concatenate([ones, cos, cos, z32], axis=1)
    sa = jnp.concatenate([z64, -sin, z16, z32], axis=1)
    sb = jnp.concatenate([z64, z16, sin, z32], axis=1)
    return c, sa, sb


def _ssm_discretise(a_re, a_im, log_dt, b_re, b_im):
    dt = jnp.exp(log_dt)[..., None]
    mag = jnp.exp(a_re * dt)
    lb_re = mag * jnp.cos(a_im * dt)
    lb_im = mag * jnp.sin(a_im * dt)
    nr, ni = lb_re - 1.0, lb_im
    den = a_re * a_re + a_im * a_im
    f_re = (nr * a_re + ni * a_im) / den
    f_im = (ni * a_re - nr * a_im) / den
    bb_re = f_re[..., None] * b_re - f_im[..., None] * b_im
    bb_im = f_re[..., None] * b_im + f_im[..., None] * b_re
    return lb_re, lb_im, bb_re, bb_im


_GPB = SSM_CB // SSM_GROUP


def _bd_in(bb):
    nb = SSM_GROUPS // _GPB
    t = bb.reshape(nb, _GPB, SSM_STATE, SSM_GROUP)
    eye = jnp.eye(_GPB, dtype=bb.dtype)
    return jnp.einsum("ngpc,gh->ngchp", t, eye).reshape(nb, SSM_CB, _GPB * SSM_STATE)


def _bd_in_t(d):
    nb = SSM_GROUPS // _GPB
    t = d.reshape(nb, _GPB, SSM_GROUP, _GPB, SSM_STATE)
    eye = jnp.eye(_GPB, dtype=d.dtype)
    return jnp.einsum("ngchp,gh->ngpc", t, eye).reshape(SSM_GROUPS, SSM_STATE, SSM_GROUP)


def _bd_out(c):
    nb = SSM_GROUPS // _GPB
    t = c.reshape(nb, _GPB, SSM_GROUP, SSM_STATE)
    eye = jnp.eye(_GPB, dtype=c.dtype)
    return jnp.einsum("ngcp,gh->ngphc", t, eye).reshape(nb, _GPB * SSM_STATE, SSM_CB)


def _interleave(a):
    S, w = a.shape
    return a.reshape(SUBLANES, S // SUBLANES, w).transpose(1, 0, 2).reshape(S, w)


def _deinterleave(a):
    S, w = a.shape
    return a.reshape(S // SUBLANES, SUBLANES, w).transpose(1, 0, 2).reshape(S, w)


IN_SHARD = D_IN // N_DEV
ROPE_OWNER = ROPE_SLOT_LO // IN_SHARD
assert ROPE_OWNER * IN_SHARD <= ROPE_SLOT_LO and ROPE_SLOT_LO + MLA_ROPE <= (ROPE_OWNER + 1) * IN_SHARD


def _w_in_from_shards(g):
    pieces = []
    for j in range(N_DEV):
        if j == ROPE_OWNER:
            a = ROPE_SLOT_LO - j * IN_SHARD
            z = lambda n: jnp.zeros((g.shape[1], n), g.dtype)
            pieces += [g[j][:, :a], z(MLA_NOPE), g[j][:, a:a + MLA_ROPE], z(LANES - MLA_NOPE - MLA_ROPE),
                       g[j][:, a + MLA_ROPE:]]
        else:
            pieces.append(g[j])
    return jnp.concatenate(pieces, axis=1)


def _w_in_to_shards(d):
    shift = LANES - MLA_ROPE
    out = []
    for j in range(N_DEV):
        lo, hi = j * IN_SHARD, (j + 1) * IN_SHARD
        if j < ROPE_OWNER:
            out.append(d[:, lo:hi])
        elif j > ROPE_OWNER:
            out.append(d[:, lo + shift:hi + shift])
        else:
            r = ROPE_SLOT_LO + MLA_NOPE
            out.append(jnp.concatenate([d[:, lo:ROPE_SLOT_LO], d[:, r:r + MLA_ROPE],
                                        d[:, ROPE_SLOT_LO + LANES:hi + shift]], axis=1))
    return jnp.stack(out)


def _adamw_math(w, g, m, v):
    m = ADAM_B1 * m + (1.0 - ADAM_B1) * g
    v = ADAM_B2 * v + (1.0 - ADAM_B2) * (g * g)
    m_hat = m / (1.0 - ADAM_B1 ** ADAM_STEP)
    v_hat = v / (1.0 - ADAM_B2 ** ADAM_STEP)
    delta = -ADAM_LR * (m_hat / (jnp.sqrt(v_hat) + ADAM_EPS) + ADAM_WD * w)
    return delta, m, v


def _adamw_sharded(parts, w, m, v, tile, name):
    L, _, R, C = parts.shape
    assert R % tile == 0

    def body(p_ref, w_ref, m_ref, v_ref, g_out, d_out, m_out, v_out):
        g = p_ref[0, 0].astype(F32)
        for k in range(1, N_DEV):
            g = g + p_ref[0, k].astype(F32)
        d, mn, vn = _adamw_math(w_ref[0], g, m_ref[0], v_ref[0])
        g_out[0] = g
        d_out[0] = d
        m_out[0] = mn
        v_out[0] = vn

    spec = pl.BlockSpec((1, tile, C), lambda l, i: (l, i, 0))
    shp = jax.ShapeDtypeStruct((L, R, C), F32)
    return pl.pallas_call(
        body, name=name, grid=(L, R // tile),
        in_specs=[pl.BlockSpec((1, N_DEV, tile, C), lambda l, i: (l, 0, i, 0)), spec, spec, spec],
        out_specs=(spec,) * 4, out_shape=(shp,) * 4, compiler_params=_cparams(("parallel", "parallel")),
    )(parts, w, m, v)


COL_GROUP = (("w_glu", 512), ("w_ukv", 128), ("p_ssm", 512), ("p_mla", 512), ("p_mem", 512), ("w_uq", 256))
ROW_GROUP = ("w_mem_kv", "w_out")
SMALL = ("b_gate", "ssm_a_re", "ssm_a_im", "ssm_log_dt", "ssm_b_re", "ssm_b_im", "ssm_c_re", "ssm_c_im", "ssm_d",
         "b_glu", "mla_q_norm", "mla_kv_norm", "ln_g", "ln_b")
SMALL_TILE = 512
UQ_COLS = MLA_NOPE + MLA_ROPE


def _pad_lanes(a):
    return jnp.concatenate([a, jnp.zeros(a.shape[:-1] + (LANES - a.shape[-1],), a.dtype)], axis=-1)


def _group_buffers(d, dtype):
    col = jnp.concatenate([_pad_lanes(d[n]) if n == "w_uq" else d[n] for n, _ in COL_GROUP], axis=1)
    row = jnp.concatenate([d[n] for n in ROW_GROUP], axis=1)
    return d["w_in"].astype(dtype), col.astype(dtype), row.astype(dtype)


def _ungroup(bufs):
    b_in, col, row = bufs
    out, off = {"w_in": b_in}, 0
    for n, rows in COL_GROUP:
        t = col[:, off:off + rows]
        out[n] = t[..., :UQ_COLS] if n == "w_uq" else t
        off += rows
    k = row.shape[1] // 2
    out["w_mem_kv"], out["w_out"] = row[:, :k], row[:, k:]
    return out


def _colcat(t):
    return t.transpose(1, 0, 2).reshape(t.shape[1], -1)


def _colsplit(g, n):
    return g.reshape(g.shape[0], N_DEV, n).transpose(1, 0, 2)


def _unpack_weights(g_in, g_col, g_row):
    off, parts = 0, {}
    for n, rows in COL_GROUP:
        parts[n] = g_col[:, off:off + rows]
        off += rows
    ukv = parts["w_ukv"]
    lane = lax.broadcasted_iota(jnp.int32, ukv.shape, 2)
    k = g_row.shape[1] // 2
    return dict(
        w_in=_w_in_from_shards(g_in), w_glu=_colcat(parts["w_glu"]), w_uq=_colcat(parts["w_uq"]),
        w_k=_colcat(jnp.where(lane < MLA_NOPE, ukv, jnp.zeros_like(ukv))), w_v=_colcat(ukv[..., MLA_NOPE:]),
        p_ssm=_colcat(parts["p_ssm"]), p_mla=_colcat(parts["p_mla"]), p_mem=_colcat(parts["p_mem"]),
        w_mem_kv=g_row[:, :k].reshape(-1, g_row.shape[2]), w_out=g_row[:, k:].reshape(-1, g_row.shape[2]))


def _pack_grads(d):
    ukv = jnp.concatenate([d["w_k"].reshape(-1, MLA_HEADS, LANES)[..., :MLA_NOPE],
                           d["w_v"].reshape(-1, MLA_HEADS, MLA_V)], axis=-1).transpose(1, 0, 2)
    cols = dict(w_glu=_colsplit(d["w_glu"], LANES), w_ukv=ukv, p_ssm=_colsplit(d["p_ssm"], LANES),
                p_mla=_colsplit(d["p_mla"], LANES), p_mem=_colsplit(d["p_mem"], LANES), w_uq=_colsplit(d["w_uq"], LANES))
    col = jnp.concatenate([cols[n] for n, _ in COL_GROUP], axis=1)
    row = jnp.concatenate([d[n].reshape(N_DEV, -1, d[n].shape[1]) for n in ROW_GROUP], axis=1)
    return [_w_in_to_shards(d["w_in"]).astype(BF), col.astype(BF), row.astype(BF)]


def _pack_small(d):
    parts = []
    for n in SMALL:
        f = d[n].reshape(-1)
        pad = (-f.shape[0]) % (SUBLANES * LANES)
        if pad:
            f = jnp.concatenate([f, jnp.zeros((pad,), f.dtype)])
        parts.append(f.reshape(-1, LANES))
    rows = sum(p.shape[0] for p in parts)
    pad = (-rows) % SMALL_TILE
    if pad:
        parts.append(jnp.zeros((pad, LANES), parts[0].dtype))
    return jnp.concatenate(parts, axis=0)


def _unpack_small(buf, like):
    out, off = {}, 0
    for n in SMALL:
        size = math.prod(like[n].shape)
        rows = -(-size // (SUBLANES * LANES)) * SUBLANES
        out[n] = buf[off:off + rows].reshape(-1)[:size].reshape(like[n].shape)
        off += rows
    return out


WEIGHTS = ("w_in", "b_gate", "ssm_a_re", "ssm_a_im", "ssm_log_dt", "ssm_b_re", "ssm_b_im", "ssm_c_re", "ssm_c_im",
           "ssm_d", "w_glu", "b_glu", "mla_q_norm", "w_uq", "mla_kv_norm", "w_ukv", "w_mem_kv", "p_ssm", "p_mla",
           "p_mem", "w_out", "ln_g", "ln_b")
BIG = ("w_in",) + tuple(n for n, _ in COL_GROUP) + ROW_GROUP


def _train_step(x, mem, pos, target, wl, ws):
    S = x.shape[0]
    tc, tsa, tsb = _rope_tables(pos)
    loc = _group_buffers(wl, BF)
    loc = [[b[l] for b in loc] for l in range(DEPTH)]

    def ssm_mats(l):
        lb_re, lb_im, bb_re, bb_im = _ssm_discretise(ws["ssm_a_re"][l], ws["ssm_a_im"][l], ws["ssm_log_dt"][l],
                                                     ws["ssm_b_re"][l], ws["ssm_b_im"][l])
        nb = SSM_GROUPS // _GPB
        return (_bd_in(bb_re), _bd_in(bb_im), lb_re.reshape(nb, 1, -1), lb_im.reshape(nb, 1, -1),
                _bd_out(ws["ssm_c_re"][l]), _bd_out(ws["ssm_c_im"][l]), ws["ssm_d"][l].reshape(1, -1))

    def small(n, l):
        return ws[n][l].reshape(1, -1)

    (gathered,) = _exchange_call("weights_gather_first", [_Exchange("ag", loc[0])])
    W = [None] * DEPTH
    W[0] = _unpack_weights(*gathered)
    saved = []
    xs, xb = x, x.astype(BF)
    for l in range(DEPTH):
        w = W[l]
        proj = _mm(xb, w["w_in"], name="proj_fwd", tm=S, tn=512)
        mats = ssm_mats(l)
        u_il = _interleave(proj[:, :SSM_WIDTH])
        y_raw = _deinterleave(_ssm_fwd(u_il, *mats))
        o_ssm = _ssm_post_fwd(y_raw, proj, w["w_glu"], small("b_glu", l), w["p_ssm"])
        q, k, v, qt, kt, vt = _mla_pre_fwd(proj, small("mla_q_norm", l), small("mla_kv_norm", l), w["w_uq"], w["w_k"], w["w_v"],
                               tc, tsa, tsb)
        nxt = [_Exchange("ag", loc[l + 1])] if l + 1 < DEPTH else []
        (o_att, lse, lse_t), gathered = _flash_fwd(q, kt, v, nxt)
        if nxt:
            W[l + 1] = _unpack_weights(*gathered[0])
        kvm = _mm(mem, w["w_mem_kv"], name="memkv_fwd", out_dtype=BF)
        km, vm = kvm[:, :512], kvm[:, 512:]
        o_mem = _mem_fwd(proj, km, vm, w["p_mem"])
        xn, xnb, pre, merged = _merge_fwd(xs, proj, o_ssm, o_att, o_mem, small("b_gate", l), w["p_mla"], w["w_out"],
                                          small("ln_g", l), small("ln_b", l))
        saved.append(dict(xb=xb, proj=proj, u_il=u_il, y_raw=y_raw, o_ssm=o_ssm, q=q, k=k, v=v, qt=qt, kt=kt, vt=vt, o_att=o_att,
                          lse=lse, lse_t=lse_t,
                          km=km, vm=vm, o_mem=o_mem, pre=pre, merged=merged))
        xs, xb = xn, xnb

    dxs, lvec = _loss_head(xs, target)
    loss = lvec[0, 0]

    gs = {n: [None] * DEPTH for n in SMALL}
    disc_ct = [None] * DEPTH
    got = [None] * DEPTH
    pending = None
    for l in reversed(range(DEPTH)):
        sv, w = saved[l], W[l]
        proj = sv["proj"]
        (dx_res, dlg, do_ssm, do_mem, do_att, dz_mla, do_att_t, delta, delta_t, d_w_out, d_p_mla, d_b_gate, d_ln_g,
         d_ln_b) = _merge_bwd(
            dxs, sv["pre"], sv["merged"], proj, sv["o_ssm"], sv["o_att"], sv["o_mem"], small("b_gate", l), w["p_mla"],
            w["w_out"], small("ln_g", l))
        dq_mem, dz_mem, d_km, d_vm, d_p_mem = _mem_bwd(do_mem, proj, sv["km"], sv["vm"], w["p_mem"])
        d_w_mem = _mm(mem, jnp.concatenate([d_km, d_vm], axis=1), name="memkv_bwd", ta=True)
        dq, _ = _flash_bwd_dq(sv["q"], sv["k"], sv["kt"], sv["vt"], do_att, sv["lse"], delta)
        exs = [_Exchange("a2a", pending)] if pending is not None else []
        dk, dv, arrived = _flash_bwd_dkv(sv["q"], sv["qt"], sv["k"], sv["v"], do_att, do_att_t, sv["lse_t"], delta_t, exs)
        if exs:
            got[l + 1] = arrived[0]
        dcq, dckv, dslot, d_wuq, d_wk, d_wv, d_qn, d_kn = _mla_pre_bwd(
            dq, dk, dv, proj, small("mla_q_norm", l), small("mla_kv_norm", l), w["w_uq"], w["w_k"], w["w_v"],
            tc, tsa, tsb)
        dy_raw, dz_ssm, d_w_glu, d_b_glu, d_p_ssm = _ssm_post_bwd(do_ssm, sv["y_raw"], proj, w["w_glu"],
                                                                 small("b_glu", l), w["p_ssm"])
        du_il, dbbre, dbbim, dare, daim, dcre, dcim, dd = _ssm_bwd(sv["u_il"], _interleave(dy_raw), *ssm_mats(l))
        du = _deinterleave(du_il).astype(BF)
        disc_ct[l] = (dare.reshape(SSM_GROUPS, SSM_STATE), daim.reshape(SSM_GROUPS, SSM_STATE), _bd_in_t(dbbre),
                      _bd_in_t(dbbim))
        dproj = jnp.concatenate([du, dz_ssm, dcq, dckv, dslot, dz_mla, dq_mem, dz_mem, dlg], axis=1)
        d_w_in = _mm(sv["xb"], dproj, name="proj_dw", ta=True, tm=1024, tn=512, tk=S)
        if l > 0:
            dxs = _mm(dproj, w["w_in"], name="proj_dx", tb=True, add=dx_res, tm=1024, tn=1024, tk=1024)
        pending = _pack_grads(dict(w_in=d_w_in, w_glu=d_w_glu, w_uq=d_wuq, w_k=d_wk, w_v=d_wv, w_mem_kv=d_w_mem,
                                   p_ssm=d_p_ssm, p_mla=d_p_mla, p_mem=d_p_mem, w_out=d_w_out))
        gs["b_gate"][l] = d_b_gate.reshape(-1)
        gs["ssm_c_re"][l] = _bd_in_t(dcre).transpose(0, 2, 1)
        gs["ssm_c_im"][l] = _bd_in_t(dcim).transpose(0, 2, 1)
        gs["ssm_d"][l] = dd.reshape(-1)
        gs["b_glu"][l] = d_b_glu.reshape(-1)
        gs["mla_q_norm"][l] = d_qn.reshape(-1)
        gs["mla_kv_norm"][l] = d_kn.reshape(-1)
        gs["ln_g"][l] = d_ln_g.reshape(-1)
        gs["ln_b"][l] = d_ln_b.reshape(-1)

    disc_in = tuple(ws[n] for n in ("ssm_a_re", "ssm_a_im", "ssm_log_dt", "ssm_b_re", "ssm_b_im"))
    _, disc_vjp = jax.vjp(_ssm_discretise, *disc_in)
    d_disc = disc_vjp(tuple(jnp.stack([disc_ct[l][i] for l in range(DEPTH)]) for i in range(4)))
    gsm = {n: jnp.stack(v) for n, v in gs.items() if v[0] is not None}
    for n, g in zip(("ssm_a_re", "ssm_a_im", "ssm_log_dt", "ssm_b_re", "ssm_b_im"), d_disc):
        gsm[n] = g
    dxs, (got[0], (small_parts,)) = _mm(
        dproj, w["w_in"], name="proj_dx", tb=True, add=dx_res, tm=1024, tn=1024, tk=1024,
        exs=[_Exchange("a2a", pending), _Exchange("ag", [_pack_small(gsm)])])
    return loss, dxs, got, small_parts


def kernel(x, mem, positions, w_in, b_gate, ssm_a_re, ssm_a_im, ssm_log_dt, ssm_b_re, ssm_b_im, ssm_c_re, ssm_c_im, ssm_d, w_glu, b_glu, mla_q_norm, w_uq, mla_kv_norm, w_ukv, w_mem_kv, p_ssm, p_mla, p_mem, w_out, ln_g, ln_b, loss_target, m_w_in, m_b_gate, m_ssm_a_re, m_ssm_a_im, m_ssm_log_dt, m_ssm_b_re, m_ssm_b_im, m_ssm_c_re, m_ssm_c_im, m_ssm_d, m_w_glu, m_b_glu, m_mla_q_norm, m_w_uq, m_mla_kv_norm, m_w_ukv, m_w_mem_kv, m_p_ssm, m_p_mla, m_p_mem, m_w_out, m_ln_g, m_ln_b, v_w_in, v_b_gate, v_ssm_a_re, v_ssm_a_im, v_ssm_log_dt, v_ssm_b_re, v_ssm_b_im, v_ssm_c_re, v_ssm_c_im, v_ssm_d, v_w_glu, v_b_glu, v_mla_q_norm, v_w_uq, v_mla_kv_norm, v_w_ukv, v_w_mem_kv, v_p_ssm, v_p_mla, v_p_mem, v_w_out, v_ln_g, v_ln_b):
    w = dict(w_in=w_in, b_gate=b_gate, ssm_a_re=ssm_a_re, ssm_a_im=ssm_a_im, ssm_log_dt=ssm_log_dt, ssm_b_re=ssm_b_re,
             ssm_b_im=ssm_b_im, ssm_c_re=ssm_c_re, ssm_c_im=ssm_c_im, ssm_d=ssm_d, w_glu=w_glu, b_glu=b_glu,
             mla_q_norm=mla_q_norm, w_uq=w_uq, mla_kv_norm=mla_kv_norm, w_ukv=w_ukv, w_mem_kv=w_mem_kv, p_ssm=p_ssm,
             p_mla=p_mla, p_mem=p_mem, w_out=w_out, ln_g=ln_g, ln_b=ln_b)
    m = dict(w_in=m_w_in, b_gate=m_b_gate, ssm_a_re=m_ssm_a_re, ssm_a_im=m_ssm_a_im, ssm_log_dt=m_ssm_log_dt,
             ssm_b_re=m_ssm_b_re, ssm_b_im=m_ssm_b_im, ssm_c_re=m_ssm_c_re, ssm_c_im=m_ssm_c_im, ssm_d=m_ssm_d,
             w_glu=m_w_glu, b_glu=m_b_glu, mla_q_norm=m_mla_q_norm, w_uq=m_w_uq, mla_kv_norm=m_mla_kv_norm,
             w_ukv=m_w_ukv, w_mem_kv=m_w_mem_kv, p_ssm=m_p_ssm, p_mla=m_p_mla, p_mem=m_p_mem, w_out=m_w_out,
             ln_g=m_ln_g, ln_b=m_ln_b)
    v = dict(w_in=v_w_in, b_gate=v_b_gate, ssm_a_re=v_ssm_a_re, ssm_a_im=v_ssm_a_im, ssm_log_dt=v_ssm_log_dt,
             ssm_b_re=v_ssm_b_re, ssm_b_im=v_ssm_b_im, ssm_c_re=v_ssm_c_re, ssm_c_im=v_ssm_c_im, ssm_d=v_ssm_d,
             w_glu=v_w_glu, b_glu=v_b_glu, mla_q_norm=v_mla_q_norm, w_uq=v_w_uq, mla_kv_norm=v_mla_kv_norm,
             w_ukv=v_w_ukv, w_mem_kv=v_w_mem_kv, p_ssm=v_p_ssm, p_mla=v_p_mla, p_mem=v_p_mem, w_out=v_w_out,
             ln_g=v_ln_g, ln_b=v_ln_b)

    wl = {n: w[n] for n in BIG}
    small = {n: w[n] for n in SMALL}
    loss_local, dx, got, small_parts = _train_step(x[0], mem[0], positions[0], loss_target[0], wl, small)
    loss = lax.psum(loss_local, ("x", "y", "c"))

    grads, delta, new_m, new_v = {}, {}, {}, {}
    wg = _group_buffers(wl, F32)
    mg = _group_buffers({n: m[n] for n in BIG}, F32)
    vg = _group_buffers({n: v[n] for n in BIG}, F32)
    res = []
    for i, (tile, tag) in enumerate(((256, "in"), (128, "col"), (256, "row"))):
        parts = jnp.stack([got[l][i] for l in range(DEPTH)])
        res.append(_adamw_sharded(parts, wg[i], mg[i], vg[i], tile, "adamw_" + tag))
    for dst, j in ((grads, 0), (delta, 1), (new_m, 2), (new_v, 3)):
        dst.update(_ungroup([r[j] for r in res]))

    sw, sm, sv = (_pack_small(small), _pack_small({n: m[n] for n in SMALL}), _pack_small({n: v[n] for n in SMALL}))
    rs = _adamw_sharded(small_parts[None], sw[None], sm[None], sv[None], SMALL_TILE, "adamw_replicated")
    for dst, buf in zip((grads, delta, new_m, new_v), rs):
        dst.update(_unpack_small(buf[0], small))

    return (loss, dx[None], *[grads[n] for n in WEIGHTS], *[delta[n] for n in WEIGHTS],
            *[new_m[n] for n in WEIGHTS], *[new_v[n] for n in WEIGHTS])
```

```python
import math

import jax
import jax.numpy as jnp
from jax import lax
from jax.experimental import pallas as pl
from jax.experimental.pallas import tpu as pltpu

F32 = jnp.float32
BF = jnp.bfloat16

D_MODEL = 1024
DEPTH = 4
N_DEV = 8
SSM_WIDTH = 512
SSM_GROUP = 16
SSM_GROUPS = 32
SSM_STATE = 64
MLA_HEADS = 8
MLA_NOPE = 64
MLA_ROPE = 32
MLA_V = 64
MLA_Q_RANK = 256
MLA_KV_RANK = 128
ROPE_THETA = 10000.0
X_HEADS = 4
X_HEAD_DIM = 128
D_IN = 6048
ALPHA = (2 * DEPTH) ** 0.25
NORM_EPS = 1e-5
ADAM_LR = 0.001
ADAM_B1 = 0.9
ADAM_B2 = 0.999
ADAM_EPS = 1e-08
ADAM_WD = 0.01
ADAM_STEP = 10

LANES = 128
SUBLANES = 8
VMEM_LIMIT = 56 * 1024 * 1024

PW = 6144
ROPE_SLOT_LO = 1408
MLA_SCALE = (MLA_NOPE + MLA_ROPE) ** -0.5
MEM_SCALE = X_HEAD_DIM ** -0.5
NEG = -1e30

T_ROWS = 512
T_ROWS_BWD = 256
T_ATT = 1024
T_MM = 512

MESH = pl.DeviceIdType.MESH


def _cparams(sem):
    return pltpu.CompilerParams(dimension_semantics=sem, vmem_limit_bytes=VMEM_LIMIT)


def _dot(a, b):
    return lax.dot_general(a, b, (((1,), (0,)), ((), ())), preferred_element_type=F32)


def _dot_nt(a, b):
    return lax.dot_general(a, b, (((1,), (1,)), ((), ())), preferred_element_type=F32)


def _dot_tn(a, b):
    return lax.dot_general(a, b, (((0,), (0,)), ((), ())), preferred_element_type=F32)


def _sigmoid(x):
    return 0.5 * jnp.tanh(0.5 * x) + 0.5


def _silu(x):
    return x * _sigmoid(x)


def _dsilu(x):
    s = _sigmoid(x)
    return s * (1.0 + x * (1.0 - s))


_GELU_C = math.sqrt(2.0 / math.pi)


def _gelu(x):
    return 0.5 * x * (1.0 + jnp.tanh(_GELU_C * (x + 0.044715 * x * x * x)))


def _dgelu(x):
    t = jnp.tanh(_GELU_C * (x + 0.044715 * x * x * x))
    return 0.5 * (1.0 + t) + 0.5 * x * (1.0 - t * t) * _GELU_C * (1.0 + 3 * 0.044715 * x * x)


def _rows(tr, w, col=0):
    return pl.BlockSpec((tr, w), lambda i: (i, col))


def _cols(h, tc):
    return pl.BlockSpec((h, tc), lambda i: (0, i))


def _full(shape):
    n = len(shape)
    return pl.BlockSpec(shape, lambda i: (0,) * n)


class _RowBlock:
    def __init__(self, arr, rows, blk):
        self.arr, self.rows, self.blk = arr, rows, blk

    def spec(self):
        blk = self.blk
        return pl.BlockSpec((self.rows, self.arr.shape[1]), lambda i: (blk, 0))


class _LayerRow:
    def __init__(self, arr, l):
        self.arr, self.l = arr, l

    def spec(self):
        l = self.l
        return pl.BlockSpec((1, 1, self.arr.shape[2]), lambda i: (l, 0, 0))


def _mm(a, b, *, name, ta=False, tb=False, out_dtype=F32, add=None, tm=T_MM, tn=T_MM, tk=1024, exs=None):
    M, K = (a.shape[1], a.shape[0]) if ta else a.shape
    N = b.shape[0] if tb else b.shape[1]
    tm, tn, tk = min(tm, M), min(tn, N), min(tk, K)
    assert M % tm == 0 and N % tn == 0 and K % tk == 0, (M, N, K)
    nk = K // tk
    dn = (((0 if ta else 1,), (1 if tb else 0,)), ((), ()))

    def body(*refs):
        if add is not None:
            a_ref, b_ref, c_ref, o_ref = refs[:4]
        else:
            a_ref, b_ref, o_ref = refs[:3]
        part = lax.dot_general(a_ref[...].astype(BF), b_ref[...].astype(BF), dn, preferred_element_type=F32)
        if nk == 1:
            if add is not None:
                part = part + c_ref[...]
            o_ref[...] = part.astype(out_dtype)
            return
        acc = refs[-1]
        k = pl.program_id(2)

        @pl.when(k == 0)
        def _():
            acc[...] = part

        @pl.when(k != 0)
        def _():
            acc[...] += part

        @pl.when(k == nk - 1)
        def _():
            r = acc[...]
            if add is not None:
                r = r + c_ref[...]
            o_ref[...] = r.astype(out_dtype)

    a_spec = pl.BlockSpec((tk, tm), lambda i, j, k: (k, i)) if ta else pl.BlockSpec((tm, tk), lambda i, j, k: (i, k))
    b_spec = pl.BlockSpec((tn, tk), lambda i, j, k: (j, k)) if tb else pl.BlockSpec((tk, tn), lambda i, j, k: (k, j))
    o_spec = pl.BlockSpec((tm, tn), lambda i, j, k: (i, j))
    in_specs = [a_spec, b_spec] + ([o_spec] if add is not None else [])
    args = (a, b) + ((add,) if add is not None else ())
    (out,), got = _carry_call(
        body, name, (M // tm, N // tn, nk), in_specs, [o_spec], [jax.ShapeDtypeStruct((M, N), out_dtype)],
        [pltpu.VMEM((tm, tn), F32)] if nk > 1 else [], ("parallel", "parallel", "arbitrary"), args, exs)
    return out if exs is None else (out, got)


def _cpow(ar, ai, n):
    rr, ri = None, None
    br, bi = ar, ai
    while n:
        if n & 1:
            if rr is None:
                rr, ri = br, bi
            else:
                rr, ri = rr * br - ri * bi, rr * bi + ri * br
        n >>= 1
        if n:
            br, bi = br * br - bi * bi, 2.0 * br * bi
    return rr, ri


def _seg_shift(v, k, reverse):
    sub = lax.broadcasted_iota(jnp.int32, v.shape, 0)
    if not reverse:
        return jnp.where(sub >= k, pltpu.roll(v, k, 0), 0.0)
    return jnp.where(sub < SUBLANES - k, pltpu.roll(v, SUBLANES - k, 0), 0.0)


def _steps(n, step, init, unroll):
    u = unroll if n % unroll == 0 else 1

    def trip(i, c):
        for s in range(u):
            c = step(i * u + s, c)
        return c

    return lax.fori_loop(0, n // u, trip, init)


def _ssm_scan(hre, him, ar, ai, seglen, reverse, tail=None, tail_init=()):
    w = hre.shape[1]
    zero = jnp.zeros((SUBLANES, w), F32)

    def rows(j):
        jj = (seglen - 1 - j) if reverse else j
        return pl.ds(pl.multiple_of(jj * SUBLANES, SUBLANES), SUBLANES)

    def local(j, c):
        hr, hi = c
        r = rows(j)
        nhr = ar * hr - ai * hi + hre[r, :]
        nhi = ar * hi + ai * hr + him[r, :]
        hre[r, :] = nhr
        him[r, :] = nhi
        return nhr, nhi

    er, ei = _steps(seglen, local, (zero, zero), 4 if reverse else 1)
    pr, pi_ = _cpow(ar, ai, seglen)
    for k in (1, 2, 4):
        sr, si = _seg_shift(er, k, reverse), _seg_shift(ei, k, reverse)
        er, ei = er + pr * sr - pi_ * si, ei + pr * si + pi_ * sr
        pr, pi_ = pr * pr - pi_ * pi_, 2.0 * pr * pi_
    cr, ci = _seg_shift(er, 1, reverse), _seg_shift(ei, 1, reverse)

    def carry_in(j, c):
        tr, ti = c[0] * ar - c[1] * ai, c[0] * ai + c[1] * ar
        r = rows(j)
        fr = hre[r, :] + tr
        fi = him[r, :] + ti
        hre[r, :] = fr
        him[r, :] = fi
        if tail is None:
            return tr, ti
        return (tr, ti) + tuple(tail(j, fr, fi, c[2:]))

    out = _steps(seglen, carry_in, (cr, ci) + tuple(tail_init), 4)
    return out[2:]


SSM_CB = 128
SSM_SB = 256


def _ssm_specs(S, l):
    u_spec = pl.BlockSpec((S, SSM_CB), lambda g, h: (0, g))
    bb_spec = pl.BlockSpec((1, 1, SSM_CB, SSM_SB), lambda g, h: (l, g, 0, h))
    a_spec = pl.BlockSpec((1, 1, 1, SSM_SB), lambda g, h: (l, g, 0, h))
    c_spec = pl.BlockSpec((1, 1, SSM_SB, SSM_CB), lambda g, h: (l, g, h, 0))
    d_spec = pl.BlockSpec((1, 1, SSM_CB), lambda g, h: (l, 0, g))
    return u_spec, bb_spec, a_spec, c_spec, d_spec


def _ssm_fwd(u, mats, l):
    S = u.shape[0]
    seglen = S // SUBLANES
    ch = min(512, S)
    nch = S // ch

    def body(u_ref, bbre_ref, bbim_ref, are_ref, aim_ref, cre_ref, cim_ref, d_ref, y_ref, hre, him):
        hf = pl.program_id(1)
        wre = bbre_ref[0, 0].astype(BF)
        wim = bbim_ref[0, 0].astype(BF)

        def mk(c, _):
            r = pl.ds(pl.multiple_of(c * ch, ch), ch)
            ub = u_ref[r, :].astype(BF)
            hre[r, :] = _dot(ub, wre)
            him[r, :] = _dot(ub, wim)
            return 0

        lax.fori_loop(0, nch, mk, 0)
        ar = jnp.broadcast_to(are_ref[0, 0], (SUBLANES, SSM_SB))
        ai = jnp.broadcast_to(aim_ref[0, 0], (SUBLANES, SSM_SB))
        _ssm_scan(hre, him, ar, ai, seglen, False)
        cr = cre_ref[0, 0].astype(BF)
        ci = cim_ref[0, 0].astype(BF)

        def out(c, _):
            r = pl.ds(pl.multiple_of(c * ch, ch), ch)
            y = _dot(hre[r, :].astype(BF), cr) - _dot(him[r, :].astype(BF), ci)

            @pl.when(hf == 0)
            def _():
                y_ref[r, :] = y + d_ref[0] * u_ref[r, :]

            @pl.when(hf != 0)
            def _():
                y_ref[r, :] = y_ref[r, :] + y

            return 0

        lax.fori_loop(0, nch, out, 0)

    u_spec, bb_spec, a_spec, c_spec, d_spec = _ssm_specs(S, l)
    return pl.pallas_call(
        body, name="ssm_fwd", grid=(SSM_WIDTH // SSM_CB, 2),
        in_specs=[u_spec, bb_spec, bb_spec, a_spec, a_spec, c_spec, c_spec, d_spec], out_specs=u_spec,
        out_shape=jax.ShapeDtypeStruct((S, SSM_WIDTH), F32),
        scratch_shapes=[pltpu.VMEM((S, SSM_SB), F32), pltpu.VMEM((S, SSM_SB), F32)],
        compiler_params=_cparams(("parallel", "arbitrary")),
    )(u, *mats)


def _ssm_bwd(u, dy, mats, l, exs=()):
    S = u.shape[0]
    seglen = S // SUBLANES
    ch = min(512, S)
    nch = S // ch
    nblk = SSM_WIDTH // SSM_CB

    def body(u_ref, dy_ref, bbre_ref, bbim_ref, are_ref, aim_ref, cre_ref, cim_ref, d_ref,
             du_ref, dbbre_ref, dbbim_ref, dare_ref, daim_ref, dcre_ref, dcim_ref, dd_ref,
             hre, him, lre, lim):
        hf = pl.program_id(1)
        wre = bbre_ref[0, 0].astype(BF)
        wim = bbim_ref[0, 0].astype(BF)
        wre_t, wim_t = wre.T, wim.T
        cr_t = cre_ref[0, 0].astype(BF).T
        ci_t = cim_ref[0, 0].astype(BF).T

        def mk(c, _):
            r = pl.ds(pl.multiple_of(c * ch, ch), ch)
            ub = u_ref[r, :].astype(BF)
            hre[r, :] = _dot(ub, wre)
            him[r, :] = _dot(ub, wim)
            return 0

        lax.fori_loop(0, nch, mk, 0)
        ar = jnp.broadcast_to(are_ref[0, 0], (SUBLANES, SSM_SB))
        ai = jnp.broadcast_to(aim_ref[0, 0], (SUBLANES, SSM_SB))
        _ssm_scan(hre, him, ar, ai, seglen, False)

        dcre_ref[...] = jnp.zeros_like(dcre_ref)
        dcim_ref[...] = jnp.zeros_like(dcim_ref)

        @pl.when(hf == 0)
        def _():
            dd_ref[...] = jnp.zeros_like(dd_ref)

        def cot(c, _):
            r = pl.ds(pl.multiple_of(c * ch, ch), ch)
            dyv = dy_ref[r, :]
            dyb = dyv.astype(BF)
            lre[r, :] = _dot(dyb, cr_t)
            lim[r, :] = -_dot(dyb, ci_t)
            dcre_ref[0] = dcre_ref[0] + _dot_tn(dyb, hre[r, :].astype(BF))
            dcim_ref[0] = dcim_ref[0] - _dot_tn(dyb, him[r, :].astype(BF))

            @pl.when(hf == 0)
            def _():
                dd_ref[...] = dd_ref[...] + jnp.sum(dyv * u_ref[r, :], axis=0, keepdims=True)

            return 0

        lax.fori_loop(0, nch, cot, 0)

        last = pl.ds((seglen - 1) * SUBLANES, SUBLANES)
        pr0 = _seg_shift(hre[last, :], 1, False)
        pi0 = _seg_shift(him[last, :], 1, False)

        def da(j, lr, li, c):
            acr, aci = c
            jp = jnp.maximum(seglen - 2 - j, 0)
            rp = pl.ds(pl.multiple_of(jp * SUBLANES, SUBLANES), SUBLANES)
            inner = j < seglen - 1
            pr = jnp.where(inner, hre[rp, :], pr0)
            pi_ = jnp.where(inner, him[rp, :], pi0)
            return acr + lr * pr + li * pi_, aci + li * pr - lr * pi_

        zero = jnp.zeros((SUBLANES, SSM_SB), F32)
        acr, aci = _ssm_scan(lre, lim, ar, -ai, seglen, True, tail=da, tail_init=(zero, zero))
        dare_ref[0] = jnp.sum(acr, axis=0, keepdims=True)
        daim_ref[0] = jnp.sum(aci, axis=0, keepdims=True)

        dbbre_ref[...] = jnp.zeros_like(dbbre_ref)
        dbbim_ref[...] = jnp.zeros_like(dbbim_ref)

        def fin(c, _):
            r = pl.ds(pl.multiple_of(c * ch, ch), ch)
            lrb = lre[r, :].astype(BF)
            lib = lim[r, :].astype(BF)
            ub = u_ref[r, :].astype(BF)
            du = _dot(lrb, wre_t) + _dot(lib, wim_t)
            dbbre_ref[0] = dbbre_ref[0] + _dot_tn(ub, lrb)
            dbbim_ref[0] = dbbim_ref[0] + _dot_tn(ub, lib)

            @pl.when(hf == 0)
            def _():
                du_ref[r, :] = du + d_ref[0] * dy_ref[r, :]

            @pl.when(hf != 0)
            def _():
                du_ref[r, :] = du_ref[r, :] + du

            return 0

        lax.fori_loop(0, nch, fin, 0)

    u_spec, bb_spec, a_spec, c_spec, d_spec = _ssm_specs(S, l)
    dbb_spec = pl.BlockSpec((1, SSM_CB, SSM_SB), lambda g, h: (g, 0, h))
    da_spec = pl.BlockSpec((1, 1, SSM_SB), lambda g, h: (g, 0, h))
    dd_spec = pl.BlockSpec((1, SSM_CB), lambda g, h: (0, g))
    out_shape = (
        jax.ShapeDtypeStruct((S, SSM_WIDTH), F32),
        jax.ShapeDtypeStruct((nblk, SSM_CB, 2 * SSM_SB), F32), jax.ShapeDtypeStruct((nblk, SSM_CB, 2 * SSM_SB), F32),
        jax.ShapeDtypeStruct((nblk, 1, 2 * SSM_SB), F32), jax.ShapeDtypeStruct((nblk, 1, 2 * SSM_SB), F32),
        jax.ShapeDtypeStruct((nblk, SSM_CB, 2 * SSM_SB), F32), jax.ShapeDtypeStruct((nblk, SSM_CB, 2 * SSM_SB), F32),
        jax.ShapeDtypeStruct((1, SSM_WIDTH), F32),
    )
    return _carry_call(
        body, "ssm_bwd", (nblk, 2), [u_spec, u_spec, bb_spec, bb_spec, a_spec, a_spec, c_spec, c_spec, d_spec],
        (u_spec, dbb_spec, dbb_spec, da_spec, da_spec, dbb_spec, dbb_spec, dd_spec), out_shape,
        [pltpu.VMEM((S, SSM_SB), F32) for _ in range(4)], ("parallel", "arbitrary"), (u, dy) + tuple(mats), exs)


def _ssm_post_fwd(y_raw, proj, w_glu, b_glu, p_ssm):
    S = y_raw.shape[0]
    tr = min(T_ROWS, S)

    def body(y_ref, z_ref, wg_ref, bg_ref, p_ref, o_ref):
        g = _gelu(y_ref[...])
        t = _dot(g.astype(BF), wg_ref[...]) + bg_ref[0]
        glu = t[:, :SSM_WIDTH] * _sigmoid(t[:, SSM_WIDTH:])
        ys = glu * _silu(z_ref[...])
        o_ref[...] = _dot(ys.astype(BF), p_ref[...])

    return pl.pallas_call(
        body, name="ssm_post_fwd", grid=(S // tr,),
        in_specs=[_rows(tr, 512), _rows(tr, 512, 1), w_glu.spec(), b_glu.spec(), p_ssm.spec()],
        out_specs=_rows(tr, 1024), out_shape=jax.ShapeDtypeStruct((S, D_MODEL), F32),
        compiler_params=_cparams(("parallel",)),
    )(y_raw, proj, w_glu.arr, b_glu.arr, p_ssm.arr)


def _ssm_post_bwd(do, y_raw, proj, w_glu, b_glu, p_ssm):
    S = y_raw.shape[0]
    tr = min(T_ROWS_BWD, S)

    def body(do_ref, y_ref, z_ref, wg_ref, bg_ref, p_ref, dy_ref, dz_ref, dwg_ref, dbg_ref, dp_ref):
        @pl.when(pl.program_id(0) == 0)
        def _():
            dwg_ref[...] = jnp.zeros_like(dwg_ref)
            dbg_ref[...] = jnp.zeros_like(dbg_ref)
            dp_ref[...] = jnp.zeros_like(dp_ref)

        y = y_ref[...]
        z = z_ref[...]
        g = _gelu(y)
        gb = g.astype(BF)
        t = _dot(gb, wg_ref[...]) + bg_ref[0]
        a = t[:, :SSM_WIDTH]
        sb = _sigmoid(t[:, SSM_WIDTH:])
        glu = a * sb
        ys = glu * _silu(z)
        dob = do_ref[...].astype(BF)
        dys = _dot_nt(dob, p_ref[...])
        dp_ref[...] += _dot_tn(ys.astype(BF), dob)
        dglu = dys * _silu(z)
        dz_ref[...] = (dys * glu * _dsilu(z)).astype(dz_ref.dtype)
        dt = jnp.concatenate([dglu * sb, dglu * a * sb * (1.0 - sb)], axis=1)
        dbg_ref[...] += jnp.sum(dt, axis=0, keepdims=True)
        dtb = dt.astype(BF)
        dg = _dot_nt(dtb, wg_ref[...])
        dwg_ref[...] += _dot_tn(gb, dtb)
        dy_ref[...] = dg * _dgelu(y)

    return pl.pallas_call(
        body, name="ssm_post_bwd", grid=(S // tr,),
        in_specs=[_rows(tr, 1024), _rows(tr, 512), _rows(tr, 512, 1), w_glu.spec(), b_glu.spec(), p_ssm.spec()],
        out_specs=(_rows(tr, 512), _rows(tr, 512), _full((512, 1024)), _full((1, 1024)), _full((512, 1024))),
        out_shape=(jax.ShapeDtypeStruct((S, 512), F32), jax.ShapeDtypeStruct((S, 512), BF),
                   jax.ShapeDtypeStruct((512, 1024), F32), jax.ShapeDtypeStruct((1, 1024), F32),
                   jax.ShapeDtypeStruct((512, 1024), F32)),
        compiler_params=_cparams(("arbitrary",)),
    )(do, y_raw, proj, w_glu.arr, b_glu.arr, p_ssm.arr)


def _rope(t, c, sa, sb):
    return t * c + pltpu.roll(t, LANES - 16, 1) * sa + pltpu.roll(t, 16, 1) * sb


def _rope_t(dy, c, sa, sb):
    return dy * c + pltpu.roll(dy * sa, 16, 1) + pltpu.roll(dy * sb, LANES - 16, 1)


def _rms(x, g):
    r = lax.rsqrt(jnp.mean(x * x, axis=-1, keepdims=True) + NORM_EPS)
    return x * r * g, r


def _mla_pre_fwd(proj, q_norm, kv_norm, wuq, wk, wv, tc, tsa, tsb):
    S = proj.shape[0]
    tr = min(T_ROWS, S)

    def body(cq_ref, ckv_ref, slot_ref, qn_ref, kn_ref, wuq_ref, wk_ref, wv_ref, c_ref, sa_ref, sb_ref,
             q_out, k_out, v_out, qt_out, kt_out, vt_out):
        c, sa, sb = c_ref[...], sa_ref[...], sb_ref[...]
        qn, _ = _rms(cq_ref[...], qn_ref[0])
        q = _dot(qn.astype(BF), wuq_ref[...]) * MLA_SCALE
        kn, _ = _rms(ckv_ref[...], kn_ref[0])
        knb = kn.astype(BF)
        kp = _dot(knb, wk_ref[...])
        v = _dot(knb, wv_ref[...]).astype(BF)
        v_out[...] = v
        vt_out[...] = v.T
        kr = _rope(slot_ref[...], c, sa, sb)
        for h in range(MLA_HEADS):
            cs = slice(h * LANES, (h + 1) * LANES)
            qh = _rope(q[:, cs], c, sa, sb).astype(BF)
            kh = (kp[:, cs] + kr).astype(BF)
            q_out[:, cs] = qh
            k_out[:, cs] = kh
            qt_out[cs, :] = qh.T
            kt_out[cs, :] = kh.T

    return pl.pallas_call(
        body, name="mla_pre_fwd", grid=(S // tr,),
        in_specs=[_rows(tr, 256, 4), _rows(tr, 128, 10), _rows(tr, 128, 11), q_norm.spec(), kv_norm.spec(),
                  wuq.spec(), _full((128, 1024)), _full((128, 512)),
                  _rows(tr, 128), _rows(tr, 128), _rows(tr, 128)],
        out_specs=(_rows(tr, 1024), _rows(tr, 1024), _rows(tr, 512), _cols(1024, tr), _cols(1024, tr), _cols(512, tr)),
        out_shape=(jax.ShapeDtypeStruct((S, 1024), BF), jax.ShapeDtypeStruct((S, 1024), BF),
                   jax.ShapeDtypeStruct((S, 512), BF), jax.ShapeDtypeStruct((1024, S), BF),
                   jax.ShapeDtypeStruct((1024, S), BF), jax.ShapeDtypeStruct((512, S), BF)),
        compiler_params=_cparams(("parallel",)),
    )(proj, proj, proj, q_norm.arr, kv_norm.arr, wuq.arr, wk, wv, tc, tsa, tsb)


def _mla_pre_bwd(dq, dk, dv, proj, q_norm, kv_norm, wuq, wk, wv, tc, tsa, tsb):
    S = proj.shape[0]
    tr = min(T_ROWS_BWD, S)

    def body(dq_ref, dk_ref, dv_ref, cq_ref, ckv_ref, qn_ref, kn_ref, wuq_ref, wk_ref, wv_ref, c_ref, sa_ref, sb_ref,
             dcq_ref, dckv_ref, dslot_ref, dwuq_ref, dwk_ref, dwv_ref, dqn_ref, dkn_ref, dqp):
        @pl.when(pl.program_id(0) == 0)
        def _():
            dwuq_ref[...] = jnp.zeros_like(dwuq_ref)
            dwk_ref[...] = jnp.zeros_like(dwk_ref)
            dwv_ref[...] = jnp.zeros_like(dwv_ref)
            dqn_ref[...] = jnp.zeros_like(dqn_ref)
            dkn_ref[...] = jnp.zeros_like(dkn_ref)

        c, sa, sb = c_ref[...], sa_ref[...], sb_ref[...]
        dkr = jnp.zeros((tr, LANES), F32)
        for h in range(MLA_HEADS):
            cs = slice(h * LANES, (h + 1) * LANES)
            dqp[:, cs] = (_rope_t(dq_ref[:, cs], c, sa, sb) * MLA_SCALE).astype(BF)
            dkr = dkr + dk_ref[:, cs]
        lane = lax.broadcasted_iota(jnp.int32, (tr, LANES), 1)
        in_rope = (lane >= MLA_NOPE) & (lane < MLA_NOPE + MLA_ROPE)
        dslot_ref[...] = jnp.where(in_rope, _rope_t(dkr, c, sa, sb), 0.0).astype(dslot_ref.dtype)

        cq = cq_ref[...]
        gq = qn_ref[0]
        qn, rq = _rms(cq, gq)
        dqpb = dqp[...]
        dwuq_ref[...] += _dot_tn(qn.astype(BF), dqpb)
        dqn = _dot_nt(dqpb, wuq_ref[...])
        dqn_ref[...] += jnp.sum(dqn * cq * rq, axis=0, keepdims=True)
        dyg = dqn * gq
        dcq_ref[...] = (rq * dyg - cq * (rq * rq * rq) * jnp.mean(dyg * cq, axis=-1, keepdims=True)).astype(dcq_ref.dtype)

        ckv = ckv_ref[...]
        gk = kn_ref[0]
        kn, rk = _rms(ckv, gk)
        knb = kn.astype(BF)
        dkb = dk_ref[...].astype(BF)
        dvb = dv_ref[...].astype(BF)
        dwk_ref[...] += _dot_tn(knb, dkb)
        dwv_ref[...] += _dot_tn(knb, dvb)
        dkn = _dot_nt(dkb, wk_ref[...]) + _dot_nt(dvb, wv_ref[...])
        dkn_ref[...] += jnp.sum(dkn * ckv * rk, axis=0, keepdims=True)
        dyk = dkn * gk
        dckv_ref[...] = (rk * dyk - ckv * (rk * rk * rk) * jnp.mean(dyk * ckv, axis=-1, keepdims=True)).astype(dckv_ref.dtype)

    return pl.pallas_call(
        body, name="mla_pre_bwd", grid=(S // tr,),
        in_specs=[_rows(tr, 1024), _rows(tr, 1024), _rows(tr, 512), _rows(tr, 256, 4), _rows(tr, 128, 10),
                  q_norm.spec(), kv_norm.spec(), wuq.spec(), _full((128, 1024)), _full((128, 512)),
                  _rows(tr, 128), _rows(tr, 128), _rows(tr, 128)],
        out_specs=(_rows(tr, 256), _rows(tr, 128), _rows(tr, 128), _full((256, 1024)), _full((128, 1024)),
                   _full((128, 512)), _full((1, 256)), _full((1, 128))),
        out_shape=(jax.ShapeDtypeStruct((S, 256), BF), jax.ShapeDtypeStruct((S, 128), BF),
                   jax.ShapeDtypeStruct((S, 128), BF), jax.ShapeDtypeStruct((256, 1024), F32),
                   jax.ShapeDtypeStruct((128, 1024), F32), jax.ShapeDtypeStruct((128, 512), F32),
                   jax.ShapeDtypeStruct((1, 256), F32), jax.ShapeDtypeStruct((1, 128), F32)),
        scratch_shapes=[pltpu.VMEM((tr, 1024), BF)],
        compiler_params=_cparams(("arbitrary",)),
    )(dq, dk, dv, proj, proj, q_norm.arr, kv_norm.arr, wuq.arr, wk, wv, tc, tsa, tsb)


ANY = pl.BlockSpec(memory_space=pl.ANY)
N_REL = N_DEV - 1


def _coords():
    return lax.axis_index("x"), lax.axis_index("y"), lax.axis_index("c")


def _sem_shapes(nbuf):
    return [pltpu.SemaphoreType.DMA((N_REL * nbuf,)), pltpu.SemaphoreType.DMA((N_REL * nbuf,)),
            pltpu.SemaphoreType.DMA((nbuf,))]


def _ag_plan(srcs, dsts, sems):
    send_sems, recv_sems, _ = sems
    plan = []
    for b, (src, dst) in enumerate(zip(srcs, dsts)):
        def slot(px, py, pc, dst=dst):
            return dst.at[4 * px + 2 * py + pc]

        def copy(k, blk, to, s=None, b=b, slot=slot):
            return pltpu.make_async_remote_copy(
                src_ref=slot(*blk) if s is None else s, dst_ref=slot(*blk), send_sem=send_sems.at[N_REL * b + k],
                recv_sem=recv_sems.at[N_REL * b + k], device_id=to, device_id_type=MESH)

        plan.append((b, src, slot, copy))
    return plan


def _ag_start(srcs, dsts, sems):
    x, y, c = _coords()
    chips = [(1 - x, y), (x, 1 - y), (1 - x, 1 - y)]
    for b, src, slot, copy in _ag_plan(srcs, dsts, sems):
        pltpu.make_async_copy(src, slot(x, y, c), sems[2].at[b]).start()
        copy(0, (x, y, c), (x, y, 1 - c), src).start()
        for j, chip in enumerate(chips):
            copy(1 + j, (x, y, c), (*chip, c), src).start()


def _ag_finish(srcs, dsts, sems):
    x, y, c = _coords()
    me, sibling = (x, y, c), (x, y, 1 - c)
    chips = [(1 - x, y), (x, 1 - y), (1 - x, 1 - y)]
    plan = _ag_plan(srcs, dsts, sems)
    for b, src, slot, copy in plan:
        for j, chip in enumerate(chips):
            copy(1 + j, (*chip, c), me).wait_recv()
            copy(4 + j, (*chip, c), sibling).start()
    for b, src, slot, copy in plan:
        copy(0, sibling, me).wait_recv()
        for j, chip in enumerate(chips):
            copy(4 + j, (*chip, 1 - c), me).wait_recv()
        copy(0, me, sibling, src).wait_send()
        for j, chip in enumerate(chips):
            copy(1 + j, me, (*chip, c), src).wait_send()
            copy(4 + j, (*chip, c), sibling).wait_send()
        pltpu.make_async_copy(src, slot(*me), sems[2].at[b]).wait()


def _a2a_copies(srcs, dsts, sems):
    send_sems, recv_sems, local_sems = sems
    x, y, c = _coords()
    me = 4 * x + 2 * y + c
    local, remote = [], []
    for b, (src, dst) in enumerate(zip(srcs, dsts)):
        for rel in range(1, N_DEV):
            px = 1 - x if rel & 4 else x
            py = 1 - y if rel & 2 else y
            pc = 1 - c if rel & 1 else c
            remote.append(pltpu.make_async_remote_copy(
                src_ref=src.at[4 * px + 2 * py + pc], dst_ref=dst.at[me], send_sem=send_sems.at[N_REL * b + rel - 1],
                recv_sem=recv_sems.at[N_REL * b + rel - 1], device_id=(px, py, pc), device_id_type=MESH))
        local.append(pltpu.make_async_copy(src.at[me], dst.at[me], local_sems.at[b]))
    return local, remote


def _a2a_start(srcs, dsts, sems):
    local, remote = _a2a_copies(srcs, dsts, sems)
    for d in local + remote:
        d.start()


def _a2a_finish(srcs, dsts, sems):
    local, remote = _a2a_copies(srcs, dsts, sems)
    for d in remote + local:
        d.wait()


class _Exchange:
    def __init__(self, kind, srcs):
        self.kind, self.srcs = kind, list(srcs)
        self.n = len(self.srcs)

    def out_shapes(self):
        if self.kind == "ag":
            return [jax.ShapeDtypeStruct((N_DEV,) + s.shape, s.dtype) for s in self.srcs]
        return [jax.ShapeDtypeStruct(s.shape, s.dtype) for s in self.srcs]

    def start(self, src_refs, dst_refs, sems):
        (_ag_start if self.kind == "ag" else _a2a_start)(src_refs, dst_refs, sems)

    def finish(self, src_refs, dst_refs, sems):
        (_ag_finish if self.kind == "ag" else _a2a_finish)(src_refs, dst_refs, sems)


def _carry_call(body, name, grid, in_specs, out_specs, out_shape, scratch, semantics, args, exs):
    in_specs, out_specs, out_shape, scratch = list(in_specs), list(out_specs), list(out_shape), list(scratch)
    if not exs:
        return pl.pallas_call(body, name=name, grid=grid, in_specs=in_specs, out_specs=out_specs, out_shape=out_shape,
                              scratch_shapes=scratch, compiler_params=_cparams(semantics))(*args), []
    n_in, n_out, n_scr = len(in_specs), len(out_specs), len(scratch)
    n_ex = sum(e.n for e in exs)

    def wrapped(*refs):
        ins, refs = refs[:n_in], refs[n_in:]
        srcs, refs = refs[:n_ex], refs[n_ex:]
        outs, refs = refs[:n_out], refs[n_out:]
        dsts, refs = refs[:n_ex], refs[n_ex:]
        scr, sems = refs[:n_scr], refs[n_scr:]
        views, off = [], 0
        for i, e in enumerate(exs):
            views.append((srcs[off:off + e.n], dsts[off:off + e.n], sems[3 * i:3 * i + 3]))
            off += e.n
        first = last = None
        for axis, size in enumerate(grid):
            at0, at1 = pl.program_id(axis) == 0, pl.program_id(axis) == size - 1
            first = at0 if first is None else first & at0
            last = at1 if last is None else last & at1

        @pl.when(first)
        def _():
            for e, view in zip(exs, views):
                e.start(*view)

        body(*ins, *outs, *scr)

        @pl.when(last)
        def _():
            for e, view in zip(exs, views):
                e.finish(*view)

    res = pl.pallas_call(
        wrapped, name=name + "_x", grid=grid, in_specs=in_specs + [ANY] * n_ex, out_specs=out_specs + [ANY] * n_ex,
        out_shape=out_shape + [s for e in exs for s in e.out_shapes()],
        scratch_shapes=scratch + [s for e in exs for s in _sem_shapes(e.n)],
        compiler_params=_cparams(("arbitrary",) * len(grid)))(*args, *[s for e in exs for s in e.srcs])
    got, off = [], n_out
    for e in exs:
        got.append(list(res[off:off + e.n]))
        off += e.n
    return res[:n_out], got


def _exchange_call(name, exs):
    tot = sum(e.n for e in exs)

    def body(*refs):
        srcs, dsts, sems = refs[:tot], refs[tot:2 * tot], refs[2 * tot:]
        views, off = [], 0
        for i, e in enumerate(exs):
            views.append((srcs[off:off + e.n], dsts[off:off + e.n], sems[3 * i:3 * i + 3]))
            off += e.n
        for e, view in zip(exs, views):
            e.start(*view)
        for e, view in zip(exs, views):
            e.finish(*view)

    outs = pl.pallas_call(
        body, name=name, in_specs=[ANY] * tot, out_specs=[ANY] * tot,
        out_shape=[s for e in exs for s in e.out_shapes()],
        scratch_shapes=[s for e in exs for s in _sem_shapes(e.n)],
    )(*[s for e in exs for s in e.srcs])
    res, off = [], 0
    for e in exs:
        res.append(list(outs[off:off + e.n]))
        off += e.n
    return res


def _flash_call(body, name, exs, in_specs, out_specs, out_shape, scratch, n, args):
    return _carry_call(body, name, (MLA_HEADS // 2, n, n), in_specs, out_specs, out_shape, scratch,
                       ("parallel", "parallel", "arbitrary"), args, exs)


def _pair_rows(a):
    at = a.T
    return jnp.concatenate([at[0:1, :], at[MLA_V:MLA_V + 1, :], jnp.zeros((SUBLANES - 2, a.shape[0]), a.dtype)], axis=0)


def _lower_tri(t):
    return lax.broadcasted_iota(jnp.int32, (t, t), 0) >= lax.broadcasted_iota(jnp.int32, (t, t), 1)


def _upper_tri(t):
    return lax.broadcasted_iota(jnp.int32, (t, t), 1) >= lax.broadcasted_iota(jnp.int32, (t, t), 0)


def _flash_fwd(q, kt, v, exs=()):
    S = q.shape[0]
    t = min(T_ATT, S)
    n = S // t

    def body(q_ref, kt_ref, v_ref, o_ref, lse_ref, lse_t_ref, m_s, l_s, acc):
        qi, ki = pl.program_id(1), pl.program_id(2)
        lo = lax.broadcasted_iota(jnp.int32, (t, LANES), 1) < MLA_V

        @pl.when(ki == 0)
        def _():
            m_s[...] = jnp.full_like(m_s, NEG)
            l_s[...] = jnp.zeros_like(l_s)
            acc[...] = jnp.zeros_like(acc)

        @pl.when(ki <= qi)
        def _():
            keep = _lower_tri(t) | (ki < qi)
            vv = v_ref[...]
            heads = range(2)
            ss = [jnp.where(keep, _dot(q_ref[:, h * LANES:(h + 1) * LANES], kt_ref[h * LANES:(h + 1) * LANES, :]), NEG)
                  for h in heads]
            m_prev = [m_s[h] for h in heads]
            l_prev = [l_s[h] for h in heads]
            m_new = [jnp.maximum(m_prev[h], jnp.max(ss[h], axis=1, keepdims=True)) for h in heads]
            al = [jnp.exp(m_prev[h] - m_new[h]) for h in heads]
            ps = [jnp.exp(ss[h] - m_new[h][:, :1]) for h in heads]
            l_new = [al[h] * l_prev[h] + jnp.sum(ps[h], axis=1, keepdims=True) for h in heads]
            pv = [_dot(ps[h].astype(BF), vv) for h in heads]
            for h in heads:
                m_s[h] = m_new[h]
                l_s[h] = l_new[h]
            acc[...] = jnp.where(lo, al[0], al[1]) * acc[...] + jnp.where(lo, pv[0], pv[1])

        @pl.when(ki == qi)
        def _():
            o_ref[...] = acc[...] / jnp.where(lo, l_s[0], l_s[1])
            lse = jnp.where(lo, m_s[0] + jnp.log(l_s[0]), m_s[1] + jnp.log(l_s[1]))
            lse_ref[0] = lse
            lse_t_ref[0] = _pair_rows(lse)

    return _flash_call(
        body, "mla_flash_fwd", exs,
        [pl.BlockSpec((t, 256), lambda p, i, j: (i, p)),
         pl.BlockSpec((256, t), lambda p, i, j: (p, jnp.minimum(j, i))),
         pl.BlockSpec((t, 128), lambda p, i, j: (jnp.minimum(j, i), p))],
        [pl.BlockSpec((t, 128), lambda p, i, j: (i, p)), pl.BlockSpec((1, t, 128), lambda p, i, j: (p, i, 0)),
         pl.BlockSpec((1, SUBLANES, t), lambda p, i, j: (p, 0, i))],
        [jax.ShapeDtypeStruct((S, 512), F32), jax.ShapeDtypeStruct((MLA_HEADS // 2, S, 128), F32),
         jax.ShapeDtypeStruct((MLA_HEADS // 2, SUBLANES, S), F32)],
        [pltpu.VMEM((2, t, 128), F32), pltpu.VMEM((2, t, 128), F32), pltpu.VMEM((t, 128), F32)], n, (q, kt, v))


def _flash_bwd_dq(q, k, kt, vt, do, lse, delta, exs=()):
    S = q.shape[0]
    t = min(T_ATT, S)
    n = S // t

    def body(q_ref, k_ref, kt_ref, vt_ref, do_ref, lse_ref, dl_ref, dq_ref, acc):
        qi, ki = pl.program_id(1), pl.program_id(2)
        lo = lax.broadcasted_iota(jnp.int32, (t, LANES), 1) < MLA_V

        @pl.when(ki == 0)
        def _():
            acc[...] = jnp.zeros_like(acc)

        @pl.when(ki <= qi)
        def _():
            keep = _lower_tri(t) | (ki < qi)
            heads = range(2)
            cs = [slice(h * LANES, (h + 1) * LANES) for h in heads]
            col = [slice(h * MLA_V, h * MLA_V + 1) for h in heads]
            lse, dl, dov, vt = lse_ref[0], dl_ref[0], do_ref[...], vt_ref[...]
            ss = [jnp.where(keep, _dot(q_ref[:, cs[h]], kt_ref[cs[h], :]), NEG) for h in heads]
            dp = [_dot(jnp.where(lo if h == 0 else ~lo, dov, 0).astype(BF), vt) for h in heads]
            ds = [(jnp.exp(ss[h] - lse[:, col[h]]) * (dp[h] - dl[:, col[h]])).astype(BF) for h in heads]
            dq = [_dot(ds[h], k_ref[:, cs[h]]) for h in heads]
            acc[...] += jnp.concatenate(dq, axis=1)

        @pl.when(ki == qi)
        def _():
            dq_ref[...] = acc[...]

    (dq,), got = _flash_call(
        body, "mla_flash_dq", exs,
        [pl.BlockSpec((t, 256), lambda p, i, j: (i, p)),
         pl.BlockSpec((t, 256), lambda p, i, j: (jnp.minimum(j, i), p)),
         pl.BlockSpec((256, t), lambda p, i, j: (p, jnp.minimum(j, i))),
         pl.BlockSpec((128, t), lambda p, i, j: (p, jnp.minimum(j, i))),
         pl.BlockSpec((t, 128), lambda p, i, j: (i, p)),
         pl.BlockSpec((1, t, 128), lambda p, i, j: (p, i, 0)),
         pl.BlockSpec((1, t, 128), lambda p, i, j: (p, i, 0))],
        [pl.BlockSpec((t, 256), lambda p, i, j: (i, p))],
        [jax.ShapeDtypeStruct((S, 1024), F32)],
        [pltpu.VMEM((t, 256), F32)], n, (q, k, kt, vt, do, lse, delta))
    return dq, got


def _flash_bwd_dkv(q, qt, k, v, do, dot_, lse_t, delta_t, exs=()):
    S = q.shape[0]
    t = min(T_ATT, S)
    n = S // t

    def body(q_ref, qt_ref, k_ref, v_ref, do_ref, dot_ref, lse_ref, dl_ref, dk_ref, dv_ref, dk_acc, dv_acc):
        ki, qi = pl.program_id(1), pl.program_id(2)
        lo = lax.broadcasted_iota(jnp.int32, (t, LANES), 1) < MLA_V
        top = lax.broadcasted_iota(jnp.int32, (LANES, t), 0) < MLA_V

        @pl.when(qi == 0)
        def _():
            dk_acc[...] = jnp.zeros_like(dk_acc)
            dv_acc[...] = jnp.zeros_like(dv_acc)

        @pl.when(qi >= ki)
        def _():
            keep = _upper_tri(t) | (qi > ki)
            heads = range(2)
            cs = [slice(h * LANES, (h + 1) * LANES) for h in heads]
            vv, lse, dl, dov, dot_v = v_ref[...], lse_ref[0], dl_ref[0], do_ref[...], dot_ref[...]
            st = [jnp.where(keep, _dot(k_ref[:, cs[h]], qt_ref[cs[h], :]), NEG) for h in heads]
            dpt = [_dot(vv, jnp.where(top if h == 0 else ~top, dot_v, 0).astype(BF)) for h in heads]
            pt = [jnp.exp(st[h] - lse[h:h + 1, :]) for h in heads]
            dst = [(pt[h] * (dpt[h] - dl[h:h + 1, :])).astype(BF) for h in heads]
            dv = [_dot(pt[h].astype(BF), jnp.where(lo if h == 0 else ~lo, dov, 0).astype(BF)) for h in heads]
            dk = [_dot(dst[h], q_ref[:, cs[h]]) for h in heads]
            dv_acc[...] += dv[0] + dv[1]
            dk_acc[...] += jnp.concatenate(dk, axis=1)

        @pl.when(qi == n - 1)
        def _():
            dk_ref[...] = dk_acc[...]
            dv_ref[...] = dv_acc[...]

    (dk, dv), got = _flash_call(
        body, "mla_flash_dkv", exs,
        [pl.BlockSpec((t, 256), lambda p, j, i: (jnp.maximum(i, j), p)),
         pl.BlockSpec((256, t), lambda p, j, i: (p, jnp.maximum(i, j))),
         pl.BlockSpec((t, 256), lambda p, j, i: (j, p)),
         pl.BlockSpec((t, 128), lambda p, j, i: (j, p)),
         pl.BlockSpec((t, 128), lambda p, j, i: (jnp.maximum(i, j), p)),
         pl.BlockSpec((128, t), lambda p, j, i: (p, jnp.maximum(i, j))),
         pl.BlockSpec((1, SUBLANES, t), lambda p, j, i: (p, 0, jnp.maximum(i, j))),
         pl.BlockSpec((1, SUBLANES, t), lambda p, j, i: (p, 0, jnp.maximum(i, j)))],
        [pl.BlockSpec((t, 256), lambda p, j, i: (j, p)), pl.BlockSpec((t, 128), lambda p, j, i: (j, p))],
        [jax.ShapeDtypeStruct((S, 1024), F32), jax.ShapeDtypeStruct((S, 512), F32)],
        [pltpu.VMEM((t, 256), F32), pltpu.VMEM((t, 128), F32)], n, (q, qt, k, v, do, dot_, lse_t, delta_t))
    return dk, dv, got


def _mem_heads(qm, km_ref, vm_ref):
    ps, os_ = [], []
    for h in range(X_HEADS):
        cs = slice(h * X_HEAD_DIM, (h + 1) * X_HEAD_DIM)
        s = _dot_nt(qm[:, cs].astype(BF), km_ref[:, cs]) * MEM_SCALE
        e = jnp.exp(s - jnp.max(s, axis=1, keepdims=True))
        p = e / jnp.sum(e, axis=1, keepdims=True)
        ps.append(p)
        os_.append(_dot(p.astype(BF), vm_ref[:, cs]))
    return ps, jnp.concatenate(os_, axis=1)


def _mem_fwd(proj, km, vm, p_mem):
    S = proj.shape[0]
    tr = min(T_ROWS, S)
    M = km.shape[0]

    def body(q_ref, z_ref, km_ref, vm_ref, p_ref, o_ref):
        _, o = _mem_heads(q_ref[...], km_ref, vm_ref)
        y = o * _silu(z_ref[...])
        o_ref[...] = _dot(y.astype(BF), p_ref[...])

    return pl.pallas_call(
        body, name="mem_fwd", grid=(S // tr,),
        in_specs=[_rows(tr, 512, 4), _rows(tr, 512, 5), _full((M, 512)), _full((M, 512)), p_mem.spec()],
        out_specs=_rows(tr, 1024), out_shape=jax.ShapeDtypeStruct((S, D_MODEL), F32),
        compiler_params=_cparams(("parallel",)),
    )(proj, proj, km, vm, p_mem.arr)


def _mem_bwd(do, proj, km, vm, p_mem):
    S = proj.shape[0]
    tr = min(T_ROWS_BWD, S)
    M = km.shape[0]

    def body(do_ref, q_ref, z_ref, km_ref, vm_ref, p_ref, dq_ref, dz_ref, dkm_ref, dvm_ref, dp_ref):
        @pl.when(pl.program_id(0) == 0)
        def _():
            dkm_ref[...] = jnp.zeros_like(dkm_ref)
            dvm_ref[...] = jnp.zeros_like(dvm_ref)
            dp_ref[...] = jnp.zeros_like(dp_ref)

        qm = q_ref[...]
        z = z_ref[...]
        ps, o = _mem_heads(qm, km_ref, vm_ref)
        sz = _silu(z)
        y = o * sz
        dob = do_ref[...].astype(BF)
        dy = _dot_nt(dob, p_ref[...])
        dp_ref[...] += _dot_tn(y.astype(BF), dob)
        dz_ref[...] = (dy * o * _dsilu(z)).astype(dz_ref.dtype)
        d_o = dy * sz
        for h in range(X_HEADS):
            cs = slice(h * X_HEAD_DIM, (h + 1) * X_HEAD_DIM)
            doh = d_o[:, cs]
            dohb = doh.astype(BF)
            p = ps[h]
            dpr = _dot_nt(dohb, vm_ref[:, cs])
            ds = (p * (dpr - jnp.sum(doh * o[:, cs], axis=1, keepdims=True)) * MEM_SCALE).astype(BF)
            dq_ref[:, cs] = _dot(ds, km_ref[:, cs]).astype(dq_ref.dtype)
            dkm_ref[:, cs] += _dot_tn(ds, qm[:, cs].astype(BF))
            dvm_ref[:, cs] += _dot_tn(p.astype(BF), dohb)

    return pl.pallas_call(
        body, name="mem_bwd", grid=(S // tr,),
        in_specs=[_rows(tr, 1024), _rows(tr, 512, 4), _rows(tr, 512, 5), _full((M, 512)), _full((M, 512)),
                  p_mem.spec()],
        out_specs=(_rows(tr, 512), _rows(tr, 512), _full((M, 512)), _full((M, 512)), _full((512, 1024))),
        out_shape=(jax.ShapeDtypeStruct((S, 512), BF), jax.ShapeDtypeStruct((S, 512), BF),
                   jax.ShapeDtypeStruct((M, 512), F32), jax.ShapeDtypeStruct((M, 512), F32),
                   jax.ShapeDtypeStruct((512, 1024), F32)),
        compiler_params=_cparams(("arbitrary",)),
    )(do, proj, proj, km, vm, p_mem.arr)


def _merge_fwd(x, proj, o_ssm, o_att, o_mem, b_gate, p_mla, w_out, ln_g, ln_b):
    S = x.shape[0]
    tr = min(T_ROWS, S)

    def body(x_ref, lg_ref, z_ref, os_ref, oa_ref, om_ref, bg_ref, p_ref, w_ref, g_ref, b_ref,
             xn_ref, xb_ref, pre_ref, mg_ref):
        gates = _sigmoid(lg_ref[...] + bg_ref[0])
        ya = oa_ref[...] * _silu(z_ref[...])
        o_mla = _dot(ya.astype(BF), p_ref[...])
        merged = (gates[:, :D_MODEL] * os_ref[...] + gates[:, D_MODEL:2 * D_MODEL] * o_mla
                  + gates[:, 2 * D_MODEL:] * om_ref[...])
        mb = merged.astype(BF)
        mg_ref[...] = mb
        pre = ALPHA * x_ref[...] + _dot(mb, w_ref[...])
        pre_ref[...] = pre
        mu = jnp.mean(pre, axis=-1, keepdims=True)
        xc = pre - mu
        var = jnp.mean(xc * xc, axis=-1, keepdims=True)
        xn = xc * lax.rsqrt(var + NORM_EPS) * g_ref[0] + b_ref[0]
        xn_ref[...] = xn
        xb_ref[...] = xn.astype(BF)

    return pl.pallas_call(
        body, name="merge_fwd", grid=(S // tr,),
        in_specs=[_rows(tr, 1024), _rows(tr, 3072, 1), _rows(tr, 512, 3), _rows(tr, 1024), _rows(tr, 512),
                  _rows(tr, 1024), b_gate.spec(), p_mla.spec(), _full((1024, 1024)), ln_g.spec(), ln_b.spec()],
        out_specs=(_rows(tr, 1024), _rows(tr, 1024), _rows(tr, 1024), _rows(tr, 1024)),
        out_shape=(jax.ShapeDtypeStruct((S, 1024), F32), jax.ShapeDtypeStruct((S, 1024), BF),
                   jax.ShapeDtypeStruct((S, 1024), F32), jax.ShapeDtypeStruct((S, 1024), BF)),
        compiler_params=_cparams(("parallel",)),
    )(x, proj, proj, o_ssm, o_att, o_mem, b_gate.arr, p_mla.arr, w_out, ln_g.arr, ln_b.arr)


def _merge_bwd(dxn, pre, merged, proj, o_ssm, o_att, o_mem, b_gate, p_mla, w_out, ln_g):
    S = pre.shape[0]
    tr = min(T_ROWS_BWD, S)

    def body(dxn_ref, pre_ref, mg_ref, lg_ref, z_ref, os_ref, oa_ref, om_ref, bg_ref, p_ref, w_ref, g_ref,
             dxr_ref, dlg_ref, dos_ref, dom_ref, doa_ref, dz_ref, doat_ref, dl_ref, dlt_ref, dw_ref, dp_ref, dbg_ref,
             dg_ref, db_ref):
        @pl.when(pl.program_id(0) == 0)
        def _():
            dw_ref[...] = jnp.zeros_like(dw_ref)
            dp_ref[...] = jnp.zeros_like(dp_ref)
            dbg_ref[...] = jnp.zeros_like(dbg_ref)
            dg_ref[...] = jnp.zeros_like(dg_ref)
            db_ref[...] = jnp.zeros_like(db_ref)

        dxn = dxn_ref[...]
        pre = pre_ref[...]
        mu = jnp.mean(pre, axis=-1, keepdims=True)
        xc = pre - mu
        rstd = lax.rsqrt(jnp.mean(xc * xc, axis=-1, keepdims=True) + NORM_EPS)
        xhat = xc * rstd
        dg_ref[...] += jnp.sum(dxn * xhat, axis=0, keepdims=True)
        db_ref[...] += jnp.sum(dxn, axis=0, keepdims=True)
        dxh = dxn * g_ref[0]
        dpre = rstd * (dxh - jnp.mean(dxh, axis=-1, keepdims=True)
                       - xhat * jnp.mean(dxh * xhat, axis=-1, keepdims=True))
        dxr_ref[...] = ALPHA * dpre
        dpb = dpre.astype(BF)
        dw_ref[...] += _dot_tn(mg_ref[...], dpb)
        dm = _dot_nt(dpb, w_ref[...])

        gates = _sigmoid(lg_ref[...] + bg_ref[0])
        g0, g1, g2 = gates[:, :D_MODEL], gates[:, D_MODEL:2 * D_MODEL], gates[:, 2 * D_MODEL:]
        z = z_ref[...]
        oa = oa_ref[...]
        sz = _silu(z)
        ya = (oa * sz).astype(BF)
        o_mla = _dot(ya, p_ref[...])
        dos_ref[...] = (g0 * dm).astype(dos_ref.dtype)
        dom_ref[...] = (g2 * dm).astype(dom_ref.dtype)
        do_mla = (g1 * dm).astype(BF)
        dl0 = dm * os_ref[...] * g0 * (1.0 - g0)
        dl1 = dm * o_mla * g1 * (1.0 - g1)
        dl2 = dm * om_ref[...] * g2 * (1.0 - g2)
        dl = jnp.concatenate([dl0, dl1, dl2], axis=1)
        dbg_ref[...] += jnp.sum(dl, axis=0, keepdims=True)
        dlg_ref[...] = dl.astype(dlg_ref.dtype)
        dp_ref[...] += _dot_tn(ya, do_mla)
        dya = _dot_nt(do_mla, p_ref[...])
        doa = dya * sz
        doab = doa.astype(BF)
        doa_ref[...] = doab
        doat_ref[...] = doab.T
        dz_ref[...] = (dya * oa * _dsilu(z)).astype(dz_ref.dtype)
        prod = doa * oa
        lo = lax.broadcasted_iota(jnp.int32, (tr, LANES), 1) < MLA_V
        for pr in range(MLA_HEADS // 2):
            blk = prod[:, pr * LANES:(pr + 1) * LANES]
            d0 = jnp.sum(jnp.where(lo, blk, 0.0), axis=1, keepdims=True)
            d1 = jnp.sum(jnp.where(lo, 0.0, blk), axis=1, keepdims=True)
            dl = jnp.where(lo, d0, d1)
            dl_ref[pr] = dl
            dlt_ref[pr] = _pair_rows(dl)

    return pl.pallas_call(
        body, name="merge_bwd", grid=(S // tr,),
        in_specs=[_rows(tr, 1024), _rows(tr, 1024), _rows(tr, 1024), _rows(tr, 3072, 1), _rows(tr, 512, 3),
                  _rows(tr, 1024), _rows(tr, 512), _rows(tr, 1024), b_gate.spec(), p_mla.spec(),
                  _full((1024, 1024)), ln_g.spec()],
        out_specs=(_rows(tr, 1024), _rows(tr, 3072), _rows(tr, 1024), _rows(tr, 1024), _rows(tr, 512),
                   _rows(tr, 512), _cols(512, tr), pl.BlockSpec((MLA_HEADS // 2, tr, 128), lambda i: (0, i, 0)),
                   pl.BlockSpec((MLA_HEADS // 2, SUBLANES, tr), lambda i: (0, 0, i)),
                   _full((1024, 1024)), _full((512, 1024)), _full((1, 3072)), _full((1, 1024)), _full((1, 1024))),
        out_shape=(jax.ShapeDtypeStruct((S, 1024), F32), jax.ShapeDtypeStruct((S, 3072), BF),
                   jax.ShapeDtypeStruct((S, 1024), BF), jax.ShapeDtypeStruct((S, 1024), BF),
                   jax.ShapeDtypeStruct((S, 512), BF), jax.ShapeDtypeStruct((S, 512), BF),
                   jax.ShapeDtypeStruct((512, S), BF), jax.ShapeDtypeStruct((MLA_HEADS // 2, S, 128), F32),
                   jax.ShapeDtypeStruct((MLA_HEADS // 2, SUBLANES, S), F32),
                   jax.ShapeDtypeStruct((1024, 1024), F32), jax.ShapeDtypeStruct((512, 1024), F32),
                   jax.ShapeDtypeStruct((1, 3072), F32), jax.ShapeDtypeStruct((1, 1024), F32),
                   jax.ShapeDtypeStruct((1, 1024), F32)),
        compiler_params=_cparams(("arbitrary",)),
    )(dxn, pre, merged, proj, proj, o_ssm, o_att, o_mem, b_gate.arr, p_mla.arr, w_out, ln_g.arr)


def _loss_head(y, t):
    S = y.shape[0]
    tr = min(T_ROWS, S)
    n = S // tr

    def body(y_ref, t_ref, dy_ref, l_ref, acc):
        i = pl.program_id(0)

        @pl.when(i == 0)
        def _():
            acc[...] = jnp.zeros_like(acc)

        e = y_ref[...] - t_ref[...]
        dy_ref[...] = e * (1.0 / D_MODEL)
        acc[...] += jnp.sum(e * e, axis=0, keepdims=True)

        @pl.when(i == n - 1)
        def _():
            tot = jnp.sum(acc[...], axis=1, keepdims=True) * (0.5 / D_MODEL)
            l_ref[...] = jnp.broadcast_to(tot, l_ref.shape)

    return pl.pallas_call(
        body, name="loss_head", grid=(n,),
        in_specs=[_rows(tr, 1024), _rows(tr, 1024)],
        out_specs=(_rows(tr, 1024), _full((SUBLANES, LANES))),
        out_shape=(jax.ShapeDtypeStruct((S, 1024), F32), jax.ShapeDtypeStruct((SUBLANES, LANES), F32)),
        scratch_shapes=[pltpu.VMEM((1, 1024), F32)],
        compiler_params=_cparams(("arbitrary",)),
    )(y, t)


def _rope_tables(pos):
    inv_freq = ROPE_THETA ** (-jnp.arange(0, MLA_ROPE, 2, dtype=F32) / MLA_ROPE)
    ang = pos.astype(F32)[:, None] * inv_freq
    cos, sin = jnp.cos(ang), jnp.sin(ang)
    S = pos.shape[0]
    half = MLA_ROPE // 2
    ones = jnp.ones((S, MLA_NOPE), F32)
    z16 = jnp.zeros((S, half), F32)
    z32 = jnp.zeros((S, LANES - MLA_NOPE - MLA_ROPE), F32)
    z64 = jnp.zeros((S, MLA_NOPE), F32)
    c = jnp.concatenate([ones, cos, cos, z32], axis=1)
    sa = jnp.concatenate([z64, -sin, z16, z32], axis=1)
    sb = jnp.concatenate([z64, z16, sin, z32], axis=1)
    return c, sa, sb


def _ssm_discretise(a_re, a_im, log_dt, b_re, b_im):
    dt = jnp.exp(log_dt)[..., None]
    mag = jnp.exp(a_re * dt)
    lb_re = mag * jnp.cos(a_im * dt)
    lb_im = mag * jnp.sin(a_im * dt)
    nr, ni = lb_re - 1.0, lb_im
    den = a_re * a_re + a_im * a_im
    f_re = (nr * a_re + ni * a_im) / den
    f_im = (ni * a_re - nr * a_im) / den
    bb_re = f_re[..., None] * b_re - f_im[..., None] * b_im
    bb_im = f_re[..., None] * b_im + f_im[..., None] * b_re
    return lb_re, lb_im, bb_re, bb_im


_GPB = SSM_CB // SSM_GROUP


def _bd_in(bb):
    nb = SSM_GROUPS // _GPB
    t = bb.reshape(nb, _GPB, SSM_STATE, SSM_GROUP)
    eye = jnp.eye(_GPB, dtype=bb.dtype)
    return jnp.einsum("ngpc,gh->ngchp", t, eye).reshape(nb, SSM_CB, _GPB * SSM_STATE)


def _bd_in_t(d):
    nb = SSM_GROUPS // _GPB
    t = d.reshape(nb, _GPB, SSM_GROUP, _GPB, SSM_STATE)
    eye = jnp.eye(_GPB, dtype=d.dtype)
    return jnp.einsum("ngchp,gh->ngpc", t, eye).reshape(SSM_GROUPS, SSM_STATE, SSM_GROUP)


def _bd_out(c):
    nb = SSM_GROUPS // _GPB
    t = c.reshape(nb, _GPB, SSM_GROUP, SSM_STATE)
    eye = jnp.eye(_GPB, dtype=c.dtype)
    return jnp.einsum("ngcp,gh->ngphc", t, eye).reshape(nb, _GPB * SSM_STATE, SSM_CB)


def _interleave(a):
    S, w = a.shape
    return a.reshape(SUBLANES, S // SUBLANES, w).transpose(1, 0, 2).reshape(S, w)


def _deinterleave(a):
    S, w = a.shape
    return a.reshape(S // SUBLANES, SUBLANES, w).transpose(1, 0, 2).reshape(S, w)


IN_SHARD = D_IN // N_DEV
ROPE_OWNER = ROPE_SLOT_LO // IN_SHARD
assert ROPE_OWNER * IN_SHARD <= ROPE_SLOT_LO and ROPE_SLOT_LO + MLA_ROPE <= (ROPE_OWNER + 1) * IN_SHARD


def _w_in_from_shards(g):
    pieces = []
    for j in range(N_DEV):
        if j == ROPE_OWNER:
            a = ROPE_SLOT_LO - j * IN_SHARD
            z = lambda n: jnp.zeros((g.shape[1], n), g.dtype)
            pieces += [g[j][:, :a], z(MLA_NOPE), g[j][:, a:a + MLA_ROPE], z(LANES - MLA_NOPE - MLA_ROPE),
                       g[j][:, a + MLA_ROPE:]]
        else:
            pieces.append(g[j])
    return jnp.concatenate(pieces, axis=1)


def _w_in_to_shards(d):
    shift = LANES - MLA_ROPE
    out = []
    for j in range(N_DEV):
        lo, hi = j * IN_SHARD, (j + 1) * IN_SHARD
        if j < ROPE_OWNER:
            out.append(d[:, lo:hi])
        elif j > ROPE_OWNER:
            out.append(d[:, lo + shift:hi + shift])
        else:
            r = ROPE_SLOT_LO + MLA_NOPE
            out.append(jnp.concatenate([d[:, lo:ROPE_SLOT_LO], d[:, r:r + MLA_ROPE],
                                        d[:, ROPE_SLOT_LO + LANES:hi + shift]], axis=1))
    return jnp.stack(out)


def _adamw_math(w, g, m, v):
    m = ADAM_B1 * m + (1.0 - ADAM_B1) * g
    v = ADAM_B2 * v + (1.0 - ADAM_B2) * (g * g)
    m_hat = m / (1.0 - ADAM_B1 ** ADAM_STEP)
    v_hat = v / (1.0 - ADAM_B2 ** ADAM_STEP)
    delta = -ADAM_LR * (m_hat / (jnp.sqrt(v_hat) + ADAM_EPS) + ADAM_WD * w)
    return delta, m, v


def _adamw_sharded(parts, w, m, v, tile, name):
    L, _, R, C = parts.shape
    assert R % tile == 0

    def body(p_ref, w_ref, m_ref, v_ref, g_out, d_out, m_out, v_out):
        g = p_ref[0, 0].astype(F32)
        for k in range(1, N_DEV):
            g = g + p_ref[0, k].astype(F32)
        d, mn, vn = _adamw_math(w_ref[0], g, m_ref[0], v_ref[0])
        g_out[0] = g
        d_out[0] = d
        m_out[0] = mn
        v_out[0] = vn

    spec = pl.BlockSpec((1, tile, C), lambda l, i: (l, i, 0))
    shp = jax.ShapeDtypeStruct((L, R, C), F32)
    return pl.pallas_call(
        body, name=name, grid=(L, R // tile),
        in_specs=[pl.BlockSpec((1, N_DEV, tile, C), lambda l, i: (l, 0, i, 0)), spec, spec, spec],
        out_specs=(spec,) * 4, out_shape=(shp,) * 4, compiler_params=_cparams(("parallel", "parallel")),
    )(parts, w, m, v)


COL_GROUP = (("w_glu", 512), ("p_ssm", 512), ("p_mla", 512), ("p_mem", 512), ("w_uq", 256), ("w_ukv", 128))
COL_AT = {n: sum(r for _, r in COL_GROUP[:i]) // rows for i, (n, rows) in enumerate(COL_GROUP)}
assert all(sum(r for _, r in COL_GROUP[:i]) % rows == 0 for i, (_, rows) in enumerate(COL_GROUP))
COL_ROWS = dict(COL_GROUP)
ROW_GROUP = ("w_mem_kv", "w_out")
SMALL = ("b_gate", "ssm_a_re", "ssm_a_im", "ssm_log_dt", "ssm_b_re", "ssm_b_im", "ssm_c_re", "ssm_c_im", "ssm_d",
         "b_glu", "mla_q_norm", "mla_kv_norm", "ln_g", "ln_b")
SMALL_TILE = 512
UQ_COLS = MLA_NOPE + MLA_ROPE


def _pad_lanes(a):
    return jnp.concatenate([a, jnp.zeros(a.shape[:-1] + (LANES - a.shape[-1],), a.dtype)], axis=-1)


def _group_buffers(d, dtype):
    col = jnp.concatenate([_pad_lanes(d[n]) if n == "w_uq" else d[n] for n, _ in COL_GROUP], axis=1)
    row = jnp.concatenate([d[n] for n in ROW_GROUP], axis=1)
    return d["w_in"].astype(dtype), col.astype(dtype), row.astype(dtype)


def _ungroup(bufs):
    b_in, col, row = bufs
    out, off = {"w_in": b_in}, 0
    for n, rows in COL_GROUP:
        t = col[:, off:off + rows]
        out[n] = t[..., :UQ_COLS] if n == "w_uq" else t
        off += rows
    k = row.shape[1] // 2
    out["w_mem_kv"], out["w_out"] = row[:, :k], row[:, k:]
    return out


def _colcat(t):
    return t.transpose(1, 0, 2).reshape(t.shape[1], -1)


def _colsplit(g, n):
    return g.reshape(g.shape[0], N_DEV, n).transpose(1, 0, 2)


def _unpack_weights(g_in, g_col, g_row):
    wc = _colcat(g_col)
    at = lambda n: _RowBlock(wc, COL_ROWS[n], COL_AT[n])
    lo = COL_AT["w_ukv"] * COL_ROWS["w_ukv"]
    ukv = wc[lo:lo + COL_ROWS["w_ukv"]].reshape(-1, MLA_HEADS, LANES)
    lane = lax.broadcasted_iota(jnp.int32, ukv.shape, 2)
    k = g_row.shape[1] // 2
    return dict(
        w_in=_w_in_from_shards(g_in), w_glu=at("w_glu"), w_uq=at("w_uq"), p_ssm=at("p_ssm"), p_mla=at("p_mla"),
        p_mem=at("p_mem"), w_k=jnp.where(lane < MLA_NOPE, ukv, jnp.zeros_like(ukv)).reshape(ukv.shape[0], -1),
        w_v=ukv[..., MLA_NOPE:].reshape(ukv.shape[0], -1),
        w_mem_kv=g_row[:, :k].reshape(-1, g_row.shape[2]), w_out=g_row[:, k:].reshape(-1, g_row.shape[2]))


def _pack_grads_in(d_w_in):
    return _w_in_to_shards(d_w_in).astype(BF)


def _pack_grads_rest(d):
    ukv = jnp.concatenate([d["w_k"].reshape(-1, MLA_HEADS, LANES)[..., :MLA_NOPE],
                           d["w_v"].reshape(-1, MLA_HEADS, MLA_V)], axis=-1).reshape(d["w_k"].shape[0], -1)
    col = jnp.concatenate([ukv if n == "w_ukv" else d[n] for n, _ in COL_GROUP], axis=0)
    row = jnp.concatenate([d[n].reshape(N_DEV, -1, d[n].shape[1]) for n in ROW_GROUP], axis=1)
    return [_colsplit(col, LANES).astype(BF), row.astype(BF)]


def _pack_small(d):
    parts = []
    for n in SMALL:
        f = d[n].reshape(-1)
        pad = (-f.shape[0]) % (SUBLANES * LANES)
        if pad:
            f = jnp.concatenate([f, jnp.zeros((pad,), f.dtype)])
        parts.append(f.reshape(-1, LANES))
    rows = sum(p.shape[0] for p in parts)
    pad = (-rows) % SMALL_TILE
    if pad:
        parts.append(jnp.zeros((pad, LANES), parts[0].dtype))
    return jnp.concatenate(parts, axis=0)


def _unpack_small(buf, like):
    out, off = {}, 0
    for n in SMALL:
        size = math.prod(like[n].shape)
        rows = -(-size // (SUBLANES * LANES)) * SUBLANES
        out[n] = buf[off:off + rows].reshape(-1)[:size].reshape(like[n].shape)
        off += rows
    return out


WEIGHTS = ("w_in", "b_gate", "ssm_a_re", "ssm_a_im", "ssm_log_dt", "ssm_b_re", "ssm_b_im", "ssm_c_re", "ssm_c_im",
           "ssm_d", "w_glu", "b_glu", "mla_q_norm", "w_uq", "mla_kv_norm", "w_ukv", "w_mem_kv", "p_ssm", "p_mla",
           "p_mem", "w_out", "ln_g", "ln_b")
BIG = ("w_in",) + tuple(n for n, _ in COL_GROUP) + ROW_GROUP


def _train_step(x, mem, pos, target, wl, ws):
    S = x.shape[0]
    tc, tsa, tsb = _rope_tables(pos)
    loc = _group_buffers(wl, BF)
    loc = [[b[l] for b in loc] for l in range(DEPTH)]

    lb_re, lb_im, bb_re, bb_im = _ssm_discretise(ws["ssm_a_re"], ws["ssm_a_im"], ws["ssm_log_dt"], ws["ssm_b_re"],
                                                 ws["ssm_b_im"])
    nb = SSM_GROUPS // _GPB
    mats = (jax.vmap(_bd_in)(bb_re), jax.vmap(_bd_in)(bb_im), lb_re.reshape(DEPTH, nb, 1, -1),
            lb_im.reshape(DEPTH, nb, 1, -1), jax.vmap(_bd_out)(ws["ssm_c_re"]), jax.vmap(_bd_out)(ws["ssm_c_im"]),
            ws["ssm_d"].reshape(DEPTH, 1, -1))

    rows3 = {n: ws[n].reshape(DEPTH, 1, -1) for n in ("b_glu", "mla_q_norm", "mla_kv_norm", "b_gate", "ln_g", "ln_b")}

    def small(n, l):
        return _LayerRow(rows3[n], l)

    (gathered,) = _exchange_call("weights_gather_first", [_Exchange("ag", loc[0])])
    W = [None] * DEPTH
    W[0] = _unpack_weights(*gathered)
    saved = []
    xs, xb = x, x.astype(BF)
    for l in range(DEPTH):
        w = W[l]
        proj = _mm(xb, w["w_in"], name="proj_fwd", tm=S, tn=512)
        u_il = _interleave(proj[:, :SSM_WIDTH])
        y_raw = _deinterleave(_ssm_fwd(u_il, mats, l))
        o_ssm = _ssm_post_fwd(y_raw, proj, w["w_glu"], small("b_glu", l), w["p_ssm"])
        q, k, v, qt, kt, vt = _mla_pre_fwd(proj, small("mla_q_norm", l), small("mla_kv_norm", l), w["w_uq"], w["w_k"], w["w_v"],
                               tc, tsa, tsb)
        nxt = [_Exchange("ag", loc[l + 1])] if l + 1 < DEPTH else []
        (o_att, lse, lse_t), gathered = _flash_fwd(q, kt, v, nxt)
        if nxt:
            W[l + 1] = _unpack_weights(*gathered[0])
        kvm = _mm(mem, w["w_mem_kv"], name="memkv_fwd", out_dtype=BF)
        km, vm = kvm[:, :512], kvm[:, 512:]
        o_mem = _mem_fwd(proj, km, vm, w["p_mem"])
        xn, xnb, pre, merged = _merge_fwd(xs, proj, o_ssm, o_att, o_mem, small("b_gate", l), w["p_mla"], w["w_out"],
                                          small("ln_g", l), small("ln_b", l))
        saved.append(dict(xb=xb, proj=proj, u_il=u_il, y_raw=y_raw, o_ssm=o_ssm, q=q, k=k, v=v, qt=qt, kt=kt, vt=vt, o_att=o_att,
                          lse=lse, lse_t=lse_t,
                          km=km, vm=vm, o_mem=o_mem, pre=pre, merged=merged))
        xs, xb = xn, xnb

    dxs, lvec = _loss_head(xs, target)
    loss = lvec[0, 0]

    gs = {n: [None] * DEPTH for n in SMALL}
    disc_ct = [None] * DEPTH
    got = [None] * DEPTH
    pending = None
    for l in reversed(range(DEPTH)):
        sv, w = saved[l], W[l]
        proj = sv["proj"]
        (dx_res, dlg, do_ssm, do_mem, do_att, dz_mla, do_att_t, delta, delta_t, d_w_out, d_p_mla, d_b_gate, d_ln_g,
         d_ln_b) = _merge_bwd(
            dxs, sv["pre"], sv["merged"], proj, sv["o_ssm"], sv["o_att"], sv["o_mem"], small("b_gate", l), w["p_mla"],
            w["w_out"], small("ln_g", l))
        dq_mem, dz_mem, d_km, d_vm, d_p_mem = _mem_bwd(do_mem, proj, sv["km"], sv["vm"], w["p_mem"])
        d_w_mem = _mm(mem, jnp.concatenate([d_km, d_vm], axis=1), name="memkv_bwd", ta=True)
        dq, _ = _flash_bwd_dq(sv["q"], sv["k"], sv["kt"], sv["vt"], do_att, sv["lse"], delta)
        exs = [_Exchange("a2a", pending)] if pending is not None else []
        dk, dv, arrived = _flash_bwd_dkv(sv["q"], sv["qt"], sv["k"], sv["v"], do_att, do_att_t, sv["lse_t"], delta_t, exs)
        if exs:
            got[l + 1] = arrived[0]
        dcq, dckv, dslot, d_wuq, d_wk, d_wv, d_qn, d_kn = _mla_pre_bwd(
            dq, dk, dv, proj, small("mla_q_norm", l), small("mla_kv_norm", l), w["w_uq"], w["w_k"], w["w_v"],
            tc, tsa, tsb)
        dy_raw, dz_ssm, d_w_glu, d_b_glu, d_p_ssm = _ssm_post_bwd(do_ssm, sv["y_raw"], proj, w["w_glu"],
                                                                 small("b_glu", l), w["p_ssm"])
        rest = _pack_grads_rest(dict(w_glu=d_w_glu, w_uq=d_wuq, w_k=d_wk, w_v=d_wv, w_mem_kv=d_w_mem, p_ssm=d_p_ssm,
                                     p_mla=d_p_mla, p_mem=d_p_mem, w_out=d_w_out))
        (du_il, dbbre, dbbim, dare, daim, dcre, dcim, dd), early = _ssm_bwd(
            sv["u_il"], _interleave(dy_raw), mats, l, [_Exchange("a2a", rest)] if l == 0 else [])
        du = _deinterleave(du_il).astype(BF)
        disc_ct[l] = (dare.reshape(SSM_GROUPS, SSM_STATE), daim.reshape(SSM_GROUPS, SSM_STATE), _bd_in_t(dbbre),
                      _bd_in_t(dbbim))
        dproj = jnp.concatenate([du, dz_ssm, dcq, dckv, dslot, dz_mla, dq_mem, dz_mem, dlg], axis=1)
        d_w_in = _mm(sv["xb"], dproj, name="proj_dw", ta=True, tm=1024, tn=512, tk=S)
        if l > 0:
            dxs = _mm(dproj, w["w_in"], name="proj_dx", tb=True, add=dx_res, tm=1024, tn=1024, tk=1024)
        pending = [_pack_grads_in(d_w_in)] + (rest if l > 0 else [])
        gs["b_gate"][l] = d_b_gate.reshape(-1)
        gs["ssm_c_re"][l] = _bd_in_t(dcre).transpose(0, 2, 1)
        gs["ssm_c_im"][l] = _bd_in_t(dcim).transpose(0, 2, 1)
        gs["ssm_d"][l] = dd.reshape(-1)
        gs["b_glu"][l] = d_b_glu.reshape(-1)
        gs["mla_q_norm"][l] = d_qn.reshape(-1)
        gs["mla_kv_norm"][l] = d_kn.reshape(-1)
        gs["ln_g"][l] = d_ln_g.reshape(-1)
        gs["ln_b"][l] = d_ln_b.reshape(-1)

    disc_in = tuple(ws[n] for n in ("ssm_a_re", "ssm_a_im", "ssm_log_dt", "ssm_b_re", "ssm_b_im"))
    _, disc_vjp = jax.vjp(_ssm_discretise, *disc_in)
    d_disc = disc_vjp(tuple(jnp.stack([disc_ct[l][i] for l in range(DEPTH)]) for i in range(4)))
    gsm = {n: jnp.stack(v) for n, v in gs.items() if v[0] is not None}
    for n, g in zip(("ssm_a_re", "ssm_a_im", "ssm_log_dt", "ssm_b_re", "ssm_b_im"), d_disc):
        gsm[n] = g
    dxs, (last_in, (small_parts,)) = _mm(
        dproj, w["w_in"], name="proj_dx", tb=True, add=dx_res, tm=1024, tn=1024, tk=1024,
        exs=[_Exchange("a2a", pending), _Exchange("ag", [_pack_small(gsm)])])
    got[0] = last_in + early[0]
    return loss, dxs, got, small_parts


def kernel(x, mem, positions, w_in, b_gate, ssm_a_re, ssm_a_im, ssm_log_dt, ssm_b_re, ssm_b_im, ssm_c_re, ssm_c_im, ssm_d, w_glu, b_glu, mla_q_norm, w_uq, mla_kv_norm, w_ukv, w_mem_kv, p_ssm, p_mla, p_mem, w_out, ln_g, ln_b, loss_target, m_w_in, m_b_gate, m_ssm_a_re, m_ssm_a_im, m_ssm_log_dt, m_ssm_b_re, m_ssm_b_im, m_ssm_c_re, m_ssm_c_im, m_ssm_d, m_w_glu, m_b_glu, m_mla_q_norm, m_w_uq, m_mla_kv_norm, m_w_ukv, m_w_mem_kv, m_p_ssm, m_p_mla, m_p_mem, m_w_out, m_ln_g, m_ln_b, v_w_in, v_b_gate, v_ssm_a_re, v_ssm_a_im, v_ssm_log_dt, v_ssm_b_re, v_ssm_b_im, v_ssm_c_re, v_ssm_c_im, v_ssm_d, v_w_glu, v_b_glu, v_mla_q_norm, v_w_uq, v_mla_kv_norm, v_w_ukv, v_w_mem_kv, v_p_ssm, v_p_mla, v_p_mem, v_w_out, v_ln_g, v_ln_b):
    w = dict(w_in=w_in, b_gate=b_gate, ssm_a_re=ssm_a_re, ssm_a_im=ssm_a_im, ssm_log_dt=ssm_log_dt, ssm_b_re=ssm_b_re,
             ssm_b_im=ssm_b_im, ssm_c_re=ssm_c_re, ssm_c_im=ssm_c_im, ssm_d=ssm_d, w_glu=w_glu, b_glu=b_glu,
             mla_q_norm=mla_q_norm, w_uq=w_uq, mla_kv_norm=mla_kv_norm, w_ukv=w_ukv, w_mem_kv=w_mem_kv, p_ssm=p_ssm,
             p_mla=p_mla, p_mem=p_mem, w_out=w_out, ln_g=ln_g, ln_b=ln_b)
    m = dict(w_in=m_w_in, b_gate=m_b_gate, ssm_a_re=m_ssm_a_re, ssm_a_im=m_ssm_a_im, ssm_log_dt=m_ssm_log_dt,
             ssm_b_re=m_ssm_b_re, ssm_b_im=m_ssm_b_im, ssm_c_re=m_ssm_c_re, ssm_c_im=m_ssm_c_im, ssm_d=m_ssm_d,
             w_glu=m_w_glu, b_glu=m_b_glu, mla_q_norm=m_mla_q_norm, w_uq=m_w_uq, mla_kv_norm=m_mla_kv_norm,
             w_ukv=m_w_ukv, w_mem_kv=m_w_mem_kv, p_ssm=m_p_ssm, p_mla=m_p_mla, p_mem=m_p_mem, w_out=m_w_out,
             ln_g=m_ln_g, ln_b=m_ln_b)
    v = dict(w_in=v_w_in, b_gate=v_b_gate, ssm_a_re=v_ssm_a_re, ssm_a_im=v_ssm_a_im, ssm_log_dt=v_ssm_log_dt,
             ssm_b_re=v_ssm_b_re, ssm_b_im=v_ssm_b_im, ssm_c_re=v_ssm_c_re, ssm_c_im=v_ssm_c_im, ssm_d=v_ssm_d,
             w_glu=v_w_glu, b_glu=v_b_glu, mla_q_norm=v_mla_q_norm, w_uq=v_w_uq, mla_kv_norm=v_mla_kv_norm,
             w_ukv=v_w_ukv, w_mem_kv=v_w_mem_kv, p_ssm=v_p_ssm, p_mla=v_p_mla, p_mem=v_p_mem, w_out=v_w_out,
             ln_g=v_ln_g, ln_b=v_ln_b)

    wl = {n: w[n] for n in BIG}
    small = {n: w[n] for n in SMALL}
    loss_local, dx, got, small_parts = _train_step(x[0], mem[0], positions[0], loss_target[0], wl, small)
    loss = lax.psum(loss_local, ("x", "y", "c"))

    grads, delta, new_m, new_v = {}, {}, {}, {}
    wg = _group_buffers(wl, F32)
    mg = _group_buffers({n: m[n] for n in BIG}, F32)
    vg = _group_buffers({n: v[n] for n in BIG}, F32)
    res = []
    for i, (tile, tag) in enumerate(((256, "in"), (128, "col"), (256, "row"))):
        parts = jnp.stack([got[l][i] for l in range(DEPTH)])
        res.append(_adamw_sharded(parts, wg[i], mg[i], vg[i], tile, "adamw_" + tag))
    for dst, j in ((grads, 0), (delta, 1), (new_m, 2), (new_v, 3)):
        dst.update(_ungroup([r[j] for r in res]))

    sw, sm, sv = (_pack_small(small), _pack_small({n: m[n] for n in SMALL}), _pack_small({n: v[n] for n in SMALL}))
    rs = _adamw_sharded(small_parts[None], sw[None], sm[None], sv[None], SMALL_TILE, "adamw_replicated")
    for dst, buf in zip((grads, delta, new_m, new_v), rs):
        dst.update(_unpack_small(buf[0], small))

    return (loss, dx[None], *[grads[n] for n in WEIGHTS], *[delta[n] for n in WEIGHTS],
            *[new_m[n] for n in WEIGHTS], *[new_v[n] for n in WEIGHTS])
```

```python
import math

import jax
import jax.numpy as jnp
from jax import lax
from jax.experimental import pallas as pl
from jax.experimental.pallas import tpu as pltpu

F32 = jnp.float32
BF = jnp.bfloat16
ACT = jnp.bfloat16

D_MODEL = 1024
DEPTH = 4
N_DEV = 8
SSM_WIDTH = 512
SSM_GROUP = 16
SSM_GROUPS = 32
SSM_STATE = 64
MLA_HEADS = 8
MLA_NOPE = 64
MLA_ROPE = 32
MLA_V = 64
MLA_Q_RANK = 256
MLA_KV_RANK = 128
ROPE_THETA = 10000.0
X_HEADS = 4
X_HEAD_DIM = 128
D_IN = 6048
ALPHA = (2 * DEPTH) ** 0.25
NORM_EPS = 1e-5
ADAM_LR = 0.001
ADAM_B1 = 0.9
ADAM_B2 = 0.999
ADAM_EPS = 1e-08
ADAM_WD = 0.01
ADAM_STEP = 10

LANES = 128
SUBLANES = 8
VMEM_LIMIT = 56 * 1024 * 1024

PW = 6144
ROPE_SLOT_LO = 1408
MLA_SCALE = (MLA_NOPE + MLA_ROPE) ** -0.5
MEM_SCALE = X_HEAD_DIM ** -0.5
NEG = -1e30

T_ROWS = 512
T_ROWS_BWD = 256
T_ATT = 1024
T_MM = 512

MESH = pl.DeviceIdType.MESH


def _cparams(sem):
    return pltpu.CompilerParams(dimension_semantics=sem, vmem_limit_bytes=VMEM_LIMIT)


def _dot(a, b):
    return lax.dot_general(a, b, (((1,), (0,)), ((), ())), preferred_element_type=F32)


def _dot_nt(a, b):
    return lax.dot_general(a, b, (((1,), (1,)), ((), ())), preferred_element_type=F32)


def _dot_tn(a, b):
    return lax.dot_general(a, b, (((0,), (0,)), ((), ())), preferred_element_type=F32)


def _sigmoid(x):
    return 0.5 * jnp.tanh(0.5 * x) + 0.5


def _silu(x):
    return x * _sigmoid(x)


def _dsilu(x):
    s = _sigmoid(x)
    return s * (1.0 + x * (1.0 - s))


_GELU_C = math.sqrt(2.0 / math.pi)


def _gelu(x):
    return 0.5 * x * (1.0 + jnp.tanh(_GELU_C * (x + 0.044715 * x * x * x)))


def _dgelu(x):
    t = jnp.tanh(_GELU_C * (x + 0.044715 * x * x * x))
    return 0.5 * (1.0 + t) + 0.5 * x * (1.0 - t * t) * _GELU_C * (1.0 + 3 * 0.044715 * x * x)


def _rows(tr, w, col=0):
    return pl.BlockSpec((tr, w), lambda i: (i, col))


def _cols(h, tc):
    return pl.BlockSpec((h, tc), lambda i: (0, i))


def _full(shape):
    n = len(shape)
    return pl.BlockSpec(shape, lambda i: (0,) * n)


class _RowBlock:
    def __init__(self, arr, rows, blk):
        self.arr, self.rows, self.blk = arr, rows, blk

    def spec(self):
        blk = self.blk
        return pl.BlockSpec((self.rows, self.arr.shape[1]), lambda i: (blk, 0))


class _LayerRow:
    def __init__(self, arr, l):
        self.arr, self.l = arr, l

    def spec(self):
        l = self.l
        return pl.BlockSpec((1, 1, self.arr.shape[2]), lambda i: (l, 0, 0))


def _mm(a, b, *, name, ta=False, tb=False, out_dtype=F32, add=None, tm=T_MM, tn=T_MM, tk=1024, exs=None):
    M, K = (a.shape[1], a.shape[0]) if ta else a.shape
    N = b.shape[0] if tb else b.shape[1]
    tm, tn, tk = min(tm, M), min(tn, N), min(tk, K)
    assert M % tm == 0 and N % tn == 0 and K % tk == 0, (M, N, K)
    nk = K // tk
    dn = (((0 if ta else 1,), (1 if tb else 0,)), ((), ()))

    def body(*refs):
        if add is not None:
            a_ref, b_ref, c_ref, o_ref = refs[:4]
        else:
            a_ref, b_ref, o_ref = refs[:3]
        part = lax.dot_general(a_ref[...].astype(BF), b_ref[...].astype(BF), dn, preferred_element_type=F32)
        if nk == 1:
            if add is not None:
                part = part + c_ref[...]
            o_ref[...] = part.astype(out_dtype)
            return
        acc = refs[-1]
        k = pl.program_id(2)

        @pl.when(k == 0)
        def _():
            acc[...] = part

        @pl.when(k != 0)
        def _():
            acc[...] += part

        @pl.when(k == nk - 1)
        def _():
            r = acc[...]
            if add is not None:
                r = r + c_ref[...]
            o_ref[...] = r.astype(out_dtype)

    a_spec = pl.BlockSpec((tk, tm), lambda i, j, k: (k, i)) if ta else pl.BlockSpec((tm, tk), lambda i, j, k: (i, k))
    b_spec = pl.BlockSpec((tn, tk), lambda i, j, k: (j, k)) if tb else pl.BlockSpec((tk, tn), lambda i, j, k: (k, j))
    o_spec = pl.BlockSpec((tm, tn), lambda i, j, k: (i, j))
    in_specs = [a_spec, b_spec] + ([o_spec] if add is not None else [])
    args = (a, b) + ((add,) if add is not None else ())
    (out,), got = _carry_call(
        body, name, (M // tm, N // tn, nk), in_specs, [o_spec], [jax.ShapeDtypeStruct((M, N), out_dtype)],
        [pltpu.VMEM((tm, tn), F32)] if nk > 1 else [], ("parallel", "parallel", "arbitrary"), args, exs)
    return out if exs is None else (out, got)


def _cpow(ar, ai, n):
    rr, ri = None, None
    br, bi = ar, ai
    while n:
        if n & 1:
            if rr is None:
                rr, ri = br, bi
            else:
                rr, ri = rr * br - ri * bi, rr * bi + ri * br
        n >>= 1
        if n:
            br, bi = br * br - bi * bi, 2.0 * br * bi
    return rr, ri


def _seg_shift(v, k, reverse):
    sub = lax.broadcasted_iota(jnp.int32, v.shape, 0)
    if not reverse:
        return jnp.where(sub >= k, pltpu.roll(v, k, 0), 0.0)
    return jnp.where(sub < SUBLANES - k, pltpu.roll(v, SUBLANES - k, 0), 0.0)


def _steps(n, step, init, unroll):
    u = unroll if n % unroll == 0 else 1

    def trip(i, c):
        for s in range(u):
            c = step(i * u + s, c)
        return c

    return lax.fori_loop(0, n // u, trip, init)


def _ssm_scan(hre, him, ar, ai, seglen, reverse, tail=None, tail_init=()):
    w = hre.shape[1]
    zero = jnp.zeros((SUBLANES, w), F32)

    def rows(j):
        jj = (seglen - 1 - j) if reverse else j
        return pl.ds(pl.multiple_of(jj * SUBLANES, SUBLANES), SUBLANES)

    def local(j, c):
        hr, hi = c
        r = rows(j)
        nhr = ar * hr - ai * hi + hre[r, :]
        nhi = ar * hi + ai * hr + him[r, :]
        hre[r, :] = nhr
        him[r, :] = nhi
        return nhr, nhi

    er, ei = _steps(seglen, local, (zero, zero), 4 if reverse else 1)
    pr, pi_ = _cpow(ar, ai, seglen)
    for k in (1, 2, 4):
        sr, si = _seg_shift(er, k, reverse), _seg_shift(ei, k, reverse)
        er, ei = er + pr * sr - pi_ * si, ei + pr * si + pi_ * sr
        pr, pi_ = pr * pr - pi_ * pi_, 2.0 * pr * pi_
    cr, ci = _seg_shift(er, 1, reverse), _seg_shift(ei, 1, reverse)

    def carry_in(j, c):
        tr, ti = c[0] * ar - c[1] * ai, c[0] * ai + c[1] * ar
        r = rows(j)
        fr = hre[r, :] + tr
        fi = him[r, :] + ti
        hre[r, :] = fr
        him[r, :] = fi
        if tail is None:
            return tr, ti
        return (tr, ti) + tuple(tail(j, fr, fi, c[2:]))

    out = _steps(seglen, carry_in, (cr, ci) + tuple(tail_init), 4)
    return out[2:]


SSM_CB = 128
SSM_SB = 256


def _ssm_specs(S, l):
    u_spec = pl.BlockSpec((S, SSM_CB), lambda g, h: (0, g))
    bb_spec = pl.BlockSpec((1, 1, SSM_CB, SSM_SB), lambda g, h: (l, g, 0, h))
    a_spec = pl.BlockSpec((1, 1, 1, SSM_SB), lambda g, h: (l, g, 0, h))
    c_spec = pl.BlockSpec((1, 1, SSM_SB, SSM_CB), lambda g, h: (l, g, h, 0))
    d_spec = pl.BlockSpec((1, 1, SSM_CB), lambda g, h: (l, 0, g))
    return u_spec, bb_spec, a_spec, c_spec, d_spec


def _ssm_fwd(u, mats, l):
    S = u.shape[0]
    seglen = S // SUBLANES
    ch = min(512, S)
    nch = S // ch

    def body(u_ref, bbre_ref, bbim_ref, are_ref, aim_ref, cre_ref, cim_ref, d_ref, y_ref, hre, him):
        hf = pl.program_id(1)
        wre = bbre_ref[0, 0].astype(BF)
        wim = bbim_ref[0, 0].astype(BF)

        def mk(c, _):
            r = pl.ds(pl.multiple_of(c * ch, ch), ch)
            ub = u_ref[r, :].astype(BF)
            hre[r, :] = _dot(ub, wre)
            him[r, :] = _dot(ub, wim)
            return 0

        lax.fori_loop(0, nch, mk, 0)
        ar = jnp.broadcast_to(are_ref[0, 0], (SUBLANES, SSM_SB))
        ai = jnp.broadcast_to(aim_ref[0, 0], (SUBLANES, SSM_SB))
        _ssm_scan(hre, him, ar, ai, seglen, False)
        cr = cre_ref[0, 0].astype(BF)
        ci = cim_ref[0, 0].astype(BF)

        def out(c, _):
            r = pl.ds(pl.multiple_of(c * ch, ch), ch)
            y = _dot(hre[r, :].astype(BF), cr) - _dot(him[r, :].astype(BF), ci)

            @pl.when(hf == 0)
            def _():
                y_ref[r, :] = y + d_ref[0] * u_ref[r, :].astype(F32)

            @pl.when(hf != 0)
            def _():
                y_ref[r, :] = y_ref[r, :] + y

            return 0

        lax.fori_loop(0, nch, out, 0)

    u_spec, bb_spec, a_spec, c_spec, d_spec = _ssm_specs(S, l)
    return pl.pallas_call(
        body, name="ssm_fwd", grid=(SSM_WIDTH // SSM_CB, 2),
        in_specs=[u_spec, bb_spec, bb_spec, a_spec, a_spec, c_spec, c_spec, d_spec], out_specs=u_spec,
        out_shape=jax.ShapeDtypeStruct((S, SSM_WIDTH), F32),
        scratch_shapes=[pltpu.VMEM((S, SSM_SB), F32), pltpu.VMEM((S, SSM_SB), F32)],
        compiler_params=_cparams(("parallel", "arbitrary")),
    )(u, *mats)


def _ssm_bwd(u, dy, mats, l, exs=()):
    S = u.shape[0]
    seglen = S // SUBLANES
    ch = min(512, S)
    nch = S // ch
    nblk = SSM_WIDTH // SSM_CB

    def body(u_ref, dy_ref, bbre_ref, bbim_ref, are_ref, aim_ref, cre_ref, cim_ref, d_ref,
             du_ref, dbbre_ref, dbbim_ref, dare_ref, daim_ref, dcre_ref, dcim_ref, dd_ref,
             hre, him, lre, lim):
        hf = pl.program_id(1)
        wre = bbre_ref[0, 0].astype(BF)
        wim = bbim_ref[0, 0].astype(BF)
        wre_t, wim_t = wre.T, wim.T
        cr_t = cre_ref[0, 0].astype(BF).T
        ci_t = cim_ref[0, 0].astype(BF).T

        def mk(c, _):
            r = pl.ds(pl.multiple_of(c * ch, ch), ch)
            ub = u_ref[r, :].astype(BF)
            hre[r, :] = _dot(ub, wre)
            him[r, :] = _dot(ub, wim)
            return 0

        lax.fori_loop(0, nch, mk, 0)
        ar = jnp.broadcast_to(are_ref[0, 0], (SUBLANES, SSM_SB))
        ai = jnp.broadcast_to(aim_ref[0, 0], (SUBLANES, SSM_SB))
        _ssm_scan(hre, him, ar, ai, seglen, False)

        dcre_ref[...] = jnp.zeros_like(dcre_ref)
        dcim_ref[...] = jnp.zeros_like(dcim_ref)

        @pl.when(hf == 0)
        def _():
            dd_ref[...] = jnp.zeros_like(dd_ref)

        def cot(c, _):
            r = pl.ds(pl.multiple_of(c * ch, ch), ch)
            dyv = dy_ref[r, :]
            dyb = dyv.astype(BF)
            lre[r, :] = _dot(dyb, cr_t)
            lim[r, :] = -_dot(dyb, ci_t)
            dcre_ref[0] = dcre_ref[0] + _dot_tn(dyb, hre[r, :].astype(BF))
            dcim_ref[0] = dcim_ref[0] - _dot_tn(dyb, him[r, :].astype(BF))

            @pl.when(hf == 0)
            def _():
                dd_ref[...] = dd_ref[...] + jnp.sum(dyv * u_ref[r, :].astype(F32), axis=0, keepdims=True)

            return 0

        lax.fori_loop(0, nch, cot, 0)

        last = pl.ds((seglen - 1) * SUBLANES, SUBLANES)
        pr0 = _seg_shift(hre[last, :], 1, False)
        pi0 = _seg_shift(him[last, :], 1, False)

        def da(j, lr, li, c):
            acr, aci = c
            jp = jnp.maximum(seglen - 2 - j, 0)
            rp = pl.ds(pl.multiple_of(jp * SUBLANES, SUBLANES), SUBLANES)
            inner = j < seglen - 1
            pr = jnp.where(inner, hre[rp, :], pr0)
            pi_ = jnp.where(inner, him[rp, :], pi0)
            return acr + lr * pr + li * pi_, aci + li * pr - lr * pi_

        zero = jnp.zeros((SUBLANES, SSM_SB), F32)
        acr, aci = _ssm_scan(lre, lim, ar, -ai, seglen, True, tail=da, tail_init=(zero, zero))
        dare_ref[0] = jnp.sum(acr, axis=0, keepdims=True)
        daim_ref[0] = jnp.sum(aci, axis=0, keepdims=True)

        dbbre_ref[...] = jnp.zeros_like(dbbre_ref)
        dbbim_ref[...] = jnp.zeros_like(dbbim_ref)

        def fin(c, _):
            r = pl.ds(pl.multiple_of(c * ch, ch), ch)
            lrb = lre[r, :].astype(BF)
            lib = lim[r, :].astype(BF)
            ub = u_ref[r, :].astype(BF)
            du = _dot(lrb, wre_t) + _dot(lib, wim_t)
            dbbre_ref[0] = dbbre_ref[0] + _dot_tn(ub, lrb)
            dbbim_ref[0] = dbbim_ref[0] + _dot_tn(ub, lib)

            @pl.when(hf == 0)
            def _():
                du_ref[r, :] = du + d_ref[0] * dy_ref[r, :]

            @pl.when(hf != 0)
            def _():
                du_ref[r, :] = du_ref[r, :] + du

            return 0

        lax.fori_loop(0, nch, fin, 0)

    u_spec, bb_spec, a_spec, c_spec, d_spec = _ssm_specs(S, l)
    dbb_spec = pl.BlockSpec((1, SSM_CB, SSM_SB), lambda g, h: (g, 0, h))
    da_spec = pl.BlockSpec((1, 1, SSM_SB), lambda g, h: (g, 0, h))
    dd_spec = pl.BlockSpec((1, SSM_CB), lambda g, h: (0, g))
    out_shape = (
        jax.ShapeDtypeStruct((S, SSM_WIDTH), F32),
        jax.ShapeDtypeStruct((nblk, SSM_CB, 2 * SSM_SB), F32), jax.ShapeDtypeStruct((nblk, SSM_CB, 2 * SSM_SB), F32),
        jax.ShapeDtypeStruct((nblk, 1, 2 * SSM_SB), F32), jax.ShapeDtypeStruct((nblk, 1, 2 * SSM_SB), F32),
        jax.ShapeDtypeStruct((nblk, SSM_CB, 2 * SSM_SB), F32), jax.ShapeDtypeStruct((nblk, SSM_CB, 2 * SSM_SB), F32),
        jax.ShapeDtypeStruct((1, SSM_WIDTH), F32),
    )
    return _carry_call(
        body, "ssm_bwd", (nblk, 2), [u_spec, u_spec, bb_spec, bb_spec, a_spec, a_spec, c_spec, c_spec, d_spec],
        (u_spec, dbb_spec, dbb_spec, da_spec, da_spec, dbb_spec, dbb_spec, dd_spec), out_shape,
        [pltpu.VMEM((S, SSM_SB), F32) for _ in range(4)], ("parallel", "arbitrary"), (u, dy) + tuple(mats), exs)


def _ssm_post_fwd(y_raw, proj, w_glu, b_glu, p_ssm):
    S = y_raw.shape[0]
    tr = min(T_ROWS, S)

    def body(y_ref, z_ref, wg_ref, bg_ref, p_ref, o_ref):
        g = _gelu(y_ref[...])
        t = _dot(g.astype(BF), wg_ref[...]) + bg_ref[0]
        glu = t[:, :SSM_WIDTH] * _sigmoid(t[:, SSM_WIDTH:])
        ys = glu * _silu(z_ref[...].astype(F32))
        o_ref[...] = _dot(ys.astype(BF), p_ref[...]).astype(o_ref.dtype)

    return pl.pallas_call(
        body, name="ssm_post_fwd", grid=(S // tr,),
        in_specs=[_rows(tr, 512), _rows(tr, 512, 1), w_glu.spec(), b_glu.spec(), p_ssm.spec()],
        out_specs=_rows(tr, 1024), out_shape=jax.ShapeDtypeStruct((S, D_MODEL), ACT),
        compiler_params=_cparams(("parallel",)),
    )(y_raw, proj, w_glu.arr, b_glu.arr, p_ssm.arr)


def _ssm_post_bwd(do, y_raw, proj, w_glu, b_glu, p_ssm):
    S = y_raw.shape[0]
    tr = min(T_ROWS_BWD, S)

    def body(do_ref, y_ref, z_ref, wg_ref, bg_ref, p_ref, dy_ref, dz_ref, dwg_ref, dbg_ref, dp_ref):
        @pl.when(pl.program_id(0) == 0)
        def _():
            dwg_ref[...] = jnp.zeros_like(dwg_ref)
            dbg_ref[...] = jnp.zeros_like(dbg_ref)
            dp_ref[...] = jnp.zeros_like(dp_ref)

        y = y_ref[...]
        z = z_ref[...].astype(F32)
        g = _gelu(y)
        gb = g.astype(BF)
        t = _dot(gb, wg_ref[...]) + bg_ref[0]
        a = t[:, :SSM_WIDTH]
        sb = _sigmoid(t[:, SSM_WIDTH:])
        glu = a * sb
        ys = glu * _silu(z)
        dob = do_ref[...].astype(BF)
        dys = _dot_nt(dob, p_ref[...])
        dp_ref[...] += _dot_tn(ys.astype(BF), dob)
        dglu = dys * _silu(z)
        dz_ref[...] = (dys * glu * _dsilu(z)).astype(dz_ref.dtype)
        dt = jnp.concatenate([dglu * sb, dglu * a * sb * (1.0 - sb)], axis=1)
        dbg_ref[...] += jnp.sum(dt, axis=0, keepdims=True)
        dtb = dt.astype(BF)
        dg = _dot_nt(dtb, wg_ref[...])
        dwg_ref[...] += _dot_tn(gb, dtb)
        dy_ref[...] = dg * _dgelu(y)

    return pl.pallas_call(
        body, name="ssm_post_bwd", grid=(S // tr,),
        in_specs=[_rows(tr, 1024), _rows(tr, 512), _rows(tr, 512, 1), w_glu.spec(), b_glu.spec(), p_ssm.spec()],
        out_specs=(_rows(tr, 512), _rows(tr, 512), _full((512, 1024)), _full((1, 1024)), _full((512, 1024))),
        out_shape=(jax.ShapeDtypeStruct((S, 512), F32), jax.ShapeDtypeStruct((S, 512), BF),
                   jax.ShapeDtypeStruct((512, 1024), F32), jax.ShapeDtypeStruct((1, 1024), F32),
                   jax.ShapeDtypeStruct((512, 1024), F32)),
        compiler_params=_cparams(("arbitrary",)),
    )(do, y_raw, proj, w_glu.arr, b_glu.arr, p_ssm.arr)


def _rope(t, c, sa, sb):
    return t * c + pltpu.roll(t, LANES - 16, 1) * sa + pltpu.roll(t, 16, 1) * sb


def _rope_t(dy, c, sa, sb):
    return dy * c + pltpu.roll(dy * sa, 16, 1) + pltpu.roll(dy * sb, LANES - 16, 1)


def _rms(x, g):
    r = lax.rsqrt(jnp.mean(x * x, axis=-1, keepdims=True) + NORM_EPS)
    return x * r * g, r


def _mla_pre_fwd(proj, q_norm, kv_norm, wuq, wk, wv, tc, tsa, tsb):
    S = proj.shape[0]
    tr = min(T_ROWS, S)

    def body(cq_ref, ckv_ref, slot_ref, qn_ref, kn_ref, wuq_ref, wk_ref, wv_ref, c_ref, sa_ref, sb_ref,
             q_out, k_out, v_out, qt_out, kt_out, vt_out):
        c, sa, sb = c_ref[...], sa_ref[...], sb_ref[...]
        qn, _ = _rms(cq_ref[...].astype(F32), qn_ref[0])
        q = _dot(qn.astype(BF), wuq_ref[...]) * MLA_SCALE
        kn, _ = _rms(ckv_ref[...].astype(F32), kn_ref[0])
        knb = kn.astype(BF)
        kp = _dot(knb, wk_ref[...])
        v = _dot(knb, wv_ref[...]).astype(BF)
        v_out[...] = v
        vt_out[...] = v.T
        kr = _rope(slot_ref[...].astype(F32), c, sa, sb)
        for h in range(MLA_HEADS):
            cs = slice(h * LANES, (h + 1) * LANES)
            qh = _rope(q[:, cs], c, sa, sb).astype(BF)
            kh = (kp[:, cs] + kr).astype(BF)
            q_out[:, cs] = qh
            k_out[:, cs] = kh
            qt_out[cs, :] = qh.T
            kt_out[cs, :] = kh.T

    return pl.pallas_call(
        body, name="mla_pre_fwd", grid=(S // tr,),
        in_specs=[_rows(tr, 256, 4), _rows(tr, 128, 10), _rows(tr, 128, 11), q_norm.spec(), kv_norm.spec(),
                  wuq.spec(), _full((128, 1024)), _full((128, 512)),
                  _rows(tr, 128), _rows(tr, 128), _rows(tr, 128)],
        out_specs=(_rows(tr, 1024), _rows(tr, 1024), _rows(tr, 512), _cols(1024, tr), _cols(1024, tr), _cols(512, tr)),
        out_shape=(jax.ShapeDtypeStruct((S, 1024), BF), jax.ShapeDtypeStruct((S, 1024), BF),
                   jax.ShapeDtypeStruct((S, 512), BF), jax.ShapeDtypeStruct((1024, S), BF),
                   jax.ShapeDtypeStruct((1024, S), BF), jax.ShapeDtypeStruct((512, S), BF)),
        compiler_params=_cparams(("parallel",)),
    )(proj, proj, proj, q_norm.arr, kv_norm.arr, wuq.arr, wk, wv, tc, tsa, tsb)


def _mla_pre_bwd(dq, dk, dv, proj, q_norm, kv_norm, wuq, wk, wv, tc, tsa, tsb):
    S = proj.shape[0]
    tr = min(T_ROWS_BWD, S)

    def body(dq_ref, dk_ref, dv_ref, cq_ref, ckv_ref, qn_ref, kn_ref, wuq_ref, wk_ref, wv_ref, c_ref, sa_ref, sb_ref,
             dcq_ref, dckv_ref, dslot_ref, dwuq_ref, dwk_ref, dwv_ref, dqn_ref, dkn_ref, dqp):
        @pl.when(pl.program_id(0) == 0)
        def _():
            dwuq_ref[...] = jnp.zeros_like(dwuq_ref)
            dwk_ref[...] = jnp.zeros_like(dwk_ref)
            dwv_ref[...] = jnp.zeros_like(dwv_ref)
            dqn_ref[...] = jnp.zeros_like(dqn_ref)
            dkn_ref[...] = jnp.zeros_like(dkn_ref)

        c, sa, sb = c_ref[...], sa_ref[...], sb_ref[...]
        dkr = jnp.zeros((tr, LANES), F32)
        for h in range(MLA_HEADS):
            cs = slice(h * LANES, (h + 1) * LANES)
            dqp[:, cs] = (_rope_t(dq_ref[:, cs], c, sa, sb) * MLA_SCALE).astype(BF)
            dkr = dkr + dk_ref[:, cs]
        lane = lax.broadcasted_iota(jnp.int32, (tr, LANES), 1)
        in_rope = (lane >= MLA_NOPE) & (lane < MLA_NOPE + MLA_ROPE)
        dslot_ref[...] = jnp.where(in_rope, _rope_t(dkr, c, sa, sb), 0.0).astype(dslot_ref.dtype)

        cq = cq_ref[...].astype(F32)
        gq = qn_ref[0]
        qn, rq = _rms(cq, gq)
        dqpb = dqp[...]
        dwuq_ref[...] += _dot_tn(qn.astype(BF), dqpb)
        dqn = _dot_nt(dqpb, wuq_ref[...])
        dqn_ref[...] += jnp.sum(dqn * cq * rq, axis=0, keepdims=True)
        dyg = dqn * gq
        dcq_ref[...] = (rq * dyg - cq * (rq * rq * rq) * jnp.mean(dyg * cq, axis=-1, keepdims=True)).astype(dcq_ref.dtype)

        ckv = ckv_ref[...].astype(F32)
        gk = kn_ref[0]
        kn, rk = _rms(ckv, gk)
        knb = kn.astype(BF)
        dkb = dk_ref[...].astype(BF)
        dvb = dv_ref[...].astype(BF)
        dwk_ref[...] += _dot_tn(knb, dkb)
        dwv_ref[...] += _dot_tn(knb, dvb)
        dkn = _dot_nt(dkb, wk_ref[...]) + _dot_nt(dvb, wv_ref[...])
        dkn_ref[...] += jnp.sum(dkn * ckv * rk, axis=0, keepdims=True)
        dyk = dkn * gk
        dckv_ref[...] = (rk * dyk - ckv * (rk * rk * rk) * jnp.mean(dyk * ckv, axis=-1, keepdims=True)).astype(dckv_ref.dtype)

    return pl.pallas_call(
        body, name="mla_pre_bwd", grid=(S // tr,),
        in_specs=[_rows(tr, 1024), _rows(tr, 1024), _rows(tr, 512), _rows(tr, 256, 4), _rows(tr, 128, 10),
                  q_norm.spec(), kv_norm.spec(), wuq.spec(), _full((128, 1024)), _full((128, 512)),
                  _rows(tr, 128), _rows(tr, 128), _rows(tr, 128)],
        out_specs=(_rows(tr, 256), _rows(tr, 128), _rows(tr, 128), _full((256, 1024)), _full((128, 1024)),
                   _full((128, 512)), _full((1, 256)), _full((1, 128))),
        out_shape=(jax.ShapeDtypeStruct((S, 256), BF), jax.ShapeDtypeStruct((S, 128), BF),
                   jax.ShapeDtypeStruct((S, 128), BF), jax.ShapeDtypeStruct((256, 1024), F32),
                   jax.ShapeDtypeStruct((128, 1024), F32), jax.ShapeDtypeStruct((128, 512), F32),
                   jax.ShapeDtypeStruct((1, 256), F32), jax.ShapeDtypeStruct((1, 128), F32)),
        scratch_shapes=[pltpu.VMEM((tr, 1024), BF)],
        compiler_params=_cparams(("arbitrary",)),
    )(dq, dk, dv, proj, proj, q_norm.arr, kv_norm.arr, wuq.arr, wk, wv, tc, tsa, tsb)


ANY = pl.BlockSpec(memory_space=pl.ANY)
N_REL = N_DEV - 1


def _coords():
    return lax.axis_index("x"), lax.axis_index("y"), lax.axis_index("c")


def _sem_shapes(nbuf):
    return [pltpu.SemaphoreType.DMA((N_REL * nbuf,)), pltpu.SemaphoreType.DMA((N_REL * nbuf,)),
            pltpu.SemaphoreType.DMA((nbuf,))]


def _ag_plan(srcs, dsts, sems):
    send_sems, recv_sems, _ = sems
    plan = []
    for b, (src, dst) in enumerate(zip(srcs, dsts)):
        def slot(px, py, pc, dst=dst):
            return dst.at[4 * px + 2 * py + pc]

        def copy(k, blk, to, s=None, b=b, slot=slot):
            return pltpu.make_async_remote_copy(
                src_ref=slot(*blk) if s is None else s, dst_ref=slot(*blk), send_sem=send_sems.at[N_REL * b + k],
                recv_sem=recv_sems.at[N_REL * b + k], device_id=to, device_id_type=MESH)

        plan.append((b, src, slot, copy))
    return plan


def _ag_start(srcs, dsts, sems):
    x, y, c = _coords()
    chips = [(1 - x, y), (x, 1 - y), (1 - x, 1 - y)]
    for b, src, slot, copy in _ag_plan(srcs, dsts, sems):
        pltpu.make_async_copy(src, slot(x, y, c), sems[2].at[b]).start()
        copy(0, (x, y, c), (x, y, 1 - c), src).start()
        for j, chip in enumerate(chips):
            copy(1 + j, (x, y, c), (*chip, c), src).start()


def _ag_finish(srcs, dsts, sems):
    x, y, c = _coords()
    me, sibling = (x, y, c), (x, y, 1 - c)
    chips = [(1 - x, y), (x, 1 - y), (1 - x, 1 - y)]
    plan = _ag_plan(srcs, dsts, sems)
    for b, src, slot, copy in plan:
        for j, chip in enumerate(chips):
            copy(1 + j, (*chip, c), me).wait_recv()
            copy(4 + j, (*chip, c), sibling).start()
    for b, src, slot, copy in plan:
        copy(0, sibling, me).wait_recv()
        for j, chip in enumerate(chips):
            copy(4 + j, (*chip, 1 - c), me).wait_recv()
        copy(0, me, sibling, src).wait_send()
        for j, chip in enumerate(chips):
            copy(1 + j, me, (*chip, c), src).wait_send()
            copy(4 + j, (*chip, c), sibling).wait_send()
        pltpu.make_async_copy(src, slot(*me), sems[2].at[b]).wait()


def _a2a_copies(srcs, dsts, sems):
    send_sems, recv_sems, local_sems = sems
    x, y, c = _coords()
    me = 4 * x + 2 * y + c
    local, remote = [], []
    for b, (src, dst) in enumerate(zip(srcs, dsts)):
        for rel in range(1, N_DEV):
            px = 1 - x if rel & 4 else x
            py = 1 - y if rel & 2 else y
            pc = 1 - c if rel & 1 else c
            remote.append(pltpu.make_async_remote_copy(
                src_ref=src.at[4 * px + 2 * py + pc], dst_ref=dst.at[me], send_sem=send_sems.at[N_REL * b + rel - 1],
                recv_sem=recv_sems.at[N_REL * b + rel - 1], device_id=(px, py, pc), device_id_type=MESH))
        local.append(pltpu.make_async_copy(src.at[me], dst.at[me], local_sems.at[b]))
    return local, remote


def _a2a_start(srcs, dsts, sems):
    local, remote = _a2a_copies(srcs, dsts, sems)
    for d in local + remote:
        d.start()


def _a2a_finish(srcs, dsts, sems):
    local, remote = _a2a_copies(srcs, dsts, sems)
    for d in remote + local:
        d.wait()


class _Exchange:
    def __init__(self, kind, srcs):
        self.kind, self.srcs = kind, list(srcs)
        self.n = len(self.srcs)

    def out_shapes(self):
        if self.kind == "ag":
            return [jax.ShapeDtypeStruct((N_DEV,) + s.shape, s.dtype) for s in self.srcs]
        return [jax.ShapeDtypeStruct(s.shape, s.dtype) for s in self.srcs]

    def start(self, src_refs, dst_refs, sems):
        (_ag_start if self.kind == "ag" else _a2a_start)(src_refs, dst_refs, sems)

    def finish(self, src_refs, dst_refs, sems):
        (_ag_finish if self.kind == "ag" else _a2a_finish)(src_refs, dst_refs, sems)


def _carry_call(body, name, grid, in_specs, out_specs, out_shape, scratch, semantics, args, exs):
    in_specs, out_specs, out_shape, scratch = list(in_specs), list(out_specs), list(out_shape), list(scratch)
    if not exs:
        return pl.pallas_call(body, name=name, grid=grid, in_specs=in_specs, out_specs=out_specs, out_shape=out_shape,
                              scratch_shapes=scratch, compiler_params=_cparams(semantics))(*args), []
    n_in, n_out, n_scr = len(in_specs), len(out_specs), len(scratch)
    n_ex = sum(e.n for e in exs)

    def wrapped(*refs):
        ins, refs = refs[:n_in], refs[n_in:]
        srcs, refs = refs[:n_ex], refs[n_ex:]
        outs, refs = refs[:n_out], refs[n_out:]
        dsts, refs = refs[:n_ex], refs[n_ex:]
        scr, sems = refs[:n_scr], refs[n_scr:]
        views, off = [], 0
        for i, e in enumerate(exs):
            views.append((srcs[off:off + e.n], dsts[off:off + e.n], sems[3 * i:3 * i + 3]))
            off += e.n
        first = last = None
        for axis, size in enumerate(grid):
            at0, at1 = pl.program_id(axis) == 0, pl.program_id(axis) == size - 1
            first = at0 if first is None else first & at0
            last = at1 if last is None else last & at1

        @pl.when(first)
        def _():
            for e, view in zip(exs, views):
                e.start(*view)

        body(*ins, *outs, *scr)

        @pl.when(last)
        def _():
            for e, view in zip(exs, views):
                e.finish(*view)

    res = pl.pallas_call(
        wrapped, name=name + "_x", grid=grid, in_specs=in_specs + [ANY] * n_ex, out_specs=out_specs + [ANY] * n_ex,
        out_shape=out_shape + [s for e in exs for s in e.out_shapes()],
        scratch_shapes=scratch + [s for e in exs for s in _sem_shapes(e.n)],
        compiler_params=_cparams(("arbitrary",) * len(grid)))(*args, *[s for e in exs for s in e.srcs])
    got, off = [], n_out
    for e in exs:
        got.append(list(res[off:off + e.n]))
        off += e.n
    return res[:n_out], got


def _exchange_call(name, exs):
    tot = sum(e.n for e in exs)

    def body(*refs):
        srcs, dsts, sems = refs[:tot], refs[tot:2 * tot], refs[2 * tot:]
        views, off = [], 0
        for i, e in enumerate(exs):
            views.append((srcs[off:off + e.n], dsts[off:off + e.n], sems[3 * i:3 * i + 3]))
            off += e.n
        for e, view in zip(exs, views):
            e.start(*view)
        for e, view in zip(exs, views):
            e.finish(*view)

    outs = pl.pallas_call(
        body, name=name, in_specs=[ANY] * tot, out_specs=[ANY] * tot,
        out_shape=[s for e in exs for s in e.out_shapes()],
        scratch_shapes=[s for e in exs for s in _sem_shapes(e.n)],
    )(*[s for e in exs for s in e.srcs])
    res, off = [], 0
    for e in exs:
        res.append(list(outs[off:off + e.n]))
        off += e.n
    return res


def _flash_call(body, name, exs, in_specs, out_specs, out_shape, scratch, n, args):
    return _carry_call(body, name, (MLA_HEADS // 2, n * (n + 1) // 2), in_specs, out_specs, out_shape, scratch,
                       ("parallel", "arbitrary"), args, exs)


def _tri_rows(s, n):
    at = [(s >= r * (r + 1) // 2).astype(jnp.int32) for r in range(1, n)]
    return sum(at), s - sum(a * r for a, r in zip(at, range(1, n)))


def _tri_cols(s, n):
    starts = [c * n - c * (c - 1) // 2 for c in range(n)]
    col = sum((s >= starts[c]).astype(jnp.int32) for c in range(1, n))
    start = sum(jnp.where(col == c, starts[c], 0) for c in range(n))
    return s - start + col, col


def _pair_rows(a):
    at = a.T
    return jnp.concatenate([at[0:1, :], at[MLA_V:MLA_V + 1, :], jnp.zeros((SUBLANES - 2, a.shape[0]), a.dtype)], axis=0)


def _lower_tri(t):
    return lax.broadcasted_iota(jnp.int32, (t, t), 0) >= lax.broadcasted_iota(jnp.int32, (t, t), 1)


def _upper_tri(t):
    return lax.broadcasted_iota(jnp.int32, (t, t), 1) >= lax.broadcasted_iota(jnp.int32, (t, t), 0)


def _flash_fwd(q, kt, v, exs=()):
    S = q.shape[0]
    t = min(T_ATT, S)
    n = S // t

    def body(q_ref, kt_ref, v_ref, o_ref, lse_ref, lse_t_ref, m_s, l_s, acc):
        qi, ki = _tri_rows(pl.program_id(1), n)
        lo = lax.broadcasted_iota(jnp.int32, (t, LANES), 1) < MLA_V

        @pl.when(ki == 0)
        def _():
            m_s[...] = jnp.full_like(m_s, NEG)
            l_s[...] = jnp.zeros_like(l_s)
            acc[...] = jnp.zeros_like(acc)

        keep = _lower_tri(t) | (ki < qi)
        vv = v_ref[...]
        heads = range(2)
        ss = [jnp.where(keep, _dot(q_ref[:, h * LANES:(h + 1) * LANES], kt_ref[h * LANES:(h + 1) * LANES, :]), NEG)
              for h in heads]
        m_prev = [m_s[h] for h in heads]
        l_prev = [l_s[h] for h in heads]
        m_new = [jnp.maximum(m_prev[h], jnp.max(ss[h], axis=1, keepdims=True)) for h in heads]
        al = [jnp.exp(m_prev[h] - m_new[h]) for h in heads]
        ps = [jnp.exp(ss[h] - m_new[h][:, :1]) for h in heads]
        l_new = [al[h] * l_prev[h] + jnp.sum(ps[h], axis=1, keepdims=True) for h in heads]
        pv = [_dot(ps[h].astype(BF), vv) for h in heads]
        for h in heads:
            m_s[h] = m_new[h]
            l_s[h] = l_new[h]
        acc[...] = jnp.where(lo, al[0], al[1]) * acc[...] + jnp.where(lo, pv[0], pv[1])

        @pl.when(ki == qi)
        def _():
            o_ref[...] = acc[...] / jnp.where(lo, l_s[0], l_s[1])
            lse = jnp.where(lo, m_s[0] + jnp.log(l_s[0]), m_s[1] + jnp.log(l_s[1]))
            lse_ref[0] = lse
            lse_t_ref[0] = _pair_rows(lse)

    return _flash_call(
        body, "mla_flash_fwd", exs,
        [pl.BlockSpec((t, 256), lambda p, s: (_tri_rows(s, n)[0], p)),
         pl.BlockSpec((256, t), lambda p, s: (p, _tri_rows(s, n)[1])),
         pl.BlockSpec((t, 128), lambda p, s: (_tri_rows(s, n)[1], p))],
        [pl.BlockSpec((t, 128), lambda p, s: (_tri_rows(s, n)[0], p)),
         pl.BlockSpec((1, t, 128), lambda p, s: (p, _tri_rows(s, n)[0], 0)),
         pl.BlockSpec((1, SUBLANES, t), lambda p, s: (p, 0, _tri_rows(s, n)[0]))],
        [jax.ShapeDtypeStruct((S, 512), F32), jax.ShapeDtypeStruct((MLA_HEADS // 2, S, 128), F32),
         jax.ShapeDtypeStruct((MLA_HEADS // 2, SUBLANES, S), F32)],
        [pltpu.VMEM((2, t, 128), F32), pltpu.VMEM((2, t, 128), F32), pltpu.VMEM((t, 128), F32)], n, (q, kt, v))


def _flash_bwd_dq(q, k, kt, vt, do, lse, delta, exs=()):
    S = q.shape[0]
    t = min(T_ATT, S)
    n = S // t

    def body(q_ref, k_ref, kt_ref, vt_ref, do_ref, lse_ref, dl_ref, dq_ref, acc):
        qi, ki = _tri_rows(pl.program_id(1), n)
        lo = lax.broadcasted_iota(jnp.int32, (t, LANES), 1) < MLA_V

        @pl.when(ki == 0)
        def _():
            acc[...] = jnp.zeros_like(acc)

        keep = _lower_tri(t) | (ki < qi)
        heads = range(2)
        cs = [slice(h * LANES, (h + 1) * LANES) for h in heads]
        col = [slice(h * MLA_V, h * MLA_V + 1) for h in heads]
        lse, dl, dov, vt = lse_ref[0], dl_ref[0], do_ref[...], vt_ref[...]
        ss = [jnp.where(keep, _dot(q_ref[:, cs[h]], kt_ref[cs[h], :]), NEG) for h in heads]
        dp = [_dot(jnp.where(lo if h == 0 else ~lo, dov, 0).astype(BF), vt) for h in heads]
        ds = [(jnp.exp(ss[h] - lse[:, col[h]]) * (dp[h] - dl[:, col[h]])).astype(BF) for h in heads]
        dq = [_dot(ds[h], k_ref[:, cs[h]]) for h in heads]
        acc[...] += jnp.concatenate(dq, axis=1)

        @pl.when(ki == qi)
        def _():
            dq_ref[...] = acc[...]

    (dq,), got = _flash_call(
        body, "mla_flash_dq", exs,
        [pl.BlockSpec((t, 256), lambda p, s: (_tri_rows(s, n)[0], p)),
         pl.BlockSpec((t, 256), lambda p, s: (_tri_rows(s, n)[1], p)),
         pl.BlockSpec((256, t), lambda p, s: (p, _tri_rows(s, n)[1])),
         pl.BlockSpec((128, t), lambda p, s: (p, _tri_rows(s, n)[1])),
         pl.BlockSpec((t, 128), lambda p, s: (_tri_rows(s, n)[0], p)),
         pl.BlockSpec((1, t, 128), lambda p, s: (p, _tri_rows(s, n)[0], 0)),
         pl.BlockSpec((1, t, 128), lambda p, s: (p, _tri_rows(s, n)[0], 0))],
        [pl.BlockSpec((t, 256), lambda p, s: (_tri_rows(s, n)[0], p))],
        [jax.ShapeDtypeStruct((S, 1024), F32)],
        [pltpu.VMEM((t, 256), F32)], n, (q, k, kt, vt, do, lse, delta))
    return dq, got


def _flash_bwd_dkv(q, qt, k, v, do, dot_, lse_t, delta_t, exs=()):
    S = q.shape[0]
    t = min(T_ATT, S)
    n = S // t

    def body(q_ref, qt_ref, k_ref, v_ref, do_ref, dot_ref, lse_ref, dl_ref, dk_ref, dv_ref, dk_acc, dv_acc):
        qi, ki = _tri_cols(pl.program_id(1), n)
        lo = lax.broadcasted_iota(jnp.int32, (t, LANES), 1) < MLA_V
        top = lax.broadcasted_iota(jnp.int32, (LANES, t), 0) < MLA_V

        @pl.when(qi == ki)
        def _():
            dk_acc[...] = jnp.zeros_like(dk_acc)
            dv_acc[...] = jnp.zeros_like(dv_acc)

        keep = _upper_tri(t) | (qi > ki)
        heads = range(2)
        cs = [slice(h * LANES, (h + 1) * LANES) for h in heads]
        vv, lse, dl, dov, dot_v = v_ref[...], lse_ref[0], dl_ref[0], do_ref[...], dot_ref[...]
        st = [jnp.where(keep, _dot(k_ref[:, cs[h]], qt_ref[cs[h], :]), NEG) for h in heads]
        dpt = [_dot(vv, jnp.where(top if h == 0 else ~top, dot_v, 0).astype(BF)) for h in heads]
        pt = [jnp.exp(st[h] - lse[h:h + 1, :]) for h in heads]
        dst = [(pt[h] * (dpt[h] - dl[h:h + 1, :])).astype(BF) for h in heads]
        dv = [_dot(pt[h].astype(BF), jnp.where(lo if h == 0 else ~lo, dov, 0).astype(BF)) for h in heads]
        dk = [_dot(dst[h], q_ref[:, cs[h]]) for h in heads]
        dv_acc[...] += dv[0] + dv[1]
        dk_acc[...] += jnp.concatenate(dk, axis=1)

        @pl.when(qi == n - 1)
        def _():
            dk_ref[...] = dk_acc[...]
            dv_ref[...] = dv_acc[...]

    (dk, dv), got = _flash_call(
        body, "mla_flash_dkv", exs,
        [pl.BlockSpec((t, 256), lambda p, s: (_tri_cols(s, n)[0], p)),
         pl.BlockSpec((256, t), lambda p, s: (p, _tri_cols(s, n)[0])),
         pl.BlockSpec((t, 256), lambda p, s: (_tri_cols(s, n)[1], p)),
         pl.BlockSpec((t, 128), lambda p, s: (_tri_cols(s, n)[1], p)),
         pl.BlockSpec((t, 128), lambda p, s: (_tri_cols(s, n)[0], p)),
         pl.BlockSpec((128, t), lambda p, s: (p, _tri_cols(s, n)[0])),
         pl.BlockSpec((1, SUBLANES, t), lambda p, s: (p, 0, _tri_cols(s, n)[0])),
         pl.BlockSpec((1, SUBLANES, t), lambda p, s: (p, 0, _tri_cols(s, n)[0]))],
        [pl.BlockSpec((t, 256), lambda p, s: (_tri_cols(s, n)[1], p)),
         pl.BlockSpec((t, 128), lambda p, s: (_tri_cols(s, n)[1], p))],
        [jax.ShapeDtypeStruct((S, 1024), F32), jax.ShapeDtypeStruct((S, 512), F32)],
        [pltpu.VMEM((t, 256), F32), pltpu.VMEM((t, 128), F32)], n, (q, qt, k, v, do, dot_, lse_t, delta_t))
    return dk, dv, got


def _mem_heads(qm, km_ref, vm_ref):
    ps, os_ = [], []
    for h in range(X_HEADS):
        cs = slice(h * X_HEAD_DIM, (h + 1) * X_HEAD_DIM)
        s = _dot_nt(qm[:, cs].astype(BF), km_ref[:, cs]) * MEM_SCALE
        e = jnp.exp(s - jnp.max(s, axis=1, keepdims=True))
        p = e / jnp.sum(e, axis=1, keepdims=True)
        ps.append(p)
        os_.append(_dot(p.astype(BF), vm_ref[:, cs]))
    return ps, jnp.concatenate(os_, axis=1)


def _mem_fwd(proj, km, vm, p_mem):
    S = proj.shape[0]
    tr = min(T_ROWS, S)
    M = km.shape[0]

    def body(q_ref, z_ref, km_ref, vm_ref, p_ref, o_ref):
        _, o = _mem_heads(q_ref[...], km_ref, vm_ref)
        y = o * _silu(z_ref[...].astype(F32))
        o_ref[...] = _dot(y.astype(BF), p_ref[...]).astype(o_ref.dtype)

    return pl.pallas_call(
        body, name="mem_fwd", grid=(S // tr,),
        in_specs=[_rows(tr, 512, 4), _rows(tr, 512, 5), _full((M, 512)), _full((M, 512)), p_mem.spec()],
        out_specs=_rows(tr, 1024), out_shape=jax.ShapeDtypeStruct((S, D_MODEL), ACT),
        compiler_params=_cparams(("parallel",)),
    )(proj, proj, km, vm, p_mem.arr)


def _mem_bwd(do, proj, km, vm, p_mem):
    S = proj.shape[0]
    tr = min(T_ROWS_BWD, S)
    M = km.shape[0]

    def body(do_ref, q_ref, z_ref, km_ref, vm_ref, p_ref, dq_ref, dz_ref, dkm_ref, dvm_ref, dp_ref):
        @pl.when(pl.program_id(0) == 0)
        def _():
            dkm_ref[...] = jnp.zeros_like(dkm_ref)
            dvm_ref[...] = jnp.zeros_like(dvm_ref)
            dp_ref[...] = jnp.zeros_like(dp_ref)

        qm = q_ref[...]
        z = z_ref[...].astype(F32)
        ps, o = _mem_heads(qm, km_ref, vm_ref)
        sz = _silu(z)
        y = o * sz
        dob = do_ref[...].astype(BF)
        dy = _dot_nt(dob, p_ref[...])
        dp_ref[...] += _dot_tn(y.astype(BF), dob)
        dz_ref[...] = (dy * o * _dsilu(z)).astype(dz_ref.dtype)
        d_o = dy * sz
        for h in range(X_HEADS):
            cs = slice(h * X_HEAD_DIM, (h + 1) * X_HEAD_DIM)
            doh = d_o[:, cs]
            dohb = doh.astype(BF)
            p = ps[h]
            dpr = _dot_nt(dohb, vm_ref[:, cs])
            ds = (p * (dpr - jnp.sum(doh * o[:, cs], axis=1, keepdims=True)) * MEM_SCALE).astype(BF)
            dq_ref[:, cs] = _dot(ds, km_ref[:, cs]).astype(dq_ref.dtype)
            dkm_ref[:, cs] += _dot_tn(ds, qm[:, cs].astype(BF))
            dvm_ref[:, cs] += _dot_tn(p.astype(BF), dohb)

    return pl.pallas_call(
        body, name="mem_bwd", grid=(S // tr,),
        in_specs=[_rows(tr, 1024), _rows(tr, 512, 4), _rows(tr, 512, 5), _full((M, 512)), _full((M, 512)),
                  p_mem.spec()],
        out_specs=(_rows(tr, 512), _rows(tr, 512), _full((M, 512)), _full((M, 512)), _full((512, 1024))),
        out_shape=(jax.ShapeDtypeStruct((S, 512), BF), jax.ShapeDtypeStruct((S, 512), BF),
                   jax.ShapeDtypeStruct((M, 512), F32), jax.ShapeDtypeStruct((M, 512), F32),
                   jax.ShapeDtypeStruct((512, 1024), F32)),
        compiler_params=_cparams(("arbitrary",)),
    )(do, proj, proj, km, vm, p_mem.arr)


def _merge_fwd(x, proj, o_ssm, o_att, o_mem, b_gate, p_mla, w_out, ln_g, ln_b):
    S = x.shape[0]
    tr = min(T_ROWS, S)

    def body(x_ref, lg_ref, z_ref, os_ref, oa_ref, om_ref, bg_ref, p_ref, w_ref, g_ref, b_ref,
             xn_ref, xb_ref, pre_ref, mg_ref):
        gates = _sigmoid(lg_ref[...].astype(F32) + bg_ref[0])
        ya = oa_ref[...] * _silu(z_ref[...].astype(F32))
        o_mla = _dot(ya.astype(BF), p_ref[...])
        merged = (gates[:, :D_MODEL] * os_ref[...].astype(F32) + gates[:, D_MODEL:2 * D_MODEL] * o_mla
                  + gates[:, 2 * D_MODEL:] * om_ref[...].astype(F32))
        mb = merged.astype(BF)
        mg_ref[...] = mb
        pre = ALPHA * x_ref[...] + _dot(mb, w_ref[...])
        pre_ref[...] = pre
        mu = jnp.mean(pre, axis=-1, keepdims=True)
        xc = pre - mu
        var = jnp.mean(xc * xc, axis=-1, keepdims=True)
        xn = xc * lax.rsqrt(var + NORM_EPS) * g_ref[0] + b_ref[0]
        xn_ref[...] = xn
        xb_ref[...] = xn.astype(BF)

    return pl.pallas_call(
        body, name="merge_fwd", grid=(S // tr,),
        in_specs=[_rows(tr, 1024), _rows(tr, 3072, 1), _rows(tr, 512, 3), _rows(tr, 1024), _rows(tr, 512),
                  _rows(tr, 1024), b_gate.spec(), p_mla.spec(), _full((1024, 1024)), ln_g.spec(), ln_b.spec()],
        out_specs=(_rows(tr, 1024), _rows(tr, 1024), _rows(tr, 1024), _rows(tr, 1024)),
        out_shape=(jax.ShapeDtypeStruct((S, 1024), F32), jax.ShapeDtypeStruct((S, 1024), BF),
                   jax.ShapeDtypeStruct((S, 1024), F32), jax.ShapeDtypeStruct((S, 1024), BF)),
        compiler_params=_cparams(("parallel",)),
    )(x, proj, proj, o_ssm, o_att, o_mem, b_gate.arr, p_mla.arr, w_out, ln_g.arr, ln_b.arr)


def _merge_bwd(dxn, pre, merged, proj, o_ssm, o_att, o_mem, b_gate, p_mla, w_out, ln_g):
    S = pre.shape[0]
    tr = min(T_ROWS_BWD, S)

    def body(dxn_ref, pre_ref, mg_ref, lg_ref, z_ref, os_ref, oa_ref, om_ref, bg_ref, p_ref, w_ref, g_ref,
             dxr_ref, dlg_ref, dos_ref, dom_ref, doa_ref, dz_ref, doat_ref, dl_ref, dlt_ref, dw_ref, dp_ref, dbg_ref,
             dg_ref, db_ref):
        @pl.when(pl.program_id(0) == 0)
        def _():
            dw_ref[...] = jnp.zeros_like(dw_ref)
            dp_ref[...] = jnp.zeros_like(dp_ref)
            dbg_ref[...] = jnp.zeros_like(dbg_ref)
            dg_ref[...] = jnp.zeros_like(dg_ref)
            db_ref[...] = jnp.zeros_like(db_ref)

        dxn = dxn_ref[...]
        pre = pre_ref[...]
        mu = jnp.mean(pre, axis=-1, keepdims=True)
        xc = pre - mu
        rstd = lax.rsqrt(jnp.mean(xc * xc, axis=-1, keepdims=True) + NORM_EPS)
        xhat = xc * rstd
        dg_ref[...] += jnp.sum(dxn * xhat, axis=0, keepdims=True)
        db_ref[...] += jnp.sum(dxn, axis=0, keepdims=True)
        dxh = dxn * g_ref[0]
        dpre = rstd * (dxh - jnp.mean(dxh, axis=-1, keepdims=True)
                       - xhat * jnp.mean(dxh * xhat, axis=-1, keepdims=True))
        dxr_ref[...] = ALPHA * dpre
        dpb = dpre.astype(BF)
        dw_ref[...] += _dot_tn(mg_ref[...], dpb)
        dm = _dot_nt(dpb, w_ref[...])

        gates = _sigmoid(lg_ref[...].astype(F32) + bg_ref[0])
        g0, g1, g2 = gates[:, :D_MODEL], gates[:, D_MODEL:2 * D_MODEL], gates[:, 2 * D_MODEL:]
        z = z_ref[...].astype(F32)
        oa = oa_ref[...]
        sz = _silu(z)
        ya = (oa * sz).astype(BF)
        o_mla = _dot(ya, p_ref[...])
        dos_ref[...] = (g0 * dm).astype(dos_ref.dtype)
        dom_ref[...] = (g2 * dm).astype(dom_ref.dtype)
        do_mla = (g1 * dm).astype(BF)
        dl0 = dm * os_ref[...].astype(F32) * g0 * (1.0 - g0)
        dl1 = dm * o_mla * g1 * (1.0 - g1)
        dl2 = dm * om_ref[...].astype(F32) * g2 * (1.0 - g2)
        dl = jnp.concatenate([dl0, dl1, dl2], axis=1)
        dbg_ref[...] += jnp.sum(dl, axis=0, keepdims=True)
        dlg_ref[...] = dl.astype(dlg_ref.dtype)
        dp_ref[...] += _dot_tn(ya, do_mla)
        dya = _dot_nt(do_mla, p_ref[...])
        doa = dya * sz
        doab = doa.astype(BF)
        doa_ref[...] = doab
        doat_ref[...] = doab.T
        dz_ref[...] = (dya * oa * _dsilu(z)).astype(dz_ref.dtype)
        prod = doa * oa
        lo = lax.broadcasted_iota(jnp.int32, (tr, LANES), 1) < MLA_V
        for pr in range(MLA_HEADS // 2):
            blk = prod[:, pr * LANES:(pr + 1) * LANES]
            d0 = jnp.sum(jnp.where(lo, blk, 0.0), axis=1, keepdims=True)
            d1 = jnp.sum(jnp.where(lo, 0.0, blk), axis=1, keepdims=True)
            dl = jnp.where(lo, d0, d1)
            dl_ref[pr] = dl
            dlt_ref[pr] = _pair_rows(dl)

    return pl.pallas_call(
        body, name="merge_bwd", grid=(S // tr,),
        in_specs=[_rows(tr, 1024), _rows(tr, 1024), _rows(tr, 1024), _rows(tr, 3072, 1), _rows(tr, 512, 3),
                  _rows(tr, 1024), _rows(tr, 512), _rows(tr, 1024), b_gate.spec(), p_mla.spec(),
                  _full((1024, 1024)), ln_g.spec()],
        out_specs=(_rows(tr, 1024), _rows(tr, 3072), _rows(tr, 1024), _rows(tr, 1024), _rows(tr, 512),
                   _rows(tr, 512), _cols(512, tr), pl.BlockSpec((MLA_HEADS // 2, tr, 128), lambda i: (0, i, 0)),
                   pl.BlockSpec((MLA_HEADS // 2, SUBLANES, tr), lambda i: (0, 0, i)),
                   _full((1024, 1024)), _full((512, 1024)), _full((1, 3072)), _full((1, 1024)), _full((1, 1024))),
        out_shape=(jax.ShapeDtypeStruct((S, 1024), F32), jax.ShapeDtypeStruct((S, 3072), BF),
                   jax.ShapeDtypeStruct((S, 1024), BF), jax.ShapeDtypeStruct((S, 1024), BF),
                   jax.ShapeDtypeStruct((S, 512), BF), jax.ShapeDtypeStruct((S, 512), BF),
                   jax.ShapeDtypeStruct((512, S), BF), jax.ShapeDtypeStruct((MLA_HEADS // 2, S, 128), F32),
                   jax.ShapeDtypeStruct((MLA_HEADS // 2, SUBLANES, S), F32),
                   jax.ShapeDtypeStruct((1024, 1024), F32), jax.ShapeDtypeStruct((512, 1024), F32),
                   jax.ShapeDtypeStruct((1, 3072), F32), jax.ShapeDtypeStruct((1, 1024), F32),
                   jax.ShapeDtypeStruct((1, 1024), F32)),
        compiler_params=_cparams(("arbitrary",)),
    )(dxn, pre, merged, proj, proj, o_ssm, o_att, o_mem, b_gate.arr, p_mla.arr, w_out, ln_g.arr)


def _loss_head(y, t):
    S = y.shape[0]
    tr = min(T_ROWS, S)
    n = S // tr

    def body(y_ref, t_ref, dy_ref, l_ref, acc):
        i = pl.program_id(0)

        @pl.when(i == 0)
        def _():
            acc[...] = jnp.zeros_like(acc)

        e = y_ref[...] - t_ref[...]
        dy_ref[...] = e * (1.0 / D_MODEL)
        acc[...] += jnp.sum(e * e, axis=0, keepdims=True)

        @pl.when(i == n - 1)
        def _():
            tot = jnp.sum(acc[...], axis=1, keepdims=True) * (0.5 / D_MODEL)
            l_ref[...] = jnp.broadcast_to(tot, l_ref.shape)

    return pl.pallas_call(
        body, name="loss_head", grid=(n,),
        in_specs=[_rows(tr, 1024), _rows(tr, 1024)],
        out_specs=(_rows(tr, 1024), _full((SUBLANES, LANES))),
        out_shape=(jax.ShapeDtypeStruct((S, 1024), F32), jax.ShapeDtypeStruct((SUBLANES, LANES), F32)),
        scratch_shapes=[pltpu.VMEM((1, 1024), F32)],
        compiler_params=_cparams(("arbitrary",)),
    )(y, t)


def _rope_tables(pos):
    inv_freq = ROPE_THETA ** (-jnp.arange(0, MLA_ROPE, 2, dtype=F32) / MLA_ROPE)
    ang = pos.astype(F32)[:, None] * inv_freq
    cos, sin = jnp.cos(ang), jnp.sin(ang)
    S = pos.shape[0]
    half = MLA_ROPE // 2
    ones = jnp.ones((S, MLA_NOPE), F32)
    z16 = jnp.zeros((S, half), F32)
    z32 = jnp.zeros((S, LANES - MLA_NOPE - MLA_ROPE), F32)
    z64 = jnp.zeros((S, MLA_NOPE), F32)
    c = jnp.concatenate([ones, cos, cos, z32], axis=1)
    sa = jnp.concatenate([z64, -sin, z16, z32], axis=1)
    sb = jnp.concatenate([z64, z16, sin, z32], axis=1)
    return c, sa, sb


def _ssm_discretise(a_re, a_im, log_dt, b_re, b_im):
    dt = jnp.exp(log_dt)[..., None]
    mag = jnp.exp(a_re * dt)
    lb_re = mag * jnp.cos(a_im * dt)
    lb_im = mag * jnp.sin(a_im * dt)
    nr, ni = lb_re - 1.0, lb_im
    den = a_re * a_re + a_im * a_im
    f_re = (nr * a_re + ni * a_im) / den
    f_im = (ni * a_re - nr * a_im) / den
    bb_re = f_re[..., None] * b_re - f_im[..., None] * b_im
    bb_im = f_re[..., None] * b_im + f_im[..., None] * b_re
    return lb_re, lb_im, bb_re, bb_im


_GPB = SSM_CB // SSM_GROUP


def _bd_in(bb):
    nb = SSM_GROUPS // _GPB
    t = bb.reshape(nb, _GPB, SSM_STATE, SSM_GROUP)
    eye = jnp.eye(_GPB, dtype=bb.dtype)
    return jnp.einsum("ngpc,gh->ngchp", t, eye).reshape(nb, SSM_CB, _GPB * SSM_STATE)


def _bd_in_t(d):
    nb = SSM_GROUPS // _GPB
    t = d.reshape(nb, _GPB, SSM_GROUP, _GPB, SSM_STATE)
    eye = jnp.eye(_GPB, dtype=d.dtype)
    return jnp.einsum("ngchp,gh->ngpc", t, eye).reshape(SSM_GROUPS, SSM_STATE, SSM_GROUP)


def _bd_out(c):
    nb = SSM_GROUPS // _GPB
    t = c.reshape(nb, _GPB, SSM_GROUP, SSM_STATE)
    eye = jnp.eye(_GPB, dtype=c.dtype)
    return jnp.einsum("ngcp,gh->ngphc", t, eye).reshape(nb, _GPB * SSM_STATE, SSM_CB)


def _interleave(a):
    S, w = a.shape
    return a.reshape(SUBLANES, S // SUBLANES, w).transpose(1, 0, 2).reshape(S, w)


def _deinterleave(a):
    S, w = a.shape
    return a.reshape(S // SUBLANES, SUBLANES, w).transpose(1, 0, 2).reshape(S, w)


IN_SHARD = D_IN // N_DEV
ROPE_OWNER = ROPE_SLOT_LO // IN_SHARD
assert ROPE_OWNER * IN_SHARD <= ROPE_SLOT_LO and ROPE_SLOT_LO + MLA_ROPE <= (ROPE_OWNER + 1) * IN_SHARD


def _w_in_from_shards(g):
    pieces = []
    for j in range(N_DEV):
        if j == ROPE_OWNER:
            a = ROPE_SLOT_LO - j * IN_SHARD
            z = lambda n: jnp.zeros((g.shape[1], n), g.dtype)
            pieces += [g[j][:, :a], z(MLA_NOPE), g[j][:, a:a + MLA_ROPE], z(LANES - MLA_NOPE - MLA_ROPE),
                       g[j][:, a + MLA_ROPE:]]
        else:
            pieces.append(g[j])
    return jnp.concatenate(pieces, axis=1)


def _w_in_to_shards(d):
    shift = LANES - MLA_ROPE
    out = []
    for j in range(N_DEV):
        lo, hi = j * IN_SHARD, (j + 1) * IN_SHARD
        if j < ROPE_OWNER:
            out.append(d[:, lo:hi])
        elif j > ROPE_OWNER:
            out.append(d[:, lo + shift:hi + shift])
        else:
            r = ROPE_SLOT_LO + MLA_NOPE
            out.append(jnp.concatenate([d[:, lo:ROPE_SLOT_LO], d[:, r:r + MLA_ROPE],
                                        d[:, ROPE_SLOT_LO + LANES:hi + shift]], axis=1))
    return jnp.stack(out)


def _adamw_math(w, g, m, v):
    m = ADAM_B1 * m + (1.0 - ADAM_B1) * g
    v = ADAM_B2 * v + (1.0 - ADAM_B2) * (g * g)
    m_hat = m / (1.0 - ADAM_B1 ** ADAM_STEP)
    v_hat = v / (1.0 - ADAM_B2 ** ADAM_STEP)
    delta = -ADAM_LR * (m_hat / (jnp.sqrt(v_hat) + ADAM_EPS) + ADAM_WD * w)
    return delta, m, v


def _adamw_sharded(parts, w, m, v, tile, name):
    L, _, R, C = parts.shape
    assert R % tile == 0

    def body(p_ref, w_ref, m_ref, v_ref, g_out, d_out, m_out, v_out):
        g = p_ref[0, 0].astype(F32)
        for k in range(1, N_DEV):
            g = g + p_ref[0, k].astype(F32)
        d, mn, vn = _adamw_math(w_ref[0], g, m_ref[0], v_ref[0])
        g_out[0] = g
        d_out[0] = d
        m_out[0] = mn
        v_out[0] = vn

    spec = pl.BlockSpec((1, tile, C), lambda l, i: (l, i, 0))
    shp = jax.ShapeDtypeStruct((L, R, C), F32)
    return pl.pallas_call(
        body, name=name, grid=(L, R // tile),
        in_specs=[pl.BlockSpec((1, N_DEV, tile, C), lambda l, i: (l, 0, i, 0)), spec, spec, spec],
        out_specs=(spec,) * 4, out_shape=(shp,) * 4, compiler_params=_cparams(("parallel", "parallel")),
    )(parts, w, m, v)


COL_GROUP = (("w_glu", 512), ("p_ssm", 512), ("p_mla", 512), ("p_mem", 512), ("w_uq", 256), ("w_ukv", 128))
COL_AT = {n: sum(r for _, r in COL_GROUP[:i]) // rows for i, (n, rows) in enumerate(COL_GROUP)}
assert all(sum(r for _, r in COL_GROUP[:i]) % rows == 0 for i, (_, rows) in enumerate(COL_GROUP))
COL_ROWS = dict(COL_GROUP)
ROW_GROUP = ("w_mem_kv", "w_out")
SMALL = ("b_gate", "ssm_a_re", "ssm_a_im", "ssm_log_dt", "ssm_b_re", "ssm_b_im", "ssm_c_re", "ssm_c_im", "ssm_d",
         "b_glu", "mla_q_norm", "mla_kv_norm", "ln_g", "ln_b")
SMALL_TILE = 512
UQ_COLS = MLA_NOPE + MLA_ROPE


def _pad_lanes(a):
    return jnp.concatenate([a, jnp.zeros(a.shape[:-1] + (LANES - a.shape[-1],), a.dtype)], axis=-1)


def _group_buffers(d, dtype):
    col = jnp.concatenate([_pad_lanes(d[n]) if n == "w_uq" else d[n] for n, _ in COL_GROUP], axis=1)
    row = jnp.concatenate([d[n] for n in ROW_GROUP], axis=1)
    return d["w_in"].astype(dtype), col.astype(dtype), row.astype(dtype)


def _ungroup(bufs):
    b_in, col, row = bufs
    out, off = {"w_in": b_in}, 0
    for n, rows in COL_GROUP:
        t = col[:, off:off + rows]
        out[n] = t[..., :UQ_COLS] if n == "w_uq" else t
        off += rows
    k = row.shape[1] // 2
    out["w_mem_kv"], out["w_out"] = row[:, :k], row[:, k:]
    return out


def _colcat(t):
    return t.transpose(1, 0, 2).reshape(t.shape[1], -1)


def _colsplit(g, n):
    return g.reshape(g.shape[0], N_DEV, n).transpose(1, 0, 2)


def _unpack_weights(g_in, g_col, g_row):
    wc = _colcat(g_col)
    at = lambda n: _RowBlock(wc, COL_ROWS[n], COL_AT[n])
    lo = COL_AT["w_ukv"] * COL_ROWS["w_ukv"]
    ukv = wc[lo:lo + COL_ROWS["w_ukv"]].reshape(-1, MLA_HEADS, LANES)
    lane = lax.broadcasted_iota(jnp.int32, ukv.shape, 2)
    k = g_row.shape[1] // 2
    return dict(
        w_in=_w_in_from_shards(g_in), w_glu=at("w_glu"), w_uq=at("w_uq"), p_ssm=at("p_ssm"), p_mla=at("p_mla"),
        p_mem=at("p_mem"), w_k=jnp.where(lane < MLA_NOPE, ukv, jnp.zeros_like(ukv)).reshape(ukv.shape[0], -1),
        w_v=ukv[..., MLA_NOPE:].reshape(ukv.shape[0], -1),
        w_mem_kv=g_row[:, :k].reshape(-1, g_row.shape[2]), w_out=g_row[:, k:].reshape(-1, g_row.shape[2]))


def _pack_grads_in(d_w_in):
    return _w_in_to_shards(d_w_in).astype(BF)


def _pack_grads_rest(d):
    ukv = jnp.concatenate([d["w_k"].reshape(-1, MLA_HEADS, LANES)[..., :MLA_NOPE],
                           d["w_v"].reshape(-1, MLA_HEADS, MLA_V)], axis=-1).reshape(d["w_k"].shape[0], -1)
    col = jnp.concatenate([ukv if n == "w_ukv" else d[n] for n, _ in COL_GROUP], axis=0)
    row = jnp.concatenate([d[n].reshape(N_DEV, -1, d[n].shape[1]) for n in ROW_GROUP], axis=1)
    return [_colsplit(col, LANES).astype(BF), row.astype(BF)]


def _pack_small(d):
    parts = []
    for n in SMALL:
        f = d[n].reshape(-1)
        pad = (-f.shape[0]) % (SUBLANES * LANES)
        if pad:
            f = jnp.concatenate([f, jnp.zeros((pad,), f.dtype)])
        parts.append(f.reshape(-1, LANES))
    rows = sum(p.shape[0] for p in parts)
    pad = (-rows) % SMALL_TILE
    if pad:
        parts.append(jnp.zeros((pad, LANES), parts[0].dtype))
    return jnp.concatenate(parts, axis=0)


def _unpack_small(buf, like):
    out, off = {}, 0
    for n in SMALL:
        size = math.prod(like[n].shape)
        rows = -(-size // (SUBLANES * LANES)) * SUBLANES
        out[n] = buf[off:off + rows].reshape(-1)[:size].reshape(like[n].shape)
        off += rows
    return out


WEIGHTS = ("w_in", "b_gate", "ssm_a_re", "ssm_a_im", "ssm_log_dt", "ssm_b_re", "ssm_b_im", "ssm_c_re", "ssm_c_im",
           "ssm_d", "w_glu", "b_glu", "mla_q_norm", "w_uq", "mla_kv_norm", "w_ukv", "w_mem_kv", "p_ssm", "p_mla",
           "p_mem", "w_out", "ln_g", "ln_b")
BIG = ("w_in",) + tuple(n for n, _ in COL_GROUP) + ROW_GROUP


def _train_step(x, mem, pos, target, wl, ws):
    S = x.shape[0]
    tc, tsa, tsb = _rope_tables(pos)
    loc = _group_buffers(wl, BF)
    loc = [[b[l] for b in loc] for l in range(DEPTH)]

    lb_re, lb_im, bb_re, bb_im = _ssm_discretise(ws["ssm_a_re"], ws["ssm_a_im"], ws["ssm_log_dt"], ws["ssm_b_re"],
                                                 ws["ssm_b_im"])
    nb = SSM_GROUPS // _GPB
    mats = (jax.vmap(_bd_in)(bb_re), jax.vmap(_bd_in)(bb_im), lb_re.reshape(DEPTH, nb, 1, -1),
            lb_im.reshape(DEPTH, nb, 1, -1), jax.vmap(_bd_out)(ws["ssm_c_re"]), jax.vmap(_bd_out)(ws["ssm_c_im"]),
            ws["ssm_d"].reshape(DEPTH, 1, -1))

    rows3 = {n: ws[n].reshape(DEPTH, 1, -1) for n in ("b_glu", "mla_q_norm", "mla_kv_norm", "b_gate", "ln_g", "ln_b")}

    def small(n, l):
        return _LayerRow(rows3[n], l)

    (gathered,) = _exchange_call("weights_gather_first", [_Exchange("ag", loc[0])])
    W = [None] * DEPTH
    W[0] = _unpack_weights(*gathered)
    saved = []
    xs, xb = x, x.astype(BF)
    for l in range(DEPTH):
        w = W[l]
        proj = _mm(xb, w["w_in"], name="proj_fwd", tm=S, tn=512, out_dtype=ACT)
        u_il = _interleave(proj[:, :SSM_WIDTH])
        y_raw = _deinterleave(_ssm_fwd(u_il, mats, l))
        o_ssm = _ssm_post_fwd(y_raw, proj, w["w_glu"], small("b_glu", l), w["p_ssm"])
        q, k, v, qt, kt, vt = _mla_pre_fwd(proj, small("mla_q_norm", l), small("mla_kv_norm", l), w["w_uq"], w["w_k"], w["w_v"],
                               tc, tsa, tsb)
        nxt = [_Exchange("ag", loc[l + 1])] if l + 1 < DEPTH else []
        (o_att, lse, lse_t), gathered = _flash_fwd(q, kt, v, nxt)
        if nxt:
            W[l + 1] = _unpack_weights(*gathered[0])
        kvm = _mm(mem, w["w_mem_kv"], name="memkv_fwd", out_dtype=BF)
        km, vm = kvm[:, :512], kvm[:, 512:]
        o_mem = _mem_fwd(proj, km, vm, w["p_mem"])
        xn, xnb, pre, merged = _merge_fwd(xs, proj, o_ssm, o_att, o_mem, small("b_gate", l), w["p_mla"], w["w_out"],
                                          small("ln_g", l), small("ln_b", l))
        saved.append(dict(xb=xb, proj=proj, u_il=u_il, y_raw=y_raw, o_ssm=o_ssm, q=q, k=k, v=v, qt=qt, kt=kt, vt=vt, o_att=o_att,
                          lse=lse, lse_t=lse_t,
                          km=km, vm=vm, o_mem=o_mem, pre=pre, merged=merged))
        xs, xb = xn, xnb

    dxs, lvec = _loss_head(xs, target)
    loss = lvec[0, 0]

    gs = {n: [None] * DEPTH for n in SMALL}
    disc_ct = [None] * DEPTH
    got = [None] * DEPTH
    pending = None
    for l in reversed(range(DEPTH)):
        sv, w = saved[l], W[l]
        proj = sv["proj"]
        (dx_res, dlg, do_ssm, do_mem, do_att, dz_mla, do_att_t, delta, delta_t, d_w_out, d_p_mla, d_b_gate, d_ln_g,
         d_ln_b) = _merge_bwd(
            dxs, sv["pre"], sv["merged"], proj, sv["o_ssm"], sv["o_att"], sv["o_mem"], small("b_gate", l), w["p_mla"],
            w["w_out"], small("ln_g", l))
        dq_mem, dz_mem, d_km, d_vm, d_p_mem = _mem_bwd(do_mem, proj, sv["km"], sv["vm"], w["p_mem"])
        d_w_mem = _mm(mem, jnp.concatenate([d_km, d_vm], axis=1), name="memkv_bwd", ta=True)
        dq, _ = _flash_bwd_dq(sv["q"], sv["k"], sv["kt"], sv["vt"], do_att, sv["lse"], delta)
        exs = [_Exchange("a2a", pending)] if pending is not None else []
        dk, dv, arrived = _flash_bwd_dkv(sv["q"], sv["qt"], sv["k"], sv["v"], do_att, do_att_t, sv["lse_t"], delta_t, exs)
        if exs:
            got[l + 1] = arrived[0]
        dcq, dckv, dslot, d_wuq, d_wk, d_wv, d_qn, d_kn = _mla_pre_bwd(
            dq, dk, dv, proj, small("mla_q_norm", l), small("mla_kv_norm", l), w["w_uq"], w["w_k"], w["w_v"],
            tc, tsa, tsb)
        dy_raw, dz_ssm, d_w_glu, d_b_glu, d_p_ssm = _ssm_post_bwd(do_ssm, sv["y_raw"], proj, w["w_glu"],
                                                                 small("b_glu", l), w["p_ssm"])
        rest = _pack_grads_rest(dict(w_glu=d_w_glu, w_uq=d_wuq, w_k=d_wk, w_v=d_wv, w_mem_kv=d_w_mem, p_ssm=d_p_ssm,
                                     p_mla=d_p_mla, p_mem=d_p_mem, w_out=d_w_out))
        (du_il, dbbre, dbbim, dare, daim, dcre, dcim, dd), early = _ssm_bwd(
            sv["u_il"], _interleave(dy_raw), mats, l, [_Exchange("a2a", rest)] if l == 0 else [])
        du = _deinterleave(du_il).astype(BF)
        disc_ct[l] = (dare.reshape(SSM_GROUPS, SSM_STATE), daim.reshape(SSM_GROUPS, SSM_STATE), _bd_in_t(dbbre),
                      _bd_in_t(dbbim))
        dproj = jnp.concatenate([du, dz_ssm, dcq, dckv, dslot, dz_mla, dq_mem, dz_mem, dlg], axis=1)
        d_w_in = _mm(sv["xb"], dproj, name="proj_dw", ta=True, tm=1024, tn=512, tk=S)
        if l > 0:
            dxs = _mm(dproj, w["w_in"], name="proj_dx", tb=True, add=dx_res, tm=1024, tn=1024, tk=1024)
        pending = [_pack_grads_in(d_w_in)] + (rest if l > 0 else [])
        gs["b_gate"][l] = d_b_gate.reshape(-1)
        gs["ssm_c_re"][l] = _bd_in_t(dcre).transpose(0, 2, 1)
        gs["ssm_c_im"][l] = _bd_in_t(dcim).transpose(0, 2, 1)
        gs["ssm_d"][l] = dd.reshape(-1)
        gs["b_glu"][l] = d_b_glu.reshape(-1)
        gs["mla_q_norm"][l] = d_qn.reshape(-1)
        gs["mla_kv_norm"][l] = d_kn.reshape(-1)
        gs["ln_g"][l] = d_ln_g.reshape(-1)
        gs["ln_b"][l] = d_ln_b.reshape(-1)

    disc_in = tuple(ws[n] for n in ("ssm_a_re", "ssm_a_im", "ssm_log_dt", "ssm_b_re", "ssm_b_im"))
    _, disc_vjp = jax.vjp(_ssm_discretise, *disc_in)
    d_disc = disc_vjp(tuple(jnp.stack([disc_ct[l][i] for l in range(DEPTH)]) for i in range(4)))
    gsm = {n: jnp.stack(v) for n, v in gs.items() if v[0] is not None}
    for n, g in zip(("ssm_a_re", "ssm_a_im", "ssm_log_dt", "ssm_b_re", "ssm_b_im"), d_disc):
        gsm[n] = g
    dxs, (last_in, (small_parts,)) = _mm(
        dproj, w["w_in"], name="proj_dx", tb=True, add=dx_res, tm=1024, tn=1024, tk=1024,
        exs=[_Exchange("a2a", pending), _Exchange("ag", [_pack_small(gsm)])])
    got[0] = last_in + early[0]
    return loss, dxs, got, small_parts


def kernel(x, mem, positions, w_in, b_gate, ssm_a_re, ssm_a_im, ssm_log_dt, ssm_b_re, ssm_b_im, ssm_c_re, ssm_c_im, ssm_d, w_glu, b_glu, mla_q_norm, w_uq, mla_kv_norm, w_ukv, w_mem_kv, p_ssm, p_mla, p_mem, w_out, ln_g, ln_b, loss_target, m_w_in, m_b_gate, m_ssm_a_re, m_ssm_a_im, m_ssm_log_dt, m_ssm_b_re, m_ssm_b_im, m_ssm_c_re, m_ssm_c_im, m_ssm_d, m_w_glu, m_b_glu, m_mla_q_norm, m_w_uq, m_mla_kv_norm, m_w_ukv, m_w_mem_kv, m_p_ssm, m_p_mla, m_p_mem, m_w_out, m_ln_g, m_ln_b, v_w_in, v_b_gate, v_ssm_a_re, v_ssm_a_im, v_ssm_log_dt, v_ssm_b_re, v_ssm_b_im, v_ssm_c_re, v_ssm_c_im, v_ssm_d, v_w_glu, v_b_glu, v_mla_q_norm, v_w_uq, v_mla_kv_norm, v_w_ukv, v_w_mem_kv, v_p_ssm, v_p_mla, v_p_mem, v_w_out, v_ln_g, v_ln_b):
    w = dict(w_in=w_in, b_gate=b_gate, ssm_a_re=ssm_a_re, ssm_a_im=ssm_a_im, ssm_log_dt=ssm_log_dt, ssm_b_re=ssm_b_re,
             ssm_b_im=ssm_b_im, ssm_c_re=ssm_c_re, ssm_c_im=ssm_c_im, ssm_d=ssm_d, w_glu=w_glu, b_glu=b_glu,
             mla_q_norm=mla_q_norm, w_uq=w_uq, mla_kv_norm=mla_kv_norm, w_ukv=w_ukv, w_mem_kv=w_mem_kv, p_ssm=p_ssm,
             p_mla=p_mla, p_mem=p_mem, w_out=w_out, ln_g=ln_g, ln_b=ln_b)
    m = dict(w_in=m_w_in, b_gate=m_b_gate, ssm_a_re=m_ssm_a_re, ssm_a_im=m_ssm_a_im, ssm_log_dt=m_ssm_log_dt,
             ssm_b_re=m_ssm_b_re, ssm_b_im=m_ssm_b_im, ssm_c_re=m_ssm_c_re, ssm_c_im=m_ssm_c_im, ssm_d=m_ssm_d,
             w_glu=m_w_glu, b_glu=m_b_glu, mla_q_norm=m_mla_q_norm, w_uq=m_w_uq, mla_kv_norm=m_mla_kv_norm,
             w_ukv=m_w_ukv, w_mem_kv=m_w_mem_kv, p_ssm=m_p_ssm, p_mla=m_p_mla, p_mem=m_p_mem, w_out=m_w_out,
             ln_g=m_ln_g, ln_b=m_ln_b)
    v = dict(w_in=v_w_in, b_gate=v_b_gate, ssm_a_re=v_ssm_a_re, ssm_a_im=v_ssm_a_im, ssm_log_dt=v_ssm_log_dt,
             ssm_b_re=v_ssm_b_re, ssm_b_im=v_ssm_b_im, ssm_c_re=v_ssm_c_re, ssm_c_im=v_ssm_c_im, ssm_d=v_ssm_d,
             w_glu=v_w_glu, b_glu=v_b_glu, mla_q_norm=v_mla_q_norm, w_uq=v_w_uq, mla_kv_norm=v_mla_kv_norm,
             w_ukv=v_w_ukv, w_mem_kv=v_w_mem_kv, p_ssm=v_p_ssm, p_mla=v_p_mla, p_mem=v_p_mem, w_out=v_w_out,
             ln_g=v_ln_g, ln_b=v_ln_b)

    wl = {n: w[n] for n in BIG}
    small = {n: w[n] for n in SMALL}
    loss_local, dx, got, small_parts = _train_step(x[0], mem[0], positions[0], loss_target[0], wl, small)
    loss = lax.psum(loss_local, ("x", "y", "c"))

    grads, delta, new_m, new_v = {}, {}, {}, {}
    wg = _group_buffers(wl, F32)
    mg = _group_buffers({n: m[n] for n in BIG}, F32)
    vg = _group_buffers({n: v[n] for n in BIG}, F32)
    res = []
    for i, (tile, tag) in enumerate(((256, "in"), (128, "col"), (256, "row"))):
        parts = jnp.stack([got[l][i] for l in range(DEPTH)])
        res.append(_adamw_sharded(parts, wg[i], mg[i], vg[i], tile, "adamw_" + tag))
    for dst, j in ((grads, 0), (delta, 1), (new_m, 2), (new_v, 3)):
        dst.update(_ungroup([r[j] for r in res]))

    sw, sm, sv = (_pack_small(small), _pack_small({n: m[n] for n in SMALL}), _pack_small({n: v[n] for n in SMALL}))
    rs = _adamw_sharded(small_parts[None], sw[None], sm[None], sv[None], SMALL_TILE, "adamw_replicated")
    for dst, buf in zip((grads, delta, new_m, new_v), rs):
        dst.update(_unpack_small(buf[0], small))

    return (loss, dx[None], *[grads[n] for n in WEIGHTS], *[delta[n] for n in WEIGHTS],
            *[new_m[n] for n in WEIGHTS], *[new_v[n] for n in WEIGHTS])
```

```python
import math

import jax
import jax.numpy as jnp
from jax import lax
from jax.experimental import pallas as pl
from jax.experimental.pallas import tpu as pltpu

F32 = jnp.float32
BF = jnp.bfloat16
ACT = jnp.bfloat16

D_MODEL = 1024
DEPTH = 4
N_DEV = 8
SSM_WIDTH = 512
SSM_GROUP = 16
SSM_GROUPS = 32
SSM_STATE = 64
MLA_HEADS = 8
MLA_NOPE = 64
MLA_ROPE = 32
MLA_V = 64
MLA_Q_RANK = 256
MLA_KV_RANK = 128
ROPE_THETA = 10000.0
X_HEADS = 4
X_HEAD_DIM = 128
D_IN = 6048
ALPHA = (2 * DEPTH) ** 0.25
NORM_EPS = 1e-5
ADAM_LR = 0.001
ADAM_B1 = 0.9
ADAM_B2 = 0.999
ADAM_EPS = 1e-08
ADAM_WD = 0.01
ADAM_STEP = 10

LANES = 128
SUBLANES = 8
VMEM_LIMIT = 56 * 1024 * 1024

PW = 6144
ROPE_SLOT_LO = 1408
MLA_SCALE = (MLA_NOPE + MLA_ROPE) ** -0.5
MEM_SCALE = X_HEAD_DIM ** -0.5
NEG = -1e30

T_ROWS = 512
T_ROWS_BWD = 256
T_ATT = 1024
T_MM = 512

MESH = pl.DeviceIdType.MESH


def _cparams(sem):
    return pltpu.CompilerParams(dimension_semantics=sem, vmem_limit_bytes=VMEM_LIMIT)


def _dot(a, b):
    return lax.dot_general(a, b, (((1,), (0,)), ((), ())), preferred_element_type=F32)


def _dot_nt(a, b):
    return lax.dot_general(a, b, (((1,), (1,)), ((), ())), preferred_element_type=F32)


def _dot_tn(a, b):
    return lax.dot_general(a, b, (((0,), (0,)), ((), ())), preferred_element_type=F32)


def _sigmoid(x):
    return 0.5 * jnp.tanh(0.5 * x) + 0.5


def _silu(x):
    return x * _sigmoid(x)


def _dsilu(x):
    s = _sigmoid(x)
    return s * (1.0 + x * (1.0 - s))


_GELU_C = math.sqrt(2.0 / math.pi)


def _gelu(x):
    return 0.5 * x * (1.0 + jnp.tanh(_GELU_C * (x + 0.044715 * x * x * x)))


def _dgelu(x):
    t = jnp.tanh(_GELU_C * (x + 0.044715 * x * x * x))
    return 0.5 * (1.0 + t) + 0.5 * x * (1.0 - t * t) * _GELU_C * (1.0 + 3 * 0.044715 * x * x)


def _rows(tr, w, col=0):
    return pl.BlockSpec((tr, w), lambda i: (i, col))


def _cols(h, tc):
    return pl.BlockSpec((h, tc), lambda i: (0, i))


def _full(shape):
    n = len(shape)
    return pl.BlockSpec(shape, lambda i: (0,) * n)


class _RowBlock:
    def __init__(self, arr, rows, blk):
        self.arr, self.rows, self.blk = arr, rows, blk

    def spec(self):
        blk = self.blk
        return pl.BlockSpec((self.rows, self.arr.shape[1]), lambda i: (blk, 0))


class _LayerRow:
    def __init__(self, arr, l):
        self.arr, self.l = arr, l

    def spec(self):
        l = self.l
        return pl.BlockSpec((1, 1, self.arr.shape[2]), lambda i: (l, 0, 0))


def _mm(a, b, *, name, ta=False, tb=False, out_dtype=F32, add=None, tm=T_MM, tn=T_MM, tk=1024, exs=None):
    M, K = (a.shape[1], a.shape[0]) if ta else a.shape
    N = b.shape[0] if tb else b.shape[1]
    tm, tn, tk = min(tm, M), min(tn, N), min(tk, K)
    assert M % tm == 0 and N % tn == 0 and K % tk == 0, (M, N, K)
    nk = K // tk
    dn = (((0 if ta else 1,), (1 if tb else 0,)), ((), ()))

    def body(*refs):
        if add is not None:
            a_ref, b_ref, c_ref, o_ref = refs[:4]
        else:
            a_ref, b_ref, o_ref = refs[:3]
        part = lax.dot_general(a_ref[...].astype(BF), b_ref[...].astype(BF), dn, preferred_element_type=F32)
        if nk == 1:
            if add is not None:
                part = part + c_ref[...]
            o_ref[...] = part.astype(out_dtype)
            return
        acc = refs[-1]
        k = pl.program_id(2)

        @pl.when(k == 0)
        def _():
            acc[...] = part

        @pl.when(k != 0)
        def _():
            acc[...] += part

        @pl.when(k == nk - 1)
        def _():
            r = acc[...]
            if add is not None:
                r = r + c_ref[...]
            o_ref[...] = r.astype(out_dtype)

    a_spec = pl.BlockSpec((tk, tm), lambda i, j, k: (k, i)) if ta else pl.BlockSpec((tm, tk), lambda i, j, k: (i, k))
    b_spec = pl.BlockSpec((tn, tk), lambda i, j, k: (j, k)) if tb else pl.BlockSpec((tk, tn), lambda i, j, k: (k, j))
    o_spec = pl.BlockSpec((tm, tn), lambda i, j, k: (i, j))
    in_specs = [a_spec, b_spec] + ([o_spec] if add is not None else [])
    args = (a, b) + ((add,) if add is not None else ())
    (out,), got = _carry_call(
        body, name, (M // tm, N // tn, nk), in_specs, [o_spec], [jax.ShapeDtypeStruct((M, N), out_dtype)],
        [pltpu.VMEM((tm, tn), F32)] if nk > 1 else [], ("parallel", "parallel", "arbitrary"), args, exs)
    return out if exs is None else (out, got)


def _cpow(ar, ai, n):
    rr, ri = None, None
    br, bi = ar, ai
    while n:
        if n & 1:
            if rr is None:
                rr, ri = br, bi
            else:
                rr, ri = rr * br - ri * bi, rr * bi + ri * br
        n >>= 1
        if n:
            br, bi = br * br - bi * bi, 2.0 * br * bi
    return rr, ri


def _seg_shift(v, k, reverse):
    sub = lax.broadcasted_iota(jnp.int32, v.shape, 0)
    if not reverse:
        return jnp.where(sub >= k, pltpu.roll(v, k, 0), 0.0)
    return jnp.where(sub < SUBLANES - k, pltpu.roll(v, SUBLANES - k, 0), 0.0)


def _steps(n, step, init, unroll):
    u = unroll if n % unroll == 0 else 1

    def trip(i, c):
        for s in range(u):
            c = step(i * u + s, c)
        return c

    return lax.fori_loop(0, n // u, trip, init)


def _ssm_scan(hre, him, ar, ai, seglen, reverse, tail=None, tail_init=()):
    w = hre.shape[1]
    zero = jnp.zeros((SUBLANES, w), F32)

    def rows(j):
        jj = (seglen - 1 - j) if reverse else j
        return pl.ds(pl.multiple_of(jj * SUBLANES, SUBLANES), SUBLANES)

    def local(j, c):
        hr, hi = c
        r = rows(j)
        nhr = ar * hr - ai * hi + hre[r, :]
        nhi = ar * hi + ai * hr + him[r, :]
        hre[r, :] = nhr
        him[r, :] = nhi
        return nhr, nhi

    er, ei = _steps(seglen, local, (zero, zero), 4 if reverse else 1)
    pr, pi_ = _cpow(ar, ai, seglen)
    for k in (1, 2, 4):
        sr, si = _seg_shift(er, k, reverse), _seg_shift(ei, k, reverse)
        er, ei = er + pr * sr - pi_ * si, ei + pr * si + pi_ * sr
        pr, pi_ = pr * pr - pi_ * pi_, 2.0 * pr * pi_
    cr, ci = _seg_shift(er, 1, reverse), _seg_shift(ei, 1, reverse)

    def carry_in(j, c):
        tr, ti = c[0] * ar - c[1] * ai, c[0] * ai + c[1] * ar
        r = rows(j)
        fr = hre[r, :] + tr
        fi = him[r, :] + ti
        hre[r, :] = fr
        him[r, :] = fi
        if tail is None:
            return tr, ti
        return (tr, ti) + tuple(tail(j, fr, fi, c[2:]))

    out = _steps(seglen, carry_in, (cr, ci) + tuple(tail_init), 4)
    return out[2:]


SSM_CB = 128
SSM_SB = 256


def _ssm_specs(S, l):
    u_spec = pl.BlockSpec((S, SSM_CB), lambda g, h: (0, g))
    bb_spec = pl.BlockSpec((1, 1, SSM_CB, SSM_SB), lambda g, h: (l, g, 0, h))
    a_spec = pl.BlockSpec((1, 1, 1, SSM_SB), lambda g, h: (l, g, 0, h))
    c_spec = pl.BlockSpec((1, 1, SSM_SB, SSM_CB), lambda g, h: (l, g, h, 0))
    d_spec = pl.BlockSpec((1, 1, SSM_CB), lambda g, h: (l, 0, g))
    return u_spec, bb_spec, a_spec, c_spec, d_spec


def _ssm_fwd(u, mats, l):
    S = u.shape[0]
    seglen = S // SUBLANES
    ch = min(512, S)
    nch = S // ch

    def body(u_ref, bbre_ref, bbim_ref, are_ref, aim_ref, cre_ref, cim_ref, d_ref, y_ref, hre, him):
        hf = pl.program_id(1)
        wre = bbre_ref[0, 0].astype(BF)
        wim = bbim_ref[0, 0].astype(BF)

        def mk(c, _):
            r = pl.ds(pl.multiple_of(c * ch, ch), ch)
            ub = u_ref[r, :].astype(BF)
            hre[r, :] = _dot(ub, wre)
            him[r, :] = _dot(ub, wim)
            return 0

        lax.fori_loop(0, nch, mk, 0)
        ar = jnp.broadcast_to(are_ref[0, 0], (SUBLANES, SSM_SB))
        ai = jnp.broadcast_to(aim_ref[0, 0], (SUBLANES, SSM_SB))
        _ssm_scan(hre, him, ar, ai, seglen, False)
        cr = cre_ref[0, 0].astype(BF)
        ci = cim_ref[0, 0].astype(BF)

        def out(c, _):
            r = pl.ds(pl.multiple_of(c * ch, ch), ch)
            y = _dot(hre[r, :].astype(BF), cr) - _dot(him[r, :].astype(BF), ci)

            @pl.when(hf == 0)
            def _():
                y_ref[r, :] = y + d_ref[0] * u_ref[r, :].astype(F32)

            @pl.when(hf != 0)
            def _():
                y_ref[r, :] = y_ref[r, :] + y

            return 0

        lax.fori_loop(0, nch, out, 0)

    u_spec, bb_spec, a_spec, c_spec, d_spec = _ssm_specs(S, l)
    return pl.pallas_call(
        body, name="ssm_fwd", grid=(SSM_WIDTH // SSM_CB, 2),
        in_specs=[u_spec, bb_spec, bb_spec, a_spec, a_spec, c_spec, c_spec, d_spec], out_specs=u_spec,
        out_shape=jax.ShapeDtypeStruct((S, SSM_WIDTH), F32),
        scratch_shapes=[pltpu.VMEM((S, SSM_SB), F32), pltpu.VMEM((S, SSM_SB), F32)],
        compiler_params=_cparams(("parallel", "arbitrary")),
    )(u, *mats)


def _ssm_bwd(u, dy, mats, l, exs=()):
    S = u.shape[0]
    seglen = S // SUBLANES
    ch = min(512, S)
    nch = S // ch
    nblk = SSM_WIDTH // SSM_CB

    def body(u_ref, dy_ref, bbre_ref, bbim_ref, are_ref, aim_ref, cre_ref, cim_ref, d_ref,
             du_ref, dbbre_ref, dbbim_ref, dare_ref, daim_ref, dcre_ref, dcim_ref, dd_ref,
             hre, him, lre, lim):
        hf = pl.program_id(1)
        wre = bbre_ref[0, 0].astype(BF)
        wim = bbim_ref[0, 0].astype(BF)
        wre_t, wim_t = wre.T, wim.T
        cr_t = cre_ref[0, 0].astype(BF).T
        ci_t = cim_ref[0, 0].astype(BF).T

        def mk(c, _):
            r = pl.ds(pl.multiple_of(c * ch, ch), ch)
            ub = u_ref[r, :].astype(BF)
            hre[r, :] = _dot(ub, wre)
            him[r, :] = _dot(ub, wim)
            return 0

        lax.fori_loop(0, nch, mk, 0)
        ar = jnp.broadcast_to(are_ref[0, 0], (SUBLANES, SSM_SB))
        ai = jnp.broadcast_to(aim_ref[0, 0], (SUBLANES, SSM_SB))
        _ssm_scan(hre, him, ar, ai, seglen, False)

        dcre_ref[...] = jnp.zeros_like(dcre_ref)
        dcim_ref[...] = jnp.zeros_like(dcim_ref)

        @pl.when(hf == 0)
        def _():
            dd_ref[...] = jnp.zeros_like(dd_ref)

        def cot(c, _):
            r = pl.ds(pl.multiple_of(c * ch, ch), ch)
            dyv = dy_ref[r, :]
            dyb = dyv.astype(BF)
            lre[r, :] = _dot(dyb, cr_t)
            lim[r, :] = -_dot(dyb, ci_t)
            dcre_ref[0] = dcre_ref[0] + _dot_tn(dyb, hre[r, :].astype(BF))
            dcim_ref[0] = dcim_ref[0] - _dot_tn(dyb, him[r, :].astype(BF))

            @pl.when(hf == 0)
            def _():
                dd_ref[...] = dd_ref[...] + jnp.sum(dyv * u_ref[r, :].astype(F32), axis=0, keepdims=True)

            return 0

        lax.fori_loop(0, nch, cot, 0)

        last = pl.ds((seglen - 1) * SUBLANES, SUBLANES)
        pr0 = _seg_shift(hre[last, :], 1, False)
        pi0 = _seg_shift(him[last, :], 1, False)

        def da(j, lr, li, c):
            acr, aci = c
            jp = jnp.maximum(seglen - 2 - j, 0)
            rp = pl.ds(pl.multiple_of(jp * SUBLANES, SUBLANES), SUBLANES)
            inner = j < seglen - 1
            pr = jnp.where(inner, hre[rp, :], pr0)
            pi_ = jnp.where(inner, him[rp, :], pi0)
            return acr + lr * pr + li * pi_, aci + li * pr - lr * pi_

        zero = jnp.zeros((SUBLANES, SSM_SB), F32)
        acr, aci = _ssm_scan(lre, lim, ar, -ai, seglen, True, tail=da, tail_init=(zero, zero))
        dare_ref[0] = jnp.sum(acr, axis=0, keepdims=True)
        daim_ref[0] = jnp.sum(aci, axis=0, keepdims=True)

        dbbre_ref[...] = jnp.zeros_like(dbbre_ref)
        dbbim_ref[...] = jnp.zeros_like(dbbim_ref)

        def fin(c, _):
            r = pl.ds(pl.multiple_of(c * ch, ch), ch)
            lrb = lre[r, :].astype(BF)
            lib = lim[r, :].astype(BF)
            ub = u_ref[r, :].astype(BF)
            du = _dot(lrb, wre_t) + _dot(lib, wim_t)
            dbbre_ref[0] = dbbre_ref[0] + _dot_tn(ub, lrb)
            dbbim_ref[0] = dbbim_ref[0] + _dot_tn(ub, lib)

            @pl.when(hf == 0)
            def _():
                du_ref[r, :] = du + d_ref[0] * dy_ref[r, :]

            @pl.when(hf != 0)
            def _():
                du_ref[r, :] = du_ref[r, :] + du

            return 0

        lax.fori_loop(0, nch, fin, 0)

    u_spec, bb_spec, a_spec, c_spec, d_spec = _ssm_specs(S, l)
    dbb_spec = pl.BlockSpec((1, SSM_CB, SSM_SB), lambda g, h: (g, 0, h))
    da_spec = pl.BlockSpec((1, 1, SSM_SB), lambda g, h: (g, 0, h))
    dd_spec = pl.BlockSpec((1, SSM_CB), lambda g, h: (0, g))
    out_shape = (
        jax.ShapeDtypeStruct((S, SSM_WIDTH), F32),
        jax.ShapeDtypeStruct((nblk, SSM_CB, 2 * SSM_SB), F32), jax.ShapeDtypeStruct((nblk, SSM_CB, 2 * SSM_SB), F32),
        jax.ShapeDtypeStruct((nblk, 1, 2 * SSM_SB), F32), jax.ShapeDtypeStruct((nblk, 1, 2 * SSM_SB), F32),
        jax.ShapeDtypeStruct((nblk, SSM_CB, 2 * SSM_SB), F32), jax.ShapeDtypeStruct((nblk, SSM_CB, 2 * SSM_SB), F32),
        jax.ShapeDtypeStruct((1, SSM_WIDTH), F32),
    )
    return _carry_call(
        body, "ssm_bwd", (nblk, 2), [u_spec, u_spec, bb_spec, bb_spec, a_spec, a_spec, c_spec, c_spec, d_spec],
        (u_spec, dbb_spec, dbb_spec, da_spec, da_spec, dbb_spec, dbb_spec, dd_spec), out_shape,
        [pltpu.VMEM((S, SSM_SB), F32) for _ in range(4)], ("parallel", "arbitrary"), (u, dy) + tuple(mats), exs)


def _ssm_post_fwd(y_raw, proj, w_glu, b_glu, p_ssm):
    S = y_raw.shape[0]
    tr = min(T_ROWS, S)

    def body(y_ref, z_ref, wg_ref, bg_ref, p_ref, o_ref):
        g = _gelu(y_ref[...])
        t = _dot(g.astype(BF), wg_ref[...]) + bg_ref[0]
        glu = t[:, :SSM_WIDTH] * _sigmoid(t[:, SSM_WIDTH:])
        ys = glu * _silu(z_ref[...].astype(F32))
        o_ref[...] = _dot(ys.astype(BF), p_ref[...]).astype(o_ref.dtype)

    return pl.pallas_call(
        body, name="ssm_post_fwd", grid=(S // tr,),
        in_specs=[_rows(tr, 512), _rows(tr, 512, 1), w_glu.spec(), b_glu.spec(), p_ssm.spec()],
        out_specs=_rows(tr, 1024), out_shape=jax.ShapeDtypeStruct((S, D_MODEL), ACT),
        compiler_params=_cparams(("parallel",)),
    )(y_raw, proj, w_glu.arr, b_glu.arr, p_ssm.arr)


def _ssm_post_bwd(do, y_raw, proj, w_glu, b_glu, p_ssm):
    S = y_raw.shape[0]
    tr = min(T_ROWS_BWD, S)

    def body(do_ref, y_ref, z_ref, wg_ref, bg_ref, p_ref, dy_ref, dz_ref, dwg_ref, dbg_ref, dp_ref):
        @pl.when(pl.program_id(0) == 0)
        def _():
            dwg_ref[...] = jnp.zeros_like(dwg_ref)
            dbg_ref[...] = jnp.zeros_like(dbg_ref)
            dp_ref[...] = jnp.zeros_like(dp_ref)

        y = y_ref[...]
        z = z_ref[...].astype(F32)
        g = _gelu(y)
        gb = g.astype(BF)
        t = _dot(gb, wg_ref[...]) + bg_ref[0]
        a = t[:, :SSM_WIDTH]
        sb = _sigmoid(t[:, SSM_WIDTH:])
        glu = a * sb
        ys = glu * _silu(z)
        dob = do_ref[...].astype(BF)
        dys = _dot_nt(dob, p_ref[...])
        dp_ref[...] += _dot_tn(ys.astype(BF), dob)
        dglu = dys * _silu(z)
        dz_ref[...] = (dys * glu * _dsilu(z)).astype(dz_ref.dtype)
        dt = jnp.concatenate([dglu * sb, dglu * a * sb * (1.0 - sb)], axis=1)
        dbg_ref[...] += jnp.sum(dt, axis=0, keepdims=True)
        dtb = dt.astype(BF)
        dg = _dot_nt(dtb, wg_ref[...])
        dwg_ref[...] += _dot_tn(gb, dtb)
        dy_ref[...] = dg * _dgelu(y)

    return pl.pallas_call(
        body, name="ssm_post_bwd", grid=(S // tr,),
        in_specs=[_rows(tr, 1024), _rows(tr, 512), _rows(tr, 512, 1), w_glu.spec(), b_glu.spec(), p_ssm.spec()],
        out_specs=(_rows(tr, 512), _rows(tr, 512), _full((512, 1024)), _full((1, 1024)), _full((512, 1024))),
        out_shape=(jax.ShapeDtypeStruct((S, 512), F32), jax.ShapeDtypeStruct((S, 512), BF),
                   jax.ShapeDtypeStruct((512, 1024), F32), jax.ShapeDtypeStruct((1, 1024), F32),
                   jax.ShapeDtypeStruct((512, 1024), F32)),
        compiler_params=_cparams(("arbitrary",)),
    )(do, y_raw, proj, w_glu.arr, b_glu.arr, p_ssm.arr)


def _rope(t, c, sa, sb):
    return t * c + pltpu.roll(t, LANES - 16, 1) * sa + pltpu.roll(t, 16, 1) * sb


def _rope_t(dy, c, sa, sb):
    return dy * c + pltpu.roll(dy * sa, 16, 1) + pltpu.roll(dy * sb, LANES - 16, 1)


def _rms(x, g):
    r = lax.rsqrt(jnp.mean(x * x, axis=-1, keepdims=True) + NORM_EPS)
    return x * r * g, r


def _mla_pre_fwd(proj, q_norm, kv_norm, wuq, wk, wv, tc, tsa, tsb):
    S = proj.shape[0]
    tr = min(T_ROWS, S)

    def body(cq_ref, ckv_ref, slot_ref, qn_ref, kn_ref, wuq_ref, wk_ref, wv_ref, c_ref, sa_ref, sb_ref,
             q_out, k_out, v_out, qt_out, kt_out, vt_out):
        c, sa, sb = c_ref[...], sa_ref[...], sb_ref[...]
        qn, _ = _rms(cq_ref[...].astype(F32), qn_ref[0])
        q = _dot(qn.astype(BF), wuq_ref[...]) * MLA_SCALE
        kn, _ = _rms(ckv_ref[...].astype(F32), kn_ref[0])
        knb = kn.astype(BF)
        kp = _dot(knb, wk_ref[...])
        v = _dot(knb, wv_ref[...]).astype(BF)
        v_out[...] = v
        vt_out[...] = v.T
        kr = _rope(slot_ref[...].astype(F32), c, sa, sb)
        for h in range(MLA_HEADS):
            cs = slice(h * LANES, (h + 1) * LANES)
            qh = _rope(q[:, cs], c, sa, sb).astype(BF)
            kh = (kp[:, cs] + kr).astype(BF)
            q_out[:, cs] = qh
            k_out[:, cs] = kh
            qt_out[cs, :] = qh.T
            kt_out[cs, :] = kh.T

    return pl.pallas_call(
        body, name="mla_pre_fwd", grid=(S // tr,),
        in_specs=[_rows(tr, 256, 4), _rows(tr, 128, 10), _rows(tr, 128, 11), q_norm.spec(), kv_norm.spec(),
                  wuq.spec(), _full((128, 1024)), _full((128, 512)),
                  _rows(tr, 128), _rows(tr, 128), _rows(tr, 128)],
        out_specs=(_rows(tr, 1024), _rows(tr, 1024), _rows(tr, 512), _cols(1024, tr), _cols(1024, tr), _cols(512, tr)),
        out_shape=(jax.ShapeDtypeStruct((S, 1024), BF), jax.ShapeDtypeStruct((S, 1024), BF),
                   jax.ShapeDtypeStruct((S, 512), BF), jax.ShapeDtypeStruct((1024, S), BF),
                   jax.ShapeDtypeStruct((1024, S), BF), jax.ShapeDtypeStruct((512, S), BF)),
        compiler_params=_cparams(("parallel",)),
    )(proj, proj, proj, q_norm.arr, kv_norm.arr, wuq.arr, wk, wv, tc, tsa, tsb)


def _mla_pre_bwd(dq, dk, dv, proj, q_norm, kv_norm, wuq, wk, wv, tc, tsa, tsb):
    S = proj.shape[0]
    tr = min(T_ROWS_BWD, S)

    def body(dq_ref, dk_ref, dv_ref, cq_ref, ckv_ref, qn_ref, kn_ref, wuq_ref, wk_ref, wv_ref, c_ref, sa_ref, sb_ref,
             dcq_ref, dckv_ref, dslot_ref, dwuq_ref, dwk_ref, dwv_ref, dqn_ref, dkn_ref, dqp):
        @pl.when(pl.program_id(0) == 0)
        def _():
            dwuq_ref[...] = jnp.zeros_like(dwuq_ref)
            dwk_ref[...] = jnp.zeros_like(dwk_ref)
            dwv_ref[...] = jnp.zeros_like(dwv_ref)
            dqn_ref[...] = jnp.zeros_like(dqn_ref)
            dkn_ref[...] = jnp.zeros_like(dkn_ref)

        c, sa, sb = c_ref[...], sa_ref[...], sb_ref[...]
        dkr = jnp.zeros((tr, LANES), F32)
        for h in range(MLA_HEADS):
            cs = slice(h * LANES, (h + 1) * LANES)
            dqp[:, cs] = (_rope_t(dq_ref[:, cs], c, sa, sb) * MLA_SCALE).astype(BF)
            dkr = dkr + dk_ref[:, cs]
        lane = lax.broadcasted_iota(jnp.int32, (tr, LANES), 1)
        in_rope = (lane >= MLA_NOPE) & (lane < MLA_NOPE + MLA_ROPE)
        dslot_ref[...] = jnp.where(in_rope, _rope_t(dkr, c, sa, sb), 0.0).astype(dslot_ref.dtype)

        cq = cq_ref[...].astype(F32)
        gq = qn_ref[0]
        qn, rq = _rms(cq, gq)
        dqpb = dqp[...]
        dwuq_ref[...] += _dot_tn(qn.astype(BF), dqpb)
        dqn = _dot_nt(dqpb, wuq_ref[...])
        dqn_ref[...] += jnp.sum(dqn * cq * rq, axis=0, keepdims=True)
        dyg = dqn * gq
        dcq_ref[...] = (rq * dyg - cq * (rq * rq * rq) * jnp.mean(dyg * cq, axis=-1, keepdims=True)).astype(dcq_ref.dtype)

        ckv = ckv_ref[...].astype(F32)
        gk = kn_ref[0]
        kn, rk = _rms(ckv, gk)
        knb = kn.astype(BF)
        dkb = dk_ref[...].astype(BF)
        dvb = dv_ref[...].astype(BF)
        dwk_ref[...] += _dot_tn(knb, dkb)
        dwv_ref[...] += _dot_tn(knb, dvb)
        dkn = _dot_nt(dkb, wk_ref[...]) + _dot_nt(dvb, wv_ref[...])
        dkn_ref[...] += jnp.sum(dkn * ckv * rk, axis=0, keepdims=True)
        dyk = dkn * gk
        dckv_ref[...] = (rk * dyk - ckv * (rk * rk * rk) * jnp.mean(dyk * ckv, axis=-1, keepdims=True)).astype(dckv_ref.dtype)

    return pl.pallas_call(
        body, name="mla_pre_bwd", grid=(S // tr,),
        in_specs=[_rows(tr, 1024), _rows(tr, 1024), _rows(tr, 512), _rows(tr, 256, 4), _rows(tr, 128, 10),
                  q_norm.spec(), kv_norm.spec(), wuq.spec(), _full((128, 1024)), _full((128, 512)),
                  _rows(tr, 128), _rows(tr, 128), _rows(tr, 128)],
        out_specs=(_rows(tr, 256), _rows(tr, 128), _rows(tr, 128), _full((256, 1024)), _full((128, 1024)),
                   _full((128, 512)), _full((1, 256)), _full((1, 128))),
        out_shape=(jax.ShapeDtypeStruct((S, 256), BF), jax.ShapeDtypeStruct((S, 128), BF),
                   jax.ShapeDtypeStruct((S, 128), BF), jax.ShapeDtypeStruct((256, 1024), F32),
                   jax.ShapeDtypeStruct((128, 1024), F32), jax.ShapeDtypeStruct((128, 512), F32),
                   jax.ShapeDtypeStruct((1, 256), F32), jax.ShapeDtypeStruct((1, 128), F32)),
        scratch_shapes=[pltpu.VMEM((tr, 1024), BF)],
        compiler_params=_cparams(("arbitrary",)),
    )(dq, dk, dv, proj, proj, q_norm.arr, kv_norm.arr, wuq.arr, wk, wv, tc, tsa, tsb)


ANY = pl.BlockSpec(memory_space=pl.ANY)
N_REL = N_DEV - 1


def _coords():
    return lax.axis_index("x"), lax.axis_index("y"), lax.axis_index("c")


def _sem_shapes(nbuf):
    return [pltpu.SemaphoreType.DMA((N_REL * nbuf,)), pltpu.SemaphoreType.DMA((N_REL * nbuf,)),
            pltpu.SemaphoreType.DMA((nbuf,))]


def _ag_plan(srcs, dsts, sems):
    send_sems, recv_sems, _ = sems
    plan = []
    for b, (src, dst) in enumerate(zip(srcs, dsts)):
        def slot(px, py, pc, dst=dst):
            return dst.at[4 * px + 2 * py + pc]

        def copy(k, blk, to, s=None, b=b, slot=slot):
            return pltpu.make_async_remote_copy(
                src_ref=slot(*blk) if s is None else s, dst_ref=slot(*blk), send_sem=send_sems.at[N_REL * b + k],
                recv_sem=recv_sems.at[N_REL * b + k], device_id=to, device_id_type=MESH)

        plan.append((b, src, slot, copy))
    return plan


def _ag_start(srcs, dsts, sems):
    x, y, c = _coords()
    chips = [(1 - x, y), (x, 1 - y), (1 - x, 1 - y)]
    for b, src, slot, copy in _ag_plan(srcs, dsts, sems):
        pltpu.make_async_copy(src, slot(x, y, c), sems[2].at[b]).start()
        copy(0, (x, y, c), (x, y, 1 - c), src).start()
        for j, chip in enumerate(chips):
            copy(1 + j, (x, y, c), (*chip, c), src).start()


def _ag_finish(srcs, dsts, sems):
    x, y, c = _coords()
    me, sibling = (x, y, c), (x, y, 1 - c)
    chips = [(1 - x, y), (x, 1 - y), (1 - x, 1 - y)]
    plan = _ag_plan(srcs, dsts, sems)
    for b, src, slot, copy in plan:
        for j, chip in enumerate(chips):
            copy(1 + j, (*chip, c), me).wait_recv()
            copy(4 + j, (*chip, c), sibling).start()
    for b, src, slot, copy in plan:
        copy(0, sibling, me).wait_recv()
        for j, chip in enumerate(chips):
            copy(4 + j, (*chip, 1 - c), me).wait_recv()
        copy(0, me, sibling, src).wait_send()
        for j, chip in enumerate(chips):
            copy(1 + j, me, (*chip, c), src).wait_send()
            copy(4 + j, (*chip, c), sibling).wait_send()
        pltpu.make_async_copy(src, slot(*me), sems[2].at[b]).wait()


def _a2a_copies(srcs, dsts, sems):
    send_sems, recv_sems, local_sems = sems
    x, y, c = _coords()
    me = 4 * x + 2 * y + c
    local, remote = [], []
    for b, (src, dst) in enumerate(zip(srcs, dsts)):
        for rel in range(1, N_DEV):
            px = 1 - x if rel & 4 else x
            py = 1 - y if rel & 2 else y
            pc = 1 - c if rel & 1 else c
            remote.append(pltpu.make_async_remote_copy(
                src_ref=src.at[4 * px + 2 * py + pc], dst_ref=dst.at[me], send_sem=send_sems.at[N_REL * b + rel - 1],
                recv_sem=recv_sems.at[N_REL * b + rel - 1], device_id=(px, py, pc), device_id_type=MESH))
        local.append(pltpu.make_async_copy(src.at[me], dst.at[me], local_sems.at[b]))
    return local, remote


def _a2a_start(srcs, dsts, sems):
    local, remote = _a2a_copies(srcs, dsts, sems)
    for d in local + remote:
        d.start()


def _a2a_finish(srcs, dsts, sems):
    local, remote = _a2a_copies(srcs, dsts, sems)
    for d in remote + local:
        d.wait()


class _Exchange:
    def __init__(self, kind, srcs):
        self.kind, self.srcs = kind, list(srcs)
        self.n = len(self.srcs)

    def out_shapes(self):
        if self.kind == "ag":
            return [jax.ShapeDtypeStruct((N_DEV,) + s.shape, s.dtype) for s in self.srcs]
        return [jax.ShapeDtypeStruct(s.shape, s.dtype) for s in self.srcs]

    def start(self, src_refs, dst_refs, sems):
        (_ag_start if self.kind == "ag" else _a2a_start)(src_refs, dst_refs, sems)

    def finish(self, src_refs, dst_refs, sems):
        (_ag_finish if self.kind == "ag" else _a2a_finish)(src_refs, dst_refs, sems)


def _carry_call(body, name, grid, in_specs, out_specs, out_shape, scratch, semantics, args, exs):
    in_specs, out_specs, out_shape, scratch = list(in_specs), list(out_specs), list(out_shape), list(scratch)
    if not exs:
        return pl.pallas_call(body, name=name, grid=grid, in_specs=in_specs, out_specs=out_specs, out_shape=out_shape,
                              scratch_shapes=scratch, compiler_params=_cparams(semantics))(*args), []
    n_in, n_out, n_scr = len(in_specs), len(out_specs), len(scratch)
    n_ex = sum(e.n for e in exs)

    def wrapped(*refs):
        ins, refs = refs[:n_in], refs[n_in:]
        srcs, refs = refs[:n_ex], refs[n_ex:]
        outs, refs = refs[:n_out], refs[n_out:]
        dsts, refs = refs[:n_ex], refs[n_ex:]
        scr, sems = refs[:n_scr], refs[n_scr:]
        views, off = [], 0
        for i, e in enumerate(exs):
            views.append((srcs[off:off + e.n], dsts[off:off + e.n], sems[3 * i:3 * i + 3]))
            off += e.n
        first = last = None
        for axis, size in enumerate(grid):
            at0, at1 = pl.program_id(axis) == 0, pl.program_id(axis) == size - 1
            first = at0 if first is None else first & at0
            last = at1 if last is None else last & at1

        @pl.when(first)
        def _():
            for e, view in zip(exs, views):
                e.start(*view)

        body(*ins, *outs, *scr)

        @pl.when(last)
        def _():
            for e, view in zip(exs, views):
                e.finish(*view)

    res = pl.pallas_call(
        wrapped, name=name + "_x", grid=grid, in_specs=in_specs + [ANY] * n_ex, out_specs=out_specs + [ANY] * n_ex,
        out_shape=out_shape + [s for e in exs for s in e.out_shapes()],
        scratch_shapes=scratch + [s for e in exs for s in _sem_shapes(e.n)],
        compiler_params=_cparams(("arbitrary",) * len(grid)))(*args, *[s for e in exs for s in e.srcs])
    got, off = [], n_out
    for e in exs:
        got.append(list(res[off:off + e.n]))
        off += e.n
    return res[:n_out], got


def _exchange_call(name, exs):
    tot = sum(e.n for e in exs)

    def body(*refs):
        srcs, dsts, sems = refs[:tot], refs[tot:2 * tot], refs[2 * tot:]
        views, off = [], 0
        for i, e in enumerate(exs):
            views.append((srcs[off:off + e.n], dsts[off:off + e.n], sems[3 * i:3 * i + 3]))
            off += e.n
        for e, view in zip(exs, views):
            e.start(*view)
        for e, view in zip(exs, views):
            e.finish(*view)

    outs = pl.pallas_call(
        body, name=name, in_specs=[ANY] * tot, out_specs=[ANY] * tot,
        out_shape=[s for e in exs for s in e.out_shapes()],
        scratch_shapes=[s for e in exs for s in _sem_shapes(e.n)],
    )(*[s for e in exs for s in e.srcs])
    res, off = [], 0
    for e in exs:
        res.append(list(outs[off:off + e.n]))
        off += e.n
    return res


def _flash_call(body, name, exs, in_specs, out_specs, out_shape, scratch, n, args):
    return _carry_call(body, name, (MLA_HEADS // 2, n * (n + 1) // 2), in_specs, out_specs, out_shape, scratch,
                       ("parallel", "arbitrary"), args, exs)


def _tri_rows(s, n):
    at = [(s >= r * (r + 1) // 2).astype(jnp.int32) for r in range(1, n)]
    return sum(at), s - sum(a * r for a, r in zip(at, range(1, n)))


def _tri_cols(s, n):
    starts = [c * n - c * (c - 1) // 2 for c in range(n)]
    col = sum((s >= starts[c]).astype(jnp.int32) for c in range(1, n))
    start = sum(jnp.where(col == c, starts[c], 0) for c in range(n))
    return s - start + col, col


def _pair_rows(a):
    at = a.T
    return jnp.concatenate([at[0:1, :], at[MLA_V:MLA_V + 1, :], jnp.zeros((SUBLANES - 2, a.shape[0]), a.dtype)], axis=0)


def _lower_tri(t):
    return lax.broadcasted_iota(jnp.int32, (t, t), 0) >= lax.broadcasted_iota(jnp.int32, (t, t), 1)


def _upper_tri(t):
    return lax.broadcasted_iota(jnp.int32, (t, t), 1) >= lax.broadcasted_iota(jnp.int32, (t, t), 0)


def _flash_fwd(q, kt, v, exs=()):
    S = q.shape[0]
    t = min(T_ATT, S)
    n = S // t

    def body(q_ref, kt_ref, v_ref, o_ref, lse_ref, lse_t_ref, m_s, l_s, acc):
        qi, ki = _tri_rows(pl.program_id(1), n)
        lo = lax.broadcasted_iota(jnp.int32, (t, LANES), 1) < MLA_V

        @pl.when(ki == 0)
        def _():
            m_s[...] = jnp.full_like(m_s, NEG)
            l_s[...] = jnp.zeros_like(l_s)
            acc[...] = jnp.zeros_like(acc)

        keep = _lower_tri(t) | (ki < qi)
        vv = v_ref[...]
        heads = range(2)
        ss = [jnp.where(keep, _dot(q_ref[:, h * LANES:(h + 1) * LANES], kt_ref[h * LANES:(h + 1) * LANES, :]), NEG)
              for h in heads]
        m_prev = [m_s[h] for h in heads]
        l_prev = [l_s[h] for h in heads]
        m_new = [jnp.maximum(m_prev[h], jnp.max(ss[h], axis=1, keepdims=True)) for h in heads]
        al = [jnp.exp(m_prev[h] - m_new[h]) for h in heads]
        ps = [jnp.exp(ss[h] - m_new[h][:, :1]) for h in heads]
        l_new = [al[h] * l_prev[h] + jnp.sum(ps[h], axis=1, keepdims=True) for h in heads]
        pv = [_dot(ps[h].astype(BF), vv) for h in heads]
        for h in heads:
            m_s[h] = m_new[h]
            l_s[h] = l_new[h]
        acc[...] = jnp.where(lo, al[0], al[1]) * acc[...] + jnp.where(lo, pv[0], pv[1])

        @pl.when(ki == qi)
        def _():
            o_ref[...] = acc[...] / jnp.where(lo, l_s[0], l_s[1])
            lse = jnp.where(lo, m_s[0] + jnp.log(l_s[0]), m_s[1] + jnp.log(l_s[1]))
            lse_ref[0] = lse
            lse_t_ref[0] = _pair_rows(lse)

    return _flash_call(
        body, "mla_flash_fwd", exs,
        [pl.BlockSpec((t, 256), lambda p, s: (_tri_rows(s, n)[0], p)),
         pl.BlockSpec((256, t), lambda p, s: (p, _tri_rows(s, n)[1])),
         pl.BlockSpec((t, 128), lambda p, s: (_tri_rows(s, n)[1], p))],
        [pl.BlockSpec((t, 128), lambda p, s: (_tri_rows(s, n)[0], p)),
         pl.BlockSpec((1, t, 128), lambda p, s: (p, _tri_rows(s, n)[0], 0)),
         pl.BlockSpec((1, SUBLANES, t), lambda p, s: (p, 0, _tri_rows(s, n)[0]))],
        [jax.ShapeDtypeStruct((S, 512), F32), jax.ShapeDtypeStruct((MLA_HEADS // 2, S, 128), F32),
         jax.ShapeDtypeStruct((MLA_HEADS // 2, SUBLANES, S), F32)],
        [pltpu.VMEM((2, t, 128), F32), pltpu.VMEM((2, t, 128), F32), pltpu.VMEM((t, 128), F32)], n, (q, kt, v))


def _flash_bwd_dq(q, k, kt, vt, do, lse, delta, exs=()):
    S = q.shape[0]
    t = min(T_ATT, S)
    n = S // t

    def body(q_ref, k_ref, kt_ref, vt_ref, do_ref, lse_ref, dl_ref, dq_ref, acc):
        qi, ki = _tri_rows(pl.program_id(1), n)
        lo = lax.broadcasted_iota(jnp.int32, (t, LANES), 1) < MLA_V

        @pl.when(ki == 0)
        def _():
            acc[...] = jnp.zeros_like(acc)

        keep = _lower_tri(t) | (ki < qi)
        heads = range(2)
        cs = [slice(h * LANES, (h + 1) * LANES) for h in heads]
        col = [slice(h * MLA_V, h * MLA_V + 1) for h in heads]
        lse, dl, dov, vt = lse_ref[0], dl_ref[0], do_ref[...], vt_ref[...]
        ss = [jnp.where(keep, _dot(q_ref[:, cs[h]], kt_ref[cs[h], :]), NEG) for h in heads]
        dp = [_dot(jnp.where(lo if h == 0 else ~lo, dov, 0).astype(BF), vt) for h in heads]
        ds = [(jnp.exp(ss[h] - lse[:, col[h]]) * (dp[h] - dl[:, col[h]])).astype(BF) for h in heads]
        dq = [_dot(ds[h], k_ref[:, cs[h]]) for h in heads]
        acc[...] += jnp.concatenate(dq, axis=1)

        @pl.when(ki == qi)
        def _():
            dq_ref[...] = acc[...]

    (dq,), got = _flash_call(
        body, "mla_flash_dq", exs,
        [pl.BlockSpec((t, 256), lambda p, s: (_tri_rows(s, n)[0], p)),
         pl.BlockSpec((t, 256), lambda p, s: (_tri_rows(s, n)[1], p)),
         pl.BlockSpec((256, t), lambda p, s: (p, _tri_rows(s, n)[1])),
         pl.BlockSpec((128, t), lambda p, s: (p, _tri_rows(s, n)[1])),
         pl.BlockSpec((t, 128), lambda p, s: (_tri_rows(s, n)[0], p)),
         pl.BlockSpec((1, t, 128), lambda p, s: (p, _tri_rows(s, n)[0], 0)),
         pl.BlockSpec((1, t, 128), lambda p, s: (p, _tri_rows(s, n)[0], 0))],
        [pl.BlockSpec((t, 256), lambda p, s: (_tri_rows(s, n)[0], p))],
        [jax.ShapeDtypeStruct((S, 1024), F32)],
        [pltpu.VMEM((t, 256), F32)], n, (q, k, kt, vt, do, lse, delta))
    return dq, got


def _flash_bwd_dkv(q, qt, k, v, do, dot_, lse_t, delta_t, exs=()):
    S = q.shape[0]
    t = min(T_ATT, S)
    n = S // t

    def body(q_ref, qt_ref, k_ref, v_ref, do_ref, dot_ref, lse_ref, dl_ref, dk_ref, dv_ref, dk_acc, dv_acc):
        qi, ki = _tri_cols(pl.program_id(1), n)
        lo = lax.broadcasted_iota(jnp.int32, (t, LANES), 1) < MLA_V
        top = lax.broadcasted_iota(jnp.int32, (LANES, t), 0) < MLA_V

        @pl.when(qi == ki)
        def _():
            dk_acc[...] = jnp.zeros_like(dk_acc)
            dv_acc[...] = jnp.zeros_like(dv_acc)

        keep = _upper_tri(t) | (qi > ki)
        heads = range(2)
        cs = [slice(h * LANES, (h + 1) * LANES) for h in heads]
        vv, lse, dl, dov, dot_v = v_ref[...], lse_ref[0], dl_ref[0], do_ref[...], dot_ref[...]
        st = [jnp.where(keep, _dot(k_ref[:, cs[h]], qt_ref[cs[h], :]), NEG) for h in heads]
        dpt = [_dot(vv, jnp.where(top if h == 0 else ~top, dot_v, 0).astype(BF)) for h in heads]
        pt = [jnp.exp(st[h] - lse[h:h + 1, :]) for h in heads]
        dst = [(pt[h] * (dpt[h] - dl[h:h + 1, :])).astype(BF) for h in heads]
        dv = [_dot(pt[h].astype(BF), jnp.where(lo if h == 0 else ~lo, dov, 0).astype(BF)) for h in heads]
        dk = [_dot(dst[h], q_ref[:, cs[h]]) for h in heads]
        dv_acc[...] += dv[0] + dv[1]
        dk_acc[...] += jnp.concatenate(dk, axis=1)

        @pl.when(qi == n - 1)
        def _():
            dk_ref[...] = dk_acc[...]
            dv_ref[...] = dv_acc[...]

    (dk, dv), got = _flash_call(
        body, "mla_flash_dkv", exs,
        [pl.BlockSpec((t, 256), lambda p, s: (_tri_cols(s, n)[0], p)),
         pl.BlockSpec((256, t), lambda p, s: (p, _tri_cols(s, n)[0])),
         pl.BlockSpec((t, 256), lambda p, s: (_tri_cols(s, n)[1], p)),
         pl.BlockSpec((t, 128), lambda p, s: (_tri_cols(s, n)[1], p)),
         pl.BlockSpec((t, 128), lambda p, s: (_tri_cols(s, n)[0], p)),
         pl.BlockSpec((128, t), lambda p, s: (p, _tri_cols(s, n)[0])),
         pl.BlockSpec((1, SUBLANES, t), lambda p, s: (p, 0, _tri_cols(s, n)[0])),
         pl.BlockSpec((1, SUBLANES, t), lambda p, s: (p, 0, _tri_cols(s, n)[0]))],
        [pl.BlockSpec((t, 256), lambda p, s: (_tri_cols(s, n)[1], p)),
         pl.BlockSpec((t, 128), lambda p, s: (_tri_cols(s, n)[1], p))],
        [jax.ShapeDtypeStruct((S, 1024), F32), jax.ShapeDtypeStruct((S, 512), F32)],
        [pltpu.VMEM((t, 256), F32), pltpu.VMEM((t, 128), F32)], n, (q, qt, k, v, do, dot_, lse_t, delta_t))
    return dk, dv, got


def _mem_heads(qm, km_ref, vm_ref):
    ps, os_ = [], []
    for h in range(X_HEADS):
        cs = slice(h * X_HEAD_DIM, (h + 1) * X_HEAD_DIM)
        s = _dot_nt(qm[:, cs].astype(BF), km_ref[:, cs]) * MEM_SCALE
        e = jnp.exp(s - jnp.max(s, axis=1, keepdims=True))
        p = e / jnp.sum(e, axis=1, keepdims=True)
        ps.append(p)
        os_.append(_dot(p.astype(BF), vm_ref[:, cs]))
    return ps, jnp.concatenate(os_, axis=1)


def _mem_fwd(proj, km, vm, p_mem):
    S = proj.shape[0]
    tr = min(T_ROWS, S)
    M = km.shape[0]

    def body(q_ref, z_ref, km_ref, vm_ref, p_ref, o_ref):
        _, o = _mem_heads(q_ref[...], km_ref, vm_ref)
        y = o * _silu(z_ref[...].astype(F32))
        o_ref[...] = _dot(y.astype(BF), p_ref[...]).astype(o_ref.dtype)

    return pl.pallas_call(
        body, name="mem_fwd", grid=(S // tr,),
        in_specs=[_rows(tr, 512, 4), _rows(tr, 512, 5), _full((M, 512)), _full((M, 512)), p_mem.spec()],
        out_specs=_rows(tr, 1024), out_shape=jax.ShapeDtypeStruct((S, D_MODEL), ACT),
        compiler_params=_cparams(("parallel",)),
    )(proj, proj, km, vm, p_mem.arr)


def _mem_bwd(do, proj, km, vm, p_mem):
    S = proj.shape[0]
    tr = min(T_ROWS_BWD, S)
    M = km.shape[0]

    def body(do_ref, q_ref, z_ref, km_ref, vm_ref, p_ref, dq_ref, dz_ref, dkm_ref, dvm_ref, dp_ref):
        @pl.when(pl.program_id(0) == 0)
        def _():
            dkm_ref[...] = jnp.zeros_like(dkm_ref)
            dvm_ref[...] = jnp.zeros_like(dvm_ref)
            dp_ref[...] = jnp.zeros_like(dp_ref)

        qm = q_ref[...]
        z = z_ref[...].astype(F32)
        ps, o = _mem_heads(qm, km_ref, vm_ref)
        sz = _silu(z)
        y = o * sz
        dob = do_ref[...].astype(BF)
        dy = _dot_nt(dob, p_ref[...])
        dp_ref[...] += _dot_tn(y.astype(BF), dob)
        dz_ref[...] = (dy * o * _dsilu(z)).astype(dz_ref.dtype)
        d_o = dy * sz
        for h in range(X_HEADS):
            cs = slice(h * X_HEAD_DIM, (h + 1) * X_HEAD_DIM)
            doh = d_o[:, cs]
            dohb = doh.astype(BF)
            p = ps[h]
            dpr = _dot_nt(dohb, vm_ref[:, cs])
            ds = (p * (dpr - jnp.sum(doh * o[:, cs], axis=1, keepdims=True)) * MEM_SCALE).astype(BF)
            dq_ref[:, cs] = _dot(ds, km_ref[:, cs]).astype(dq_ref.dtype)
            dkm_ref[:, cs] += _dot_tn(ds, qm[:, cs].astype(BF))
            dvm_ref[:, cs] += _dot_tn(p.astype(BF), dohb)

    return pl.pallas_call(
        body, name="mem_bwd", grid=(S // tr,),
        in_specs=[_rows(tr, 1024), _rows(tr, 512, 4), _rows(tr, 512, 5), _full((M, 512)), _full((M, 512)),
                  p_mem.spec()],
        out_specs=(_rows(tr, 512), _rows(tr, 512), _full((M, 512)), _full((M, 512)), _full((512, 1024))),
        out_shape=(jax.ShapeDtypeStruct((S, 512), BF), jax.ShapeDtypeStruct((S, 512), BF),
                   jax.ShapeDtypeStruct((M, 512), F32), jax.ShapeDtypeStruct((M, 512), F32),
                   jax.ShapeDtypeStruct((512, 1024), F32)),
        compiler_params=_cparams(("arbitrary",)),
    )(do, proj, proj, km, vm, p_mem.arr)


def _merge_fwd(x, proj, o_ssm, o_att, o_mem, b_gate, p_mla, w_out, ln_g, ln_b):
    S = x.shape[0]
    tr = min(T_ROWS, S)

    def body(x_ref, lg_ref, z_ref, os_ref, oa_ref, om_ref, bg_ref, p_ref, w_ref, g_ref, b_ref,
             xn_ref, xb_ref, pre_ref, mg_ref):
        gates = _sigmoid(lg_ref[...].astype(F32) + bg_ref[0])
        ya = oa_ref[...] * _silu(z_ref[...].astype(F32))
        o_mla = _dot(ya.astype(BF), p_ref[...])
        merged = (gates[:, :D_MODEL] * os_ref[...].astype(F32) + gates[:, D_MODEL:2 * D_MODEL] * o_mla
                  + gates[:, 2 * D_MODEL:] * om_ref[...].astype(F32))
        mb = merged.astype(BF)
        mg_ref[...] = mb
        pre = ALPHA * x_ref[...] + _dot(mb, w_ref[...])
        pre_ref[...] = pre
        mu = jnp.mean(pre, axis=-1, keepdims=True)
        xc = pre - mu
        var = jnp.mean(xc * xc, axis=-1, keepdims=True)
        xn = xc * lax.rsqrt(var + NORM_EPS) * g_ref[0] + b_ref[0]
        xn_ref[...] = xn
        xb_ref[...] = xn.astype(BF)

    return pl.pallas_call(
        body, name="merge_fwd", grid=(S // tr,),
        in_specs=[_rows(tr, 1024), _rows(tr, 3072, 1), _rows(tr, 512, 3), _rows(tr, 1024), _rows(tr, 512),
                  _rows(tr, 1024), b_gate.spec(), p_mla.spec(), _full((1024, 1024)), ln_g.spec(), ln_b.spec()],
        out_specs=(_rows(tr, 1024), _rows(tr, 1024), _rows(tr, 1024), _rows(tr, 1024)),
        out_shape=(jax.ShapeDtypeStruct((S, 1024), F32), jax.ShapeDtypeStruct((S, 1024), BF),
                   jax.ShapeDtypeStruct((S, 1024), F32), jax.ShapeDtypeStruct((S, 1024), BF)),
        compiler_params=_cparams(("parallel",)),
    )(x, proj, proj, o_ssm, o_att, o_mem, b_gate.arr, p_mla.arr, w_out, ln_g.arr, ln_b.arr)


def _merge_bwd(dxn, pre, merged, proj, o_ssm, o_att, o_mem, b_gate, p_mla, w_out, ln_g):
    S = pre.shape[0]
    tr = min(T_ROWS_BWD, S)

    def body(dxn_ref, pre_ref, mg_ref, lg_ref, z_ref, os_ref, oa_ref, om_ref, bg_ref, p_ref, w_ref, g_ref,
             dxr_ref, dlg_ref, dos_ref, dom_ref, doa_ref, dz_ref, doat_ref, dl_ref, dlt_ref, dw_ref, dp_ref, dbg_ref,
             dg_ref, db_ref):
        @pl.when(pl.program_id(0) == 0)
        def _():
            dw_ref[...] = jnp.zeros_like(dw_ref)
            dp_ref[...] = jnp.zeros_like(dp_ref)
            dbg_ref[...] = jnp.zeros_like(dbg_ref)
            dg_ref[...] = jnp.zeros_like(dg_ref)
            db_ref[...] = jnp.zeros_like(db_ref)

        dxn = dxn_ref[...]
        pre = pre_ref[...]
        mu = jnp.mean(pre, axis=-1, keepdims=True)
        xc = pre - mu
        rstd = lax.rsqrt(jnp.mean(xc * xc, axis=-1, keepdims=True) + NORM_EPS)
        xhat = xc * rstd
        dg_ref[...] += jnp.sum(dxn * xhat, axis=0, keepdims=True)
        db_ref[...] += jnp.sum(dxn, axis=0, keepdims=True)
        dxh = dxn * g_ref[0]
        dpre = rstd * (dxh - jnp.mean(dxh, axis=-1, keepdims=True)
                       - xhat * jnp.mean(dxh * xhat, axis=-1, keepdims=True))
        dxr_ref[...] = ALPHA * dpre
        dpb = dpre.astype(BF)
        dw_ref[...] += _dot_tn(mg_ref[...], dpb)
        dm = _dot_nt(dpb, w_ref[...])

        gates = _sigmoid(lg_ref[...].astype(F32) + bg_ref[0])
        g0, g1, g2 = gates[:, :D_MODEL], gates[:, D_MODEL:2 * D_MODEL], gates[:, 2 * D_MODEL:]
        z = z_ref[...].astype(F32)
        oa = oa_ref[...]
        sz = _silu(z)
        ya = (oa * sz).astype(BF)
        o_mla = _dot(ya, p_ref[...])
        dos_ref[...] = (g0 * dm).astype(dos_ref.dtype)
        dom_ref[...] = (g2 * dm).astype(dom_ref.dtype)
        do_mla = (g1 * dm).astype(BF)
        dl0 = dm * os_ref[...].astype(F32) * g0 * (1.0 - g0)
        dl1 = dm * o_mla * g1 * (1.0 - g1)
        dl2 = dm * om_ref[...].astype(F32) * g2 * (1.0 - g2)
        dl = jnp.concatenate([dl0, dl1, dl2], axis=1)
        dbg_ref[...] += jnp.sum(dl, axis=0, keepdims=True)
        dlg_ref[...] = dl.astype(dlg_ref.dtype)
        dp_ref[...] += _dot_tn(ya, do_mla)
        dya = _dot_nt(do_mla, p_ref[...])
        doa = dya * sz
        doab = doa.astype(BF)
        doa_ref[...] = doab
        doat_ref[...] = doab.T
        dz_ref[...] = (dya * oa * _dsilu(z)).astype(dz_ref.dtype)
        prod = doa * oa
        lo = lax.broadcasted_iota(jnp.int32, (tr, LANES), 1) < MLA_V
        for pr in range(MLA_HEADS // 2):
            blk = prod[:, pr * LANES:(pr + 1) * LANES]
            d0 = jnp.sum(jnp.where(lo, blk, 0.0), axis=1, keepdims=True)
            d1 = jnp.sum(jnp.where(lo, 0.0, blk), axis=1, keepdims=True)
            dl = jnp.where(lo, d0, d1)
            dl_ref[pr] = dl
            dlt_ref[pr] = _pair_rows(dl)

    return pl.pallas_call(
        body, name="merge_bwd", grid=(S // tr,),
        in_specs=[_rows(tr, 1024), _rows(tr, 1024), _rows(tr, 1024), _rows(tr, 3072, 1), _rows(tr, 512, 3),
                  _rows(tr, 1024), _rows(tr, 512), _rows(tr, 1024), b_gate.spec(), p_mla.spec(),
                  _full((1024, 1024)), ln_g.spec()],
        out_specs=(_rows(tr, 1024), _rows(tr, 3072), _rows(tr, 1024), _rows(tr, 1024), _rows(tr, 512),
                   _rows(tr, 512), _cols(512, tr), pl.BlockSpec((MLA_HEADS // 2, tr, 128), lambda i: (0, i, 0)),
                   pl.BlockSpec((MLA_HEADS // 2, SUBLANES, tr), lambda i: (0, 0, i)),
                   _full((1024, 1024)), _full((512, 1024)), _full((1, 3072)), _full((1, 1024)), _full((1, 1024))),
        out_shape=(jax.ShapeDtypeStruct((S, 1024), F32), jax.ShapeDtypeStruct((S, 3072), BF),
                   jax.ShapeDtypeStruct((S, 1024), BF), jax.ShapeDtypeStruct((S, 1024), BF),
                   jax.ShapeDtypeStruct((S, 512), BF), jax.ShapeDtypeStruct((S, 512), BF),
                   jax.ShapeDtypeStruct((512, S), BF), jax.ShapeDtypeStruct((MLA_HEADS // 2, S, 128), F32),
                   jax.ShapeDtypeStruct((MLA_HEADS // 2, SUBLANES, S), F32),
                   jax.ShapeDtypeStruct((1024, 1024), F32), jax.ShapeDtypeStruct((512, 1024), F32),
                   jax.ShapeDtypeStruct((1, 3072), F32), jax.ShapeDtypeStruct((1, 1024), F32),
                   jax.ShapeDtypeStruct((1, 1024), F32)),
        compiler_params=_cparams(("arbitrary",)),
    )(dxn, pre, merged, proj, proj, o_ssm, o_att, o_mem, b_gate.arr, p_mla.arr, w_out, ln_g.arr)


def _loss_head(y, t):
    S = y.shape[0]
    tr = min(T_ROWS, S)
    n = S // tr

    def body(y_ref, t_ref, dy_ref, l_ref, acc):
        i = pl.program_id(0)

        @pl.when(i == 0)
        def _():
            acc[...] = jnp.zeros_like(acc)

        e = y_ref[...] - t_ref[...]
        dy_ref[...] = e * (1.0 / D_MODEL)
        acc[...] += jnp.sum(e * e, axis=0, keepdims=True)

        @pl.when(i == n - 1)
        def _():
            tot = jnp.sum(acc[...], axis=1, keepdims=True) * (0.5 / D_MODEL)
            l_ref[...] = jnp.broadcast_to(tot, l_ref.shape)

    return pl.pallas_call(
        body, name="loss_head", grid=(n,),
        in_specs=[_rows(tr, 1024), _rows(tr, 1024)],
        out_specs=(_rows(tr, 1024), _full((SUBLANES, LANES))),
        out_shape=(jax.ShapeDtypeStruct((S, 1024), F32), jax.ShapeDtypeStruct((SUBLANES, LANES), F32)),
        scratch_shapes=[pltpu.VMEM((1, 1024), F32)],
        compiler_params=_cparams(("arbitrary",)),
    )(y, t)


def _rope_tables(pos):
    inv_freq = ROPE_THETA ** (-jnp.arange(0, MLA_ROPE, 2, dtype=F32) / MLA_ROPE)
    ang = pos.astype(F32)[:, None] * inv_freq
    cos, sin = jnp.cos(ang), jnp.sin(ang)
    S = pos.shape[0]
    half = MLA_ROPE // 2
    ones = jnp.ones((S, MLA_NOPE), F32)
    z16 = jnp.zeros((S, half), F32)
    z32 = jnp.zeros((S, LANES - MLA_NOPE - MLA_ROPE), F32)
    z64 = jnp.zeros((S, MLA_NOPE), F32)
    c = jnp.concatenate([ones, cos, cos, z32], axis=1)
    sa = jnp.concatenate([z64, -sin, z16, z32], axis=1)
    sb = jnp.concatenate([z64, z16, sin, z32], axis=1)
    return c, sa, sb


def _ssm_discretise(a_re, a_im, log_dt, b_re, b_im):
    dt = jnp.exp(log_dt)[..., None]
    mag = jnp.exp(a_re * dt)
    lb_re = mag * jnp.cos(a_im * dt)
    lb_im = mag * jnp.sin(a_im * dt)
    nr, ni = lb_re - 1.0, lb_im
    den = a_re * a_re + a_im * a_im
    f_re = (nr * a_re + ni * a_im) / den
    f_im = (ni * a_re - nr * a_im) / den
    bb_re = f_re[..., None] * b_re - f_im[..., None] * b_im
    bb_im = f_re[..., None] * b_im + f_im[..., None] * b_re
    return lb_re, lb_im, bb_re, bb_im


_GPB = SSM_CB // SSM_GROUP


def _bd_in(bb):
    nb = SSM_GROUPS // _GPB
    t = bb.reshape(nb, _GPB, SSM_STATE, SSM_GROUP)
    eye = jnp.eye(_GPB, dtype=bb.dtype)
    return jnp.einsum("ngpc,gh->ngchp", t, eye).reshape(nb, SSM_CB, _GPB * SSM_STATE)


def _bd_in_t(d):
    nb = SSM_GROUPS // _GPB
    t = d.reshape(nb, _GPB, SSM_GROUP, _GPB, SSM_STATE)
    eye = jnp.eye(_GPB, dtype=d.dtype)
    return jnp.einsum("ngchp,gh->ngpc", t, eye).reshape(SSM_GROUPS, SSM_STATE, SSM_GROUP)


def _bd_out(c):
    nb = SSM_GROUPS // _GPB
    t = c.reshape(nb, _GPB, SSM_GROUP, SSM_STATE)
    eye = jnp.eye(_GPB, dtype=c.dtype)
    return jnp.einsum("ngcp,gh->ngphc", t, eye).reshape(nb, _GPB * SSM_STATE, SSM_CB)


def _interleave(a):
    S, w = a.shape
    return a.reshape(SUBLANES, S // SUBLANES, w).transpose(1, 0, 2).reshape(S, w)


def _deinterleave(a):
    S, w = a.shape
    return a.reshape(S // SUBLANES, SUBLANES, w).transpose(1, 0, 2).reshape(S, w)


IN_SHARD = D_IN // N_DEV
ROPE_OWNER = ROPE_SLOT_LO // IN_SHARD
assert ROPE_OWNER * IN_SHARD <= ROPE_SLOT_LO and ROPE_SLOT_LO + MLA_ROPE <= (ROPE_OWNER + 1) * IN_SHARD


def _w_in_from_shards(g):
    pieces = []
    for j in range(N_DEV):
        if j == ROPE_OWNER:
            a = ROPE_SLOT_LO - j * IN_SHARD
            z = lambda n: jnp.zeros((g.shape[1], n), g.dtype)
            pieces += [g[j][:, :a], z(MLA_NOPE), g[j][:, a:a + MLA_ROPE], z(LANES - MLA_NOPE - MLA_ROPE),
                       g[j][:, a + MLA_ROPE:]]
        else:
            pieces.append(g[j])
    return jnp.concatenate(pieces, axis=1)


def _w_in_to_shards(d):
    shift = LANES - MLA_ROPE
    out = []
    for j in range(N_DEV):
        lo, hi = j * IN_SHARD, (j + 1) * IN_SHARD
        if j < ROPE_OWNER:
            out.append(d[:, lo:hi])
        elif j > ROPE_OWNER:
            out.append(d[:, lo + shift:hi + shift])
        else:
            r = ROPE_SLOT_LO + MLA_NOPE
            out.append(jnp.concatenate([d[:, lo:ROPE_SLOT_LO], d[:, r:r + MLA_ROPE],
                                        d[:, ROPE_SLOT_LO + LANES:hi + shift]], axis=1))
    return jnp.stack(out)


def _adamw_math(w, g, m, v):
    m = ADAM_B1 * m + (1.0 - ADAM_B1) * g
    v = ADAM_B2 * v + (1.0 - ADAM_B2) * (g * g)
    m_hat = m / (1.0 - ADAM_B1 ** ADAM_STEP)
    v_hat = v / (1.0 - ADAM_B2 ** ADAM_STEP)
    delta = -ADAM_LR * (m_hat / (jnp.sqrt(v_hat) + ADAM_EPS) + ADAM_WD * w)
    return delta, m, v


def _adamw_sharded(parts, w, m, v, tile, name):
    L, _, R, C = parts.shape
    assert R % tile == 0

    def body(p_ref, w_ref, m_ref, v_ref, g_out, d_out, m_out, v_out):
        g = p_ref[0, 0].astype(F32)
        for k in range(1, N_DEV):
            g = g + p_ref[0, k].astype(F32)
        d, mn, vn = _adamw_math(w_ref[0], g, m_ref[0], v_ref[0])
        g_out[0] = g
        d_out[0] = d
        m_out[0] = mn
        v_out[0] = vn

    spec = pl.BlockSpec((1, tile, C), lambda l, i: (l, i, 0))
    shp = jax.ShapeDtypeStruct((L, R, C), F32)
    return pl.pallas_call(
        body, name=name, grid=(L, R // tile),
        in_specs=[pl.BlockSpec((1, N_DEV, tile, C), lambda l, i: (l, 0, i, 0)), spec, spec, spec],
        out_specs=(spec,) * 4, out_shape=(shp,) * 4, compiler_params=_cparams(("parallel", "parallel")),
    )(parts, w, m, v)


COL_GROUP = (("w_glu", 512), ("p_ssm", 512), ("p_mla", 512), ("p_mem", 512), ("w_uq", 256), ("w_ukv", 128))
COL_AT = {n: sum(r for _, r in COL_GROUP[:i]) // rows for i, (n, rows) in enumerate(COL_GROUP)}
assert all(sum(r for _, r in COL_GROUP[:i]) % rows == 0 for i, (_, rows) in enumerate(COL_GROUP))
COL_ROWS = dict(COL_GROUP)
ROW_GROUP = ("w_mem_kv", "w_out")
SMALL = ("b_gate", "ssm_a_re", "ssm_a_im", "ssm_log_dt", "ssm_b_re", "ssm_b_im", "ssm_c_re", "ssm_c_im", "ssm_d",
         "b_glu", "mla_q_norm", "mla_kv_norm", "ln_g", "ln_b")
SMALL_TILE = 512
UQ_COLS = MLA_NOPE + MLA_ROPE


def _pad_lanes(a):
    return jnp.concatenate([a, jnp.zeros(a.shape[:-1] + (LANES - a.shape[-1],), a.dtype)], axis=-1)


def _group_buffers(d, dtype):
    col = jnp.concatenate([_pad_lanes(d[n]) if n == "w_uq" else d[n] for n, _ in COL_GROUP], axis=1)
    row = jnp.concatenate([d[n] for n in ROW_GROUP], axis=1)
    return d["w_in"].astype(dtype), col.astype(dtype), row.astype(dtype)


def _ungroup(bufs):
    b_in, col, row = bufs
    out, off = {"w_in": b_in}, 0
    for n, rows in COL_GROUP:
        t = col[:, off:off + rows]
        out[n] = t[..., :UQ_COLS] if n == "w_uq" else t
        off += rows
    k = row.shape[1] // 2
    out["w_mem_kv"], out["w_out"] = row[:, :k], row[:, k:]
    return out


def _colcat(t):
    return t.transpose(1, 0, 2).reshape(t.shape[1], -1)


def _colsplit(g, n):
    return g.reshape(g.shape[0], N_DEV, n).transpose(1, 0, 2)


def _unpack_weights(g_in, g_col, g_row):
    wc = _colcat(g_col)
    at = lambda n: _RowBlock(wc, COL_ROWS[n], COL_AT[n])
    lo = COL_AT["w_ukv"] * COL_ROWS["w_ukv"]
    ukv = wc[lo:lo + COL_ROWS["w_ukv"]].reshape(-1, MLA_HEADS, LANES)
    lane = lax.broadcasted_iota(jnp.int32, ukv.shape, 2)
    k = g_row.shape[1] // 2
    return dict(
        w_in=_w_in_from_shards(g_in), w_glu=at("w_glu"), w_uq=at("w_uq"), p_ssm=at("p_ssm"), p_mla=at("p_mla"),
        p_mem=at("p_mem"), w_k=jnp.where(lane < MLA_NOPE, ukv, jnp.zeros_like(ukv)).reshape(ukv.shape[0], -1),
        w_v=ukv[..., MLA_NOPE:].reshape(ukv.shape[0], -1),
        w_mem_kv=g_row[:, :k].reshape(-1, g_row.shape[2]), w_out=g_row[:, k:].reshape(-1, g_row.shape[2]))


def _pack_grads_in(d_w_in):
    return _w_in_to_shards(d_w_in).astype(BF)


def _pack_grads_rest(d):
    ukv = jnp.concatenate([d["w_k"].reshape(-1, MLA_HEADS, LANES)[..., :MLA_NOPE],
                           d["w_v"].reshape(-1, MLA_HEADS, MLA_V)], axis=-1).reshape(d["w_k"].shape[0], -1)
    col = jnp.concatenate([ukv if n == "w_ukv" else d[n] for n, _ in COL_GROUP], axis=0)
    row = jnp.concatenate([d[n].reshape(N_DEV, -1, d[n].shape[1]) for n in ROW_GROUP], axis=1)
    return [_colsplit(col, LANES).astype(BF), row.astype(BF)]


def _pack_small(d):
    parts = []
    for n in SMALL:
        f = d[n].reshape(-1)
        pad = (-f.shape[0]) % (SUBLANES * LANES)
        if pad:
            f = jnp.concatenate([f, jnp.zeros((pad,), f.dtype)])
        parts.append(f.reshape(-1, LANES))
    rows = sum(p.shape[0] for p in parts)
    pad = (-rows) % SMALL_TILE
    if pad:
        parts.append(jnp.zeros((pad, LANES), parts[0].dtype))
    return jnp.concatenate(parts, axis=0)


def _unpack_small(buf, like):
    out, off = {}, 0
    for n in SMALL:
        size = math.prod(like[n].shape)
        rows = -(-size // (SUBLANES * LANES)) * SUBLANES
        out[n] = buf[off:off + rows].reshape(-1)[:size].reshape(like[n].shape)
        off += rows
    return out


WEIGHTS = ("w_in", "b_gate", "ssm_a_re", "ssm_a_im", "ssm_log_dt", "ssm_b_re", "ssm_b_im", "ssm_c_re", "ssm_c_im",
           "ssm_d", "w_glu", "b_glu", "mla_q_norm", "w_uq", "mla_kv_norm", "w_ukv", "w_mem_kv", "p_ssm", "p_mla",
           "p_mem", "w_out", "ln_g", "ln_b")
BIG = ("w_in",) + tuple(n for n, _ in COL_GROUP) + ROW_GROUP


def _train_step(x, mem, pos, target, wl, ws):
    S = x.shape[0]
    tc, tsa, tsb = _rope_tables(pos)
    loc = _group_buffers(wl, BF)
    loc = [[b[l] for b in loc] for l in range(DEPTH)]

    lb_re, lb_im, bb_re, bb_im = _ssm_discretise(ws["ssm_a_re"], ws["ssm_a_im"], ws["ssm_log_dt"], ws["ssm_b_re"],
                                                 ws["ssm_b_im"])
    nb = SSM_GROUPS // _GPB
    mats = (jax.vmap(_bd_in)(bb_re), jax.vmap(_bd_in)(bb_im), lb_re.reshape(DEPTH, nb, 1, -1),
            lb_im.reshape(DEPTH, nb, 1, -1), jax.vmap(_bd_out)(ws["ssm_c_re"]), jax.vmap(_bd_out)(ws["ssm_c_im"]),
            ws["ssm_d"].reshape(DEPTH, 1, -1))

    rows3 = {n: ws[n].reshape(DEPTH, 1, -1) for n in ("b_glu", "mla_q_norm", "mla_kv_norm", "b_gate", "ln_g", "ln_b")}

    def small(n, l):
        return _LayerRow(rows3[n], l)

    (gathered,) = _exchange_call("weights_gather_first", [_Exchange("ag", loc[0])])
    W = [None] * DEPTH
    W[0] = _unpack_weights(*gathered)
    saved = []
    xs, xb = x, x.astype(BF)
    for l in range(DEPTH):
        w = W[l]
        proj = _mm(xb, w["w_in"], name="proj_fwd", tm=S, tn=512, out_dtype=ACT)
        u_il = _interleave(proj[:, :SSM_WIDTH])
        y_raw = _deinterleave(_ssm_fwd(u_il, mats, l))
        o_ssm = _ssm_post_fwd(y_raw, proj, w["w_glu"], small("b_glu", l), w["p_ssm"])
        q, k, v, qt, kt, vt = _mla_pre_fwd(proj, small("mla_q_norm", l), small("mla_kv_norm", l), w["w_uq"], w["w_k"], w["w_v"],
                               tc, tsa, tsb)
        nxt = [_Exchange("ag", loc[l + 1])] if l + 1 < DEPTH else []
        (o_att, lse, lse_t), gathered = _flash_fwd(q, kt, v, nxt)
        if nxt:
            W[l + 1] = _unpack_weights(*gathered[0])
        kvm = _mm(mem, w["w_mem_kv"], name="memkv_fwd", out_dtype=BF)
        km, vm = kvm[:, :512], kvm[:, 512:]
        o_mem = _mem_fwd(proj, km, vm, w["p_mem"])
        xn, xnb, pre, merged = _merge_fwd(xs, proj, o_ssm, o_att, o_mem, small("b_gate", l), w["p_mla"], w["w_out"],
                                          small("ln_g", l), small("ln_b", l))
        saved.append(dict(xb=xb, proj=proj, u_il=u_il, y_raw=y_raw, o_ssm=o_ssm, q=q, k=k, v=v, qt=qt, kt=kt, vt=vt, o_att=o_att,
                          lse=lse, lse_t=lse_t,
                          km=km, vm=vm, o_mem=o_mem, pre=pre, merged=merged))
        xs, xb = xn, xnb

    dxs, lvec = _loss_head(xs, target)
    loss = lvec[0, 0]

    gs = {n: [None] * DEPTH for n in SMALL}
    disc_ct = [None] * DEPTH
    got = [None] * DEPTH
    pending = None
    for l in reversed(range(DEPTH)):
        sv, w = saved[l], W[l]
        proj = sv["proj"]
        (dx_res, dlg, do_ssm, do_mem, do_att, dz_mla, do_att_t, delta, delta_t, d_w_out, d_p_mla, d_b_gate, d_ln_g,
         d_ln_b) = _merge_bwd(
            dxs, sv["pre"], sv["merged"], proj, sv["o_ssm"], sv["o_att"], sv["o_mem"], small("b_gate", l), w["p_mla"],
            w["w_out"], small("ln_g", l))
        dq_mem, dz_mem, d_km, d_vm, d_p_mem = _mem_bwd(do_mem, proj, sv["km"], sv["vm"], w["p_mem"])
        d_w_mem = _mm(mem, jnp.concatenate([d_km, d_vm], axis=1), name="memkv_bwd", ta=True)
        dq, arrived_rest = _flash_bwd_dq(sv["q"], sv["k"], sv["kt"], sv["vt"], do_att, sv["lse"], delta,
                                         [_Exchange("a2a", pending[1:])] if pending is not None else [])
        dk, dv, arrived_in = _flash_bwd_dkv(sv["q"], sv["qt"], sv["k"], sv["v"], do_att, do_att_t, sv["lse_t"], delta_t,
                                            [_Exchange("a2a", pending[:1])] if pending is not None else [])
        if pending is not None:
            got[l + 1] = arrived_in[0] + arrived_rest[0]
        dcq, dckv, dslot, d_wuq, d_wk, d_wv, d_qn, d_kn = _mla_pre_bwd(
            dq, dk, dv, proj, small("mla_q_norm", l), small("mla_kv_norm", l), w["w_uq"], w["w_k"], w["w_v"],
            tc, tsa, tsb)
        dy_raw, dz_ssm, d_w_glu, d_b_glu, d_p_ssm = _ssm_post_bwd(do_ssm, sv["y_raw"], proj, w["w_glu"],
                                                                 small("b_glu", l), w["p_ssm"])
        rest = _pack_grads_rest(dict(w_glu=d_w_glu, w_uq=d_wuq, w_k=d_wk, w_v=d_wv, w_mem_kv=d_w_mem, p_ssm=d_p_ssm,
                                     p_mla=d_p_mla, p_mem=d_p_mem, w_out=d_w_out))
        (du_il, dbbre, dbbim, dare, daim, dcre, dcim, dd), early = _ssm_bwd(
            sv["u_il"], _interleave(dy_raw), mats, l, [_Exchange("a2a", rest)] if l == 0 else [])
        du = _deinterleave(du_il).astype(BF)
        disc_ct[l] = (dare.reshape(SSM_GROUPS, SSM_STATE), daim.reshape(SSM_GROUPS, SSM_STATE), _bd_in_t(dbbre),
                      _bd_in_t(dbbim))
        dproj = jnp.concatenate([du, dz_ssm, dcq, dckv, dslot, dz_mla, dq_mem, dz_mem, dlg], axis=1)
        d_w_in = _mm(sv["xb"], dproj, name="proj_dw", ta=True, tm=1024, tn=512, tk=S)
        if l > 0:
            dxs = _mm(dproj, w["w_in"], name="proj_dx", tb=True, add=dx_res, tm=1024, tn=1024, tk=1024)
        pending = [_pack_grads_in(d_w_in)] + (rest if l > 0 else [])
        gs["b_gate"][l] = d_b_gate.reshape(-1)
        gs["ssm_c_re"][l] = _bd_in_t(dcre).transpose(0, 2, 1)
        gs["ssm_c_im"][l] = _bd_in_t(dcim).transpose(0, 2, 1)
        gs["ssm_d"][l] = dd.reshape(-1)
        gs["b_glu"][l] = d_b_glu.reshape(-1)
        gs["mla_q_norm"][l] = d_qn.reshape(-1)
        gs["mla_kv_norm"][l] = d_kn.reshape(-1)
        gs["ln_g"][l] = d_ln_g.reshape(-1)
        gs["ln_b"][l] = d_ln_b.reshape(-1)

    disc_in = tuple(ws[n] for n in ("ssm_a_re", "ssm_a_im", "ssm_log_dt", "ssm_b_re", "ssm_b_im"))
    _, disc_vjp = jax.vjp(_ssm_discretise, *disc_in)
    d_disc = disc_vjp(tuple(jnp.stack([disc_ct[l][i] for l in range(DEPTH)]) for i in range(4)))
    gsm = {n: jnp.stack(v) for n, v in gs.items() if v[0] is not None}
    for n, g in zip(("ssm_a_re", "ssm_a_im", "ssm_log_dt", "ssm_b_re", "ssm_b_im"), d_disc):
        gsm[n] = g
    dxs, (last_in, (small_parts,)) = _mm(
        dproj, w["w_in"], name="proj_dx", tb=True, add=dx_res, tm=1024, tn=1024, tk=1024,
        exs=[_Exchange("a2a", pending), _Exchange("ag", [_pack_small(gsm)])])
    got[0] = last_in + early[0]
    return loss, dxs, got, small_parts


def kernel(x, mem, positions, w_in, b_gate, ssm_a_re, ssm_a_im, ssm_log_dt, ssm_b_re, ssm_b_im, ssm_c_re, ssm_c_im, ssm_d, w_glu, b_glu, mla_q_norm, w_uq, mla_kv_norm, w_ukv, w_mem_kv, p_ssm, p_mla, p_mem, w_out, ln_g, ln_b, loss_target, m_w_in, m_b_gate, m_ssm_a_re, m_ssm_a_im, m_ssm_log_dt, m_ssm_b_re, m_ssm_b_im, m_ssm_c_re, m_ssm_c_im, m_ssm_d, m_w_glu, m_b_glu, m_mla_q_norm, m_w_uq, m_mla_kv_norm, m_w_ukv, m_w_mem_kv, m_p_ssm, m_p_mla, m_p_mem, m_w_out, m_ln_g, m_ln_b, v_w_in, v_b_gate, v_ssm_a_re, v_ssm_a_im, v_ssm_log_dt, v_ssm_b_re, v_ssm_b_im, v_ssm_c_re, v_ssm_c_im, v_ssm_d, v_w_glu, v_b_glu, v_mla_q_norm, v_w_uq, v_mla_kv_norm, v_w_ukv, v_w_mem_kv, v_p_ssm, v_p_mla, v_p_mem, v_w_out, v_ln_g, v_ln_b):
    w = dict(w_in=w_in, b_gate=b_gate, ssm_a_re=ssm_a_re, ssm_a_im=ssm_a_im, ssm_log_dt=ssm_log_dt, ssm_b_re=ssm_b_re,
             ssm_b_im=ssm_b_im, ssm_c_re=ssm_c_re, ssm_c_im=ssm_c_im, ssm_d=ssm_d, w_glu=w_glu, b_glu=b_glu,
             mla_q_norm=mla_q_norm, w_uq=w_uq, mla_kv_norm=mla_kv_norm, w_ukv=w_ukv, w_mem_kv=w_mem_kv, p_ssm=p_ssm,
             p_mla=p_mla, p_mem=p_mem, w_out=w_out, ln_g=ln_g, ln_b=ln_b)
    m = dict(w_in=m_w_in, b_gate=m_b_gate, ssm_a_re=m_ssm_a_re, ssm_a_im=m_ssm_a_im, ssm_log_dt=m_ssm_log_dt,
             ssm_b_re=m_ssm_b_re, ssm_b_im=m_ssm_b_im, ssm_c_re=m_ssm_c_re, ssm_c_im=m_ssm_c_im, ssm_d=m_ssm_d,
             w_glu=m_w_glu, b_glu=m_b_glu, mla_q_norm=m_mla_q_norm, w_uq=m_w_uq, mla_kv_norm=m_mla_kv_norm,
             w_ukv=m_w_ukv, w_mem_kv=m_w_mem_kv, p_ssm=m_p_ssm, p_mla=m_p_mla, p_mem=m_p_mem, w_out=m_w_out,
             ln_g=m_ln_g, ln_b=m_ln_b)
    v = dict(w_in=v_w_in, b_gate=v_b_gate, ssm_a_re=v_ssm_a_re, ssm_a_im=v_ssm_a_im, ssm_log_dt=v_ssm_log_dt,
             ssm_b_re=v_ssm_b_re, ssm_b_im=v_ssm_b_im, ssm_c_re=v_ssm_c_re, ssm_c_im=v_ssm_c_im, ssm_d=v_ssm_d,
             w_glu=v_w_glu, b_glu=v_b_glu, mla_q_norm=v_mla_q_norm, w_uq=v_w_uq, mla_kv_norm=v_mla_kv_norm,
             w_ukv=v_w_ukv, w_mem_kv=v_w_mem_kv, p_ssm=v_p_ssm, p_mla=v_p_mla, p_mem=v_p_mem, w_out=v_w_out,
             ln_g=v_ln_g, ln_b=v_ln_b)

    wl = {n: w[n] for n in BIG}
    small = {n: w[n] for n in SMALL}
    loss_local, dx, got, small_parts = _train_step(x[0], mem[0], positions[0], loss_target[0], wl, small)
    loss = lax.psum(loss_local, ("x", "y", "c"))

    grads, delta, new_m, new_v = {}, {}, {}, {}
    wg = _group_buffers(wl, F32)
    mg = _group_buffers({n: m[n] for n in BIG}, F32)
    vg = _group_buffers({n: v[n] for n in BIG}, F32)
    res = []
    for i, (tile, tag) in enumerate(((256, "in"), (128, "col"), (256, "row"))):
        parts = jnp.stack([got[l][i] for l in range(DEPTH)])
        res.append(_adamw_sharded(parts, wg[i], mg[i], vg[i], tile, "adamw_" + tag))
    for dst, j in ((grads, 0), (delta, 1), (new_m, 2), (new_v, 3)):
        dst.update(_ungroup([r[j] for r in res]))

    sw, sm, sv = (_pack_small(small), _pack_small({n: m[n] for n in SMALL}), _pack_small({n: v[n] for n in SMALL}))
    rs = _adamw_sharded(small_parts[None], sw[None], sm[None], sv[None], SMALL_TILE, "adamw_replicated")
    for dst, buf in zip((grads, delta, new_m, new_v), rs):
        dst.update(_unpack_small(buf[0], small))

    return (loss, dx[None], *[grads[n] for n in WEIGHTS], *[delta[n] for n in WEIGHTS],
            *[new_m[n] for n in WEIGHTS], *[new_v[n] for n in WEIGHTS])
```

```python
import math

import jax
import jax.numpy as jnp
from jax import lax
from jax.experimental import pallas as pl
from jax.experimental.pallas import tpu as pltpu

F32 = jnp.float32
BF = jnp.bfloat16
ACT = jnp.bfloat16

D_MODEL = 1024
DEPTH = 4
N_DEV = 8
SSM_WIDTH = 512
SSM_GROUP = 16
SSM_GROUPS = 32
SSM_STATE = 64
MLA_HEADS = 8
MLA_NOPE = 64
MLA_ROPE = 32
MLA_V = 64
MLA_Q_RANK = 256
MLA_KV_RANK = 128
ROPE_THETA = 10000.0
X_HEADS = 4
X_HEAD_DIM = 128
D_IN = 6048
ALPHA = (2 * DEPTH) ** 0.25
NORM_EPS = 1e-5
ADAM_LR = 0.001
ADAM_B1 = 0.9
ADAM_B2 = 0.999
ADAM_EPS = 1e-08
ADAM_WD = 0.01
ADAM_STEP = 10

LANES = 128
SUBLANES = 8
VMEM_LIMIT = 56 * 1024 * 1024

PW = 6144
ROPE_SLOT_LO = 1408
MLA_SCALE = (MLA_NOPE + MLA_ROPE) ** -0.5
MEM_SCALE = X_HEAD_DIM ** -0.5
NEG = -1e30

T_ROWS = 512
T_ROWS_BWD = 256
T_ATT = 1024
T_MM = 512

MESH = pl.DeviceIdType.MESH


def _cparams(sem):
    return pltpu.CompilerParams(dimension_semantics=sem, vmem_limit_bytes=VMEM_LIMIT)


def _dot(a, b):
    return lax.dot_general(a, b, (((1,), (0,)), ((), ())), preferred_element_type=F32)


def _dot_nt(a, b):
    return lax.dot_general(a, b, (((1,), (1,)), ((), ())), preferred_element_type=F32)


def _dot_tn(a, b):
    return lax.dot_general(a, b, (((0,), (0,)), ((), ())), preferred_element_type=F32)


def _sigmoid(x):
    return 0.5 * jnp.tanh(0.5 * x) + 0.5


def _silu(x):
    return x * _sigmoid(x)


def _dsilu(x):
    s = _sigmoid(x)
    return s * (1.0 + x * (1.0 - s))


_GELU_C = math.sqrt(2.0 / math.pi)


def _gelu(x):
    return 0.5 * x * (1.0 + jnp.tanh(_GELU_C * (x + 0.044715 * x * x * x)))


def _dgelu(x):
    t = jnp.tanh(_GELU_C * (x + 0.044715 * x * x * x))
    return 0.5 * (1.0 + t) + 0.5 * x * (1.0 - t * t) * _GELU_C * (1.0 + 3 * 0.044715 * x * x)


def _rows(tr, w, col=0):
    return pl.BlockSpec((tr, w), lambda i: (i, col))


def _cols(h, tc):
    return pl.BlockSpec((h, tc), lambda i: (0, i))


def _full(shape):
    n = len(shape)
    return pl.BlockSpec(shape, lambda i: (0,) * n)


class _RowBlock:
    def __init__(self, arr, rows, blk):
        self.arr, self.rows, self.blk = arr, rows, blk

    def spec(self):
        blk = self.blk
        return pl.BlockSpec((self.rows, self.arr.shape[1]), lambda i: (blk, 0))


class _LayerRow:
    def __init__(self, arr, l):
        self.arr, self.l = arr, l

    def spec(self):
        l = self.l
        return pl.BlockSpec((1, 1, self.arr.shape[2]), lambda i: (l, 0, 0))


def _mm(a, b, *, name, ta=False, tb=False, out_dtype=F32, add=None, tm=T_MM, tn=T_MM, tk=1024, exs=None):
    M, K = (a.shape[1], a.shape[0]) if ta else a.shape
    N = b.shape[0] if tb else b.shape[1]
    tm, tn, tk = min(tm, M), min(tn, N), min(tk, K)
    assert M % tm == 0 and N % tn == 0 and K % tk == 0, (M, N, K)
    nk = K // tk
    dn = (((0 if ta else 1,), (1 if tb else 0,)), ((), ()))

    def body(*refs):
        if add is not None:
            a_ref, b_ref, c_ref, o_ref = refs[:4]
        else:
            a_ref, b_ref, o_ref = refs[:3]
        part = lax.dot_general(a_ref[...].astype(BF), b_ref[...].astype(BF), dn, preferred_element_type=F32)
        if nk == 1:
            if add is not None:
                part = part + c_ref[...]
            o_ref[...] = part.astype(out_dtype)
            return
        acc = refs[-1]
        k = pl.program_id(2)

        @pl.when(k == 0)
        def _():
            acc[...] = part

        @pl.when(k != 0)
        def _():
            acc[...] += part

        @pl.when(k == nk - 1)
        def _():
            r = acc[...]
            if add is not None:
                r = r + c_ref[...]
            o_ref[...] = r.astype(out_dtype)

    a_spec = pl.BlockSpec((tk, tm), lambda i, j, k: (k, i)) if ta else pl.BlockSpec((tm, tk), lambda i, j, k: (i, k))
    b_spec = pl.BlockSpec((tn, tk), lambda i, j, k: (j, k)) if tb else pl.BlockSpec((tk, tn), lambda i, j, k: (k, j))
    o_spec = pl.BlockSpec((tm, tn), lambda i, j, k: (i, j))
    in_specs = [a_spec, b_spec] + ([o_spec] if add is not None else [])
    args = (a, b) + ((add,) if add is not None else ())
    (out,), got = _carry_call(
        body, name, (M // tm, N // tn, nk), in_specs, [o_spec], [jax.ShapeDtypeStruct((M, N), out_dtype)],
        [pltpu.VMEM((tm, tn), F32)] if nk > 1 else [], ("parallel", "parallel", "arbitrary"), args, exs)
    return out if exs is None else (out, got)


def _cpow(ar, ai, n):
    rr, ri = None, None
    br, bi = ar, ai
    while n:
        if n & 1:
            if rr is None:
                rr, ri = br, bi
            else:
                rr, ri = rr * br - ri * bi, rr * bi + ri * br
        n >>= 1
        if n:
            br, bi = br * br - bi * bi, 2.0 * br * bi
    return rr, ri


def _seg_shift(v, k, reverse):
    sub = lax.broadcasted_iota(jnp.int32, v.shape, 0)
    if not reverse:
        return jnp.where(sub >= k, pltpu.roll(v, k, 0), 0.0)
    return jnp.where(sub < SUBLANES - k, pltpu.roll(v, SUBLANES - k, 0), 0.0)


def _steps(n, step, init, unroll):
    u = unroll if n % unroll == 0 else 1

    def trip(i, c):
        for s in range(u):
            c = step(i * u + s, c)
        return c

    return lax.fori_loop(0, n // u, trip, init)


def _ssm_scan(hre, him, ar, ai, seglen, reverse, tail=None, tail_init=()):
    w = hre.shape[1]
    zero = jnp.zeros((SUBLANES, w), F32)

    def rows(j):
        jj = (seglen - 1 - j) if reverse else j
        return pl.ds(pl.multiple_of(jj * SUBLANES, SUBLANES), SUBLANES)

    def local(j, c):
        hr, hi = c
        r = rows(j)
        nhr = ar * hr - ai * hi + hre[r, :]
        nhi = ar * hi + ai * hr + him[r, :]
        hre[r, :] = nhr
        him[r, :] = nhi
        return nhr, nhi

    er, ei = _steps(seglen, local, (zero, zero), 4 if reverse else 1)
    pr, pi_ = _cpow(ar, ai, seglen)
    for k in (1, 2, 4):
        sr, si = _seg_shift(er, k, reverse), _seg_shift(ei, k, reverse)
        er, ei = er + pr * sr - pi_ * si, ei + pr * si + pi_ * sr
        pr, pi_ = pr * pr - pi_ * pi_, 2.0 * pr * pi_
    cr, ci = _seg_shift(er, 1, reverse), _seg_shift(ei, 1, reverse)

    def carry_in(j, c):
        tr, ti = c[0] * ar - c[1] * ai, c[0] * ai + c[1] * ar
        r = rows(j)
        fr = hre[r, :] + tr
        fi = him[r, :] + ti
        hre[r, :] = fr
        him[r, :] = fi
        if tail is None:
            return tr, ti
        return (tr, ti) + tuple(tail(j, fr, fi, c[2:]))

    out = _steps(seglen, carry_in, (cr, ci) + tuple(tail_init), 4)
    return out[2:]


SSM_CB = 128
SSM_SB = 256


def _ssm_specs(S, l):
    u_spec = pl.BlockSpec((S, SSM_CB), lambda g, h: (0, g))
    bb_spec = pl.BlockSpec((1, 1, SSM_CB, SSM_SB), lambda g, h: (l, g, 0, h))
    a_spec = pl.BlockSpec((1, 1, 1, SSM_SB), lambda g, h: (l, g, 0, h))
    c_spec = pl.BlockSpec((1, 1, SSM_SB, SSM_CB), lambda g, h: (l, g, h, 0))
    d_spec = pl.BlockSpec((1, 1, SSM_CB), lambda g, h: (l, 0, g))
    return u_spec, bb_spec, a_spec, c_spec, d_spec


def _ssm_fwd(u, mats, l):
    S = u.shape[0]
    seglen = S // SUBLANES
    ch = min(512, S)
    nch = S // ch

    def body(u_ref, bbre_ref, bbim_ref, are_ref, aim_ref, cre_ref, cim_ref, d_ref, y_ref, hre, him):
        hf = pl.program_id(1)
        wre = bbre_ref[0, 0].astype(BF)
        wim = bbim_ref[0, 0].astype(BF)

        def mk(c, _):
            r = pl.ds(pl.multiple_of(c * ch, ch), ch)
            ub = u_ref[r, :].astype(BF)
            hre[r, :] = _dot(ub, wre)
            him[r, :] = _dot(ub, wim)
            return 0

        lax.fori_loop(0, nch, mk, 0)
        ar = jnp.broadcast_to(are_ref[0, 0], (SUBLANES, SSM_SB))
        ai = jnp.broadcast_to(aim_ref[0, 0], (SUBLANES, SSM_SB))
        _ssm_scan(hre, him, ar, ai, seglen, False)
        cr = cre_ref[0, 0].astype(BF)
        ci = cim_ref[0, 0].astype(BF)

        def out(c, _):
            r = pl.ds(pl.multiple_of(c * ch, ch), ch)
            y = _dot(hre[r, :].astype(BF), cr) - _dot(him[r, :].astype(BF), ci)

            @pl.when(hf == 0)
            def _():
                y_ref[r, :] = y + d_ref[0] * u_ref[r, :].astype(F32)

            @pl.when(hf != 0)
            def _():
                y_ref[r, :] = y_ref[r, :] + y

            return 0

        lax.fori_loop(0, nch, out, 0)

    u_spec, bb_spec, a_spec, c_spec, d_spec = _ssm_specs(S, l)
    return pl.pallas_call(
        body, name="ssm_fwd", grid=(SSM_WIDTH // SSM_CB, 2),
        in_specs=[u_spec, bb_spec, bb_spec, a_spec, a_spec, c_spec, c_spec, d_spec], out_specs=u_spec,
        out_shape=jax.ShapeDtypeStruct((S, SSM_WIDTH), F32),
        scratch_shapes=[pltpu.VMEM((S, SSM_SB), F32), pltpu.VMEM((S, SSM_SB), F32)],
        compiler_params=_cparams(("parallel", "arbitrary")),
    )(u, *mats)


def _ssm_bwd(u, dy, mats, l, exs=()):
    S = u.shape[0]
    seglen = S // SUBLANES
    ch = min(512, S)
    nch = S // ch
    nblk = SSM_WIDTH // SSM_CB

    def body(u_ref, dy_ref, bbre_ref, bbim_ref, are_ref, aim_ref, cre_ref, cim_ref, d_ref,
             du_ref, dbbre_ref, dbbim_ref, dare_ref, daim_ref, dcre_ref, dcim_ref, dd_ref,
             hre, him, lre, lim):
        hf = pl.program_id(1)
        wre = bbre_ref[0, 0].astype(BF)
        wim = bbim_ref[0, 0].astype(BF)
        wre_t, wim_t = wre.T, wim.T
        cr_t = cre_ref[0, 0].astype(BF).T
        ci_t = cim_ref[0, 0].astype(BF).T

        def mk(c, _):
            r = pl.ds(pl.multiple_of(c * ch, ch), ch)
            ub = u_ref[r, :].astype(BF)
            hre[r, :] = _dot(ub, wre)
            him[r, :] = _dot(ub, wim)
            return 0

        lax.fori_loop(0, nch, mk, 0)
        ar = jnp.broadcast_to(are_ref[0, 0], (SUBLANES, SSM_SB))
        ai = jnp.broadcast_to(aim_ref[0, 0], (SUBLANES, SSM_SB))
        _ssm_scan(hre, him, ar, ai, seglen, False)

        dcre_ref[...] = jnp.zeros_like(dcre_ref)
        dcim_ref[...] = jnp.zeros_like(dcim_ref)

        @pl.when(hf == 0)
        def _():
            dd_ref[...] = jnp.zeros_like(dd_ref)

        def cot(c, _):
            r = pl.ds(pl.multiple_of(c * ch, ch), ch)
            dyv = dy_ref[r, :]
            dyb = dyv.astype(BF)
            lre[r, :] = _dot(dyb, cr_t)
            lim[r, :] = -_dot(dyb, ci_t)
            dcre_ref[0] = dcre_ref[0] + _dot_tn(dyb, hre[r, :].astype(BF))
            dcim_ref[0] = dcim_ref[0] - _dot_tn(dyb, him[r, :].astype(BF))

            @pl.when(hf == 0)
            def _():
                dd_ref[...] = dd_ref[...] + jnp.sum(dyv * u_ref[r, :].astype(F32), axis=0, keepdims=True)

            return 0

        lax.fori_loop(0, nch, cot, 0)

        last = pl.ds((seglen - 1) * SUBLANES, SUBLANES)
        pr0 = _seg_shift(hre[last, :], 1, False)
        pi0 = _seg_shift(him[last, :], 1, False)

        def da(j, lr, li, c):
            acr, aci = c
            jp = jnp.maximum(seglen - 2 - j, 0)
            rp = pl.ds(pl.multiple_of(jp * SUBLANES, SUBLANES), SUBLANES)
            inner = j < seglen - 1
            pr = jnp.where(inner, hre[rp, :], pr0)
            pi_ = jnp.where(inner, him[rp, :], pi0)
            return acr + lr * pr + li * pi_, aci + li * pr - lr * pi_

        zero = jnp.zeros((SUBLANES, SSM_SB), F32)
        acr, aci = _ssm_scan(lre, lim, ar, -ai, seglen, True, tail=da, tail_init=(zero, zero))
        dare_ref[0] = jnp.sum(acr, axis=0, keepdims=True)
        daim_ref[0] = jnp.sum(aci, axis=0, keepdims=True)

        dbbre_ref[...] = jnp.zeros_like(dbbre_ref)
        dbbim_ref[...] = jnp.zeros_like(dbbim_ref)

        def fin(c, _):
            r = pl.ds(pl.multiple_of(c * ch, ch), ch)
            lrb = lre[r, :].astype(BF)
            lib = lim[r, :].astype(BF)
            ub = u_ref[r, :].astype(BF)
            du = _dot(lrb, wre_t) + _dot(lib, wim_t)
            dbbre_ref[0] = dbbre_ref[0] + _dot_tn(ub, lrb)
            dbbim_ref[0] = dbbim_ref[0] + _dot_tn(ub, lib)

            @pl.when(hf == 0)
            def _():
                du_ref[r, :] = du + d_ref[0] * dy_ref[r, :]

            @pl.when(hf != 0)
            def _():
                du_ref[r, :] = du_ref[r, :] + du

            return 0

        lax.fori_loop(0, nch, fin, 0)

    u_spec, bb_spec, a_spec, c_spec, d_spec = _ssm_specs(S, l)
    dbb_spec = pl.BlockSpec((1, SSM_CB, SSM_SB), lambda g, h: (g, 0, h))
    da_spec = pl.BlockSpec((1, 1, SSM_SB), lambda g, h: (g, 0, h))
    dd_spec = pl.BlockSpec((1, SSM_CB), lambda g, h: (0, g))
    out_shape = (
        jax.ShapeDtypeStruct((S, SSM_WIDTH), F32),
        jax.ShapeDtypeStruct((nblk, SSM_CB, 2 * SSM_SB), F32), jax.ShapeDtypeStruct((nblk, SSM_CB, 2 * SSM_SB), F32),
        jax.ShapeDtypeStruct((nblk, 1, 2 * SSM_SB), F32), jax.ShapeDtypeStruct((nblk, 1, 2 * SSM_SB), F32),
        jax.ShapeDtypeStruct((nblk, SSM_CB, 2 * SSM_SB), F32), jax.ShapeDtypeStruct((nblk, SSM_CB, 2 * SSM_SB), F32),
        jax.ShapeDtypeStruct((1, SSM_WIDTH), F32),
    )
    return _carry_call(
        body, "ssm_bwd", (nblk, 2), [u_spec, u_spec, bb_spec, bb_spec, a_spec, a_spec, c_spec, c_spec, d_spec],
        (u_spec, dbb_spec, dbb_spec, da_spec, da_spec, dbb_spec, dbb_spec, dd_spec), out_shape,
        [pltpu.VMEM((S, SSM_SB), F32) for _ in range(4)], ("parallel", "arbitrary"), (u, dy) + tuple(mats), exs)


def _ssm_post_fwd(y_raw, proj, w_glu, b_glu, p_ssm):
    S = y_raw.shape[0]
    tr = min(T_ROWS, S)

    def body(y_ref, z_ref, wg_ref, bg_ref, p_ref, o_ref):
        g = _gelu(y_ref[...])
        t = _dot(g.astype(BF), wg_ref[...]) + bg_ref[0]
        glu = t[:, :SSM_WIDTH] * _sigmoid(t[:, SSM_WIDTH:])
        ys = glu * _silu(z_ref[...].astype(F32))
        o_ref[...] = _dot(ys.astype(BF), p_ref[...]).astype(o_ref.dtype)

    return pl.pallas_call(
        body, name="ssm_post_fwd", grid=(S // tr,),
        in_specs=[_rows(tr, 512), _rows(tr, 512, 1), w_glu.spec(), b_glu.spec(), p_ssm.spec()],
        out_specs=_rows(tr, 1024), out_shape=jax.ShapeDtypeStruct((S, D_MODEL), ACT),
        compiler_params=_cparams(("parallel",)),
    )(y_raw, proj, w_glu.arr, b_glu.arr, p_ssm.arr)


def _ssm_post_bwd(do, y_raw, proj, w_glu, b_glu, p_ssm):
    S = y_raw.shape[0]
    tr = min(T_ROWS_BWD, S)

    def body(do_ref, y_ref, z_ref, wg_ref, bg_ref, p_ref, dy_ref, dz_ref, dwg_ref, dbg_ref, dp_ref):
        @pl.when(pl.program_id(0) == 0)
        def _():
            dwg_ref[...] = jnp.zeros_like(dwg_ref)
            dbg_ref[...] = jnp.zeros_like(dbg_ref)
            dp_ref[...] = jnp.zeros_like(dp_ref)

        y = y_ref[...]
        z = z_ref[...].astype(F32)
        g = _gelu(y)
        gb = g.astype(BF)
        t = _dot(gb, wg_ref[...]) + bg_ref[0]
        a = t[:, :SSM_WIDTH]
        sb = _sigmoid(t[:, SSM_WIDTH:])
        glu = a * sb
        ys = glu * _silu(z)
        dob = do_ref[...].astype(BF)
        dys = _dot_nt(dob, p_ref[...])
        dp_ref[...] += _dot_tn(ys.astype(BF), dob)
        dglu = dys * _silu(z)
        dz_ref[...] = (dys * glu * _dsilu(z)).astype(dz_ref.dtype)
        dt = jnp.concatenate([dglu * sb, dglu * a * sb * (1.0 - sb)], axis=1)
        dbg_ref[...] += jnp.sum(dt, axis=0, keepdims=True)
        dtb = dt.astype(BF)
        dg = _dot_nt(dtb, wg_ref[...])
        dwg_ref[...] += _dot_tn(gb, dtb)
        dy_ref[...] = dg * _dgelu(y)

    return pl.pallas_call(
        body, name="ssm_post_bwd", grid=(S // tr,),
        in_specs=[_rows(tr, 1024), _rows(tr, 512), _rows(tr, 512, 1), w_glu.spec(), b_glu.spec(), p_ssm.spec()],
        out_specs=(_rows(tr, 512), _rows(tr, 512), _full((512, 1024)), _full((1, 1024)), _full((512, 1024))),
        out_shape=(jax.ShapeDtypeStruct((S, 512), F32), jax.ShapeDtypeStruct((S, 512), BF),
                   jax.ShapeDtypeStruct((512, 1024), F32), jax.ShapeDtypeStruct((1, 1024), F32),
                   jax.ShapeDtypeStruct((512, 1024), F32)),
        compiler_params=_cparams(("arbitrary",)),
    )(do, y_raw, proj, w_glu.arr, b_glu.arr, p_ssm.arr)


def _rope(t, c, sa, sb):
    return t * c + pltpu.roll(t, LANES - 16, 1) * sa + pltpu.roll(t, 16, 1) * sb


def _rope_t(dy, c, sa, sb):
    return dy * c + pltpu.roll(dy * sa, 16, 1) + pltpu.roll(dy * sb, LANES - 16, 1)


def _rms(x, g):
    r = lax.rsqrt(jnp.mean(x * x, axis=-1, keepdims=True) + NORM_EPS)
    return x * r * g, r


def _mla_pre_fwd(proj, q_norm, kv_norm, wuq, wk, wv, tc, tsa, tsb):
    S = proj.shape[0]
    tr = min(T_ROWS, S)

    def body(cq_ref, ckv_ref, slot_ref, qn_ref, kn_ref, wuq_ref, wk_ref, wv_ref, c_ref, sa_ref, sb_ref,
             q_out, k_out, v_out, qt_out, kt_out, vt_out):
        c, sa, sb = c_ref[...], sa_ref[...], sb_ref[...]
        qn, _ = _rms(cq_ref[...].astype(F32), qn_ref[0])
        q = _dot(qn.astype(BF), wuq_ref[...]) * MLA_SCALE
        kn, _ = _rms(ckv_ref[...].astype(F32), kn_ref[0])
        knb = kn.astype(BF)
        kp = _dot(knb, wk_ref[...])
        v = _dot(knb, wv_ref[...]).astype(BF)
        v_out[...] = v
        vt_out[...] = v.T
        kr = _rope(slot_ref[...].astype(F32), c, sa, sb)
        for h in range(MLA_HEADS):
            cs = slice(h * LANES, (h + 1) * LANES)
            qh = _rope(q[:, cs], c, sa, sb).astype(BF)
            kh = (kp[:, cs] + kr).astype(BF)
            q_out[:, cs] = qh
            k_out[:, cs] = kh
            qt_out[cs, :] = qh.T
            kt_out[cs, :] = kh.T

    return pl.pallas_call(
        body, name="mla_pre_fwd", grid=(S // tr,),
        in_specs=[_rows(tr, 256, 4), _rows(tr, 128, 10), _rows(tr, 128, 11), q_norm.spec(), kv_norm.spec(),
                  wuq.spec(), _full((128, 1024)), _full((128, 512)),
                  _rows(tr, 128), _rows(tr, 128), _rows(tr, 128)],
        out_specs=(_rows(tr, 1024), _rows(tr, 1024), _rows(tr, 512), _cols(1024, tr), _cols(1024, tr), _cols(512, tr)),
        out_shape=(jax.ShapeDtypeStruct((S, 1024), BF), jax.ShapeDtypeStruct((S, 1024), BF),
                   jax.ShapeDtypeStruct((S, 512), BF), jax.ShapeDtypeStruct((1024, S), BF),
                   jax.ShapeDtypeStruct((1024, S), BF), jax.ShapeDtypeStruct((512, S), BF)),
        compiler_params=_cparams(("parallel",)),
    )(proj, proj, proj, q_norm.arr, kv_norm.arr, wuq.arr, wk, wv, tc, tsa, tsb)


def _mla_pre_bwd(dq, dk, dv, proj, q_norm, kv_norm, wuq, wk, wv, tc, tsa, tsb):
    S = proj.shape[0]
    tr = min(T_ROWS_BWD, S)

    def body(dq_ref, dk_ref, dv_ref, cq_ref, ckv_ref, qn_ref, kn_ref, wuq_ref, wk_ref, wv_ref, c_ref, sa_ref, sb_ref,
             dcq_ref, dckv_ref, dslot_ref, dwuq_ref, dwk_ref, dwv_ref, dqn_ref, dkn_ref, dqp):
        @pl.when(pl.program_id(0) == 0)
        def _():
            dwuq_ref[...] = jnp.zeros_like(dwuq_ref)
            dwk_ref[...] = jnp.zeros_like(dwk_ref)
            dwv_ref[...] = jnp.zeros_like(dwv_ref)
            dqn_ref[...] = jnp.zeros_like(dqn_ref)
            dkn_ref[...] = jnp.zeros_like(dkn_ref)

        c, sa, sb = c_ref[...], sa_ref[...], sb_ref[...]
        dkr = jnp.zeros((tr, LANES), F32)
        for h in range(MLA_HEADS):
            cs = slice(h * LANES, (h + 1) * LANES)
            dqp[:, cs] = (_rope_t(dq_ref[:, cs], c, sa, sb) * MLA_SCALE).astype(BF)
            dkr = dkr + dk_ref[:, cs]
        lane = lax.broadcasted_iota(jnp.int32, (tr, LANES), 1)
        in_rope = (lane >= MLA_NOPE) & (lane < MLA_NOPE + MLA_ROPE)
        dslot_ref[...] = jnp.where(in_rope, _rope_t(dkr, c, sa, sb), 0.0).astype(dslot_ref.dtype)

        cq = cq_ref[...].astype(F32)
        gq = qn_ref[0]
        qn, rq = _rms(cq, gq)
        dqpb = dqp[...]
        dwuq_ref[...] += _dot_tn(qn.astype(BF), dqpb)
        dqn = _dot_nt(dqpb, wuq_ref[...])
        dqn_ref[...] += jnp.sum(dqn * cq * rq, axis=0, keepdims=True)
        dyg = dqn * gq
        dcq_ref[...] = (rq * dyg - cq * (rq * rq * rq) * jnp.mean(dyg * cq, axis=-1, keepdims=True)).astype(dcq_ref.dtype)

        ckv = ckv_ref[...].astype(F32)
        gk = kn_ref[0]
        kn, rk = _rms(ckv, gk)
        knb = kn.astype(BF)
        dkb = dk_ref[...].astype(BF)
        dvb = dv_ref[...].astype(BF)
        dwk_ref[...] += _dot_tn(knb, dkb)
        dwv_ref[...] += _dot_tn(knb, dvb)
        dkn = _dot_nt(dkb, wk_ref[...]) + _dot_nt(dvb, wv_ref[...])
        dkn_ref[...] += jnp.sum(dkn * ckv * rk, axis=0, keepdims=True)
        dyk = dkn * gk
        dckv_ref[...] = (rk * dyk - ckv * (rk * rk * rk) * jnp.mean(dyk * ckv, axis=-1, keepdims=True)).astype(dckv_ref.dtype)

    return pl.pallas_call(
        body, name="mla_pre_bwd", grid=(S // tr,),
        in_specs=[_rows(tr, 1024), _rows(tr, 1024), _rows(tr, 512), _rows(tr, 256, 4), _rows(tr, 128, 10),
                  q_norm.spec(), kv_norm.spec(), wuq.spec(), _full((128, 1024)), _full((128, 512)),
                  _rows(tr, 128), _rows(tr, 128), _rows(tr, 128)],
        out_specs=(_rows(tr, 256), _rows(tr, 128), _rows(tr, 128), _full((256, 1024)), _full((128, 1024)),
                   _full((128, 512)), _full((1, 256)), _full((1, 128))),
        out_shape=(jax.ShapeDtypeStruct((S, 256), BF), jax.ShapeDtypeStruct((S, 128), BF),
                   jax.ShapeDtypeStruct((S, 128), BF), jax.ShapeDtypeStruct((256, 1024), F32),
                   jax.ShapeDtypeStruct((128, 1024), F32), jax.ShapeDtypeStruct((128, 512), F32),
                   jax.ShapeDtypeStruct((1, 256), F32), jax.ShapeDtypeStruct((1, 128), F32)),
        scratch_shapes=[pltpu.VMEM((tr, 1024), BF)],
        compiler_params=_cparams(("arbitrary",)),
    )(dq, dk, dv, proj, proj, q_norm.arr, kv_norm.arr, wuq.arr, wk, wv, tc, tsa, tsb)


ANY = pl.BlockSpec(memory_space=pl.ANY)
N_REL = N_DEV - 1


def _coords():
    return lax.axis_index("x"), lax.axis_index("y"), lax.axis_index("c")


def _sem_shapes(nbuf):
    return [pltpu.SemaphoreType.DMA((N_REL * nbuf,)), pltpu.SemaphoreType.DMA((N_REL * nbuf,)),
            pltpu.SemaphoreType.DMA((nbuf,))]


def _ag_plan(srcs, dsts, sems):
    send_sems, recv_sems, _ = sems
    plan = []
    for b, (src, dst) in enumerate(zip(srcs, dsts)):
        def slot(px, py, pc, dst=dst):
            return dst.at[4 * px + 2 * py + pc]

        def copy(k, blk, to, s=None, b=b, slot=slot):
            return pltpu.make_async_remote_copy(
                src_ref=slot(*blk) if s is None else s, dst_ref=slot(*blk), send_sem=send_sems.at[N_REL * b + k],
                recv_sem=recv_sems.at[N_REL * b + k], device_id=to, device_id_type=MESH)

        plan.append((b, src, slot, copy))
    return plan


def _ag_start(srcs, dsts, sems):
    x, y, c = _coords()
    chips = [(1 - x, y), (x, 1 - y), (1 - x, 1 - y)]
    for b, src, slot, copy in _ag_plan(srcs, dsts, sems):
        pltpu.make_async_copy(src, slot(x, y, c), sems[2].at[b]).start()
        copy(0, (x, y, c), (x, y, 1 - c), src).start()
        for j, chip in enumerate(chips):
            copy(1 + j, (x, y, c), (*chip, c), src).start()


def _ag_finish(srcs, dsts, sems):
    x, y, c = _coords()
    me, sibling = (x, y, c), (x, y, 1 - c)
    chips = [(1 - x, y), (x, 1 - y), (1 - x, 1 - y)]
    plan = _ag_plan(srcs, dsts, sems)
    for b, src, slot, copy in plan:
        for j, chip in enumerate(chips):
            copy(1 + j, (*chip, c), me).wait_recv()
            copy(4 + j, (*chip, c), sibling).start()
    for b, src, slot, copy in plan:
        copy(0, sibling, me).wait_recv()
        for j, chip in enumerate(chips):
            copy(4 + j, (*chip, 1 - c), me).wait_recv()
        copy(0, me, sibling, src).wait_send()
        for j, chip in enumerate(chips):
            copy(1 + j, me, (*chip, c), src).wait_send()
            copy(4 + j, (*chip, c), sibling).wait_send()
        pltpu.make_async_copy(src, slot(*me), sems[2].at[b]).wait()


def _a2a_copies(srcs, dsts, sems):
    send_sems, recv_sems, local_sems = sems
    x, y, c = _coords()
    me = 4 * x + 2 * y + c
    local, remote = [], []
    for b, (src, dst) in enumerate(zip(srcs, dsts)):
        for rel in range(1, N_DEV):
            px = 1 - x if rel & 4 else x
            py = 1 - y if rel & 2 else y
            pc = 1 - c if rel & 1 else c
            remote.append(pltpu.make_async_remote_copy(
                src_ref=src.at[4 * px + 2 * py + pc], dst_ref=dst.at[me], send_sem=send_sems.at[N_REL * b + rel - 1],
                recv_sem=recv_sems.at[N_REL * b + rel - 1], device_id=(px, py, pc), device_id_type=MESH))
        local.append(pltpu.make_async_copy(src.at[me], dst.at[me], local_sems.at[b]))
    return local, remote


def _a2a_start(srcs, dsts, sems):
    local, remote = _a2a_copies(srcs, dsts, sems)
    for d in local + remote:
        d.start()


def _a2a_finish(srcs, dsts, sems):
    local, remote = _a2a_copies(srcs, dsts, sems)
    for d in remote + local:
        d.wait()


class _Exchange:
    def __init__(self, kind, srcs):
        self.kind, self.srcs = kind, list(srcs)
        self.n = len(self.srcs)

    def out_shapes(self):
        if self.kind == "ag":
            return [jax.ShapeDtypeStruct((N_DEV,) + s.shape, s.dtype) for s in self.srcs]
        return [jax.ShapeDtypeStruct(s.shape, s.dtype) for s in self.srcs]

    def start(self, src_refs, dst_refs, sems):
        (_ag_start if self.kind == "ag" else _a2a_start)(src_refs, dst_refs, sems)

    def finish(self, src_refs, dst_refs, sems):
        (_ag_finish if self.kind == "ag" else _a2a_finish)(src_refs, dst_refs, sems)


def _carry_call(body, name, grid, in_specs, out_specs, out_shape, scratch, semantics, args, exs):
    in_specs, out_specs, out_shape, scratch = list(in_specs), list(out_specs), list(out_shape), list(scratch)
    if not exs:
        return pl.pallas_call(body, name=name, grid=grid, in_specs=in_specs, out_specs=out_specs, out_shape=out_shape,
                              scratch_shapes=scratch, compiler_params=_cparams(semantics))(*args), []
    n_in, n_out, n_scr = len(in_specs), len(out_specs), len(scratch)
    n_ex = sum(e.n for e in exs)

    def wrapped(*refs):
        ins, refs = refs[:n_in], refs[n_in:]
        srcs, refs = refs[:n_ex], refs[n_ex:]
        outs, refs = refs[:n_out], refs[n_out:]
        dsts, refs = refs[:n_ex], refs[n_ex:]
        scr, sems = refs[:n_scr], refs[n_scr:]
        views, off = [], 0
        for i, e in enumerate(exs):
            views.append((srcs[off:off + e.n], dsts[off:off + e.n], sems[3 * i:3 * i + 3]))
            off += e.n
        first = last = None
        for axis, size in enumerate(grid):
            at0, at1 = pl.program_id(axis) == 0, pl.program_id(axis) == size - 1
            first = at0 if first is None else first & at0
            last = at1 if last is None else last & at1

        @pl.when(first)
        def _():
            for e, view in zip(exs, views):
                e.start(*view)

        body(*ins, *outs, *scr)

        @pl.when(last)
        def _():
            for e, view in zip(exs, views):
                e.finish(*view)

    res = pl.pallas_call(
        wrapped, name=name + "_x", grid=grid, in_specs=in_specs + [ANY] * n_ex, out_specs=out_specs + [ANY] * n_ex,
        out_shape=out_shape + [s for e in exs for s in e.out_shapes()],
        scratch_shapes=scratch + [s for e in exs for s in _sem_shapes(e.n)],
        compiler_params=_cparams(("arbitrary",) * len(grid)))(*args, *[s for e in exs for s in e.srcs])
    got, off = [], n_out
    for e in exs:
        got.append(list(res[off:off + e.n]))
        off += e.n
    return res[:n_out], got


def _exchange_call(name, exs):
    tot = sum(e.n for e in exs)

    def body(*refs):
        srcs, dsts, sems = refs[:tot], refs[tot:2 * tot], refs[2 * tot:]
        views, off = [], 0
        for i, e in enumerate(exs):
            views.append((srcs[off:off + e.n], dsts[off:off + e.n], sems[3 * i:3 * i + 3]))
            off += e.n
        for e, view in zip(exs, views):
            e.start(*view)
        for e, view in zip(exs, views):
            e.finish(*view)

    outs = pl.pallas_call(
        body, name=name, in_specs=[ANY] * tot, out_specs=[ANY] * tot,
        out_shape=[s for e in exs for s in e.out_shapes()],
        scratch_shapes=[s for e in exs for s in _sem_shapes(e.n)],
    )(*[s for e in exs for s in e.srcs])
    res, off = [], 0
    for e in exs:
        res.append(list(outs[off:off + e.n]))
        off += e.n
    return res


def _flash_call(body, name, exs, in_specs, out_specs, out_shape, scratch, n, args):
    return _carry_call(body, name, (MLA_HEADS // 2, n * (n + 1) // 2), in_specs, out_specs, out_shape, scratch,
                       ("parallel", "arbitrary"), args, exs)


def _tri_rows(s, n):
    at = [(s >= r * (r + 1) // 2).astype(jnp.int32) for r in range(1, n)]
    return sum(at), s - sum(a * r for a, r in zip(at, range(1, n)))


def _tri_cols(s, n):
    starts = [c * n - c * (c - 1) // 2 for c in range(n)]
    col = sum((s >= starts[c]).astype(jnp.int32) for c in range(1, n))
    start = sum(jnp.where(col == c, starts[c], 0) for c in range(n))
    return s - start + col, col


def _pair_rows(a):
    at = a.T
    return jnp.concatenate([at[0:1, :], at[MLA_V:MLA_V + 1, :], jnp.zeros((SUBLANES - 2, a.shape[0]), a.dtype)], axis=0)


def _lower_tri(t):
    return lax.broadcasted_iota(jnp.int32, (t, t), 0) >= lax.broadcasted_iota(jnp.int32, (t, t), 1)


def _upper_tri(t):
    return lax.broadcasted_iota(jnp.int32, (t, t), 1) >= lax.broadcasted_iota(jnp.int32, (t, t), 0)


def _flash_fwd(q, kt, v, exs=()):
    S = q.shape[0]
    t = min(T_ATT, S)
    n = S // t

    def body(q_ref, kt_ref, v_ref, o_ref, lse_ref, lse_t_ref, m_s, l_s, acc):
        qi, ki = _tri_rows(pl.program_id(1), n)
        lo = lax.broadcasted_iota(jnp.int32, (t, LANES), 1) < MLA_V

        @pl.when(ki == 0)
        def _():
            m_s[...] = jnp.full_like(m_s, NEG)
            l_s[...] = jnp.zeros_like(l_s)
            acc[...] = jnp.zeros_like(acc)

        keep = _lower_tri(t) | (ki < qi)
        vv = v_ref[...]
        heads = range(2)
        ss = [jnp.where(keep, _dot(q_ref[:, h * LANES:(h + 1) * LANES], kt_ref[h * LANES:(h + 1) * LANES, :]), NEG)
              for h in heads]
        m_prev = [m_s[h] for h in heads]
        l_prev = [l_s[h] for h in heads]
        m_new = [jnp.maximum(m_prev[h], jnp.max(ss[h], axis=1, keepdims=True)) for h in heads]
        al = [jnp.exp(m_prev[h] - m_new[h]) for h in heads]
        ps = [jnp.exp(ss[h] - m_new[h][:, :1]) for h in heads]
        l_new = [al[h] * l_prev[h] + jnp.sum(ps[h], axis=1, keepdims=True) for h in heads]
        pv = [_dot(ps[h].astype(BF), vv) for h in heads]
        for h in heads:
            m_s[h] = m_new[h]
            l_s[h] = l_new[h]
        acc[...] = jnp.where(lo, al[0], al[1]) * acc[...] + jnp.where(lo, pv[0], pv[1])

        @pl.when(ki == qi)
        def _():
            o_ref[...] = acc[...] / jnp.where(lo, l_s[0], l_s[1])
            lse = jnp.where(lo, m_s[0] + jnp.log(l_s[0]), m_s[1] + jnp.log(l_s[1]))
            lse_ref[0] = lse
            lse_t_ref[0] = _pair_rows(lse)

    return _flash_call(
        body, "mla_flash_fwd", exs,
        [pl.BlockSpec((t, 256), lambda p, s: (_tri_rows(s, n)[0], p)),
         pl.BlockSpec((256, t), lambda p, s: (p, _tri_rows(s, n)[1])),
         pl.BlockSpec((t, 128), lambda p, s: (_tri_rows(s, n)[1], p))],
        [pl.BlockSpec((t, 128), lambda p, s: (_tri_rows(s, n)[0], p)),
         pl.BlockSpec((1, t, 128), lambda p, s: (p, _tri_rows(s, n)[0], 0)),
         pl.BlockSpec((1, SUBLANES, t), lambda p, s: (p, 0, _tri_rows(s, n)[0]))],
        [jax.ShapeDtypeStruct((S, 512), F32), jax.ShapeDtypeStruct((MLA_HEADS // 2, S, 128), F32),
         jax.ShapeDtypeStruct((MLA_HEADS // 2, SUBLANES, S), F32)],
        [pltpu.VMEM((2, t, 128), F32), pltpu.VMEM((2, t, 128), F32), pltpu.VMEM((t, 128), F32)], n, (q, kt, v))


def _flash_bwd_dq(q, k, kt, vt, do, lse, delta, exs=()):
    S = q.shape[0]
    t = min(T_ATT, S)
    n = S // t

    def body(q_ref, k_ref, kt_ref, vt_ref, do_ref, lse_ref, dl_ref, dq_ref, acc):
        qi, ki = _tri_rows(pl.program_id(1), n)
        lo = lax.broadcasted_iota(jnp.int32, (t, LANES), 1) < MLA_V

        @pl.when(ki == 0)
        def _():
            acc[...] = jnp.zeros_like(acc)

        keep = _lower_tri(t) | (ki < qi)
        heads = range(2)
        cs = [slice(h * LANES, (h + 1) * LANES) for h in heads]
        col = [slice(h * MLA_V, h * MLA_V + 1) for h in heads]
        lse, dl, dov, vt = lse_ref[0], dl_ref[0], do_ref[...], vt_ref[...]
        ss = [jnp.where(keep, _dot(q_ref[:, cs[h]], kt_ref[cs[h], :]), NEG) for h in heads]
        dp = [_dot(jnp.where(lo if h == 0 else ~lo, dov, 0).astype(BF), vt) for h in heads]
        ds = [(jnp.exp(ss[h] - lse[:, col[h]]) * (dp[h] - dl[:, col[h]])).astype(BF) for h in heads]
        dq = [_dot(ds[h], k_ref[:, cs[h]]) for h in heads]
        acc[...] += jnp.concatenate(dq, axis=1)

        @pl.when(ki == qi)
        def _():
            dq_ref[...] = acc[...]

    (dq,), got = _flash_call(
        body, "mla_flash_dq", exs,
        [pl.BlockSpec((t, 256), lambda p, s: (_tri_rows(s, n)[0], p)),
         pl.BlockSpec((t, 256), lambda p, s: (_tri_rows(s, n)[1], p)),
         pl.BlockSpec((256, t), lambda p, s: (p, _tri_rows(s, n)[1])),
         pl.BlockSpec((128, t), lambda p, s: (p, _tri_rows(s, n)[1])),
         pl.BlockSpec((t, 128), lambda p, s: (_tri_rows(s, n)[0], p)),
         pl.BlockSpec((1, t, 128), lambda p, s: (p, _tri_rows(s, n)[0], 0)),
         pl.BlockSpec((1, t, 128), lambda p, s: (p, _tri_rows(s, n)[0], 0))],
        [pl.BlockSpec((t, 256), lambda p, s: (_tri_rows(s, n)[0], p))],
        [jax.ShapeDtypeStruct((S, 1024), F32)],
        [pltpu.VMEM((t, 256), F32)], n, (q, k, kt, vt, do, lse, delta))
    return dq, got


def _flash_bwd_dkv(q, qt, k, v, do, dot_, lse_t, delta_t, exs=()):
    S = q.shape[0]
    t = min(T_ATT, S)
    n = S // t

    def body(q_ref, qt_ref, k_ref, v_ref, do_ref, dot_ref, lse_ref, dl_ref, dk_ref, dv_ref, dk_acc, dv_acc):
        qi, ki = _tri_cols(pl.program_id(1), n)
        lo = lax.broadcasted_iota(jnp.int32, (t, LANES), 1) < MLA_V
        top = lax.broadcasted_iota(jnp.int32, (LANES, t), 0) < MLA_V

        @pl.when(qi == ki)
        def _():
            dk_acc[...] = jnp.zeros_like(dk_acc)
            dv_acc[...] = jnp.zeros_like(dv_acc)

        keep = _upper_tri(t) | (qi > ki)
        heads = range(2)
        cs = [slice(h * LANES, (h + 1) * LANES) for h in heads]
        vv, lse, dl, dov, dot_v = v_ref[...], lse_ref[0], dl_ref[0], do_ref[...], dot_ref[...]
        st = [jnp.where(keep, _dot(k_ref[:, cs[h]], qt_ref[cs[h], :]), NEG) for h in heads]
        dpt = [_dot(vv, jnp.where(top if h == 0 else ~top, dot_v, 0).astype(BF)) for h in heads]
        pt = [jnp.exp(st[h] - lse[h:h + 1, :]) for h in heads]
        dst = [(pt[h] * (dpt[h] - dl[h:h + 1, :])).astype(BF) for h in heads]
        dv = [_dot(pt[h].astype(BF), jnp.where(lo if h == 0 else ~lo, dov, 0).astype(BF)) for h in heads]
        dk = [_dot(dst[h], q_ref[:, cs[h]]) for h in heads]
        dv_acc[...] += dv[0] + dv[1]
        dk_acc[...] += jnp.concatenate(dk, axis=1)

        @pl.when(qi == n - 1)
        def _():
            dk_ref[...] = dk_acc[...]
            dv_ref[...] = dv_acc[...]

    (dk, dv), got = _flash_call(
        body, "mla_flash_dkv", exs,
        [pl.BlockSpec((t, 256), lambda p, s: (_tri_cols(s, n)[0], p)),
         pl.BlockSpec((256, t), lambda p, s: (p, _tri_cols(s, n)[0])),
         pl.BlockSpec((t, 256), lambda p, s: (_tri_cols(s, n)[1], p)),
         pl.BlockSpec((t, 128), lambda p, s: (_tri_cols(s, n)[1], p)),
         pl.BlockSpec((t, 128), lambda p, s: (_tri_cols(s, n)[0], p)),
         pl.BlockSpec((128, t), lambda p, s: (p, _tri_cols(s, n)[0])),
         pl.BlockSpec((1, SUBLANES, t), lambda p, s: (p, 0, _tri_cols(s, n)[0])),
         pl.BlockSpec((1, SUBLANES, t), lambda p, s: (p, 0, _tri_cols(s, n)[0]))],
        [pl.BlockSpec((t, 256), lambda p, s: (_tri_cols(s, n)[1], p)),
         pl.BlockSpec((t, 128), lambda p, s: (_tri_cols(s, n)[1], p))],
        [jax.ShapeDtypeStruct((S, 1024), F32), jax.ShapeDtypeStruct((S, 512), F32)],
        [pltpu.VMEM((t, 256), F32), pltpu.VMEM((t, 128), F32)], n, (q, qt, k, v, do, dot_, lse_t, delta_t))
    return dk, dv, got


def _mem_heads(qm, km_ref, vm_ref):
    ps, os_ = [], []
    for h in range(X_HEADS):
        cs = slice(h * X_HEAD_DIM, (h + 1) * X_HEAD_DIM)
        s = _dot_nt(qm[:, cs].astype(BF), km_ref[:, cs]) * MEM_SCALE
        e = jnp.exp(s - jnp.max(s, axis=1, keepdims=True))
        p = e / jnp.sum(e, axis=1, keepdims=True)
        ps.append(p)
        os_.append(_dot(p.astype(BF), vm_ref[:, cs]))
    return ps, jnp.concatenate(os_, axis=1)


def _mem_fwd(proj, km, vm, p_mem):
    S = proj.shape[0]
    tr = min(T_ROWS, S)
    M = km.shape[0]

    def body(q_ref, z_ref, km_ref, vm_ref, p_ref, o_ref):
        _, o = _mem_heads(q_ref[...], km_ref, vm_ref)
        y = o * _silu(z_ref[...].astype(F32))
        o_ref[...] = _dot(y.astype(BF), p_ref[...]).astype(o_ref.dtype)

    return pl.pallas_call(
        body, name="mem_fwd", grid=(S // tr,),
        in_specs=[_rows(tr, 512, 4), _rows(tr, 512, 5), _full((M, 512)), _full((M, 512)), p_mem.spec()],
        out_specs=_rows(tr, 1024), out_shape=jax.ShapeDtypeStruct((S, D_MODEL), ACT),
        compiler_params=_cparams(("parallel",)),
    )(proj, proj, km, vm, p_mem.arr)


def _mem_bwd(do, proj, km, vm, p_mem):
    S = proj.shape[0]
    tr = min(T_ROWS_BWD, S)
    M = km.shape[0]

    def body(do_ref, q_ref, z_ref, km_ref, vm_ref, p_ref, dq_ref, dz_ref, dkm_ref, dvm_ref, dp_ref):
        @pl.when(pl.program_id(0) == 0)
        def _():
            dkm_ref[...] = jnp.zeros_like(dkm_ref)
            dvm_ref[...] = jnp.zeros_like(dvm_ref)
            dp_ref[...] = jnp.zeros_like(dp_ref)

        qm = q_ref[...]
        z = z_ref[...].astype(F32)
        ps, o = _mem_heads(qm, km_ref, vm_ref)
        sz = _silu(z)
        y = o * sz
        dob = do_ref[...].astype(BF)
        dy = _dot_nt(dob, p_ref[...])
        dp_ref[...] += _dot_tn(y.astype(BF), dob)
        dz_ref[...] = (dy * o * _dsilu(z)).astype(dz_ref.dtype)
        d_o = dy * sz
        for h in range(X_HEADS):
            cs = slice(h * X_HEAD_DIM, (h + 1) * X_HEAD_DIM)
            doh = d_o[:, cs]
            dohb = doh.astype(BF)
            p = ps[h]
            dpr = _dot_nt(dohb, vm_ref[:, cs])
            ds = (p * (dpr - jnp.sum(doh * o[:, cs], axis=1, keepdims=True)) * MEM_SCALE).astype(BF)
            dq_ref[:, cs] = _dot(ds, km_ref[:, cs]).astype(dq_ref.dtype)
            dkm_ref[:, cs] += _dot_tn(ds, qm[:, cs].astype(BF))
            dvm_ref[:, cs] += _dot_tn(p.astype(BF), dohb)

    return pl.pallas_call(
        body, name="mem_bwd", grid=(S // tr,),
        in_specs=[_rows(tr, 1024), _rows(tr, 512, 4), _rows(tr, 512, 5), _full((M, 512)), _full((M, 512)),
                  p_mem.spec()],
        out_specs=(_rows(tr, 512), _rows(tr, 512), _full((M, 512)), _full((M, 512)), _full((512, 1024))),
        out_shape=(jax.ShapeDtypeStruct((S, 512), BF), jax.ShapeDtypeStruct((S, 512), BF),
                   jax.ShapeDtypeStruct((M, 512), F32), jax.ShapeDtypeStruct((M, 512), F32),
                   jax.ShapeDtypeStruct((512, 1024), F32)),
        compiler_params=_cparams(("arbitrary",)),
    )(do, proj, proj, km, vm, p_mem.arr)


def _merge_fwd(x, proj, o_ssm, o_att, o_mem, b_gate, p_mla, w_out, ln_g, ln_b):
    S = x.shape[0]
    tr = min(T_ROWS, S)

    def body(x_ref, lg_ref, z_ref, os_ref, oa_ref, om_ref, bg_ref, p_ref, w_ref, g_ref, b_ref,
             xn_ref, xb_ref, pre_ref, mg_ref):
        gates = _sigmoid(lg_ref[...].astype(F32) + bg_ref[0])
        ya = oa_ref[...] * _silu(z_ref[...].astype(F32))
        o_mla = _dot(ya.astype(BF), p_ref[...])
        merged = (gates[:, :D_MODEL] * os_ref[...].astype(F32) + gates[:, D_MODEL:2 * D_MODEL] * o_mla
                  + gates[:, 2 * D_MODEL:] * om_ref[...].astype(F32))
        mb = merged.astype(BF)
        mg_ref[...] = mb
        pre = ALPHA * x_ref[...] + _dot(mb, w_ref[...])
        pre_ref[...] = pre
        mu = jnp.mean(pre, axis=-1, keepdims=True)
        xc = pre - mu
        var = jnp.mean(xc * xc, axis=-1, keepdims=True)
        xn = xc * lax.rsqrt(var + NORM_EPS) * g_ref[0] + b_ref[0]
        xn_ref[...] = xn
        xb_ref[...] = xn.astype(BF)

    return pl.pallas_call(
        body, name="merge_fwd", grid=(S // tr,),
        in_specs=[_rows(tr, 1024), _rows(tr, 3072, 1), _rows(tr, 512, 3), _rows(tr, 1024), _rows(tr, 512),
                  _rows(tr, 1024), b_gate.spec(), p_mla.spec(), _full((1024, 1024)), ln_g.spec(), ln_b.spec()],
        out_specs=(_rows(tr, 1024), _rows(tr, 1024), _rows(tr, 1024), _rows(tr, 1024)),
        out_shape=(jax.ShapeDtypeStruct((S, 1024), F32), jax.ShapeDtypeStruct((S, 1024), BF),
                   jax.ShapeDtypeStruct((S, 1024), F32), jax.ShapeDtypeStruct((S, 1024), BF)),
        compiler_params=_cparams(("parallel",)),
    )(x, proj, proj, o_ssm, o_att, o_mem, b_gate.arr, p_mla.arr, w_out, ln_g.arr, ln_b.arr)


def _merge_bwd(dxn, pre, merged, proj, o_ssm, o_att, o_mem, b_gate, p_mla, w_out, ln_g):
    S = pre.shape[0]
    tr = min(T_ROWS_BWD, S)

    def body(dxn_ref, pre_ref, mg_ref, lg_ref, z_ref, os_ref, oa_ref, om_ref, bg_ref, p_ref, w_ref, g_ref,
             dxr_ref, dlg_ref, dos_ref, dom_ref, doa_ref, dz_ref, doat_ref, dl_ref, dlt_ref, dw_ref, dp_ref, dbg_ref,
             dg_ref, db_ref):
        @pl.when(pl.program_id(0) == 0)
        def _():
            dw_ref[...] = jnp.zeros_like(dw_ref)
            dp_ref[...] = jnp.zeros_like(dp_ref)
            dbg_ref[...] = jnp.zeros_like(dbg_ref)
            dg_ref[...] = jnp.zeros_like(dg_ref)
            db_ref[...] = jnp.zeros_like(db_ref)

        dxn = dxn_ref[...]
        pre = pre_ref[...]
        mu = jnp.mean(pre, axis=-1, keepdims=True)
        xc = pre - mu
        rstd = lax.rsqrt(jnp.mean(xc * xc, axis=-1, keepdims=True) + NORM_EPS)
        xhat = xc * rstd
        dg_ref[...] += jnp.sum(dxn * xhat, axis=0, keepdims=True)
        db_ref[...] += jnp.sum(dxn, axis=0, keepdims=True)
        dxh = dxn * g_ref[0]
        dpre = rstd * (dxh - jnp.mean(dxh, axis=-1, keepdims=True)
                       - xhat * jnp.mean(dxh * xhat, axis=-1, keepdims=True))
        dxr_ref[...] = ALPHA * dpre
        dpb = dpre.astype(BF)
        dw_ref[...] += _dot_tn(mg_ref[...], dpb)
        dm = _dot_nt(dpb, w_ref[...])

        gates = _sigmoid(lg_ref[...].astype(F32) + bg_ref[0])
        g0, g1, g2 = gates[:, :D_MODEL], gates[:, D_MODEL:2 * D_MODEL], gates[:, 2 * D_MODEL:]
        z = z_ref[...].astype(F32)
        oa = oa_ref[...]
        sz = _silu(z)
        ya = (oa * sz).astype(BF)
        o_mla = _dot(ya, p_ref[...])
        dos_ref[...] = (g0 * dm).astype(dos_ref.dtype)
        dom_ref[...] = (g2 * dm).astype(dom_ref.dtype)
        do_mla = (g1 * dm).astype(BF)
        dl0 = dm * os_ref[...].astype(F32) * g0 * (1.0 - g0)
        dl1 = dm * o_mla * g1 * (1.0 - g1)
        dl2 = dm * om_ref[...].astype(F32) * g2 * (1.0 - g2)
        dl = jnp.concatenate([dl0, dl1, dl2], axis=1)
        dbg_ref[...] += jnp.sum(dl, axis=0, keepdims=True)
        dlg_ref[...] = dl.astype(dlg_ref.dtype)
        dp_ref[...] += _dot_tn(ya, do_mla)
        dya = _dot_nt(do_mla, p_ref[...])
        doa = dya * sz
        doab = doa.astype(BF)
        doa_ref[...] = doab
        doat_ref[...] = doab.T
        dz_ref[...] = (dya * oa * _dsilu(z)).astype(dz_ref.dtype)
        prod = doa * oa
        lo = lax.broadcasted_iota(jnp.int32, (tr, LANES), 1) < MLA_V
        for pr in range(MLA_HEADS // 2):
            blk = prod[:, pr * LANES:(pr + 1) * LANES]
            d0 = jnp.sum(jnp.where(lo, blk, 0.0), axis=1, keepdims=True)
            d1 = jnp.sum(jnp.where(lo, 0.0, blk), axis=1, keepdims=True)
            dl = jnp.where(lo, d0, d1)
            dl_ref[pr] = dl
            dlt_ref[pr] = _pair_rows(dl)

    return pl.pallas_call(
        body, name="merge_bwd", grid=(S // tr,),
        in_specs=[_rows(tr, 1024), _rows(tr, 1024), _rows(tr, 1024), _rows(tr, 3072, 1), _rows(tr, 512, 3),
                  _rows(tr, 1024), _rows(tr, 512), _rows(tr, 1024), b_gate.spec(), p_mla.spec(),
                  _full((1024, 1024)), ln_g.spec()],
        out_specs=(_rows(tr, 1024), _rows(tr, 3072), _rows(tr, 1024), _rows(tr, 1024), _rows(tr, 512),
                   _rows(tr, 512), _cols(512, tr), pl.BlockSpec((MLA_HEADS // 2, tr, 128), lambda i: (0, i, 0)),
                   pl.BlockSpec((MLA_HEADS // 2, SUBLANES, tr), lambda i: (0, 0, i)),
                   _full((1024, 1024)), _full((512, 1024)), _full((1, 3072)), _full((1, 1024)), _full((1, 1024))),
        out_shape=(jax.ShapeDtypeStruct((S, 1024), F32), jax.ShapeDtypeStruct((S, 3072), BF),
                   jax.ShapeDtypeStruct((S, 1024), BF), jax.ShapeDtypeStruct((S, 1024), BF),
                   jax.ShapeDtypeStruct((S, 512), BF), jax.ShapeDtypeStruct((S, 512), BF),
                   jax.ShapeDtypeStruct((512, S), BF), jax.ShapeDtypeStruct((MLA_HEADS // 2, S, 128), F32),
                   jax.ShapeDtypeStruct((MLA_HEADS // 2, SUBLANES, S), F32),
                   jax.ShapeDtypeStruct((1024, 1024), F32), jax.ShapeDtypeStruct((512, 1024), F32),
                   jax.ShapeDtypeStruct((1, 3072), F32), jax.ShapeDtypeStruct((1, 1024), F32),
                   jax.ShapeDtypeStruct((1, 1024), F32)),
        compiler_params=_cparams(("arbitrary",)),
    )(dxn, pre, merged, proj, proj, o_ssm, o_att, o_mem, b_gate.arr, p_mla.arr, w_out, ln_g.arr)


def _loss_head(y, t):
    S = y.shape[0]
    tr = min(T_ROWS, S)
    n = S // tr

    def body(y_ref, t_ref, dy_ref, l_ref, acc):
        i = pl.program_id(0)

        @pl.when(i == 0)
        def _():
            acc[...] = jnp.zeros_like(acc)

        e = y_ref[...] - t_ref[...]
        dy_ref[...] = e * (1.0 / D_MODEL)
        acc[...] += jnp.sum(e * e, axis=0, keepdims=True)

        @pl.when(i == n - 1)
        def _():
            tot = jnp.sum(acc[...], axis=1, keepdims=True) * (0.5 / D_MODEL)
            l_ref[...] = jnp.broadcast_to(tot, l_ref.shape)

    return pl.pallas_call(
        body, name="loss_head", grid=(n,),
        in_specs=[_rows(tr, 1024), _rows(tr, 1024)],
        out_specs=(_rows(tr, 1024), _full((SUBLANES, LANES))),
        out_shape=(jax.ShapeDtypeStruct((S, 1024), F32), jax.ShapeDtypeStruct((SUBLANES, LANES), F32)),
        scratch_shapes=[pltpu.VMEM((1, 1024), F32)],
        compiler_params=_cparams(("arbitrary",)),
    )(y, t)


def _rope_tables(pos):
    inv_freq = ROPE_THETA ** (-jnp.arange(0, MLA_ROPE, 2, dtype=F32) / MLA_ROPE)
    ang = pos.astype(F32)[:, None] * inv_freq
    cos, sin = jnp.cos(ang), jnp.sin(ang)
    S = pos.shape[0]
    half = MLA_ROPE // 2
    ones = jnp.ones((S, MLA_NOPE), F32)
    z16 = jnp.zeros((S, half), F32)
    z32 = jnp.zeros((S, LANES - MLA_NOPE - MLA_ROPE), F32)
    z64 = jnp.zeros((S, MLA_NOPE), F32)
    c = jnp.concatenate([ones, cos, cos, z32], axis=1)
    sa = jnp.concatenate([z64, -sin, z16, z32], axis=1)
    sb = jnp.concatenate([z64, z16, sin, z32], axis=1)
    return c, sa, sb


def _ssm_discretise(a_re, a_im, log_dt, b_re, b_im):
    dt = jnp.exp(log_dt)[..., None]
    mag = jnp.exp(a_re * dt)
    lb_re = mag * jnp.cos(a_im * dt)
    lb_im = mag * jnp.sin(a_im * dt)
    nr, ni = lb_re - 1.0, lb_im
    den = a_re * a_re + a_im * a_im
    f_re = (nr * a_re + ni * a_im) / den
    f_im = (ni * a_re - nr * a_im) / den
    bb_re = f_re[..., None] * b_re - f_im[..., None] * b_im
    bb_im = f_re[..., None] * b_im + f_im[..., None] * b_re
    return lb_re, lb_im, bb_re, bb_im


_GPB = SSM_CB // SSM_GROUP


def _bd_in(bb):
    nb = SSM_GROUPS // _GPB
    t = bb.reshape(nb, _GPB, SSM_STATE, SSM_GROUP)
    eye = jnp.eye(_GPB, dtype=bb.dtype)
    return jnp.einsum("ngpc,gh->ngchp", t, eye).reshape(nb, SSM_CB, _GPB * SSM_STATE)


def _bd_in_t(d):
    nb = SSM_GROUPS // _GPB
    t = d.reshape(nb, _GPB, SSM_GROUP, _GPB, SSM_STATE)
    eye = jnp.eye(_GPB, dtype=d.dtype)
    return jnp.einsum("ngchp,gh->ngpc", t, eye).reshape(SSM_GROUPS, SSM_STATE, SSM_GROUP)


def _bd_out(c):
    nb = SSM_GROUPS // _GPB
    t = c.reshape(nb, _GPB, SSM_GROUP, SSM_STATE)
    eye = jnp.eye(_GPB, dtype=c.dtype)
    return jnp.einsum("ngcp,gh->ngphc", t, eye).reshape(nb, _GPB * SSM_STATE, SSM_CB)


def _interleave(a):
    S, w = a.shape
    return a.reshape(SUBLANES, S // SUBLANES, w).transpose(1, 0, 2).reshape(S, w)


def _deinterleave(a):
    S, w = a.shape
    return a.reshape(S // SUBLANES, SUBLANES, w).transpose(1, 0, 2).reshape(S, w)


IN_SHARD = D_IN // N_DEV
ROPE_OWNER = ROPE_SLOT_LO // IN_SHARD
assert ROPE_OWNER * IN_SHARD <= ROPE_SLOT_LO and ROPE_SLOT_LO + MLA_ROPE <= (ROPE_OWNER + 1) * IN_SHARD


def _w_in_from_shards(g):
    pieces = []
    for j in range(N_DEV):
        if j == ROPE_OWNER:
            a = ROPE_SLOT_LO - j * IN_SHARD
            z = lambda n: jnp.zeros((g.shape[1], n), g.dtype)
            pieces += [g[j][:, :a], z(MLA_NOPE), g[j][:, a:a + MLA_ROPE], z(LANES - MLA_NOPE - MLA_ROPE),
                       g[j][:, a + MLA_ROPE:]]
        else:
            pieces.append(g[j])
    return jnp.concatenate(pieces, axis=1)


def _w_in_to_shards(d):
    shift = LANES - MLA_ROPE
    out = []
    for j in range(N_DEV):
        lo, hi = j * IN_SHARD, (j + 1) * IN_SHARD
        if j < ROPE_OWNER:
            out.append(d[:, lo:hi])
        elif j > ROPE_OWNER:
            out.append(d[:, lo + shift:hi + shift])
        else:
            r = ROPE_SLOT_LO + MLA_NOPE
            out.append(jnp.concatenate([d[:, lo:ROPE_SLOT_LO], d[:, r:r + MLA_ROPE],
                                        d[:, ROPE_SLOT_LO + LANES:hi + shift]], axis=1))
    return jnp.stack(out)


def _adamw_math(w, g, m, v):
    m = ADAM_B1 * m + (1.0 - ADAM_B1) * g
    v = ADAM_B2 * v + (1.0 - ADAM_B2) * (g * g)
    m_hat = m / (1.0 - ADAM_B1 ** ADAM_STEP)
    v_hat = v / (1.0 - ADAM_B2 ** ADAM_STEP)
    delta = -ADAM_LR * (m_hat / (jnp.sqrt(v_hat) + ADAM_EPS) + ADAM_WD * w)
    return delta, m, v


def _adamw_sharded(parts, w, m, v, tile, name):
    L, _, R, C = parts.shape
    assert R % tile == 0

    def body(p_ref, w_ref, m_ref, v_ref, g_out, d_out, m_out, v_out):
        g = p_ref[0, 0].astype(F32)
        for k in range(1, N_DEV):
            g = g + p_ref[0, k].astype(F32)
        d, mn, vn = _adamw_math(w_ref[0], g, m_ref[0], v_ref[0])
        g_out[0] = g
        d_out[0] = d
        m_out[0] = mn
        v_out[0] = vn

    spec = pl.BlockSpec((1, tile, C), lambda l, i: (l, i, 0))
    shp = jax.ShapeDtypeStruct((L, R, C), F32)
    return pl.pallas_call(
        body, name=name, grid=(L, R // tile),
        in_specs=[pl.BlockSpec((1, N_DEV, tile, C), lambda l, i: (l, 0, i, 0)), spec, spec, spec],
        out_specs=(spec,) * 4, out_shape=(shp,) * 4, compiler_params=_cparams(("parallel", "parallel")),
    )(parts, w, m, v)


COL_GROUP = (("w_glu", 512), ("p_ssm", 512), ("p_mla", 512), ("p_mem", 512), ("w_uq", 256), ("w_ukv", 128))
COL_AT = {n: sum(r for _, r in COL_GROUP[:i]) // rows for i, (n, rows) in enumerate(COL_GROUP)}
assert all(sum(r for _, r in COL_GROUP[:i]) % rows == 0 for i, (_, rows) in enumerate(COL_GROUP))
COL_ROWS = dict(COL_GROUP)
ROW_GROUP = ("w_mem_kv", "w_out")
SMALL = ("b_gate", "ssm_a_re", "ssm_a_im", "ssm_log_dt", "ssm_b_re", "ssm_b_im", "ssm_c_re", "ssm_c_im", "ssm_d",
         "b_glu", "mla_q_norm", "mla_kv_norm", "ln_g", "ln_b")
UQ_COLS = MLA_NOPE + MLA_ROPE


def _pad_lanes(a):
    return jnp.concatenate([a, jnp.zeros(a.shape[:-1] + (LANES - a.shape[-1],), a.dtype)], axis=-1)


def _group_buffers(d, dtype):
    col = jnp.concatenate([_pad_lanes(d[n]) if n == "w_uq" else d[n] for n, _ in COL_GROUP], axis=1)
    row = jnp.concatenate([d[n] for n in ROW_GROUP], axis=1)
    return d["w_in"].astype(dtype), col.astype(dtype), row.astype(dtype)


def _ungroup(bufs):
    b_in, col, row = bufs
    out, off = {"w_in": b_in}, 0
    for n, rows in COL_GROUP:
        t = col[:, off:off + rows]
        out[n] = t[..., :UQ_COLS] if n == "w_uq" else t
        off += rows
    k = row.shape[1] // 2
    out["w_mem_kv"], out["w_out"] = row[:, :k], row[:, k:]
    return out


def _colcat(t):
    return t.transpose(1, 0, 2).reshape(t.shape[1], -1)


def _colsplit(g, n):
    return g.reshape(g.shape[0], N_DEV, n).transpose(1, 0, 2)


def _unpack_weights(g_in, g_col, g_row):
    wc = _colcat(g_col)
    at = lambda n: _RowBlock(wc, COL_ROWS[n], COL_AT[n])
    lo = COL_AT["w_ukv"] * COL_ROWS["w_ukv"]
    ukv = wc[lo:lo + COL_ROWS["w_ukv"]].reshape(-1, MLA_HEADS, LANES)
    lane = lax.broadcasted_iota(jnp.int32, ukv.shape, 2)
    k = g_row.shape[1] // 2
    return dict(
        w_in=_w_in_from_shards(g_in), w_glu=at("w_glu"), w_uq=at("w_uq"), p_ssm=at("p_ssm"), p_mla=at("p_mla"),
        p_mem=at("p_mem"), w_k=jnp.where(lane < MLA_NOPE, ukv, jnp.zeros_like(ukv)).reshape(ukv.shape[0], -1),
        w_v=ukv[..., MLA_NOPE:].reshape(ukv.shape[0], -1),
        w_mem_kv=g_row[:, :k].reshape(-1, g_row.shape[2]), w_out=g_row[:, k:].reshape(-1, g_row.shape[2]))


def _pack_grads_in(d_w_in):
    return _w_in_to_shards(d_w_in).astype(BF)


def _pack_grads_rest(d):
    ukv = jnp.concatenate([d["w_k"].reshape(-1, MLA_HEADS, LANES)[..., :MLA_NOPE],
                           d["w_v"].reshape(-1, MLA_HEADS, MLA_V)], axis=-1).reshape(d["w_k"].shape[0], -1)
    col = jnp.concatenate([ukv if n == "w_ukv" else d[n] for n, _ in COL_GROUP], axis=0)
    row = jnp.concatenate([d[n].reshape(N_DEV, -1, d[n].shape[1]) for n in ROW_GROUP], axis=1)
    return [_colsplit(col, LANES).astype(BF), row.astype(BF)]


def _pack_small(d, lead):
    parts = []
    for n in SMALL:
        keep = d[n].shape[:lead]
        f = d[n].reshape(keep + (-1,))
        pad = (-f.shape[-1]) % (SUBLANES * LANES)
        if pad:
            f = jnp.concatenate([f, jnp.zeros(keep + (pad,), f.dtype)], axis=-1)
        parts.append(f.reshape(keep + (-1, LANES)))
    return jnp.concatenate(parts, axis=lead)


def _unpack_small(buf, like):
    out, off = {}, 0
    for n in SMALL:
        size = math.prod(like[n].shape[1:])
        rows = -(-size // (SUBLANES * LANES)) * SUBLANES
        out[n] = buf[:, off:off + rows].reshape(buf.shape[0], -1)[:, :size].reshape(like[n].shape)
        off += rows
    return out


WEIGHTS = ("w_in", "b_gate", "ssm_a_re", "ssm_a_im", "ssm_log_dt", "ssm_b_re", "ssm_b_im", "ssm_c_re", "ssm_c_im",
           "ssm_d", "w_glu", "b_glu", "mla_q_norm", "w_uq", "mla_kv_norm", "w_ukv", "w_mem_kv", "p_ssm", "p_mla",
           "p_mem", "w_out", "ln_g", "ln_b")
BIG = ("w_in",) + tuple(n for n, _ in COL_GROUP) + ROW_GROUP


def _train_step(x, mem, pos, target, wl, ws):
    S = x.shape[0]
    tc, tsa, tsb = _rope_tables(pos)
    loc = _group_buffers(wl, BF)
    loc = [[b[l] for b in loc] for l in range(DEPTH)]

    lb_re, lb_im, bb_re, bb_im = _ssm_discretise(ws["ssm_a_re"], ws["ssm_a_im"], ws["ssm_log_dt"], ws["ssm_b_re"],
                                                 ws["ssm_b_im"])
    nb = SSM_GROUPS // _GPB
    mats = (jax.vmap(_bd_in)(bb_re), jax.vmap(_bd_in)(bb_im), lb_re.reshape(DEPTH, nb, 1, -1),
            lb_im.reshape(DEPTH, nb, 1, -1), jax.vmap(_bd_out)(ws["ssm_c_re"]), jax.vmap(_bd_out)(ws["ssm_c_im"]),
            ws["ssm_d"].reshape(DEPTH, 1, -1))

    rows3 = {n: ws[n].reshape(DEPTH, 1, -1) for n in ("b_glu", "mla_q_norm", "mla_kv_norm", "b_gate", "ln_g", "ln_b")}

    def small(n, l):
        return _LayerRow(rows3[n], l)

    ((g_in,),) = _exchange_call("weights_gather_first", [_Exchange("ag", loc[0][:1])])
    W = [None] * DEPTH
    saved = []
    xs, xb = x, x.astype(BF)
    for l in range(DEPTH):
        if l == 0:
            proj, (g_rest,) = _mm(xb, _w_in_from_shards(g_in), name="proj_fwd", tm=S, tn=512, out_dtype=ACT,
                                  exs=[_Exchange("ag", loc[0][1:])])
            W[0] = _unpack_weights(g_in, *g_rest)
        else:
            proj = _mm(xb, W[l]["w_in"], name="proj_fwd", tm=S, tn=512, out_dtype=ACT)
        w = W[l]
        u_il = _interleave(proj[:, :SSM_WIDTH])
        y_raw = _deinterleave(_ssm_fwd(u_il, mats, l))
        o_ssm = _ssm_post_fwd(y_raw, proj, w["w_glu"], small("b_glu", l), w["p_ssm"])
        q, k, v, qt, kt, vt = _mla_pre_fwd(proj, small("mla_q_norm", l), small("mla_kv_norm", l), w["w_uq"], w["w_k"], w["w_v"],
                               tc, tsa, tsb)
        nxt = [_Exchange("ag", loc[l + 1])] if l + 1 < DEPTH else []
        (o_att, lse, lse_t), gathered = _flash_fwd(q, kt, v, nxt)
        if nxt:
            W[l + 1] = _unpack_weights(*gathered[0])
        kvm = _mm(mem, w["w_mem_kv"], name="memkv_fwd", out_dtype=BF)
        km, vm = kvm[:, :512], kvm[:, 512:]
        o_mem = _mem_fwd(proj, km, vm, w["p_mem"])
        xn, xnb, pre, merged = _merge_fwd(xs, proj, o_ssm, o_att, o_mem, small("b_gate", l), w["p_mla"], w["w_out"],
                                          small("ln_g", l), small("ln_b", l))
        saved.append(dict(xb=xb, proj=proj, u_il=u_il, y_raw=y_raw, o_ssm=o_ssm, q=q, k=k, v=v, qt=qt, kt=kt, vt=vt, o_att=o_att,
                          lse=lse, lse_t=lse_t,
                          km=km, vm=vm, o_mem=o_mem, pre=pre, merged=merged))
        xs, xb = xn, xnb

    dxs, lvec = _loss_head(xs, target)
    loss = lvec[0, 0]

    disc_names = ("ssm_a_re", "ssm_a_im", "ssm_log_dt", "ssm_b_re", "ssm_b_im")
    got = [None] * DEPTH
    got_small = [None] * DEPTH
    pending = None
    pending_small = None
    for l in reversed(range(DEPTH)):
        sv, w = saved[l], W[l]
        proj = sv["proj"]
        (dx_res, dlg, do_ssm, do_mem, do_att, dz_mla, do_att_t, delta, delta_t, d_w_out, d_p_mla, d_b_gate, d_ln_g,
         d_ln_b) = _merge_bwd(
            dxs, sv["pre"], sv["merged"], proj, sv["o_ssm"], sv["o_att"], sv["o_mem"], small("b_gate", l), w["p_mla"],
            w["w_out"], small("ln_g", l))
        dq_mem, dz_mem, d_km, d_vm, d_p_mem = _mem_bwd(do_mem, proj, sv["km"], sv["vm"], w["p_mem"])
        d_w_mem = _mm(mem, jnp.concatenate([d_km, d_vm], axis=1), name="memkv_bwd", ta=True)
        dq, arrived_rest = _flash_bwd_dq(
            sv["q"], sv["k"], sv["kt"], sv["vt"], do_att, sv["lse"], delta,
            [_Exchange("a2a", pending[1:]), _Exchange("ag", [pending_small])] if pending is not None else [])
        dk, dv, arrived_in = _flash_bwd_dkv(sv["q"], sv["qt"], sv["k"], sv["v"], do_att, do_att_t, sv["lse_t"], delta_t,
                                            [_Exchange("a2a", pending[:1])] if pending is not None else [])
        if pending is not None:
            got[l + 1] = arrived_in[0] + arrived_rest[0]
            got_small[l + 1] = arrived_rest[1][0]
        dcq, dckv, dslot, d_wuq, d_wk, d_wv, d_qn, d_kn = _mla_pre_bwd(
            dq, dk, dv, proj, small("mla_q_norm", l), small("mla_kv_norm", l), w["w_uq"], w["w_k"], w["w_v"],
            tc, tsa, tsb)
        dy_raw, dz_ssm, d_w_glu, d_b_glu, d_p_ssm = _ssm_post_bwd(do_ssm, sv["y_raw"], proj, w["w_glu"],
                                                                 small("b_glu", l), w["p_ssm"])
        rest = _pack_grads_rest(dict(w_glu=d_w_glu, w_uq=d_wuq, w_k=d_wk, w_v=d_wv, w_mem_kv=d_w_mem, p_ssm=d_p_ssm,
                                     p_mla=d_p_mla, p_mem=d_p_mem, w_out=d_w_out))
        (du_il, dbbre, dbbim, dare, daim, dcre, dcim, dd), early = _ssm_bwd(
            sv["u_il"], _interleave(dy_raw), mats, l, [_Exchange("a2a", rest)] if l == 0 else [])
        du = _deinterleave(du_il).astype(BF)
        _, disc_vjp = jax.vjp(_ssm_discretise, *[ws[n][l] for n in disc_names])
        d_disc = disc_vjp((dare.reshape(SSM_GROUPS, SSM_STATE), daim.reshape(SSM_GROUPS, SSM_STATE), _bd_in_t(dbbre),
                           _bd_in_t(dbbim)))
        dproj = jnp.concatenate([du, dz_ssm, dcq, dckv, dslot, dz_mla, dq_mem, dz_mem, dlg], axis=1)
        d_w_in = _mm(sv["xb"], dproj, name="proj_dw", ta=True, tm=1024, tn=512, tk=S)
        if l > 0:
            dxs = _mm(dproj, w["w_in"], name="proj_dx", tb=True, add=dx_res, tm=1024, tn=1024, tk=1024)
        pending = [_pack_grads_in(d_w_in)] + (rest if l > 0 else [])
        gsl = dict(zip(disc_names, d_disc))
        gsl.update(b_gate=d_b_gate, ssm_c_re=_bd_in_t(dcre).transpose(0, 2, 1), ssm_c_im=_bd_in_t(dcim).transpose(0, 2, 1),
                   ssm_d=dd, b_glu=d_b_glu, mla_q_norm=d_qn, mla_kv_norm=d_kn, ln_g=d_ln_g, ln_b=d_ln_b)
        pending_small = _pack_small(gsl, 0)

    dxs, (last_in, (got_small[0],)) = _mm(
        dproj, w["w_in"], name="proj_dx", tb=True, add=dx_res, tm=1024, tn=1024, tk=1024,
        exs=[_Exchange("a2a", pending), _Exchange("ag", [pending_small])])
    got[0] = last_in + early[0]
    return loss, dxs, got, got_small


def kernel(x, mem, positions, w_in, b_gate, ssm_a_re, ssm_a_im, ssm_log_dt, ssm_b_re, ssm_b_im, ssm_c_re, ssm_c_im, ssm_d, w_glu, b_glu, mla_q_norm, w_uq, mla_kv_norm, w_ukv, w_mem_kv, p_ssm, p_mla, p_mem, w_out, ln_g, ln_b, loss_target, m_w_in, m_b_gate, m_ssm_a_re, m_ssm_a_im, m_ssm_log_dt, m_ssm_b_re, m_ssm_b_im, m_ssm_c_re, m_ssm_c_im, m_ssm_d, m_w_glu, m_b_glu, m_mla_q_norm, m_w_uq, m_mla_kv_norm, m_w_ukv, m_w_mem_kv, m_p_ssm, m_p_mla, m_p_mem, m_w_out, m_ln_g, m_ln_b, v_w_in, v_b_gate, v_ssm_a_re, v_ssm_a_im, v_ssm_log_dt, v_ssm_b_re, v_ssm_b_im, v_ssm_c_re, v_ssm_c_im, v_ssm_d, v_w_glu, v_b_glu, v_mla_q_norm, v_w_uq, v_mla_kv_norm, v_w_ukv, v_w_mem_kv, v_p_ssm, v_p_mla, v_p_mem, v_w_out, v_ln_g, v_ln_b):
    w = dict(w_in=w_in, b_gate=b_gate, ssm_a_re=ssm_a_re, ssm_a_im=ssm_a_im, ssm_log_dt=ssm_log_dt, ssm_b_re=ssm_b_re,
             ssm_b_im=ssm_b_im, ssm_c_re=ssm_c_re, ssm_c_im=ssm_c_im, ssm_d=ssm_d, w_glu=w_glu, b_glu=b_glu,
             mla_q_norm=mla_q_norm, w_uq=w_uq, mla_kv_norm=mla_kv_norm, w_ukv=w_ukv, w_mem_kv=w_mem_kv, p_ssm=p_ssm,
             p_mla=p_mla, p_mem=p_mem, w_out=w_out, ln_g=ln_g, ln_b=ln_b)
    m = dict(w_in=m_w_in, b_gate=m_b_gate, ssm_a_re=m_ssm_a_re, ssm_a_im=m_ssm_a_im, ssm_log_dt=m_ssm_log_dt,
             ssm_b_re=m_ssm_b_re, ssm_b_im=m_ssm_b_im, ssm_c_re=m_ssm_c_re, ssm_c_im=m_ssm_c_im, ssm_d=m_ssm_d,
             w_glu=m_w_glu, b_glu=m_b_glu, mla_q_norm=m_mla_q_norm, w_uq=m_w_uq, mla_kv_norm=m_mla_kv_norm,
             w_ukv=m_w_ukv, w_mem_kv=m_w_mem_kv, p_ssm=m_p_ssm, p_mla=m_p_mla, p_mem=m_p_mem, w_out=m_w_out,
             ln_g=m_ln_g, ln_b=m_ln_b)
    v = dict(w_in=v_w_in, b_gate=v_b_gate, ssm_a_re=v_ssm_a_re, ssm_a_im=v_ssm_a_im, ssm_log_dt=v_ssm_log_dt,
             ssm_b_re=v_ssm_b_re, ssm_b_im=v_ssm_b_im, ssm_c_re=v_ssm_c_re, ssm_c_im=v_ssm_c_im, ssm_d=v_ssm_d,
             w_glu=v_w_glu, b_glu=v_b_glu, mla_q_norm=v_mla_q_norm, w_uq=v_w_uq, mla_kv_norm=v_mla_kv_norm,
             w_ukv=v_w_ukv, w_mem_kv=v_w_mem_kv, p_ssm=v_p_ssm, p_mla=v_p_mla, p_mem=v_p_mem, w_out=v_w_out,
             ln_g=v_ln_g, ln_b=v_ln_b)

    wl = {n: w[n] for n in BIG}
    small = {n: w[n] for n in SMALL}
    loss_local, dx, got, got_small = _train_step(x[0], mem[0], positions[0], loss_target[0], wl, small)
    loss = lax.psum(loss_local, ("x", "y", "c"))

    grads, delta, new_m, new_v = {}, {}, {}, {}
    wg = _group_buffers(wl, F32)
    mg = _group_buffers({n: m[n] for n in BIG}, F32)
    vg = _group_buffers({n: v[n] for n in BIG}, F32)
    res = []
    for i, (tile, tag) in enumerate(((256, "in"), (128, "col"), (256, "row"))):
        parts = jnp.stack([got[l][i] for l in range(DEPTH)])
        res.append(_adamw_sharded(parts, wg[i], mg[i], vg[i], tile, "adamw_" + tag))
    for dst, j in ((grads, 0), (delta, 1), (new_m, 2), (new_v, 3)):
        dst.update(_ungroup([r[j] for r in res]))

    sw, sm, sv = (_pack_small(small, 1), _pack_small({n: m[n] for n in SMALL}, 1), _pack_small({n: v[n] for n in SMALL}, 1))
    rs = _adamw_sharded(jnp.stack(got_small), sw, sm, sv, sw.shape[1], "adamw_replicated")
    for dst, buf in zip((grads, delta, new_m, new_v), rs):
        dst.update(_unpack_small(buf, small))

    return (loss, dx[None], *[grads[n] for n in WEIGHTS], *[delta[n] for n in WEIGHTS],
            *[new_m[n] for n in WEIGHTS], *[new_v[n] for n in WEIGHTS])
```

```python
import math

import jax
import jax.numpy as jnp
from jax import lax
from jax.experimental import pallas as pl
from jax.experimental.pallas import tpu as pltpu

F32 = jnp.float32
BF = jnp.bfloat16
ACT = jnp.bfloat16

D_MODEL = 1024
DEPTH = 4
N_DEV = 8
SSM_WIDTH = 512
SSM_GROUP = 16
SSM_GROUPS = 32
SSM_STATE = 64
MLA_HEADS = 8
MLA_NOPE = 64
MLA_ROPE = 32
MLA_V = 64
MLA_Q_RANK = 256
MLA_KV_RANK = 128
ROPE_THETA = 10000.0
X_HEADS = 4
X_HEAD_DIM = 128
D_IN = 6048
ALPHA = (2 * DEPTH) ** 0.25
NORM_EPS = 1e-5
ADAM_LR = 0.001
ADAM_B1 = 0.9
ADAM_B2 = 0.999
ADAM_EPS = 1e-08
ADAM_WD = 0.01
ADAM_STEP = 10

LANES = 128
SUBLANES = 8
VMEM_LIMIT = 56 * 1024 * 1024

PW = 6144
ROPE_SLOT_LO = 1408
MLA_SCALE = (MLA_NOPE + MLA_ROPE) ** -0.5
MEM_SCALE = X_HEAD_DIM ** -0.5
NEG = -1e30

T_ROWS = 512
T_ROWS_BWD = 256
T_ROWS_BWD_WIDE = 512
T_ATT = 1024
T_MM = 512

MESH = pl.DeviceIdType.MESH


def _cparams(sem):
    return pltpu.CompilerParams(dimension_semantics=sem, vmem_limit_bytes=VMEM_LIMIT)


def _dot(a, b):
    return lax.dot_general(a, b, (((1,), (0,)), ((), ())), preferred_element_type=F32)


def _dot_nt(a, b):
    return lax.dot_general(a, b, (((1,), (1,)), ((), ())), preferred_element_type=F32)


def _dot_tn(a, b):
    return lax.dot_general(a, b, (((0,), (0,)), ((), ())), preferred_element_type=F32)


def _sigmoid(x):
    return 0.5 * jnp.tanh(0.5 * x) + 0.5


def _silu(x):
    return x * _sigmoid(x)


def _dsilu(x):
    s = _sigmoid(x)
    return s * (1.0 + x * (1.0 - s))


_GELU_C = math.sqrt(2.0 / math.pi)


def _gelu(x):
    return 0.5 * x * (1.0 + jnp.tanh(_GELU_C * (x + 0.044715 * x * x * x)))


def _dgelu(x):
    t = jnp.tanh(_GELU_C * (x + 0.044715 * x * x * x))
    return 0.5 * (1.0 + t) + 0.5 * x * (1.0 - t * t) * _GELU_C * (1.0 + 3 * 0.044715 * x * x)


def _rows(tr, w, col=0):
    return pl.BlockSpec((tr, w), lambda i: (i, col))


def _cols(h, tc):
    return pl.BlockSpec((h, tc), lambda i: (0, i))


def _full(shape):
    n = len(shape)
    return pl.BlockSpec(shape, lambda i: (0,) * n)


class _RowBlock:
    def __init__(self, arr, rows, blk):
        self.arr, self.rows, self.blk = arr, rows, blk

    def spec(self):
        blk = self.blk
        return pl.BlockSpec((self.rows, self.arr.shape[1]), lambda i: (blk, 0))


class _LayerRow:
    def __init__(self, arr, l):
        self.arr, self.l = arr, l

    def spec(self):
        l = self.l
        return pl.BlockSpec((1, 1, self.arr.shape[2]), lambda i: (l, 0, 0))


def _mm(a, b, *, name, ta=False, tb=False, out_dtype=F32, add=None, tm=T_MM, tn=T_MM, tk=1024, exs=None):
    M, K = (a.shape[1], a.shape[0]) if ta else a.shape
    N = b.shape[0] if tb else b.shape[1]
    tm, tn, tk = min(tm, M), min(tn, N), min(tk, K)
    assert M % tm == 0 and N % tn == 0 and K % tk == 0, (M, N, K)
    nk = K // tk
    dn = (((0 if ta else 1,), (1 if tb else 0,)), ((), ()))

    def body(*refs):
        if add is not None:
            a_ref, b_ref, c_ref, o_ref = refs[:4]
        else:
            a_ref, b_ref, o_ref = refs[:3]
        part = lax.dot_general(a_ref[...].astype(BF), b_ref[...].astype(BF), dn, preferred_element_type=F32)
        if nk == 1:
            if add is not None:
                part = part + c_ref[...]
            o_ref[...] = part.astype(out_dtype)
            return
        acc = refs[-1]
        k = pl.program_id(2)

        @pl.when(k == 0)
        def _():
            acc[...] = part

        @pl.when(k != 0)
        def _():
            acc[...] += part

        @pl.when(k == nk - 1)
        def _():
            r = acc[...]
            if add is not None:
                r = r + c_ref[...]
            o_ref[...] = r.astype(out_dtype)

    a_spec = pl.BlockSpec((tk, tm), lambda i, j, k: (k, i)) if ta else pl.BlockSpec((tm, tk), lambda i, j, k: (i, k))
    b_spec = pl.BlockSpec((tn, tk), lambda i, j, k: (j, k)) if tb else pl.BlockSpec((tk, tn), lambda i, j, k: (k, j))
    o_spec = pl.BlockSpec((tm, tn), lambda i, j, k: (i, j))
    in_specs = [a_spec, b_spec] + ([o_spec] if add is not None else [])
    args = (a, b) + ((add,) if add is not None else ())
    (out,), got = _carry_call(
        body, name, (M // tm, N // tn, nk), in_specs, [o_spec], [jax.ShapeDtypeStruct((M, N), out_dtype)],
        [pltpu.VMEM((tm, tn), F32)] if nk > 1 else [], ("parallel", "parallel", "arbitrary"), args, exs)
    return out if exs is None else (out, got)


def _cpow(ar, ai, n):
    rr, ri = None, None
    br, bi = ar, ai
    while n:
        if n & 1:
            if rr is None:
                rr, ri = br, bi
            else:
                rr, ri = rr * br - ri * bi, rr * bi + ri * br
        n >>= 1
        if n:
            br, bi = br * br - bi * bi, 2.0 * br * bi
    return rr, ri


def _seg_shift(v, k, reverse):
    sub = lax.broadcasted_iota(jnp.int32, v.shape, 0)
    if not reverse:
        return jnp.where(sub >= k, pltpu.roll(v, k, 0), 0.0)
    return jnp.where(sub < SUBLANES - k, pltpu.roll(v, SUBLANES - k, 0), 0.0)


def _steps(n, step, init, unroll):
    u = unroll if n % unroll == 0 else 1

    def trip(i, c):
        for s in range(u):
            c = step(i * u + s, c)
        return c

    return lax.fori_loop(0, n // u, trip, init)


def _ssm_scan(hre, him, ar, ai, seglen, reverse, tail=None, tail_init=()):
    w = hre.shape[1]
    zero = jnp.zeros((SUBLANES, w), F32)

    def rows(j):
        jj = (seglen - 1 - j) if reverse else j
        return pl.ds(pl.multiple_of(jj * SUBLANES, SUBLANES), SUBLANES)

    def local(j, c):
        hr, hi = c
        r = rows(j)
        nhr = ar * hr - ai * hi + hre[r, :]
        nhi = ar * hi + ai * hr + him[r, :]
        hre[r, :] = nhr
        him[r, :] = nhi
        return nhr, nhi

    er, ei = _steps(seglen, local, (zero, zero), 4 if reverse else 1)
    pr, pi_ = _cpow(ar, ai, seglen)
    for k in (1, 2, 4):
        sr, si = _seg_shift(er, k, reverse), _seg_shift(ei, k, reverse)
        er, ei = er + pr * sr - pi_ * si, ei + pr * si + pi_ * sr
        pr, pi_ = pr * pr - pi_ * pi_, 2.0 * pr * pi_
    cr, ci = _seg_shift(er, 1, reverse), _seg_shift(ei, 1, reverse)

    def carry_in(j, c):
        tr, ti = c[0] * ar - c[1] * ai, c[0] * ai + c[1] * ar
        r = rows(j)
        fr = hre[r, :] + tr
        fi = him[r, :] + ti
        hre[r, :] = fr
        him[r, :] = fi
        if tail is None:
            return tr, ti
        return (tr, ti) + tuple(tail(j, fr, fi, c[2:]))

    out = _steps(seglen, carry_in, (cr, ci) + tuple(tail_init), 4)
    return out[2:]


SSM_CB = 128
SSM_SB = 256


def _ssm_specs(S, l):
    u_spec = pl.BlockSpec((S, SSM_CB), lambda g, h: (0, g))
    bb_spec = pl.BlockSpec((1, 1, SSM_CB, SSM_SB), lambda g, h: (l, g, 0, h))
    a_spec = pl.BlockSpec((1, 1, 1, SSM_SB), lambda g, h: (l, g, 0, h))
    c_spec = pl.BlockSpec((1, 1, SSM_SB, SSM_CB), lambda g, h: (l, g, h, 0))
    d_spec = pl.BlockSpec((1, 1, SSM_CB), lambda g, h: (l, 0, g))
    return u_spec, bb_spec, a_spec, c_spec, d_spec


def _ssm_fwd(u, mats, l):
    S = u.shape[0]
    seglen = S // SUBLANES
    ch = min(512, S)
    nch = S // ch

    def body(u_ref, bbre_ref, bbim_ref, are_ref, aim_ref, cre_ref, cim_ref, d_ref, y_ref, hre, him):
        hf = pl.program_id(1)
        wre = bbre_ref[0, 0].astype(BF)
        wim = bbim_ref[0, 0].astype(BF)

        def mk(c, _):
            r = pl.ds(pl.multiple_of(c * ch, ch), ch)
            ub = u_ref[r, :].astype(BF)
            hre[r, :] = _dot(ub, wre)
            him[r, :] = _dot(ub, wim)
            return 0

        lax.fori_loop(0, nch, mk, 0)
        ar = jnp.broadcast_to(are_ref[0, 0], (SUBLANES, SSM_SB))
        ai = jnp.broadcast_to(aim_ref[0, 0], (SUBLANES, SSM_SB))
        _ssm_scan(hre, him, ar, ai, seglen, False)
        cr = cre_ref[0, 0].astype(BF)
        ci = cim_ref[0, 0].astype(BF)

        def out(c, _):
            r = pl.ds(pl.multiple_of(c * ch, ch), ch)
            y = _dot(hre[r, :].astype(BF), cr) - _dot(him[r, :].astype(BF), ci)

            @pl.when(hf == 0)
            def _():
                y_ref[r, :] = y + d_ref[0] * u_ref[r, :].astype(F32)

            @pl.when(hf != 0)
            def _():
                y_ref[r, :] = y_ref[r, :] + y

            return 0

        lax.fori_loop(0, nch, out, 0)

    u_spec, bb_spec, a_spec, c_spec, d_spec = _ssm_specs(S, l)
    return pl.pallas_call(
        body, name="ssm_fwd", grid=(SSM_WIDTH // SSM_CB, 2),
        in_specs=[u_spec, bb_spec, bb_spec, a_spec, a_spec, c_spec, c_spec, d_spec], out_specs=u_spec,
        out_shape=jax.ShapeDtypeStruct((S, SSM_WIDTH), F32),
        scratch_shapes=[pltpu.VMEM((S, SSM_SB), F32), pltpu.VMEM((S, SSM_SB), F32)],
        compiler_params=_cparams(("parallel", "arbitrary")),
    )(u, *mats)


def _ssm_bwd(u, dy, mats, l, exs=()):
    S = u.shape[0]
    seglen = S // SUBLANES
    ch = min(512, S)
    nch = S // ch
    nblk = SSM_WIDTH // SSM_CB

    def body(u_ref, dy_ref, bbre_ref, bbim_ref, are_ref, aim_ref, cre_ref, cim_ref, d_ref,
             du_ref, dbbre_ref, dbbim_ref, dare_ref, daim_ref, dcre_ref, dcim_ref, dd_ref,
             hre, him, lre, lim):
        hf = pl.program_id(1)
        wre = bbre_ref[0, 0].astype(BF)
        wim = bbim_ref[0, 0].astype(BF)
        wre_t, wim_t = wre.T, wim.T
        cr_t = cre_ref[0, 0].astype(BF).T
        ci_t = cim_ref[0, 0].astype(BF).T

        def mk(c, _):
            r = pl.ds(pl.multiple_of(c * ch, ch), ch)
            ub = u_ref[r, :].astype(BF)
            hre[r, :] = _dot(ub, wre)
            him[r, :] = _dot(ub, wim)
            return 0

        lax.fori_loop(0, nch, mk, 0)
        ar = jnp.broadcast_to(are_ref[0, 0], (SUBLANES, SSM_SB))
        ai = jnp.broadcast_to(aim_ref[0, 0], (SUBLANES, SSM_SB))
        _ssm_scan(hre, him, ar, ai, seglen, False)

        dcre_ref[...] = jnp.zeros_like(dcre_ref)
        dcim_ref[...] = jnp.zeros_like(dcim_ref)

        @pl.when(hf == 0)
        def _():
            dd_ref[...] = jnp.zeros_like(dd_ref)

        def cot(c, _):
            r = pl.ds(pl.multiple_of(c * ch, ch), ch)
            dyv = dy_ref[r, :]
            dyb = dyv.astype(BF)
            lre[r, :] = _dot(dyb, cr_t)
            lim[r, :] = -_dot(dyb, ci_t)
            dcre_ref[0] = dcre_ref[0] + _dot_tn(dyb, hre[r, :].astype(BF))
            dcim_ref[0] = dcim_ref[0] - _dot_tn(dyb, him[r, :].astype(BF))

            @pl.when(hf == 0)
            def _():
                dd_ref[...] = dd_ref[...] + jnp.sum(dyv * u_ref[r, :].astype(F32), axis=0, keepdims=True)

            return 0

        lax.fori_loop(0, nch, cot, 0)

        last = pl.ds((seglen - 1) * SUBLANES, SUBLANES)
        pr0 = _seg_shift(hre[last, :], 1, False)
        pi0 = _seg_shift(him[last, :], 1, False)

        def da(j, lr, li, c):
            acr, aci = c
            jp = jnp.maximum(seglen - 2 - j, 0)
            rp = pl.ds(pl.multiple_of(jp * SUBLANES, SUBLANES), SUBLANES)
            inner = j < seglen - 1
            pr = jnp.where(inner, hre[rp, :], pr0)
            pi_ = jnp.where(inner, him[rp, :], pi0)
            return acr + lr * pr + li * pi_, aci + li * pr - lr * pi_

        zero = jnp.zeros((SUBLANES, SSM_SB), F32)
        acr, aci = _ssm_scan(lre, lim, ar, -ai, seglen, True, tail=da, tail_init=(zero, zero))
        dare_ref[0] = jnp.sum(acr, axis=0, keepdims=True)
        daim_ref[0] = jnp.sum(aci, axis=0, keepdims=True)

        dbbre_ref[...] = jnp.zeros_like(dbbre_ref)
        dbbim_ref[...] = jnp.zeros_like(dbbim_ref)

        def fin(c, _):
            r = pl.ds(pl.multiple_of(c * ch, ch), ch)
            lrb = lre[r, :].astype(BF)
            lib = lim[r, :].astype(BF)
            ub = u_ref[r, :].astype(BF)
            du = _dot(lrb, wre_t) + _dot(lib, wim_t)
            dbbre_ref[0] = dbbre_ref[0] + _dot_tn(ub, lrb)
            dbbim_ref[0] = dbbim_ref[0] + _dot_tn(ub, lib)

            @pl.when(hf == 0)
            def _():
                du_ref[r, :] = du + d_ref[0] * dy_ref[r, :]

            @pl.when(hf != 0)
            def _():
                du_ref[r, :] = du_ref[r, :] + du

            return 0

        lax.fori_loop(0, nch, fin, 0)

    u_spec, bb_spec, a_spec, c_spec, d_spec = _ssm_specs(S, l)
    dbb_spec = pl.BlockSpec((1, SSM_CB, SSM_SB), lambda g, h: (g, 0, h))
    da_spec = pl.BlockSpec((1, 1, SSM_SB), lambda g, h: (g, 0, h))
    dd_spec = pl.BlockSpec((1, SSM_CB), lambda g, h: (0, g))
    out_shape = (
        jax.ShapeDtypeStruct((S, SSM_WIDTH), F32),
        jax.ShapeDtypeStruct((nblk, SSM_CB, 2 * SSM_SB), F32), jax.ShapeDtypeStruct((nblk, SSM_CB, 2 * SSM_SB), F32),
        jax.ShapeDtypeStruct((nblk, 1, 2 * SSM_SB), F32), jax.ShapeDtypeStruct((nblk, 1, 2 * SSM_SB), F32),
        jax.ShapeDtypeStruct((nblk, SSM_CB, 2 * SSM_SB), F32), jax.ShapeDtypeStruct((nblk, SSM_CB, 2 * SSM_SB), F32),
        jax.ShapeDtypeStruct((1, SSM_WIDTH), F32),
    )
    return _carry_call(
        body, "ssm_bwd", (nblk, 2), [u_spec, u_spec, bb_spec, bb_spec, a_spec, a_spec, c_spec, c_spec, d_spec],
        (u_spec, dbb_spec, dbb_spec, da_spec, da_spec, dbb_spec, dbb_spec, dd_spec), out_shape,
        [pltpu.VMEM((S, SSM_SB), F32) for _ in range(4)], ("parallel", "arbitrary"), (u, dy) + tuple(mats), exs)


def _ssm_post_fwd(y_raw, proj, w_glu, b_glu, p_ssm):
    S = y_raw.shape[0]
    tr = min(T_ROWS, S)

    def body(y_ref, z_ref, wg_ref, bg_ref, p_ref, o_ref):
        g = _gelu(y_ref[...])
        t = _dot(g.astype(BF), wg_ref[...]) + bg_ref[0]
        glu = t[:, :SSM_WIDTH] * _sigmoid(t[:, SSM_WIDTH:])
        ys = glu * _silu(z_ref[...].astype(F32))
        o_ref[...] = _dot(ys.astype(BF), p_ref[...]).astype(o_ref.dtype)

    return pl.pallas_call(
        body, name="ssm_post_fwd", grid=(S // tr,),
        in_specs=[_rows(tr, 512), _rows(tr, 512, 1), w_glu.spec(), b_glu.spec(), p_ssm.spec()],
        out_specs=_rows(tr, 1024), out_shape=jax.ShapeDtypeStruct((S, D_MODEL), ACT),
        compiler_params=_cparams(("parallel",)),
    )(y_raw, proj, w_glu.arr, b_glu.arr, p_ssm.arr)


def _ssm_post_bwd(do, y_raw, proj, w_glu, b_glu, p_ssm):
    S = y_raw.shape[0]
    tr = min(T_ROWS_BWD_WIDE, S)

    def body(do_ref, y_ref, z_ref, wg_ref, bg_ref, p_ref, dy_ref, dz_ref, dwg_ref, dbg_ref, dp_ref):
        @pl.when(pl.program_id(0) == 0)
        def _():
            dwg_ref[...] = jnp.zeros_like(dwg_ref)
            dbg_ref[...] = jnp.zeros_like(dbg_ref)
            dp_ref[...] = jnp.zeros_like(dp_ref)

        y = y_ref[...]
        z = z_ref[...].astype(F32)
        g = _gelu(y)
        gb = g.astype(BF)
        t = _dot(gb, wg_ref[...]) + bg_ref[0]
        a = t[:, :SSM_WIDTH]
        sb = _sigmoid(t[:, SSM_WIDTH:])
        glu = a * sb
        ys = glu * _silu(z)
        dob = do_ref[...].astype(BF)
        dys = _dot_nt(dob, p_ref[...])
        dp_ref[...] += _dot_tn(ys.astype(BF), dob)
        dglu = dys * _silu(z)
        dz_ref[...] = (dys * glu * _dsilu(z)).astype(dz_ref.dtype)
        dt = jnp.concatenate([dglu * sb, dglu * a * sb * (1.0 - sb)], axis=1)
        dbg_ref[...] += jnp.sum(dt, axis=0, keepdims=True)
        dtb = dt.astype(BF)
        dg = _dot_nt(dtb, wg_ref[...])
        dwg_ref[...] += _dot_tn(gb, dtb)
        dy_ref[...] = dg * _dgelu(y)

    return pl.pallas_call(
        body, name="ssm_post_bwd", grid=(S // tr,),
        in_specs=[_rows(tr, 1024), _rows(tr, 512), _rows(tr, 512, 1), w_glu.spec(), b_glu.spec(), p_ssm.spec()],
        out_specs=(_rows(tr, 512), _rows(tr, 512), _full((512, 1024)), _full((1, 1024)), _full((512, 1024))),
        out_shape=(jax.ShapeDtypeStruct((S, 512), F32), jax.ShapeDtypeStruct((S, 512), BF),
                   jax.ShapeDtypeStruct((512, 1024), F32), jax.ShapeDtypeStruct((1, 1024), F32),
                   jax.ShapeDtypeStruct((512, 1024), F32)),
        compiler_params=_cparams(("arbitrary",)),
    )(do, y_raw, proj, w_glu.arr, b_glu.arr, p_ssm.arr)


def _rope(t, c, sa, sb):
    return t * c + pltpu.roll(t, LANES - 16, 1) * sa + pltpu.roll(t, 16, 1) * sb


def _rope_t(dy, c, sa, sb):
    return dy * c + pltpu.roll(dy * sa, 16, 1) + pltpu.roll(dy * sb, LANES - 16, 1)


def _rms(x, g):
    r = lax.rsqrt(jnp.mean(x * x, axis=-1, keepdims=True) + NORM_EPS)
    return x * r * g, r


def _mla_pre_fwd(proj, q_norm, kv_norm, wuq, wk, wv, tc, tsa, tsb):
    S = proj.shape[0]
    tr = min(T_ROWS, S)

    def body(cq_ref, ckv_ref, slot_ref, qn_ref, kn_ref, wuq_ref, wk_ref, wv_ref, c_ref, sa_ref, sb_ref,
             q_out, k_out, v_out, qt_out, kt_out, vt_out):
        c, sa, sb = c_ref[...], sa_ref[...], sb_ref[...]
        qn, _ = _rms(cq_ref[...].astype(F32), qn_ref[0])
        q = _dot(qn.astype(BF), wuq_ref[...]) * MLA_SCALE
        kn, _ = _rms(ckv_ref[...].astype(F32), kn_ref[0])
        knb = kn.astype(BF)
        kp = _dot(knb, wk_ref[...])
        v = _dot(knb, wv_ref[...]).astype(BF)
        v_out[...] = v
        vt_out[...] = v.T
        kr = _rope(slot_ref[...].astype(F32), c, sa, sb)
        for h in range(MLA_HEADS):
            cs = slice(h * LANES, (h + 1) * LANES)
            qh = _rope(q[:, cs], c, sa, sb).astype(BF)
            kh = (kp[:, cs] + kr).astype(BF)
            q_out[:, cs] = qh
            k_out[:, cs] = kh
            qt_out[cs, :] = qh.T
            kt_out[cs, :] = kh.T

    return pl.pallas_call(
        body, name="mla_pre_fwd", grid=(S // tr,),
        in_specs=[_rows(tr, 256, 4), _rows(tr, 128, 10), _rows(tr, 128, 11), q_norm.spec(), kv_norm.spec(),
                  wuq.spec(), _full((128, 1024)), _full((128, 512)),
                  _rows(tr, 128), _rows(tr, 128), _rows(tr, 128)],
        out_specs=(_rows(tr, 1024), _rows(tr, 1024), _rows(tr, 512), _cols(1024, tr), _cols(1024, tr), _cols(512, tr)),
        out_shape=(jax.ShapeDtypeStruct((S, 1024), BF), jax.ShapeDtypeStruct((S, 1024), BF),
                   jax.ShapeDtypeStruct((S, 512), BF), jax.ShapeDtypeStruct((1024, S), BF),
                   jax.ShapeDtypeStruct((1024, S), BF), jax.ShapeDtypeStruct((512, S), BF)),
        compiler_params=_cparams(("parallel",)),
    )(proj, proj, proj, q_norm.arr, kv_norm.arr, wuq.arr, wk, wv, tc, tsa, tsb)


def _mla_pre_bwd(dq, dk, dv, proj, q_norm, kv_norm, wuq, wk, wv, tc, tsa, tsb):
    S = proj.shape[0]
    tr = min(T_ROWS_BWD, S)

    def body(dq_ref, dk_ref, dv_ref, cq_ref, ckv_ref, qn_ref, kn_ref, wuq_ref, wk_ref, wv_ref, c_ref, sa_ref, sb_ref,
             dcq_ref, dckv_ref, dslot_ref, dwuq_ref, dwk_ref, dwv_ref, dqn_ref, dkn_ref, dqp):
        @pl.when(pl.program_id(0) == 0)
        def _():
            dwuq_ref[...] = jnp.zeros_like(dwuq_ref)
            dwk_ref[...] = jnp.zeros_like(dwk_ref)
            dwv_ref[...] = jnp.zeros_like(dwv_ref)
            dqn_ref[...] = jnp.zeros_like(dqn_ref)
            dkn_ref[...] = jnp.zeros_like(dkn_ref)

        c, sa, sb = c_ref[...], sa_ref[...], sb_ref[...]
        dkr = jnp.zeros((tr, LANES), F32)
        for h in range(MLA_HEADS):
            cs = slice(h * LANES, (h + 1) * LANES)
            dqp[:, cs] = (_rope_t(dq_ref[:, cs], c, sa, sb) * MLA_SCALE).astype(BF)
            dkr = dkr + dk_ref[:, cs]
        lane = lax.broadcasted_iota(jnp.int32, (tr, LANES), 1)
        in_rope = (lane >= MLA_NOPE) & (lane < MLA_NOPE + MLA_ROPE)
        dslot_ref[...] = jnp.where(in_rope, _rope_t(dkr, c, sa, sb), 0.0).astype(dslot_ref.dtype)

        cq = cq_ref[...].astype(F32)
        gq = qn_ref[0]
        qn, rq = _rms(cq, gq)
        dqpb = dqp[...]
        dwuq_ref[...] += _dot_tn(qn.astype(BF), dqpb)
        dqn = _dot_nt(dqpb, wuq_ref[...])
        dqn_ref[...] += jnp.sum(dqn * cq * rq, axis=0, keepdims=True)
        dyg = dqn * gq
        dcq_ref[...] = (rq * dyg - cq * (rq * rq * rq) * jnp.mean(dyg * cq, axis=-1, keepdims=True)).astype(dcq_ref.dtype)

        ckv = ckv_ref[...].astype(F32)
        gk = kn_ref[0]
        kn, rk = _rms(ckv, gk)
        knb = kn.astype(BF)
        dkb = dk_ref[...].astype(BF)
        dvb = dv_ref[...].astype(BF)
        dwk_ref[...] += _dot_tn(knb, dkb)
        dwv_ref[...] += _dot_tn(knb, dvb)
        dkn = _dot_nt(dkb, wk_ref[...]) + _dot_nt(dvb, wv_ref[...])
        dkn_ref[...] += jnp.sum(dkn * ckv * rk, axis=0, keepdims=True)
        dyk = dkn * gk
        dckv_ref[...] = (rk * dyk - ckv * (rk * rk * rk) * jnp.mean(dyk * ckv, axis=-1, keepdims=True)).astype(dckv_ref.dtype)

    return pl.pallas_call(
        body, name="mla_pre_bwd", grid=(S // tr,),
        in_specs=[_rows(tr, 1024), _rows(tr, 1024), _rows(tr, 512), _rows(tr, 256, 4), _rows(tr, 128, 10),
                  q_norm.spec(), kv_norm.spec(), wuq.spec(), _full((128, 1024)), _full((128, 512)),
                  _rows(tr, 128), _rows(tr, 128), _rows(tr, 128)],
        out_specs=(_rows(tr, 256), _rows(tr, 128), _rows(tr, 128), _full((256, 1024)), _full((128, 1024)),
                   _full((128, 512)), _full((1, 256)), _full((1, 128))),
        out_shape=(jax.ShapeDtypeStruct((S, 256), BF), jax.ShapeDtypeStruct((S, 128), BF),
                   jax.ShapeDtypeStruct((S, 128), BF), jax.ShapeDtypeStruct((256, 1024), F32),
                   jax.ShapeDtypeStruct((128, 1024), F32), jax.ShapeDtypeStruct((128, 512), F32),
                   jax.ShapeDtypeStruct((1, 256), F32), jax.ShapeDtypeStruct((1, 128), F32)),
        scratch_shapes=[pltpu.VMEM((tr, 1024), BF)],
        compiler_params=_cparams(("arbitrary",)),
    )(dq, dk, dv, proj, proj, q_norm.arr, kv_norm.arr, wuq.arr, wk, wv, tc, tsa, tsb)


ANY = pl.BlockSpec(memory_space=pl.ANY)
N_REL = N_DEV - 1


def _coords():
    return lax.axis_index("x"), lax.axis_index("y"), lax.axis_index("c")


def _sem_shapes(nbuf):
    return [pltpu.SemaphoreType.DMA((N_REL * nbuf,)), pltpu.SemaphoreType.DMA((N_REL * nbuf,)),
            pltpu.SemaphoreType.DMA((nbuf,))]


def _ag_plan(srcs, dsts, sems):
    send_sems, recv_sems, _ = sems
    plan = []
    for b, (src, dst) in enumerate(zip(srcs, dsts)):
        def slot(px, py, pc, dst=dst):
            return dst.at[4 * px + 2 * py + pc]

        def copy(k, blk, to, s=None, b=b, slot=slot):
            return pltpu.make_async_remote_copy(
                src_ref=slot(*blk) if s is None else s, dst_ref=slot(*blk), send_sem=send_sems.at[N_REL * b + k],
                recv_sem=recv_sems.at[N_REL * b + k], device_id=to, device_id_type=MESH)

        plan.append((b, src, slot, copy))
    return plan


def _ag_start(srcs, dsts, sems):
    x, y, c = _coords()
    chips = [(1 - x, y), (x, 1 - y), (1 - x, 1 - y)]
    for b, src, slot, copy in _ag_plan(srcs, dsts, sems):
        pltpu.make_async_copy(src, slot(x, y, c), sems[2].at[b]).start()
        copy(0, (x, y, c), (x, y, 1 - c), src).start()
        for j, chip in enumerate(chips):
            copy(1 + j, (x, y, c), (*chip, c), src).start()


def _ag_relay(srcs, dsts, sems):
    x, y, c = _coords()
    me, sibling = (x, y, c), (x, y, 1 - c)
    chips = [(1 - x, y), (x, 1 - y), (1 - x, 1 - y)]
    for b, src, slot, copy in _ag_plan(srcs, dsts, sems):
        for j, chip in enumerate(chips):
            copy(1 + j, (*chip, c), me).wait_recv()
            copy(4 + j, (*chip, c), sibling).start()


def _ag_finish(srcs, dsts, sems, relayed):
    x, y, c = _coords()
    me, sibling = (x, y, c), (x, y, 1 - c)
    chips = [(1 - x, y), (x, 1 - y), (1 - x, 1 - y)]
    if not relayed:
        _ag_relay(srcs, dsts, sems)
    plan = _ag_plan(srcs, dsts, sems)
    for b, src, slot, copy in plan:
        copy(0, sibling, me).wait_recv()
        for j, chip in enumerate(chips):
            copy(4 + j, (*chip, 1 - c), me).wait_recv()
        copy(0, me, sibling, src).wait_send()
        for j, chip in enumerate(chips):
            copy(1 + j, me, (*chip, c), src).wait_send()
            copy(4 + j, (*chip, c), sibling).wait_send()
        pltpu.make_async_copy(src, slot(*me), sems[2].at[b]).wait()


def _a2a_copies(srcs, dsts, sems):
    send_sems, recv_sems, local_sems = sems
    x, y, c = _coords()
    me = 4 * x + 2 * y + c
    local, remote = [], []
    for b, (src, dst) in enumerate(zip(srcs, dsts)):
        for rel in range(1, N_DEV):
            px = 1 - x if rel & 4 else x
            py = 1 - y if rel & 2 else y
            pc = 1 - c if rel & 1 else c
            remote.append(pltpu.make_async_remote_copy(
                src_ref=src.at[4 * px + 2 * py + pc], dst_ref=dst.at[me], send_sem=send_sems.at[N_REL * b + rel - 1],
                recv_sem=recv_sems.at[N_REL * b + rel - 1], device_id=(px, py, pc), device_id_type=MESH))
        local.append(pltpu.make_async_copy(src.at[me], dst.at[me], local_sems.at[b]))
    return local, remote


def _a2a_start(srcs, dsts, sems):
    local, remote = _a2a_copies(srcs, dsts, sems)
    for d in local + remote:
        d.start()


def _a2a_finish(srcs, dsts, sems):
    local, remote = _a2a_copies(srcs, dsts, sems)
    for d in remote + local:
        d.wait()


class _Exchange:
    def __init__(self, kind, srcs):
        self.kind, self.srcs = kind, list(srcs)
        self.n = len(self.srcs)

    def out_shapes(self):
        if self.kind == "ag":
            return [jax.ShapeDtypeStruct((N_DEV,) + s.shape, s.dtype) for s in self.srcs]
        return [jax.ShapeDtypeStruct(s.shape, s.dtype) for s in self.srcs]

    def start(self, src_refs, dst_refs, sems):
        (_ag_start if self.kind == "ag" else _a2a_start)(src_refs, dst_refs, sems)

    def relay(self, src_refs, dst_refs, sems):
        if self.kind == "ag":
            _ag_relay(src_refs, dst_refs, sems)

    def finish(self, src_refs, dst_refs, sems, relayed=False):
        if self.kind == "ag":
            _ag_finish(src_refs, dst_refs, sems, relayed)
        else:
            _a2a_finish(src_refs, dst_refs, sems)


def _carry_call(body, name, grid, in_specs, out_specs, out_shape, scratch, semantics, args, exs):
    in_specs, out_specs, out_shape, scratch = list(in_specs), list(out_specs), list(out_shape), list(scratch)
    if not exs:
        return pl.pallas_call(body, name=name, grid=grid, in_specs=in_specs, out_specs=out_specs, out_shape=out_shape,
                              scratch_shapes=scratch, compiler_params=_cparams(semantics))(*args), []
    n_in, n_out, n_scr = len(in_specs), len(out_specs), len(scratch)
    n_ex = sum(e.n for e in exs)

    def wrapped(*refs):
        ins, refs = refs[:n_in], refs[n_in:]
        srcs, refs = refs[:n_ex], refs[n_ex:]
        outs, refs = refs[:n_out], refs[n_out:]
        dsts, refs = refs[:n_ex], refs[n_ex:]
        scr, sems = refs[:n_scr], refs[n_scr:]
        views, off = [], 0
        for i, e in enumerate(exs):
            views.append((srcs[off:off + e.n], dsts[off:off + e.n], sems[3 * i:3 * i + 3]))
            off += e.n
        first = last = late = None
        for axis, size in enumerate(grid):
            at0, at1 = pl.program_id(axis) == 0, pl.program_id(axis) == size - 1
            first = at0 if first is None else first & at0
            last = at1 if last is None else last & at1
            late = at1 if late is None else late & at0
        relay_early = grid[0] > 1

        @pl.when(first)
        def _():
            for e, view in zip(exs, views):
                e.start(*view)

        if relay_early:
            @pl.when(late)
            def _():
                for e, view in zip(exs, views):
                    e.relay(*view)

        body(*ins, *outs, *scr)

        @pl.when(last)
        def _():
            for e, view in zip(exs, views):
                e.finish(*view, relayed=relay_early)

    res = pl.pallas_call(
        wrapped, name=name + "_x", grid=grid, in_specs=in_specs + [ANY] * n_ex, out_specs=out_specs + [ANY] * n_ex,
        out_shape=out_shape + [s for e in exs for s in e.out_shapes()],
        scratch_shapes=scratch + [s for e in exs for s in _sem_shapes(e.n)],
        compiler_params=_cparams(("arbitrary",) * len(grid)))(*args, *[s for e in exs for s in e.srcs])
    got, off = [], n_out
    for e in exs:
        got.append(list(res[off:off + e.n]))
        off += e.n
    return res[:n_out], got


def _exchange_call(name, exs):
    tot = sum(e.n for e in exs)

    def body(*refs):
        srcs, dsts, sems = refs[:tot], refs[tot:2 * tot], refs[2 * tot:]
        views, off = [], 0
        for i, e in enumerate(exs):
            views.append((srcs[off:off + e.n], dsts[off:off + e.n], sems[3 * i:3 * i + 3]))
            off += e.n
        for e, view in zip(exs, views):
            e.start(*view)
        for e, view in zip(exs, views):
            e.finish(*view)

    outs = pl.pallas_call(
        body, name=name, in_specs=[ANY] * tot, out_specs=[ANY] * tot,
        out_shape=[s for e in exs for s in e.out_shapes()],
        scratch_shapes=[s for e in exs for s in _sem_shapes(e.n)],
    )(*[s for e in exs for s in e.srcs])
    res, off = [], 0
    for e in exs:
        res.append(list(outs[off:off + e.n]))
        off += e.n
    return res


def _flash_call(body, name, exs, in_specs, out_specs, out_shape, scratch, n, args):
    return _carry_call(body, name, (MLA_HEADS // 2, n * (n + 1) // 2), in_specs, out_specs, out_shape, scratch,
                       ("parallel", "arbitrary"), args, exs)


def _tri_rows(s, n):
    at = [(s >= r * (r + 1) // 2).astype(jnp.int32) for r in range(1, n)]
    return sum(at), s - sum(a * r for a, r in zip(at, range(1, n)))


def _tri_cols(s, n):
    starts = [c * n - c * (c - 1) // 2 for c in range(n)]
    col = sum((s >= starts[c]).astype(jnp.int32) for c in range(1, n))
    start = sum(jnp.where(col == c, starts[c], 0) for c in range(n))
    return s - start + col, col


def _pair_rows(a):
    at = a.T
    return jnp.concatenate([at[0:1, :], at[MLA_V:MLA_V + 1, :], jnp.zeros((SUBLANES - 2, a.shape[0]), a.dtype)], axis=0)


def _lower_tri(t):
    return lax.broadcasted_iota(jnp.int32, (t, t), 0) >= lax.broadcasted_iota(jnp.int32, (t, t), 1)


def _upper_tri(t):
    return lax.broadcasted_iota(jnp.int32, (t, t), 1) >= lax.broadcasted_iota(jnp.int32, (t, t), 0)


def _flash_fwd(q, kt, v, exs=()):
    S = q.shape[0]
    t = min(T_ATT, S)
    n = S // t

    def body(q_ref, kt_ref, v_ref, o_ref, lse_ref, lse_t_ref, m_s, l_s, acc):
        qi, ki = _tri_rows(pl.program_id(1), n)
        lo = lax.broadcasted_iota(jnp.int32, (t, LANES), 1) < MLA_V

        @pl.when(ki == 0)
        def _():
            m_s[...] = jnp.full_like(m_s, NEG)
            l_s[...] = jnp.zeros_like(l_s)
            acc[...] = jnp.zeros_like(acc)

        keep = _lower_tri(t) | (ki < qi)
        vv = v_ref[...]
        heads = range(2)
        ss = [jnp.where(keep, _dot(q_ref[:, h * LANES:(h + 1) * LANES], kt_ref[h * LANES:(h + 1) * LANES, :]), NEG)
              for h in heads]
        m_prev = [m_s[h] for h in heads]
        l_prev = [l_s[h] for h in heads]
        m_new = [jnp.maximum(m_prev[h], jnp.max(ss[h], axis=1, keepdims=True)) for h in heads]
        al = [jnp.exp(m_prev[h] - m_new[h]) for h in heads]
        ps = [jnp.exp(ss[h] - m_new[h][:, :1]) for h in heads]
        l_new = [al[h] * l_prev[h] + jnp.sum(ps[h], axis=1, keepdims=True) for h in heads]
        pv = [_dot(ps[h].astype(BF), vv) for h in heads]
        for h in heads:
            m_s[h] = m_new[h]
            l_s[h] = l_new[h]
        acc[...] = jnp.where(lo, al[0], al[1]) * acc[...] + jnp.where(lo, pv[0], pv[1])

        @pl.when(ki == qi)
        def _():
            o_ref[...] = acc[...] / jnp.where(lo, l_s[0], l_s[1])
            lse = jnp.where(lo, m_s[0] + jnp.log(l_s[0]), m_s[1] + jnp.log(l_s[1]))
            lse_ref[0] = lse
            lse_t_ref[0] = _pair_rows(lse)

    return _flash_call(
        body, "mla_flash_fwd", exs,
        [pl.BlockSpec((t, 256), lambda p, s: (_tri_rows(s, n)[0], p)),
         pl.BlockSpec((256, t), lambda p, s: (p, _tri_rows(s, n)[1])),
         pl.BlockSpec((t, 128), lambda p, s: (_tri_rows(s, n)[1], p))],
        [pl.BlockSpec((t, 128), lambda p, s: (_tri_rows(s, n)[0], p)),
         pl.BlockSpec((1, t, 128), lambda p, s: (p, _tri_rows(s, n)[0], 0)),
         pl.BlockSpec((1, SUBLANES, t), lambda p, s: (p, 0, _tri_rows(s, n)[0]))],
        [jax.ShapeDtypeStruct((S, 512), F32), jax.ShapeDtypeStruct((MLA_HEADS // 2, S, 128), F32),
         jax.ShapeDtypeStruct((MLA_HEADS // 2, SUBLANES, S), F32)],
        [pltpu.VMEM((2, t, 128), F32), pltpu.VMEM((2, t, 128), F32), pltpu.VMEM((t, 128), F32)], n, (q, kt, v))


def _flash_bwd_dq(q, k, kt, vt, do, lse, delta, exs=()):
    S = q.shape[0]
    t = min(T_ATT, S)
    n = S // t

    def body(q_ref, k_ref, kt_ref, vt_ref, do_ref, lse_ref, dl_ref, dq_ref, acc):
        qi, ki = _tri_rows(pl.program_id(1), n)
        lo = lax.broadcasted_iota(jnp.int32, (t, LANES), 1) < MLA_V

        @pl.when(ki == 0)
        def _():
            acc[...] = jnp.zeros_like(acc)

        keep = _lower_tri(t) | (ki < qi)
        heads = range(2)
        cs = [slice(h * LANES, (h + 1) * LANES) for h in heads]
        col = [slice(h * MLA_V, h * MLA_V + 1) for h in heads]
        lse, dl, dov, vt = lse_ref[0], dl_ref[0], do_ref[...], vt_ref[...]
        ss = [jnp.where(keep, _dot(q_ref[:, cs[h]], kt_ref[cs[h], :]), NEG) for h in heads]
        dp = [_dot(jnp.where(lo if h == 0 else ~lo, dov, 0).astype(BF), vt) for h in heads]
        ds = [(jnp.exp(ss[h] - lse[:, col[h]]) * (dp[h] - dl[:, col[h]])).astype(BF) for h in heads]
        dq = [_dot(ds[h], k_ref[:, cs[h]]) for h in heads]
        acc[...] += jnp.concatenate(dq, axis=1)

        @pl.when(ki == qi)
        def _():
            dq_ref[...] = acc[...]

    (dq,), got = _flash_call(
        body, "mla_flash_dq", exs,
        [pl.BlockSpec((t, 256), lambda p, s: (_tri_rows(s, n)[0], p)),
         pl.BlockSpec((t, 256), lambda p, s: (_tri_rows(s, n)[1], p)),
         pl.BlockSpec((256, t), lambda p, s: (p, _tri_rows(s, n)[1])),
         pl.BlockSpec((128, t), lambda p, s: (p, _tri_rows(s, n)[1])),
         pl.BlockSpec((t, 128), lambda p, s: (_tri_rows(s, n)[0], p)),
         pl.BlockSpec((1, t, 128), lambda p, s: (p, _tri_rows(s, n)[0], 0)),
         pl.BlockSpec((1, t, 128), lambda p, s: (p, _tri_rows(s, n)[0], 0))],
        [pl.BlockSpec((t, 256), lambda p, s: (_tri_rows(s, n)[0], p))],
        [jax.ShapeDtypeStruct((S, 1024), F32)],
        [pltpu.VMEM((t, 256), F32)], n, (q, k, kt, vt, do, lse, delta))
    return dq, got


def _flash_bwd_dkv(q, qt, k, v, do, dot_, lse_t, delta_t, exs=()):
    S = q.shape[0]
    t = min(T_ATT, S)
    n = S // t

    def body(q_ref, qt_ref, k_ref, v_ref, do_ref, dot_ref, lse_ref, dl_ref, dk_ref, dv_ref, dk_acc, dv_acc):
        qi, ki = _tri_cols(pl.program_id(1), n)
        lo = lax.broadcasted_iota(jnp.int32, (t, LANES), 1) < MLA_V
        top = lax.broadcasted_iota(jnp.int32, (LANES, t), 0) < MLA_V

        @pl.when(qi == ki)
        def _():
            dk_acc[...] = jnp.zeros_like(dk_acc)
            dv_acc[...] = jnp.zeros_like(dv_acc)

        keep = _upper_tri(t) | (qi > ki)
        heads = range(2)
        cs = [slice(h * LANES, (h + 1) * LANES) for h in heads]
        vv, lse, dl, dov, dot_v = v_ref[...], lse_ref[0], dl_ref[0], do_ref[...], dot_ref[...]
        st = [jnp.where(keep, _dot(k_ref[:, cs[h]], qt_ref[cs[h], :]), NEG) for h in heads]
        dpt = [_dot(vv, jnp.where(top if h == 0 else ~top, dot_v, 0).astype(BF)) for h in heads]
        pt = [jnp.exp(st[h] - lse[h:h + 1, :]) for h in heads]
        dst = [(pt[h] * (dpt[h] - dl[h:h + 1, :])).astype(BF) for h in heads]
        dv = [_dot(pt[h].astype(BF), jnp.where(lo if h == 0 else ~lo, dov, 0).astype(BF)) for h in heads]
        dk = [_dot(dst[h], q_ref[:, cs[h]]) for h in heads]
        dv_acc[...] += dv[0] + dv[1]
        dk_acc[...] += jnp.concatenate(dk, axis=1)

        @pl.when(qi == n - 1)
        def _():
            dk_ref[...] = dk_acc[...]
            dv_ref[...] = dv_acc[...]

    (dk, dv), got = _flash_call(
        body, "mla_flash_dkv", exs,
        [pl.BlockSpec((t, 256), lambda p, s: (_tri_cols(s, n)[0], p)),
         pl.BlockSpec((256, t), lambda p, s: (p, _tri_cols(s, n)[0])),
         pl.BlockSpec((t, 256), lambda p, s: (_tri_cols(s, n)[1], p)),
         pl.BlockSpec((t, 128), lambda p, s: (_tri_cols(s, n)[1], p)),
         pl.BlockSpec((t, 128), lambda p, s: (_tri_cols(s, n)[0], p)),
         pl.BlockSpec((128, t), lambda p, s: (p, _tri_cols(s, n)[0])),
         pl.BlockSpec((1, SUBLANES, t), lambda p, s: (p, 0, _tri_cols(s, n)[0])),
         pl.BlockSpec((1, SUBLANES, t), lambda p, s: (p, 0, _tri_cols(s, n)[0]))],
        [pl.BlockSpec((t, 256), lambda p, s: (_tri_cols(s, n)[1], p)),
         pl.BlockSpec((t, 128), lambda p, s: (_tri_cols(s, n)[1], p))],
        [jax.ShapeDtypeStruct((S, 1024), F32), jax.ShapeDtypeStruct((S, 512), F32)],
        [pltpu.VMEM((t, 256), F32), pltpu.VMEM((t, 128), F32)], n, (q, qt, k, v, do, dot_, lse_t, delta_t))
    return dk, dv, got


def _mem_heads(qm, km_ref, vm_ref):
    ps, os_ = [], []
    for h in range(X_HEADS):
        cs = slice(h * X_HEAD_DIM, (h + 1) * X_HEAD_DIM)
        s = _dot_nt(qm[:, cs].astype(BF), km_ref[:, cs]) * MEM_SCALE
        e = jnp.exp(s - jnp.max(s, axis=1, keepdims=True))
        p = e / jnp.sum(e, axis=1, keepdims=True)
        ps.append(p)
        os_.append(_dot(p.astype(BF), vm_ref[:, cs]))
    return ps, jnp.concatenate(os_, axis=1)


def _mem_fwd(proj, km, vm, p_mem):
    S = proj.shape[0]
    tr = min(T_ROWS, S)
    M = km.shape[0]

    def body(q_ref, z_ref, km_ref, vm_ref, p_ref, o_ref):
        _, o = _mem_heads(q_ref[...], km_ref, vm_ref)
        y = o * _silu(z_ref[...].astype(F32))
        o_ref[...] = _dot(y.astype(BF), p_ref[...]).astype(o_ref.dtype)

    return pl.pallas_call(
        body, name="mem_fwd", grid=(S // tr,),
        in_specs=[_rows(tr, 512, 4), _rows(tr, 512, 5), _full((M, 512)), _full((M, 512)), p_mem.spec()],
        out_specs=_rows(tr, 1024), out_shape=jax.ShapeDtypeStruct((S, D_MODEL), ACT),
        compiler_params=_cparams(("parallel",)),
    )(proj, proj, km, vm, p_mem.arr)


def _mem_bwd(do, proj, km, vm, p_mem):
    S = proj.shape[0]
    tr = min(T_ROWS_BWD_WIDE, S)
    M = km.shape[0]

    def body(do_ref, q_ref, z_ref, km_ref, vm_ref, p_ref, dq_ref, dz_ref, dkm_ref, dvm_ref, dp_ref):
        @pl.when(pl.program_id(0) == 0)
        def _():
            dkm_ref[...] = jnp.zeros_like(dkm_ref)
            dvm_ref[...] = jnp.zeros_like(dvm_ref)
            dp_ref[...] = jnp.zeros_like(dp_ref)

        qm = q_ref[...]
        z = z_ref[...].astype(F32)
        ps, o = _mem_heads(qm, km_ref, vm_ref)
        sz = _silu(z)
        y = o * sz
        dob = do_ref[...].astype(BF)
        dy = _dot_nt(dob, p_ref[...])
        dp_ref[...] += _dot_tn(y.astype(BF), dob)
        dz_ref[...] = (dy * o * _dsilu(z)).astype(dz_ref.dtype)
        d_o = dy * sz
        for h in range(X_HEADS):
            cs = slice(h * X_HEAD_DIM, (h + 1) * X_HEAD_DIM)
            doh = d_o[:, cs]
            dohb = doh.astype(BF)
            p = ps[h]
            dpr = _dot_nt(dohb, vm_ref[:, cs])
            ds = (p * (dpr - jnp.sum(doh * o[:, cs], axis=1, keepdims=True)) * MEM_SCALE).astype(BF)
            dq_ref[:, cs] = _dot(ds, km_ref[:, cs]).astype(dq_ref.dtype)
            dkm_ref[:, cs] += _dot_tn(ds, qm[:, cs].astype(BF))
            dvm_ref[:, cs] += _dot_tn(p.astype(BF), dohb)

    return pl.pallas_call(
        body, name="mem_bwd", grid=(S // tr,),
        in_specs=[_rows(tr, 1024), _rows(tr, 512, 4), _rows(tr, 512, 5), _full((M, 512)), _full((M, 512)),
                  p_mem.spec()],
        out_specs=(_rows(tr, 512), _rows(tr, 512), _full((M, 512)), _full((M, 512)), _full((512, 1024))),
        out_shape=(jax.ShapeDtypeStruct((S, 512), BF), jax.ShapeDtypeStruct((S, 512), BF),
                   jax.ShapeDtypeStruct((M, 512), F32), jax.ShapeDtypeStruct((M, 512), F32),
                   jax.ShapeDtypeStruct((512, 1024), F32)),
        compiler_params=_cparams(("arbitrary",)),
    )(do, proj, proj, km, vm, p_mem.arr)


def _merge_fwd(x, proj, o_ssm, o_att, o_mem, b_gate, p_mla, w_out, ln_g, ln_b):
    S = x.shape[0]
    tr = min(T_ROWS, S)

    def body(x_ref, lg_ref, z_ref, os_ref, oa_ref, om_ref, bg_ref, p_ref, w_ref, g_ref, b_ref,
             xn_ref, xb_ref, pre_ref, mg_ref):
        gates = _sigmoid(lg_ref[...].astype(F32) + bg_ref[0])
        ya = oa_ref[...] * _silu(z_ref[...].astype(F32))
        o_mla = _dot(ya.astype(BF), p_ref[...])
        merged = (gates[:, :D_MODEL] * os_ref[...].astype(F32) + gates[:, D_MODEL:2 * D_MODEL] * o_mla
                  + gates[:, 2 * D_MODEL:] * om_ref[...].astype(F32))
        mb = merged.astype(BF)
        mg_ref[...] = mb
        pre = ALPHA * x_ref[...] + _dot(mb, w_ref[...])
        pre_ref[...] = pre
        mu = jnp.mean(pre, axis=-1, keepdims=True)
        xc = pre - mu
        var = jnp.mean(xc * xc, axis=-1, keepdims=True)
        xn = xc * lax.rsqrt(var + NORM_EPS) * g_ref[0] + b_ref[0]
        xn_ref[...] = xn
        xb_ref[...] = xn.astype(BF)

    return pl.pallas_call(
        body, name="merge_fwd", grid=(S // tr,),
        in_specs=[_rows(tr, 1024), _rows(tr, 3072, 1), _rows(tr, 512, 3), _rows(tr, 1024), _rows(tr, 512),
                  _rows(tr, 1024), b_gate.spec(), p_mla.spec(), _full((1024, 1024)), ln_g.spec(), ln_b.spec()],
        out_specs=(_rows(tr, 1024), _rows(tr, 1024), _rows(tr, 1024), _rows(tr, 1024)),
        out_shape=(jax.ShapeDtypeStruct((S, 1024), F32), jax.ShapeDtypeStruct((S, 1024), BF),
                   jax.ShapeDtypeStruct((S, 1024), F32), jax.ShapeDtypeStruct((S, 1024), BF)),
        compiler_params=_cparams(("parallel",)),
    )(x, proj, proj, o_ssm, o_att, o_mem, b_gate.arr, p_mla.arr, w_out, ln_g.arr, ln_b.arr)


def _merge_bwd(dxn, pre, merged, proj, o_ssm, o_att, o_mem, b_gate, p_mla, w_out, ln_g):
    S = pre.shape[0]
    tr = min(T_ROWS_BWD, S)

    def body(dxn_ref, pre_ref, mg_ref, lg_ref, z_ref, os_ref, oa_ref, om_ref, bg_ref, p_ref, w_ref, g_ref,
             dxr_ref, dlg_ref, dos_ref, dom_ref, doa_ref, dz_ref, doat_ref, dl_ref, dlt_ref, dw_ref, dp_ref, dbg_ref,
             dg_ref, db_ref):
        @pl.when(pl.program_id(0) == 0)
        def _():
            dw_ref[...] = jnp.zeros_like(dw_ref)
            dp_ref[...] = jnp.zeros_like(dp_ref)
            dbg_ref[...] = jnp.zeros_like(dbg_ref)
            dg_ref[...] = jnp.zeros_like(dg_ref)
            db_ref[...] = jnp.zeros_like(db_ref)

        dxn = dxn_ref[...]
        pre = pre_ref[...]
        mu = jnp.mean(pre, axis=-1, keepdims=True)
        xc = pre - mu
        rstd = lax.rsqrt(jnp.mean(xc * xc, axis=-1, keepdims=True) + NORM_EPS)
        xhat = xc * rstd
        dg_ref[...] += jnp.sum(dxn * xhat, axis=0, keepdims=True)
        db_ref[...] += jnp.sum(dxn, axis=0, keepdims=True)
        dxh = dxn * g_ref[0]
        dpre = rstd * (dxh - jnp.mean(dxh, axis=-1, keepdims=True)
                       - xhat * jnp.mean(dxh * xhat, axis=-1, keepdims=True))
        dxr_ref[...] = ALPHA * dpre
        dpb = dpre.astype(BF)
        dw_ref[...] += _dot_tn(mg_ref[...], dpb)
        dm = _dot_nt(dpb, w_ref[...])

        gates = _sigmoid(lg_ref[...].astype(F32) + bg_ref[0])
        g0, g1, g2 = gates[:, :D_MODEL], gates[:, D_MODEL:2 * D_MODEL], gates[:, 2 * D_MODEL:]
        z = z_ref[...].astype(F32)
        oa = oa_ref[...]
        sz = _silu(z)
        ya = (oa * sz).astype(BF)
        o_mla = _dot(ya, p_ref[...])
        dos_ref[...] = (g0 * dm).astype(dos_ref.dtype)
        dom_ref[...] = (g2 * dm).astype(dom_ref.dtype)
        do_mla = (g1 * dm).astype(BF)
        dl0 = dm * os_ref[...].astype(F32) * g0 * (1.0 - g0)
        dl1 = dm * o_mla * g1 * (1.0 - g1)
        dl2 = dm * om_ref[...].astype(F32) * g2 * (1.0 - g2)
        dl = jnp.concatenate([dl0, dl1, dl2], axis=1)
        dbg_ref[...] += jnp.sum(dl, axis=0, keepdims=True)
        dlg_ref[...] = dl.astype(dlg_ref.dtype)
        dp_ref[...] += _dot_tn(ya, do_mla)
        dya = _dot_nt(do_mla, p_ref[...])
        doa = dya * sz
        doab = doa.astype(BF)
        doa_ref[...] = doab
        doat_ref[...] = doab.T
        dz_ref[...] = (dya * oa * _dsilu(z)).astype(dz_ref.dtype)
        prod = doa * oa
        lo = lax.broadcasted_iota(jnp.int32, (tr, LANES), 1) < MLA_V
        for pr in range(MLA_HEADS // 2):
            blk = prod[:, pr * LANES:(pr + 1) * LANES]
            d0 = jnp.sum(jnp.where(lo, blk, 0.0), axis=1, keepdims=True)
            d1 = jnp.sum(jnp.where(lo, 0.0, blk), axis=1, keepdims=True)
            dl = jnp.where(lo, d0, d1)
            dl_ref[pr] = dl
            dlt_ref[pr] = _pair_rows(dl)

    return pl.pallas_call(
        body, name="merge_bwd", grid=(S // tr,),
        in_specs=[_rows(tr, 1024), _rows(tr, 1024), _rows(tr, 1024), _rows(tr, 3072, 1), _rows(tr, 512, 3),
                  _rows(tr, 1024), _rows(tr, 512), _rows(tr, 1024), b_gate.spec(), p_mla.spec(),
                  _full((1024, 1024)), ln_g.spec()],
        out_specs=(_rows(tr, 1024), _rows(tr, 3072), _rows(tr, 1024), _rows(tr, 1024), _rows(tr, 512),
                   _rows(tr, 512), _cols(512, tr), pl.BlockSpec((MLA_HEADS // 2, tr, 128), lambda i: (0, i, 0)),
                   pl.BlockSpec((MLA_HEADS // 2, SUBLANES, tr), lambda i: (0, 0, i)),
                   _full((1024, 1024)), _full((512, 1024)), _full((1, 3072)), _full((1, 1024)), _full((1, 1024))),
        out_shape=(jax.ShapeDtypeStruct((S, 1024), F32), jax.ShapeDtypeStruct((S, 3072), BF),
                   jax.ShapeDtypeStruct((S, 1024), BF), jax.ShapeDtypeStruct((S, 1024), BF),
                   jax.ShapeDtypeStruct((S, 512), BF), jax.ShapeDtypeStruct((S, 512), BF),
                   jax.ShapeDtypeStruct((512, S), BF), jax.ShapeDtypeStruct((MLA_HEADS // 2, S, 128), F32),
                   jax.ShapeDtypeStruct((MLA_HEADS // 2, SUBLANES, S), F32),
                   jax.ShapeDtypeStruct((1024, 1024), F32), jax.ShapeDtypeStruct((512, 1024), F32),
                   jax.ShapeDtypeStruct((1, 3072), F32), jax.ShapeDtypeStruct((1, 1024), F32),
                   jax.ShapeDtypeStruct((1, 1024), F32)),
        compiler_params=_cparams(("arbitrary",)),
    )(dxn, pre, merged, proj, proj, o_ssm, o_att, o_mem, b_gate.arr, p_mla.arr, w_out, ln_g.arr)


def _loss_head(y, t):
    S = y.shape[0]
    tr = min(T_ROWS, S)
    n = S // tr

    def body(y_ref, t_ref, dy_ref, l_ref, acc):
        i = pl.program_id(0)

        @pl.when(i == 0)
        def _():
            acc[...] = jnp.zeros_like(acc)

        e = y_ref[...] - t_ref[...]
        dy_ref[...] = e * (1.0 / D_MODEL)
        acc[...] += jnp.sum(e * e, axis=0, keepdims=True)

        @pl.when(i == n - 1)
        def _():
            tot = jnp.sum(acc[...], axis=1, keepdims=True) * (0.5 / D_MODEL)
            l_ref[...] = jnp.broadcast_to(tot, l_ref.shape)

    return pl.pallas_call(
        body, name="loss_head", grid=(n,),
        in_specs=[_rows(tr, 1024), _rows(tr, 1024)],
        out_specs=(_rows(tr, 1024), _full((SUBLANES, LANES))),
        out_shape=(jax.ShapeDtypeStruct((S, 1024), F32), jax.ShapeDtypeStruct((SUBLANES, LANES), F32)),
        scratch_shapes=[pltpu.VMEM((1, 1024), F32)],
        compiler_params=_cparams(("arbitrary",)),
    )(y, t)


def _rope_tables(pos):
    inv_freq = ROPE_THETA ** (-jnp.arange(0, MLA_ROPE, 2, dtype=F32) / MLA_ROPE)
    ang = pos.astype(F32)[:, None] * inv_freq
    cos, sin = jnp.cos(ang), jnp.sin(ang)
    S = pos.shape[0]
    half = MLA_ROPE // 2
    ones = jnp.ones((S, MLA_NOPE), F32)
    z16 = jnp.zeros((S, half), F32)
    z32 = jnp.zeros((S, LANES - MLA_NOPE - MLA_ROPE), F32)
    z64 = jnp.zeros((S, MLA_NOPE), F32)
    c = jnp.concatenate([ones, cos, cos, z32], axis=1)
    sa = jnp.concatenate([z64, -sin, z16, z32], axis=1)
    sb = jnp.concatenate([z64, z16, sin, z32], axis=1)
    return c, sa, sb


def _ssm_discretise(a_re, a_im, log_dt, b_re, b_im):
    dt = jnp.exp(log_dt)[..., None]
    mag = jnp.exp(a_re * dt)
    lb_re = mag * jnp.cos(a_im * dt)
    lb_im = mag * jnp.sin(a_im * dt)
    nr, ni = lb_re - 1.0, lb_im
    den = a_re * a_re + a_im * a_im
    f_re = (nr * a_re + ni * a_im) / den
    f_im = (ni * a_re - nr * a_im) / den
    bb_re = f_re[..., None] * b_re - f_im[..., None] * b_im
    bb_im = f_re[..., None] * b_im + f_im[..., None] * b_re
    return lb_re, lb_im, bb_re, bb_im


_GPB = SSM_CB // SSM_GROUP


def _bd_in(bb):
    nb = SSM_GROUPS // _GPB
    t = bb.reshape(nb, _GPB, SSM_STATE, SSM_GROUP)
    eye = jnp.eye(_GPB, dtype=bb.dtype)
    return jnp.einsum("ngpc,gh->ngchp", t, eye).reshape(nb, SSM_CB, _GPB * SSM_STATE)


def _bd_in_t(d):
    nb = SSM_GROUPS // _GPB
    t = d.reshape(nb, _GPB, SSM_GROUP, _GPB, SSM_STATE)
    eye = jnp.eye(_GPB, dtype=d.dtype)
    return jnp.einsum("ngchp,gh->ngpc", t, eye).reshape(SSM_GROUPS, SSM_STATE, SSM_GROUP)


def _bd_out(c):
    nb = SSM_GROUPS // _GPB
    t = c.reshape(nb, _GPB, SSM_GROUP, SSM_STATE)
    eye = jnp.eye(_GPB, dtype=c.dtype)
    return jnp.einsum("ngcp,gh->ngphc", t, eye).reshape(nb, _GPB * SSM_STATE, SSM_CB)


def _interleave(a):
    S, w = a.shape
    return a.reshape(SUBLANES, S // SUBLANES, w).transpose(1, 0, 2).reshape(S, w)


def _deinterleave(a):
    S, w = a.shape
    return a.reshape(S // SUBLANES, SUBLANES, w).transpose(1, 0, 2).reshape(S, w)


IN_SHARD = D_IN // N_DEV
ROPE_OWNER = ROPE_SLOT_LO // IN_SHARD
assert ROPE_OWNER * IN_SHARD <= ROPE_SLOT_LO and ROPE_SLOT_LO + MLA_ROPE <= (ROPE_OWNER + 1) * IN_SHARD


def _w_in_from_shards(g):
    pieces = []
    for j in range(N_DEV):
        if j == ROPE_OWNER:
            a = ROPE_SLOT_LO - j * IN_SHARD
            z = lambda n: jnp.zeros((g.shape[1], n), g.dtype)
            pieces += [g[j][:, :a], z(MLA_NOPE), g[j][:, a:a + MLA_ROPE], z(LANES - MLA_NOPE - MLA_ROPE),
                       g[j][:, a + MLA_ROPE:]]
        else:
            pieces.append(g[j])
    return jnp.concatenate(pieces, axis=1)


def _w_in_to_shards(d):
    shift = LANES - MLA_ROPE
    out = []
    for j in range(N_DEV):
        lo, hi = j * IN_SHARD, (j + 1) * IN_SHARD
        if j < ROPE_OWNER:
            out.append(d[:, lo:hi])
        elif j > ROPE_OWNER:
            out.append(d[:, lo + shift:hi + shift])
        else:
            r = ROPE_SLOT_LO + MLA_NOPE
            out.append(jnp.concatenate([d[:, lo:ROPE_SLOT_LO], d[:, r:r + MLA_ROPE],
                                        d[:, ROPE_SLOT_LO + LANES:hi + shift]], axis=1))
    return jnp.stack(out)


def _adamw_math(w, g, m, v):
    m = ADAM_B1 * m + (1.0 - ADAM_B1) * g
    v = ADAM_B2 * v + (1.0 - ADAM_B2) * (g * g)
    m_hat = m / (1.0 - ADAM_B1 ** ADAM_STEP)
    v_hat = v / (1.0 - ADAM_B2 ** ADAM_STEP)
    delta = -ADAM_LR * (m_hat / (jnp.sqrt(v_hat) + ADAM_EPS) + ADAM_WD * w)
    return delta, m, v


def _adamw_sharded(parts, w, m, v, tile, name):
    L, _, R, C = parts.shape
    assert R % tile == 0

    def body(p_ref, w_ref, m_ref, v_ref, g_out, d_out, m_out, v_out):
        g = p_ref[0, 0].astype(F32)
        for k in range(1, N_DEV):
            g = g + p_ref[0, k].astype(F32)
        d, mn, vn = _adamw_math(w_ref[0], g, m_ref[0], v_ref[0])
        g_out[0] = g
        d_out[0] = d
        m_out[0] = mn
        v_out[0] = vn

    spec = pl.BlockSpec((1, tile, C), lambda l, i: (l, i, 0))
    shp = jax.ShapeDtypeStruct((L, R, C), F32)
    return pl.pallas_call(
        body, name=name, grid=(L, R // tile),
        in_specs=[pl.BlockSpec((1, N_DEV, tile, C), lambda l, i: (l, 0, i, 0)), spec, spec, spec],
        out_specs=(spec,) * 4, out_shape=(shp,) * 4, compiler_params=_cparams(("parallel", "parallel")),
    )(parts, w, m, v)


COL_GROUP = (("w_glu", 512), ("p_ssm", 512), ("p_mla", 512), ("p_mem", 512), ("w_uq", 256), ("w_ukv", 128))
COL_AT = {n: sum(r for _, r in COL_GROUP[:i]) // rows for i, (n, rows) in enumerate(COL_GROUP)}
assert all(sum(r for _, r in COL_GROUP[:i]) % rows == 0 for i, (_, rows) in enumerate(COL_GROUP))
COL_ROWS = dict(COL_GROUP)
ROW_GROUP = ("w_mem_kv", "w_out")
SMALL = ("b_gate", "ssm_a_re", "ssm_a_im", "ssm_log_dt", "ssm_b_re", "ssm_b_im", "ssm_c_re", "ssm_c_im", "ssm_d",
         "b_glu", "mla_q_norm", "mla_kv_norm", "ln_g", "ln_b")
UQ_COLS = MLA_NOPE + MLA_ROPE


def _pad_lanes(a):
    return jnp.concatenate([a, jnp.zeros(a.shape[:-1] + (LANES - a.shape[-1],), a.dtype)], axis=-1)


def _group_buffers(d, dtype):
    col = jnp.concatenate([_pad_lanes(d[n]) if n == "w_uq" else d[n] for n, _ in COL_GROUP], axis=1)
    row = jnp.concatenate([d[n] for n in ROW_GROUP], axis=1)
    return d["w_in"].astype(dtype), col.astype(dtype), row.astype(dtype)


def _ungroup(bufs):
    b_in, col, row = bufs
    out, off = {"w_in": b_in}, 0
    for n, rows in COL_GROUP:
        t = col[:, off:off + rows]
        out[n] = t[..., :UQ_COLS] if n == "w_uq" else t
        off += rows
    k = row.shape[1] // 2
    out["w_mem_kv"], out["w_out"] = row[:, :k], row[:, k:]
    return out


def _colcat(t):
    return t.transpose(1, 0, 2).reshape(t.shape[1], -1)


def _colsplit(g, n):
    return g.reshape(g.shape[0], N_DEV, n).transpose(1, 0, 2)


def _unpack_weights(g_in, g_col, g_row):
    wc = _colcat(g_col)
    at = lambda n: _RowBlock(wc, COL_ROWS[n], COL_AT[n])
    lo = COL_AT["w_ukv"] * COL_ROWS["w_ukv"]
    ukv = wc[lo:lo + COL_ROWS["w_ukv"]].reshape(-1, MLA_HEADS, LANES)
    lane = lax.broadcasted_iota(jnp.int32, ukv.shape, 2)
    k = g_row.shape[1] // 2
    return dict(
        w_in=_w_in_from_shards(g_in), w_glu=at("w_glu"), w_uq=at("w_uq"), p_ssm=at("p_ssm"), p_mla=at("p_mla"),
        p_mem=at("p_mem"), w_k=jnp.where(lane < MLA_NOPE, ukv, jnp.zeros_like(ukv)).reshape(ukv.shape[0], -1),
        w_v=ukv[..., MLA_NOPE:].reshape(ukv.shape[0], -1),
        w_mem_kv=g_row[:, :k].reshape(-1, g_row.shape[2]), w_out=g_row[:, k:].reshape(-1, g_row.shape[2]))


def _pack_grads_in(d_w_in):
    return _w_in_to_shards(d_w_in).astype(BF)


def _pack_grads_rest(d):
    ukv = jnp.concatenate([d["w_k"].reshape(-1, MLA_HEADS, LANES)[..., :MLA_NOPE],
                           d["w_v"].reshape(-1, MLA_HEADS, MLA_V)], axis=-1).reshape(d["w_k"].shape[0], -1)
    col = jnp.concatenate([ukv if n == "w_ukv" else d[n] for n, _ in COL_GROUP], axis=0)
    row = jnp.concatenate([d[n].reshape(N_DEV, -1, d[n].shape[1]) for n in ROW_GROUP], axis=1)
    return [_colsplit(col, LANES).astype(BF), row.astype(BF)]


def _pack_small(d, lead):
    parts = []
    for n in SMALL:
        keep = d[n].shape[:lead]
        f = d[n].reshape(keep + (-1,))
        pad = (-f.shape[-1]) % (SUBLANES * LANES)
        if pad:
            f = jnp.concatenate([f, jnp.zeros(keep + (pad,), f.dtype)], axis=-1)
        parts.append(f.reshape(keep + (-1, LANES)))
    return jnp.concatenate(parts, axis=lead)


def _unpack_small(buf, like):
    out, off = {}, 0
    for n in SMALL:
        size = math.prod(like[n].shape[1:])
        rows = -(-size // (SUBLANES * LANES)) * SUBLANES
        out[n] = buf[:, off:off + rows].reshape(buf.shape[0], -1)[:, :size].reshape(like[n].shape)
        off += rows
    return out


WEIGHTS = ("w_in", "b_gate", "ssm_a_re", "ssm_a_im", "ssm_log_dt", "ssm_b_re", "ssm_b_im", "ssm_c_re", "ssm_c_im",
           "ssm_d", "w_glu", "b_glu", "mla_q_norm", "w_uq", "mla_kv_norm", "w_ukv", "w_mem_kv", "p_ssm", "p_mla",
           "p_mem", "w_out", "ln_g", "ln_b")
BIG = ("w_in",) + tuple(n for n, _ in COL_GROUP) + ROW_GROUP


def _train_step(x, mem, pos, target, wl, ws):
    S = x.shape[0]
    tc, tsa, tsb = _rope_tables(pos)
    loc = _group_buffers(wl, BF)
    loc = [[b[l] for b in loc] for l in range(DEPTH)]

    lb_re, lb_im, bb_re, bb_im = _ssm_discretise(ws["ssm_a_re"], ws["ssm_a_im"], ws["ssm_log_dt"], ws["ssm_b_re"],
                                                 ws["ssm_b_im"])
    nb = SSM_GROUPS // _GPB
    mats = (jax.vmap(_bd_in)(bb_re), jax.vmap(_bd_in)(bb_im), lb_re.reshape(DEPTH, nb, 1, -1),
            lb_im.reshape(DEPTH, nb, 1, -1), jax.vmap(_bd_out)(ws["ssm_c_re"]), jax.vmap(_bd_out)(ws["ssm_c_im"]),
            ws["ssm_d"].reshape(DEPTH, 1, -1))

    rows3 = {n: ws[n].reshape(DEPTH, 1, -1) for n in ("b_glu", "mla_q_norm", "mla_kv_norm", "b_gate", "ln_g", "ln_b")}

    def small(n, l):
        return _LayerRow(rows3[n], l)

    ((g_in,),) = _exchange_call("weights_gather_first", [_Exchange("ag", loc[0][:1])])
    W = [None] * DEPTH
    saved = []
    xs, xb = x, x.astype(BF)
    for l in range(DEPTH):
        if l == 0:
            proj, (g_rest,) = _mm(xb, _w_in_from_shards(g_in), name="proj_fwd", tm=S, tn=512, out_dtype=ACT,
                                  exs=[_Exchange("ag", loc[0][1:])])
            W[0] = _unpack_weights(g_in, *g_rest)
        else:
            proj = _mm(xb, W[l]["w_in"], name="proj_fwd", tm=S, tn=512, out_dtype=ACT)
        w = W[l]
        u_il = _interleave(proj[:, :SSM_WIDTH])
        y_raw = _deinterleave(_ssm_fwd(u_il, mats, l))
        o_ssm = _ssm_post_fwd(y_raw, proj, w["w_glu"], small("b_glu", l), w["p_ssm"])
        q, k, v, qt, kt, vt = _mla_pre_fwd(proj, small("mla_q_norm", l), small("mla_kv_norm", l), w["w_uq"], w["w_k"], w["w_v"],
                               tc, tsa, tsb)
        nxt = [_Exchange("ag", loc[l + 1])] if l + 1 < DEPTH else []
        (o_att, lse, lse_t), gathered = _flash_fwd(q, kt, v, nxt)
        if nxt:
            W[l + 1] = _unpack_weights(*gathered[0])
        kvm = _mm(mem, w["w_mem_kv"], name="memkv_fwd", out_dtype=BF)
        km, vm = kvm[:, :512], kvm[:, 512:]
        o_mem = _mem_fwd(proj, km, vm, w["p_mem"])
        xn, xnb, pre, merged = _merge_fwd(xs, proj, o_ssm, o_att, o_mem, small("b_gate", l), w["p_mla"], w["w_out"],
                                          small("ln_g", l), small("ln_b", l))
        saved.append(dict(xb=xb, proj=proj, u_il=u_il, y_raw=y_raw, o_ssm=o_ssm, q=q, k=k, v=v, qt=qt, kt=kt, vt=vt, o_att=o_att,
                          lse=lse, lse_t=lse_t,
                          km=km, vm=vm, o_mem=o_mem, pre=pre, merged=merged))
        xs, xb = xn, xnb

    dxs, lvec = _loss_head(xs, target)
    loss = lvec[0, 0]

    disc_names = ("ssm_a_re", "ssm_a_im", "ssm_log_dt", "ssm_b_re", "ssm_b_im")
    got = [None] * DEPTH
    got_small = [None] * DEPTH
    pending = None
    pending_small = None
    for l in reversed(range(DEPTH)):
        sv, w = saved[l], W[l]
        proj = sv["proj"]
        (dx_res, dlg, do_ssm, do_mem, do_att, dz_mla, do_att_t, delta, delta_t, d_w_out, d_p_mla, d_b_gate, d_ln_g,
         d_ln_b) = _merge_bwd(
            dxs, sv["pre"], sv["merged"], proj, sv["o_ssm"], sv["o_att"], sv["o_mem"], small("b_gate", l), w["p_mla"],
            w["w_out"], small("ln_g", l))
        dq_mem, dz_mem, d_km, d_vm, d_p_mem = _mem_bwd(do_mem, proj, sv["km"], sv["vm"], w["p_mem"])
        d_w_mem = _mm(mem, jnp.concatenate([d_km, d_vm], axis=1), name="memkv_bwd", ta=True)
        dq, arrived_rest = _flash_bwd_dq(
            sv["q"], sv["k"], sv["kt"], sv["vt"], do_att, sv["lse"], delta,
            [_Exchange("a2a", pending[1:]), _Exchange("ag", [pending_small])] if pending is not None else [])
        dk, dv, arrived_in = _flash_bwd_dkv(sv["q"], sv["qt"], sv["k"], sv["v"], do_att, do_att_t, sv["lse_t"], delta_t,
                                            [_Exchange("a2a", pending[:1])] if pending is not None else [])
        if pending is not None:
            got[l + 1] = arrived_in[0] + arrived_rest[0]
            got_small[l + 1] = arrived_rest[1][0]
        dcq, dckv, dslot, d_wuq, d_wk, d_wv, d_qn, d_kn = _mla_pre_bwd(
            dq, dk, dv, proj, small("mla_q_norm", l), small("mla_kv_norm", l), w["w_uq"], w["w_k"], w["w_v"],
            tc, tsa, tsb)
        dy_raw, dz_ssm, d_w_glu, d_b_glu, d_p_ssm = _ssm_post_bwd(do_ssm, sv["y_raw"], proj, w["w_glu"],
                                                                 small("b_glu", l), w["p_ssm"])
        rest = _pack_grads_rest(dict(w_glu=d_w_glu, w_uq=d_wuq, w_k=d_wk, w_v=d_wv, w_mem_kv=d_w_mem, p_ssm=d_p_ssm,
                                     p_mla=d_p_mla, p_mem=d_p_mem, w_out=d_w_out))
        (du_il, dbbre, dbbim, dare, daim, dcre, dcim, dd), early = _ssm_bwd(
            sv["u_il"], _interleave(dy_raw), mats, l, [_Exchange("a2a", rest)] if l == 0 else [])
        du = _deinterleave(du_il).astype(BF)
        _, disc_vjp = jax.vjp(_ssm_discretise, *[ws[n][l] for n in disc_names])
        d_disc = disc_vjp((dare.reshape(SSM_GROUPS, SSM_STATE), daim.reshape(SSM_GROUPS, SSM_STATE), _bd_in_t(dbbre),
                           _bd_in_t(dbbim)))
        dproj = jnp.concatenate([du, dz_ssm, dcq, dckv, dslot, dz_mla, dq_mem, dz_mem, dlg], axis=1)
        d_w_in = _mm(sv["xb"], dproj, name="proj_dw", ta=True, tm=1024, tn=512, tk=S)
        if l > 0:
            dxs = _mm(dproj, w["w_in"], name="proj_dx", tb=True, add=dx_res, tm=1024, tn=1024, tk=1024)
        pending = [_pack_grads_in(d_w_in)] + (rest if l > 0 else [])
        gsl = dict(zip(disc_names, d_disc))
        gsl.update(b_gate=d_b_gate, ssm_c_re=_bd_in_t(dcre).transpose(0, 2, 1), ssm_c_im=_bd_in_t(dcim).transpose(0, 2, 1),
                   ssm_d=dd, b_glu=d_b_glu, mla_q_norm=d_qn, mla_kv_norm=d_kn, ln_g=d_ln_g, ln_b=d_ln_b)
        pending_small = _pack_small(gsl, 0)

    dxs, (last_in, (got_small[0],)) = _mm(
        dproj, w["w_in"], name="proj_dx", tb=True, add=dx_res, tm=1024, tn=1024, tk=1024,
        exs=[_Exchange("a2a", pending), _Exchange("ag", [pending_small])])
    got[0] = last_in + early[0]
    return loss, dxs, got, got_small


def kernel(x, mem, positions, w_in, b_gate, ssm_a_re, ssm_a_im, ssm_log_dt, ssm_b_re, ssm_b_im, ssm_c_re, ssm_c_im, ssm_d, w_glu, b_glu, mla_q_norm, w_uq, mla_kv_norm, w_ukv, w_mem_kv, p_ssm, p_mla, p_mem, w_out, ln_g, ln_b, loss_target, m_w_in, m_b_gate, m_ssm_a_re, m_ssm_a_im, m_ssm_log_dt, m_ssm_b_re, m_ssm_b_im, m_ssm_c_re, m_ssm_c_im, m_ssm_d, m_w_glu, m_b_glu, m_mla_q_norm, m_w_uq, m_mla_kv_norm, m_w_ukv, m_w_mem_kv, m_p_ssm, m_p_mla, m_p_mem, m_w_out, m_ln_g, m_ln_b, v_w_in, v_b_gate, v_ssm_a_re, v_ssm_a_im, v_ssm_log_dt, v_ssm_b_re, v_ssm_b_im, v_ssm_c_re, v_ssm_c_im, v_ssm_d, v_w_glu, v_b_glu, v_mla_q_norm, v_w_uq, v_mla_kv_norm, v_w_ukv, v_w_mem_kv, v_p_ssm, v_p_mla, v_p_mem, v_w_out, v_ln_g, v_ln_b):
    w = dict(w_in=w_in, b_gate=b_gate, ssm_a_re=ssm_a_re, ssm_a_im=ssm_a_im, ssm_log_dt=ssm_log_dt, ssm_b_re=ssm_b_re,
             ssm_b_im=ssm_b_im, ssm_c_re=ssm_c_re, ssm_c_im=ssm_c_im, ssm_d=ssm_d, w_glu=w_glu, b_glu=b_glu,
             mla_q_norm=mla_q_norm, w_uq=w_uq, mla_kv_norm=mla_kv_norm, w_ukv=w_ukv, w_mem_kv=w_mem_kv, p_ssm=p_ssm,
             p_mla=p_mla, p_mem=p_mem, w_out=w_out, ln_g=ln_g, ln_b=ln_b)
    m = dict(w_in=m_w_in, b_gate=m_b_gate, ssm_a_re=m_ssm_a_re, ssm_a_im=m_ssm_a_im, ssm_log_dt=m_ssm_log_dt,
             ssm_b_re=m_ssm_b_re, ssm_b_im=m_ssm_b_im, ssm_c_re=m_ssm_c_re, ssm_c_im=m_ssm_c_im, ssm_d=m_ssm_d,
             w_glu=m_w_glu, b_glu=m_b_glu, mla_q_norm=m_mla_q_norm, w_uq=m_w_uq, mla_kv_norm=m_mla_kv_norm,
             w_ukv=m_w_ukv, w_mem_kv=m_w_mem_kv, p_ssm=m_p_ssm, p_mla=m_p_mla, p_mem=m_p_mem, w_out=m_w_out,
             ln_g=m_ln_g, ln_b=m_ln_b)
    v = dict(w_in=v_w_in, b_gate=v_b_gate, ssm_a_re=v_ssm_a_re, ssm_a_im=v_ssm_a_im, ssm_log_dt=v_ssm_log_dt,
             ssm_b_re=v_ssm_b_re, ssm_b_im=v_ssm_b_im, ssm_c_re=v_ssm_c_re, ssm_c_im=v_ssm_c_im, ssm_d=v_ssm_d,
             w_glu=v_w_glu, b_glu=v_b_glu, mla_q_norm=v_mla_q_norm, w_uq=v_w_uq, mla_kv_norm=v_mla_kv_norm,
             w_ukv=v_w_ukv, w_mem_kv=v_w_mem_kv, p_ssm=v_p_ssm, p_mla=v_p_mla, p_mem=v_p_mem, w_out=v_w_out,
             ln_g=v_ln_g, ln_b=v_ln_b)

    wl = {n: w[n] for n in BIG}
    small = {n: w[n] for n in SMALL}
    loss_local, dx, got, got_small = _train_step(x[0], mem[0], positions[0], loss_target[0], wl, small)
    loss = lax.psum(loss_local, ("x", "y", "c"))

    grads, delta, new_m, new_v = {}, {}, {}, {}
    wg = _group_buffers(wl, F32)
    mg = _group_buffers({n: m[n] for n in BIG}, F32)
    vg = _group_buffers({n: v[n] for n in BIG}, F32)
    res = []
    for i, (tile, tag) in enumerate(((256, "in"), (128, "col"), (256, "row"))):
        parts = jnp.stack([got[l][i] for l in range(DEPTH)])
        res.append(_adamw_sharded(parts, wg[i], mg[i], vg[i], tile, "adamw_" + tag))
    for dst, j in ((grads, 0), (delta, 1), (new_m, 2), (new_v, 3)):
        dst.update(_ungroup([r[j] for r in res]))

    sw, sm, sv = (_pack_small(small, 1), _pack_small({n: m[n] for n in SMALL}, 1), _pack_small({n: v[n] for n in SMALL}, 1))
    rs = _adamw_sharded(jnp.stack(got_small), sw, sm, sv, sw.shape[1], "adamw_replicated")
    for dst, buf in zip((grads, delta, new_m, new_v), rs):
        dst.update(_unpack_small(buf, small))

    return (loss, dx[None], *[grads[n] for n in WEIGHTS], *[delta[n] for n in WEIGHTS],
            *[new_m[n] for n in WEIGHTS], *[new_v[n] for n in WEIGHTS])
```

```python
import math

import jax
import jax.numpy as jnp
from jax import lax
from jax.experimental import pallas as pl
from jax.experimental.pallas import tpu as pltpu

F32 = jnp.float32
BF = jnp.bfloat16
ACT = jnp.bfloat16

D_MODEL = 1024
DEPTH = 4
N_DEV = 8
SSM_WIDTH = 512
SSM_GROUP = 16
SSM_GROUPS = 32
SSM_STATE = 64
MLA_HEADS = 8
MLA_NOPE = 64
MLA_ROPE = 32
MLA_V = 64
MLA_Q_RANK = 256
MLA_KV_RANK = 128
ROPE_THETA = 10000.0
X_HEADS = 4
X_HEAD_DIM = 128
D_IN = 6048
ALPHA = (2 * DEPTH) ** 0.25
NORM_EPS = 1e-5
ADAM_LR = 0.001
ADAM_B1 = 0.9
ADAM_B2 = 0.999
ADAM_EPS = 1e-08
ADAM_WD = 0.01
ADAM_STEP = 10

LANES = 128
SUBLANES = 8
VMEM_LIMIT = 56 * 1024 * 1024

PW = 6144
ROPE_SLOT_LO = 1408
MLA_SCALE = (MLA_NOPE + MLA_ROPE) ** -0.5
MEM_SCALE = X_HEAD_DIM ** -0.5
NEG = -1e30

T_ROWS = 512
T_ROWS_BWD = 256
T_ROWS_BWD_WIDE = 512
T_ATT = 1024
T_MM = 512

MESH = pl.DeviceIdType.MESH


def _cparams(sem):
    return pltpu.CompilerParams(dimension_semantics=sem, vmem_limit_bytes=VMEM_LIMIT)


def _dot(a, b):
    return lax.dot_general(a, b, (((1,), (0,)), ((), ())), preferred_element_type=F32)


def _dot_nt(a, b):
    return lax.dot_general(a, b, (((1,), (1,)), ((), ())), preferred_element_type=F32)


def _dot_tn(a, b):
    return lax.dot_general(a, b, (((0,), (0,)), ((), ())), preferred_element_type=F32)


def _sigmoid(x):
    return 0.5 * jnp.tanh(0.5 * x) + 0.5


def _silu(x):
    return x * _sigmoid(x)


def _dsilu(x):
    s = _sigmoid(x)
    return s * (1.0 + x * (1.0 - s))


_GELU_C = math.sqrt(2.0 / math.pi)


def _gelu(x):
    return 0.5 * x * (1.0 + jnp.tanh(_GELU_C * (x + 0.044715 * x * x * x)))


def _dgelu(x):
    t = jnp.tanh(_GELU_C * (x + 0.044715 * x * x * x))
    return 0.5 * (1.0 + t) + 0.5 * x * (1.0 - t * t) * _GELU_C * (1.0 + 3 * 0.044715 * x * x)


def _rows(tr, w, col=0):
    return pl.BlockSpec((tr, w), lambda i: (i, col))


def _cols(h, tc):
    return pl.BlockSpec((h, tc), lambda i: (0, i))


def _full(shape):
    n = len(shape)
    return pl.BlockSpec(shape, lambda i: (0,) * n)


class _RowBlock:
    def __init__(self, arr, rows, blk):
        self.arr, self.rows, self.blk = arr, rows, blk

    def spec(self):
        blk = self.blk
        return pl.BlockSpec((self.rows, self.arr.shape[1]), lambda i: (blk, 0))


class _LayerRow:
    def __init__(self, arr, l):
        self.arr, self.l = arr, l

    def spec(self):
        l = self.l
        return pl.BlockSpec((1, 1, self.arr.shape[2]), lambda i: (l, 0, 0))


def _mm(a, b, *, name, ta=False, tb=False, out_dtype=F32, add=None, tm=T_MM, tn=T_MM, tk=1024, exs=None):
    M, K = (a.shape[1], a.shape[0]) if ta else a.shape
    N = b.shape[0] if tb else b.shape[1]
    tm, tn, tk = min(tm, M), min(tn, N), min(tk, K)
    assert M % tm == 0 and N % tn == 0 and K % tk == 0, (M, N, K)
    nk = K // tk
    dn = (((0 if ta else 1,), (1 if tb else 0,)), ((), ()))

    def body(*refs):
        if add is not None:
            a_ref, b_ref, c_ref, o_ref = refs[:4]
        else:
            a_ref, b_ref, o_ref = refs[:3]
        part = lax.dot_general(a_ref[...].astype(BF), b_ref[...].astype(BF), dn, preferred_element_type=F32)
        if nk == 1:
            if add is not None:
                part = part + c_ref[...]
            o_ref[...] = part.astype(out_dtype)
            return
        acc = refs[-1]
        k = pl.program_id(2)

        @pl.when(k == 0)
        def _():
            acc[...] = part

        @pl.when(k != 0)
        def _():
            acc[...] += part

        @pl.when(k == nk - 1)
        def _():
            r = acc[...]
            if add is not None:
                r = r + c_ref[...]
            o_ref[...] = r.astype(out_dtype)

    a_spec = pl.BlockSpec((tk, tm), lambda i, j, k: (k, i)) if ta else pl.BlockSpec((tm, tk), lambda i, j, k: (i, k))
    b_spec = pl.BlockSpec((tn, tk), lambda i, j, k: (j, k)) if tb else pl.BlockSpec((tk, tn), lambda i, j, k: (k, j))
    o_spec = pl.BlockSpec((tm, tn), lambda i, j, k: (i, j))
    in_specs = [a_spec, b_spec] + ([o_spec] if add is not None else [])
    args = (a, b) + ((add,) if add is not None else ())
    (out,), got = _carry_call(
        body, name, (M // tm, N // tn, nk), in_specs, [o_spec], [jax.ShapeDtypeStruct((M, N), out_dtype)],
        [pltpu.VMEM((tm, tn), F32)] if nk > 1 else [], ("parallel", "parallel", "arbitrary"), args, exs)
    return out if exs is None else (out, got)


def _cpow(ar, ai, n):
    rr, ri = None, None
    br, bi = ar, ai
    while n:
        if n & 1:
            if rr is None:
                rr, ri = br, bi
            else:
                rr, ri = rr * br - ri * bi, rr * bi + ri * br
        n >>= 1
        if n:
            br, bi = br * br - bi * bi, 2.0 * br * bi
    return rr, ri


def _seg_shift(v, k, reverse):
    sub = lax.broadcasted_iota(jnp.int32, v.shape, 0)
    if not reverse:
        return jnp.where(sub >= k, pltpu.roll(v, k, 0), 0.0)
    return jnp.where(sub < SUBLANES - k, pltpu.roll(v, SUBLANES - k, 0), 0.0)


def _steps(n, step, init, unroll):
    u = unroll if n % unroll == 0 else 1

    def trip(i, c):
        for s in range(u):
            c = step(i * u + s, c)
        return c

    return lax.fori_loop(0, n // u, trip, init)


def _ssm_scan(hre, him, ar, ai, seglen, reverse, tail=None, tail_init=()):
    w = hre.shape[1]
    zero = jnp.zeros((SUBLANES, w), F32)

    def rows(j):
        jj = (seglen - 1 - j) if reverse else j
        return pl.ds(pl.multiple_of(jj * SUBLANES, SUBLANES), SUBLANES)

    def local(j, c):
        hr, hi = c
        r = rows(j)
        nhr = ar * hr - ai * hi + hre[r, :]
        nhi = ar * hi + ai * hr + him[r, :]
        hre[r, :] = nhr
        him[r, :] = nhi
        return nhr, nhi

    er, ei = _steps(seglen, local, (zero, zero), 4 if reverse else 1)
    pr, pi_ = _cpow(ar, ai, seglen)
    for k in (1, 2, 4):
        sr, si = _seg_shift(er, k, reverse), _seg_shift(ei, k, reverse)
        er, ei = er + pr * sr - pi_ * si, ei + pr * si + pi_ * sr
        pr, pi_ = pr * pr - pi_ * pi_, 2.0 * pr * pi_
    cr, ci = _seg_shift(er, 1, reverse), _seg_shift(ei, 1, reverse)

    def carry_in(j, c):
        tr, ti = c[0] * ar - c[1] * ai, c[0] * ai + c[1] * ar
        r = rows(j)
        fr = hre[r, :] + tr
        fi = him[r, :] + ti
        hre[r, :] = fr
        him[r, :] = fi
        if tail is None:
            return tr, ti
        return (tr, ti) + tuple(tail(j, fr, fi, c[2:]))

    out = _steps(seglen, carry_in, (cr, ci) + tuple(tail_init), 4)
    return out[2:]


SSM_CB = 128
SSM_SB = 256
SSM_SB_F = 512


def _ssm_specs(S, l, sb=SSM_SB):
    u_spec = pl.BlockSpec((S, SSM_CB), lambda g, h: (0, g))
    bb_spec = pl.BlockSpec((1, 1, SSM_CB, sb), lambda g, h: (l, g, 0, h))
    a_spec = pl.BlockSpec((1, 1, 1, sb), lambda g, h: (l, g, 0, h))
    c_spec = pl.BlockSpec((1, 1, sb, SSM_CB), lambda g, h: (l, g, h, 0))
    d_spec = pl.BlockSpec((1, 1, SSM_CB), lambda g, h: (l, 0, g))
    return u_spec, bb_spec, a_spec, c_spec, d_spec


def _ssm_fwd(u, mats, l):
    S = u.shape[0]
    seglen = S // SUBLANES
    ch = min(512, S)
    nch = S // ch

    def body(u_ref, bbre_ref, bbim_ref, are_ref, aim_ref, cre_ref, cim_ref, d_ref, y_ref, hre, him):
        hf = pl.program_id(1)
        wre = bbre_ref[0, 0].astype(BF)
        wim = bbim_ref[0, 0].astype(BF)

        def mk(c, _):
            r = pl.ds(pl.multiple_of(c * ch, ch), ch)
            ub = u_ref[r, :].astype(BF)
            hre[r, :] = _dot(ub, wre)
            him[r, :] = _dot(ub, wim)
            return 0

        lax.fori_loop(0, nch, mk, 0)
        ar = jnp.broadcast_to(are_ref[0, 0], (SUBLANES, SSM_SB_F))
        ai = jnp.broadcast_to(aim_ref[0, 0], (SUBLANES, SSM_SB_F))
        _ssm_scan(hre, him, ar, ai, seglen, False)
        cr = cre_ref[0, 0].astype(BF)
        ci = cim_ref[0, 0].astype(BF)

        def out(c, _):
            r = pl.ds(pl.multiple_of(c * ch, ch), ch)
            y = _dot(hre[r, :].astype(BF), cr) - _dot(him[r, :].astype(BF), ci)

            @pl.when(hf == 0)
            def _():
                y_ref[r, :] = y + d_ref[0] * u_ref[r, :].astype(F32)

            @pl.when(hf != 0)
            def _():
                y_ref[r, :] = y_ref[r, :] + y

            return 0

        lax.fori_loop(0, nch, out, 0)

    u_spec, bb_spec, a_spec, c_spec, d_spec = _ssm_specs(S, l, SSM_SB_F)
    return pl.pallas_call(
        body, name="ssm_fwd", grid=(SSM_WIDTH // SSM_CB, (SSM_CB // SSM_GROUP * SSM_STATE) // SSM_SB_F),
        in_specs=[u_spec, bb_spec, bb_spec, a_spec, a_spec, c_spec, c_spec, d_spec], out_specs=u_spec,
        out_shape=jax.ShapeDtypeStruct((S, SSM_WIDTH), F32),
        scratch_shapes=[pltpu.VMEM((S, SSM_SB_F), F32), pltpu.VMEM((S, SSM_SB_F), F32)],
        compiler_params=_cparams(("parallel", "arbitrary")),
    )(u, *mats)


def _ssm_bwd(u, dy, mats, l, exs=()):
    S = u.shape[0]
    seglen = S // SUBLANES
    ch = min(512, S)
    nch = S // ch
    nblk = SSM_WIDTH // SSM_CB

    def body(u_ref, dy_ref, bbre_ref, bbim_ref, are_ref, aim_ref, cre_ref, cim_ref, d_ref,
             du_ref, dbbre_ref, dbbim_ref, dare_ref, daim_ref, dcre_ref, dcim_ref, dd_ref,
             hre, him, lre, lim):
        hf = pl.program_id(1)
        wre = bbre_ref[0, 0].astype(BF)
        wim = bbim_ref[0, 0].astype(BF)
        wre_t, wim_t = wre.T, wim.T
        cr_t = cre_ref[0, 0].astype(BF).T
        ci_t = cim_ref[0, 0].astype(BF).T

        def mk(c, _):
            r = pl.ds(pl.multiple_of(c * ch, ch), ch)
            ub = u_ref[r, :].astype(BF)
            hre[r, :] = _dot(ub, wre)
            him[r, :] = _dot(ub, wim)
            return 0

        lax.fori_loop(0, nch, mk, 0)
        ar = jnp.broadcast_to(are_ref[0, 0], (SUBLANES, SSM_SB))
        ai = jnp.broadcast_to(aim_ref[0, 0], (SUBLANES, SSM_SB))
        _ssm_scan(hre, him, ar, ai, seglen, False)

        dcre_ref[...] = jnp.zeros_like(dcre_ref)
        dcim_ref[...] = jnp.zeros_like(dcim_ref)

        @pl.when(hf == 0)
        def _():
            dd_ref[...] = jnp.zeros_like(dd_ref)

        def cot(c, _):
            r = pl.ds(pl.multiple_of(c * ch, ch), ch)
            dyv = dy_ref[r, :]
            dyb = dyv.astype(BF)
            lre[r, :] = _dot(dyb, cr_t)
            lim[r, :] = -_dot(dyb, ci_t)
            dcre_ref[0] = dcre_ref[0] + _dot_tn(dyb, hre[r, :].astype(BF))
            dcim_ref[0] = dcim_ref[0] - _dot_tn(dyb, him[r, :].astype(BF))

            @pl.when(hf == 0)
            def _():
                dd_ref[...] = dd_ref[...] + jnp.sum(dyv * u_ref[r, :].astype(F32), axis=0, keepdims=True)

            return 0

        lax.fori_loop(0, nch, cot, 0)

        last = pl.ds((seglen - 1) * SUBLANES, SUBLANES)
        pr0 = _seg_shift(hre[last, :], 1, False)
        pi0 = _seg_shift(him[last, :], 1, False)

        def da(j, lr, li, c):
            acr, aci = c
            jp = jnp.maximum(seglen - 2 - j, 0)
            rp = pl.ds(pl.multiple_of(jp * SUBLANES, SUBLANES), SUBLANES)
            inner = j < seglen - 1
            pr = jnp.where(inner, hre[rp, :], pr0)
            pi_ = jnp.where(inner, him[rp, :], pi0)
            return acr + lr * pr + li * pi_, aci + li * pr - lr * pi_

        zero = jnp.zeros((SUBLANES, SSM_SB), F32)
        acr, aci = _ssm_scan(lre, lim, ar, -ai, seglen, True, tail=da, tail_init=(zero, zero))
        dare_ref[0] = jnp.sum(acr, axis=0, keepdims=True)
        daim_ref[0] = jnp.sum(aci, axis=0, keepdims=True)

        dbbre_ref[...] = jnp.zeros_like(dbbre_ref)
        dbbim_ref[...] = jnp.zeros_like(dbbim_ref)

        def fin(c, _):
            r = pl.ds(pl.multiple_of(c * ch, ch), ch)
            lrb = lre[r, :].astype(BF)
            lib = lim[r, :].astype(BF)
            ub = u_ref[r, :].astype(BF)
            du = _dot(lrb, wre_t) + _dot(lib, wim_t)
            dbbre_ref[0] = dbbre_ref[0] + _dot_tn(ub, lrb)
            dbbim_ref[0] = dbbim_ref[0] + _dot_tn(ub, lib)

            @pl.when(hf == 0)
            def _():
                du_ref[r, :] = du + d_ref[0] * dy_ref[r, :]

            @pl.when(hf != 0)
            def _():
                du_ref[r, :] = du_ref[r, :] + du

            return 0

        lax.fori_loop(0, nch, fin, 0)

    u_spec, bb_spec, a_spec, c_spec, d_spec = _ssm_specs(S, l)
    dbb_spec = pl.BlockSpec((1, SSM_CB, SSM_SB), lambda g, h: (g, 0, h))
    da_spec = pl.BlockSpec((1, 1, SSM_SB), lambda g, h: (g, 0, h))
    dd_spec = pl.BlockSpec((1, SSM_CB), lambda g, h: (0, g))
    out_shape = (
        jax.ShapeDtypeStruct((S, SSM_WIDTH), F32),
        jax.ShapeDtypeStruct((nblk, SSM_CB, 2 * SSM_SB), F32), jax.ShapeDtypeStruct((nblk, SSM_CB, 2 * SSM_SB), F32),
        jax.ShapeDtypeStruct((nblk, 1, 2 * SSM_SB), F32), jax.ShapeDtypeStruct((nblk, 1, 2 * SSM_SB), F32),
        jax.ShapeDtypeStruct((nblk, SSM_CB, 2 * SSM_SB), F32), jax.ShapeDtypeStruct((nblk, SSM_CB, 2 * SSM_SB), F32),
        jax.ShapeDtypeStruct((1, SSM_WIDTH), F32),
    )
    return _carry_call(
        body, "ssm_bwd", (nblk, 2), [u_spec, u_spec, bb_spec, bb_spec, a_spec, a_spec, c_spec, c_spec, d_spec],
        (u_spec, dbb_spec, dbb_spec, da_spec, da_spec, dbb_spec, dbb_spec, dd_spec), out_shape,
        [pltpu.VMEM((S, SSM_SB), F32) for _ in range(4)], ("parallel", "arbitrary"), (u, dy) + tuple(mats), exs)


def _ssm_post_fwd(y_raw, proj, w_glu, b_glu, p_ssm):
    S = y_raw.shape[0]
    tr = min(T_ROWS, S)

    def body(y_ref, z_ref, wg_ref, bg_ref, p_ref, o_ref):
        g = _gelu(y_ref[...])
        t = _dot(g.astype(BF), wg_ref[...]) + bg_ref[0]
        glu = t[:, :SSM_WIDTH] * _sigmoid(t[:, SSM_WIDTH:])
        ys = glu * _silu(z_ref[...].astype(F32))
        o_ref[...] = _dot(ys.astype(BF), p_ref[...]).astype(o_ref.dtype)

    return pl.pallas_call(
        body, name="ssm_post_fwd", grid=(S // tr,),
        in_specs=[_rows(tr, 512), _rows(tr, 512, 1), w_glu.spec(), b_glu.spec(), p_ssm.spec()],
        out_specs=_rows(tr, 1024), out_shape=jax.ShapeDtypeStruct((S, D_MODEL), ACT),
        compiler_params=_cparams(("parallel",)),
    )(y_raw, proj, w_glu.arr, b_glu.arr, p_ssm.arr)


def _ssm_post_bwd(do, y_raw, proj, w_glu, b_glu, p_ssm):
    S = y_raw.shape[0]
    tr = min(T_ROWS_BWD_WIDE, S)

    def body(do_ref, y_ref, z_ref, wg_ref, bg_ref, p_ref, dy_ref, dz_ref, dwg_ref, dbg_ref, dp_ref):
        @pl.when(pl.program_id(0) == 0)
        def _():
            dwg_ref[...] = jnp.zeros_like(dwg_ref)
            dbg_ref[...] = jnp.zeros_like(dbg_ref)
            dp_ref[...] = jnp.zeros_like(dp_ref)

        y = y_ref[...]
        z = z_ref[...].astype(F32)
        g = _gelu(y)
        gb = g.astype(BF)
        t = _dot(gb, wg_ref[...]) + bg_ref[0]
        a = t[:, :SSM_WIDTH]
        sb = _sigmoid(t[:, SSM_WIDTH:])
        glu = a * sb
        ys = glu * _silu(z)
        dob = do_ref[...].astype(BF)
        dys = _dot_nt(dob, p_ref[...])
        dp_ref[...] += _dot_tn(ys.astype(BF), dob)
        dglu = dys * _silu(z)
        dz_ref[...] = (dys * glu * _dsilu(z)).astype(dz_ref.dtype)
        dt = jnp.concatenate([dglu * sb, dglu * a * sb * (1.0 - sb)], axis=1)
        dbg_ref[...] += jnp.sum(dt, axis=0, keepdims=True)
        dtb = dt.astype(BF)
        dg = _dot_nt(dtb, wg_ref[...])
        dwg_ref[...] += _dot_tn(gb, dtb)
        dy_ref[...] = dg * _dgelu(y)

    return pl.pallas_call(
        body, name="ssm_post_bwd", grid=(S // tr,),
        in_specs=[_rows(tr, 1024), _rows(tr, 512), _rows(tr, 512, 1), w_glu.spec(), b_glu.spec(), p_ssm.spec()],
        out_specs=(_rows(tr, 512), _rows(tr, 512), _full((512, 1024)), _full((1, 1024)), _full((512, 1024))),
        out_shape=(jax.ShapeDtypeStruct((S, 512), F32), jax.ShapeDtypeStruct((S, 512), BF),
                   jax.ShapeDtypeStruct((512, 1024), F32), jax.ShapeDtypeStruct((1, 1024), F32),
                   jax.ShapeDtypeStruct((512, 1024), F32)),
        compiler_params=_cparams(("arbitrary",)),
    )(do, y_raw, proj, w_glu.arr, b_glu.arr, p_ssm.arr)


def _rope(t, c, sa, sb):
    return t * c + pltpu.roll(t, LANES - 16, 1) * sa + pltpu.roll(t, 16, 1) * sb


def _rope_t(dy, c, sa, sb):
    return dy * c + pltpu.roll(dy * sa, 16, 1) + pltpu.roll(dy * sb, LANES - 16, 1)


def _rms(x, g):
    r = lax.rsqrt(jnp.mean(x * x, axis=-1, keepdims=True) + NORM_EPS)
    return x * r * g, r


def _mla_pre_fwd(proj, q_norm, kv_norm, wuq, wk, wv, tc, tsa, tsb):
    S = proj.shape[0]
    tr = min(T_ROWS, S)

    def body(cq_ref, ckv_ref, slot_ref, qn_ref, kn_ref, wuq_ref, wk_ref, wv_ref, c_ref, sa_ref, sb_ref,
             q_out, k_out, v_out, qt_out, kt_out, vt_out):
        c, sa, sb = c_ref[...], sa_ref[...], sb_ref[...]
        qn, _ = _rms(cq_ref[...].astype(F32), qn_ref[0])
        q = _dot(qn.astype(BF), wuq_ref[...]) * MLA_SCALE
        kn, _ = _rms(ckv_ref[...].astype(F32), kn_ref[0])
        knb = kn.astype(BF)
        kp = _dot(knb, wk_ref[...])
        v = _dot(knb, wv_ref[...]).astype(BF)
        v_out[...] = v
        vt_out[...] = v.T
        kr = _rope(slot_ref[...].astype(F32), c, sa, sb)
        for h in range(MLA_HEADS):
            cs = slice(h * LANES, (h + 1) * LANES)
            qh = _rope(q[:, cs], c, sa, sb).astype(BF)
            kh = (kp[:, cs] + kr).astype(BF)
            q_out[:, cs] = qh
            k_out[:, cs] = kh
            qt_out[cs, :] = qh.T
            kt_out[cs, :] = kh.T

    return pl.pallas_call(
        body, name="mla_pre_fwd", grid=(S // tr,),
        in_specs=[_rows(tr, 256, 4), _rows(tr, 128, 10), _rows(tr, 128, 11), q_norm.spec(), kv_norm.spec(),
                  wuq.spec(), _full((128, 1024)), _full((128, 512)),
                  _rows(tr, 128), _rows(tr, 128), _rows(tr, 128)],
        out_specs=(_rows(tr, 1024), _rows(tr, 1024), _rows(tr, 512), _cols(1024, tr), _cols(1024, tr), _cols(512, tr)),
        out_shape=(jax.ShapeDtypeStruct((S, 1024), BF), jax.ShapeDtypeStruct((S, 1024), BF),
                   jax.ShapeDtypeStruct((S, 512), BF), jax.ShapeDtypeStruct((1024, S), BF),
                   jax.ShapeDtypeStruct((1024, S), BF), jax.ShapeDtypeStruct((512, S), BF)),
        compiler_params=_cparams(("parallel",)),
    )(proj, proj, proj, q_norm.arr, kv_norm.arr, wuq.arr, wk, wv, tc, tsa, tsb)


def _mla_pre_bwd(dq, dk, dv, proj, q_norm, kv_norm, wuq, wk, wv, tc, tsa, tsb):
    S = proj.shape[0]
    tr = min(T_ROWS_BWD, S)

    def body(dq_ref, dk_ref, dv_ref, cq_ref, ckv_ref, qn_ref, kn_ref, wuq_ref, wk_ref, wv_ref, c_ref, sa_ref, sb_ref,
             dcq_ref, dckv_ref, dslot_ref, dwuq_ref, dwk_ref, dwv_ref, dqn_ref, dkn_ref, dqp):
        @pl.when(pl.program_id(0) == 0)
        def _():
            dwuq_ref[...] = jnp.zeros_like(dwuq_ref)
            dwk_ref[...] = jnp.zeros_like(dwk_ref)
            dwv_ref[...] = jnp.zeros_like(dwv_ref)
            dqn_ref[...] = jnp.zeros_like(dqn_ref)
            dkn_ref[...] = jnp.zeros_like(dkn_ref)

        c, sa, sb = c_ref[...], sa_ref[...], sb_ref[...]
        dkr = jnp.zeros((tr, LANES), F32)
        for h in range(MLA_HEADS):
            cs = slice(h * LANES, (h + 1) * LANES)
            dqp[:, cs] = (_rope_t(dq_ref[:, cs], c, sa, sb) * MLA_SCALE).astype(BF)
            dkr = dkr + dk_ref[:, cs]
        lane = lax.broadcasted_iota(jnp.int32, (tr, LANES), 1)
        in_rope = (lane >= MLA_NOPE) & (lane < MLA_NOPE + MLA_ROPE)
        dslot_ref[...] = jnp.where(in_rope, _rope_t(dkr, c, sa, sb), 0.0).astype(dslot_ref.dtype)

        cq = cq_ref[...].astype(F32)
        gq = qn_ref[0]
        qn, rq = _rms(cq, gq)
        dqpb = dqp[...]
        dwuq_ref[...] += _dot_tn(qn.astype(BF), dqpb)
        dqn = _dot_nt(dqpb, wuq_ref[...])
        dqn_ref[...] += jnp.sum(dqn * cq * rq, axis=0, keepdims=True)
        dyg = dqn * gq
        dcq_ref[...] = (rq * dyg - cq * (rq * rq * rq) * jnp.mean(dyg * cq, axis=-1, keepdims=True)).astype(dcq_ref.dtype)

        ckv = ckv_ref[...].astype(F32)
        gk = kn_ref[0]
        kn, rk = _rms(ckv, gk)
        knb = kn.astype(BF)
        dkb = dk_ref[...].astype(BF)
        dvb = dv_ref[...].astype(BF)
        dwk_ref[...] += _dot_tn(knb, dkb)
        dwv_ref[...] += _dot_tn(knb, dvb)
        dkn = _dot_nt(dkb, wk_ref[...]) + _dot_nt(dvb, wv_ref[...])
        dkn_ref[...] += jnp.sum(dkn * ckv * rk, axis=0, keepdims=True)
        dyk = dkn * gk
        dckv_ref[...] = (rk * dyk - ckv * (rk * rk * rk) * jnp.mean(dyk * ckv, axis=-1, keepdims=True)).astype(dckv_ref.dtype)

    return pl.pallas_call(
        body, name="mla_pre_bwd", grid=(S // tr,),
        in_specs=[_rows(tr, 1024), _rows(tr, 1024), _rows(tr, 512), _rows(tr, 256, 4), _rows(tr, 128, 10),
                  q_norm.spec(), kv_norm.spec(), wuq.spec(), _full((128, 1024)), _full((128, 512)),
                  _rows(tr, 128), _rows(tr, 128), _rows(tr, 128)],
        out_specs=(_rows(tr, 256), _rows(tr, 128), _rows(tr, 128), _full((256, 1024)), _full((128, 1024)),
                   _full((128, 512)), _full((1, 256)), _full((1, 128))),
        out_shape=(jax.ShapeDtypeStruct((S, 256), BF), jax.ShapeDtypeStruct((S, 128), BF),
                   jax.ShapeDtypeStruct((S, 128), BF), jax.ShapeDtypeStruct((256, 1024), F32),
                   jax.ShapeDtypeStruct((128, 1024), F32), jax.ShapeDtypeStruct((128, 512), F32),
                   jax.ShapeDtypeStruct((1, 256), F32), jax.ShapeDtypeStruct((1, 128), F32)),
        scratch_shapes=[pltpu.VMEM((tr, 1024), BF)],
        compiler_params=_cparams(("arbitrary",)),
    )(dq, dk, dv, proj, proj, q_norm.arr, kv_norm.arr, wuq.arr, wk, wv, tc, tsa, tsb)


ANY = pl.BlockSpec(memory_space=pl.ANY)
N_REL = N_DEV - 1


def _coords():
    return lax.axis_index("x"), lax.axis_index("y"), lax.axis_index("c")


def _sem_shapes(nbuf):
    return [pltpu.SemaphoreType.DMA((N_REL * nbuf,)), pltpu.SemaphoreType.DMA((N_REL * nbuf,)),
            pltpu.SemaphoreType.DMA((nbuf,))]


def _ag_plan(srcs, dsts, sems):
    send_sems, recv_sems, _ = sems
    plan = []
    for b, (src, dst) in enumerate(zip(srcs, dsts)):
        def slot(px, py, pc, dst=dst):
            return dst.at[4 * px + 2 * py + pc]

        def copy(k, blk, to, s=None, b=b, slot=slot):
            return pltpu.make_async_remote_copy(
                src_ref=slot(*blk) if s is None else s, dst_ref=slot(*blk), send_sem=send_sems.at[N_REL * b + k],
                recv_sem=recv_sems.at[N_REL * b + k], device_id=to, device_id_type=MESH)

        plan.append((b, src, slot, copy))
    return plan


def _ag_start(srcs, dsts, sems):
    x, y, c = _coords()
    chips = [(1 - x, y), (x, 1 - y), (1 - x, 1 - y)]
    for b, src, slot, copy in _ag_plan(srcs, dsts, sems):
        pltpu.make_async_copy(src, slot(x, y, c), sems[2].at[b]).start()
        copy(0, (x, y, c), (x, y, 1 - c), src).start()
        for j, chip in enumerate(chips):
            copy(1 + j, (x, y, c), (*chip, c), src).start()


def _ag_relay(srcs, dsts, sems):
    x, y, c = _coords()
    me, sibling = (x, y, c), (x, y, 1 - c)
    chips = [(1 - x, y), (x, 1 - y), (1 - x, 1 - y)]
    for b, src, slot, copy in _ag_plan(srcs, dsts, sems):
        for j, chip in enumerate(chips):
            copy(1 + j, (*chip, c), me).wait_recv()
            copy(4 + j, (*chip, c), sibling).start()


def _ag_finish(srcs, dsts, sems, relayed):
    x, y, c = _coords()
    me, sibling = (x, y, c), (x, y, 1 - c)
    chips = [(1 - x, y), (x, 1 - y), (1 - x, 1 - y)]
    if not relayed:
        _ag_relay(srcs, dsts, sems)
    plan = _ag_plan(srcs, dsts, sems)
    for b, src, slot, copy in plan:
        copy(0, sibling, me).wait_recv()
        for j, chip in enumerate(chips):
            copy(4 + j, (*chip, 1 - c), me).wait_recv()
        copy(0, me, sibling, src).wait_send()
        for j, chip in enumerate(chips):
            copy(1 + j, me, (*chip, c), src).wait_send()
            copy(4 + j, (*chip, c), sibling).wait_send()
        pltpu.make_async_copy(src, slot(*me), sems[2].at[b]).wait()


def _a2a_copies(srcs, dsts, sems):
    send_sems, recv_sems, local_sems = sems
    x, y, c = _coords()
    me = 4 * x + 2 * y + c
    local, remote = [], []
    for b, (src, dst) in enumerate(zip(srcs, dsts)):
        for rel in range(1, N_DEV):
            px = 1 - x if rel & 4 else x
            py = 1 - y if rel & 2 else y
            pc = 1 - c if rel & 1 else c
            remote.append(pltpu.make_async_remote_copy(
                src_ref=src.at[4 * px + 2 * py + pc], dst_ref=dst.at[me], send_sem=send_sems.at[N_REL * b + rel - 1],
                recv_sem=recv_sems.at[N_REL * b + rel - 1], device_id=(px, py, pc), device_id_type=MESH))
        local.append(pltpu.make_async_copy(src.at[me], dst.at[me], local_sems.at[b]))
    return local, remote


def _a2a_start(srcs, dsts, sems):
    local, remote = _a2a_copies(srcs, dsts, sems)
    for d in local + remote:
        d.start()


def _a2a_finish(srcs, dsts, sems):
    local, remote = _a2a_copies(srcs, dsts, sems)
    for d in remote + local:
        d.wait()


class _Exchange:
    def __init__(self, kind, srcs):
        self.kind, self.srcs = kind, list(srcs)
        self.n = len(self.srcs)

    def out_shapes(self):
        if self.kind == "ag":
            return [jax.ShapeDtypeStruct((N_DEV,) + s.shape, s.dtype) for s in self.srcs]
        return [jax.ShapeDtypeStruct(s.shape, s.dtype) for s in self.srcs]

    def start(self, src_refs, dst_refs, sems):
        (_ag_start if self.kind == "ag" else _a2a_start)(src_refs, dst_refs, sems)

    def relay(self, src_refs, dst_refs, sems):
        if self.kind == "ag":
            _ag_relay(src_refs, dst_refs, sems)

    def finish(self, src_refs, dst_refs, sems, relayed=False):
        if self.kind == "ag":
            _ag_finish(src_refs, dst_refs, sems, relayed)
        else:
            _a2a_finish(src_refs, dst_refs, sems)


def _carry_call(body, name, grid, in_specs, out_specs, out_shape, scratch, semantics, args, exs):
    in_specs, out_specs, out_shape, scratch = list(in_specs), list(out_specs), list(out_shape), list(scratch)
    if not exs:
        return pl.pallas_call(body, name=name, grid=grid, in_specs=in_specs, out_specs=out_specs, out_shape=out_shape,
                              scratch_shapes=scratch, compiler_params=_cparams(semantics))(*args), []
    n_in, n_out, n_scr = len(in_specs), len(out_specs), len(scratch)
    n_ex = sum(e.n for e in exs)

    def wrapped(*refs):
        ins, refs = refs[:n_in], refs[n_in:]
        srcs, refs = refs[:n_ex], refs[n_ex:]
        outs, refs = refs[:n_out], refs[n_out:]
        dsts, refs = refs[:n_ex], refs[n_ex:]
        scr, sems = refs[:n_scr], refs[n_scr:]
        views, off = [], 0
        for i, e in enumerate(exs):
            views.append((srcs[off:off + e.n], dsts[off:off + e.n], sems[3 * i:3 * i + 3]))
            off += e.n
        first = last = late = None
        for axis, size in enumerate(grid):
            at0, at1 = pl.program_id(axis) == 0, pl.program_id(axis) == size - 1
            first = at0 if first is None else first & at0
            last = at1 if last is None else last & at1
            late = at1 if late is None else late & at0
        relay_early = grid[0] > 1 and all(e.kind == "ag" for e in exs)

        @pl.when(first)
        def _():
            for e, view in zip(exs, views):
                e.start(*view)

        if relay_early:
            @pl.when(late)
            def _():
                for e, view in zip(exs, views):
                    e.relay(*view)

        body(*ins, *outs, *scr)

        @pl.when(last)
        def _():
            for e, view in zip(exs, views):
                e.finish(*view, relayed=relay_early)

    res = pl.pallas_call(
        wrapped, name=name + "_x", grid=grid, in_specs=in_specs + [ANY] * n_ex, out_specs=out_specs + [ANY] * n_ex,
        out_shape=out_shape + [s for e in exs for s in e.out_shapes()],
        scratch_shapes=scratch + [s for e in exs for s in _sem_shapes(e.n)],
        compiler_params=_cparams(("arbitrary",) * len(grid)))(*args, *[s for e in exs for s in e.srcs])
    got, off = [], n_out
    for e in exs:
        got.append(list(res[off:off + e.n]))
        off += e.n
    return res[:n_out], got


def _exchange_call(name, exs):
    tot = sum(e.n for e in exs)

    def body(*refs):
        srcs, dsts, sems = refs[:tot], refs[tot:2 * tot], refs[2 * tot:]
        views, off = [], 0
        for i, e in enumerate(exs):
            views.append((srcs[off:off + e.n], dsts[off:off + e.n], sems[3 * i:3 * i + 3]))
            off += e.n
        for e, view in zip(exs, views):
            e.start(*view)
        for e, view in zip(exs, views):
            e.finish(*view)

    outs = pl.pallas_call(
        body, name=name, in_specs=[ANY] * tot, out_specs=[ANY] * tot,
        out_shape=[s for e in exs for s in e.out_shapes()],
        scratch_shapes=[s for e in exs for s in _sem_shapes(e.n)],
    )(*[s for e in exs for s in e.srcs])
    res, off = [], 0
    for e in exs:
        res.append(list(outs[off:off + e.n]))
        off += e.n
    return res


def _flash_call(body, name, exs, in_specs, out_specs, out_shape, scratch, n, args):
    return _carry_call(body, name, (MLA_HEADS // 2, n * (n + 1) // 2), in_specs, out_specs, out_shape, scratch,
                       ("parallel", "arbitrary"), args, exs)


def _tri_rows(s, n):
    at = [(s >= r * (r + 1) // 2).astype(jnp.int32) for r in range(1, n)]
    return sum(at), s - sum(a * r for a, r in zip(at, range(1, n)))


def _tri_cols(s, n):
    starts = [c * n - c * (c - 1) // 2 for c in range(n)]
    col = sum((s >= starts[c]).astype(jnp.int32) for c in range(1, n))
    start = sum(jnp.where(col == c, starts[c], 0) for c in range(n))
    return s - start + col, col


def _pair_rows(a):
    at = a.T
    return jnp.concatenate([at[0:1, :], at[MLA_V:MLA_V + 1, :], jnp.zeros((SUBLANES - 2, a.shape[0]), a.dtype)], axis=0)


def _lower_tri(t):
    return lax.broadcasted_iota(jnp.int32, (t, t), 0) >= lax.broadcasted_iota(jnp.int32, (t, t), 1)


def _upper_tri(t):
    return lax.broadcasted_iota(jnp.int32, (t, t), 1) >= lax.broadcasted_iota(jnp.int32, (t, t), 0)


def _flash_fwd(q, kt, v, exs=()):
    S = q.shape[0]
    t = min(T_ATT, S)
    n = S // t

    def body(q_ref, kt_ref, v_ref, o_ref, lse_ref, lse_t_ref, m_s, l_s, acc):
        qi, ki = _tri_rows(pl.program_id(1), n)
        lo = lax.broadcasted_iota(jnp.int32, (t, LANES), 1) < MLA_V

        @pl.when(ki == 0)
        def _():
            m_s[...] = jnp.full_like(m_s, NEG)
            l_s[...] = jnp.zeros_like(l_s)
            acc[...] = jnp.zeros_like(acc)

        keep = _lower_tri(t) | (ki < qi)
        vv = v_ref[...]
        heads = range(2)
        ss = [jnp.where(keep, _dot(q_ref[:, h * LANES:(h + 1) * LANES], kt_ref[h * LANES:(h + 1) * LANES, :]), NEG)
              for h in heads]
        m_prev = [m_s[h] for h in heads]
        l_prev = [l_s[h] for h in heads]
        m_new = [jnp.maximum(m_prev[h], jnp.max(ss[h], axis=1, keepdims=True)) for h in heads]
        al = [jnp.exp(m_prev[h] - m_new[h]) for h in heads]
        ps = [jnp.exp(ss[h] - m_new[h][:, :1]) for h in heads]
        l_new = [al[h] * l_prev[h] + jnp.sum(ps[h], axis=1, keepdims=True) for h in heads]
        pv = [_dot(ps[h].astype(BF), vv) for h in heads]
        for h in heads:
            m_s[h] = m_new[h]
            l_s[h] = l_new[h]
        acc[...] = jnp.where(lo, al[0], al[1]) * acc[...] + jnp.where(lo, pv[0], pv[1])

        @pl.when(ki == qi)
        def _():
            o_ref[...] = acc[...] / jnp.where(lo, l_s[0], l_s[1])
            lse = jnp.where(lo, m_s[0] + jnp.log(l_s[0]), m_s[1] + jnp.log(l_s[1]))
            lse_ref[0] = lse
            lse_t_ref[0] = _pair_rows(lse)

    return _flash_call(
        body, "mla_flash_fwd", exs,
        [pl.BlockSpec((t, 256), lambda p, s: (_tri_rows(s, n)[0], p)),
         pl.BlockSpec((256, t), lambda p, s: (p, _tri_rows(s, n)[1])),
         pl.BlockSpec((t, 128), lambda p, s: (_tri_rows(s, n)[1], p))],
        [pl.BlockSpec((t, 128), lambda p, s: (_tri_rows(s, n)[0], p)),
         pl.BlockSpec((1, t, 128), lambda p, s: (p, _tri_rows(s, n)[0], 0)),
         pl.BlockSpec((1, SUBLANES, t), lambda p, s: (p, 0, _tri_rows(s, n)[0]))],
        [jax.ShapeDtypeStruct((S, 512), F32), jax.ShapeDtypeStruct((MLA_HEADS // 2, S, 128), F32),
         jax.ShapeDtypeStruct((MLA_HEADS // 2, SUBLANES, S), F32)],
        [pltpu.VMEM((2, t, 128), F32), pltpu.VMEM((2, t, 128), F32), pltpu.VMEM((t, 128), F32)], n, (q, kt, v))


def _flash_bwd_dq(q, k, kt, vt, do, lse, delta, exs=()):
    S = q.shape[0]
    t = min(T_ATT, S)
    n = S // t

    def body(q_ref, k_ref, kt_ref, vt_ref, do_ref, lse_ref, dl_ref, dq_ref, acc):
        qi, ki = _tri_rows(pl.program_id(1), n)
        lo = lax.broadcasted_iota(jnp.int32, (t, LANES), 1) < MLA_V

        @pl.when(ki == 0)
        def _():
            acc[...] = jnp.zeros_like(acc)

        keep = _lower_tri(t) | (ki < qi)
        heads = range(2)
        cs = [slice(h * LANES, (h + 1) * LANES) for h in heads]
        col = [slice(h * MLA_V, h * MLA_V + 1) for h in heads]
        lse, dl, dov, vt = lse_ref[0], dl_ref[0], do_ref[...], vt_ref[...]
        ss = [jnp.where(keep, _dot(q_ref[:, cs[h]], kt_ref[cs[h], :]), NEG) for h in heads]
        dp = [_dot(jnp.where(lo if h == 0 else ~lo, dov, 0).astype(BF), vt) for h in heads]
        ds = [(jnp.exp(ss[h] - lse[:, col[h]]) * (dp[h] - dl[:, col[h]])).astype(BF) for h in heads]
        dq = [_dot(ds[h], k_ref[:, cs[h]]) for h in heads]
        acc[...] += jnp.concatenate(dq, axis=1)

        @pl.when(ki == qi)
        def _():
            dq_ref[...] = acc[...]

    (dq,), got = _flash_call(
        body, "mla_flash_dq", exs,
        [pl.BlockSpec((t, 256), lambda p, s: (_tri_rows(s, n)[0], p)),
         pl.BlockSpec((t, 256), lambda p, s: (_tri_rows(s, n)[1], p)),
         pl.BlockSpec((256, t), lambda p, s: (p, _tri_rows(s, n)[1])),
         pl.BlockSpec((128, t), lambda p, s: (p, _tri_rows(s, n)[1])),
         pl.BlockSpec((t, 128), lambda p, s: (_tri_rows(s, n)[0], p)),
         pl.BlockSpec((1, t, 128), lambda p, s: (p, _tri_rows(s, n)[0], 0)),
         pl.BlockSpec((1, t, 128), lambda p, s: (p, _tri_rows(s, n)[0], 0))],
        [pl.BlockSpec((t, 256), lambda p, s: (_tri_rows(s, n)[0], p))],
        [jax.ShapeDtypeStruct((S, 1024), F32)],
        [pltpu.VMEM((t, 256), F32)], n, (q, k, kt, vt, do, lse, delta))
    return dq, got


def _flash_bwd_dkv(q, qt, k, v, do, dot_, lse_t, delta_t, exs=()):
    S = q.shape[0]
    t = min(T_ATT, S)
    n = S // t

    def body(q_ref, qt_ref, k_ref, v_ref, do_ref, dot_ref, lse_ref, dl_ref, dk_ref, dv_ref, dk_acc, dv_acc):
        qi, ki = _tri_cols(pl.program_id(1), n)
        lo = lax.broadcasted_iota(jnp.int32, (t, LANES), 1) < MLA_V
        top = lax.broadcasted_iota(jnp.int32, (LANES, t), 0) < MLA_V

        @pl.when(qi == ki)
        def _():
            dk_acc[...] = jnp.zeros_like(dk_acc)
            dv_acc[...] = jnp.zeros_like(dv_acc)

        keep = _upper_tri(t) | (qi > ki)
        heads = range(2)
        cs = [slice(h * LANES, (h + 1) * LANES) for h in heads]
        vv, lse, dl, dov, dot_v = v_ref[...], lse_ref[0], dl_ref[0], do_ref[...], dot_ref[...]
        st = [jnp.where(keep, _dot(k_ref[:, cs[h]], qt_ref[cs[h], :]), NEG) for h in heads]
        dpt = [_dot(vv, jnp.where(top if h == 0 else ~top, dot_v, 0).astype(BF)) for h in heads]
        pt = [jnp.exp(st[h] - lse[h:h + 1, :]) for h in heads]
        dst = [(pt[h] * (dpt[h] - dl[h:h + 1, :])).astype(BF) for h in heads]
        dv = [_dot(pt[h].astype(BF), jnp.where(lo if h == 0 else ~lo, dov, 0).astype(BF)) for h in heads]
        dk = [_dot(dst[h], q_ref[:, cs[h]]) for h in heads]
        dv_acc[...] += dv[0] + dv[1]
        dk_acc[...] += jnp.concatenate(dk, axis=1)

        @pl.when(qi == n - 1)
        def _():
            dk_ref[...] = dk_acc[...]
            dv_ref[...] = dv_acc[...]

    (dk, dv), got = _flash_call(
        body, "mla_flash_dkv", exs,
        [pl.BlockSpec((t, 256), lambda p, s: (_tri_cols(s, n)[0], p)),
         pl.BlockSpec((256, t), lambda p, s: (p, _tri_cols(s, n)[0])),
         pl.BlockSpec((t, 256), lambda p, s: (_tri_cols(s, n)[1], p)),
         pl.BlockSpec((t, 128), lambda p, s: (_tri_cols(s, n)[1], p)),
         pl.BlockSpec((t, 128), lambda p, s: (_tri_cols(s, n)[0], p)),
         pl.BlockSpec((128, t), lambda p, s: (p, _tri_cols(s, n)[0])),
         pl.BlockSpec((1, SUBLANES, t), lambda p, s: (p, 0, _tri_cols(s, n)[0])),
         pl.BlockSpec((1, SUBLANES, t), lambda p, s: (p, 0, _tri_cols(s, n)[0]))],
        [pl.BlockSpec((t, 256), lambda p, s: (_tri_cols(s, n)[1], p)),
         pl.BlockSpec((t, 128), lambda p, s: (_tri_cols(s, n)[1], p))],
        [jax.ShapeDtypeStruct((S, 1024), F32), jax.ShapeDtypeStruct((S, 512), F32)],
        [pltpu.VMEM((t, 256), F32), pltpu.VMEM((t, 128), F32)], n, (q, qt, k, v, do, dot_, lse_t, delta_t))
    return dk, dv, got


def _mem_heads(qm, km_ref, vm_ref):
    ps, os_ = [], []
    for h in range(X_HEADS):
        cs = slice(h * X_HEAD_DIM, (h + 1) * X_HEAD_DIM)
        s = _dot_nt(qm[:, cs].astype(BF), km_ref[:, cs]) * MEM_SCALE
        e = jnp.exp(s - jnp.max(s, axis=1, keepdims=True))
        p = e / jnp.sum(e, axis=1, keepdims=True)
        ps.append(p)
        os_.append(_dot(p.astype(BF), vm_ref[:, cs]))
    return ps, jnp.concatenate(os_, axis=1)


def _mem_fwd(proj, km, vm, p_mem):
    S = proj.shape[0]
    tr = min(T_ROWS, S)
    M = km.shape[0]

    def body(q_ref, z_ref, km_ref, vm_ref, p_ref, o_ref):
        _, o = _mem_heads(q_ref[...], km_ref, vm_ref)
        y = o * _silu(z_ref[...].astype(F32))
        o_ref[...] = _dot(y.astype(BF), p_ref[...]).astype(o_ref.dtype)

    return pl.pallas_call(
        body, name="mem_fwd", grid=(S // tr,),
        in_specs=[_rows(tr, 512, 4), _rows(tr, 512, 5), _full((M, 512)), _full((M, 512)), p_mem.spec()],
        out_specs=_rows(tr, 1024), out_shape=jax.ShapeDtypeStruct((S, D_MODEL), ACT),
        compiler_params=_cparams(("parallel",)),
    )(proj, proj, km, vm, p_mem.arr)


def _mem_bwd(do, proj, km, vm, p_mem):
    S = proj.shape[0]
    tr = min(T_ROWS_BWD_WIDE, S)
    M = km.shape[0]

    def body(do_ref, q_ref, z_ref, km_ref, vm_ref, p_ref, dq_ref, dz_ref, dkm_ref, dvm_ref, dp_ref):
        @pl.when(pl.program_id(0) == 0)
        def _():
            dkm_ref[...] = jnp.zeros_like(dkm_ref)
            dvm_ref[...] = jnp.zeros_like(dvm_ref)
            dp_ref[...] = jnp.zeros_like(dp_ref)

        qm = q_ref[...]
        z = z_ref[...].astype(F32)
        ps, o = _mem_heads(qm, km_ref, vm_ref)
        sz = _silu(z)
        y = o * sz
        dob = do_ref[...].astype(BF)
        dy = _dot_nt(dob, p_ref[...])
        dp_ref[...] += _dot_tn(y.astype(BF), dob)
        dz_ref[...] = (dy * o * _dsilu(z)).astype(dz_ref.dtype)
        d_o = dy * sz
        for h in range(X_HEADS):
            cs = slice(h * X_HEAD_DIM, (h + 1) * X_HEAD_DIM)
            doh = d_o[:, cs]
            dohb = doh.astype(BF)
            p = ps[h]
            dpr = _dot_nt(dohb, vm_ref[:, cs])
            ds = (p * (dpr - jnp.sum(doh * o[:, cs], axis=1, keepdims=True)) * MEM_SCALE).astype(BF)
            dq_ref[:, cs] = _dot(ds, km_ref[:, cs]).astype(dq_ref.dtype)
            dkm_ref[:, cs] += _dot_tn(ds, qm[:, cs].astype(BF))
            dvm_ref[:, cs] += _dot_tn(p.astype(BF), dohb)

    return pl.pallas_call(
        body, name="mem_bwd", grid=(S // tr,),
        in_specs=[_rows(tr, 1024), _rows(tr, 512, 4), _rows(tr, 512, 5), _full((M, 512)), _full((M, 512)),
                  p_mem.spec()],
        out_specs=(_rows(tr, 512), _rows(tr, 512), _full((M, 512)), _full((M, 512)), _full((512, 1024))),
        out_shape=(jax.ShapeDtypeStruct((S, 512), BF), jax.ShapeDtypeStruct((S, 512), BF),
                   jax.ShapeDtypeStruct((M, 512), F32), jax.ShapeDtypeStruct((M, 512), F32),
                   jax.ShapeDtypeStruct((512, 1024), F32)),
        compiler_params=_cparams(("arbitrary",)),
    )(do, proj, proj, km, vm, p_mem.arr)


def _merge_fwd(x, proj, o_ssm, o_att, o_mem, b_gate, p_mla, w_out, ln_g, ln_b):
    S = x.shape[0]
    tr = min(T_ROWS, S)

    def body(x_ref, lg_ref, z_ref, os_ref, oa_ref, om_ref, bg_ref, p_ref, w_ref, g_ref, b_ref,
             xn_ref, xb_ref, pre_ref, mg_ref):
        gates = _sigmoid(lg_ref[...].astype(F32) + bg_ref[0])
        ya = oa_ref[...] * _silu(z_ref[...].astype(F32))
        o_mla = _dot(ya.astype(BF), p_ref[...])
        merged = (gates[:, :D_MODEL] * os_ref[...].astype(F32) + gates[:, D_MODEL:2 * D_MODEL] * o_mla
                  + gates[:, 2 * D_MODEL:] * om_ref[...].astype(F32))
        mb = merged.astype(BF)
        mg_ref[...] = mb
        pre = ALPHA * x_ref[...] + _dot(mb, w_ref[...])
        pre_ref[...] = pre
        mu = jnp.mean(pre, axis=-1, keepdims=True)
        xc = pre - mu
        var = jnp.mean(xc * xc, axis=-1, keepdims=True)
        xn = xc * lax.rsqrt(var + NORM_EPS) * g_ref[0] + b_ref[0]
        xn_ref[...] = xn
        xb_ref[...] = xn.astype(BF)

    return pl.pallas_call(
        body, name="merge_fwd", grid=(S // tr,),
        in_specs=[_rows(tr, 1024), _rows(tr, 3072, 1), _rows(tr, 512, 3), _rows(tr, 1024), _rows(tr, 512),
                  _rows(tr, 1024), b_gate.spec(), p_mla.spec(), _full((1024, 1024)), ln_g.spec(), ln_b.spec()],
        out_specs=(_rows(tr, 1024), _rows(tr, 1024), _rows(tr, 1024), _rows(tr, 1024)),
        out_shape=(jax.ShapeDtypeStruct((S, 1024), F32), jax.ShapeDtypeStruct((S, 1024), BF),
                   jax.ShapeDtypeStruct((S, 1024), F32), jax.ShapeDtypeStruct((S, 1024), BF)),
        compiler_params=_cparams(("parallel",)),
    )(x, proj, proj, o_ssm, o_att, o_mem, b_gate.arr, p_mla.arr, w_out, ln_g.arr, ln_b.arr)


def _merge_bwd(dxn, pre, merged, proj, o_ssm, o_att, o_mem, b_gate, p_mla, w_out, ln_g):
    S = pre.shape[0]
    tr = min(T_ROWS_BWD, S)

    def body(dxn_ref, pre_ref, mg_ref, lg_ref, z_ref, os_ref, oa_ref, om_ref, bg_ref, p_ref, w_ref, g_ref,
             dxr_ref, dlg_ref, dos_ref, dom_ref, doa_ref, dz_ref, doat_ref, dl_ref, dlt_ref, dw_ref, dp_ref, dbg_ref,
             dg_ref, db_ref):
        @pl.when(pl.program_id(0) == 0)
        def _():
            dw_ref[...] = jnp.zeros_like(dw_ref)
            dp_ref[...] = jnp.zeros_like(dp_ref)
            dbg_ref[...] = jnp.zeros_like(dbg_ref)
            dg_ref[...] = jnp.zeros_like(dg_ref)
            db_ref[...] = jnp.zeros_like(db_ref)

        dxn = dxn_ref[...]
        pre = pre_ref[...]
        mu = jnp.mean(pre, axis=-1, keepdims=True)
        xc = pre - mu
        rstd = lax.rsqrt(jnp.mean(xc * xc, axis=-1, keepdims=True) + NORM_EPS)
        xhat = xc * rstd
        dg_ref[...] += jnp.sum(dxn * xhat, axis=0, keepdims=True)
        db_ref[...] += jnp.sum(dxn, axis=0, keepdims=True)
        dxh = dxn * g_ref[0]
        dpre = rstd * (dxh - jnp.mean(dxh, axis=-1, keepdims=True)
                       - xhat * jnp.mean(dxh * xhat, axis=-1, keepdims=True))
        dxr_ref[...] = ALPHA * dpre
        dpb = dpre.astype(BF)
        dw_ref[...] += _dot_tn(mg_ref[...], dpb)
        dm = _dot_nt(dpb, w_ref[...])

        gates = _sigmoid(lg_ref[...].astype(F32) + bg_ref[0])
        g0, g1, g2 = gates[:, :D_MODEL], gates[:, D_MODEL:2 * D_MODEL], gates[:, 2 * D_MODEL:]
        z = z_ref[...].astype(F32)
        oa = oa_ref[...]
        sz = _silu(z)
        ya = (oa * sz).astype(BF)
        o_mla = _dot(ya, p_ref[...])
        dos_ref[...] = (g0 * dm).astype(dos_ref.dtype)
        dom_ref[...] = (g2 * dm).astype(dom_ref.dtype)
        do_mla = (g1 * dm).astype(BF)
        dl0 = dm * os_ref[...].astype(F32) * g0 * (1.0 - g0)
        dl1 = dm * o_mla * g1 * (1.0 - g1)
        dl2 = dm * om_ref[...].astype(F32) * g2 * (1.0 - g2)
        dl = jnp.concatenate([dl0, dl1, dl2], axis=1)
        dbg_ref[...] += jnp.sum(dl, axis=0, keepdims=True)
        dlg_ref[...] = dl.astype(dlg_ref.dtype)
        dp_ref[...] += _dot_tn(ya, do_mla)
        dya = _dot_nt(do_mla, p_ref[...])
        doa = dya * sz
        doab = doa.astype(BF)
        doa_ref[...] = doab
        doat_ref[...] = doab.T
        dz_ref[...] = (dya * oa * _dsilu(z)).astype(dz_ref.dtype)
        prod = doa * oa
        lo = lax.broadcasted_iota(jnp.int32, (tr, LANES), 1) < MLA_V
        for pr in range(MLA_HEADS // 2):
            blk = prod[:, pr * LANES:(pr + 1) * LANES]
            d0 = jnp.sum(jnp.where(lo, blk, 0.0), axis=1, keepdims=True)
            d1 = jnp.sum(jnp.where(lo, 0.0, blk), axis=1, keepdims=True)
            dl = jnp.where(lo, d0, d1)
            dl_ref[pr] = dl
            dlt_ref[pr] = _pair_rows(dl)

    return pl.pallas_call(
        body, name="merge_bwd", grid=(S // tr,),
        in_specs=[_rows(tr, 1024), _rows(tr, 1024), _rows(tr, 1024), _rows(tr, 3072, 1), _rows(tr, 512, 3),
                  _rows(tr, 1024), _rows(tr, 512), _rows(tr, 1024), b_gate.spec(), p_mla.spec(),
                  _full((1024, 1024)), ln_g.spec()],
        out_specs=(_rows(tr, 1024), _rows(tr, 3072), _rows(tr, 1024), _rows(tr, 1024), _rows(tr, 512),
                   _rows(tr, 512), _cols(512, tr), pl.BlockSpec((MLA_HEADS // 2, tr, 128), lambda i: (0, i, 0)),
                   pl.BlockSpec((MLA_HEADS // 2, SUBLANES, tr), lambda i: (0, 0, i)),
                   _full((1024, 1024)), _full((512, 1024)), _full((1, 3072)), _full((1, 1024)), _full((1, 1024))),
        out_shape=(jax.ShapeDtypeStruct((S, 1024), F32), jax.ShapeDtypeStruct((S, 3072), BF),
                   jax.ShapeDtypeStruct((S, 1024), BF), jax.ShapeDtypeStruct((S, 1024), BF),
                   jax.ShapeDtypeStruct((S, 512), BF), jax.ShapeDtypeStruct((S, 512), BF),
                   jax.ShapeDtypeStruct((512, S), BF), jax.ShapeDtypeStruct((MLA_HEADS // 2, S, 128), F32),
                   jax.ShapeDtypeStruct((MLA_HEADS // 2, SUBLANES, S), F32),
                   jax.ShapeDtypeStruct((1024, 1024), F32), jax.ShapeDtypeStruct((512, 1024), F32),
                   jax.ShapeDtypeStruct((1, 3072), F32), jax.ShapeDtypeStruct((1, 1024), F32),
                   jax.ShapeDtypeStruct((1, 1024), F32)),
        compiler_params=_cparams(("arbitrary",)),
    )(dxn, pre, merged, proj, proj, o_ssm, o_att, o_mem, b_gate.arr, p_mla.arr, w_out, ln_g.arr)


def _loss_head(y, t):
    S = y.shape[0]
    tr = min(T_ROWS, S)
    n = S // tr

    def body(y_ref, t_ref, dy_ref, l_ref, acc):
        i = pl.program_id(0)

        @pl.when(i == 0)
        def _():
            acc[...] = jnp.zeros_like(acc)

        e = y_ref[...] - t_ref[...]
        dy_ref[...] = e * (1.0 / D_MODEL)
        acc[...] += jnp.sum(e * e, axis=0, keepdims=True)

        @pl.when(i == n - 1)
        def _():
            tot = jnp.sum(acc[...], axis=1, keepdims=True) * (0.5 / D_MODEL)
            l_ref[...] = jnp.broadcast_to(tot, l_ref.shape)

    return pl.pallas_call(
        body, name="loss_head", grid=(n,),
        in_specs=[_rows(tr, 1024), _rows(tr, 1024)],
        out_specs=(_rows(tr, 1024), _full((SUBLANES, LANES))),
        out_shape=(jax.ShapeDtypeStruct((S, 1024), F32), jax.ShapeDtypeStruct((SUBLANES, LANES), F32)),
        scratch_shapes=[pltpu.VMEM((1, 1024), F32)],
        compiler_params=_cparams(("arbitrary",)),
    )(y, t)


def _rope_tables(pos):
    inv_freq = ROPE_THETA ** (-jnp.arange(0, MLA_ROPE, 2, dtype=F32) / MLA_ROPE)
    ang = pos.astype(F32)[:, None] * inv_freq
    cos, sin = jnp.cos(ang), jnp.sin(ang)
    S = pos.shape[0]
    half = MLA_ROPE // 2
    ones = jnp.ones((S, MLA_NOPE), F32)
    z16 = jnp.zeros((S, half), F32)
    z32 = jnp.zeros((S, LANES - MLA_NOPE - MLA_ROPE), F32)
    z64 = jnp.zeros((S, MLA_NOPE), F32)
    c = jnp.concatenate([ones, cos, cos, z32], axis=1)
    sa = jnp.concatenate([z64, -sin, z16, z32], axis=1)
    sb = jnp.concatenate([z64, z16, sin, z32], axis=1)
    return c, sa, sb


def _ssm_discretise(a_re, a_im, log_dt, b_re, b_im):
    dt = jnp.exp(log_dt)[..., None]
    mag = jnp.exp(a_re * dt)
    lb_re = mag * jnp.cos(a_im * dt)
    lb_im = mag * jnp.sin(a_im * dt)
    nr, ni = lb_re - 1.0, lb_im
    den = a_re * a_re + a_im * a_im
    f_re = (nr * a_re + ni * a_im) / den
    f_im = (ni * a_re - nr * a_im) / den
    bb_re = f_re[..., None] * b_re - f_im[..., None] * b_im
    bb_im = f_re[..., None] * b_im + f_im[..., None] * b_re
    return lb_re, lb_im, bb_re, bb_im


_GPB = SSM_CB // SSM_GROUP


def _bd_in(bb):
    nb = SSM_GROUPS // _GPB
    t = bb.reshape(nb, _GPB, SSM_STATE, SSM_GROUP)
    eye = jnp.eye(_GPB, dtype=bb.dtype)
    return jnp.einsum("ngpc,gh->ngchp", t, eye).reshape(nb, SSM_CB, _GPB * SSM_STATE)


def _bd_in_t(d):
    nb = SSM_GROUPS // _GPB
    t = d.reshape(nb, _GPB, SSM_GROUP, _GPB, SSM_STATE)
    eye = jnp.eye(_GPB, dtype=d.dtype)
    return jnp.einsum("ngchp,gh->ngpc", t, eye).reshape(SSM_GROUPS, SSM_STATE, SSM_GROUP)


def _bd_out(c):
    nb = SSM_GROUPS // _GPB
    t = c.reshape(nb, _GPB, SSM_GROUP, SSM_STATE)
    eye = jnp.eye(_GPB, dtype=c.dtype)
    return jnp.einsum("ngcp,gh->ngphc", t, eye).reshape(nb, _GPB * SSM_STATE, SSM_CB)


def _interleave(a):
    S, w = a.shape
    return a.reshape(SUBLANES, S // SUBLANES, w).transpose(1, 0, 2).reshape(S, w)


def _deinterleave(a):
    S, w = a.shape
    return a.reshape(S // SUBLANES, SUBLANES, w).transpose(1, 0, 2).reshape(S, w)


IN_SHARD = D_IN // N_DEV
ROPE_OWNER = ROPE_SLOT_LO // IN_SHARD
assert ROPE_OWNER * IN_SHARD <= ROPE_SLOT_LO and ROPE_SLOT_LO + MLA_ROPE <= (ROPE_OWNER + 1) * IN_SHARD


def _w_in_from_shards(g):
    pieces = []
    for j in range(N_DEV):
        if j == ROPE_OWNER:
            a = ROPE_SLOT_LO - j * IN_SHARD
            z = lambda n: jnp.zeros((g.shape[1], n), g.dtype)
            pieces += [g[j][:, :a], z(MLA_NOPE), g[j][:, a:a + MLA_ROPE], z(LANES - MLA_NOPE - MLA_ROPE),
                       g[j][:, a + MLA_ROPE:]]
        else:
            pieces.append(g[j])
    return jnp.concatenate(pieces, axis=1)


def _w_in_to_shards(d):
    shift = LANES - MLA_ROPE
    out = []
    for j in range(N_DEV):
        lo, hi = j * IN_SHARD, (j + 1) * IN_SHARD
        if j < ROPE_OWNER:
            out.append(d[:, lo:hi])
        elif j > ROPE_OWNER:
            out.append(d[:, lo + shift:hi + shift])
        else:
            r = ROPE_SLOT_LO + MLA_NOPE
            out.append(jnp.concatenate([d[:, lo:ROPE_SLOT_LO], d[:, r:r + MLA_ROPE],
                                        d[:, ROPE_SLOT_LO + LANES:hi + shift]], axis=1))
    return jnp.stack(out)


def _adamw_math(w, g, m, v):
    m = ADAM_B1 * m + (1.0 - ADAM_B1) * g
    v = ADAM_B2 * v + (1.0 - ADAM_B2) * (g * g)
    m_hat = m / (1.0 - ADAM_B1 ** ADAM_STEP)
    v_hat = v / (1.0 - ADAM_B2 ** ADAM_STEP)
    delta = -ADAM_LR * (m_hat / (jnp.sqrt(v_hat) + ADAM_EPS) + ADAM_WD * w)
    return delta, m, v


def _adamw_sharded(parts, w, m, v, tile, name):
    L, _, R, C = parts.shape
    assert R % tile == 0

    def body(p_ref, w_ref, m_ref, v_ref, g_out, d_out, m_out, v_out):
        g = p_ref[0, 0].astype(F32)
        for k in range(1, N_DEV):
            g = g + p_ref[0, k].astype(F32)
        d, mn, vn = _adamw_math(w_ref[0], g, m_ref[0], v_ref[0])
        g_out[0] = g
        d_out[0] = d
        m_out[0] = mn
        v_out[0] = vn

    spec = pl.BlockSpec((1, tile, C), lambda l, i: (l, i, 0))
    shp = jax.ShapeDtypeStruct((L, R, C), F32)
    return pl.pallas_call(
        body, name=name, grid=(L, R // tile),
        in_specs=[pl.BlockSpec((1, N_DEV, tile, C), lambda l, i: (l, 0, i, 0)), spec, spec, spec],
        out_specs=(spec,) * 4, out_shape=(shp,) * 4, compiler_params=_cparams(("parallel", "parallel")),
    )(parts, w, m, v)


COL_GROUP = (("w_glu", 512), ("p_ssm", 512), ("p_mla", 512), ("p_mem", 512), ("w_uq", 256), ("w_ukv", 128))
COL_AT = {n: sum(r for _, r in COL_GROUP[:i]) // rows for i, (n, rows) in enumerate(COL_GROUP)}
assert all(sum(r for _, r in COL_GROUP[:i]) % rows == 0 for i, (_, rows) in enumerate(COL_GROUP))
COL_ROWS = dict(COL_GROUP)
ROW_GROUP = ("w_mem_kv", "w_out")
SMALL = ("b_gate", "ssm_a_re", "ssm_a_im", "ssm_log_dt", "ssm_b_re", "ssm_b_im", "ssm_c_re", "ssm_c_im", "ssm_d",
         "b_glu", "mla_q_norm", "mla_kv_norm", "ln_g", "ln_b")
UQ_COLS = MLA_NOPE + MLA_ROPE


def _pad_lanes(a):
    return jnp.concatenate([a, jnp.zeros(a.shape[:-1] + (LANES - a.shape[-1],), a.dtype)], axis=-1)


def _group_buffers(d, dtype):
    col = jnp.concatenate([_pad_lanes(d[n]) if n == "w_uq" else d[n] for n, _ in COL_GROUP], axis=1)
    row = jnp.concatenate([d[n] for n in ROW_GROUP], axis=1)
    return d["w_in"].astype(dtype), col.astype(dtype), row.astype(dtype)


def _ungroup(bufs):
    b_in, col, row = bufs
    out, off = {"w_in": b_in}, 0
    for n, rows in COL_GROUP:
        t = col[:, off:off + rows]
        out[n] = t[..., :UQ_COLS] if n == "w_uq" else t
        off += rows
    k = row.shape[1] // 2
    out["w_mem_kv"], out["w_out"] = row[:, :k], row[:, k:]
    return out


def _colcat(t):
    return t.transpose(1, 0, 2).reshape(t.shape[1], -1)


def _colsplit(g, n):
    return g.reshape(g.shape[0], N_DEV, n).transpose(1, 0, 2)


def _unpack_weights(g_in, g_col, g_row):
    wc = _colcat(g_col)
    at = lambda n: _RowBlock(wc, COL_ROWS[n], COL_AT[n])
    lo = COL_AT["w_ukv"] * COL_ROWS["w_ukv"]
    ukv = wc[lo:lo + COL_ROWS["w_ukv"]].reshape(-1, MLA_HEADS, LANES)
    lane = lax.broadcasted_iota(jnp.int32, ukv.shape, 2)
    k = g_row.shape[1] // 2
    return dict(
        w_in=_w_in_from_shards(g_in), w_glu=at("w_glu"), w_uq=at("w_uq"), p_ssm=at("p_ssm"), p_mla=at("p_mla"),
        p_mem=at("p_mem"), w_k=jnp.where(lane < MLA_NOPE, ukv, jnp.zeros_like(ukv)).reshape(ukv.shape[0], -1),
        w_v=ukv[..., MLA_NOPE:].reshape(ukv.shape[0], -1),
        w_mem_kv=g_row[:, :k].reshape(-1, g_row.shape[2]), w_out=g_row[:, k:].reshape(-1, g_row.shape[2]))


def _pack_grads_in(d_w_in):
    return _w_in_to_shards(d_w_in).astype(BF)


def _pack_grads_rest(d):
    ukv = jnp.concatenate([d["w_k"].reshape(-1, MLA_HEADS, LANES)[..., :MLA_NOPE],
                           d["w_v"].reshape(-1, MLA_HEADS, MLA_V)], axis=-1).reshape(d["w_k"].shape[0], -1)
    col = jnp.concatenate([ukv if n == "w_ukv" else d[n] for n, _ in COL_GROUP], axis=0)
    row = jnp.concatenate([d[n].reshape(N_DEV, -1, d[n].shape[1]) for n in ROW_GROUP], axis=1)
    return [_colsplit(col, LANES).astype(BF), row.astype(BF)]


def _pack_small(d, lead):
    parts = []
    for n in SMALL:
        keep = d[n].shape[:lead]
        f = d[n].reshape(keep + (-1,))
        pad = (-f.shape[-1]) % (SUBLANES * LANES)
        if pad:
            f = jnp.concatenate([f, jnp.zeros(keep + (pad,), f.dtype)], axis=-1)
        parts.append(f.reshape(keep + (-1, LANES)))
    return jnp.concatenate(parts, axis=lead)


def _unpack_small(buf, like):
    out, off = {}, 0
    for n in SMALL:
        size = math.prod(like[n].shape[1:])
        rows = -(-size // (SUBLANES * LANES)) * SUBLANES
        out[n] = buf[:, off:off + rows].reshape(buf.shape[0], -1)[:, :size].reshape(like[n].shape)
        off += rows
    return out


WEIGHTS = ("w_in", "b_gate", "ssm_a_re", "ssm_a_im", "ssm_log_dt", "ssm_b_re", "ssm_b_im", "ssm_c_re", "ssm_c_im",
           "ssm_d", "w_glu", "b_glu", "mla_q_norm", "w_uq", "mla_kv_norm", "w_ukv", "w_mem_kv", "p_ssm", "p_mla",
           "p_mem", "w_out", "ln_g", "ln_b")
BIG = ("w_in",) + tuple(n for n, _ in COL_GROUP) + ROW_GROUP


def _train_step(x, mem, pos, target, wl, ws):
    S = x.shape[0]
    tc, tsa, tsb = _rope_tables(pos)
    loc = _group_buffers(wl, BF)
    loc = [[b[l] for b in loc] for l in range(DEPTH)]

    lb_re, lb_im, bb_re, bb_im = _ssm_discretise(ws["ssm_a_re"], ws["ssm_a_im"], ws["ssm_log_dt"], ws["ssm_b_re"],
                                                 ws["ssm_b_im"])
    nb = SSM_GROUPS // _GPB
    mats = (jax.vmap(_bd_in)(bb_re), jax.vmap(_bd_in)(bb_im), lb_re.reshape(DEPTH, nb, 1, -1),
            lb_im.reshape(DEPTH, nb, 1, -1), jax.vmap(_bd_out)(ws["ssm_c_re"]), jax.vmap(_bd_out)(ws["ssm_c_im"]),
            ws["ssm_d"].reshape(DEPTH, 1, -1))

    rows3 = {n: ws[n].reshape(DEPTH, 1, -1) for n in ("b_glu", "mla_q_norm", "mla_kv_norm", "b_gate", "ln_g", "ln_b")}

    def small(n, l):
        return _LayerRow(rows3[n], l)

    ((g_in,),) = _exchange_call("weights_gather_first", [_Exchange("ag", loc[0][:1])])
    W = [None] * DEPTH
    saved = []
    xs, xb = x, x.astype(BF)
    for l in range(DEPTH):
        if l == 0:
            proj, (g_rest,) = _mm(xb, _w_in_from_shards(g_in), name="proj_fwd", tm=S, tn=512, out_dtype=ACT,
                                  exs=[_Exchange("ag", loc[0][1:])])
            W[0] = _unpack_weights(g_in, *g_rest)
        else:
            proj = _mm(xb, W[l]["w_in"], name="proj_fwd", tm=S, tn=512, out_dtype=ACT)
        w = W[l]
        u_il = _interleave(proj[:, :SSM_WIDTH])
        y_raw = _deinterleave(_ssm_fwd(u_il, mats, l))
        o_ssm = _ssm_post_fwd(y_raw, proj, w["w_glu"], small("b_glu", l), w["p_ssm"])
        q, k, v, qt, kt, vt = _mla_pre_fwd(proj, small("mla_q_norm", l), small("mla_kv_norm", l), w["w_uq"], w["w_k"], w["w_v"],
                               tc, tsa, tsb)
        nxt = [_Exchange("ag", loc[l + 1])] if l + 1 < DEPTH else []
        (o_att, lse, lse_t), gathered = _flash_fwd(q, kt, v, nxt)
        if nxt:
            W[l + 1] = _unpack_weights(*gathered[0])
        kvm = _mm(mem, w["w_mem_kv"], name="memkv_fwd", out_dtype=BF)
        km, vm = kvm[:, :512], kvm[:, 512:]
        o_mem = _mem_fwd(proj, km, vm, w["p_mem"])
        xn, xnb, pre, merged = _merge_fwd(xs, proj, o_ssm, o_att, o_mem, small("b_gate", l), w["p_mla"], w["w_out"],
                                          small("ln_g", l), small("ln_b", l))
        saved.append(dict(xb=xb, proj=proj, u_il=u_il, y_raw=y_raw, o_ssm=o_ssm, q=q, k=k, v=v, qt=qt, kt=kt, vt=vt, o_att=o_att,
                          lse=lse, lse_t=lse_t,
                          km=km, vm=vm, o_mem=o_mem, pre=pre, merged=merged))
        xs, xb = xn, xnb

    dxs, lvec = _loss_head(xs, target)
    loss = lvec[0, 0]

    disc_names = ("ssm_a_re", "ssm_a_im", "ssm_log_dt", "ssm_b_re", "ssm_b_im")
    got = [None] * DEPTH
    got_small = [None] * DEPTH
    pending = None
    pending_small = None
    for l in reversed(range(DEPTH)):
        sv, w = saved[l], W[l]
        proj = sv["proj"]
        (dx_res, dlg, do_ssm, do_mem, do_att, dz_mla, do_att_t, delta, delta_t, d_w_out, d_p_mla, d_b_gate, d_ln_g,
         d_ln_b) = _merge_bwd(
            dxs, sv["pre"], sv["merged"], proj, sv["o_ssm"], sv["o_att"], sv["o_mem"], small("b_gate", l), w["p_mla"],
            w["w_out"], small("ln_g", l))
        dq_mem, dz_mem, d_km, d_vm, d_p_mem = _mem_bwd(do_mem, proj, sv["km"], sv["vm"], w["p_mem"])
        d_w_mem = _mm(mem, jnp.concatenate([d_km, d_vm], axis=1), name="memkv_bwd", ta=True)
        dq, arrived_rest = _flash_bwd_dq(
            sv["q"], sv["k"], sv["kt"], sv["vt"], do_att, sv["lse"], delta,
            [_Exchange("a2a", pending[1:]), _Exchange("ag", [pending_small])] if pending is not None else [])
        dk, dv, arrived_in = _flash_bwd_dkv(sv["q"], sv["qt"], sv["k"], sv["v"], do_att, do_att_t, sv["lse_t"], delta_t,
                                            [_Exchange("a2a", pending[:1])] if pending is not None else [])
        if pending is not None:
            got[l + 1] = arrived_in[0] + arrived_rest[0]
            got_small[l + 1] = arrived_rest[1][0]
        dcq, dckv, dslot, d_wuq, d_wk, d_wv, d_qn, d_kn = _mla_pre_bwd(
            dq, dk, dv, proj, small("mla_q_norm", l), small("mla_kv_norm", l), w["w_uq"], w["w_k"], w["w_v"],
            tc, tsa, tsb)
        dy_raw, dz_ssm, d_w_glu, d_b_glu, d_p_ssm = _ssm_post_bwd(do_ssm, sv["y_raw"], proj, w["w_glu"],
                                                                 small("b_glu", l), w["p_ssm"])
        rest = _pack_grads_rest(dict(w_glu=d_w_glu, w_uq=d_wuq, w_k=d_wk, w_v=d_wv, w_mem_kv=d_w_mem, p_ssm=d_p_ssm,
                                     p_mla=d_p_mla, p_mem=d_p_mem, w_out=d_w_out))
        (du_il, dbbre, dbbim, dare, daim, dcre, dcim, dd), early = _ssm_bwd(
            sv["u_il"], _interleave(dy_raw), mats, l, [_Exchange("a2a", rest)] if l == 0 else [])
        du = _deinterleave(du_il).astype(BF)
        _, disc_vjp = jax.vjp(_ssm_discretise, *[ws[n][l] for n in disc_names])
        d_disc = disc_vjp((dare.reshape(SSM_GROUPS, SSM_STATE), daim.reshape(SSM_GROUPS, SSM_STATE), _bd_in_t(dbbre),
                           _bd_in_t(dbbim)))
        dproj = jnp.concatenate([du, dz_ssm, dcq, dckv, dslot, dz_mla, dq_mem, dz_mem, dlg], axis=1)
        d_w_in = _mm(sv["xb"], dproj, name="proj_dw", ta=True, tm=1024, tn=512, tk=S)
        if l > 0:
            dxs = _mm(dproj, w["w_in"], name="proj_dx", tb=True, add=dx_res, tm=1024, tn=1024, tk=1024)
        pending = [_pack_grads_in(d_w_in)] + (rest if l > 0 else [])
        gsl = dict(zip(disc_names, d_disc))
        gsl.update(b_gate=d_b_gate, ssm_c_re=_bd_in_t(dcre).transpose(0, 2, 1), ssm_c_im=_bd_in_t(dcim).transpose(0, 2, 1),
                   ssm_d=dd, b_glu=d_b_glu, mla_q_norm=d_qn, mla_kv_norm=d_kn, ln_g=d_ln_g, ln_b=d_ln_b)
        pending_small = _pack_small(gsl, 0)

    dxs, (last_in, (got_small[0],)) = _mm(
        dproj, w["w_in"], name="proj_dx", tb=True, add=dx_res, tm=1024, tn=1024, tk=1024,
        exs=[_Exchange("a2a", pending), _Exchange("ag", [pending_small])])
    got[0] = last_in + early[0]
    return loss, dxs, got, got_small


def kernel(x, mem, positions, w_in, b_gate, ssm_a_re, ssm_a_im, ssm_log_dt, ssm_b_re, ssm_b_im, ssm_c_re, ssm_c_im, ssm_d, w_glu, b_glu, mla_q_norm, w_uq, mla_kv_norm, w_ukv, w_mem_kv, p_ssm, p_mla, p_mem, w_out, ln_g, ln_b, loss_target, m_w_in, m_b_gate, m_ssm_a_re, m_ssm_a_im, m_ssm_log_dt, m_ssm_b_re, m_ssm_b_im, m_ssm_c_re, m_ssm_c_im, m_ssm_d, m_w_glu, m_b_glu, m_mla_q_norm, m_w_uq, m_mla_kv_norm, m_w_ukv, m_w_mem_kv, m_p_ssm, m_p_mla, m_p_mem, m_w_out, m_ln_g, m_ln_b, v_w_in, v_b_gate, v_ssm_a_re, v_ssm_a_im, v_ssm_log_dt, v_ssm_b_re, v_ssm_b_im, v_ssm_c_re, v_ssm_c_im, v_ssm_d, v_w_glu, v_b_glu, v_mla_q_norm, v_w_uq, v_mla_kv_norm, v_w_ukv, v_w_mem_kv, v_p_ssm, v_p_mla, v_p_mem, v_w_out, v_ln_g, v_ln_b):
    w = dict(w_in=w_in, b_gate=b_gate, ssm_a_re=ssm_a_re, ssm_a_im=ssm_a_im, ssm_log_dt=ssm_log_dt, ssm_b_re=ssm_b_re,
             ssm_b_im=ssm_b_im, ssm_c_re=ssm_c_re, ssm_c_im=ssm_c_im, ssm_d=ssm_d, w_glu=w_glu, b_glu=b_glu,
             mla_q_norm=mla_q_norm, w_uq=w_uq, mla_kv_norm=mla_kv_norm, w_ukv=w_ukv, w_mem_kv=w_mem_kv, p_ssm=p_ssm,
             p_mla=p_mla, p_mem=p_mem, w_out=w_out, ln_g=ln_g, ln_b=ln_b)
    m = dict(w_in=m_w_in, b_gate=m_b_gate, ssm_a_re=m_ssm_a_re, ssm_a_im=m_ssm_a_im, ssm_log_dt=m_ssm_log_dt,
             ssm_b_re=m_ssm_b_re, ssm_b_im=m_ssm_b_im, ssm_c_re=m_ssm_c_re, ssm_c_im=m_ssm_c_im, ssm_d=m_ssm_d,
             w_glu=m_w_glu, b_glu=m_b_glu, mla_q_norm=m_mla_q_norm, w_uq=m_w_uq, mla_kv_norm=m_mla_kv_norm,
             w_ukv=m_w_ukv, w_mem_kv=m_w_mem_kv, p_ssm=m_p_ssm, p_mla=m_p_mla, p_mem=m_p_mem, w_out=m_w_out,
             ln_g=m_ln_g, ln_b=m_ln_b)
    v = dict(w_in=v_w_in, b_gate=v_b_gate, ssm_a_re=v_ssm_a_re, ssm_a_im=v_ssm_a_im, ssm_log_dt=v_ssm_log_dt,
             ssm_b_re=v_ssm_b_re, ssm_b_im=v_ssm_b_im, ssm_c_re=v_ssm_c_re, ssm_c_im=v_ssm_c_im, ssm_d=v_ssm_d,
             w_glu=v_w_glu, b_glu=v_b_glu, mla_q_norm=v_mla_q_norm, w_uq=v_w_uq, mla_kv_norm=v_mla_kv_norm,
             w_ukv=v_w_ukv, w_mem_kv=v_w_mem_kv, p_ssm=v_p_ssm, p_mla=v_p_mla, p_mem=v_p_mem, w_out=v_w_out,
             ln_g=v_ln_g, ln_b=v_ln_b)

    wl = {n: w[n] for n in BIG}
    small = {n: w[n] for n in SMALL}
    loss_local, dx, got, got_small = _train_step(x[0], mem[0], positions[0], loss_target[0], wl, small)
    loss = lax.psum(loss_local, ("x", "y", "c"))

    grads, delta, new_m, new_v = {}, {}, {}, {}
    wg = _group_buffers(wl, F32)
    mg = _group_buffers({n: m[n] for n in BIG}, F32)
    vg = _group_buffers({n: v[n] for n in BIG}, F32)
    res = []
    for i, (tile, tag) in enumerate(((256, "in"), (128, "col"), (256, "row"))):
        parts = jnp.stack([got[l][i] for l in range(DEPTH)])
        res.append(_adamw_sharded(parts, wg[i], mg[i], vg[i], tile, "adamw_" + tag))
    for dst, j in ((grads, 0), (delta, 1), (new_m, 2), (new_v, 3)):
        dst.update(_ungroup([r[j] for r in res]))

    sw, sm, sv = (_pack_small(small, 1), _pack_small({n: m[n] for n in SMALL}, 1), _pack_small({n: v[n] for n in SMALL}, 1))
    rs = _adamw_sharded(jnp.stack(got_small), sw, sm, sv, sw.shape[1], "adamw_replicated")
    for dst, buf in zip((grads, delta, new_m, new_v), rs):
        dst.update(_unpack_small(buf, small))

    return (loss, dx[None], *[grads[n] for n in WEIGHTS], *[delta[n] for n in WEIGHTS],
            *[new_m[n] for n in WEIGHTS], *[new_v[n] for n in WEIGHTS])
```

```python
import math

import jax
import jax.numpy as jnp
from jax import lax
from jax.experimental import pallas as pl
from jax.experimental.pallas import tpu as pltpu

F32 = jnp.float32
BF = jnp.bfloat16
ACT = jnp.bfloat16

D_MODEL = 1024
DEPTH = 4
N_DEV = 8
SSM_WIDTH = 512
SSM_GROUP = 16
SSM_GROUPS = 32
SSM_STATE = 64
MLA_HEADS = 8
MLA_NOPE = 64
MLA_ROPE = 32
MLA_V = 64
MLA_Q_RANK = 256
MLA_KV_RANK = 128
ROPE_THETA = 10000.0
X_HEADS = 4
X_HEAD_DIM = 128
D_IN = 6048
ALPHA = (2 * DEPTH) ** 0.25
NORM_EPS = 1e-5
ADAM_LR = 0.001
ADAM_B1 = 0.9
ADAM_B2 = 0.999
ADAM_EPS = 1e-08
ADAM_WD = 0.01
ADAM_STEP = 10

LANES = 128
SUBLANES = 8
VMEM_LIMIT = 56 * 1024 * 1024

PW = 6144
ROPE_SLOT_LO = 1408
MLA_SCALE = (MLA_NOPE + MLA_ROPE) ** -0.5
MEM_SCALE = X_HEAD_DIM ** -0.5
NEG = -1e30

T_ROWS = 512
T_ROWS_BWD = 256
T_ROWS_BWD_WIDE = 512
T_ATT = 1024
T_MM = 512

MESH = pl.DeviceIdType.MESH


def _cparams(sem):
    return pltpu.CompilerParams(dimension_semantics=sem, vmem_limit_bytes=VMEM_LIMIT)


def _dot(a, b):
    return lax.dot_general(a, b, (((1,), (0,)), ((), ())), preferred_element_type=F32)


def _dot_nt(a, b):
    return lax.dot_general(a, b, (((1,), (1,)), ((), ())), preferred_element_type=F32)


def _dot_tn(a, b):
    return lax.dot_general(a, b, (((0,), (0,)), ((), ())), preferred_element_type=F32)


def _sigmoid(x):
    return 0.5 * jnp.tanh(0.5 * x) + 0.5


def _silu(x):
    return x * _sigmoid(x)


def _dsilu(x):
    s = _sigmoid(x)
    return s * (1.0 + x * (1.0 - s))


_GELU_C = math.sqrt(2.0 / math.pi)


def _gelu(x):
    return 0.5 * x * (1.0 + jnp.tanh(_GELU_C * (x + 0.044715 * x * x * x)))


def _dgelu(x):
    t = jnp.tanh(_GELU_C * (x + 0.044715 * x * x * x))
    return 0.5 * (1.0 + t) + 0.5 * x * (1.0 - t * t) * _GELU_C * (1.0 + 3 * 0.044715 * x * x)


def _rows(tr, w, col=0):
    return pl.BlockSpec((tr, w), lambda i: (i, col))


def _cols(h, tc):
    return pl.BlockSpec((h, tc), lambda i: (0, i))


def _full(shape):
    n = len(shape)
    return pl.BlockSpec(shape, lambda i: (0,) * n)


class _RowBlock:
    def __init__(self, arr, rows, blk):
        self.arr, self.rows, self.blk = arr, rows, blk

    def spec(self):
        blk = self.blk
        return pl.BlockSpec((self.rows, self.arr.shape[1]), lambda i: (blk, 0))


class _LayerRow:
    def __init__(self, arr, l):
        self.arr, self.l = arr, l

    def spec(self):
        l = self.l
        return pl.BlockSpec((1, 1, self.arr.shape[2]), lambda i: (l, 0, 0))


def _mm(a, b, *, name, ta=False, tb=False, out_dtype=F32, add=None, tm=T_MM, tn=T_MM, tk=1024, exs=None):
    M, K = (a.shape[1], a.shape[0]) if ta else a.shape
    N = b.shape[0] if tb else b.shape[1]
    tm, tn, tk = min(tm, M), min(tn, N), min(tk, K)
    assert M % tm == 0 and N % tn == 0 and K % tk == 0, (M, N, K)
    nk = K // tk
    dn = (((0 if ta else 1,), (1 if tb else 0,)), ((), ()))

    def body(*refs):
        if add is not None:
            a_ref, b_ref, c_ref, o_ref = refs[:4]
        else:
            a_ref, b_ref, o_ref = refs[:3]
        part = lax.dot_general(a_ref[...].astype(BF), b_ref[...].astype(BF), dn, preferred_element_type=F32)
        if nk == 1:
            if add is not None:
                part = part + c_ref[...]
            o_ref[...] = part.astype(out_dtype)
            return
        acc = refs[-1]
        k = pl.program_id(2)

        @pl.when(k == 0)
        def _():
            acc[...] = part

        @pl.when(k != 0)
        def _():
            acc[...] += part

        @pl.when(k == nk - 1)
        def _():
            r = acc[...]
            if add is not None:
                r = r + c_ref[...]
            o_ref[...] = r.astype(out_dtype)

    a_spec = pl.BlockSpec((tk, tm), lambda i, j, k: (k, i)) if ta else pl.BlockSpec((tm, tk), lambda i, j, k: (i, k))
    b_spec = pl.BlockSpec((tn, tk), lambda i, j, k: (j, k)) if tb else pl.BlockSpec((tk, tn), lambda i, j, k: (k, j))
    o_spec = pl.BlockSpec((tm, tn), lambda i, j, k: (i, j))
    in_specs = [a_spec, b_spec] + ([o_spec] if add is not None else [])
    args = (a, b) + ((add,) if add is not None else ())
    (out,), got = _carry_call(
        body, name, (M // tm, N // tn, nk), in_specs, [o_spec], [jax.ShapeDtypeStruct((M, N), out_dtype)],
        [pltpu.VMEM((tm, tn), F32)] if nk > 1 else [], ("parallel", "parallel", "arbitrary"), args, exs)
    return out if exs is None else (out, got)


def _cpow(ar, ai, n):
    rr, ri = None, None
    br, bi = ar, ai
    while n:
        if n & 1:
            if rr is None:
                rr, ri = br, bi
            else:
                rr, ri = rr * br - ri * bi, rr * bi + ri * br
        n >>= 1
        if n:
            br, bi = br * br - bi * bi, 2.0 * br * bi
    return rr, ri


def _seg_shift(v, k, reverse):
    sub = lax.broadcasted_iota(jnp.int32, v.shape, 0)
    if not reverse:
        return jnp.where(sub >= k, pltpu.roll(v, k, 0), 0.0)
    return jnp.where(sub < SUBLANES - k, pltpu.roll(v, SUBLANES - k, 0), 0.0)


def _steps(n, step, init, unroll):
    u = unroll if n % unroll == 0 else 1

    def trip(i, c):
        for s in range(u):
            c = step(i * u + s, c)
        return c

    return lax.fori_loop(0, n // u, trip, init)


def _ssm_scan(hre, him, ar, ai, seglen, reverse, tail=None, tail_init=()):
    w = hre.shape[1]
    zero = jnp.zeros((SUBLANES, w), F32)

    def rows(j):
        jj = (seglen - 1 - j) if reverse else j
        return pl.ds(pl.multiple_of(jj * SUBLANES, SUBLANES), SUBLANES)

    def local(j, c):
        hr, hi = c
        r = rows(j)
        nhr = ar * hr - ai * hi + hre[r, :]
        nhi = ar * hi + ai * hr + him[r, :]
        hre[r, :] = nhr
        him[r, :] = nhi
        return nhr, nhi

    er, ei = _steps(seglen, local, (zero, zero), 4 if reverse else 1)
    pr, pi_ = _cpow(ar, ai, seglen)
    for k in (1, 2, 4):
        sr, si = _seg_shift(er, k, reverse), _seg_shift(ei, k, reverse)
        er, ei = er + pr * sr - pi_ * si, ei + pr * si + pi_ * sr
        pr, pi_ = pr * pr - pi_ * pi_, 2.0 * pr * pi_
    cr, ci = _seg_shift(er, 1, reverse), _seg_shift(ei, 1, reverse)

    def carry_in(j, c):
        tr, ti = c[0] * ar - c[1] * ai, c[0] * ai + c[1] * ar
        r = rows(j)
        fr = hre[r, :] + tr
        fi = him[r, :] + ti
        hre[r, :] = fr
        him[r, :] = fi
        if tail is None:
            return tr, ti
        return (tr, ti) + tuple(tail(j, fr, fi, c[2:]))

    out = _steps(seglen, carry_in, (cr, ci) + tuple(tail_init), 4)
    return out[2:]


SSM_CB = 128
SSM_BS = SSM_CB // SSM_GROUP * SSM_STATE
SSM_SB = 512
SSM_SB_F = 512


def _ssm_specs(S, l, sb=SSM_SB):
    u_spec = pl.BlockSpec((S, SSM_CB), lambda g, h: (0, g))
    bb_spec = pl.BlockSpec((1, 1, SSM_CB, sb), lambda g, h: (l, g, 0, h))
    a_spec = pl.BlockSpec((1, 1, 1, sb), lambda g, h: (l, g, 0, h))
    c_spec = pl.BlockSpec((1, 1, sb, SSM_CB), lambda g, h: (l, g, h, 0))
    d_spec = pl.BlockSpec((1, 1, SSM_CB), lambda g, h: (l, 0, g))
    return u_spec, bb_spec, a_spec, c_spec, d_spec


def _ssm_fwd(u, mats, l):
    S = u.shape[0]
    seglen = S // SUBLANES
    ch = min(512, S)
    nch = S // ch

    def body(u_ref, bbre_ref, bbim_ref, are_ref, aim_ref, cre_ref, cim_ref, d_ref, y_ref, hre, him):
        hf = pl.program_id(1)
        wre = bbre_ref[0, 0].astype(BF)
        wim = bbim_ref[0, 0].astype(BF)

        def mk(c, _):
            r = pl.ds(pl.multiple_of(c * ch, ch), ch)
            ub = u_ref[r, :].astype(BF)
            hre[r, :] = _dot(ub, wre)
            him[r, :] = _dot(ub, wim)
            return 0

        lax.fori_loop(0, nch, mk, 0)
        ar = jnp.broadcast_to(are_ref[0, 0], (SUBLANES, SSM_SB_F))
        ai = jnp.broadcast_to(aim_ref[0, 0], (SUBLANES, SSM_SB_F))
        _ssm_scan(hre, him, ar, ai, seglen, False)
        cr = cre_ref[0, 0].astype(BF)
        ci = cim_ref[0, 0].astype(BF)

        def out(c, _):
            r = pl.ds(pl.multiple_of(c * ch, ch), ch)
            y = _dot(hre[r, :].astype(BF), cr) - _dot(him[r, :].astype(BF), ci)

            @pl.when(hf == 0)
            def _():
                y_ref[r, :] = y + d_ref[0] * u_ref[r, :].astype(F32)

            @pl.when(hf != 0)
            def _():
                y_ref[r, :] = y_ref[r, :] + y

            return 0

        lax.fori_loop(0, nch, out, 0)

    u_spec, bb_spec, a_spec, c_spec, d_spec = _ssm_specs(S, l, SSM_SB_F)
    return pl.pallas_call(
        body, name="ssm_fwd", grid=(SSM_WIDTH // SSM_CB, SSM_BS // SSM_SB_F),
        in_specs=[u_spec, bb_spec, bb_spec, a_spec, a_spec, c_spec, c_spec, d_spec], out_specs=u_spec,
        out_shape=jax.ShapeDtypeStruct((S, SSM_WIDTH), F32),
        scratch_shapes=[pltpu.VMEM((S, SSM_SB_F), F32), pltpu.VMEM((S, SSM_SB_F), F32)],
        compiler_params=_cparams(("parallel", "arbitrary")),
    )(u, *mats)


def _ssm_bwd(u, dy, mats, l, exs=()):
    S = u.shape[0]
    seglen = S // SUBLANES
    ch = min(512, S)
    nch = S // ch
    nblk = SSM_WIDTH // SSM_CB

    def body(u_ref, dy_ref, bbre_ref, bbim_ref, are_ref, aim_ref, cre_ref, cim_ref, d_ref,
             du_ref, dbbre_ref, dbbim_ref, dare_ref, daim_ref, dcre_ref, dcim_ref, dd_ref,
             hre, him, lre, lim):
        hf = pl.program_id(1)
        wre = bbre_ref[0, 0].astype(BF)
        wim = bbim_ref[0, 0].astype(BF)
        wre_t, wim_t = wre.T, wim.T
        cr_t = cre_ref[0, 0].astype(BF).T
        ci_t = cim_ref[0, 0].astype(BF).T

        def mk(c, _):
            r = pl.ds(pl.multiple_of(c * ch, ch), ch)
            ub = u_ref[r, :].astype(BF)
            hre[r, :] = _dot(ub, wre)
            him[r, :] = _dot(ub, wim)
            return 0

        lax.fori_loop(0, nch, mk, 0)
        ar = jnp.broadcast_to(are_ref[0, 0], (SUBLANES, SSM_SB))
        ai = jnp.broadcast_to(aim_ref[0, 0], (SUBLANES, SSM_SB))
        _ssm_scan(hre, him, ar, ai, seglen, False)

        dcre_ref[...] = jnp.zeros_like(dcre_ref)
        dcim_ref[...] = jnp.zeros_like(dcim_ref)

        @pl.when(hf == 0)
        def _():
            dd_ref[...] = jnp.zeros_like(dd_ref)

        def cot(c, _):
            r = pl.ds(pl.multiple_of(c * ch, ch), ch)
            dyv = dy_ref[r, :]
            dyb = dyv.astype(BF)
            lre[r, :] = _dot(dyb, cr_t)
            lim[r, :] = -_dot(dyb, ci_t)
            dcre_ref[0] = dcre_ref[0] + _dot_tn(dyb, hre[r, :].astype(BF))
            dcim_ref[0] = dcim_ref[0] - _dot_tn(dyb, him[r, :].astype(BF))

            @pl.when(hf == 0)
            def _():
                dd_ref[...] = dd_ref[...] + jnp.sum(dyv * u_ref[r, :].astype(F32), axis=0, keepdims=True)

            return 0

        lax.fori_loop(0, nch, cot, 0)

        last = pl.ds((seglen - 1) * SUBLANES, SUBLANES)
        pr0 = _seg_shift(hre[last, :], 1, False)
        pi0 = _seg_shift(him[last, :], 1, False)

        def da(j, lr, li, c):
            acr, aci = c
            jp = jnp.maximum(seglen - 2 - j, 0)
            rp = pl.ds(pl.multiple_of(jp * SUBLANES, SUBLANES), SUBLANES)
            inner = j < seglen - 1
            pr = jnp.where(inner, hre[rp, :], pr0)
            pi_ = jnp.where(inner, him[rp, :], pi0)
            return acr + lr * pr + li * pi_, aci + li * pr - lr * pi_

        zero = jnp.zeros((SUBLANES, SSM_SB), F32)
        acr, aci = _ssm_scan(lre, lim, ar, -ai, seglen, True, tail=da, tail_init=(zero, zero))
        dare_ref[0] = jnp.sum(acr, axis=0, keepdims=True)
        daim_ref[0] = jnp.sum(aci, axis=0, keepdims=True)

        dbbre_ref[...] = jnp.zeros_like(dbbre_ref)
        dbbim_ref[...] = jnp.zeros_like(dbbim_ref)

        def fin(c, _):
            r = pl.ds(pl.multiple_of(c * ch, ch), ch)
            lrb = lre[r, :].astype(BF)
            lib = lim[r, :].astype(BF)
            ub = u_ref[r, :].astype(BF)
            du = _dot(lrb, wre_t) + _dot(lib, wim_t)
            dbbre_ref[0] = dbbre_ref[0] + _dot_tn(ub, lrb)
            dbbim_ref[0] = dbbim_ref[0] + _dot_tn(ub, lib)

            @pl.when(hf == 0)
            def _():
                du_ref[r, :] = du + d_ref[0] * dy_ref[r, :]

            @pl.when(hf != 0)
            def _():
                du_ref[r, :] = du_ref[r, :] + du

            return 0

        lax.fori_loop(0, nch, fin, 0)

    u_spec, bb_spec, a_spec, c_spec, d_spec = _ssm_specs(S, l)
    dbb_spec = pl.BlockSpec((1, SSM_CB, SSM_SB), lambda g, h: (g, 0, h))
    da_spec = pl.BlockSpec((1, 1, SSM_SB), lambda g, h: (g, 0, h))
    dd_spec = pl.BlockSpec((1, SSM_CB), lambda g, h: (0, g))
    out_shape = (
        jax.ShapeDtypeStruct((S, SSM_WIDTH), F32),
        jax.ShapeDtypeStruct((nblk, SSM_CB, SSM_BS), F32), jax.ShapeDtypeStruct((nblk, SSM_CB, SSM_BS), F32),
        jax.ShapeDtypeStruct((nblk, 1, SSM_BS), F32), jax.ShapeDtypeStruct((nblk, 1, SSM_BS), F32),
        jax.ShapeDtypeStruct((nblk, SSM_CB, SSM_BS), F32), jax.ShapeDtypeStruct((nblk, SSM_CB, SSM_BS), F32),
        jax.ShapeDtypeStruct((1, SSM_WIDTH), F32),
    )
    return _carry_call(
        body, "ssm_bwd", (nblk, SSM_BS // SSM_SB), [u_spec, u_spec, bb_spec, bb_spec, a_spec, a_spec, c_spec, c_spec, d_spec],
        (u_spec, dbb_spec, dbb_spec, da_spec, da_spec, dbb_spec, dbb_spec, dd_spec), out_shape,
        [pltpu.VMEM((S, SSM_SB), F32) for _ in range(4)], ("parallel", "arbitrary"), (u, dy) + tuple(mats), exs)


def _ssm_post_fwd(y_raw, proj, w_glu, b_glu, p_ssm):
    S = y_raw.shape[0]
    tr = min(T_ROWS, S)

    def body(y_ref, z_ref, wg_ref, bg_ref, p_ref, o_ref):
        g = _gelu(y_ref[...])
        t = _dot(g.astype(BF), wg_ref[...]) + bg_ref[0]
        glu = t[:, :SSM_WIDTH] * _sigmoid(t[:, SSM_WIDTH:])
        ys = glu * _silu(z_ref[...].astype(F32))
        o_ref[...] = _dot(ys.astype(BF), p_ref[...]).astype(o_ref.dtype)

    return pl.pallas_call(
        body, name="ssm_post_fwd", grid=(S // tr,),
        in_specs=[_rows(tr, 512), _rows(tr, 512, 1), w_glu.spec(), b_glu.spec(), p_ssm.spec()],
        out_specs=_rows(tr, 1024), out_shape=jax.ShapeDtypeStruct((S, D_MODEL), ACT),
        compiler_params=_cparams(("parallel",)),
    )(y_raw, proj, w_glu.arr, b_glu.arr, p_ssm.arr)


def _ssm_post_bwd(do, y_raw, proj, w_glu, b_glu, p_ssm):
    S = y_raw.shape[0]
    tr = min(T_ROWS_BWD_WIDE, S)

    def body(do_ref, y_ref, z_ref, wg_ref, bg_ref, p_ref, dy_ref, dz_ref, dwg_ref, dbg_ref, dp_ref):
        @pl.when(pl.program_id(0) == 0)
        def _():
            dwg_ref[...] = jnp.zeros_like(dwg_ref)
            dbg_ref[...] = jnp.zeros_like(dbg_ref)
            dp_ref[...] = jnp.zeros_like(dp_ref)

        y = y_ref[...]
        z = z_ref[...].astype(F32)
        g = _gelu(y)
        gb = g.astype(BF)
        t = _dot(gb, wg_ref[...]) + bg_ref[0]
        a = t[:, :SSM_WIDTH]
        sb = _sigmoid(t[:, SSM_WIDTH:])
        glu = a * sb
        ys = glu * _silu(z)
        dob = do_ref[...].astype(BF)
        dys = _dot_nt(dob, p_ref[...])
        dp_ref[...] += _dot_tn(ys.astype(BF), dob)
        dglu = dys * _silu(z)
        dz_ref[...] = (dys * glu * _dsilu(z)).astype(dz_ref.dtype)
        dt = jnp.concatenate([dglu * sb, dglu * a * sb * (1.0 - sb)], axis=1)
        dbg_ref[...] += jnp.sum(dt, axis=0, keepdims=True)
        dtb = dt.astype(BF)
        dg = _dot_nt(dtb, wg_ref[...])
        dwg_ref[...] += _dot_tn(gb, dtb)
        dy_ref[...] = dg * _dgelu(y)

    return pl.pallas_call(
        body, name="ssm_post_bwd", grid=(S // tr,),
        in_specs=[_rows(tr, 1024), _rows(tr, 512), _rows(tr, 512, 1), w_glu.spec(), b_glu.spec(), p_ssm.spec()],
        out_specs=(_rows(tr, 512), _rows(tr, 512), _full((512, 1024)), _full((1, 1024)), _full((512, 1024))),
        out_shape=(jax.ShapeDtypeStruct((S, 512), F32), jax.ShapeDtypeStruct((S, 512), BF),
                   jax.ShapeDtypeStruct((512, 1024), F32), jax.ShapeDtypeStruct((1, 1024), F32),
                   jax.ShapeDtypeStruct((512, 1024), F32)),
        compiler_params=_cparams(("arbitrary",)),
    )(do, y_raw, proj, w_glu.arr, b_glu.arr, p_ssm.arr)


def _rope(t, c, sa, sb):
    return t * c + pltpu.roll(t, LANES - 16, 1) * sa + pltpu.roll(t, 16, 1) * sb


def _rope_t(dy, c, sa, sb):
    return dy * c + pltpu.roll(dy * sa, 16, 1) + pltpu.roll(dy * sb, LANES - 16, 1)


def _rms(x, g):
    r = lax.rsqrt(jnp.mean(x * x, axis=-1, keepdims=True) + NORM_EPS)
    return x * r * g, r


def _mla_pre_fwd(proj, q_norm, kv_norm, wuq, wk, wv, tc, tsa, tsb):
    S = proj.shape[0]
    tr = min(T_ROWS, S)

    def body(cq_ref, ckv_ref, slot_ref, qn_ref, kn_ref, wuq_ref, wk_ref, wv_ref, c_ref, sa_ref, sb_ref,
             q_out, k_out, v_out, qt_out, kt_out, vt_out):
        c, sa, sb = c_ref[...], sa_ref[...], sb_ref[...]
        qn, _ = _rms(cq_ref[...].astype(F32), qn_ref[0])
        q = _dot(qn.astype(BF), wuq_ref[...]) * MLA_SCALE
        kn, _ = _rms(ckv_ref[...].astype(F32), kn_ref[0])
        knb = kn.astype(BF)
        kp = _dot(knb, wk_ref[...])
        v = _dot(knb, wv_ref[...]).astype(BF)
        v_out[...] = v
        vt_out[...] = v.T
        kr = _rope(slot_ref[...].astype(F32), c, sa, sb)
        for h in range(MLA_HEADS):
            cs = slice(h * LANES, (h + 1) * LANES)
            qh = _rope(q[:, cs], c, sa, sb).astype(BF)
            kh = (kp[:, cs] + kr).astype(BF)
            q_out[:, cs] = qh
            k_out[:, cs] = kh
            qt_out[cs, :] = qh.T
            kt_out[cs, :] = kh.T

    return pl.pallas_call(
        body, name="mla_pre_fwd", grid=(S // tr,),
        in_specs=[_rows(tr, 256, 4), _rows(tr, 128, 10), _rows(tr, 128, 11), q_norm.spec(), kv_norm.spec(),
                  wuq.spec(), _full((128, 1024)), _full((128, 512)),
                  _rows(tr, 128), _rows(tr, 128), _rows(tr, 128)],
        out_specs=(_rows(tr, 1024), _rows(tr, 1024), _rows(tr, 512), _cols(1024, tr), _cols(1024, tr), _cols(512, tr)),
        out_shape=(jax.ShapeDtypeStruct((S, 1024), BF), jax.ShapeDtypeStruct((S, 1024), BF),
                   jax.ShapeDtypeStruct((S, 512), BF), jax.ShapeDtypeStruct((1024, S), BF),
                   jax.ShapeDtypeStruct((1024, S), BF), jax.ShapeDtypeStruct((512, S), BF)),
        compiler_params=_cparams(("parallel",)),
    )(proj, proj, proj, q_norm.arr, kv_norm.arr, wuq.arr, wk, wv, tc, tsa, tsb)


def _mla_pre_bwd(dq, dk, dv, proj, q_norm, kv_norm, wuq, wk, wv, tc, tsa, tsb):
    S = proj.shape[0]
    tr = min(T_ROWS_BWD, S)

    def body(dq_ref, dk_ref, dv_ref, cq_ref, ckv_ref, qn_ref, kn_ref, wuq_ref, wk_ref, wv_ref, c_ref, sa_ref, sb_ref,
             dcq_ref, dckv_ref, dslot_ref, dwuq_ref, dwk_ref, dwv_ref, dqn_ref, dkn_ref, dqp):
        @pl.when(pl.program_id(0) == 0)
        def _():
            dwuq_ref[...] = jnp.zeros_like(dwuq_ref)
            dwk_ref[...] = jnp.zeros_like(dwk_ref)
            dwv_ref[...] = jnp.zeros_like(dwv_ref)
            dqn_ref[...] = jnp.zeros_like(dqn_ref)
            dkn_ref[...] = jnp.zeros_like(dkn_ref)

        c, sa, sb = c_ref[...], sa_ref[...], sb_ref[...]
        dkr = jnp.zeros((tr, LANES), F32)
        for h in range(MLA_HEADS):
            cs = slice(h * LANES, (h + 1) * LANES)
            dqp[:, cs] = (_rope_t(dq_ref[:, cs], c, sa, sb) * MLA_SCALE).astype(BF)
            dkr = dkr + dk_ref[:, cs]
        lane = lax.broadcasted_iota(jnp.int32, (tr, LANES), 1)
        in_rope = (lane >= MLA_NOPE) & (lane < MLA_NOPE + MLA_ROPE)
        dslot_ref[...] = jnp.where(in_rope, _rope_t(dkr, c, sa, sb), 0.0).astype(dslot_ref.dtype)

        cq = cq_ref[...].astype(F32)
        gq = qn_ref[0]
        qn, rq = _rms(cq, gq)
        dqpb = dqp[...]
        dwuq_ref[...] += _dot_tn(qn.astype(BF), dqpb)
        dqn = _dot_nt(dqpb, wuq_ref[...])
        dqn_ref[...] += jnp.sum(dqn * cq * rq, axis=0, keepdims=True)
        dyg = dqn * gq
        dcq_ref[...] = (rq * dyg - cq * (rq * rq * rq) * jnp.mean(dyg * cq, axis=-1, keepdims=True)).astype(dcq_ref.dtype)

        ckv = ckv_ref[...].astype(F32)
        gk = kn_ref[0]
        kn, rk = _rms(ckv, gk)
        knb = kn.astype(BF)
        dkb = dk_ref[...].astype(BF)
        dvb = dv_ref[...].astype(BF)
        dwk_ref[...] += _dot_tn(knb, dkb)
        dwv_ref[...] += _dot_tn(knb, dvb)
        dkn = _dot_nt(dkb, wk_ref[...]) + _dot_nt(dvb, wv_ref[...])
        dkn_ref[...] += jnp.sum(dkn * ckv * rk, axis=0, keepdims=True)
        dyk = dkn * gk
        dckv_ref[...] = (rk * dyk - ckv * (rk * rk * rk) * jnp.mean(dyk * ckv, axis=-1, keepdims=True)).astype(dckv_ref.dtype)

    return pl.pallas_call(
        body, name="mla_pre_bwd", grid=(S // tr,),
        in_specs=[_rows(tr, 1024), _rows(tr, 1024), _rows(tr, 512), _rows(tr, 256, 4), _rows(tr, 128, 10),
                  q_norm.spec(), kv_norm.spec(), wuq.spec(), _full((128, 1024)), _full((128, 512)),
                  _rows(tr, 128), _rows(tr, 128), _rows(tr, 128)],
        out_specs=(_rows(tr, 256), _rows(tr, 128), _rows(tr, 128), _full((256, 1024)), _full((128, 1024)),
                   _full((128, 512)), _full((1, 256)), _full((1, 128))),
        out_shape=(jax.ShapeDtypeStruct((S, 256), BF), jax.ShapeDtypeStruct((S, 128), BF),
                   jax.ShapeDtypeStruct((S, 128), BF), jax.ShapeDtypeStruct((256, 1024), F32),
                   jax.ShapeDtypeStruct((128, 1024), F32), jax.ShapeDtypeStruct((128, 512), F32),
                   jax.ShapeDtypeStruct((1, 256), F32), jax.ShapeDtypeStruct((1, 128), F32)),
        scratch_shapes=[pltpu.VMEM((tr, 1024), BF)],
        compiler_params=_cparams(("arbitrary",)),
    )(dq, dk, dv, proj, proj, q_norm.arr, kv_norm.arr, wuq.arr, wk, wv, tc, tsa, tsb)


ANY = pl.BlockSpec(memory_space=pl.ANY)
N_REL = N_DEV - 1


def _coords():
    return lax.axis_index("x"), lax.axis_index("y"), lax.axis_index("c")


def _sem_shapes(nbuf):
    return [pltpu.SemaphoreType.DMA((N_REL * nbuf,)), pltpu.SemaphoreType.DMA((N_REL * nbuf,)),
            pltpu.SemaphoreType.DMA((nbuf,))]


def _ag_plan(srcs, dsts, sems):
    send_sems, recv_sems, _ = sems
    plan = []
    for b, (src, dst) in enumerate(zip(srcs, dsts)):
        def slot(px, py, pc, dst=dst):
            return dst.at[4 * px + 2 * py + pc]

        def copy(k, blk, to, s=None, b=b, slot=slot):
            return pltpu.make_async_remote_copy(
                src_ref=slot(*blk) if s is None else s, dst_ref=slot(*blk), send_sem=send_sems.at[N_REL * b + k],
                recv_sem=recv_sems.at[N_REL * b + k], device_id=to, device_id_type=MESH)

        plan.append((b, src, slot, copy))
    return plan


def _ag_start(srcs, dsts, sems):
    x, y, c = _coords()
    chips = [(1 - x, y), (x, 1 - y), (1 - x, 1 - y)]
    for b, src, slot, copy in _ag_plan(srcs, dsts, sems):
        pltpu.make_async_copy(src, slot(x, y, c), sems[2].at[b]).start()
        copy(0, (x, y, c), (x, y, 1 - c), src).start()
        for j, chip in enumerate(chips):
            copy(1 + j, (x, y, c), (*chip, c), src).start()


def _ag_relay(srcs, dsts, sems):
    x, y, c = _coords()
    me, sibling = (x, y, c), (x, y, 1 - c)
    chips = [(1 - x, y), (x, 1 - y), (1 - x, 1 - y)]
    for b, src, slot, copy in _ag_plan(srcs, dsts, sems):
        for j, chip in enumerate(chips):
            copy(1 + j, (*chip, c), me).wait_recv()
            copy(4 + j, (*chip, c), sibling).start()


def _ag_finish(srcs, dsts, sems, relayed):
    x, y, c = _coords()
    me, sibling = (x, y, c), (x, y, 1 - c)
    chips = [(1 - x, y), (x, 1 - y), (1 - x, 1 - y)]
    if not relayed:
        _ag_relay(srcs, dsts, sems)
    plan = _ag_plan(srcs, dsts, sems)
    for b, src, slot, copy in plan:
        copy(0, sibling, me).wait_recv()
        for j, chip in enumerate(chips):
            copy(4 + j, (*chip, 1 - c), me).wait_recv()
        copy(0, me, sibling, src).wait_send()
        for j, chip in enumerate(chips):
            copy(1 + j, me, (*chip, c), src).wait_send()
            copy(4 + j, (*chip, c), sibling).wait_send()
        pltpu.make_async_copy(src, slot(*me), sems[2].at[b]).wait()


def _a2a_copies(srcs, dsts, sems):
    send_sems, recv_sems, local_sems = sems
    x, y, c = _coords()
    me = 4 * x + 2 * y + c
    local, remote = [], []
    for b, (src, dst) in enumerate(zip(srcs, dsts)):
        for rel in range(1, N_DEV):
            px = 1 - x if rel & 4 else x
            py = 1 - y if rel & 2 else y
            pc = 1 - c if rel & 1 else c
            remote.append(pltpu.make_async_remote_copy(
                src_ref=src.at[4 * px + 2 * py + pc], dst_ref=dst.at[me], send_sem=send_sems.at[N_REL * b + rel - 1],
                recv_sem=recv_sems.at[N_REL * b + rel - 1], device_id=(px, py, pc), device_id_type=MESH))
        local.append(pltpu.make_async_copy(src.at[me], dst.at[me], local_sems.at[b]))
    return local, remote


def _a2a_start(srcs, dsts, sems):
    local, remote = _a2a_copies(srcs, dsts, sems)
    for d in local + remote:
        d.start()


def _a2a_finish(srcs, dsts, sems):
    local, remote = _a2a_copies(srcs, dsts, sems)
    for d in remote + local:
        d.wait()


class _Exchange:
    def __init__(self, kind, srcs):
        self.kind, self.srcs = kind, list(srcs)
        self.n = len(self.srcs)

    def out_shapes(self):
        if self.kind == "ag":
            return [jax.ShapeDtypeStruct((N_DEV,) + s.shape, s.dtype) for s in self.srcs]
        return [jax.ShapeDtypeStruct(s.shape, s.dtype) for s in self.srcs]

    def start(self, src_refs, dst_refs, sems):
        (_ag_start if self.kind == "ag" else _a2a_start)(src_refs, dst_refs, sems)

    def relay(self, src_refs, dst_refs, sems):
        if self.kind == "ag":
            _ag_relay(src_refs, dst_refs, sems)

    def finish(self, src_refs, dst_refs, sems, relayed=False):
        if self.kind == "ag":
            _ag_finish(src_refs, dst_refs, sems, relayed)
        else:
            _a2a_finish(src_refs, dst_refs, sems)


def _carry_call(body, name, grid, in_specs, out_specs, out_shape, scratch, semantics, args, exs):
    in_specs, out_specs, out_shape, scratch = list(in_specs), list(out_specs), list(out_shape), list(scratch)
    if not exs:
        return pl.pallas_call(body, name=name, grid=grid, in_specs=in_specs, out_specs=out_specs, out_shape=out_shape,
                              scratch_shapes=scratch, compiler_params=_cparams(semantics))(*args), []
    n_in, n_out, n_scr = len(in_specs), len(out_specs), len(scratch)
    n_ex = sum(e.n for e in exs)

    def wrapped(*refs):
        ins, refs = refs[:n_in], refs[n_in:]
        srcs, refs = refs[:n_ex], refs[n_ex:]
        outs, refs = refs[:n_out], refs[n_out:]
        dsts, refs = refs[:n_ex], refs[n_ex:]
        scr, sems = refs[:n_scr], refs[n_scr:]
        views, off = [], 0
        for i, e in enumerate(exs):
            views.append((srcs[off:off + e.n], dsts[off:off + e.n], sems[3 * i:3 * i + 3]))
            off += e.n
        first = last = late = None
        for axis, size in enumerate(grid):
            at0, at1 = pl.program_id(axis) == 0, pl.program_id(axis) == size - 1
            first = at0 if first is None else first & at0
            last = at1 if last is None else last & at1
            late = at1 if late is None else late & at0
        relay_early = grid[0] > 1 and all(e.kind == "ag" for e in exs)

        @pl.when(first)
        def _():
            for e, view in zip(exs, views):
                e.start(*view)

        if relay_early:
            @pl.when(late)
            def _():
                for e, view in zip(exs, views):
                    e.relay(*view)

        body(*ins, *outs, *scr)

        @pl.when(last)
        def _():
            for e, view in zip(exs, views):
                e.finish(*view, relayed=relay_early)

    res = pl.pallas_call(
        wrapped, name=name + "_x", grid=grid, in_specs=in_specs + [ANY] * n_ex, out_specs=out_specs + [ANY] * n_ex,
        out_shape=out_shape + [s for e in exs for s in e.out_shapes()],
        scratch_shapes=scratch + [s for e in exs for s in _sem_shapes(e.n)],
        compiler_params=_cparams(("arbitrary",) * len(grid)))(*args, *[s for e in exs for s in e.srcs])
    got, off = [], n_out
    for e in exs:
        got.append(list(res[off:off + e.n]))
        off += e.n
    return res[:n_out], got


def _exchange_call(name, exs):
    tot = sum(e.n for e in exs)

    def body(*refs):
        srcs, dsts, sems = refs[:tot], refs[tot:2 * tot], refs[2 * tot:]
        views, off = [], 0
        for i, e in enumerate(exs):
            views.append((srcs[off:off + e.n], dsts[off:off + e.n], sems[3 * i:3 * i + 3]))
            off += e.n
        for e, view in zip(exs, views):
            e.start(*view)
        for e, view in zip(exs, views):
            e.finish(*view)

    outs = pl.pallas_call(
        body, name=name, in_specs=[ANY] * tot, out_specs=[ANY] * tot,
        out_shape=[s for e in exs for s in e.out_shapes()],
        scratch_shapes=[s for e in exs for s in _sem_shapes(e.n)],
    )(*[s for e in exs for s in e.srcs])
    res, off = [], 0
    for e in exs:
        res.append(list(outs[off:off + e.n]))
        off += e.n
    return res


def _flash_call(body, name, exs, in_specs, out_specs, out_shape, scratch, n, args):
    return _carry_call(body, name, (MLA_HEADS // 2, n * (n + 1) // 2), in_specs, out_specs, out_shape, scratch,
                       ("parallel", "arbitrary"), args, exs)


def _tri_rows(s, n):
    at = [(s >= r * (r + 1) // 2).astype(jnp.int32) for r in range(1, n)]
    return sum(at), s - sum(a * r for a, r in zip(at, range(1, n)))


def _tri_cols(s, n):
    starts = [c * n - c * (c - 1) // 2 for c in range(n)]
    col = sum((s >= starts[c]).astype(jnp.int32) for c in range(1, n))
    start = sum(jnp.where(col == c, starts[c], 0) for c in range(n))
    return s - start + col, col


def _pair_rows(a):
    at = a.T
    return jnp.concatenate([at[0:1, :], at[MLA_V:MLA_V + 1, :], jnp.zeros((SUBLANES - 2, a.shape[0]), a.dtype)], axis=0)


def _lower_tri(t):
    return lax.broadcasted_iota(jnp.int32, (t, t), 0) >= lax.broadcasted_iota(jnp.int32, (t, t), 1)


def _upper_tri(t):
    return lax.broadcasted_iota(jnp.int32, (t, t), 1) >= lax.broadcasted_iota(jnp.int32, (t, t), 0)


def _flash_fwd(q, kt, v, exs=()):
    S = q.shape[0]
    t = min(T_ATT, S)
    n = S // t

    def body(q_ref, kt_ref, v_ref, o_ref, lse_ref, lse_t_ref, m_s, l_s, acc):
        qi, ki = _tri_rows(pl.program_id(1), n)
        lo = lax.broadcasted_iota(jnp.int32, (t, LANES), 1) < MLA_V

        @pl.when(ki == 0)
        def _():
            m_s[...] = jnp.full_like(m_s, NEG)
            l_s[...] = jnp.zeros_like(l_s)
            acc[...] = jnp.zeros_like(acc)

        keep = _lower_tri(t) | (ki < qi)
        vv = v_ref[...]
        heads = range(2)
        ss = [jnp.where(keep, _dot(q_ref[:, h * LANES:(h + 1) * LANES], kt_ref[h * LANES:(h + 1) * LANES, :]), NEG)
              for h in heads]
        m_prev = [m_s[h] for h in heads]
        l_prev = [l_s[h] for h in heads]
        m_new = [jnp.maximum(m_prev[h], jnp.max(ss[h], axis=1, keepdims=True)) for h in heads]
        al = [jnp.exp(m_prev[h] - m_new[h]) for h in heads]
        ps = [jnp.exp(ss[h] - m_new[h][:, :1]) for h in heads]
        l_new = [al[h] * l_prev[h] + jnp.sum(ps[h], axis=1, keepdims=True) for h in heads]
        pv = [_dot(ps[h].astype(BF), vv) for h in heads]
        for h in heads:
            m_s[h] = m_new[h]
            l_s[h] = l_new[h]
        acc[...] = jnp.where(lo, al[0], al[1]) * acc[...] + jnp.where(lo, pv[0], pv[1])

        @pl.when(ki == qi)
        def _():
            o_ref[...] = acc[...] / jnp.where(lo, l_s[0], l_s[1])
            lse = jnp.where(lo, m_s[0] + jnp.log(l_s[0]), m_s[1] + jnp.log(l_s[1]))
            lse_ref[0] = lse
            lse_t_ref[0] = _pair_rows(lse)

    return _flash_call(
        body, "mla_flash_fwd", exs,
        [pl.BlockSpec((t, 256), lambda p, s: (_tri_rows(s, n)[0], p)),
         pl.BlockSpec((256, t), lambda p, s: (p, _tri_rows(s, n)[1])),
         pl.BlockSpec((t, 128), lambda p, s: (_tri_rows(s, n)[1], p))],
        [pl.BlockSpec((t, 128), lambda p, s: (_tri_rows(s, n)[0], p)),
         pl.BlockSpec((1, t, 128), lambda p, s: (p, _tri_rows(s, n)[0], 0)),
         pl.BlockSpec((1, SUBLANES, t), lambda p, s: (p, 0, _tri_rows(s, n)[0]))],
        [jax.ShapeDtypeStruct((S, 512), F32), jax.ShapeDtypeStruct((MLA_HEADS // 2, S, 128), F32),
         jax.ShapeDtypeStruct((MLA_HEADS // 2, SUBLANES, S), F32)],
        [pltpu.VMEM((2, t, 128), F32), pltpu.VMEM((2, t, 128), F32), pltpu.VMEM((t, 128), F32)], n, (q, kt, v))


def _flash_bwd_dq(q, k, kt, vt, do, lse, delta, exs=()):
    S = q.shape[0]
    t = min(T_ATT, S)
    n = S // t

    def body(q_ref, k_ref, kt_ref, vt_ref, do_ref, lse_ref, dl_ref, dq_ref, acc):
        qi, ki = _tri_rows(pl.program_id(1), n)
        lo = lax.broadcasted_iota(jnp.int32, (t, LANES), 1) < MLA_V

        @pl.when(ki == 0)
        def _():
            acc[...] = jnp.zeros_like(acc)

        keep = _lower_tri(t) | (ki < qi)
        heads = range(2)
        cs = [slice(h * LANES, (h + 1) * LANES) for h in heads]
        col = [slice(h * MLA_V, h * MLA_V + 1) for h in heads]
        lse, dl, dov, vt = lse_ref[0], dl_ref[0], do_ref[...], vt_ref[...]
        ss = [jnp.where(keep, _dot(q_ref[:, cs[h]], kt_ref[cs[h], :]), NEG) for h in heads]
        dp = [_dot(jnp.where(lo if h == 0 else ~lo, dov, 0).astype(BF), vt) for h in heads]
        ds = [(jnp.exp(ss[h] - lse[:, col[h]]) * (dp[h] - dl[:, col[h]])).astype(BF) for h in heads]
        dq = [_dot(ds[h], k_ref[:, cs[h]]) for h in heads]
        acc[...] += jnp.concatenate(dq, axis=1)

        @pl.when(ki == qi)
        def _():
            dq_ref[...] = acc[...]

    (dq,), got = _flash_call(
        body, "mla_flash_dq", exs,
        [pl.BlockSpec((t, 256), lambda p, s: (_tri_rows(s, n)[0], p)),
         pl.BlockSpec((t, 256), lambda p, s: (_tri_rows(s, n)[1], p)),
         pl.BlockSpec((256, t), lambda p, s: (p, _tri_rows(s, n)[1])),
         pl.BlockSpec((128, t), lambda p, s: (p, _tri_rows(s, n)[1])),
         pl.BlockSpec((t, 128), lambda p, s: (_tri_rows(s, n)[0], p)),
         pl.BlockSpec((1, t, 128), lambda p, s: (p, _tri_rows(s, n)[0], 0)),
         pl.BlockSpec((1, t, 128), lambda p, s: (p, _tri_rows(s, n)[0], 0))],
        [pl.BlockSpec((t, 256), lambda p, s: (_tri_rows(s, n)[0], p))],
        [jax.ShapeDtypeStruct((S, 1024), F32)],
        [pltpu.VMEM((t, 256), F32)], n, (q, k, kt, vt, do, lse, delta))
    return dq, got


def _flash_bwd_dkv(q, qt, k, v, do, dot_, lse_t, delta_t, exs=()):
    S = q.shape[0]
    t = min(T_ATT, S)
    n = S // t

    def body(q_ref, qt_ref, k_ref, v_ref, do_ref, dot_ref, lse_ref, dl_ref, dk_ref, dv_ref, dk_acc, dv_acc):
        qi, ki = _tri_cols(pl.program_id(1), n)
        lo = lax.broadcasted_iota(jnp.int32, (t, LANES), 1) < MLA_V
        top = lax.broadcasted_iota(jnp.int32, (LANES, t), 0) < MLA_V

        @pl.when(qi == ki)
        def _():
            dk_acc[...] = jnp.zeros_like(dk_acc)
            dv_acc[...] = jnp.zeros_like(dv_acc)

        keep = _upper_tri(t) | (qi > ki)
        heads = range(2)
        cs = [slice(h * LANES, (h + 1) * LANES) for h in heads]
        vv, lse, dl, dov, dot_v = v_ref[...], lse_ref[0], dl_ref[0], do_ref[...], dot_ref[...]
        st = [jnp.where(keep, _dot(k_ref[:, cs[h]], qt_ref[cs[h], :]), NEG) for h in heads]
        dpt = [_dot(vv, jnp.where(top if h == 0 else ~top, dot_v, 0).astype(BF)) for h in heads]
        pt = [jnp.exp(st[h] - lse[h:h + 1, :]) for h in heads]
        dst = [(pt[h] * (dpt[h] - dl[h:h + 1, :])).astype(BF) for h in heads]
        dv = [_dot(pt[h].astype(BF), jnp.where(lo if h == 0 else ~lo, dov, 0).astype(BF)) for h in heads]
        dk = [_dot(dst[h], q_ref[:, cs[h]]) for h in heads]
        dv_acc[...] += dv[0] + dv[1]
        dk_acc[...] += jnp.concatenate(dk, axis=1)

        @pl.when(qi == n - 1)
        def _():
            dk_ref[...] = dk_acc[...]
            dv_ref[...] = dv_acc[...]

    (dk, dv), got = _flash_call(
        body, "mla_flash_dkv", exs,
        [pl.BlockSpec((t, 256), lambda p, s: (_tri_cols(s, n)[0], p)),
         pl.BlockSpec((256, t), lambda p, s: (p, _tri_cols(s, n)[0])),
         pl.BlockSpec((t, 256), lambda p, s: (_tri_cols(s, n)[1], p)),
         pl.BlockSpec((t, 128), lambda p, s: (_tri_cols(s, n)[1], p)),
         pl.BlockSpec((t, 128), lambda p, s: (_tri_cols(s, n)[0], p)),
         pl.BlockSpec((128, t), lambda p, s: (p, _tri_cols(s, n)[0])),
         pl.BlockSpec((1, SUBLANES, t), lambda p, s: (p, 0, _tri_cols(s, n)[0])),
         pl.BlockSpec((1, SUBLANES, t), lambda p, s: (p, 0, _tri_cols(s, n)[0]))],
        [pl.BlockSpec((t, 256), lambda p, s: (_tri_cols(s, n)[1], p)),
         pl.BlockSpec((t, 128), lambda p, s: (_tri_cols(s, n)[1], p))],
        [jax.ShapeDtypeStruct((S, 1024), F32), jax.ShapeDtypeStruct((S, 512), F32)],
        [pltpu.VMEM((t, 256), F32), pltpu.VMEM((t, 128), F32)], n, (q, qt, k, v, do, dot_, lse_t, delta_t))
    return dk, dv, got


def _mem_heads(qm, km_ref, vm_ref):
    ps, os_ = [], []
    for h in range(X_HEADS):
        cs = slice(h * X_HEAD_DIM, (h + 1) * X_HEAD_DIM)
        s = _dot_nt(qm[:, cs].astype(BF), km_ref[:, cs]) * MEM_SCALE
        e = jnp.exp(s - jnp.max(s, axis=1, keepdims=True))
        p = e / jnp.sum(e, axis=1, keepdims=True)
        ps.append(p)
        os_.append(_dot(p.astype(BF), vm_ref[:, cs]))
    return ps, jnp.concatenate(os_, axis=1)


def _mem_fwd(proj, km, vm, p_mem):
    S = proj.shape[0]
    tr = min(T_ROWS, S)
    M = km.shape[0]

    def body(q_ref, z_ref, km_ref, vm_ref, p_ref, o_ref):
        _, o = _mem_heads(q_ref[...], km_ref, vm_ref)
        y = o * _silu(z_ref[...].astype(F32))
        o_ref[...] = _dot(y.astype(BF), p_ref[...]).astype(o_ref.dtype)

    return pl.pallas_call(
        body, name="mem_fwd", grid=(S // tr,),
        in_specs=[_rows(tr, 512, 4), _rows(tr, 512, 5), _full((M, 512)), _full((M, 512)), p_mem.spec()],
        out_specs=_rows(tr, 1024), out_shape=jax.ShapeDtypeStruct((S, D_MODEL), ACT),
        compiler_params=_cparams(("parallel",)),
    )(proj, proj, km, vm, p_mem.arr)


def _mem_bwd(do, proj, km, vm, p_mem):
    S = proj.shape[0]
    tr = min(T_ROWS_BWD_WIDE, S)
    M = km.shape[0]

    def body(do_ref, q_ref, z_ref, km_ref, vm_ref, p_ref, dq_ref, dz_ref, dkm_ref, dvm_ref, dp_ref):
        @pl.when(pl.program_id(0) == 0)
        def _():
            dkm_ref[...] = jnp.zeros_like(dkm_ref)
            dvm_ref[...] = jnp.zeros_like(dvm_ref)
            dp_ref[...] = jnp.zeros_like(dp_ref)

        qm = q_ref[...]
        z = z_ref[...].astype(F32)
        ps, o = _mem_heads(qm, km_ref, vm_ref)
        sz = _silu(z)
        y = o * sz
        dob = do_ref[...].astype(BF)
        dy = _dot_nt(dob, p_ref[...])
        dp_ref[...] += _dot_tn(y.astype(BF), dob)
        dz_ref[...] = (dy * o * _dsilu(z)).astype(dz_ref.dtype)
        d_o = dy * sz
        for h in range(X_HEADS):
            cs = slice(h * X_HEAD_DIM, (h + 1) * X_HEAD_DIM)
            doh = d_o[:, cs]
            dohb = doh.astype(BF)
            p = ps[h]
            dpr = _dot_nt(dohb, vm_ref[:, cs])
            ds = (p * (dpr - jnp.sum(doh * o[:, cs], axis=1, keepdims=True)) * MEM_SCALE).astype(BF)
            dq_ref[:, cs] = _dot(ds, km_ref[:, cs]).astype(dq_ref.dtype)
            dkm_ref[:, cs] += _dot_tn(ds, qm[:, cs].astype(BF))
            dvm_ref[:, cs] += _dot_tn(p.astype(BF), dohb)

    return pl.pallas_call(
        body, name="mem_bwd", grid=(S // tr,),
        in_specs=[_rows(tr, 1024), _rows(tr, 512, 4), _rows(tr, 512, 5), _full((M, 512)), _full((M, 512)),
                  p_mem.spec()],
        out_specs=(_rows(tr, 512), _rows(tr, 512), _full((M, 512)), _full((M, 512)), _full((512, 1024))),
        out_shape=(jax.ShapeDtypeStruct((S, 512), BF), jax.ShapeDtypeStruct((S, 512), BF),
                   jax.ShapeDtypeStruct((M, 512), F32), jax.ShapeDtypeStruct((M, 512), F32),
                   jax.ShapeDtypeStruct((512, 1024), F32)),
        compiler_params=_cparams(("arbitrary",)),
    )(do, proj, proj, km, vm, p_mem.arr)


def _merge_fwd(x, proj, o_ssm, o_att, o_mem, b_gate, p_mla, w_out, ln_g, ln_b):
    S = x.shape[0]
    tr = min(T_ROWS, S)

    def body(x_ref, lg_ref, z_ref, os_ref, oa_ref, om_ref, bg_ref, p_ref, w_ref, g_ref, b_ref,
             xn_ref, xb_ref, pre_ref, mg_ref):
        gates = _sigmoid(lg_ref[...].astype(F32) + bg_ref[0])
        ya = oa_ref[...] * _silu(z_ref[...].astype(F32))
        o_mla = _dot(ya.astype(BF), p_ref[...])
        merged = (gates[:, :D_MODEL] * os_ref[...].astype(F32) + gates[:, D_MODEL:2 * D_MODEL] * o_mla
                  + gates[:, 2 * D_MODEL:] * om_ref[...].astype(F32))
        mb = merged.astype(BF)
        mg_ref[...] = mb
        pre = ALPHA * x_ref[...] + _dot(mb, w_ref[...])
        pre_ref[...] = pre
        mu = jnp.mean(pre, axis=-1, keepdims=True)
        xc = pre - mu
        var = jnp.mean(xc * xc, axis=-1, keepdims=True)
        xn = xc * lax.rsqrt(var + NORM_EPS) * g_ref[0] + b_ref[0]
        xn_ref[...] = xn
        xb_ref[...] = xn.astype(BF)

    return pl.pallas_call(
        body, name="merge_fwd", grid=(S // tr,),
        in_specs=[_rows(tr, 1024), _rows(tr, 3072, 1), _rows(tr, 512, 3), _rows(tr, 1024), _rows(tr, 512),
                  _rows(tr, 1024), b_gate.spec(), p_mla.spec(), _full((1024, 1024)), ln_g.spec(), ln_b.spec()],
        out_specs=(_rows(tr, 1024), _rows(tr, 1024), _rows(tr, 1024), _rows(tr, 1024)),
        out_shape=(jax.ShapeDtypeStruct((S, 1024), F32), jax.ShapeDtypeStruct((S, 1024), BF),
                   jax.ShapeDtypeStruct((S, 1024), F32), jax.ShapeDtypeStruct((S, 1024), BF)),
        compiler_params=_cparams(("parallel",)),
    )(x, proj, proj, o_ssm, o_att, o_mem, b_gate.arr, p_mla.arr, w_out, ln_g.arr, ln_b.arr)


def _merge_bwd(dxn, pre, merged, proj, o_ssm, o_att, o_mem, b_gate, p_mla, w_out, ln_g):
    S = pre.shape[0]
    tr = min(T_ROWS_BWD, S)

    def body(dxn_ref, pre_ref, mg_ref, lg_ref, z_ref, os_ref, oa_ref, om_ref, bg_ref, p_ref, w_ref, g_ref,
             dxr_ref, dlg_ref, dos_ref, dom_ref, doa_ref, dz_ref, doat_ref, dl_ref, dlt_ref, dw_ref, dp_ref, dbg_ref,
             dg_ref, db_ref):
        @pl.when(pl.program_id(0) == 0)
        def _():
            dw_ref[...] = jnp.zeros_like(dw_ref)
            dp_ref[...] = jnp.zeros_like(dp_ref)
            dbg_ref[...] = jnp.zeros_like(dbg_ref)
            dg_ref[...] = jnp.zeros_like(dg_ref)
            db_ref[...] = jnp.zeros_like(db_ref)

        dxn = dxn_ref[...]
        pre = pre_ref[...]
        mu = jnp.mean(pre, axis=-1, keepdims=True)
        xc = pre - mu
        rstd = lax.rsqrt(jnp.mean(xc * xc, axis=-1, keepdims=True) + NORM_EPS)
        xhat = xc * rstd
        dg_ref[...] += jnp.sum(dxn * xhat, axis=0, keepdims=True)
        db_ref[...] += jnp.sum(dxn, axis=0, keepdims=True)
        dxh = dxn * g_ref[0]
        dpre = rstd * (dxh - jnp.mean(dxh, axis=-1, keepdims=True)
                       - xhat * jnp.mean(dxh * xhat, axis=-1, keepdims=True))
        dxr_ref[...] = ALPHA * dpre
        dpb = dpre.astype(BF)
        dw_ref[...] += _dot_tn(mg_ref[...], dpb)
        dm = _dot_nt(dpb, w_ref[...])

        gates = _sigmoid(lg_ref[...].astype(F32) + bg_ref[0])
        g0, g1, g2 = gates[:, :D_MODEL], gates[:, D_MODEL:2 * D_MODEL], gates[:, 2 * D_MODEL:]
        z = z_ref[...].astype(F32)
        oa = oa_ref[...]
        sz = _silu(z)
        ya = (oa * sz).astype(BF)
        o_mla = _dot(ya, p_ref[...])
        dos_ref[...] = (g0 * dm).astype(dos_ref.dtype)
        dom_ref[...] = (g2 * dm).astype(dom_ref.dtype)
        do_mla = (g1 * dm).astype(BF)
        dl0 = dm * os_ref[...].astype(F32) * g0 * (1.0 - g0)
        dl1 = dm * o_mla * g1 * (1.0 - g1)
        dl2 = dm * om_ref[...].astype(F32) * g2 * (1.0 - g2)
        dl = jnp.concatenate([dl0, dl1, dl2], axis=1)
        dbg_ref[...] += jnp.sum(dl, axis=0, keepdims=True)
        dlg_ref[...] = dl.astype(dlg_ref.dtype)
        dp_ref[...] += _dot_tn(ya, do_mla)
        dya = _dot_nt(do_mla, p_ref[...])
        doa = dya * sz
        doab = doa.astype(BF)
        doa_ref[...] = doab
        doat_ref[...] = doab.T
        dz_ref[...] = (dya * oa * _dsilu(z)).astype(dz_ref.dtype)
        prod = doa * oa
        lo = lax.broadcasted_iota(jnp.int32, (tr, LANES), 1) < MLA_V
        for pr in range(MLA_HEADS // 2):
            blk = prod[:, pr * LANES:(pr + 1) * LANES]
            d0 = jnp.sum(jnp.where(lo, blk, 0.0), axis=1, keepdims=True)
            d1 = jnp.sum(jnp.where(lo, 0.0, blk), axis=1, keepdims=True)
            dl = jnp.where(lo, d0, d1)
            dl_ref[pr] = dl
            dlt_ref[pr] = _pair_rows(dl)

    return pl.pallas_call(
        body, name="merge_bwd", grid=(S // tr,),
        in_specs=[_rows(tr, 1024), _rows(tr, 1024), _rows(tr, 1024), _rows(tr, 3072, 1), _rows(tr, 512, 3),
                  _rows(tr, 1024), _rows(tr, 512), _rows(tr, 1024), b_gate.spec(), p_mla.spec(),
                  _full((1024, 1024)), ln_g.spec()],
        out_specs=(_rows(tr, 1024), _rows(tr, 3072), _rows(tr, 1024), _rows(tr, 1024), _rows(tr, 512),
                   _rows(tr, 512), _cols(512, tr), pl.BlockSpec((MLA_HEADS // 2, tr, 128), lambda i: (0, i, 0)),
                   pl.BlockSpec((MLA_HEADS // 2, SUBLANES, tr), lambda i: (0, 0, i)),
                   _full((1024, 1024)), _full((512, 1024)), _full((1, 3072)), _full((1, 1024)), _full((1, 1024))),
        out_shape=(jax.ShapeDtypeStruct((S, 1024), F32), jax.ShapeDtypeStruct((S, 3072), BF),
                   jax.ShapeDtypeStruct((S, 1024), BF), jax.ShapeDtypeStruct((S, 1024), BF),
                   jax.ShapeDtypeStruct((S, 512), BF), jax.ShapeDtypeStruct((S, 512), BF),
                   jax.ShapeDtypeStruct((512, S), BF), jax.ShapeDtypeStruct((MLA_HEADS // 2, S, 128), F32),
                   jax.ShapeDtypeStruct((MLA_HEADS // 2, SUBLANES, S), F32),
                   jax.ShapeDtypeStruct((1024, 1024), F32), jax.ShapeDtypeStruct((512, 1024), F32),
                   jax.ShapeDtypeStruct((1, 3072), F32), jax.ShapeDtypeStruct((1, 1024), F32),
                   jax.ShapeDtypeStruct((1, 1024), F32)),
        compiler_params=_cparams(("arbitrary",)),
    )(dxn, pre, merged, proj, proj, o_ssm, o_att, o_mem, b_gate.arr, p_mla.arr, w_out, ln_g.arr)


def _loss_head(y, t):
    S = y.shape[0]
    tr = min(T_ROWS, S)
    n = S // tr

    def body(y_ref, t_ref, dy_ref, l_ref, acc):
        i = pl.program_id(0)

        @pl.when(i == 0)
        def _():
            acc[...] = jnp.zeros_like(acc)

        e = y_ref[...] - t_ref[...]
        dy_ref[...] = e * (1.0 / D_MODEL)
        acc[...] += jnp.sum(e * e, axis=0, keepdims=True)

        @pl.when(i == n - 1)
        def _():
            tot = jnp.sum(acc[...], axis=1, keepdims=True) * (0.5 / D_MODEL)
            l_ref[...] = jnp.broadcast_to(tot, l_ref.shape)

    return pl.pallas_call(
        body, name="loss_head", grid=(n,),
        in_specs=[_rows(tr, 1024), _rows(tr, 1024)],
        out_specs=(_rows(tr, 1024), _full((SUBLANES, LANES))),
        out_shape=(jax.ShapeDtypeStruct((S, 1024), F32), jax.ShapeDtypeStruct((SUBLANES, LANES), F32)),
        scratch_shapes=[pltpu.VMEM((1, 1024), F32)],
        compiler_params=_cparams(("arbitrary",)),
    )(y, t)


def _rope_tables(pos):
    inv_freq = ROPE_THETA ** (-jnp.arange(0, MLA_ROPE, 2, dtype=F32) / MLA_ROPE)
    ang = pos.astype(F32)[:, None] * inv_freq
    cos, sin = jnp.cos(ang), jnp.sin(ang)
    S = pos.shape[0]
    half = MLA_ROPE // 2
    ones = jnp.ones((S, MLA_NOPE), F32)
    z16 = jnp.zeros((S, half), F32)
    z32 = jnp.zeros((S, LANES - MLA_NOPE - MLA_ROPE), F32)
    z64 = jnp.zeros((S, MLA_NOPE), F32)
    c = jnp.concatenate([ones, cos, cos, z32], axis=1)
    sa = jnp.concatenate([z64, -sin, z16, z32], axis=1)
    sb = jnp.concatenate([z64, z16, sin, z32], axis=1)
    return c, sa, sb


def _ssm_discretise(a_re, a_im, log_dt, b_re, b_im):
    dt = jnp.exp(log_dt)[..., None]
    mag = jnp.exp(a_re * dt)
    lb_re = mag * jnp.cos(a_im * dt)
    lb_im = mag * jnp.sin(a_im * dt)
    nr, ni = lb_re - 1.0, lb_im
    den = a_re * a_re + a_im * a_im
    f_re = (nr * a_re + ni * a_im) / den
    f_im = (ni * a_re - nr * a_im) / den
    bb_re = f_re[..., None] * b_re - f_im[..., None] * b_im
    bb_im = f_re[..., None] * b_im + f_im[..., None] * b_re
    return lb_re, lb_im, bb_re, bb_im


_GPB = SSM_CB // SSM_GROUP


def _bd_in(bb):
    nb = SSM_GROUPS // _GPB
    t = bb.reshape(nb, _GPB, SSM_STATE, SSM_GROUP)
    eye = jnp.eye(_GPB, dtype=bb.dtype)
    return jnp.einsum("ngpc,gh->ngchp", t, eye).reshape(nb, SSM_CB, _GPB * SSM_STATE)


def _bd_in_t(d):
    nb = SSM_GROUPS // _GPB
    t = d.reshape(nb, _GPB, SSM_GROUP, _GPB, SSM_STATE)
    eye = jnp.eye(_GPB, dtype=d.dtype)
    return jnp.einsum("ngchp,gh->ngpc", t, eye).reshape(SSM_GROUPS, SSM_STATE, SSM_GROUP)


def _bd_out(c):
    nb = SSM_GROUPS // _GPB
    t = c.reshape(nb, _GPB, SSM_GROUP, SSM_STATE)
    eye = jnp.eye(_GPB, dtype=c.dtype)
    return jnp.einsum("ngcp,gh->ngphc", t, eye).reshape(nb, _GPB * SSM_STATE, SSM_CB)


def _interleave(a):
    S, w = a.shape
    return a.reshape(SUBLANES, S // SUBLANES, w).transpose(1, 0, 2).reshape(S, w)


def _deinterleave(a):
    S, w = a.shape
    return a.reshape(S // SUBLANES, SUBLANES, w).transpose(1, 0, 2).reshape(S, w)


IN_SHARD = D_IN // N_DEV
ROPE_OWNER = ROPE_SLOT_LO // IN_SHARD
assert ROPE_OWNER * IN_SHARD <= ROPE_SLOT_LO and ROPE_SLOT_LO + MLA_ROPE <= (ROPE_OWNER + 1) * IN_SHARD


def _w_in_from_shards(g):
    pieces = []
    for j in range(N_DEV):
        if j == ROPE_OWNER:
            a = ROPE_SLOT_LO - j * IN_SHARD
            z = lambda n: jnp.zeros((g.shape[1], n), g.dtype)
            pieces += [g[j][:, :a], z(MLA_NOPE), g[j][:, a:a + MLA_ROPE], z(LANES - MLA_NOPE - MLA_ROPE),
                       g[j][:, a + MLA_ROPE:]]
        else:
            pieces.append(g[j])
    return jnp.concatenate(pieces, axis=1)


def _w_in_to_shards(d):
    shift = LANES - MLA_ROPE
    out = []
    for j in range(N_DEV):
        lo, hi = j * IN_SHARD, (j + 1) * IN_SHARD
        if j < ROPE_OWNER:
            out.append(d[:, lo:hi])
        elif j > ROPE_OWNER:
            out.append(d[:, lo + shift:hi + shift])
        else:
            r = ROPE_SLOT_LO + MLA_NOPE
            out.append(jnp.concatenate([d[:, lo:ROPE_SLOT_LO], d[:, r:r + MLA_ROPE],
                                        d[:, ROPE_SLOT_LO + LANES:hi + shift]], axis=1))
    return jnp.stack(out)


def _adamw_math(w, g, m, v):
    m = ADAM_B1 * m + (1.0 - ADAM_B1) * g
    v = ADAM_B2 * v + (1.0 - ADAM_B2) * (g * g)
    m_hat = m / (1.0 - ADAM_B1 ** ADAM_STEP)
    v_hat = v / (1.0 - ADAM_B2 ** ADAM_STEP)
    delta = -ADAM_LR * (m_hat / (jnp.sqrt(v_hat) + ADAM_EPS) + ADAM_WD * w)
    return delta, m, v


def _adamw_sharded(parts, w, m, v, tile, name):
    L, _, R, C = parts.shape
    assert R % tile == 0

    def body(p_ref, w_ref, m_ref, v_ref, g_out, d_out, m_out, v_out):
        g = p_ref[0, 0].astype(F32)
        for k in range(1, N_DEV):
            g = g + p_ref[0, k].astype(F32)
        d, mn, vn = _adamw_math(w_ref[0], g, m_ref[0], v_ref[0])
        g_out[0] = g
        d_out[0] = d
        m_out[0] = mn
        v_out[0] = vn

    spec = pl.BlockSpec((1, tile, C), lambda l, i: (l, i, 0))
    shp = jax.ShapeDtypeStruct((L, R, C), F32)
    return pl.pallas_call(
        body, name=name, grid=(L, R // tile),
        in_specs=[pl.BlockSpec((1, N_DEV, tile, C), lambda l, i: (l, 0, i, 0)), spec, spec, spec],
        out_specs=(spec,) * 4, out_shape=(shp,) * 4, compiler_params=_cparams(("parallel", "parallel")),
    )(parts, w, m, v)


COL_GROUP = (("w_glu", 512), ("p_ssm", 512), ("p_mla", 512), ("p_mem", 512), ("w_uq", 256), ("w_ukv", 128))
COL_AT = {n: sum(r for _, r in COL_GROUP[:i]) // rows for i, (n, rows) in enumerate(COL_GROUP)}
assert all(sum(r for _, r in COL_GROUP[:i]) % rows == 0 for i, (_, rows) in enumerate(COL_GROUP))
COL_ROWS = dict(COL_GROUP)
ROW_GROUP = ("w_mem_kv", "w_out")
SMALL = ("b_gate", "ssm_a_re", "ssm_a_im", "ssm_log_dt", "ssm_b_re", "ssm_b_im", "ssm_c_re", "ssm_c_im", "ssm_d",
         "b_glu", "mla_q_norm", "mla_kv_norm", "ln_g", "ln_b")
UQ_COLS = MLA_NOPE + MLA_ROPE


def _pad_lanes(a):
    return jnp.concatenate([a, jnp.zeros(a.shape[:-1] + (LANES - a.shape[-1],), a.dtype)], axis=-1)


def _group_buffers(d, dtype):
    col = jnp.concatenate([_pad_lanes(d[n]) if n == "w_uq" else d[n] for n, _ in COL_GROUP], axis=1)
    row = jnp.concatenate([d[n] for n in ROW_GROUP], axis=1)
    return d["w_in"].astype(dtype), col.astype(dtype), row.astype(dtype)


def _ungroup(bufs):
    b_in, col, row = bufs
    out, off = {"w_in": b_in}, 0
    for n, rows in COL_GROUP:
        t = col[:, off:off + rows]
        out[n] = t[..., :UQ_COLS] if n == "w_uq" else t
        off += rows
    k = row.shape[1] // 2
    out["w_mem_kv"], out["w_out"] = row[:, :k], row[:, k:]
    return out


def _colcat(t):
    return t.transpose(1, 0, 2).reshape(t.shape[1], -1)


def _colsplit(g, n):
    return g.reshape(g.shape[0], N_DEV, n).transpose(1, 0, 2)


def _unpack_weights(g_in, g_col, g_row):
    wc = _colcat(g_col)
    at = lambda n: _RowBlock(wc, COL_ROWS[n], COL_AT[n])
    lo = COL_AT["w_ukv"] * COL_ROWS["w_ukv"]
    ukv = wc[lo:lo + COL_ROWS["w_ukv"]].reshape(-1, MLA_HEADS, LANES)
    lane = lax.broadcasted_iota(jnp.int32, ukv.shape, 2)
    k = g_row.shape[1] // 2
    return dict(
        w_in=_w_in_from_shards(g_in), w_glu=at("w_glu"), w_uq=at("w_uq"), p_ssm=at("p_ssm"), p_mla=at("p_mla"),
        p_mem=at("p_mem"), w_k=jnp.where(lane < MLA_NOPE, ukv, jnp.zeros_like(ukv)).reshape(ukv.shape[0], -1),
        w_v=ukv[..., MLA_NOPE:].reshape(ukv.shape[0], -1),
        w_mem_kv=g_row[:, :k].reshape(-1, g_row.shape[2]), w_out=g_row[:, k:].reshape(-1, g_row.shape[2]))


def _pack_grads_in(d_w_in):
    return _w_in_to_shards(d_w_in).astype(BF)


def _pack_grads_rest(d):
    ukv = jnp.concatenate([d["w_k"].reshape(-1, MLA_HEADS, LANES)[..., :MLA_NOPE],
                           d["w_v"].reshape(-1, MLA_HEADS, MLA_V)], axis=-1).reshape(d["w_k"].shape[0], -1)
    col = jnp.concatenate([ukv if n == "w_ukv" else d[n] for n, _ in COL_GROUP], axis=0)
    row = jnp.concatenate([d[n].reshape(N_DEV, -1, d[n].shape[1]) for n in ROW_GROUP], axis=1)
    return [_colsplit(col, LANES).astype(BF), row.astype(BF)]


def _pack_small(d, lead):
    parts = []
    for n in SMALL:
        keep = d[n].shape[:lead]
        f = d[n].reshape(keep + (-1,))
        pad = (-f.shape[-1]) % (SUBLANES * LANES)
        if pad:
            f = jnp.concatenate([f, jnp.zeros(keep + (pad,), f.dtype)], axis=-1)
        parts.append(f.reshape(keep + (-1, LANES)))
    return jnp.concatenate(parts, axis=lead)


def _unpack_small(buf, like):
    out, off = {}, 0
    for n in SMALL:
        size = math.prod(like[n].shape[1:])
        rows = -(-size // (SUBLANES * LANES)) * SUBLANES
        out[n] = buf[:, off:off + rows].reshape(buf.shape[0], -1)[:, :size].reshape(like[n].shape)
        off += rows
    return out


WEIGHTS = ("w_in", "b_gate", "ssm_a_re", "ssm_a_im", "ssm_log_dt", "ssm_b_re", "ssm_b_im", "ssm_c_re", "ssm_c_im",
           "ssm_d", "w_glu", "b_glu", "mla_q_norm", "w_uq", "mla_kv_norm", "w_ukv", "w_mem_kv", "p_ssm", "p_mla",
           "p_mem", "w_out", "ln_g", "ln_b")
BIG = ("w_in",) + tuple(n for n, _ in COL_GROUP) + ROW_GROUP


def _train_step(x, mem, pos, target, wl, ws):
    S = x.shape[0]
    tc, tsa, tsb = _rope_tables(pos)
    loc = _group_buffers(wl, BF)
    loc = [[b[l] for b in loc] for l in range(DEPTH)]

    lb_re, lb_im, bb_re, bb_im = _ssm_discretise(ws["ssm_a_re"], ws["ssm_a_im"], ws["ssm_log_dt"], ws["ssm_b_re"],
                                                 ws["ssm_b_im"])
    nb = SSM_GROUPS // _GPB
    mats = (jax.vmap(_bd_in)(bb_re), jax.vmap(_bd_in)(bb_im), lb_re.reshape(DEPTH, nb, 1, -1),
            lb_im.reshape(DEPTH, nb, 1, -1), jax.vmap(_bd_out)(ws["ssm_c_re"]), jax.vmap(_bd_out)(ws["ssm_c_im"]),
            ws["ssm_d"].reshape(DEPTH, 1, -1))

    rows3 = {n: ws[n].reshape(DEPTH, 1, -1) for n in ("b_glu", "mla_q_norm", "mla_kv_norm", "b_gate", "ln_g", "ln_b")}

    def small(n, l):
        return _LayerRow(rows3[n], l)

    ((g_in,),) = _exchange_call("weights_gather_first", [_Exchange("ag", loc[0][:1])])
    W = [None] * DEPTH
    saved = []
    xs, xb = x, x.astype(BF)
    for l in range(DEPTH):
        if l == 0:
            proj, (g_rest,) = _mm(xb, _w_in_from_shards(g_in), name="proj_fwd", tm=S, tn=512, out_dtype=ACT,
                                  exs=[_Exchange("ag", loc[0][1:])])
            W[0] = _unpack_weights(g_in, *g_rest)
        else:
            proj = _mm(xb, W[l]["w_in"], name="proj_fwd", tm=S, tn=512, out_dtype=ACT)
        w = W[l]
        u_il = _interleave(proj[:, :SSM_WIDTH])
        y_raw = _deinterleave(_ssm_fwd(u_il, mats, l))
        o_ssm = _ssm_post_fwd(y_raw, proj, w["w_glu"], small("b_glu", l), w["p_ssm"])
        q, k, v, qt, kt, vt = _mla_pre_fwd(proj, small("mla_q_norm", l), small("mla_kv_norm", l), w["w_uq"], w["w_k"], w["w_v"],
                               tc, tsa, tsb)
        nxt = [_Exchange("ag", loc[l + 1])] if l + 1 < DEPTH else []
        (o_att, lse, lse_t), gathered = _flash_fwd(q, kt, v, nxt)
        if nxt:
            W[l + 1] = _unpack_weights(*gathered[0])
        kvm = _mm(mem, w["w_mem_kv"], name="memkv_fwd", out_dtype=BF)
        km, vm = kvm[:, :512], kvm[:, 512:]
        o_mem = _mem_fwd(proj, km, vm, w["p_mem"])
        xn, xnb, pre, merged = _merge_fwd(xs, proj, o_ssm, o_att, o_mem, small("b_gate", l), w["p_mla"], w["w_out"],
                                          small("ln_g", l), small("ln_b", l))
        saved.append(dict(xb=xb, proj=proj, u_il=u_il, y_raw=y_raw, o_ssm=o_ssm, q=q, k=k, v=v, qt=qt, kt=kt, vt=vt, o_att=o_att,
                          lse=lse, lse_t=lse_t,
                          km=km, vm=vm, o_mem=o_mem, pre=pre, merged=merged))
        xs, xb = xn, xnb

    dxs, lvec = _loss_head(xs, target)
    loss = lvec[0, 0]

    disc_names = ("ssm_a_re", "ssm_a_im", "ssm_log_dt", "ssm_b_re", "ssm_b_im")
    got = [None] * DEPTH
    got_small = [None] * DEPTH
    pending = None
    pending_small = None
    for l in reversed(range(DEPTH)):
        sv, w = saved[l], W[l]
        proj = sv["proj"]
        (dx_res, dlg, do_ssm, do_mem, do_att, dz_mla, do_att_t, delta, delta_t, d_w_out, d_p_mla, d_b_gate, d_ln_g,
         d_ln_b) = _merge_bwd(
            dxs, sv["pre"], sv["merged"], proj, sv["o_ssm"], sv["o_att"], sv["o_mem"], small("b_gate", l), w["p_mla"],
            w["w_out"], small("ln_g", l))
        dq_mem, dz_mem, d_km, d_vm, d_p_mem = _mem_bwd(do_mem, proj, sv["km"], sv["vm"], w["p_mem"])
        d_w_mem = _mm(mem, jnp.concatenate([d_km, d_vm], axis=1), name="memkv_bwd", ta=True)
        dq, arrived_rest = _flash_bwd_dq(
            sv["q"], sv["k"], sv["kt"], sv["vt"], do_att, sv["lse"], delta,
            [_Exchange("a2a", pending[1:]), _Exchange("ag", [pending_small])] if pending is not None else [])
        dk, dv, arrived_in = _flash_bwd_dkv(sv["q"], sv["qt"], sv["k"], sv["v"], do_att, do_att_t, sv["lse_t"], delta_t,
                                            [_Exchange("a2a", pending[:1])] if pending is not None else [])
        if pending is not None:
            got[l + 1] = arrived_in[0] + arrived_rest[0]
            got_small[l + 1] = arrived_rest[1][0]
        dcq, dckv, dslot, d_wuq, d_wk, d_wv, d_qn, d_kn = _mla_pre_bwd(
            dq, dk, dv, proj, small("mla_q_norm", l), small("mla_kv_norm", l), w["w_uq"], w["w_k"], w["w_v"],
            tc, tsa, tsb)
        dy_raw, dz_ssm, d_w_glu, d_b_glu, d_p_ssm = _ssm_post_bwd(do_ssm, sv["y_raw"], proj, w["w_glu"],
                                                                 small("b_glu", l), w["p_ssm"])
        rest = _pack_grads_rest(dict(w_glu=d_w_glu, w_uq=d_wuq, w_k=d_wk, w_v=d_wv, w_mem_kv=d_w_mem, p_ssm=d_p_ssm,
                                     p_mla=d_p_mla, p_mem=d_p_mem, w_out=d_w_out))
        (du_il, dbbre, dbbim, dare, daim, dcre, dcim, dd), early = _ssm_bwd(
            sv["u_il"], _interleave(dy_raw), mats, l, [_Exchange("a2a", rest)] if l == 0 else [])
        du = _deinterleave(du_il).astype(BF)
        _, disc_vjp = jax.vjp(_ssm_discretise, *[ws[n][l] for n in disc_names])
        d_disc = disc_vjp((dare.reshape(SSM_GROUPS, SSM_STATE), daim.reshape(SSM_GROUPS, SSM_STATE), _bd_in_t(dbbre),
                           _bd_in_t(dbbim)))
        dproj = jnp.concatenate([du, dz_ssm, dcq, dckv, dslot, dz_mla, dq_mem, dz_mem, dlg], axis=1)
        d_w_in = _mm(sv["xb"], dproj, name="proj_dw", ta=True, tm=1024, tn=512, tk=S)
        if l > 0:
            dxs = _mm(dproj, w["w_in"], name="proj_dx", tb=True, add=dx_res, tm=1024, tn=1024, tk=1024)
        pending = [_pack_grads_in(d_w_in)] + (rest if l > 0 else [])
        gsl = dict(zip(disc_names, d_disc))
        gsl.update(b_gate=d_b_gate, ssm_c_re=_bd_in_t(dcre).transpose(0, 2, 1), ssm_c_im=_bd_in_t(dcim).transpose(0, 2, 1),
                   ssm_d=dd, b_glu=d_b_glu, mla_q_norm=d_qn, mla_kv_norm=d_kn, ln_g=d_ln_g, ln_b=d_ln_b)
        pending_small = _pack_small(gsl, 0)

    dxs, (last_in, (got_small[0],)) = _mm(
        dproj, w["w_in"], name="proj_dx", tb=True, add=dx_res, tm=1024, tn=1024, tk=1024,
        exs=[_Exchange("a2a", pending), _Exchange("ag", [pending_small])])
    got[0] = last_in + early[0]
    return loss, dxs, got, got_small


def kernel(x, mem, positions, w_in, b_gate, ssm_a_re, ssm_a_im, ssm_log_dt, ssm_b_re, ssm_b_im, ssm_c_re, ssm_c_im, ssm_d, w_glu, b_glu, mla_q_norm, w_uq, mla_kv_norm, w_ukv, w_mem_kv, p_ssm, p_mla, p_mem, w_out, ln_g, ln_b, loss_target, m_w_in, m_b_gate, m_ssm_a_re, m_ssm_a_im, m_ssm_log_dt, m_ssm_b_re, m_ssm_b_im, m_ssm_c_re, m_ssm_c_im, m_ssm_d, m_w_glu, m_b_glu, m_mla_q_norm, m_w_uq, m_mla_kv_norm, m_w_ukv, m_w_mem_kv, m_p_ssm, m_p_mla, m_p_mem, m_w_out, m_ln_g, m_ln_b, v_w_in, v_b_gate, v_ssm_a_re, v_ssm_a_im, v_ssm_log_dt, v_ssm_b_re, v_ssm_b_im, v_ssm_c_re, v_ssm_c_im, v_ssm_d, v_w_glu, v_b_glu, v_mla_q_norm, v_w_uq, v_mla_kv_norm, v_w_ukv, v_w_mem_kv, v_p_ssm, v_p_mla, v_p_mem, v_w_out, v_ln_g, v_ln_b):
    w = dict(w_in=w_in, b_gate=b_gate, ssm_a_re=ssm_a_re, ssm_a_im=ssm_a_im, ssm_log_dt=ssm_log_dt, ssm_b_re=ssm_b_re,
             ssm_b_im=ssm_b_im, ssm_c_re=ssm_c_re, ssm_c_im=ssm_c_im, ssm_d=ssm_d, w_glu=w_glu, b_glu=b_glu,
             mla_q_norm=mla_q_norm, w_uq=w_uq, mla_kv_norm=mla_kv_norm, w_ukv=w_ukv, w_mem_kv=w_mem_kv, p_ssm=p_ssm,
             p_mla=p_mla, p_mem=p_mem, w_out=w_out, ln_g=ln_g, ln_b=ln_b)
    m = dict(w_in=m_w_in, b_gate=m_b_gate, ssm_a_re=m_ssm_a_re, ssm_a_im=m_ssm_a_im, ssm_log_dt=m_ssm_log_dt,
             ssm_b_re=m_ssm_b_re, ssm_b_im=m_ssm_b_im, ssm_c_re=m_ssm_c_re, ssm_c_im=m_ssm_c_im, ssm_d=m_ssm_d,
             w_glu=m_w_glu, b_glu=m_b_glu, mla_q_norm=m_mla_q_norm, w_uq=m_w_uq, mla_kv_norm=m_mla_kv_norm,
             w_ukv=m_w_ukv, w_mem_kv=m_w_mem_kv, p_ssm=m_p_ssm, p_mla=m_p_mla, p_mem=m_p_mem, w_out=m_w_out,
             ln_g=m_ln_g, ln_b=m_ln_b)
    v = dict(w_in=v_w_in, b_gate=v_b_gate, ssm_a_re=v_ssm_a_re, ssm_a_im=v_ssm_a_im, ssm_log_dt=v_ssm_log_dt,
             ssm_b_re=v_ssm_b_re, ssm_b_im=v_ssm_b_im, ssm_c_re=v_ssm_c_re, ssm_c_im=v_ssm_c_im, ssm_d=v_ssm_d,
             w_glu=v_w_glu, b_glu=v_b_glu, mla_q_norm=v_mla_q_norm, w_uq=v_w_uq, mla_kv_norm=v_mla_kv_norm,
             w_ukv=v_w_ukv, w_mem_kv=v_w_mem_kv, p_ssm=v_p_ssm, p_mla=v_p_mla, p_mem=v_p_mem, w_out=v_w_out,
             ln_g=v_ln_g, ln_b=v_ln_b)

    wl = {n: w[n] for n in BIG}
    small = {n: w[n] for n in SMALL}
    loss_local, dx, got, got_small = _train_step(x[0], mem[0], positions[0], loss_target[0], wl, small)
    loss = lax.psum(loss_local, ("x", "y", "c"))

    grads, delta, new_m, new_v = {}, {}, {}, {}
    wg = _group_buffers(wl, F32)
    mg = _group_buffers({n: m[n] for n in BIG}, F32)
    vg = _group_buffers({n: v[n] for n in BIG}, F32)
    res = []
    for i, (tile, tag) in enumerate(((256, "in"), (128, "col"), (256, "row"))):
        parts = jnp.stack([got[l][i] for l in range(DEPTH)])
        res.append(_adamw_sharded(parts, wg[i], mg[i], vg[i], tile, "adamw_" + tag))
    for dst, j in ((grads, 0), (delta, 1), (new_m, 2), (new_v, 3)):
        dst.update(_ungroup([r[j] for r in res]))

    sw, sm, sv = (_pack_small(small, 1), _pack_small({n: m[n] for n in SMALL}, 1), _pack_small({n: v[n] for n in SMALL}, 1))
    rs = _adamw_sharded(jnp.stack(got_small), sw, sm, sv, sw.shape[1], "adamw_replicated")
    for dst, buf in zip((grads, delta, new_m, new_v), rs):
        dst.update(_unpack_small(buf, small))

    return (loss, dx[None], *[grads[n] for n in WEIGHTS], *[delta[n] for n in WEIGHTS],
            *[new_m[n] for n in WEIGHTS], *[new_v[n] for n in WEIGHTS])
```

```python
import math

import jax
import jax.numpy as jnp
from jax import lax
from jax.experimental import pallas as pl
from jax.experimental.pallas import tpu as pltpu

F32 = jnp.float32
BF = jnp.bfloat16
ACT = jnp.bfloat16

D_MODEL = 1024
DEPTH = 4
N_DEV = 8
SSM_WIDTH = 512
SSM_GROUP = 16
SSM_GROUPS = 32
SSM_STATE = 64
MLA_HEADS = 8
MLA_NOPE = 64
MLA_ROPE = 32
MLA_V = 64
MLA_Q_RANK = 256
MLA_KV_RANK = 128
ROPE_THETA = 10000.0
X_HEADS = 4
X_HEAD_DIM = 128
D_IN = 6048
ALPHA = (2 * DEPTH) ** 0.25
NORM_EPS = 1e-5
ADAM_LR = 0.001
ADAM_B1 = 0.9
ADAM_B2 = 0.999
ADAM_EPS = 1e-08
ADAM_WD = 0.01
ADAM_STEP = 10

LANES = 128
SUBLANES = 8
VMEM_LIMIT = 56 * 1024 * 1024

PW = 6144
ROPE_SLOT_LO = 1408
MLA_SCALE = (MLA_NOPE + MLA_ROPE) ** -0.5
MEM_SCALE = X_HEAD_DIM ** -0.5
NEG = -1e30

T_ROWS = 512
T_ROWS_WIDE = 1024
T_ROWS_BWD = 256
T_ROWS_BWD_WIDE = 1024
T_ATT = 1024
T_MM = 512

MESH = pl.DeviceIdType.MESH


def _cparams(sem):
    return pltpu.CompilerParams(dimension_semantics=sem, vmem_limit_bytes=VMEM_LIMIT)


def _dot(a, b):
    return lax.dot_general(a, b, (((1,), (0,)), ((), ())), preferred_element_type=F32)


def _dot_nt(a, b):
    return lax.dot_general(a, b, (((1,), (1,)), ((), ())), preferred_element_type=F32)


def _dot_tn(a, b):
    return lax.dot_general(a, b, (((0,), (0,)), ((), ())), preferred_element_type=F32)


def _sigmoid(x):
    return 0.5 * jnp.tanh(0.5 * x) + 0.5


def _silu(x):
    return x * _sigmoid(x)


def _dsilu(x):
    s = _sigmoid(x)
    return s * (1.0 + x * (1.0 - s))


_GELU_C = math.sqrt(2.0 / math.pi)


def _gelu(x):
    return 0.5 * x * (1.0 + jnp.tanh(_GELU_C * (x + 0.044715 * x * x * x)))


def _dgelu(x):
    t = jnp.tanh(_GELU_C * (x + 0.044715 * x * x * x))
    return 0.5 * (1.0 + t) + 0.5 * x * (1.0 - t * t) * _GELU_C * (1.0 + 3 * 0.044715 * x * x)


def _rows(tr, w, col=0):
    return pl.BlockSpec((tr, w), lambda i: (i, col))


def _cols(h, tc):
    return pl.BlockSpec((h, tc), lambda i: (0, i))


def _full(shape):
    n = len(shape)
    return pl.BlockSpec(shape, lambda i: (0,) * n)


class _RowBlock:
    def __init__(self, arr, rows, blk):
        self.arr, self.rows, self.blk = arr, rows, blk

    def spec(self):
        blk = self.blk
        return pl.BlockSpec((self.rows, self.arr.shape[1]), lambda i: (blk, 0))


class _LayerRow:
    def __init__(self, arr, l):
        self.arr, self.l = arr, l

    def spec(self):
        l = self.l
        return pl.BlockSpec((1, 1, self.arr.shape[2]), lambda i: (l, 0, 0))


def _mm(a, b, *, name, ta=False, tb=False, out_dtype=F32, add=None, tm=T_MM, tn=T_MM, tk=1024, exs=None):
    M, K = (a.shape[1], a.shape[0]) if ta else a.shape
    N = b.shape[0] if tb else b.shape[1]
    tm, tn, tk = min(tm, M), min(tn, N), min(tk, K)
    assert M % tm == 0 and N % tn == 0 and K % tk == 0, (M, N, K)
    nk = K // tk
    dn = (((0 if ta else 1,), (1 if tb else 0,)), ((), ()))

    def body(*refs):
        if add is not None:
            a_ref, b_ref, c_ref, o_ref = refs[:4]
        else:
            a_ref, b_ref, o_ref = refs[:3]
        part = lax.dot_general(a_ref[...].astype(BF), b_ref[...].astype(BF), dn, preferred_element_type=F32)
        if nk == 1:
            if add is not None:
                part = part + c_ref[...]
            o_ref[...] = part.astype(out_dtype)
            return
        acc = refs[-1]
        k = pl.program_id(2)

        @pl.when(k == 0)
        def _():
            acc[...] = part

        @pl.when(k != 0)
        def _():
            acc[...] += part

        @pl.when(k == nk - 1)
        def _():
            r = acc[...]
            if add is not None:
                r = r + c_ref[...]
            o_ref[...] = r.astype(out_dtype)

    a_spec = pl.BlockSpec((tk, tm), lambda i, j, k: (k, i)) if ta else pl.BlockSpec((tm, tk), lambda i, j, k: (i, k))
    b_spec = pl.BlockSpec((tn, tk), lambda i, j, k: (j, k)) if tb else pl.BlockSpec((tk, tn), lambda i, j, k: (k, j))
    o_spec = pl.BlockSpec((tm, tn), lambda i, j, k: (i, j))
    in_specs = [a_spec, b_spec] + ([o_spec] if add is not None else [])
    args = (a, b) + ((add,) if add is not None else ())
    (out,), got = _carry_call(
        body, name, (M // tm, N // tn, nk), in_specs, [o_spec], [jax.ShapeDtypeStruct((M, N), out_dtype)],
        [pltpu.VMEM((tm, tn), F32)] if nk > 1 else [], ("parallel", "parallel", "arbitrary"), args, exs)
    return out if exs is None else (out, got)


def _cpow(ar, ai, n):
    rr, ri = None, None
    br, bi = ar, ai
    while n:
        if n & 1:
            if rr is None:
                rr, ri = br, bi
            else:
                rr, ri = rr * br - ri * bi, rr * bi + ri * br
        n >>= 1
        if n:
            br, bi = br * br - bi * bi, 2.0 * br * bi
    return rr, ri


def _seg_shift(v, k, reverse):
    sub = lax.broadcasted_iota(jnp.int32, v.shape, 0)
    if not reverse:
        return jnp.where(sub >= k, pltpu.roll(v, k, 0), 0.0)
    return jnp.where(sub < SUBLANES - k, pltpu.roll(v, SUBLANES - k, 0), 0.0)


def _steps(n, step, init, unroll):
    u = unroll if n % unroll == 0 else 1

    def trip(i, c):
        for s in range(u):
            c = step(i * u + s, c)
        return c

    return lax.fori_loop(0, n // u, trip, init)


def _ssm_scan(hre, him, ar, ai, seglen, reverse, tail=None, tail_init=()):
    w = hre.shape[1]
    zero = jnp.zeros((SUBLANES, w), F32)

    def rows(j):
        jj = (seglen - 1 - j) if reverse else j
        return pl.ds(pl.multiple_of(jj * SUBLANES, SUBLANES), SUBLANES)

    def local(j, c):
        hr, hi = c
        r = rows(j)
        nhr = ar * hr - ai * hi + hre[r, :]
        nhi = ar * hi + ai * hr + him[r, :]
        hre[r, :] = nhr
        him[r, :] = nhi
        return nhr, nhi

    er, ei = _steps(seglen, local, (zero, zero), 8 if reverse else 4)
    pr, pi_ = _cpow(ar, ai, seglen)
    for k in (1, 2, 4):
        sr, si = _seg_shift(er, k, reverse), _seg_shift(ei, k, reverse)
        er, ei = er + pr * sr - pi_ * si, ei + pr * si + pi_ * sr
        pr, pi_ = pr * pr - pi_ * pi_, 2.0 * pr * pi_
    cr, ci = _seg_shift(er, 1, reverse), _seg_shift(ei, 1, reverse)

    def carry_in(j, c):
        tr, ti = c[0] * ar - c[1] * ai, c[0] * ai + c[1] * ar
        r = rows(j)
        fr = hre[r, :] + tr
        fi = him[r, :] + ti
        hre[r, :] = fr
        him[r, :] = fi
        if tail is None:
            return tr, ti
        return (tr, ti) + tuple(tail(j, fr, fi, c[2:]))

    out = _steps(seglen, carry_in, (cr, ci) + tuple(tail_init), 8 if reverse else 4)
    return out[2:]


SSM_CB = 128
SSM_BS = SSM_CB // SSM_GROUP * SSM_STATE
SSM_SB = 512
SSM_SB_F = 512


def _ssm_specs(S, l, sb=SSM_SB):
    u_spec = pl.BlockSpec((S, SSM_CB), lambda g, h: (0, g))
    bb_spec = pl.BlockSpec((1, 1, SSM_CB, sb), lambda g, h: (l, g, 0, h))
    a_spec = pl.BlockSpec((1, 1, 1, sb), lambda g, h: (l, g, 0, h))
    c_spec = pl.BlockSpec((1, 1, sb, SSM_CB), lambda g, h: (l, g, h, 0))
    d_spec = pl.BlockSpec((1, 1, SSM_CB), lambda g, h: (l, 0, g))
    return u_spec, bb_spec, a_spec, c_spec, d_spec


def _ssm_fwd(u, mats, l):
    S = u.shape[0]
    seglen = S // SUBLANES
    ch = min(512, S)
    nch = S // ch

    def body(u_ref, bbre_ref, bbim_ref, are_ref, aim_ref, cre_ref, cim_ref, d_ref, y_ref, hre, him):
        hf = pl.program_id(1)
        wre = bbre_ref[0, 0].astype(BF)
        wim = bbim_ref[0, 0].astype(BF)

        def mk(c, _):
            r = pl.ds(pl.multiple_of(c * ch, ch), ch)
            ub = u_ref[r, :].astype(BF)
            hre[r, :] = _dot(ub, wre)
            him[r, :] = _dot(ub, wim)
            return 0

        lax.fori_loop(0, nch, mk, 0)
        ar = jnp.broadcast_to(are_ref[0, 0], (SUBLANES, SSM_SB_F))
        ai = jnp.broadcast_to(aim_ref[0, 0], (SUBLANES, SSM_SB_F))
        _ssm_scan(hre, him, ar, ai, seglen, False)
        cr = cre_ref[0, 0].astype(BF)
        ci = cim_ref[0, 0].astype(BF)

        def out(c, _):
            r = pl.ds(pl.multiple_of(c * ch, ch), ch)
            y = _dot(hre[r, :].astype(BF), cr) - _dot(him[r, :].astype(BF), ci)

            @pl.when(hf == 0)
            def _():
                y_ref[r, :] = y + d_ref[0] * u_ref[r, :].astype(F32)

            @pl.when(hf != 0)
            def _():
                y_ref[r, :] = y_ref[r, :] + y

            return 0

        lax.fori_loop(0, nch, out, 0)

    u_spec, bb_spec, a_spec, c_spec, d_spec = _ssm_specs(S, l, SSM_SB_F)
    return pl.pallas_call(
        body, name="ssm_fwd", grid=(SSM_WIDTH // SSM_CB, SSM_BS // SSM_SB_F),
        in_specs=[u_spec, bb_spec, bb_spec, a_spec, a_spec, c_spec, c_spec, d_spec], out_specs=u_spec,
        out_shape=jax.ShapeDtypeStruct((S, SSM_WIDTH), F32),
        scratch_shapes=[pltpu.VMEM((S, SSM_SB_F), F32), pltpu.VMEM((S, SSM_SB_F), F32)],
        compiler_params=_cparams(("parallel", "arbitrary")),
    )(u, *mats)


def _ssm_bwd(u, dy, mats, l, exs=()):
    S = u.shape[0]
    seglen = S // SUBLANES
    ch = min(512, S)
    nch = S // ch
    nblk = SSM_WIDTH // SSM_CB

    def body(u_ref, dy_ref, bbre_ref, bbim_ref, are_ref, aim_ref, cre_ref, cim_ref, d_ref,
             du_ref, dbbre_ref, dbbim_ref, dare_ref, daim_ref, dcre_ref, dcim_ref, dd_ref,
             hre, him, lre, lim):
        hf = pl.program_id(1)
        wre = bbre_ref[0, 0].astype(BF)
        wim = bbim_ref[0, 0].astype(BF)
        wre_t, wim_t = wre.T, wim.T
        cr_t = cre_ref[0, 0].astype(BF).T
        ci_t = cim_ref[0, 0].astype(BF).T

        def mk(c, _):
            r = pl.ds(pl.multiple_of(c * ch, ch), ch)
            ub = u_ref[r, :].astype(BF)
            hre[r, :] = _dot(ub, wre)
            him[r, :] = _dot(ub, wim)
            return 0

        lax.fori_loop(0, nch, mk, 0)
        ar = jnp.broadcast_to(are_ref[0, 0], (SUBLANES, SSM_SB))
        ai = jnp.broadcast_to(aim_ref[0, 0], (SUBLANES, SSM_SB))
        _ssm_scan(hre, him, ar, ai, seglen, False)

        dcre_ref[...] = jnp.zeros_like(dcre_ref)
        dcim_ref[...] = jnp.zeros_like(dcim_ref)

        @pl.when(hf == 0)
        def _():
            dd_ref[...] = jnp.zeros_like(dd_ref)

        def cot(c, _):
            r = pl.ds(pl.multiple_of(c * ch, ch), ch)
            dyv = dy_ref[r, :]
            dyb = dyv.astype(BF)
            lre[r, :] = _dot(dyb, cr_t)
            lim[r, :] = -_dot(dyb, ci_t)
            dcre_ref[0] = dcre_ref[0] + _dot_tn(dyb, hre[r, :].astype(BF))
            dcim_ref[0] = dcim_ref[0] - _dot_tn(dyb, him[r, :].astype(BF))

            @pl.when(hf == 0)
            def _():
                dd_ref[...] = dd_ref[...] + jnp.sum(dyv * u_ref[r, :].astype(F32), axis=0, keepdims=True)

            return 0

        lax.fori_loop(0, nch, cot, 0)

        last = pl.ds((seglen - 1) * SUBLANES, SUBLANES)
        pr0 = _seg_shift(hre[last, :], 1, False)
        pi0 = _seg_shift(him[last, :], 1, False)

        def da(j, lr, li, c):
            acr, aci = c
            jp = jnp.maximum(seglen - 2 - j, 0)
            rp = pl.ds(pl.multiple_of(jp * SUBLANES, SUBLANES), SUBLANES)
            inner = j < seglen - 1
            pr = jnp.where(inner, hre[rp, :], pr0)
            pi_ = jnp.where(inner, him[rp, :], pi0)
            return acr + lr * pr + li * pi_, aci + li * pr - lr * pi_

        zero = jnp.zeros((SUBLANES, SSM_SB), F32)
        acr, aci = _ssm_scan(lre, lim, ar, -ai, seglen, True, tail=da, tail_init=(zero, zero))
        dare_ref[0] = jnp.sum(acr, axis=0, keepdims=True)
        daim_ref[0] = jnp.sum(aci, axis=0, keepdims=True)

        dbbre_ref[...] = jnp.zeros_like(dbbre_ref)
        dbbim_ref[...] = jnp.zeros_like(dbbim_ref)

        def fin(c, _):
            r = pl.ds(pl.multiple_of(c * ch, ch), ch)
            lrb = lre[r, :].astype(BF)
            lib = lim[r, :].astype(BF)
            ub = u_ref[r, :].astype(BF)
            du = _dot(lrb, wre_t) + _dot(lib, wim_t)
            dbbre_ref[0] = dbbre_ref[0] + _dot_tn(ub, lrb)
            dbbim_ref[0] = dbbim_ref[0] + _dot_tn(ub, lib)

            @pl.when(hf == 0)
            def _():
                du_ref[r, :] = du + d_ref[0] * dy_ref[r, :]

            @pl.when(hf != 0)
            def _():
                du_ref[r, :] = du_ref[r, :] + du

            return 0

        lax.fori_loop(0, nch, fin, 0)

    u_spec, bb_spec, a_spec, c_spec, d_spec = _ssm_specs(S, l)
    dbb_spec = pl.BlockSpec((1, SSM_CB, SSM_SB), lambda g, h: (g, 0, h))
    da_spec = pl.BlockSpec((1, 1, SSM_SB), lambda g, h: (g, 0, h))
    dd_spec = pl.BlockSpec((1, SSM_CB), lambda g, h: (0, g))
    out_shape = (
        jax.ShapeDtypeStruct((S, SSM_WIDTH), F32),
        jax.ShapeDtypeStruct((nblk, SSM_CB, SSM_BS), F32), jax.ShapeDtypeStruct((nblk, SSM_CB, SSM_BS), F32),
        jax.ShapeDtypeStruct((nblk, 1, SSM_BS), F32), jax.ShapeDtypeStruct((nblk, 1, SSM_BS), F32),
        jax.ShapeDtypeStruct((nblk, SSM_CB, SSM_BS), F32), jax.ShapeDtypeStruct((nblk, SSM_CB, SSM_BS), F32),
        jax.ShapeDtypeStruct((1, SSM_WIDTH), F32),
    )
    return _carry_call(
        body, "ssm_bwd", (nblk, SSM_BS // SSM_SB), [u_spec, u_spec, bb_spec, bb_spec, a_spec, a_spec, c_spec, c_spec, d_spec],
        (u_spec, dbb_spec, dbb_spec, da_spec, da_spec, dbb_spec, dbb_spec, dd_spec), out_shape,
        [pltpu.VMEM((S, SSM_SB), F32) for _ in range(4)], ("parallel", "arbitrary"), (u, dy) + tuple(mats), exs)


def _ssm_post_fwd(y_raw, proj, w_glu, b_glu, p_ssm):
    S = y_raw.shape[0]
    tr = min(T_ROWS_WIDE, S)

    def body(y_ref, z_ref, wg_ref, bg_ref, p_ref, o_ref):
        g = _gelu(y_ref[...])
        t = _dot(g.astype(BF), wg_ref[...]) + bg_ref[0]
        glu = t[:, :SSM_WIDTH] * _sigmoid(t[:, SSM_WIDTH:])
        ys = glu * _silu(z_ref[...].astype(F32))
        o_ref[...] = _dot(ys.astype(BF), p_ref[...]).astype(o_ref.dtype)

    return pl.pallas_call(
        body, name="ssm_post_fwd", grid=(S // tr,),
        in_specs=[_rows(tr, 512), _rows(tr, 512, 1), w_glu.spec(), b_glu.spec(), p_ssm.spec()],
        out_specs=_rows(tr, 1024), out_shape=jax.ShapeDtypeStruct((S, D_MODEL), ACT),
        compiler_params=_cparams(("parallel",)),
    )(y_raw, proj, w_glu.arr, b_glu.arr, p_ssm.arr)


def _ssm_post_bwd(do, y_raw, proj, w_glu, b_glu, p_ssm):
    S = y_raw.shape[0]
    tr = min(T_ROWS_BWD_WIDE, S)

    def body(do_ref, y_ref, z_ref, wg_ref, bg_ref, p_ref, dy_ref, dz_ref, dwg_ref, dbg_ref, dp_ref):
        @pl.when(pl.program_id(0) == 0)
        def _():
            dwg_ref[...] = jnp.zeros_like(dwg_ref)
            dbg_ref[...] = jnp.zeros_like(dbg_ref)
            dp_ref[...] = jnp.zeros_like(dp_ref)

        y = y_ref[...]
        z = z_ref[...].astype(F32)
        g = _gelu(y)
        gb = g.astype(BF)
        t = _dot(gb, wg_ref[...]) + bg_ref[0]
        a = t[:, :SSM_WIDTH]
        sb = _sigmoid(t[:, SSM_WIDTH:])
        glu = a * sb
        ys = glu * _silu(z)
        dob = do_ref[...].astype(BF)
        dys = _dot_nt(dob, p_ref[...])
        dp_ref[...] += _dot_tn(ys.astype(BF), dob)
        dglu = dys * _silu(z)
        dz_ref[...] = (dys * glu * _dsilu(z)).astype(dz_ref.dtype)
        dt = jnp.concatenate([dglu * sb, dglu * a * sb * (1.0 - sb)], axis=1)
        dbg_ref[...] += jnp.sum(dt, axis=0, keepdims=True)
        dtb = dt.astype(BF)
        dg = _dot_nt(dtb, wg_ref[...])
        dwg_ref[...] += _dot_tn(gb, dtb)
        dy_ref[...] = dg * _dgelu(y)

    return pl.pallas_call(
        body, name="ssm_post_bwd", grid=(S // tr,),
        in_specs=[_rows(tr, 1024), _rows(tr, 512), _rows(tr, 512, 1), w_glu.spec(), b_glu.spec(), p_ssm.spec()],
        out_specs=(_rows(tr, 512), _rows(tr, 512), _full((512, 1024)), _full((1, 1024)), _full((512, 1024))),
        out_shape=(jax.ShapeDtypeStruct((S, 512), F32), jax.ShapeDtypeStruct((S, 512), BF),
                   jax.ShapeDtypeStruct((512, 1024), F32), jax.ShapeDtypeStruct((1, 1024), F32),
                   jax.ShapeDtypeStruct((512, 1024), F32)),
        compiler_params=_cparams(("arbitrary",)),
    )(do, y_raw, proj, w_glu.arr, b_glu.arr, p_ssm.arr)


def _rope(t, c, sa, sb):
    return t * c + pltpu.roll(t, LANES - 16, 1) * sa + pltpu.roll(t, 16, 1) * sb


def _rope_t(dy, c, sa, sb):
    return dy * c + pltpu.roll(dy * sa, 16, 1) + pltpu.roll(dy * sb, LANES - 16, 1)


def _rms(x, g):
    r = lax.rsqrt(jnp.mean(x * x, axis=-1, keepdims=True) + NORM_EPS)
    return x * r * g, r


def _mla_pre_fwd(proj, q_norm, kv_norm, wuq, wk, wv, tc, tsa, tsb):
    S = proj.shape[0]
    tr = min(T_ROWS_WIDE, S)

    def body(cq_ref, ckv_ref, slot_ref, qn_ref, kn_ref, wuq_ref, wk_ref, wv_ref, c_ref, sa_ref, sb_ref,
             q_out, k_out, v_out, qt_out, kt_out, vt_out):
        c, sa, sb = c_ref[...], sa_ref[...], sb_ref[...]
        qn, _ = _rms(cq_ref[...].astype(F32), qn_ref[0])
        q = _dot(qn.astype(BF), wuq_ref[...]) * MLA_SCALE
        kn, _ = _rms(ckv_ref[...].astype(F32), kn_ref[0])
        knb = kn.astype(BF)
        kp = _dot(knb, wk_ref[...])
        v = _dot(knb, wv_ref[...]).astype(BF)
        v_out[...] = v
        vt_out[...] = v.T
        kr = _rope(slot_ref[...].astype(F32), c, sa, sb)
        for h in range(MLA_HEADS):
            cs = slice(h * LANES, (h + 1) * LANES)
            qh = _rope(q[:, cs], c, sa, sb).astype(BF)
            kh = (kp[:, cs] + kr).astype(BF)
            q_out[:, cs] = qh
            k_out[:, cs] = kh
            qt_out[cs, :] = qh.T
            kt_out[cs, :] = kh.T

    return pl.pallas_call(
        body, name="mla_pre_fwd", grid=(S // tr,),
        in_specs=[_rows(tr, 256, 4), _rows(tr, 128, 10), _rows(tr, 128, 11), q_norm.spec(), kv_norm.spec(),
                  wuq.spec(), _full((128, 1024)), _full((128, 512)),
                  _rows(tr, 128), _rows(tr, 128), _rows(tr, 128)],
        out_specs=(_rows(tr, 1024), _rows(tr, 1024), _rows(tr, 512), _cols(1024, tr), _cols(1024, tr), _cols(512, tr)),
        out_shape=(jax.ShapeDtypeStruct((S, 1024), BF), jax.ShapeDtypeStruct((S, 1024), BF),
                   jax.ShapeDtypeStruct((S, 512), BF), jax.ShapeDtypeStruct((1024, S), BF),
                   jax.ShapeDtypeStruct((1024, S), BF), jax.ShapeDtypeStruct((512, S), BF)),
        compiler_params=_cparams(("parallel",)),
    )(proj, proj, proj, q_norm.arr, kv_norm.arr, wuq.arr, wk, wv, tc, tsa, tsb)


def _mla_pre_bwd(dq, dk, dv, proj, q_norm, kv_norm, wuq, wk, wv, tc, tsa, tsb):
    S = proj.shape[0]
    tr = min(T_ROWS_BWD, S)

    def body(dq_ref, dk_ref, dv_ref, cq_ref, ckv_ref, qn_ref, kn_ref, wuq_ref, wk_ref, wv_ref, c_ref, sa_ref, sb_ref,
             dcq_ref, dckv_ref, dslot_ref, dwuq_ref, dwk_ref, dwv_ref, dqn_ref, dkn_ref, dqp):
        @pl.when(pl.program_id(0) == 0)
        def _():
            dwuq_ref[...] = jnp.zeros_like(dwuq_ref)
            dwk_ref[...] = jnp.zeros_like(dwk_ref)
            dwv_ref[...] = jnp.zeros_like(dwv_ref)
            dqn_ref[...] = jnp.zeros_like(dqn_ref)
            dkn_ref[...] = jnp.zeros_like(dkn_ref)

        c, sa, sb = c_ref[...], sa_ref[...], sb_ref[...]
        dkr = jnp.zeros((tr, LANES), F32)
        for h in range(MLA_HEADS):
            cs = slice(h * LANES, (h + 1) * LANES)
            dqp[:, cs] = (_rope_t(dq_ref[:, cs], c, sa, sb) * MLA_SCALE).astype(BF)
            dkr = dkr + dk_ref[:, cs]
        lane = lax.broadcasted_iota(jnp.int32, (tr, LANES), 1)
        in_rope = (lane >= MLA_NOPE) & (lane < MLA_NOPE + MLA_ROPE)
        dslot_ref[...] = jnp.where(in_rope, _rope_t(dkr, c, sa, sb), 0.0).astype(dslot_ref.dtype)

        cq = cq_ref[...].astype(F32)
        gq = qn_ref[0]
        qn, rq = _rms(cq, gq)
        dqpb = dqp[...]
        dwuq_ref[...] += _dot_tn(qn.astype(BF), dqpb)
        dqn = _dot_nt(dqpb, wuq_ref[...])
        dqn_ref[...] += jnp.sum(dqn * cq * rq, axis=0, keepdims=True)
        dyg = dqn * gq
        dcq_ref[...] = (rq * dyg - cq * (rq * rq * rq) * jnp.mean(dyg * cq, axis=-1, keepdims=True)).astype(dcq_ref.dtype)

        ckv = ckv_ref[...].astype(F32)
        gk = kn_ref[0]
        kn, rk = _rms(ckv, gk)
        knb = kn.astype(BF)
        dkb = dk_ref[...].astype(BF)
        dvb = dv_ref[...].astype(BF)
        dwk_ref[...] += _dot_tn(knb, dkb)
        dwv_ref[...] += _dot_tn(knb, dvb)
        dkn = _dot_nt(dkb, wk_ref[...]) + _dot_nt(dvb, wv_ref[...])
        dkn_ref[...] += jnp.sum(dkn * ckv * rk, axis=0, keepdims=True)
        dyk = dkn * gk
        dckv_ref[...] = (rk * dyk - ckv * (rk * rk * rk) * jnp.mean(dyk * ckv, axis=-1, keepdims=True)).astype(dckv_ref.dtype)

    return pl.pallas_call(
        body, name="mla_pre_bwd", grid=(S // tr,),
        in_specs=[_rows(tr, 1024), _rows(tr, 1024), _rows(tr, 512), _rows(tr, 256, 4), _rows(tr, 128, 10),
                  q_norm.spec(), kv_norm.spec(), wuq.spec(), _full((128, 1024)), _full((128, 512)),
                  _rows(tr, 128), _rows(tr, 128), _rows(tr, 128)],
        out_specs=(_rows(tr, 256), _rows(tr, 128), _rows(tr, 128), _full((256, 1024)), _full((128, 1024)),
                   _full((128, 512)), _full((1, 256)), _full((1, 128))),
        out_shape=(jax.ShapeDtypeStruct((S, 256), BF), jax.ShapeDtypeStruct((S, 128), BF),
                   jax.ShapeDtypeStruct((S, 128), BF), jax.ShapeDtypeStruct((256, 1024), F32),
                   jax.ShapeDtypeStruct((128, 1024), F32), jax.ShapeDtypeStruct((128, 512), F32),
                   jax.ShapeDtypeStruct((1, 256), F32), jax.ShapeDtypeStruct((1, 128), F32)),
        scratch_shapes=[pltpu.VMEM((tr, 1024), BF)],
        compiler_params=_cparams(("arbitrary",)),
    )(dq, dk, dv, proj, proj, q_norm.arr, kv_norm.arr, wuq.arr, wk, wv, tc, tsa, tsb)


ANY = pl.BlockSpec(memory_space=pl.ANY)
N_REL = N_DEV - 1


def _coords():
    return lax.axis_index("x"), lax.axis_index("y"), lax.axis_index("c")


def _sem_shapes(nbuf):
    return [pltpu.SemaphoreType.DMA((N_REL * nbuf,)), pltpu.SemaphoreType.DMA((N_REL * nbuf,)),
            pltpu.SemaphoreType.DMA((nbuf,))]


def _ag_plan(srcs, dsts, sems):
    send_sems, recv_sems, _ = sems
    plan = []
    for b, (src, dst) in enumerate(zip(srcs, dsts)):
        def slot(px, py, pc, dst=dst):
            return dst.at[4 * px + 2 * py + pc]

        def copy(k, blk, to, s=None, b=b, slot=slot):
            return pltpu.make_async_remote_copy(
                src_ref=slot(*blk) if s is None else s, dst_ref=slot(*blk), send_sem=send_sems.at[N_REL * b + k],
                recv_sem=recv_sems.at[N_REL * b + k], device_id=to, device_id_type=MESH)

        plan.append((b, src, slot, copy))
    return plan


def _ag_start(srcs, dsts, sems):
    x, y, c = _coords()
    chips = [(1 - x, y), (x, 1 - y), (1 - x, 1 - y)]
    for b, src, slot, copy in _ag_plan(srcs, dsts, sems):
        pltpu.make_async_copy(src, slot(x, y, c), sems[2].at[b]).start()
        copy(0, (x, y, c), (x, y, 1 - c), src).start()
        for j, chip in enumerate(chips):
            copy(1 + j, (x, y, c), (*chip, c), src).start()


def _ag_relay(srcs, dsts, sems):
    x, y, c = _coords()
    me, sibling = (x, y, c), (x, y, 1 - c)
    chips = [(1 - x, y), (x, 1 - y), (1 - x, 1 - y)]
    for b, src, slot, copy in _ag_plan(srcs, dsts, sems):
        for j, chip in enumerate(chips):
            copy(1 + j, (*chip, c), me).wait_recv()
            copy(4 + j, (*chip, c), sibling).start()


def _ag_finish(srcs, dsts, sems, relayed):
    x, y, c = _coords()
    me, sibling = (x, y, c), (x, y, 1 - c)
    chips = [(1 - x, y), (x, 1 - y), (1 - x, 1 - y)]
    if not relayed:
        _ag_relay(srcs, dsts, sems)
    plan = _ag_plan(srcs, dsts, sems)
    for b, src, slot, copy in plan:
        copy(0, sibling, me).wait_recv()
        for j, chip in enumerate(chips):
            copy(4 + j, (*chip, 1 - c), me).wait_recv()
        copy(0, me, sibling, src).wait_send()
        for j, chip in enumerate(chips):
            copy(1 + j, me, (*chip, c), src).wait_send()
            copy(4 + j, (*chip, c), sibling).wait_send()
        pltpu.make_async_copy(src, slot(*me), sems[2].at[b]).wait()


def _a2a_copies(srcs, dsts, sems):
    send_sems, recv_sems, local_sems = sems
    x, y, c = _coords()
    me = 4 * x + 2 * y + c
    local, remote = [], []
    for b, (src, dst) in enumerate(zip(srcs, dsts)):
        for rel in range(1, N_DEV):
            px = 1 - x if rel & 4 else x
            py = 1 - y if rel & 2 else y
            pc = 1 - c if rel & 1 else c
            remote.append(pltpu.make_async_remote_copy(
                src_ref=src.at[4 * px + 2 * py + pc], dst_ref=dst.at[me], send_sem=send_sems.at[N_REL * b + rel - 1],
                recv_sem=recv_sems.at[N_REL * b + rel - 1], device_id=(px, py, pc), device_id_type=MESH))
        local.append(pltpu.make_async_copy(src.at[me], dst.at[me], local_sems.at[b]))
    return local, remote


def _a2a_start(srcs, dsts, sems):
    local, remote = _a2a_copies(srcs, dsts, sems)
    for d in local + remote:
        d.start()


def _a2a_finish(srcs, dsts, sems):
    local, remote = _a2a_copies(srcs, dsts, sems)
    for d in remote + local:
        d.wait()


class _Exchange:
    def __init__(self, kind, srcs):
        self.kind, self.srcs = kind, list(srcs)
        self.n = len(self.srcs)

    def out_shapes(self):
        if self.kind == "ag":
            return [jax.ShapeDtypeStruct((N_DEV,) + s.shape, s.dtype) for s in self.srcs]
        return [jax.ShapeDtypeStruct(s.shape, s.dtype) for s in self.srcs]

    def start(self, src_refs, dst_refs, sems):
        (_ag_start if self.kind == "ag" else _a2a_start)(src_refs, dst_refs, sems)

    def relay(self, src_refs, dst_refs, sems):
        if self.kind == "ag":
            _ag_relay(src_refs, dst_refs, sems)

    def finish(self, src_refs, dst_refs, sems, relayed=False):
        if self.kind == "ag":
            _ag_finish(src_refs, dst_refs, sems, relayed)
        else:
            _a2a_finish(src_refs, dst_refs, sems)


def _carry_call(body, name, grid, in_specs, out_specs, out_shape, scratch, semantics, args, exs):
    in_specs, out_specs, out_shape, scratch = list(in_specs), list(out_specs), list(out_shape), list(scratch)
    if not exs:
        return pl.pallas_call(body, name=name, grid=grid, in_specs=in_specs, out_specs=out_specs, out_shape=out_shape,
                              scratch_shapes=scratch, compiler_params=_cparams(semantics))(*args), []
    n_in, n_out, n_scr = len(in_specs), len(out_specs), len(scratch)
    n_ex = sum(e.n for e in exs)

    def wrapped(*refs):
        ins, refs = refs[:n_in], refs[n_in:]
        srcs, refs = refs[:n_ex], refs[n_ex:]
        outs, refs = refs[:n_out], refs[n_out:]
        dsts, refs = refs[:n_ex], refs[n_ex:]
        scr, sems = refs[:n_scr], refs[n_scr:]
        views, off = [], 0
        for i, e in enumerate(exs):
            views.append((srcs[off:off + e.n], dsts[off:off + e.n], sems[3 * i:3 * i + 3]))
            off += e.n
        first = last = late = None
        for axis, size in enumerate(grid):
            at0, at1 = pl.program_id(axis) == 0, pl.program_id(axis) == size - 1
            first = at0 if first is None else first & at0
            last = at1 if last is None else last & at1
            late = at1 if late is None else late & at0
        relay_early = grid[0] > 1 and all(e.kind == "ag" for e in exs)

        @pl.when(first)
        def _():
            for e, view in zip(exs, views):
                e.start(*view)

        if relay_early:
            @pl.when(late)
            def _():
                for e, view in zip(exs, views):
                    e.relay(*view)

        body(*ins, *outs, *scr)

        @pl.when(last)
        def _():
            for e, view in zip(exs, views):
                e.finish(*view, relayed=relay_early)

    res = pl.pallas_call(
        wrapped, name=name + "_x", grid=grid, in_specs=in_specs + [ANY] * n_ex, out_specs=out_specs + [ANY] * n_ex,
        out_shape=out_shape + [s for e in exs for s in e.out_shapes()],
        scratch_shapes=scratch + [s for e in exs for s in _sem_shapes(e.n)],
        compiler_params=_cparams(("arbitrary",) * len(grid)))(*args, *[s for e in exs for s in e.srcs])
    got, off = [], n_out
    for e in exs:
        got.append(list(res[off:off + e.n]))
        off += e.n
    return res[:n_out], got


def _exchange_call(name, exs):
    tot = sum(e.n for e in exs)

    def body(*refs):
        srcs, dsts, sems = refs[:tot], refs[tot:2 * tot], refs[2 * tot:]
        views, off = [], 0
        for i, e in enumerate(exs):
            views.append((srcs[off:off + e.n], dsts[off:off + e.n], sems[3 * i:3 * i + 3]))
            off += e.n
        for e, view in zip(exs, views):
            e.start(*view)
        for e, view in zip(exs, views):
            e.finish(*view)

    outs = pl.pallas_call(
        body, name=name, in_specs=[ANY] * tot, out_specs=[ANY] * tot,
        out_shape=[s for e in exs for s in e.out_shapes()],
        scratch_shapes=[s for e in exs for s in _sem_shapes(e.n)],
    )(*[s for e in exs for s in e.srcs])
    res, off = [], 0
    for e in exs:
        res.append(list(outs[off:off + e.n]))
        off += e.n
    return res


def _flash_call(body, name, exs, in_specs, out_specs, out_shape, scratch, n, args):
    return _carry_call(body, name, (MLA_HEADS // 2, n * (n + 1) // 2), in_specs, out_specs, out_shape, scratch,
                       ("parallel", "arbitrary"), args, exs)


def _tri_rows(s, n):
    at = [(s >= r * (r + 1) // 2).astype(jnp.int32) for r in range(1, n)]
    return sum(at), s - sum(a * r for a, r in zip(at, range(1, n)))


def _tri_cols(s, n):
    starts = [c * n - c * (c - 1) // 2 for c in range(n)]
    col = sum((s >= starts[c]).astype(jnp.int32) for c in range(1, n))
    start = sum(jnp.where(col == c, starts[c], 0) for c in range(n))
    return s - start + col, col


def _pair_rows(a):
    at = a.T
    return jnp.concatenate([at[0:1, :], at[MLA_V:MLA_V + 1, :], jnp.zeros((SUBLANES - 2, a.shape[0]), a.dtype)], axis=0)


def _lower_tri(t):
    return lax.broadcasted_iota(jnp.int32, (t, t), 0) >= lax.broadcasted_iota(jnp.int32, (t, t), 1)


def _upper_tri(t):
    return lax.broadcasted_iota(jnp.int32, (t, t), 1) >= lax.broadcasted_iota(jnp.int32, (t, t), 0)


def _flash_fwd(q, kt, v, exs=()):
    S = q.shape[0]
    t = min(T_ATT, S)
    n = S // t

    def body(q_ref, kt_ref, v_ref, o_ref, lse_ref, lse_t_ref, m_s, l_s, acc):
        qi, ki = _tri_rows(pl.program_id(1), n)
        lo = lax.broadcasted_iota(jnp.int32, (t, LANES), 1) < MLA_V

        @pl.when(ki == 0)
        def _():
            m_s[...] = jnp.full_like(m_s, NEG)
            l_s[...] = jnp.zeros_like(l_s)
            acc[...] = jnp.zeros_like(acc)

        keep = _lower_tri(t) | (ki < qi)
        vv = v_ref[...]
        heads = range(2)
        ss = [jnp.where(keep, _dot(q_ref[:, h * LANES:(h + 1) * LANES], kt_ref[h * LANES:(h + 1) * LANES, :]), NEG)
              for h in heads]
        m_prev = [m_s[h] for h in heads]
        l_prev = [l_s[h] for h in heads]
        m_new = [jnp.maximum(m_prev[h], jnp.max(ss[h], axis=1, keepdims=True)) for h in heads]
        al = [jnp.exp(m_prev[h] - m_new[h]) for h in heads]
        ps = [jnp.exp(ss[h] - m_new[h][:, :1]) for h in heads]
        l_new = [al[h] * l_prev[h] + jnp.sum(ps[h], axis=1, keepdims=True) for h in heads]
        pv = [_dot(ps[h].astype(BF), vv) for h in heads]
        for h in heads:
            m_s[h] = m_new[h]
            l_s[h] = l_new[h]
        acc[...] = jnp.where(lo, al[0], al[1]) * acc[...] + jnp.where(lo, pv[0], pv[1])

        @pl.when(ki == qi)
        def _():
            o_ref[...] = acc[...] / jnp.where(lo, l_s[0], l_s[1])
            lse = jnp.where(lo, m_s[0] + jnp.log(l_s[0]), m_s[1] + jnp.log(l_s[1]))
            lse_ref[0] = lse
            lse_t_ref[0] = _pair_rows(lse)

    return _flash_call(
        body, "mla_flash_fwd", exs,
        [pl.BlockSpec((t, 256), lambda p, s: (_tri_rows(s, n)[0], p)),
         pl.BlockSpec((256, t), lambda p, s: (p, _tri_rows(s, n)[1])),
         pl.BlockSpec((t, 128), lambda p, s: (_tri_rows(s, n)[1], p))],
        [pl.BlockSpec((t, 128), lambda p, s: (_tri_rows(s, n)[0], p)),
         pl.BlockSpec((1, t, 128), lambda p, s: (p, _tri_rows(s, n)[0], 0)),
         pl.BlockSpec((1, SUBLANES, t), lambda p, s: (p, 0, _tri_rows(s, n)[0]))],
        [jax.ShapeDtypeStruct((S, 512), F32), jax.ShapeDtypeStruct((MLA_HEADS // 2, S, 128), F32),
         jax.ShapeDtypeStruct((MLA_HEADS // 2, SUBLANES, S), F32)],
        [pltpu.VMEM((2, t, 128), F32), pltpu.VMEM((2, t, 128), F32), pltpu.VMEM((t, 128), F32)], n, (q, kt, v))


def _flash_bwd_dq(q, k, kt, vt, do, lse, delta, exs=()):
    S = q.shape[0]
    t = min(T_ATT, S)
    n = S // t

    def body(q_ref, k_ref, kt_ref, vt_ref, do_ref, lse_ref, dl_ref, dq_ref, acc):
        qi, ki = _tri_rows(pl.program_id(1), n)
        lo = lax.broadcasted_iota(jnp.int32, (t, LANES), 1) < MLA_V

        @pl.when(ki == 0)
        def _():
            acc[...] = jnp.zeros_like(acc)

        keep = _lower_tri(t) | (ki < qi)
        heads = range(2)
        cs = [slice(h * LANES, (h + 1) * LANES) for h in heads]
        col = [slice(h * MLA_V, h * MLA_V + 1) for h in heads]
        lse, dl, dov, vt = lse_ref[0], dl_ref[0], do_ref[...], vt_ref[...]
        ss = [jnp.where(keep, _dot(q_ref[:, cs[h]], kt_ref[cs[h], :]), NEG) for h in heads]
        dp = [_dot(jnp.where(lo if h == 0 else ~lo, dov, 0).astype(BF), vt) for h in heads]
        ds = [(jnp.exp(ss[h] - lse[:, col[h]]) * (dp[h] - dl[:, col[h]])).astype(BF) for h in heads]
        dq = [_dot(ds[h], k_ref[:, cs[h]]) for h in heads]
        acc[...] += jnp.concatenate(dq, axis=1)

        @pl.when(ki == qi)
        def _():
            dq_ref[...] = acc[...]

    (dq,), got = _flash_call(
        body, "mla_flash_dq", exs,
        [pl.BlockSpec((t, 256), lambda p, s: (_tri_rows(s, n)[0], p)),
         pl.BlockSpec((t, 256), lambda p, s: (_tri_rows(s, n)[1], p)),
         pl.BlockSpec((256, t), lambda p, s: (p, _tri_rows(s, n)[1])),
         pl.BlockSpec((128, t), lambda p, s: (p, _tri_rows(s, n)[1])),
         pl.BlockSpec((t, 128), lambda p, s: (_tri_rows(s, n)[0], p)),
         pl.BlockSpec((1, t, 128), lambda p, s: (p, _tri_rows(s, n)[0], 0)),
         pl.BlockSpec((1, t, 128), lambda p, s: (p, _tri_rows(s, n)[0], 0))],
        [pl.BlockSpec((t, 256), lambda p, s: (_tri_rows(s, n)[0], p))],
        [jax.ShapeDtypeStruct((S, 1024), F32)],
        [pltpu.VMEM((t, 256), F32)], n, (q, k, kt, vt, do, lse, delta))
    return dq, got


def _flash_bwd_dkv(q, qt, k, v, do, dot_, lse_t, delta_t, exs=()):
    S = q.shape[0]
    t = min(T_ATT, S)
    n = S // t

    def body(q_ref, qt_ref, k_ref, v_ref, do_ref, dot_ref, lse_ref, dl_ref, dk_ref, dv_ref, dk_acc, dv_acc):
        qi, ki = _tri_cols(pl.program_id(1), n)
        lo = lax.broadcasted_iota(jnp.int32, (t, LANES), 1) < MLA_V
        top = lax.broadcasted_iota(jnp.int32, (LANES, t), 0) < MLA_V

        @pl.when(qi == ki)
        def _():
            dk_acc[...] = jnp.zeros_like(dk_acc)
            dv_acc[...] = jnp.zeros_like(dv_acc)

        keep = _upper_tri(t) | (qi > ki)
        heads = range(2)
        cs = [slice(h * LANES, (h + 1) * LANES) for h in heads]
        vv, lse, dl, dov, dot_v = v_ref[...], lse_ref[0], dl_ref[0], do_ref[...], dot_ref[...]
        st = [jnp.where(keep, _dot(k_ref[:, cs[h]], qt_ref[cs[h], :]), NEG) for h in heads]
        dpt = [_dot(vv, jnp.where(top if h == 0 else ~top, dot_v, 0).astype(BF)) for h in heads]
        pt = [jnp.exp(st[h] - lse[h:h + 1, :]) for h in heads]
        dst = [(pt[h] * (dpt[h] - dl[h:h + 1, :])).astype(BF) for h in heads]
        dv = [_dot(pt[h].astype(BF), jnp.where(lo if h == 0 else ~lo, dov, 0).astype(BF)) for h in heads]
        dk = [_dot(dst[h], q_ref[:, cs[h]]) for h in heads]
        dv_acc[...] += dv[0] + dv[1]
        dk_acc[...] += jnp.concatenate(dk, axis=1)

        @pl.when(qi == n - 1)
        def _():
            dk_ref[...] = dk_acc[...]
            dv_ref[...] = dv_acc[...]

    (dk, dv), got = _flash_call(
        body, "mla_flash_dkv", exs,
        [pl.BlockSpec((t, 256), lambda p, s: (_tri_cols(s, n)[0], p)),
         pl.BlockSpec((256, t), lambda p, s: (p, _tri_cols(s, n)[0])),
         pl.BlockSpec((t, 256), lambda p, s: (_tri_cols(s, n)[1], p)),
         pl.BlockSpec((t, 128), lambda p, s: (_tri_cols(s, n)[1], p)),
         pl.BlockSpec((t, 128), lambda p, s: (_tri_cols(s, n)[0], p)),
         pl.BlockSpec((128, t), lambda p, s: (p, _tri_cols(s, n)[0])),
         pl.BlockSpec((1, SUBLANES, t), lambda p, s: (p, 0, _tri_cols(s, n)[0])),
         pl.BlockSpec((1, SUBLANES, t), lambda p, s: (p, 0, _tri_cols(s, n)[0]))],
        [pl.BlockSpec((t, 256), lambda p, s: (_tri_cols(s, n)[1], p)),
         pl.BlockSpec((t, 128), lambda p, s: (_tri_cols(s, n)[1], p))],
        [jax.ShapeDtypeStruct((S, 1024), F32), jax.ShapeDtypeStruct((S, 512), F32)],
        [pltpu.VMEM((t, 256), F32), pltpu.VMEM((t, 128), F32)], n, (q, qt, k, v, do, dot_, lse_t, delta_t))
    return dk, dv, got


def _mem_heads(qm, km_ref, vm_ref):
    ps, os_ = [], []
    for h in range(X_HEADS):
        cs = slice(h * X_HEAD_DIM, (h + 1) * X_HEAD_DIM)
        s = _dot_nt(qm[:, cs].astype(BF), km_ref[:, cs]) * MEM_SCALE
        e = jnp.exp(s - jnp.max(s, axis=1, keepdims=True))
        p = e / jnp.sum(e, axis=1, keepdims=True)
        ps.append(p)
        os_.append(_dot(p.astype(BF), vm_ref[:, cs]))
    return ps, jnp.concatenate(os_, axis=1)


def _mem_fwd(proj, km, vm, p_mem):
    S = proj.shape[0]
    tr = min(T_ROWS_WIDE, S)
    M = km.shape[0]

    def body(q_ref, z_ref, km_ref, vm_ref, p_ref, o_ref):
        _, o = _mem_heads(q_ref[...], km_ref, vm_ref)
        y = o * _silu(z_ref[...].astype(F32))
        o_ref[...] = _dot(y.astype(BF), p_ref[...]).astype(o_ref.dtype)

    return pl.pallas_call(
        body, name="mem_fwd", grid=(S // tr,),
        in_specs=[_rows(tr, 512, 4), _rows(tr, 512, 5), _full((M, 512)), _full((M, 512)), p_mem.spec()],
        out_specs=_rows(tr, 1024), out_shape=jax.ShapeDtypeStruct((S, D_MODEL), ACT),
        compiler_params=_cparams(("parallel",)),
    )(proj, proj, km, vm, p_mem.arr)


def _mem_bwd(do, proj, km, vm, p_mem):
    S = proj.shape[0]
    tr = min(T_ROWS_BWD_WIDE, S)
    M = km.shape[0]

    def body(do_ref, q_ref, z_ref, km_ref, vm_ref, p_ref, dq_ref, dz_ref, dkm_ref, dvm_ref, dp_ref):
        @pl.when(pl.program_id(0) == 0)
        def _():
            dkm_ref[...] = jnp.zeros_like(dkm_ref)
            dvm_ref[...] = jnp.zeros_like(dvm_ref)
            dp_ref[...] = jnp.zeros_like(dp_ref)

        qm = q_ref[...]
        z = z_ref[...].astype(F32)
        ps, o = _mem_heads(qm, km_ref, vm_ref)
        sz = _silu(z)
        y = o * sz
        dob = do_ref[...].astype(BF)
        dy = _dot_nt(dob, p_ref[...])
        dp_ref[...] += _dot_tn(y.astype(BF), dob)
        dz_ref[...] = (dy * o * _dsilu(z)).astype(dz_ref.dtype)
        d_o = dy * sz
        for h in range(X_HEADS):
            cs = slice(h * X_HEAD_DIM, (h + 1) * X_HEAD_DIM)
            doh = d_o[:, cs]
            dohb = doh.astype(BF)
            p = ps[h]
            dpr = _dot_nt(dohb, vm_ref[:, cs])
            ds = (p * (dpr - jnp.sum(doh * o[:, cs], axis=1, keepdims=True)) * MEM_SCALE).astype(BF)
            dq_ref[:, cs] = _dot(ds, km_ref[:, cs]).astype(dq_ref.dtype)
            dkm_ref[:, cs] += _dot_tn(ds, qm[:, cs].astype(BF))
            dvm_ref[:, cs] += _dot_tn(p.astype(BF), dohb)

    return pl.pallas_call(
        body, name="mem_bwd", grid=(S // tr,),
        in_specs=[_rows(tr, 1024), _rows(tr, 512, 4), _rows(tr, 512, 5), _full((M, 512)), _full((M, 512)),
                  p_mem.spec()],
        out_specs=(_rows(tr, 512), _rows(tr, 512), _full((M, 512)), _full((M, 512)), _full((512, 1024))),
        out_shape=(jax.ShapeDtypeStruct((S, 512), BF), jax.ShapeDtypeStruct((S, 512), BF),
                   jax.ShapeDtypeStruct((M, 512), F32), jax.ShapeDtypeStruct((M, 512), F32),
                   jax.ShapeDtypeStruct((512, 1024), F32)),
        compiler_params=_cparams(("arbitrary",)),
    )(do, proj, proj, km, vm, p_mem.arr)


def _merge_fwd(x, proj, o_ssm, o_att, o_mem, b_gate, p_mla, w_out, ln_g, ln_b):
    S = x.shape[0]
    tr = min(T_ROWS, S)

    def body(x_ref, lg_ref, z_ref, os_ref, oa_ref, om_ref, bg_ref, p_ref, w_ref, g_ref, b_ref,
             xn_ref, xb_ref, pre_ref, mg_ref):
        gates = _sigmoid(lg_ref[...].astype(F32) + bg_ref[0])
        ya = oa_ref[...] * _silu(z_ref[...].astype(F32))
        o_mla = _dot(ya.astype(BF), p_ref[...])
        merged = (gates[:, :D_MODEL] * os_ref[...].astype(F32) + gates[:, D_MODEL:2 * D_MODEL] * o_mla
                  + gates[:, 2 * D_MODEL:] * om_ref[...].astype(F32))
        mb = merged.astype(BF)
        mg_ref[...] = mb
        pre = ALPHA * x_ref[...] + _dot(mb, w_ref[...])
        pre_ref[...] = pre
        mu = jnp.mean(pre, axis=-1, keepdims=True)
        xc = pre - mu
        var = jnp.mean(xc * xc, axis=-1, keepdims=True)
        xn = xc * lax.rsqrt(var + NORM_EPS) * g_ref[0] + b_ref[0]
        xn_ref[...] = xn
        xb_ref[...] = xn.astype(BF)

    return pl.pallas_call(
        body, name="merge_fwd", grid=(S // tr,),
        in_specs=[_rows(tr, 1024), _rows(tr, 3072, 1), _rows(tr, 512, 3), _rows(tr, 1024), _rows(tr, 512),
                  _rows(tr, 1024), b_gate.spec(), p_mla.spec(), _full((1024, 1024)), ln_g.spec(), ln_b.spec()],
        out_specs=(_rows(tr, 1024), _rows(tr, 1024), _rows(tr, 1024), _rows(tr, 1024)),
        out_shape=(jax.ShapeDtypeStruct((S, 1024), F32), jax.ShapeDtypeStruct((S, 1024), BF),
                   jax.ShapeDtypeStruct((S, 1024), F32), jax.ShapeDtypeStruct((S, 1024), BF)),
        compiler_params=_cparams(("parallel",)),
    )(x, proj, proj, o_ssm, o_att, o_mem, b_gate.arr, p_mla.arr, w_out, ln_g.arr, ln_b.arr)


def _merge_bwd(dxn, pre, merged, proj, o_ssm, o_att, o_mem, b_gate, p_mla, w_out, ln_g):
    S = pre.shape[0]
    tr = min(T_ROWS_BWD, S)

    def body(dxn_ref, pre_ref, mg_ref, lg_ref, z_ref, os_ref, oa_ref, om_ref, bg_ref, p_ref, w_ref, g_ref,
             dxr_ref, dlg_ref, dos_ref, dom_ref, doa_ref, dz_ref, doat_ref, dl_ref, dlt_ref, dw_ref, dp_ref, dbg_ref,
             dg_ref, db_ref):
        @pl.when(pl.program_id(0) == 0)
        def _():
            dw_ref[...] = jnp.zeros_like(dw_ref)
            dp_ref[...] = jnp.zeros_like(dp_ref)
            dbg_ref[...] = jnp.zeros_like(dbg_ref)
            dg_ref[...] = jnp.zeros_like(dg_ref)
            db_ref[...] = jnp.zeros_like(db_ref)

        dxn = dxn_ref[...]
        pre = pre_ref[...]
        mu = jnp.mean(pre, axis=-1, keepdims=True)
        xc = pre - mu
        rstd = lax.rsqrt(jnp.mean(xc * xc, axis=-1, keepdims=True) + NORM_EPS)
        xhat = xc * rstd
        dg_ref[...] += jnp.sum(dxn * xhat, axis=0, keepdims=True)
        db_ref[...] += jnp.sum(dxn, axis=0, keepdims=True)
        dxh = dxn * g_ref[0]
        dpre = rstd * (dxh - jnp.mean(dxh, axis=-1, keepdims=True)
                       - xhat * jnp.mean(dxh * xhat, axis=-1, keepdims=True))
        dxr_ref[...] = ALPHA * dpre
        dpb = dpre.astype(BF)
        dw_ref[...] += _dot_tn(mg_ref[...], dpb)
        dm = _dot_nt(dpb, w_ref[...])

        gates = _sigmoid(lg_ref[...].astype(F32) + bg_ref[0])
        g0, g1, g2 = gates[:, :D_MODEL], gates[:, D_MODEL:2 * D_MODEL], gates[:, 2 * D_MODEL:]
        z = z_ref[...].astype(F32)
        oa = oa_ref[...]
        sz = _silu(z)
        ya = (oa * sz).astype(BF)
        o_mla = _dot(ya, p_ref[...])
        dos_ref[...] = (g0 * dm).astype(dos_ref.dtype)
        dom_ref[...] = (g2 * dm).astype(dom_ref.dtype)
        do_mla = (g1 * dm).astype(BF)
        dl0 = dm * os_ref[...].astype(F32) * g0 * (1.0 - g0)
        dl1 = dm * o_mla * g1 * (1.0 - g1)
        dl2 = dm * om_ref[...].astype(F32) * g2 * (1.0 - g2)
        dl = jnp.concatenate([dl0, dl1, dl2], axis=1)
        dbg_ref[...] += jnp.sum(dl, axis=0, keepdims=True)
        dlg_ref[...] = dl.astype(dlg_ref.dtype)
        dp_ref[...] += _dot_tn(ya, do_mla)
        dya = _dot_nt(do_mla, p_ref[...])
        doa = dya * sz
        doab = doa.astype(BF)
        doa_ref[...] = doab
        doat_ref[...] = doab.T
        dz_ref[...] = (dya * oa * _dsilu(z)).astype(dz_ref.dtype)
        prod = doa * oa
        lo = lax.broadcasted_iota(jnp.int32, (tr, LANES), 1) < MLA_V
        for pr in range(MLA_HEADS // 2):
            blk = prod[:, pr * LANES:(pr + 1) * LANES]
            d0 = jnp.sum(jnp.where(lo, blk, 0.0), axis=1, keepdims=True)
            d1 = jnp.sum(jnp.where(lo, 0.0, blk), axis=1, keepdims=True)
            dl = jnp.where(lo, d0, d1)
            dl_ref[pr] = dl
            dlt_ref[pr] = _pair_rows(dl)

    return pl.pallas_call(
        body, name="merge_bwd", grid=(S // tr,),
        in_specs=[_rows(tr, 1024), _rows(tr, 1024), _rows(tr, 1024), _rows(tr, 3072, 1), _rows(tr, 512, 3),
                  _rows(tr, 1024), _rows(tr, 512), _rows(tr, 1024), b_gate.spec(), p_mla.spec(),
                  _full((1024, 1024)), ln_g.spec()],
        out_specs=(_rows(tr, 1024), _rows(tr, 3072), _rows(tr, 1024), _rows(tr, 1024), _rows(tr, 512),
                   _rows(tr, 512), _cols(512, tr), pl.BlockSpec((MLA_HEADS // 2, tr, 128), lambda i: (0, i, 0)),
                   pl.BlockSpec((MLA_HEADS // 2, SUBLANES, tr), lambda i: (0, 0, i)),
                   _full((1024, 1024)), _full((512, 1024)), _full((1, 3072)), _full((1, 1024)), _full((1, 1024))),
        out_shape=(jax.ShapeDtypeStruct((S, 1024), F32), jax.ShapeDtypeStruct((S, 3072), BF),
                   jax.ShapeDtypeStruct((S, 1024), BF), jax.ShapeDtypeStruct((S, 1024), BF),
                   jax.ShapeDtypeStruct((S, 512), BF), jax.ShapeDtypeStruct((S, 512), BF),
                   jax.ShapeDtypeStruct((512, S), BF), jax.ShapeDtypeStruct((MLA_HEADS // 2, S, 128), F32),
                   jax.ShapeDtypeStruct((MLA_HEADS // 2, SUBLANES, S), F32),
                   jax.ShapeDtypeStruct((1024, 1024), F32), jax.ShapeDtypeStruct((512, 1024), F32),
                   jax.ShapeDtypeStruct((1, 3072), F32), jax.ShapeDtypeStruct((1, 1024), F32),
                   jax.ShapeDtypeStruct((1, 1024), F32)),
        compiler_params=_cparams(("arbitrary",)),
    )(dxn, pre, merged, proj, proj, o_ssm, o_att, o_mem, b_gate.arr, p_mla.arr, w_out, ln_g.arr)


def _loss_head(y, t):
    S = y.shape[0]
    tr = min(T_ROWS, S)
    n = S // tr

    def body(y_ref, t_ref, dy_ref, l_ref, acc):
        i = pl.program_id(0)

        @pl.when(i == 0)
        def _():
            acc[...] = jnp.zeros_like(acc)

        e = y_ref[...] - t_ref[...]
        dy_ref[...] = e * (1.0 / D_MODEL)
        acc[...] += jnp.sum(e * e, axis=0, keepdims=True)

        @pl.when(i == n - 1)
        def _():
            tot = jnp.sum(acc[...], axis=1, keepdims=True) * (0.5 / D_MODEL)
            l_ref[...] = jnp.broadcast_to(tot, l_ref.shape)

    return pl.pallas_call(
        body, name="loss_head", grid=(n,),
        in_specs=[_rows(tr, 1024), _rows(tr, 1024)],
        out_specs=(_rows(tr, 1024), _full((SUBLANES, LANES))),
        out_shape=(jax.ShapeDtypeStruct((S, 1024), F32), jax.ShapeDtypeStruct((SUBLANES, LANES), F32)),
        scratch_shapes=[pltpu.VMEM((1, 1024), F32)],
        compiler_params=_cparams(("arbitrary",)),
    )(y, t)


def _rope_tables(pos):
    inv_freq = ROPE_THETA ** (-jnp.arange(0, MLA_ROPE, 2, dtype=F32) / MLA_ROPE)
    ang = pos.astype(F32)[:, None] * inv_freq
    cos, sin = jnp.cos(ang), jnp.sin(ang)
    S = pos.shape[0]
    half = MLA_ROPE // 2
    ones = jnp.ones((S, MLA_NOPE), F32)
    z16 = jnp.zeros((S, half), F32)
    z32 = jnp.zeros((S, LANES - MLA_NOPE - MLA_ROPE), F32)
    z64 = jnp.zeros((S, MLA_NOPE), F32)
    c = jnp.concatenate([ones, cos, cos, z32], axis=1)
    sa = jnp.concatenate([z64, -sin, z16, z32], axis=1)
    sb = jnp.concatenate([z64, z16, sin, z32], axis=1)
    return c, sa, sb


def _ssm_discretise(a_re, a_im, log_dt, b_re, b_im):
    dt = jnp.exp(log_dt)[..., None]
    mag = jnp.exp(a_re * dt)
    lb_re = mag * jnp.cos(a_im * dt)
    lb_im = mag * jnp.sin(a_im * dt)
    nr, ni = lb_re - 1.0, lb_im
    den = a_re * a_re + a_im * a_im
    f_re = (nr * a_re + ni * a_im) / den
    f_im = (ni * a_re - nr * a_im) / den
    bb_re = f_re[..., None] * b_re - f_im[..., None] * b_im
    bb_im = f_re[..., None] * b_im + f_im[..., None] * b_re
    return lb_re, lb_im, bb_re, bb_im


_GPB = SSM_CB // SSM_GROUP


def _bd_in(bb):
    nb = SSM_GROUPS // _GPB
    t = bb.reshape(nb, _GPB, SSM_STATE, SSM_GROUP)
    eye = jnp.eye(_GPB, dtype=bb.dtype)
    return jnp.einsum("ngpc,gh->ngchp", t, eye).reshape(nb, SSM_CB, _GPB * SSM_STATE)


def _bd_in_t(d):
    nb = SSM_GROUPS // _GPB
    t = d.reshape(nb, _GPB, SSM_GROUP, _GPB, SSM_STATE)
    eye = jnp.eye(_GPB, dtype=d.dtype)
    return jnp.einsum("ngchp,gh->ngpc", t, eye).reshape(SSM_GROUPS, SSM_STATE, SSM_GROUP)


def _bd_out(c):
    nb = SSM_GROUPS // _GPB
    t = c.reshape(nb, _GPB, SSM_GROUP, SSM_STATE)
    eye = jnp.eye(_GPB, dtype=c.dtype)
    return jnp.einsum("ngcp,gh->ngphc", t, eye).reshape(nb, _GPB * SSM_STATE, SSM_CB)


def _interleave(a):
    S, w = a.shape
    return a.reshape(SUBLANES, S // SUBLANES, w).transpose(1, 0, 2).reshape(S, w)


def _deinterleave(a):
    S, w = a.shape
    return a.reshape(S // SUBLANES, SUBLANES, w).transpose(1, 0, 2).reshape(S, w)


IN_SHARD = D_IN // N_DEV
ROPE_OWNER = ROPE_SLOT_LO // IN_SHARD
assert ROPE_OWNER * IN_SHARD <= ROPE_SLOT_LO and ROPE_SLOT_LO + MLA_ROPE <= (ROPE_OWNER + 1) * IN_SHARD


def _w_in_from_shards(g):
    pieces = []
    for j in range(N_DEV):
        if j == ROPE_OWNER:
            a = ROPE_SLOT_LO - j * IN_SHARD
            z = lambda n: jnp.zeros((g.shape[1], n), g.dtype)
            pieces += [g[j][:, :a], z(MLA_NOPE), g[j][:, a:a + MLA_ROPE], z(LANES - MLA_NOPE - MLA_ROPE),
                       g[j][:, a + MLA_ROPE:]]
        else:
            pieces.append(g[j])
    return jnp.concatenate(pieces, axis=1)


def _w_in_to_shards(d):
    shift = LANES - MLA_ROPE
    out = []
    for j in range(N_DEV):
        lo, hi = j * IN_SHARD, (j + 1) * IN_SHARD
        if j < ROPE_OWNER:
            out.append(d[:, lo:hi])
        elif j > ROPE_OWNER:
            out.append(d[:, lo + shift:hi + shift])
        else:
            r = ROPE_SLOT_LO + MLA_NOPE
            out.append(jnp.concatenate([d[:, lo:ROPE_SLOT_LO], d[:, r:r + MLA_ROPE],
                                        d[:, ROPE_SLOT_LO + LANES:hi + shift]], axis=1))
    return jnp.stack(out)


def _adamw_math(w, g, m, v):
    m = ADAM_B1 * m + (1.0 - ADAM_B1) * g
    v = ADAM_B2 * v + (1.0 - ADAM_B2) * (g * g)
    m_hat = m / (1.0 - ADAM_B1 ** ADAM_STEP)
    v_hat = v / (1.0 - ADAM_B2 ** ADAM_STEP)
    delta = -ADAM_LR * (m_hat / (jnp.sqrt(v_hat) + ADAM_EPS) + ADAM_WD * w)
    return delta, m, v


def _adamw_sharded(parts, w, m, v, tile, name):
    L, _, R, C = parts.shape
    assert R % tile == 0

    def body(p_ref, w_ref, m_ref, v_ref, g_out, d_out, m_out, v_out):
        g = p_ref[0, 0].astype(F32)
        for k in range(1, N_DEV):
            g = g + p_ref[0, k].astype(F32)
        d, mn, vn = _adamw_math(w_ref[0], g, m_ref[0], v_ref[0])
        g_out[0] = g
        d_out[0] = d
        m_out[0] = mn
        v_out[0] = vn

    spec = pl.BlockSpec((1, tile, C), lambda l, i: (l, i, 0))
    shp = jax.ShapeDtypeStruct((L, R, C), F32)
    return pl.pallas_call(
        body, name=name, grid=(L, R // tile),
        in_specs=[pl.BlockSpec((1, N_DEV, tile, C), lambda l, i: (l, 0, i, 0)), spec, spec, spec],
        out_specs=(spec,) * 4, out_shape=(shp,) * 4, compiler_params=_cparams(("parallel", "parallel")),
    )(parts, w, m, v)


COL_GROUP = (("w_glu", 512), ("p_ssm", 512), ("p_mla", 512), ("p_mem", 512), ("w_uq", 256), ("w_ukv", 128))
COL_AT = {n: sum(r for _, r in COL_GROUP[:i]) // rows for i, (n, rows) in enumerate(COL_GROUP)}
assert all(sum(r for _, r in COL_GROUP[:i]) % rows == 0 for i, (_, rows) in enumerate(COL_GROUP))
COL_ROWS = dict(COL_GROUP)
ROW_GROUP = ("w_mem_kv", "w_out")
SMALL = ("b_gate", "ssm_a_re", "ssm_a_im", "ssm_log_dt", "ssm_b_re", "ssm_b_im", "ssm_c_re", "ssm_c_im", "ssm_d",
         "b_glu", "mla_q_norm", "mla_kv_norm", "ln_g", "ln_b")
UQ_COLS = MLA_NOPE + MLA_ROPE


def _pad_lanes(a):
    return jnp.concatenate([a, jnp.zeros(a.shape[:-1] + (LANES - a.shape[-1],), a.dtype)], axis=-1)


def _group_buffers(d, dtype):
    col = jnp.concatenate([_pad_lanes(d[n]) if n == "w_uq" else d[n] for n, _ in COL_GROUP], axis=1)
    row = jnp.concatenate([d[n] for n in ROW_GROUP], axis=1)
    return d["w_in"].astype(dtype), col.astype(dtype), row.astype(dtype)


def _ungroup(bufs):
    b_in, col, row = bufs
    out, off = {"w_in": b_in}, 0
    for n, rows in COL_GROUP:
        t = col[:, off:off + rows]
        out[n] = t[..., :UQ_COLS] if n == "w_uq" else t
        off += rows
    k = row.shape[1] // 2
    out["w_mem_kv"], out["w_out"] = row[:, :k], row[:, k:]
    return out


def _colcat(t):
    return t.transpose(1, 0, 2).reshape(t.shape[1], -1)


def _colsplit(g, n):
    return g.reshape(g.shape[0], N_DEV, n).transpose(1, 0, 2)


def _unpack_weights(g_in, g_col, g_row):
    wc = _colcat(g_col)
    at = lambda n: _RowBlock(wc, COL_ROWS[n], COL_AT[n])
    lo = COL_AT["w_ukv"] * COL_ROWS["w_ukv"]
    ukv = wc[lo:lo + COL_ROWS["w_ukv"]].reshape(-1, MLA_HEADS, LANES)
    lane = lax.broadcasted_iota(jnp.int32, ukv.shape, 2)
    k = g_row.shape[1] // 2
    return dict(
        w_in=_w_in_from_shards(g_in), w_glu=at("w_glu"), w_uq=at("w_uq"), p_ssm=at("p_ssm"), p_mla=at("p_mla"),
        p_mem=at("p_mem"), w_k=jnp.where(lane < MLA_NOPE, ukv, jnp.zeros_like(ukv)).reshape(ukv.shape[0], -1),
        w_v=ukv[..., MLA_NOPE:].reshape(ukv.shape[0], -1),
        w_mem_kv=g_row[:, :k].reshape(-1, g_row.shape[2]), w_out=g_row[:, k:].reshape(-1, g_row.shape[2]))


def _pack_grads_in(d_w_in):
    return _w_in_to_shards(d_w_in).astype(BF)


def _pack_grads_rest(d):
    ukv = jnp.concatenate([d["w_k"].reshape(-1, MLA_HEADS, LANES)[..., :MLA_NOPE],
                           d["w_v"].reshape(-1, MLA_HEADS, MLA_V)], axis=-1).reshape(d["w_k"].shape[0], -1)
    col = jnp.concatenate([ukv if n == "w_ukv" else d[n] for n, _ in COL_GROUP], axis=0)
    row = jnp.concatenate([d[n].reshape(N_DEV, -1, d[n].shape[1]) for n in ROW_GROUP], axis=1)
    return [_colsplit(col, LANES).astype(BF), row.astype(BF)]


def _pack_small(d, lead):
    parts = []
    for n in SMALL:
        keep = d[n].shape[:lead]
        f = d[n].reshape(keep + (-1,))
        pad = (-f.shape[-1]) % (SUBLANES * LANES)
        if pad:
            f = jnp.concatenate([f, jnp.zeros(keep + (pad,), f.dtype)], axis=-1)
        parts.append(f.reshape(keep + (-1, LANES)))
    return jnp.concatenate(parts, axis=lead)


def _unpack_small(buf, like):
    out, off = {}, 0
    for n in SMALL:
        size = math.prod(like[n].shape[1:])
        rows = -(-size // (SUBLANES * LANES)) * SUBLANES
        out[n] = buf[:, off:off + rows].reshape(buf.shape[0], -1)[:, :size].reshape(like[n].shape)
        off += rows
    return out


WEIGHTS = ("w_in", "b_gate", "ssm_a_re", "ssm_a_im", "ssm_log_dt", "ssm_b_re", "ssm_b_im", "ssm_c_re", "ssm_c_im",
           "ssm_d", "w_glu", "b_glu", "mla_q_norm", "w_uq", "mla_kv_norm", "w_ukv", "w_mem_kv", "p_ssm", "p_mla",
           "p_mem", "w_out", "ln_g", "ln_b")
BIG = ("w_in",) + tuple(n for n, _ in COL_GROUP) + ROW_GROUP


def _train_step(x, mem, pos, target, wl, ws):
    S = x.shape[0]
    tc, tsa, tsb = _rope_tables(pos)
    loc = _group_buffers(wl, BF)
    loc = [[b[l] for b in loc] for l in range(DEPTH)]

    lb_re, lb_im, bb_re, bb_im = _ssm_discretise(ws["ssm_a_re"], ws["ssm_a_im"], ws["ssm_log_dt"], ws["ssm_b_re"],
                                                 ws["ssm_b_im"])
    nb = SSM_GROUPS // _GPB
    mats = (jax.vmap(_bd_in)(bb_re), jax.vmap(_bd_in)(bb_im), lb_re.reshape(DEPTH, nb, 1, -1),
            lb_im.reshape(DEPTH, nb, 1, -1), jax.vmap(_bd_out)(ws["ssm_c_re"]), jax.vmap(_bd_out)(ws["ssm_c_im"]),
            ws["ssm_d"].reshape(DEPTH, 1, -1))

    rows3 = {n: ws[n].reshape(DEPTH, 1, -1) for n in ("b_glu", "mla_q_norm", "mla_kv_norm", "b_gate", "ln_g", "ln_b")}

    def small(n, l):
        return _LayerRow(rows3[n], l)

    ((g_in,),) = _exchange_call("weights_gather_first", [_Exchange("ag", loc[0][:1])])
    W = [None] * DEPTH
    saved = []
    xs, xb = x, x.astype(BF)
    for l in range(DEPTH):
        if l == 0:
            proj, (g_rest,) = _mm(xb, _w_in_from_shards(g_in), name="proj_fwd", tm=S, tn=512, out_dtype=ACT,
                                  exs=[_Exchange("ag", loc[0][1:])])
            W[0] = _unpack_weights(g_in, *g_rest)
        else:
            proj = _mm(xb, W[l]["w_in"], name="proj_fwd", tm=S, tn=512, out_dtype=ACT)
        w = W[l]
        u_il = _interleave(proj[:, :SSM_WIDTH])
        y_raw = _deinterleave(_ssm_fwd(u_il, mats, l))
        o_ssm = _ssm_post_fwd(y_raw, proj, w["w_glu"], small("b_glu", l), w["p_ssm"])
        q, k, v, qt, kt, vt = _mla_pre_fwd(proj, small("mla_q_norm", l), small("mla_kv_norm", l), w["w_uq"], w["w_k"], w["w_v"],
                               tc, tsa, tsb)
        nxt = [_Exchange("ag", loc[l + 1])] if l + 1 < DEPTH else []
        (o_att, lse, lse_t), gathered = _flash_fwd(q, kt, v, nxt)
        if nxt:
            W[l + 1] = _unpack_weights(*gathered[0])
        kvm = _mm(mem, w["w_mem_kv"], name="memkv_fwd", out_dtype=BF)
        km, vm = kvm[:, :512], kvm[:, 512:]
        o_mem = _mem_fwd(proj, km, vm, w["p_mem"])
        xn, xnb, pre, merged = _merge_fwd(xs, proj, o_ssm, o_att, o_mem, small("b_gate", l), w["p_mla"], w["w_out"],
                                          small("ln_g", l), small("ln_b", l))
        saved.append(dict(xb=xb, proj=proj, u_il=u_il, y_raw=y_raw, o_ssm=o_ssm, q=q, k=k, v=v, qt=qt, kt=kt, vt=vt, o_att=o_att,
                          lse=lse, lse_t=lse_t,
                          km=km, vm=vm, o_mem=o_mem, pre=pre, merged=merged))
        xs, xb = xn, xnb

    dxs, lvec = _loss_head(xs, target)
    loss = lvec[0, 0]

    disc_names = ("ssm_a_re", "ssm_a_im", "ssm_log_dt", "ssm_b_re", "ssm_b_im")
    got = [None] * DEPTH
    got_small = [None] * DEPTH
    pending = None
    pending_small = None
    for l in reversed(range(DEPTH)):
        sv, w = saved[l], W[l]
        proj = sv["proj"]
        (dx_res, dlg, do_ssm, do_mem, do_att, dz_mla, do_att_t, delta, delta_t, d_w_out, d_p_mla, d_b_gate, d_ln_g,
         d_ln_b) = _merge_bwd(
            dxs, sv["pre"], sv["merged"], proj, sv["o_ssm"], sv["o_att"], sv["o_mem"], small("b_gate", l), w["p_mla"],
            w["w_out"], small("ln_g", l))
        dq_mem, dz_mem, d_km, d_vm, d_p_mem = _mem_bwd(do_mem, proj, sv["km"], sv["vm"], w["p_mem"])
        d_w_mem = _mm(mem, jnp.concatenate([d_km, d_vm], axis=1), name="memkv_bwd", ta=True)
        dq, arrived_rest = _flash_bwd_dq(
            sv["q"], sv["k"], sv["kt"], sv["vt"], do_att, sv["lse"], delta,
            [_Exchange("a2a", pending[1:]), _Exchange("ag", [pending_small])] if pending is not None else [])
        dk, dv, arrived_in = _flash_bwd_dkv(sv["q"], sv["qt"], sv["k"], sv["v"], do_att, do_att_t, sv["lse_t"], delta_t,
                                            [_Exchange("a2a", pending[:1])] if pending is not None else [])
        if pending is not None:
            got[l + 1] = arrived_in[0] + arrived_rest[0]
            got_small[l + 1] = arrived_rest[1][0]
        dcq, dckv, dslot, d_wuq, d_wk, d_wv, d_qn, d_kn = _mla_pre_bwd(
            dq, dk, dv, proj, small("mla_q_norm", l), small("mla_kv_norm", l), w["w_uq"], w["w_k"], w["w_v"],
            tc, tsa, tsb)
        dy_raw, dz_ssm, d_w_glu, d_b_glu, d_p_ssm = _ssm_post_bwd(do_ssm, sv["y_raw"], proj, w["w_glu"],
                                                                 small("b_glu", l), w["p_ssm"])
        rest = _pack_grads_rest(dict(w_glu=d_w_glu, w_uq=d_wuq, w_k=d_wk, w_v=d_wv, w_mem_kv=d_w_mem, p_ssm=d_p_ssm,
                                     p_mla=d_p_mla, p_mem=d_p_mem, w_out=d_w_out))
        (du_il, dbbre, dbbim, dare, daim, dcre, dcim, dd), early = _ssm_bwd(
            sv["u_il"], _interleave(dy_raw), mats, l, [_Exchange("a2a", rest)] if l == 0 else [])
        du = _deinterleave(du_il).astype(BF)
        _, disc_vjp = jax.vjp(_ssm_discretise, *[ws[n][l] for n in disc_names])
        d_disc = disc_vjp((dare.reshape(SSM_GROUPS, SSM_STATE), daim.reshape(SSM_GROUPS, SSM_STATE), _bd_in_t(dbbre),
                           _bd_in_t(dbbim)))
        dproj = jnp.concatenate([du, dz_ssm, dcq, dckv, dslot, dz_mla, dq_mem, dz_mem, dlg], axis=1)
        d_w_in = _mm(sv["xb"], dproj, name="proj_dw", ta=True, tm=1024, tn=512, tk=S)
        if l > 0:
            dxs = _mm(dproj, w["w_in"], name="proj_dx", tb=True, add=dx_res, tm=1024, tn=1024, tk=1024)
        pending = [_pack_grads_in(d_w_in)] + (rest if l > 0 else [])
        gsl = dict(zip(disc_names, d_disc))
        gsl.update(b_gate=d_b_gate, ssm_c_re=_bd_in_t(dcre).transpose(0, 2, 1), ssm_c_im=_bd_in_t(dcim).transpose(0, 2, 1),
                   ssm_d=dd, b_glu=d_b_glu, mla_q_norm=d_qn, mla_kv_norm=d_kn, ln_g=d_ln_g, ln_b=d_ln_b)
        pending_small = _pack_small(gsl, 0)

    dxs, (last_in, (got_small[0],)) = _mm(
        dproj, w["w_in"], name="proj_dx", tb=True, add=dx_res, tm=1024, tn=1024, tk=1024,
        exs=[_Exchange("a2a", pending), _Exchange("ag", [pending_small])])
    got[0] = last_in + early[0]
    return loss, dxs, got, got_small


def kernel(x, mem, positions, w_in, b_gate, ssm_a_re, ssm_a_im, ssm_log_dt, ssm_b_re, ssm_b_im, ssm_c_re, ssm_c_im, ssm_d, w_glu, b_glu, mla_q_norm, w_uq, mla_kv_norm, w_ukv, w_mem_kv, p_ssm, p_mla, p_mem, w_out, ln_g, ln_b, loss_target, m_w_in, m_b_gate, m_ssm_a_re, m_ssm_a_im, m_ssm_log_dt, m_ssm_b_re, m_ssm_b_im, m_ssm_c_re, m_ssm_c_im, m_ssm_d, m_w_glu, m_b_glu, m_mla_q_norm, m_w_uq, m_mla_kv_norm, m_w_ukv, m_w_mem_kv, m_p_ssm, m_p_mla, m_p_mem, m_w_out, m_ln_g, m_ln_b, v_w_in, v_b_gate, v_ssm_a_re, v_ssm_a_im, v_ssm_log_dt, v_ssm_b_re, v_ssm_b_im, v_ssm_c_re, v_ssm_c_im, v_ssm_d, v_w_glu, v_b_glu, v_mla_q_norm, v_w_uq, v_mla_kv_norm, v_w_ukv, v_w_mem_kv, v_p_ssm, v_p_mla, v_p_mem, v_w_out, v_ln_g, v_ln_b):
    w = dict(w_in=w_in, b_gate=b_gate, ssm_a_re=ssm_a_re, ssm_a_im=ssm_a_im, ssm_log_dt=ssm_log_dt, ssm_b_re=ssm_b_re,
             ssm_b_im=ssm_b_im, ssm_c_re=ssm_c_re, ssm_c_im=ssm_c_im, ssm_d=ssm_d, w_glu=w_glu, b_glu=b_glu,
             mla_q_norm=mla_q_norm, w_uq=w_uq, mla_kv_norm=mla_kv_norm, w_ukv=w_ukv, w_mem_kv=w_mem_kv, p_ssm=p_ssm,
             p_mla=p_mla, p_mem=p_mem, w_out=w_out, ln_g=ln_g, ln_b=ln_b)
    m = dict(w_in=m_w_in, b_gate=m_b_gate, ssm_a_re=m_ssm_a_re, ssm_a_im=m_ssm_a_im, ssm_log_dt=m_ssm_log_dt,
             ssm_b_re=m_ssm_b_re, ssm_b_im=m_ssm_b_im, ssm_c_re=m_ssm_c_re, ssm_c_im=m_ssm_c_im, ssm_d=m_ssm_d,
             w_glu=m_w_glu, b_glu=m_b_glu, mla_q_norm=m_mla_q_norm, w_uq=m_w_uq, mla_kv_norm=m_mla_kv_norm,
             w_ukv=m_w_ukv, w_mem_kv=m_w_mem_kv, p_ssm=m_p_ssm, p_mla=m_p_mla, p_mem=m_p_mem, w_out=m_w_out,
             ln_g=m_ln_g, ln_b=m_ln_b)
    v = dict(w_in=v_w_in, b_gate=v_b_gate, ssm_a_re=v_ssm_a_re, ssm_a_im=v_ssm_a_im, ssm_log_dt=v_ssm_log_dt,
             ssm_b_re=v_ssm_b_re, ssm_b_im=v_ssm_b_im, ssm_c_re=v_ssm_c_re, ssm_c_im=v_ssm_c_im, ssm_d=v_ssm_d,
             w_glu=v_w_glu, b_glu=v_b_glu, mla_q_norm=v_mla_q_norm, w_uq=v_w_uq, mla_kv_norm=v_mla_kv_norm,
             w_ukv=v_w_ukv, w_mem_kv=v_w_mem_kv, p_ssm=v_p_ssm, p_mla=v_p_mla, p_mem=v_p_mem, w_out=v_w_out,
             ln_g=v_ln_g, ln_b=v_ln_b)

    wl = {n: w[n] for n in BIG}
    small = {n: w[n] for n in SMALL}
    loss_local, dx, got, got_small = _train_step(x[0], mem[0], positions[0], loss_target[0], wl, small)
    loss = lax.psum(loss_local, ("x", "y", "c"))

    grads, delta, new_m, new_v = {}, {}, {}, {}
    wg = _group_buffers(wl, F32)
    mg = _group_buffers({n: m[n] for n in BIG}, F32)
    vg = _group_buffers({n: v[n] for n in BIG}, F32)
    res = []
    for i, (tile, tag) in enumerate(((256, "in"), (128, "col"), (256, "row"))):
        parts = jnp.stack([got[l][i] for l in range(DEPTH)])
        res.append(_adamw_sharded(parts, wg[i], mg[i], vg[i], tile, "adamw_" + tag))
    for dst, j in ((grads, 0), (delta, 1), (new_m, 2), (new_v, 3)):
        dst.update(_ungroup([r[j] for r in res]))

    sw, sm, sv = (_pack_small(small, 1), _pack_small({n: m[n] for n in SMALL}, 1), _pack_small({n: v[n] for n in SMALL}, 1))
    rs = _adamw_sharded(jnp.stack(got_small), sw, sm, sv, sw.shape[1], "adamw_replicated")
    for dst, buf in zip((grads, delta, new_m, new_v), rs):
        dst.update(_unpack_small(buf, small))

    return (loss, dx[None], *[grads[n] for n in WEIGHTS], *[delta[n] for n in WEIGHTS],
            *[new_m[n] for n in WEIGHTS], *[new_v[n] for n in WEIGHTS])
```

```python
import math

import jax
import jax.numpy as jnp
from jax import lax
from jax.experimental import pallas as pl
from jax.experimental.pallas import tpu as pltpu

F32 = jnp.float32
BF = jnp.bfloat16
ACT = jnp.bfloat16

D_MODEL = 1024
DEPTH = 4
N_DEV = 8
SSM_WIDTH = 512
SSM_GROUP = 16
SSM_GROUPS = 32
SSM_STATE = 64
MLA_HEADS = 8
MLA_NOPE = 64
MLA_ROPE = 32
MLA_V = 64
MLA_Q_RANK = 256
MLA_KV_RANK = 128
ROPE_THETA = 10000.0
X_HEADS = 4
X_HEAD_DIM = 128
D_IN = 6048
ALPHA = (2 * DEPTH) ** 0.25
NORM_EPS = 1e-5
ADAM_LR = 0.001
ADAM_B1 = 0.9
ADAM_B2 = 0.999
ADAM_EPS = 1e-08
ADAM_WD = 0.01
ADAM_STEP = 10

LANES = 128
SUBLANES = 8
VMEM_LIMIT = 56 * 1024 * 1024

PW = 6144
ROPE_SLOT_LO = 1408
MLA_SCALE = (MLA_NOPE + MLA_ROPE) ** -0.5
MEM_SCALE = X_HEAD_DIM ** -0.5
NEG = -1e30

T_ROWS = 512
T_ROWS_WIDE = 1024
T_ROWS_BWD = 256
T_ROWS_BWD_WIDE = 1024
T_ATT = 1024
T_MM = 512

MESH = pl.DeviceIdType.MESH


def _cparams(sem):
    return pltpu.CompilerParams(dimension_semantics=sem, vmem_limit_bytes=VMEM_LIMIT)


def _dot(a, b):
    return lax.dot_general(a, b, (((1,), (0,)), ((), ())), preferred_element_type=F32)


def _dot_nt(a, b):
    return lax.dot_general(a, b, (((1,), (1,)), ((), ())), preferred_element_type=F32)


def _dot_tn(a, b):
    return lax.dot_general(a, b, (((0,), (0,)), ((), ())), preferred_element_type=F32)


def _sigmoid(x):
    return 0.5 * jnp.tanh(0.5 * x) + 0.5


def _silu(x):
    return x * _sigmoid(x)


def _dsilu(x):
    s = _sigmoid(x)
    return s * (1.0 + x * (1.0 - s))


_GELU_C = math.sqrt(2.0 / math.pi)


def _gelu(x):
    return 0.5 * x * (1.0 + jnp.tanh(_GELU_C * (x + 0.044715 * x * x * x)))


def _dgelu(x):
    t = jnp.tanh(_GELU_C * (x + 0.044715 * x * x * x))
    return 0.5 * (1.0 + t) + 0.5 * x * (1.0 - t * t) * _GELU_C * (1.0 + 3 * 0.044715 * x * x)


def _rows(tr, w, col=0):
    return pl.BlockSpec((tr, w), lambda i: (i, col))


def _cols(h, tc):
    return pl.BlockSpec((h, tc), lambda i: (0, i))


def _full(shape):
    n = len(shape)
    return pl.BlockSpec(shape, lambda i: (0,) * n)


class _RowBlock:
    def __init__(self, arr, rows, blk):
        self.arr, self.rows, self.blk = arr, rows, blk

    def spec(self):
        blk = self.blk
        return pl.BlockSpec((self.rows, self.arr.shape[1]), lambda i: (blk, 0))


class _LayerRow:
    def __init__(self, arr, l):
        self.arr, self.l = arr, l

    def spec(self):
        l = self.l
        return pl.BlockSpec((1, 1, self.arr.shape[2]), lambda i: (l, 0, 0))


def _mm(a, b, *, name, ta=False, tb=False, out_dtype=F32, add=None, tm=T_MM, tn=T_MM, tk=1024, exs=None):
    M, K = (a.shape[1], a.shape[0]) if ta else a.shape
    N = b.shape[0] if tb else b.shape[1]
    tm, tn, tk = min(tm, M), min(tn, N), min(tk, K)
    assert M % tm == 0 and N % tn == 0 and K % tk == 0, (M, N, K)
    nk = K // tk
    dn = (((0 if ta else 1,), (1 if tb else 0,)), ((), ()))

    def body(*refs):
        if add is not None:
            a_ref, b_ref, c_ref, o_ref = refs[:4]
        else:
            a_ref, b_ref, o_ref = refs[:3]
        part = lax.dot_general(a_ref[...].astype(BF), b_ref[...].astype(BF), dn, preferred_element_type=F32)
        if nk == 1:
            if add is not None:
                part = part + c_ref[...]
            o_ref[...] = part.astype(out_dtype)
            return
        acc = refs[-1]
        k = pl.program_id(2)

        @pl.when(k == 0)
        def _():
            acc[...] = part

        @pl.when(k != 0)
        def _():
            acc[...] += part

        @pl.when(k == nk - 1)
        def _():
            r = acc[...]
            if add is not None:
                r = r + c_ref[...]
            o_ref[...] = r.astype(out_dtype)

    a_spec = pl.BlockSpec((tk, tm), lambda i, j, k: (k, i)) if ta else pl.BlockSpec((tm, tk), lambda i, j, k: (i, k))
    b_spec = pl.BlockSpec((tn, tk), lambda i, j, k: (j, k)) if tb else pl.BlockSpec((tk, tn), lambda i, j, k: (k, j))
    o_spec = pl.BlockSpec((tm, tn), lambda i, j, k: (i, j))
    in_specs = [a_spec, b_spec] + ([o_spec] if add is not None else [])
    args = (a, b) + ((add,) if add is not None else ())
    (out,), got = _carry_call(
        body, name, (M // tm, N // tn, nk), in_specs, [o_spec], [jax.ShapeDtypeStruct((M, N), out_dtype)],
        [pltpu.VMEM((tm, tn), F32)] if nk > 1 else [], ("parallel", "parallel", "arbitrary"), args, exs)
    return out if exs is None else (out, got)


def _cpow(ar, ai, n):
    rr, ri = None, None
    br, bi = ar, ai
    while n:
        if n & 1:
            if rr is None:
                rr, ri = br, bi
            else:
                rr, ri = rr * br - ri * bi, rr * bi + ri * br
        n >>= 1
        if n:
            br, bi = br * br - bi * bi, 2.0 * br * bi
    return rr, ri


def _seg_shift(v, k, reverse):
    sub = lax.broadcasted_iota(jnp.int32, v.shape, 0)
    if not reverse:
        return jnp.where(sub >= k, pltpu.roll(v, k, 0), 0.0)
    return jnp.where(sub < SUBLANES - k, pltpu.roll(v, SUBLANES - k, 0), 0.0)


def _steps(n, step, init, unroll):
    u = unroll if n % unroll == 0 else 1

    def trip(i, c):
        for s in range(u):
            c = step(i * u + s, c)
        return c

    return lax.fori_loop(0, n // u, trip, init)


def _ssm_scan(hre, him, ar, ai, seglen, reverse, tail=None, tail_init=()):
    w = hre.shape[1]
    zero = jnp.zeros((SUBLANES, w), F32)

    def rows(j):
        jj = (seglen - 1 - j) if reverse else j
        return pl.ds(pl.multiple_of(jj * SUBLANES, SUBLANES), SUBLANES)

    def local(j, c):
        hr, hi = c
        r = rows(j)
        nhr = ar * hr - ai * hi + hre[r, :]
        nhi = ar * hi + ai * hr + him[r, :]
        hre[r, :] = nhr
        him[r, :] = nhi
        return nhr, nhi

    er, ei = _steps(seglen, local, (zero, zero), 8 if reverse else 4)
    pr, pi_ = _cpow(ar, ai, seglen)
    for k in (1, 2, 4):
        sr, si = _seg_shift(er, k, reverse), _seg_shift(ei, k, reverse)
        er, ei = er + pr * sr - pi_ * si, ei + pr * si + pi_ * sr
        pr, pi_ = pr * pr - pi_ * pi_, 2.0 * pr * pi_
    cr, ci = _seg_shift(er, 1, reverse), _seg_shift(ei, 1, reverse)

    def carry_in(j, c):
        tr, ti = c[0] * ar - c[1] * ai, c[0] * ai + c[1] * ar
        r = rows(j)
        fr = hre[r, :] + tr
        fi = him[r, :] + ti
        hre[r, :] = fr
        him[r, :] = fi
        if tail is None:
            return tr, ti
        return (tr, ti) + tuple(tail(j, fr, fi, c[2:]))

    out = _steps(seglen, carry_in, (cr, ci) + tuple(tail_init), 8 if reverse else 4)
    return out[2:]


SSM_CB = 128
SSM_BS = SSM_CB // SSM_GROUP * SSM_STATE
SSM_SB = 512
SSM_SB_F = 512


def _ssm_specs(S, l, sb=SSM_SB):
    u_spec = pl.BlockSpec((S, SSM_CB), lambda g, h: (0, g))
    bb_spec = pl.BlockSpec((1, 1, SSM_CB, sb), lambda g, h: (l, g, 0, h))
    a_spec = pl.BlockSpec((1, 1, 1, sb), lambda g, h: (l, g, 0, h))
    c_spec = pl.BlockSpec((1, 1, sb, SSM_CB), lambda g, h: (l, g, h, 0))
    d_spec = pl.BlockSpec((1, 1, SSM_CB), lambda g, h: (l, 0, g))
    return u_spec, bb_spec, a_spec, c_spec, d_spec


def _ssm_fwd(u, mats, l):
    S = u.shape[0]
    seglen = S // SUBLANES
    ch = min(512, S)
    nch = S // ch

    def body(u_ref, bbre_ref, bbim_ref, are_ref, aim_ref, cre_ref, cim_ref, d_ref, y_ref, hre, him):
        hf = pl.program_id(1)
        wre = bbre_ref[0, 0].astype(BF)
        wim = bbim_ref[0, 0].astype(BF)

        def mk(c, _):
            r = pl.ds(pl.multiple_of(c * ch, ch), ch)
            ub = u_ref[r, :].astype(BF)
            hre[r, :] = _dot(ub, wre)
            him[r, :] = _dot(ub, wim)
            return 0

        lax.fori_loop(0, nch, mk, 0)
        ar = jnp.broadcast_to(are_ref[0, 0], (SUBLANES, SSM_SB_F))
        ai = jnp.broadcast_to(aim_ref[0, 0], (SUBLANES, SSM_SB_F))
        _ssm_scan(hre, him, ar, ai, seglen, False)
        cr = cre_ref[0, 0].astype(BF)
        ci = cim_ref[0, 0].astype(BF)

        def out(c, _):
            r = pl.ds(pl.multiple_of(c * ch, ch), ch)
            y = _dot(hre[r, :].astype(BF), cr) - _dot(him[r, :].astype(BF), ci)

            @pl.when(hf == 0)
            def _():
                y_ref[r, :] = y + d_ref[0] * u_ref[r, :].astype(F32)

            @pl.when(hf != 0)
            def _():
                y_ref[r, :] = y_ref[r, :] + y

            return 0

        lax.fori_loop(0, nch, out, 0)

    u_spec, bb_spec, a_spec, c_spec, d_spec = _ssm_specs(S, l, SSM_SB_F)
    return pl.pallas_call(
        body, name="ssm_fwd", grid=(SSM_WIDTH // SSM_CB, SSM_BS // SSM_SB_F),
        in_specs=[u_spec, bb_spec, bb_spec, a_spec, a_spec, c_spec, c_spec, d_spec], out_specs=u_spec,
        out_shape=jax.ShapeDtypeStruct((S, SSM_WIDTH), F32),
        scratch_shapes=[pltpu.VMEM((S, SSM_SB_F), F32), pltpu.VMEM((S, SSM_SB_F), F32)],
        compiler_params=_cparams(("parallel", "arbitrary")),
    )(u, *mats)


def _ssm_bwd(u, dy, mats, l, exs=()):
    S = u.shape[0]
    seglen = S // SUBLANES
    ch = min(512, S)
    nch = S // ch
    nblk = SSM_WIDTH // SSM_CB

    def body(u_ref, dy_ref, bbre_ref, bbim_ref, are_ref, aim_ref, cre_ref, cim_ref, d_ref,
             du_ref, dbbre_ref, dbbim_ref, dare_ref, daim_ref, dcre_ref, dcim_ref, dd_ref,
             hre, him, lre, lim):
        hf = pl.program_id(1)
        wre = bbre_ref[0, 0].astype(BF)
        wim = bbim_ref[0, 0].astype(BF)
        wre_t, wim_t = wre.T, wim.T
        cr_t = cre_ref[0, 0].astype(BF).T
        ci_t = cim_ref[0, 0].astype(BF).T

        def mk(c, _):
            r = pl.ds(pl.multiple_of(c * ch, ch), ch)
            ub = u_ref[r, :].astype(BF)
            hre[r, :] = _dot(ub, wre)
            him[r, :] = _dot(ub, wim)
            return 0

        lax.fori_loop(0, nch, mk, 0)
        ar = jnp.broadcast_to(are_ref[0, 0], (SUBLANES, SSM_SB))
        ai = jnp.broadcast_to(aim_ref[0, 0], (SUBLANES, SSM_SB))
        _ssm_scan(hre, him, ar, ai, seglen, False)

        dcre_ref[...] = jnp.zeros_like(dcre_ref)
        dcim_ref[...] = jnp.zeros_like(dcim_ref)

        @pl.when(hf == 0)
        def _():
            dd_ref[...] = jnp.zeros_like(dd_ref)

        def cot(c, _):
            r = pl.ds(pl.multiple_of(c * ch, ch), ch)
            dyv = dy_ref[r, :]
            dyb = dyv.astype(BF)
            lre[r, :] = _dot(dyb, cr_t)
            lim[r, :] = -_dot(dyb, ci_t)
            dcre_ref[0] = dcre_ref[0] + _dot_tn(dyb, hre[r, :].astype(BF))
            dcim_ref[0] = dcim_ref[0] - _dot_tn(dyb, him[r, :].astype(BF))

            @pl.when(hf == 0)
            def _():
                dd_ref[...] = dd_ref[...] + jnp.sum(dyv * u_ref[r, :].astype(F32), axis=0, keepdims=True)

            return 0

        lax.fori_loop(0, nch, cot, 0)

        last = pl.ds((seglen - 1) * SUBLANES, SUBLANES)
        pr0 = _seg_shift(hre[last, :], 1, False)
        pi0 = _seg_shift(him[last, :], 1, False)

        def da(j, lr, li, c):
            acr, aci = c
            jp = jnp.maximum(seglen - 2 - j, 0)
            rp = pl.ds(pl.multiple_of(jp * SUBLANES, SUBLANES), SUBLANES)
            inner = j < seglen - 1
            pr = jnp.where(inner, hre[rp, :], pr0)
            pi_ = jnp.where(inner, him[rp, :], pi0)
            return acr + lr * pr + li * pi_, aci + li * pr - lr * pi_

        zero = jnp.zeros((SUBLANES, SSM_SB), F32)
        acr, aci = _ssm_scan(lre, lim, ar, -ai, seglen, True, tail=da, tail_init=(zero, zero))
        dare_ref[0] = jnp.sum(acr, axis=0, keepdims=True)
        daim_ref[0] = jnp.sum(aci, axis=0, keepdims=True)

        dbbre_ref[...] = jnp.zeros_like(dbbre_ref)
        dbbim_ref[...] = jnp.zeros_like(dbbim_ref)

        def fin(c, _):
            r = pl.ds(pl.multiple_of(c * ch, ch), ch)
            lrb = lre[r, :].astype(BF)
            lib = lim[r, :].astype(BF)
            ub = u_ref[r, :].astype(BF)
            du = _dot(lrb, wre_t) + _dot(lib, wim_t)
            dbbre_ref[0] = dbbre_ref[0] + _dot_tn(ub, lrb)
            dbbim_ref[0] = dbbim_ref[0] + _dot_tn(ub, lib)

            @pl.when(hf == 0)
            def _():
                du_ref[r, :] = du + d_ref[0] * dy_ref[r, :]

            @pl.when(hf != 0)
            def _():
                du_ref[r, :] = du_ref[r, :] + du

            return 0

        lax.fori_loop(0, nch, fin, 0)

    u_spec, bb_spec, a_spec, c_spec, d_spec = _ssm_specs(S, l)
    dbb_spec = pl.BlockSpec((1, SSM_CB, SSM_SB), lambda g, h: (g, 0, h))
    da_spec = pl.BlockSpec((1, 1, SSM_SB), lambda g, h: (g, 0, h))
    dd_spec = pl.BlockSpec((1, SSM_CB), lambda g, h: (0, g))
    out_shape = (
        jax.ShapeDtypeStruct((S, SSM_WIDTH), F32),
        jax.ShapeDtypeStruct((nblk, SSM_CB, SSM_BS), F32), jax.ShapeDtypeStruct((nblk, SSM_CB, SSM_BS), F32),
        jax.ShapeDtypeStruct((nblk, 1, SSM_BS), F32), jax.ShapeDtypeStruct((nblk, 1, SSM_BS), F32),
        jax.ShapeDtypeStruct((nblk, SSM_CB, SSM_BS), F32), jax.ShapeDtypeStruct((nblk, SSM_CB, SSM_BS), F32),
        jax.ShapeDtypeStruct((1, SSM_WIDTH), F32),
    )
    return _carry_call(
        body, "ssm_bwd", (nblk, SSM_BS // SSM_SB), [u_spec, u_spec, bb_spec, bb_spec, a_spec, a_spec, c_spec, c_spec, d_spec],
        (u_spec, dbb_spec, dbb_spec, da_spec, da_spec, dbb_spec, dbb_spec, dd_spec), out_shape,
        [pltpu.VMEM((S, SSM_SB), F32) for _ in range(4)], ("parallel", "arbitrary"), (u, dy) + tuple(mats), exs)


def _ssm_post_fwd(y_raw, proj, w_glu, b_glu, p_ssm):
    S = y_raw.shape[0]
    tr = min(T_ROWS_WIDE, S)

    def body(y_ref, z_ref, wg_ref, bg_ref, p_ref, o_ref):
        g = _gelu(y_ref[...])
        t = _dot(g.astype(BF), wg_ref[...]) + bg_ref[0]
        glu = t[:, :SSM_WIDTH] * _sigmoid(t[:, SSM_WIDTH:])
        ys = glu * _silu(z_ref[...].astype(F32))
        o_ref[...] = _dot(ys.astype(BF), p_ref[...]).astype(o_ref.dtype)

    return pl.pallas_call(
        body, name="ssm_post_fwd", grid=(S // tr,),
        in_specs=[_rows(tr, 512), _rows(tr, 512, 1), w_glu.spec(), b_glu.spec(), p_ssm.spec()],
        out_specs=_rows(tr, 1024), out_shape=jax.ShapeDtypeStruct((S, D_MODEL), ACT),
        compiler_params=_cparams(("parallel",)),
    )(y_raw, proj, w_glu.arr, b_glu.arr, p_ssm.arr)


def _ssm_post_bwd(do, y_raw, proj, w_glu, b_glu, p_ssm):
    S = y_raw.shape[0]
    tr = min(T_ROWS_BWD_WIDE, S)

    def body(do_ref, y_ref, z_ref, wg_ref, bg_ref, p_ref, dy_ref, dz_ref, dwg_ref, dbg_ref, dp_ref):
        @pl.when(pl.program_id(0) == 0)
        def _():
            dwg_ref[...] = jnp.zeros_like(dwg_ref)
            dbg_ref[...] = jnp.zeros_like(dbg_ref)
            dp_ref[...] = jnp.zeros_like(dp_ref)

        y = y_ref[...]
        z = z_ref[...].astype(F32)
        g = _gelu(y)
        gb = g.astype(BF)
        t = _dot(gb, wg_ref[...]) + bg_ref[0]
        a = t[:, :SSM_WIDTH]
        sb = _sigmoid(t[:, SSM_WIDTH:])
        glu = a * sb
        ys = glu * _silu(z)
        dob = do_ref[...].astype(BF)
        dys = _dot_nt(dob, p_ref[...])
        dp_ref[...] += _dot_tn(ys.astype(BF), dob)
        dglu = dys * _silu(z)
        dz_ref[...] = (dys * glu * _dsilu(z)).astype(dz_ref.dtype)
        dt = jnp.concatenate([dglu * sb, dglu * a * sb * (1.0 - sb)], axis=1)
        dbg_ref[...] += jnp.sum(dt, axis=0, keepdims=True)
        dtb = dt.astype(BF)
        dg = _dot_nt(dtb, wg_ref[...])
        dwg_ref[...] += _dot_tn(gb, dtb)
        dy_ref[...] = dg * _dgelu(y)

    return pl.pallas_call(
        body, name="ssm_post_bwd", grid=(S // tr,),
        in_specs=[_rows(tr, 1024), _rows(tr, 512), _rows(tr, 512, 1), w_glu.spec(), b_glu.spec(), p_ssm.spec()],
        out_specs=(_rows(tr, 512), _rows(tr, 512), _full((512, 1024)), _full((1, 1024)), _full((512, 1024))),
        out_shape=(jax.ShapeDtypeStruct((S, 512), F32), jax.ShapeDtypeStruct((S, 512), BF),
                   jax.ShapeDtypeStruct((512, 1024), F32), jax.ShapeDtypeStruct((1, 1024), F32),
                   jax.ShapeDtypeStruct((512, 1024), F32)),
        compiler_params=_cparams(("arbitrary",)),
    )(do, y_raw, proj, w_glu.arr, b_glu.arr, p_ssm.arr)


def _rope(t, c, sa, sb):
    return t * c + pltpu.roll(t, LANES - 16, 1) * sa + pltpu.roll(t, 16, 1) * sb


def _rope_t(dy, c, sa, sb):
    return dy * c + pltpu.roll(dy * sa, 16, 1) + pltpu.roll(dy * sb, LANES - 16, 1)


def _rms(x, g):
    r = lax.rsqrt(jnp.mean(x * x, axis=-1, keepdims=True) + NORM_EPS)
    return x * r * g, r


def _mla_pre_fwd(proj, q_norm, kv_norm, wuq, wk, wv, tc, tsa, tsb):
    S = proj.shape[0]
    tr = min(T_ROWS_WIDE, S)

    def body(cq_ref, ckv_ref, slot_ref, qn_ref, kn_ref, wuq_ref, wk_ref, wv_ref, c_ref, sa_ref, sb_ref,
             q_out, k_out, v_out, qt_out, kt_out, vt_out):
        c, sa, sb = c_ref[...], sa_ref[...], sb_ref[...]
        qn, _ = _rms(cq_ref[...].astype(F32), qn_ref[0])
        q = _dot(qn.astype(BF), wuq_ref[...]) * MLA_SCALE
        kn, _ = _rms(ckv_ref[...].astype(F32), kn_ref[0])
        knb = kn.astype(BF)
        kp = _dot(knb, wk_ref[...])
        v = _dot(knb, wv_ref[...]).astype(BF)
        v_out[...] = v
        vt_out[...] = v.T
        kr = _rope(slot_ref[...].astype(F32), c, sa, sb)
        for h in range(MLA_HEADS):
            cs = slice(h * LANES, (h + 1) * LANES)
            qh = _rope(q[:, cs], c, sa, sb).astype(BF)
            kh = (kp[:, cs] + kr).astype(BF)
            q_out[:, cs] = qh
            k_out[:, cs] = kh
            qt_out[cs, :] = qh.T
            kt_out[cs, :] = kh.T

    return pl.pallas_call(
        body, name="mla_pre_fwd", grid=(S // tr,),
        in_specs=[_rows(tr, 256, 4), _rows(tr, 128, 10), _rows(tr, 128, 11), q_norm.spec(), kv_norm.spec(),
                  wuq.spec(), _full((128, 1024)), _full((128, 512)),
                  _rows(tr, 128), _rows(tr, 128), _rows(tr, 128)],
        out_specs=(_rows(tr, 1024), _rows(tr, 1024), _rows(tr, 512), _cols(1024, tr), _cols(1024, tr), _cols(512, tr)),
        out_shape=(jax.ShapeDtypeStruct((S, 1024), BF), jax.ShapeDtypeStruct((S, 1024), BF),
                   jax.ShapeDtypeStruct((S, 512), BF), jax.ShapeDtypeStruct((1024, S), BF),
                   jax.ShapeDtypeStruct((1024, S), BF), jax.ShapeDtypeStruct((512, S), BF)),
        compiler_params=_cparams(("parallel",)),
    )(proj, proj, proj, q_norm.arr, kv_norm.arr, wuq.arr, wk, wv, tc, tsa, tsb)


def _mla_pre_bwd(dq, dk, dv, proj, q_norm, kv_norm, wuq, wk, wv, tc, tsa, tsb):
    S = proj.shape[0]
    tr = min(T_ROWS_BWD, S)

    def body(dq_ref, dk_ref, dv_ref, cq_ref, ckv_ref, qn_ref, kn_ref, wuq_ref, wk_ref, wv_ref, c_ref, sa_ref, sb_ref,
             dcq_ref, dckv_ref, dslot_ref, dwuq_ref, dwk_ref, dwv_ref, dqn_ref, dkn_ref, dqp):
        @pl.when(pl.program_id(0) == 0)
        def _():
            dwuq_ref[...] = jnp.zeros_like(dwuq_ref)
            dwk_ref[...] = jnp.zeros_like(dwk_ref)
            dwv_ref[...] = jnp.zeros_like(dwv_ref)
            dqn_ref[...] = jnp.zeros_like(dqn_ref)
            dkn_ref[...] = jnp.zeros_like(dkn_ref)

        c, sa, sb = c_ref[...], sa_ref[...], sb_ref[...]
        dkr = jnp.zeros((tr, LANES), F32)
        for h in range(MLA_HEADS):
            cs = slice(h * LANES, (h + 1) * LANES)
            dqp[:, cs] = (_rope_t(dq_ref[:, cs], c, sa, sb) * MLA_SCALE).astype(BF)
            dkr = dkr + dk_ref[:, cs]
        lane = lax.broadcasted_iota(jnp.int32, (tr, LANES), 1)
        in_rope = (lane >= MLA_NOPE) & (lane < MLA_NOPE + MLA_ROPE)
        dslot_ref[...] = jnp.where(in_rope, _rope_t(dkr, c, sa, sb), 0.0).astype(dslot_ref.dtype)

        cq = cq_ref[...].astype(F32)
        gq = qn_ref[0]
        qn, rq = _rms(cq, gq)
        dqpb = dqp[...]
        dwuq_ref[...] += _dot_tn(qn.astype(BF), dqpb)
        dqn = _dot_nt(dqpb, wuq_ref[...])
        dqn_ref[...] += jnp.sum(dqn * cq * rq, axis=0, keepdims=True)
        dyg = dqn * gq
        dcq_ref[...] = (rq * dyg - cq * (rq * rq * rq) * jnp.mean(dyg * cq, axis=-1, keepdims=True)).astype(dcq_ref.dtype)

        ckv = ckv_ref[...].astype(F32)
        gk = kn_ref[0]
        kn, rk = _rms(ckv, gk)
        knb = kn.astype(BF)
        dkb = dk_ref[...].astype(BF)
        dvb = dv_ref[...].astype(BF)
        dwk_ref[...] += _dot_tn(knb, dkb)
        dwv_ref[...] += _dot_tn(knb, dvb)
        dkn = _dot_nt(dkb, wk_ref[...]) + _dot_nt(dvb, wv_ref[...])
        dkn_ref[...] += jnp.sum(dkn * ckv * rk, axis=0, keepdims=True)
        dyk = dkn * gk
        dckv_ref[...] = (rk * dyk - ckv * (rk * rk * rk) * jnp.mean(dyk * ckv, axis=-1, keepdims=True)).astype(dckv_ref.dtype)

    return pl.pallas_call(
        body, name="mla_pre_bwd", grid=(S // tr,),
        in_specs=[_rows(tr, 1024), _rows(tr, 1024), _rows(tr, 512), _rows(tr, 256, 4), _rows(tr, 128, 10),
                  q_norm.spec(), kv_norm.spec(), wuq.spec(), _full((128, 1024)), _full((128, 512)),
                  _rows(tr, 128), _rows(tr, 128), _rows(tr, 128)],
        out_specs=(_rows(tr, 256), _rows(tr, 128), _rows(tr, 128), _full((256, 1024)), _full((128, 1024)),
                   _full((128, 512)), _full((1, 256)), _full((1, 128))),
        out_shape=(jax.ShapeDtypeStruct((S, 256), BF), jax.ShapeDtypeStruct((S, 128), BF),
                   jax.ShapeDtypeStruct((S, 128), BF), jax.ShapeDtypeStruct((256, 1024), F32),
                   jax.ShapeDtypeStruct((128, 1024), F32), jax.ShapeDtypeStruct((128, 512), F32),
                   jax.ShapeDtypeStruct((1, 256), F32), jax.ShapeDtypeStruct((1, 128), F32)),
        scratch_shapes=[pltpu.VMEM((tr, 1024), BF)],
        compiler_params=_cparams(("arbitrary",)),
    )(dq, dk, dv, proj, proj, q_norm.arr, kv_norm.arr, wuq.arr, wk, wv, tc, tsa, tsb)


ANY = pl.BlockSpec(memory_space=pl.ANY)
N_REL = N_DEV - 1


def _coords():
    return lax.axis_index("x"), lax.axis_index("y"), lax.axis_index("c")


def _sem_shapes(nbuf):
    return [pltpu.SemaphoreType.DMA((N_REL * nbuf,)), pltpu.SemaphoreType.DMA((N_REL * nbuf,)),
            pltpu.SemaphoreType.DMA((nbuf,))]


def _ag_plan(srcs, dsts, sems):
    send_sems, recv_sems, _ = sems
    plan = []
    for b, (src, dst) in enumerate(zip(srcs, dsts)):
        def slot(px, py, pc, dst=dst):
            return dst.at[4 * px + 2 * py + pc]

        def copy(k, blk, to, s=None, b=b, slot=slot):
            return pltpu.make_async_remote_copy(
                src_ref=slot(*blk) if s is None else s, dst_ref=slot(*blk), send_sem=send_sems.at[N_REL * b + k],
                recv_sem=recv_sems.at[N_REL * b + k], device_id=to, device_id_type=MESH)

        plan.append((b, src, slot, copy))
    return plan


def _ag_start(srcs, dsts, sems):
    x, y, c = _coords()
    chips = [(1 - x, y), (x, 1 - y), (1 - x, 1 - y)]
    for b, src, slot, copy in _ag_plan(srcs, dsts, sems):
        pltpu.make_async_copy(src, slot(x, y, c), sems[2].at[b]).start()
        copy(0, (x, y, c), (x, y, 1 - c), src).start()
        for j, chip in enumerate(chips):
            copy(1 + j, (x, y, c), (*chip, c), src).start()


def _ag_relay(srcs, dsts, sems):
    x, y, c = _coords()
    me, sibling = (x, y, c), (x, y, 1 - c)
    chips = [(1 - x, y), (x, 1 - y), (1 - x, 1 - y)]
    for b, src, slot, copy in _ag_plan(srcs, dsts, sems):
        for j, chip in enumerate(chips):
            copy(1 + j, (*chip, c), me).wait_recv()
            copy(4 + j, (*chip, c), sibling).start()


def _ag_finish(srcs, dsts, sems, relayed):
    x, y, c = _coords()
    me, sibling = (x, y, c), (x, y, 1 - c)
    chips = [(1 - x, y), (x, 1 - y), (1 - x, 1 - y)]
    if not relayed:
        _ag_relay(srcs, dsts, sems)
    plan = _ag_plan(srcs, dsts, sems)
    for b, src, slot, copy in plan:
        copy(0, sibling, me).wait_recv()
        for j, chip in enumerate(chips):
            copy(4 + j, (*chip, 1 - c), me).wait_recv()
        copy(0, me, sibling, src).wait_send()
        for j, chip in enumerate(chips):
            copy(1 + j, me, (*chip, c), src).wait_send()
            copy(4 + j, (*chip, c), sibling).wait_send()
        pltpu.make_async_copy(src, slot(*me), sems[2].at[b]).wait()


def _a2a_copies(srcs, dsts, sems):
    send_sems, recv_sems, local_sems = sems
    x, y, c = _coords()
    me = 4 * x + 2 * y + c
    local, remote = [], []
    for b, (src, dst) in enumerate(zip(srcs, dsts)):
        for rel in range(1, N_DEV):
            px = 1 - x if rel & 4 else x
            py = 1 - y if rel & 2 else y
            pc = 1 - c if rel & 1 else c
            remote.append(pltpu.make_async_remote_copy(
                src_ref=src.at[4 * px + 2 * py + pc], dst_ref=dst.at[me], send_sem=send_sems.at[N_REL * b + rel - 1],
                recv_sem=recv_sems.at[N_REL * b + rel - 1], device_id=(px, py, pc), device_id_type=MESH))
        local.append(pltpu.make_async_copy(src.at[me], dst.at[me], local_sems.at[b]))
    return local, remote


def _a2a_start(srcs, dsts, sems):
    local, remote = _a2a_copies(srcs, dsts, sems)
    for d in local + remote:
        d.start()


def _a2a_finish(srcs, dsts, sems):
    local, remote = _a2a_copies(srcs, dsts, sems)
    for d in remote + local:
        d.wait()


class _Exchange:
    def __init__(self, kind, srcs):
        self.kind, self.srcs = kind, list(srcs)
        self.n = len(self.srcs)

    def out_shapes(self):
        if self.kind == "ag":
            return [jax.ShapeDtypeStruct((N_DEV,) + s.shape, s.dtype) for s in self.srcs]
        return [jax.ShapeDtypeStruct(s.shape, s.dtype) for s in self.srcs]

    def start(self, src_refs, dst_refs, sems):
        (_ag_start if self.kind == "ag" else _a2a_start)(src_refs, dst_refs, sems)

    def relay(self, src_refs, dst_refs, sems):
        if self.kind == "ag":
            _ag_relay(src_refs, dst_refs, sems)

    def finish(self, src_refs, dst_refs, sems, relayed=False):
        if self.kind == "ag":
            _ag_finish(src_refs, dst_refs, sems, relayed)
        else:
            _a2a_finish(src_refs, dst_refs, sems)


def _carry_call(body, name, grid, in_specs, out_specs, out_shape, scratch, semantics, args, exs):
    in_specs, out_specs, out_shape, scratch = list(in_specs), list(out_specs), list(out_shape), list(scratch)
    if not exs:
        return pl.pallas_call(body, name=name, grid=grid, in_specs=in_specs, out_specs=out_specs, out_shape=out_shape,
                              scratch_shapes=scratch, compiler_params=_cparams(semantics))(*args), []
    n_in, n_out, n_scr = len(in_specs), len(out_specs), len(scratch)
    n_ex = sum(e.n for e in exs)

    def wrapped(*refs):
        ins, refs = refs[:n_in], refs[n_in:]
        srcs, refs = refs[:n_ex], refs[n_ex:]
        outs, refs = refs[:n_out], refs[n_out:]
        dsts, refs = refs[:n_ex], refs[n_ex:]
        scr, sems = refs[:n_scr], refs[n_scr:]
        views, off = [], 0
        for i, e in enumerate(exs):
            views.append((srcs[off:off + e.n], dsts[off:off + e.n], sems[3 * i:3 * i + 3]))
            off += e.n
        first = last = late = None
        for axis, size in enumerate(grid):
            at0, at1 = pl.program_id(axis) == 0, pl.program_id(axis) == size - 1
            first = at0 if first is None else first & at0
            last = at1 if last is None else last & at1
            late = at1 if late is None else late & at0
        relay_early = grid[0] > 1 and all(e.kind == "ag" for e in exs)

        @pl.when(first)
        def _():
            for e, view in zip(exs, views):
                e.start(*view)

        if relay_early:
            @pl.when(late)
            def _():
                for e, view in zip(exs, views):
                    e.relay(*view)

        body(*ins, *outs, *scr)

        @pl.when(last)
        def _():
            for e, view in zip(exs, views):
                e.finish(*view, relayed=relay_early)

    res = pl.pallas_call(
        wrapped, name=name + "_x", grid=grid, in_specs=in_specs + [ANY] * n_ex, out_specs=out_specs + [ANY] * n_ex,
        out_shape=out_shape + [s for e in exs for s in e.out_shapes()],
        scratch_shapes=scratch + [s for e in exs for s in _sem_shapes(e.n)],
        compiler_params=_cparams(("arbitrary",) * len(grid)))(*args, *[s for e in exs for s in e.srcs])
    got, off = [], n_out
    for e in exs:
        got.append(list(res[off:off + e.n]))
        off += e.n
    return res[:n_out], got


def _exchange_call(name, exs):
    tot = sum(e.n for e in exs)

    def body(*refs):
        srcs, dsts, sems = refs[:tot], refs[tot:2 * tot], refs[2 * tot:]
        views, off = [], 0
        for i, e in enumerate(exs):
            views.append((srcs[off:off + e.n], dsts[off:off + e.n], sems[3 * i:3 * i + 3]))
            off += e.n
        for e, view in zip(exs, views):
            e.start(*view)
        for e, view in zip(exs, views):
            e.finish(*view)

    outs = pl.pallas_call(
        body, name=name, in_specs=[ANY] * tot, out_specs=[ANY] * tot,
        out_shape=[s for e in exs for s in e.out_shapes()],
        scratch_shapes=[s for e in exs for s in _sem_shapes(e.n)],
    )(*[s for e in exs for s in e.srcs])
    res, off = [], 0
    for e in exs:
        res.append(list(outs[off:off + e.n]))
        off += e.n
    return res


def _flash_call(body, name, exs, in_specs, out_specs, out_shape, scratch, n, args):
    return _carry_call(body, name, (MLA_HEADS // 2, n * (n + 1) // 2), in_specs, out_specs, out_shape, scratch,
                       ("parallel", "arbitrary"), args, exs)


def _tri_rows(s, n):
    at = [(s >= r * (r + 1) // 2).astype(jnp.int32) for r in range(1, n)]
    return sum(at), s - sum(a * r for a, r in zip(at, range(1, n)))


def _tri_cols(s, n):
    starts = [c * n - c * (c - 1) // 2 for c in range(n)]
    col = sum((s >= starts[c]).astype(jnp.int32) for c in range(1, n))
    start = sum(jnp.where(col == c, starts[c], 0) for c in range(n))
    return s - start + col, col


def _pair_rows(a):
    at = a.T
    return jnp.concatenate([at[0:1, :], at[MLA_V:MLA_V + 1, :], jnp.zeros((SUBLANES - 2, a.shape[0]), a.dtype)], axis=0)


def _lower_tri(t):
    return lax.broadcasted_iota(jnp.int32, (t, t), 0) >= lax.broadcasted_iota(jnp.int32, (t, t), 1)


def _upper_tri(t):
    return lax.broadcasted_iota(jnp.int32, (t, t), 1) >= lax.broadcasted_iota(jnp.int32, (t, t), 0)


def _flash_fwd(q, kt, v, exs=()):
    S = q.shape[0]
    t = min(T_ATT, S)
    n = S // t

    def body(q_ref, kt_ref, v_ref, o_ref, lse_ref, lse_t_ref, m_s, l_s, acc):
        qi, ki = _tri_rows(pl.program_id(1), n)
        lo = lax.broadcasted_iota(jnp.int32, (t, LANES), 1) < MLA_V

        @pl.when(ki == 0)
        def _():
            m_s[...] = jnp.full_like(m_s, NEG)
            l_s[...] = jnp.zeros_like(l_s)
            acc[...] = jnp.zeros_like(acc)

        keep = _lower_tri(t) | (ki < qi)
        vv = v_ref[...]
        heads = range(2)
        ss = [jnp.where(keep, _dot(q_ref[:, h * LANES:(h + 1) * LANES], kt_ref[h * LANES:(h + 1) * LANES, :]), NEG)
              for h in heads]
        m_prev = [m_s[h] for h in heads]
        l_prev = [l_s[h] for h in heads]
        m_new = [jnp.maximum(m_prev[h], jnp.max(ss[h], axis=1, keepdims=True)) for h in heads]
        al = [jnp.exp(m_prev[h] - m_new[h]) for h in heads]
        ps = [jnp.exp(ss[h] - m_new[h][:, :1]) for h in heads]
        l_new = [al[h] * l_prev[h] + jnp.sum(ps[h], axis=1, keepdims=True) for h in heads]
        pv = [_dot(ps[h].astype(BF), vv) for h in heads]
        for h in heads:
            m_s[h] = m_new[h]
            l_s[h] = l_new[h]
        acc[...] = jnp.where(lo, al[0], al[1]) * acc[...] + jnp.where(lo, pv[0], pv[1])

        @pl.when(ki == qi)
        def _():
            o_ref[...] = acc[...] / jnp.where(lo, l_s[0], l_s[1])
            lse = jnp.where(lo, m_s[0] + jnp.log(l_s[0]), m_s[1] + jnp.log(l_s[1]))
            lse_ref[0] = lse
            lse_t_ref[0] = _pair_rows(lse)

    return _flash_call(
        body, "mla_flash_fwd", exs,
        [pl.BlockSpec((t, 256), lambda p, s: (_tri_rows(s, n)[0], p)),
         pl.BlockSpec((256, t), lambda p, s: (p, _tri_rows(s, n)[1])),
         pl.BlockSpec((t, 128), lambda p, s: (_tri_rows(s, n)[1], p))],
        [pl.BlockSpec((t, 128), lambda p, s: (_tri_rows(s, n)[0], p)),
         pl.BlockSpec((1, t, 128), lambda p, s: (p, _tri_rows(s, n)[0], 0)),
         pl.BlockSpec((1, SUBLANES, t), lambda p, s: (p, 0, _tri_rows(s, n)[0]))],
        [jax.ShapeDtypeStruct((S, 512), F32), jax.ShapeDtypeStruct((MLA_HEADS // 2, S, 128), F32),
         jax.ShapeDtypeStruct((MLA_HEADS // 2, SUBLANES, S), F32)],
        [pltpu.VMEM((2, t, 128), F32), pltpu.VMEM((2, t, 128), F32), pltpu.VMEM((t, 128), F32)], n, (q, kt, v))


def _flash_bwd_dq(q, k, kt, vt, do, lse, delta, exs=()):
    S = q.shape[0]
    t = min(T_ATT, S)
    n = S // t

    def body(q_ref, k_ref, kt_ref, vt_ref, do_ref, lse_ref, dl_ref, dq_ref, acc):
        qi, ki = _tri_rows(pl.program_id(1), n)
        lo = lax.broadcasted_iota(jnp.int32, (t, LANES), 1) < MLA_V

        @pl.when(ki == 0)
        def _():
            acc[...] = jnp.zeros_like(acc)

        keep = _lower_tri(t) | (ki < qi)
        heads = range(2)
        cs = [slice(h * LANES, (h + 1) * LANES) for h in heads]
        col = [slice(h * MLA_V, h * MLA_V + 1) for h in heads]
        lse, dl, dov, vt = lse_ref[0], dl_ref[0], do_ref[...], vt_ref[...]
        ss = [jnp.where(keep, _dot(q_ref[:, cs[h]], kt_ref[cs[h], :]), NEG) for h in heads]
        dp = [_dot(jnp.where(lo if h == 0 else ~lo, dov, 0).astype(BF), vt) for h in heads]
        ds = [(jnp.exp(ss[h] - lse[:, col[h]]) * (dp[h] - dl[:, col[h]])).astype(BF) for h in heads]
        dq = [_dot(ds[h], k_ref[:, cs[h]]) for h in heads]
        acc[...] += jnp.concatenate(dq, axis=1)

        @pl.when(ki == qi)
        def _():
            dq_ref[...] = acc[...]

    (dq,), got = _flash_call(
        body, "mla_flash_dq", exs,
        [pl.BlockSpec((t, 256), lambda p, s: (_tri_rows(s, n)[0], p)),
         pl.BlockSpec((t, 256), lambda p, s: (_tri_rows(s, n)[1], p)),
         pl.BlockSpec((256, t), lambda p, s: (p, _tri_rows(s, n)[1])),
         pl.BlockSpec((128, t), lambda p, s: (p, _tri_rows(s, n)[1])),
         pl.BlockSpec((t, 128), lambda p, s: (_tri_rows(s, n)[0], p)),
         pl.BlockSpec((1, t, 128), lambda p, s: (p, _tri_rows(s, n)[0], 0)),
         pl.BlockSpec((1, t, 128), lambda p, s: (p, _tri_rows(s, n)[0], 0))],
        [pl.BlockSpec((t, 256), lambda p, s: (_tri_rows(s, n)[0], p))],
        [jax.ShapeDtypeStruct((S, 1024), F32)],
        [pltpu.VMEM((t, 256), F32)], n, (q, k, kt, vt, do, lse, delta))
    return dq, got


def _flash_bwd_dkv(q, qt, k, v, do, dot_, lse_t, delta_t, exs=()):
    S = q.shape[0]
    t = min(T_ATT, S)
    n = S // t

    def body(q_ref, qt_ref, k_ref, v_ref, do_ref, dot_ref, lse_ref, dl_ref, dk_ref, dv_ref, dk_acc, dv_acc):
        qi, ki = _tri_cols(pl.program_id(1), n)
        lo = lax.broadcasted_iota(jnp.int32, (t, LANES), 1) < MLA_V
        top = lax.broadcasted_iota(jnp.int32, (LANES, t), 0) < MLA_V

        @pl.when(qi == ki)
        def _():
            dk_acc[...] = jnp.zeros_like(dk_acc)
            dv_acc[...] = jnp.zeros_like(dv_acc)

        keep = _upper_tri(t) | (qi > ki)
        heads = range(2)
        cs = [slice(h * LANES, (h + 1) * LANES) for h in heads]
        vv, lse, dl, dov, dot_v = v_ref[...], lse_ref[0], dl_ref[0], do_ref[...], dot_ref[...]
        st = [jnp.where(keep, _dot(k_ref[:, cs[h]], qt_ref[cs[h], :]), NEG) for h in heads]
        dpt = [_dot(vv, jnp.where(top if h == 0 else ~top, dot_v, 0).astype(BF)) for h in heads]
        pt = [jnp.exp(st[h] - lse[h:h + 1, :]) for h in heads]
        dst = [(pt[h] * (dpt[h] - dl[h:h + 1, :])).astype(BF) for h in heads]
        dv = [_dot(pt[h].astype(BF), jnp.where(lo if h == 0 else ~lo, dov, 0).astype(BF)) for h in heads]
        dk = [_dot(dst[h], q_ref[:, cs[h]]) for h in heads]
        dv_acc[...] += dv[0] + dv[1]
        dk_acc[...] += jnp.concatenate(dk, axis=1)

        @pl.when(qi == n - 1)
        def _():
            dk_ref[...] = dk_acc[...]
            dv_ref[...] = dv_acc[...]

    (dk, dv), got = _flash_call(
        body, "mla_flash_dkv", exs,
        [pl.BlockSpec((t, 256), lambda p, s: (_tri_cols(s, n)[0], p)),
         pl.BlockSpec((256, t), lambda p, s: (p, _tri_cols(s, n)[0])),
         pl.BlockSpec((t, 256), lambda p, s: (_tri_cols(s, n)[1], p)),
         pl.BlockSpec((t, 128), lambda p, s: (_tri_cols(s, n)[1], p)),
         pl.BlockSpec((t, 128), lambda p, s: (_tri_cols(s, n)[0], p)),
         pl.BlockSpec((128, t), lambda p, s: (p, _tri_cols(s, n)[0])),
         pl.BlockSpec((1, SUBLANES, t), lambda p, s: (p, 0, _tri_cols(s, n)[0])),
         pl.BlockSpec((1, SUBLANES, t), lambda p, s: (p, 0, _tri_cols(s, n)[0]))],
        [pl.BlockSpec((t, 256), lambda p, s: (_tri_cols(s, n)[1], p)),
         pl.BlockSpec((t, 128), lambda p, s: (_tri_cols(s, n)[1], p))],
        [jax.ShapeDtypeStruct((S, 1024), F32), jax.ShapeDtypeStruct((S, 512), F32)],
        [pltpu.VMEM((t, 256), F32), pltpu.VMEM((t, 128), F32)], n, (q, qt, k, v, do, dot_, lse_t, delta_t))
    return dk, dv, got


def _mem_heads(qm, km_ref, vm_ref):
    ps, os_ = [], []
    for h in range(X_HEADS):
        cs = slice(h * X_HEAD_DIM, (h + 1) * X_HEAD_DIM)
        s = _dot_nt(qm[:, cs].astype(BF), km_ref[:, cs]) * MEM_SCALE
        e = jnp.exp(s - jnp.max(s, axis=1, keepdims=True))
        p = e / jnp.sum(e, axis=1, keepdims=True)
        ps.append(p)
        os_.append(_dot(p.astype(BF), vm_ref[:, cs]))
    return ps, jnp.concatenate(os_, axis=1)


def _mem_fwd(proj, km, vm, p_mem):
    S = proj.shape[0]
    tr = min(T_ROWS_WIDE, S)
    M = km.shape[0]

    def body(q_ref, z_ref, km_ref, vm_ref, p_ref, o_ref):
        _, o = _mem_heads(q_ref[...], km_ref, vm_ref)
        y = o * _silu(z_ref[...].astype(F32))
        o_ref[...] = _dot(y.astype(BF), p_ref[...]).astype(o_ref.dtype)

    return pl.pallas_call(
        body, name="mem_fwd", grid=(S // tr,),
        in_specs=[_rows(tr, 512, 4), _rows(tr, 512, 5), _full((M, 512)), _full((M, 512)), p_mem.spec()],
        out_specs=_rows(tr, 1024), out_shape=jax.ShapeDtypeStruct((S, D_MODEL), ACT),
        compiler_params=_cparams(("parallel",)),
    )(proj, proj, km, vm, p_mem.arr)


def _mem_bwd(do, proj, km, vm, p_mem):
    S = proj.shape[0]
    tr = min(T_ROWS_BWD_WIDE, S)
    M = km.shape[0]

    def body(do_ref, q_ref, z_ref, km_ref, vm_ref, p_ref, dq_ref, dz_ref, dkm_ref, dvm_ref, dp_ref):
        @pl.when(pl.program_id(0) == 0)
        def _():
            dkm_ref[...] = jnp.zeros_like(dkm_ref)
            dvm_ref[...] = jnp.zeros_like(dvm_ref)
            dp_ref[...] = jnp.zeros_like(dp_ref)

        qm = q_ref[...]
        z = z_ref[...].astype(F32)
        ps, o = _mem_heads(qm, km_ref, vm_ref)
        sz = _silu(z)
        y = o * sz
        dob = do_ref[...].astype(BF)
        dy = _dot_nt(dob, p_ref[...])
        dp_ref[...] += _dot_tn(y.astype(BF), dob)
        dz_ref[...] = (dy * o * _dsilu(z)).astype(dz_ref.dtype)
        d_o = dy * sz
        for h in range(X_HEADS):
            cs = slice(h * X_HEAD_DIM, (h + 1) * X_HEAD_DIM)
            doh = d_o[:, cs]
            dohb = doh.astype(BF)
            p = ps[h]
            dpr = _dot_nt(dohb, vm_ref[:, cs])
            ds = (p * (dpr - jnp.sum(doh * o[:, cs], axis=1, keepdims=True)) * MEM_SCALE).astype(BF)
            dq_ref[:, cs] = _dot(ds, km_ref[:, cs]).astype(dq_ref.dtype)
            dkm_ref[:, cs] += _dot_tn(ds, qm[:, cs].astype(BF))
            dvm_ref[:, cs] += _dot_tn(p.astype(BF), dohb)

    return pl.pallas_call(
        body, name="mem_bwd", grid=(S // tr,),
        in_specs=[_rows(tr, 1024), _rows(tr, 512, 4), _rows(tr, 512, 5), _full((M, 512)), _full((M, 512)),
                  p_mem.spec()],
        out_specs=(_rows(tr, 512), _rows(tr, 512), _full((M, 512)), _full((M, 512)), _full((512, 1024))),
        out_shape=(jax.ShapeDtypeStruct((S, 512), BF), jax.ShapeDtypeStruct((S, 512), BF),
                   jax.ShapeDtypeStruct((M, 512), F32), jax.ShapeDtypeStruct((M, 512), F32),
                   jax.ShapeDtypeStruct((512, 1024), F32)),
        compiler_params=_cparams(("arbitrary",)),
    )(do, proj, proj, km, vm, p_mem.arr)


def _merge_fwd(x, proj, o_ssm, o_att, o_mem, b_gate, p_mla, w_out, ln_g, ln_b):
    S = x.shape[0]
    tr = min(T_ROWS, S)

    def body(x_ref, lg_ref, z_ref, os_ref, oa_ref, om_ref, bg_ref, p_ref, w_ref, g_ref, b_ref,
             xn_ref, xb_ref, pre_ref, mg_ref):
        gates = _sigmoid(lg_ref[...].astype(F32) + bg_ref[0])
        ya = oa_ref[...] * _silu(z_ref[...].astype(F32))
        o_mla = _dot(ya.astype(BF), p_ref[...])
        merged = (gates[:, :D_MODEL] * os_ref[...].astype(F32) + gates[:, D_MODEL:2 * D_MODEL] * o_mla
                  + gates[:, 2 * D_MODEL:] * om_ref[...].astype(F32))
        mb = merged.astype(BF)
        mg_ref[...] = mb
        pre = ALPHA * x_ref[...] + _dot(mb, w_ref[...])
        pre_ref[...] = pre
        mu = jnp.mean(pre, axis=-1, keepdims=True)
        xc = pre - mu
        var = jnp.mean(xc * xc, axis=-1, keepdims=True)
        xn = xc * lax.rsqrt(var + NORM_EPS) * g_ref[0] + b_ref[0]
        xn_ref[...] = xn
        xb_ref[...] = xn.astype(BF)

    return pl.pallas_call(
        body, name="merge_fwd", grid=(S // tr,),
        in_specs=[_rows(tr, 1024), _rows(tr, 3072, 1), _rows(tr, 512, 3), _rows(tr, 1024), _rows(tr, 512),
                  _rows(tr, 1024), b_gate.spec(), p_mla.spec(), _full((1024, 1024)), ln_g.spec(), ln_b.spec()],
        out_specs=(_rows(tr, 1024), _rows(tr, 1024), _rows(tr, 1024), _rows(tr, 1024)),
        out_shape=(jax.ShapeDtypeStruct((S, 1024), F32), jax.ShapeDtypeStruct((S, 1024), BF),
                   jax.ShapeDtypeStruct((S, 1024), F32), jax.ShapeDtypeStruct((S, 1024), BF)),
        compiler_params=_cparams(("parallel",)),
    )(x, proj, proj, o_ssm, o_att, o_mem, b_gate.arr, p_mla.arr, w_out, ln_g.arr, ln_b.arr)


def _merge_bwd(dxn, pre, merged, proj, o_ssm, o_att, o_mem, b_gate, p_mla, w_out, ln_g):
    S = pre.shape[0]
    tr = min(T_ROWS_BWD, S)

    def body(dxn_ref, pre_ref, mg_ref, lg_ref, z_ref, os_ref, oa_ref, om_ref, bg_ref, p_ref, w_ref, g_ref,
             dxr_ref, dlg_ref, dos_ref, dom_ref, doa_ref, dz_ref, doat_ref, dl_ref, dlt_ref, dw_ref, dp_ref, dbg_ref,
             dg_ref, db_ref):
        @pl.when(pl.program_id(0) == 0)
        def _():
            dw_ref[...] = jnp.zeros_like(dw_ref)
            dp_ref[...] = jnp.zeros_like(dp_ref)
            dbg_ref[...] = jnp.zeros_like(dbg_ref)
            dg_ref[...] = jnp.zeros_like(dg_ref)
            db_ref[...] = jnp.zeros_like(db_ref)

        dxn = dxn_ref[...]
        pre = pre_ref[...]
        mu = jnp.mean(pre, axis=-1, keepdims=True)
        xc = pre - mu
        rstd = lax.rsqrt(jnp.mean(xc * xc, axis=-1, keepdims=True) + NORM_EPS)
        xhat = xc * rstd
        dg_ref[...] += jnp.sum(dxn * xhat, axis=0, keepdims=True)
        db_ref[...] += jnp.sum(dxn, axis=0, keepdims=True)
        dxh = dxn * g_ref[0]
        dpre = rstd * (dxh - jnp.mean(dxh, axis=-1, keepdims=True)
                       - xhat * jnp.mean(dxh * xhat, axis=-1, keepdims=True))
        dxr_ref[...] = ALPHA * dpre
        dpb = dpre.astype(BF)
        dw_ref[...] += _dot_tn(mg_ref[...], dpb)
        dm = _dot_nt(dpb, w_ref[...])

        gates = _sigmoid(lg_ref[...].astype(F32) + bg_ref[0])
        g0, g1, g2 = gates[:, :D_MODEL], gates[:, D_MODEL:2 * D_MODEL], gates[:, 2 * D_MODEL:]
        z = z_ref[...].astype(F32)
        oa = oa_ref[...]
        sz = _silu(z)
        ya = (oa * sz).astype(BF)
        o_mla = _dot(ya, p_ref[...])
        dos_ref[...] = (g0 * dm).astype(dos_ref.dtype)
        dom_ref[...] = (g2 * dm).astype(dom_ref.dtype)
        do_mla = (g1 * dm).astype(BF)
        dl0 = dm * os_ref[...].astype(F32) * g0 * (1.0 - g0)
        dl1 = dm * o_mla * g1 * (1.0 - g1)
        dl2 = dm * om_ref[...].astype(F32) * g2 * (1.0 - g2)
        dl = jnp.concatenate([dl0, dl1, dl2], axis=1)
        dbg_ref[...] += jnp.sum(dl, axis=0, keepdims=True)
        dlg_ref[...] = dl.astype(dlg_ref.dtype)
        dp_ref[...] += _dot_tn(ya, do_mla)
        dya = _dot_nt(do_mla, p_ref[...])
        doa = dya * sz
        doab = doa.astype(BF)
        doa_ref[...] = doab
        doat_ref[...] = doab.T
        dz_ref[...] = (dya * oa * _dsilu(z)).astype(dz_ref.dtype)
        prod = doa * oa
        lo = lax.broadcasted_iota(jnp.int32, (tr, LANES), 1) < MLA_V
        for pr in range(MLA_HEADS // 2):
            blk = prod[:, pr * LANES:(pr + 1) * LANES]
            d0 = jnp.sum(jnp.where(lo, blk, 0.0), axis=1, keepdims=True)
            d1 = jnp.sum(jnp.where(lo, 0.0, blk), axis=1, keepdims=True)
            dl = jnp.where(lo, d0, d1)
            dl_ref[pr] = dl
            dlt_ref[pr] = _pair_rows(dl)

    return pl.pallas_call(
        body, name="merge_bwd", grid=(S // tr,),
        in_specs=[_rows(tr, 1024), _rows(tr, 1024), _rows(tr, 1024), _rows(tr, 3072, 1), _rows(tr, 512, 3),
                  _rows(tr, 1024), _rows(tr, 512), _rows(tr, 1024), b_gate.spec(), p_mla.spec(),
                  _full((1024, 1024)), ln_g.spec()],
        out_specs=(_rows(tr, 1024), _rows(tr, 3072), _rows(tr, 1024), _rows(tr, 1024), _rows(tr, 512),
                   _rows(tr, 512), _cols(512, tr), pl.BlockSpec((MLA_HEADS // 2, tr, 128), lambda i: (0, i, 0)),
                   pl.BlockSpec((MLA_HEADS // 2, SUBLANES, tr), lambda i: (0, 0, i)),
                   _full((1024, 1024)), _full((512, 1024)), _full((1, 3072)), _full((1, 1024)), _full((1, 1024))),
        out_shape=(jax.ShapeDtypeStruct((S, 1024), F32), jax.ShapeDtypeStruct((S, 3072), BF),
                   jax.ShapeDtypeStruct((S, 1024), BF), jax.ShapeDtypeStruct((S, 1024), BF),
                   jax.ShapeDtypeStruct((S, 512), BF), jax.ShapeDtypeStruct((S, 512), BF),
                   jax.ShapeDtypeStruct((512, S), BF), jax.ShapeDtypeStruct((MLA_HEADS // 2, S, 128), F32),
                   jax.ShapeDtypeStruct((MLA_HEADS // 2, SUBLANES, S), F32),
                   jax.ShapeDtypeStruct((1024, 1024), F32), jax.ShapeDtypeStruct((512, 1024), F32),
                   jax.ShapeDtypeStruct((1, 3072), F32), jax.ShapeDtypeStruct((1, 1024), F32),
                   jax.ShapeDtypeStruct((1, 1024), F32)),
        compiler_params=_cparams(("arbitrary",)),
    )(dxn, pre, merged, proj, proj, o_ssm, o_att, o_mem, b_gate.arr, p_mla.arr, w_out, ln_g.arr)


def _loss_head(y, t):
    S = y.shape[0]
    tr = min(T_ROWS, S)
    n = S // tr

    def body(y_ref, t_ref, dy_ref, l_ref, acc):
        i = pl.program_id(0)

        @pl.when(i == 0)
        def _():
            acc[...] = jnp.zeros_like(acc)

        e = y_ref[...] - t_ref[...]
        dy_ref[...] = e * (1.0 / D_MODEL)
        acc[...] += jnp.sum(e * e, axis=0, keepdims=True)

        @pl.when(i == n - 1)
        def _():
            tot = jnp.sum(acc[...], axis=1, keepdims=True) * (0.5 / D_MODEL)
            l_ref[...] = jnp.broadcast_to(tot, l_ref.shape)

    return pl.pallas_call(
        body, name="loss_head", grid=(n,),
        in_specs=[_rows(tr, 1024), _rows(tr, 1024)],
        out_specs=(_rows(tr, 1024), _full((SUBLANES, LANES))),
        out_shape=(jax.ShapeDtypeStruct((S, 1024), F32), jax.ShapeDtypeStruct((SUBLANES, LANES), F32)),
        scratch_shapes=[pltpu.VMEM((1, 1024), F32)],
        compiler_params=_cparams(("arbitrary",)),
    )(y, t)


def _rope_tables(pos):
    inv_freq = ROPE_THETA ** (-jnp.arange(0, MLA_ROPE, 2, dtype=F32) / MLA_ROPE)
    ang = pos.astype(F32)[:, None] * inv_freq
    cos, sin = jnp.cos(ang), jnp.sin(ang)
    S = pos.shape[0]
    half = MLA_ROPE // 2
    ones = jnp.ones((S, MLA_NOPE), F32)
    z16 = jnp.zeros((S, half), F32)
    z32 = jnp.zeros((S, LANES - MLA_NOPE - MLA_ROPE), F32)
    z64 = jnp.zeros((S, MLA_NOPE), F32)
    c = jnp.concatenate([ones, cos, cos, z32], axis=1)
    sa = jnp.concatenate([z64, -sin, z16, z32], axis=1)
    sb = jnp.concatenate([z64, z16, sin, z32], axis=1)
    return c, sa, sb


def _ssm_discretise(a_re, a_im, log_dt, b_re, b_im):
    dt = jnp.exp(log_dt)[..., None]
    mag = jnp.exp(a_re * dt)
    lb_re = mag * jnp.cos(a_im * dt)
    lb_im = mag * jnp.sin(a_im * dt)
    nr, ni = lb_re - 1.0, lb_im
    den = a_re * a_re + a_im * a_im
    f_re = (nr * a_re + ni * a_im) / den
    f_im = (ni * a_re - nr * a_im) / den
    bb_re = f_re[..., None] * b_re - f_im[..., None] * b_im
    bb_im = f_re[..., None] * b_im + f_im[..., None] * b_re
    return lb_re, lb_im, bb_re, bb_im


_GPB = SSM_CB // SSM_GROUP


def _bd_in(bb):
    nb = SSM_GROUPS // _GPB
    t = bb.reshape(nb, _GPB, SSM_STATE, SSM_GROUP)
    eye = jnp.eye(_GPB, dtype=bb.dtype)
    return jnp.einsum("ngpc,gh->ngchp", t, eye).reshape(nb, SSM_CB, _GPB * SSM_STATE)


def _bd_in_t(d):
    nb = SSM_GROUPS // _GPB
    t = d.reshape(nb, _GPB, SSM_GROUP, _GPB, SSM_STATE)
    eye = jnp.eye(_GPB, dtype=d.dtype)
    return jnp.einsum("ngchp,gh->ngpc", t, eye).reshape(SSM_GROUPS, SSM_STATE, SSM_GROUP)


def _bd_out(c):
    nb = SSM_GROUPS // _GPB
    t = c.reshape(nb, _GPB, SSM_GROUP, SSM_STATE)
    eye = jnp.eye(_GPB, dtype=c.dtype)
    return jnp.einsum("ngcp,gh->ngphc", t, eye).reshape(nb, _GPB * SSM_STATE, SSM_CB)


def _interleave(a):
    S, w = a.shape
    return a.reshape(SUBLANES, S // SUBLANES, w).transpose(1, 0, 2).reshape(S, w)


def _deinterleave(a):
    S, w = a.shape
    return a.reshape(S // SUBLANES, SUBLANES, w).transpose(1, 0, 2).reshape(S, w)


IN_SHARD = D_IN // N_DEV
ROPE_OWNER = ROPE_SLOT_LO // IN_SHARD
assert ROPE_OWNER * IN_SHARD <= ROPE_SLOT_LO and ROPE_SLOT_LO + MLA_ROPE <= (ROPE_OWNER + 1) * IN_SHARD


def _w_in_from_shards(g):
    pieces = []
    for j in range(N_DEV):
        if j == ROPE_OWNER:
            a = ROPE_SLOT_LO - j * IN_SHARD
            z = lambda n: jnp.zeros((g.shape[1], n), g.dtype)
            pieces += [g[j][:, :a], z(MLA_NOPE), g[j][:, a:a + MLA_ROPE], z(LANES - MLA_NOPE - MLA_ROPE),
                       g[j][:, a + MLA_ROPE:]]
        else:
            pieces.append(g[j])
    return jnp.concatenate(pieces, axis=1)


def _w_in_to_shards(d):
    shift = LANES - MLA_ROPE
    out = []
    for j in range(N_DEV):
        lo, hi = j * IN_SHARD, (j + 1) * IN_SHARD
        if j < ROPE_OWNER:
            out.append(d[:, lo:hi])
        elif j > ROPE_OWNER:
            out.append(d[:, lo + shift:hi + shift])
        else:
            r = ROPE_SLOT_LO + MLA_NOPE
            out.append(jnp.concatenate([d[:, lo:ROPE_SLOT_LO], d[:, r:r + MLA_ROPE],
                                        d[:, ROPE_SLOT_LO + LANES:hi + shift]], axis=1))
    return jnp.stack(out)


def _adamw_math(w, g, m, v):
    m = ADAM_B1 * m + (1.0 - ADAM_B1) * g
    v = ADAM_B2 * v + (1.0 - ADAM_B2) * (g * g)
    m_hat = m / (1.0 - ADAM_B1 ** ADAM_STEP)
    v_hat = v / (1.0 - ADAM_B2 ** ADAM_STEP)
    delta = -ADAM_LR * (m_hat / (jnp.sqrt(v_hat) + ADAM_EPS) + ADAM_WD * w)
    return delta, m, v


def _adamw_sharded(parts, w, m, v, tile, name):
    L, _, R, C = parts.shape
    assert R % tile == 0

    def body(p_ref, w_ref, m_ref, v_ref, g_out, d_out, m_out, v_out):
        g = p_ref[0, 0].astype(F32)
        for k in range(1, N_DEV):
            g = g + p_ref[0, k].astype(F32)
        d, mn, vn = _adamw_math(w_ref[0], g, m_ref[0], v_ref[0])
        g_out[0] = g
        d_out[0] = d
        m_out[0] = mn
        v_out[0] = vn

    spec = pl.BlockSpec((1, tile, C), lambda l, i: (l, i, 0))
    shp = jax.ShapeDtypeStruct((L, R, C), F32)
    return pl.pallas_call(
        body, name=name, grid=(L, R // tile),
        in_specs=[pl.BlockSpec((1, N_DEV, tile, C), lambda l, i: (l, 0, i, 0)), spec, spec, spec],
        out_specs=(spec,) * 4, out_shape=(shp,) * 4, compiler_params=_cparams(("parallel", "parallel")),
    )(parts, w, m, v)


COL_GROUP = (("w_glu", 512), ("p_ssm", 512), ("p_mla", 512), ("p_mem", 512), ("w_uq", 256), ("w_ukv", 128))
COL_AT = {n: sum(r for _, r in COL_GROUP[:i]) // rows for i, (n, rows) in enumerate(COL_GROUP)}
assert all(sum(r for _, r in COL_GROUP[:i]) % rows == 0 for i, (_, rows) in enumerate(COL_GROUP))
COL_ROWS = dict(COL_GROUP)
ROW_GROUP = ("w_mem_kv", "w_out")
SMALL = ("b_gate", "ssm_a_re", "ssm_a_im", "ssm_log_dt", "ssm_b_re", "ssm_b_im", "ssm_c_re", "ssm_c_im", "ssm_d",
         "b_glu", "mla_q_norm", "mla_kv_norm", "ln_g", "ln_b")
UQ_COLS = MLA_NOPE + MLA_ROPE


def _pad_lanes(a):
    return jnp.concatenate([a, jnp.zeros(a.shape[:-1] + (LANES - a.shape[-1],), a.dtype)], axis=-1)


def _group_buffers(d, dtype):
    col = jnp.concatenate([_pad_lanes(d[n]) if n == "w_uq" else d[n] for n, _ in COL_GROUP], axis=1)
    row = jnp.concatenate([d[n] for n in ROW_GROUP], axis=1)
    return d["w_in"].astype(dtype), col.astype(dtype), row.astype(dtype)


def _ungroup(bufs):
    b_in, col, row = bufs
    out, off = {"w_in": b_in}, 0
    for n, rows in COL_GROUP:
        t = col[:, off:off + rows]
        out[n] = t[..., :UQ_COLS] if n == "w_uq" else t
        off += rows
    k = row.shape[1] // 2
    out["w_mem_kv"], out["w_out"] = row[:, :k], row[:, k:]
    return out


def _colcat(t):
    return t.transpose(1, 0, 2).reshape(t.shape[1], -1)


def _colsplit(g, n):
    return g.reshape(g.shape[0], N_DEV, n).transpose(1, 0, 2)


def _unpack_weights(g_in, g_col, g_row):
    wc = _colcat(g_col)
    at = lambda n: _RowBlock(wc, COL_ROWS[n], COL_AT[n])
    lo = COL_AT["w_ukv"] * COL_ROWS["w_ukv"]
    ukv = wc[lo:lo + COL_ROWS["w_ukv"]].reshape(-1, MLA_HEADS, LANES)
    lane = lax.broadcasted_iota(jnp.int32, ukv.shape, 2)
    k = g_row.shape[1] // 2
    return dict(
        w_in=_w_in_from_shards(g_in), w_glu=at("w_glu"), w_uq=at("w_uq"), p_ssm=at("p_ssm"), p_mla=at("p_mla"),
        p_mem=at("p_mem"), w_k=jnp.where(lane < MLA_NOPE, ukv, jnp.zeros_like(ukv)).reshape(ukv.shape[0], -1),
        w_v=ukv[..., MLA_NOPE:].reshape(ukv.shape[0], -1),
        w_mem_kv=g_row[:, :k].reshape(-1, g_row.shape[2]), w_out=g_row[:, k:].reshape(-1, g_row.shape[2]))


def _pack_grads_in(d_w_in):
    return _w_in_to_shards(d_w_in).astype(BF)


def _pack_grads_rest(d):
    ukv = jnp.concatenate([d["w_k"].reshape(-1, MLA_HEADS, LANES)[..., :MLA_NOPE],
                           d["w_v"].reshape(-1, MLA_HEADS, MLA_V)], axis=-1).reshape(d["w_k"].shape[0], -1)
    col = jnp.concatenate([ukv if n == "w_ukv" else d[n] for n, _ in COL_GROUP], axis=0)
    row = jnp.concatenate([d[n].reshape(N_DEV, -1, d[n].shape[1]) for n in ROW_GROUP], axis=1)
    return [_colsplit(col, LANES).astype(BF), row.astype(BF)]


def _pack_small(d, lead):
    parts = []
    for n in SMALL:
        keep = d[n].shape[:lead]
        f = d[n].reshape(keep + (-1,))
        pad = (-f.shape[-1]) % (SUBLANES * LANES)
        if pad:
            f = jnp.concatenate([f, jnp.zeros(keep + (pad,), f.dtype)], axis=-1)
        parts.append(f.reshape(keep + (-1, LANES)))
    return jnp.concatenate(parts, axis=lead)


def _unpack_small(buf, like):
    out, off = {}, 0
    for n in SMALL:
        size = math.prod(like[n].shape[1:])
        rows = -(-size // (SUBLANES * LANES)) * SUBLANES
        out[n] = buf[:, off:off + rows].reshape(buf.shape[0], -1)[:, :size].reshape(like[n].shape)
        off += rows
    return out


WEIGHTS = ("w_in", "b_gate", "ssm_a_re", "ssm_a_im", "ssm_log_dt", "ssm_b_re", "ssm_b_im", "ssm_c_re", "ssm_c_im",
           "ssm_d", "w_glu", "b_glu", "mla_q_norm", "w_uq", "mla_kv_norm", "w_ukv", "w_mem_kv", "p_ssm", "p_mla",
           "p_mem", "w_out", "ln_g", "ln_b")
BIG = ("w_in",) + tuple(n for n, _ in COL_GROUP) + ROW_GROUP


def _train_step(x, mem, pos, target, wl, ws):
    S = x.shape[0]
    tc, tsa, tsb = _rope_tables(pos)
    loc = _group_buffers(wl, BF)
    loc = [[b[l] for b in loc] for l in range(DEPTH)]

    lb_re, lb_im, bb_re, bb_im = _ssm_discretise(ws["ssm_a_re"], ws["ssm_a_im"], ws["ssm_log_dt"], ws["ssm_b_re"],
                                                 ws["ssm_b_im"])
    nb = SSM_GROUPS // _GPB
    mats = (jax.vmap(_bd_in)(bb_re), jax.vmap(_bd_in)(bb_im), lb_re.reshape(DEPTH, nb, 1, -1),
            lb_im.reshape(DEPTH, nb, 1, -1), jax.vmap(_bd_out)(ws["ssm_c_re"]), jax.vmap(_bd_out)(ws["ssm_c_im"]),
            ws["ssm_d"].reshape(DEPTH, 1, -1))

    rows3 = {n: ws[n].reshape(DEPTH, 1, -1) for n in ("b_glu", "mla_q_norm", "mla_kv_norm", "b_gate", "ln_g", "ln_b")}

    def small(n, l):
        return _LayerRow(rows3[n], l)

    ((g_in,),) = _exchange_call("weights_gather_first", [_Exchange("ag", loc[0][:1])])
    W = [None] * DEPTH
    saved = []
    xs, xb = x, x.astype(BF)
    for l in range(DEPTH):
        if l == 0:
            proj, (g_rest,) = _mm(xb, _w_in_from_shards(g_in), name="proj_fwd", tm=S, tn=512, out_dtype=ACT,
                                  exs=[_Exchange("ag", loc[0][1:])])
            W[0] = _unpack_weights(g_in, *g_rest)
        else:
            proj = _mm(xb, W[l]["w_in"], name="proj_fwd", tm=S, tn=512, out_dtype=ACT)
        w = W[l]
        u_il = _interleave(proj[:, :SSM_WIDTH])
        y_raw = _deinterleave(_ssm_fwd(u_il, mats, l))
        o_ssm = _ssm_post_fwd(y_raw, proj, w["w_glu"], small("b_glu", l), w["p_ssm"])
        q, k, v, qt, kt, vt = _mla_pre_fwd(proj, small("mla_q_norm", l), small("mla_kv_norm", l), w["w_uq"], w["w_k"], w["w_v"],
                               tc, tsa, tsb)
        nxt = [_Exchange("ag", loc[l + 1])] if l + 1 < DEPTH else []
        (o_att, lse, lse_t), gathered = _flash_fwd(q, kt, v, nxt)
        if nxt:
            W[l + 1] = _unpack_weights(*gathered[0])
        kvm = _mm(mem, w["w_mem_kv"], name="memkv_fwd", out_dtype=BF)
        km, vm = kvm[:, :512], kvm[:, 512:]
        o_mem = _mem_fwd(proj, km, vm, w["p_mem"])
        xn, xnb, pre, merged = _merge_fwd(xs, proj, o_ssm, o_att, o_mem, small("b_gate", l), w["p_mla"], w["w_out"],
                                          small("ln_g", l), small("ln_b", l))
        saved.append(dict(xb=xb, proj=proj, u_il=u_il, y_raw=y_raw, o_ssm=o_ssm, q=q, k=k, v=v, qt=qt, kt=kt, vt=vt, o_att=o_att,
                          lse=lse, lse_t=lse_t,
                          km=km, vm=vm, o_mem=o_mem, pre=pre, merged=merged))
        xs, xb = xn, xnb

    dxs, lvec = _loss_head(xs, target)
    loss = lvec[0, 0]

    disc_names = ("ssm_a_re", "ssm_a_im", "ssm_log_dt", "ssm_b_re", "ssm_b_im")
    got = [None] * DEPTH
    got_small = [None] * DEPTH
    pending = None
    pending_small = None
    for l in reversed(range(DEPTH)):
        sv, w = saved[l], W[l]
        proj = sv["proj"]
        (dx_res, dlg, do_ssm, do_mem, do_att, dz_mla, do_att_t, delta, delta_t, d_w_out, d_p_mla, d_b_gate, d_ln_g,
         d_ln_b) = _merge_bwd(
            dxs, sv["pre"], sv["merged"], proj, sv["o_ssm"], sv["o_att"], sv["o_mem"], small("b_gate", l), w["p_mla"],
            w["w_out"], small("ln_g", l))
        dq_mem, dz_mem, d_km, d_vm, d_p_mem = _mem_bwd(do_mem, proj, sv["km"], sv["vm"], w["p_mem"])
        d_w_mem = _mm(mem, jnp.concatenate([d_km, d_vm], axis=1), name="memkv_bwd", ta=True)
        dq, arrived_rest = _flash_bwd_dq(
            sv["q"], sv["k"], sv["kt"], sv["vt"], do_att, sv["lse"], delta,
            [_Exchange("a2a", pending[1:]), _Exchange("ag", [pending_small])] if pending is not None else [])
        dk, dv, arrived_in = _flash_bwd_dkv(sv["q"], sv["qt"], sv["k"], sv["v"], do_att, do_att_t, sv["lse_t"], delta_t,
                                            [_Exchange("a2a", pending[:1])] if pending is not None else [])
        if pending is not None:
            got[l + 1] = arrived_in[0] + arrived_rest[0]
            got_small[l + 1] = arrived_rest[1][0]
        dcq, dckv, dslot, d_wuq, d_wk, d_wv, d_qn, d_kn = _mla_pre_bwd(
            dq, dk, dv, proj, small("mla_q_norm", l), small("mla_kv_norm", l), w["w_uq"], w["w_k"], w["w_v"],
            tc, tsa, tsb)
        dy_raw, dz_ssm, d_w_glu, d_b_glu, d_p_ssm = _ssm_post_bwd(do_ssm, sv["y_raw"], proj, w["w_glu"],
                                                                 small("b_glu", l), w["p_ssm"])
        rest = _pack_grads_rest(dict(w_glu=d_w_glu, w_uq=d_wuq, w_k=d_wk, w_v=d_wv, w_mem_kv=d_w_mem, p_ssm=d_p_ssm,
                                     p_mla=d_p_mla, p_mem=d_p_mem, w_out=d_w_out))
        (du_il, dbbre, dbbim, dare, daim, dcre, dcim, dd), early = _ssm_bwd(
            sv["u_il"], _interleave(dy_raw), mats, l, [_Exchange("a2a", rest)] if l == 0 else [])
        du = _deinterleave(du_il).astype(BF)
        _, disc_vjp = jax.vjp(_ssm_discretise, *[ws[n][l] for n in disc_names])
        d_disc = disc_vjp((dare.reshape(SSM_GROUPS, SSM_STATE), daim.reshape(SSM_GROUPS, SSM_STATE), _bd_in_t(dbbre),
                           _bd_in_t(dbbim)))
        dproj = jnp.concatenate([du, dz_ssm, dcq, dckv, dslot, dz_mla, dq_mem, dz_mem, dlg], axis=1)
        if l > 0:
            d_w_in = _mm(sv["xb"], dproj, name="proj_dw", ta=True, tm=1024, tn=512, tk=S)
            dxs = _mm(dproj, w["w_in"], name="proj_dx", tb=True, add=dx_res, tm=1024, tn=1024, tk=1024)
            pending = [_pack_grads_in(d_w_in)] + rest
        else:
            hm = D_MODEL // 2
            dw_a = _mm(sv["xb"][:, :hm], dproj, name="proj_dw_half", ta=True, tm=hm, tn=512, tk=S)
            dw_b, (first_half,) = _mm(sv["xb"][:, hm:], dproj, name="proj_dw_half", ta=True, tm=hm, tn=512, tk=S,
                                      exs=[_Exchange("a2a", [_pack_grads_in(dw_a)])])
            pending = [_pack_grads_in(dw_b)]
        gsl = dict(zip(disc_names, d_disc))
        gsl.update(b_gate=d_b_gate, ssm_c_re=_bd_in_t(dcre).transpose(0, 2, 1), ssm_c_im=_bd_in_t(dcim).transpose(0, 2, 1),
                   ssm_d=dd, b_glu=d_b_glu, mla_q_norm=d_qn, mla_kv_norm=d_kn, ln_g=d_ln_g, ln_b=d_ln_b)
        pending_small = _pack_small(gsl, 0)

    dxs, (last_in, (got_small[0],)) = _mm(
        dproj, w["w_in"], name="proj_dx", tb=True, add=dx_res, tm=1024, tn=1024, tk=1024,
        exs=[_Exchange("a2a", pending), _Exchange("ag", [pending_small])])
    got[0] = [jnp.concatenate([first_half[0], last_in[0]], axis=1)] + early[0]
    return loss, dxs, got, got_small


def kernel(x, mem, positions, w_in, b_gate, ssm_a_re, ssm_a_im, ssm_log_dt, ssm_b_re, ssm_b_im, ssm_c_re, ssm_c_im, ssm_d, w_glu, b_glu, mla_q_norm, w_uq, mla_kv_norm, w_ukv, w_mem_kv, p_ssm, p_mla, p_mem, w_out, ln_g, ln_b, loss_target, m_w_in, m_b_gate, m_ssm_a_re, m_ssm_a_im, m_ssm_log_dt, m_ssm_b_re, m_ssm_b_im, m_ssm_c_re, m_ssm_c_im, m_ssm_d, m_w_glu, m_b_glu, m_mla_q_norm, m_w_uq, m_mla_kv_norm, m_w_ukv, m_w_mem_kv, m_p_ssm, m_p_mla, m_p_mem, m_w_out, m_ln_g, m_ln_b, v_w_in, v_b_gate, v_ssm_a_re, v_ssm_a_im, v_ssm_log_dt, v_ssm_b_re, v_ssm_b_im, v_ssm_c_re, v_ssm_c_im, v_ssm_d, v_w_glu, v_b_glu, v_mla_q_norm, v_w_uq, v_mla_kv_norm, v_w_ukv, v_w_mem_kv, v_p_ssm, v_p_mla, v_p_mem, v_w_out, v_ln_g, v_ln_b):
    w = dict(w_in=w_in, b_gate=b_gate, ssm_a_re=ssm_a_re, ssm_a_im=ssm_a_im, ssm_log_dt=ssm_log_dt, ssm_b_re=ssm_b_re,
             ssm_b_im=ssm_b_im, ssm_c_re=ssm_c_re, ssm_c_im=ssm_c_im, ssm_d=ssm_d, w_glu=w_glu, b_glu=b_glu,
             mla_q_norm=mla_q_norm, w_uq=w_uq, mla_kv_norm=mla_kv_norm, w_ukv=w_ukv, w_mem_kv=w_mem_kv, p_ssm=p_ssm,
             p_mla=p_mla, p_mem=p_mem, w_out=w_out, ln_g=ln_g, ln_b=ln_b)
    m = dict(w_in=m_w_in, b_gate=m_b_gate, ssm_a_re=m_ssm_a_re, ssm_a_im=m_ssm_a_im, ssm_log_dt=m_ssm_log_dt,
             ssm_b_re=m_ssm_b_re, ssm_b_im=m_ssm_b_im, ssm_c_re=m_ssm_c_re, ssm_c_im=m_ssm_c_im, ssm_d=m_ssm_d,
             w_glu=m_w_glu, b_glu=m_b_glu, mla_q_norm=m_mla_q_norm, w_uq=m_w_uq, mla_kv_norm=m_mla_kv_norm,
             w_ukv=m_w_ukv, w_mem_kv=m_w_mem_kv, p_ssm=m_p_ssm, p_mla=m_p_mla, p_mem=m_p_mem, w_out=m_w_out,
             ln_g=m_ln_g, ln_b=m_ln_b)
    v = dict(w_in=v_w_in, b_gate=v_b_gate, ssm_a_re=v_ssm_a_re, ssm_a_im=v_ssm_a_im, ssm_log_dt=v_ssm_log_dt,
             ssm_b_re=v_ssm_b_re, ssm_b_im=v_ssm_b_im, ssm_c_re=v_ssm_c_re, ssm_c_im=v_ssm_c_im, ssm_d=v_ssm_d,
             w_glu=v_w_glu, b_glu=v_b_glu, mla_q_norm=v_mla_q_norm, w_uq=v_w_uq, mla_kv_norm=v_mla_kv_norm,
             w_ukv=v_w_ukv, w_mem_kv=v_w_mem_kv, p_ssm=v_p_ssm, p_mla=v_p_mla, p_mem=v_p_mem, w_out=v_w_out,
             ln_g=v_ln_g, ln_b=v_ln_b)

    wl = {n: w[n] for n in BIG}
    small = {n: w[n] for n in SMALL}
    loss_local, dx, got, got_small = _train_step(x[0], mem[0], positions[0], loss_target[0], wl, small)
    loss = lax.psum(loss_local, ("x", "y", "c"))

    grads, delta, new_m, new_v = {}, {}, {}, {}
    wg = _group_buffers(wl, F32)
    mg = _group_buffers({n: m[n] for n in BIG}, F32)
    vg = _group_buffers({n: v[n] for n in BIG}, F32)
    res = []
    for i, (tile, tag) in enumerate(((256, "in"), (128, "col"), (256, "row"))):
        parts = jnp.stack([got[l][i] for l in range(DEPTH)])
        res.append(_adamw_sharded(parts, wg[i], mg[i], vg[i], tile, "adamw_" + tag))
    for dst, j in ((grads, 0), (delta, 1), (new_m, 2), (new_v, 3)):
        dst.update(_ungroup([r[j] for r in res]))

    sw, sm, sv = (_pack_small(small, 1), _pack_small({n: m[n] for n in SMALL}, 1), _pack_small({n: v[n] for n in SMALL}, 1))
    rs = _adamw_sharded(jnp.stack(got_small), sw, sm, sv, sw.shape[1], "adamw_replicated")
    for dst, buf in zip((grads, delta, new_m, new_v), rs):
        dst.update(_unpack_small(buf, small))

    return (loss, dx[None], *[grads[n] for n in WEIGHTS], *[delta[n] for n in WEIGHTS],
            *[new_m[n] for n in WEIGHTS], *[new_v[n] for n in WEIGHTS])
```
